```python
import math
import jax
import jax.numpy as jnp
from jax import lax
import numpy as np

D_MODEL = 2048
BATCH = 8
SEQ = 2048
DEPTH = 2

MLA_HEADS = 8
QK_NOPE = 128
QK_ROPE = 64
V_HEAD = 128
Q_LORA = 512
KV_LORA = 512
ROPE_THETA = 10000.0
Q_BLOCK = 128
GDN_HEADS = 8
GDN_DK = 128
GDN_DV = 128
CONV_WIDTH = 4
CHUNK = 64
GDN_QK = GDN_HEADS * GDN_DK
GDN_V = GDN_HEADS * GDN_DV
CONV_CH = 2 * GDN_QK + GDN_V
D_FF = ((8 * D_MODEL // 3 + 255) // 256) * 256
EPS = 1e-6
IN_SIZES = (Q_LORA, KV_LORA, QK_ROPE, GDN_QK, GDN_QK, GDN_V, GDN_V, GDN_HEADS, GDN_HEADS, 2 * D_MODEL)
IN_WIDTH = Q_LORA + KV_LORA + QK_ROPE + 2 * GDN_QK + 2 * GDN_V + 2 * GDN_HEADS + 2 * D_MODEL

kernel_name = 'hybrid_mla_gdn_adaln_block'


def _rmsnorm(x, w):
    xf = x.astype(jnp.float32)
    y = xf * lax.rsqrt(jnp.mean(xf * xf, axis=-1, keepdims=True) + EPS)
    return (y * w.astype(jnp.float32)).astype(x.dtype)


def _split_cols(p):
    outs, off = [], 0
    for n in IN_SIZES:
        outs.append(p[..., off:off + n])
        off += n
    return outs


def _rope_tables(positions):
    inv_freq = 1.0 / (ROPE_THETA ** (jnp.arange(0, QK_ROPE, 2, dtype=jnp.float32) / QK_ROPE))
    ang = positions.astype(jnp.float32)[..., None] * inv_freq
    return jnp.cos(ang), jnp.sin(ang)


def _rope(x, cos, sin):
    xf = x.astype(jnp.float32)
    x1, x2 = jnp.split(xf, 2, axis=-1)
    return jnp.concatenate([x1 * cos - x2 * sin, x2 * cos + x1 * sin], axis=-1).astype(x.dtype)


def _mla_branch(c_q, c_kv, k_pe, q_norm, kv_norm, w_uq, w_ukv, cos, sin):
    B, T, _ = c_q.shape
    q = (_rmsnorm(c_q, q_norm) @ w_uq).reshape(B, T, MLA_HEADS, QK_NOPE + QK_ROPE)
    q_nope, q_pe = q[..., :QK_NOPE], q[..., QK_NOPE:]
    q_pe = _rope(q_pe, cos[:, :, None, :], sin[:, :, None, :])
    kv = (_rmsnorm(c_kv, kv_norm) @ w_ukv).reshape(B, T, MLA_HEADS, QK_NOPE + V_HEAD)
    k_nope, v = kv[..., :QK_NOPE], kv[..., QK_NOPE:]
    k_pe = _rope(k_pe, cos, sin)
    scale = (QK_NOPE + QK_ROPE) ** -0.5
    outs = []
    for i in range(T // Q_BLOCK):
        q0, k_end = i * Q_BLOCK, (i + 1) * Q_BLOCK
        s = (jnp.einsum('bqhd,bkhd->bhqk', q_nope[:, q0:k_end], k_nope[:, :k_end])
             + jnp.einsum('bqhr,bkr->bhqk', q_pe[:, q0:k_end], k_pe[:, :k_end]))
        s = s.astype(jnp.float32) * scale
        mask = jnp.arange(k_end)[None, :] <= (q0 + jnp.arange(Q_BLOCK))[:, None]
        p = jax.nn.softmax(jnp.where(mask, s, -jnp.inf), axis=-1).astype(v.dtype)
        outs.append(jnp.einsum('bhqk,bkhd->bqhd', p, v[:, :k_end]))
    return jnp.concatenate(outs, axis=1).reshape(B, T, MLA_HEADS * V_HEAD)


def _causal_conv_silu(u, w):
    kern = w[:, None, :].astype(u.dtype)
    y = lax.conv_general_dilated(u, kern, window_strides=(1,), padding=[(CONV_WIDTH - 1, 0)],
                                 dimension_numbers=('NWC', 'WIO', 'NWC'),
                                 feature_group_count=u.shape[-1])
    return jax.nn.silu(y)


def _l2norm(x):
    xf = x.astype(jnp.float32)
    return xf * lax.rsqrt(jnp.sum(xf * xf, axis=-1, keepdims=True) + EPS)


def _gated_delta_chunked(q, k, v, beta, g):
    B, T, H, DK = q.shape
    DV = v.shape[-1]
    N = T // CHUNK
    to_chunks = lambda a: a.reshape(B, N, CHUNK, H, -1).transpose(0, 3, 1, 2, 4)
    q, k, v = to_chunks(q), to_chunks(k), to_chunks(v)
    beta = beta.reshape(B, N, CHUNK, H).transpose(0, 3, 1, 2)
    g = g.reshape(B, N, CHUNK, H).transpose(0, 3, 1, 2)
    G = jnp.cumsum(g, axis=-1)
    idx = jnp.arange(CHUNK)
    lower = idx[:, None] >= idx[None, :]
    strict = idx[:, None] > idx[None, :]
    diff = G[..., :, None] - G[..., None, :]
    decay = jnp.where(lower, jnp.exp(jnp.where(lower, diff, 0.0)), 0.0)
    kb = k * beta[..., None]
    Lmat = jnp.where(strict, jnp.einsum('bhncd,bhnsd->bhncs', kb, k) * decay, 0.0)
    A = Lmat + jnp.eye(CHUNK, dtype=jnp.float32)
    rhs = jnp.concatenate([v * beta[..., None], kb * jnp.exp(G)[..., None]], axis=-1)
    sol = lax.linalg.triangular_solve(A, rhs, left_side=True, lower=True, unit_diagonal=True)
    u, w = sol[..., :DV], sol[..., DV:]
    attn = jnp.where(lower, jnp.einsum('bhncd,bhnsd->bhncs', q, k) * decay, 0.0)

    def step(S, inp):
        q_c, k_c, u_c, w_c, G_c, a_c = inp
        v_new = u_c - jnp.einsum('bhcd,bhde->bhce', w_c, S)
        o = (jnp.einsum('bhcd,bhde->bhce', q_c * jnp.exp(G_c)[..., None], S)
             + jnp.einsum('bhcs,bhse->bhce', a_c, v_new))
        G_last = G_c[..., -1]
        k_dec = k_c * jnp.exp(G_last[..., None] - G_c)[..., None]
        S = S * jnp.exp(G_last)[..., None, None] + jnp.einsum('bhcd,bhce->bhde', k_dec, v_new)
        return S, o

    xs = tuple(jnp.moveaxis(a, 2, 0) for a in (q, k, u, w, G, attn))
    S0 = jnp.zeros((B, H, DK, DV), jnp.float32)
    _, o = lax.scan(step, S0, xs)
    return o.transpose(1, 0, 3, 2, 4).reshape(B, T, H, DV)


def _gdn_branch(qkv, z, b_logit, a_logit, conv_w, A_log, dt_bias, gdn_norm):
    B, T, _ = qkv.shape
    dtype = qkv.dtype
    qkv = _causal_conv_silu(qkv, conv_w)
    q = _l2norm(qkv[..., :GDN_QK].reshape(B, T, GDN_HEADS, GDN_DK)) * (GDN_DK ** -0.5)
    k = _l2norm(qkv[..., GDN_QK:2 * GDN_QK].reshape(B, T, GDN_HEADS, GDN_DK))
    v = qkv[..., 2 * GDN_QK:].reshape(B, T, GDN_HEADS, GDN_DV).astype(jnp.float32)
    beta = jax.nn.sigmoid(b_logit.astype(jnp.float32))
    g = -jnp.exp(A_log.astype(jnp.float32)) * jax.nn.softplus(a_logit.astype(jnp.float32) + dt_bias.astype(jnp.float32))
    o = _gated_delta_chunked(q, k, v, beta, g)
    o = o * lax.rsqrt(jnp.mean(o * o, axis=-1, keepdims=True) + EPS) * gdn_norm.astype(jnp.float32)
    o = o * jax.nn.silu(z.reshape(B, T, GDN_HEADS, GDN_DV).astype(jnp.float32))
    return o.reshape(B, T, GDN_V).astype(dtype)


def _fwd_setup_inputs(seed: int = 0) -> dict:
    key = jax.random.key(seed)
    ks = jax.random.split(key, 24)
    L, D = DEPTH, D_MODEL
    f32 = jnp.float32

    def nrm(k, shape, fan_in, gain=1.0):
        return gain * fan_in ** -0.5 * jax.random.normal(k, shape, f32)

    def gain(k, shape):
        return 1.0 + 0.02 * jax.random.normal(k, shape, f32)

    x = jax.random.normal(ks[0], (BATCH, SEQ, D), f32)
    c = jax.random.normal(ks[1], (BATCH, D), f32)
    positions = (jnp.arange(SEQ, dtype=jnp.int32)[None, :]
                 + jax.random.randint(ks[2], (BATCH, 1), 0, 1024, dtype=jnp.int32))
    dt = jnp.exp(jax.random.uniform(ks[15], (L, GDN_HEADS), f32, math.log(1e-3), math.log(1e-1)))
    return {
        'x': x,
        'c': c,
        'positions': positions,
        'w_ada': nrm(ks[3], (L, D, 6 * D), D, 0.5),
        'b_ada': 0.01 * jax.random.normal(ks[4], (L, 6 * D), f32),
        'norm_mix': gain(ks[5], (L, D)),
        'norm_ffn': gain(ks[6], (L, D)),
        'w_in': nrm(ks[7], (L, D, IN_WIDTH), D),
        'q_a_norm': gain(ks[8], (L, Q_LORA)),
        'kv_a_norm': gain(ks[9], (L, KV_LORA)),
        'w_uq': nrm(ks[10], (L, Q_LORA, MLA_HEADS * (QK_NOPE + QK_ROPE)), Q_LORA),
        'w_ukv': nrm(ks[11], (L, KV_LORA, MLA_HEADS * (QK_NOPE + V_HEAD)), KV_LORA),
        'w_o_mla': nrm(ks[12], (L, MLA_HEADS * V_HEAD, D), MLA_HEADS * V_HEAD),
        'conv_w': nrm(ks[13], (L, CONV_WIDTH, CONV_CH), CONV_WIDTH),
        'A_log': jnp.log(jax.random.uniform(ks[14], (L, GDN_HEADS), f32, 1.0, 16.0)),
        'dt_bias': dt + jnp.log(-jnp.expm1(-dt)),
        'gdn_norm': gain(ks[16], (L, GDN_DV)),
        'w_o_gdn': nrm(ks[17], (L, GDN_V, D), GDN_V),
        'w_o': nrm(ks[18], (L, D, D), D),
        'w_gate_up': nrm(ks[19], (L, D, 2 * D_FF), D),
        'w_down': nrm(ks[20], (L, D_FF, D), D_FF),
        'final_norm': gain(ks[21], (D,)),
    }


def _fwd_reference(x, c, positions, w_ada, b_ada, norm_mix, norm_ffn, w_in, q_a_norm, kv_a_norm,
              w_uq, w_ukv, w_o_mla, conv_w, A_log, dt_bias, gdn_norm, w_o_gdn, w_o,
              w_gate_up, w_down, final_norm):
    cos, sin = _rope_tables(positions)
    c_act = jax.nn.silu(c)
    for l in range(DEPTH):
        mod = c_act @ w_ada[l] + b_ada[l]
        sh_a, sc_a, gt_a, sh_f, sc_f, gt_f = [m[:, None, :] for m in jnp.split(mod, 6, axis=-1)]
        h = _rmsnorm(x, norm_mix[l]) * (1.0 + sc_a) + sh_a
        p = h @ w_in[l]
        c_q, c_kv, k_pe, q_g, k_g, v_g, z, b_logit, a_logit, gate_logits = _split_cols(p)
        y_a = _mla_branch(c_q, c_kv, k_pe, q_a_norm[l], kv_a_norm[l], w_uq[l], w_ukv[l], cos, sin) @ w_o_mla[l]
        qkv = jnp.concatenate([q_g, k_g, v_g], axis=-1)
        y_b = _gdn_branch(qkv, z, b_logit, a_logit, conv_w[l], A_log[l], dt_bias[l], gdn_norm[l]) @ w_o_gdn[l]
        g_a, g_b = jnp.split(jax.nn.sigmoid(gate_logits), 2, axis=-1)
        mix = (g_a * y_a + g_b * y_b) @ w_o[l]
        x = x + gt_a * mix
        h = _rmsnorm(x, norm_ffn[l]) * (1.0 + sc_f) + sh_f
        gate, up = jnp.split(h @ w_gate_up[l], 2, axis=-1)
        x = x + gt_f * ((jax.nn.silu(gate) * up) @ w_down[l])
    return _rmsnorm(x, final_norm)


import jax as _jax
import jax.numpy as _jnp

TWIN_FORMAT = 'train_step'
FWD_PARAMS = ['x', 'c', 'positions', 'w_ada', 'b_ada', 'norm_mix', 'norm_ffn', 'w_in', 'q_a_norm', 'kv_a_norm', 'w_uq', 'w_ukv', 'w_o_mla', 'conv_w', 'A_log', 'dt_bias', 'gdn_norm', 'w_o_gdn', 'w_o', 'w_gate_up', 'w_down', 'final_norm']
TWIN_WEIGHTS = ['w_ada', 'b_ada', 'norm_mix', 'norm_ffn', 'w_in', 'q_a_norm', 'kv_a_norm', 'w_uq', 'w_ukv', 'w_o_mla', 'conv_w', 'A_log', 'dt_bias', 'gdn_norm', 'w_o_gdn', 'w_o', 'w_gate_up', 'w_down', 'final_norm']
TWIN_DIFF_INPUT = 'x'
TWIN_INPUTS = ['x', 'c', 'positions', 'w_ada', 'b_ada', 'norm_mix', 'norm_ffn', 'w_in', 'q_a_norm', 'kv_a_norm', 'w_uq', 'w_ukv', 'w_o_mla', 'conv_w', 'A_log', 'dt_bias', 'gdn_norm', 'w_o_gdn', 'w_o', 'w_gate_up', 'w_down', 'final_norm', 'loss_target', 'm_w_ada', 'm_b_ada', 'm_norm_mix', 'm_norm_ffn', 'm_w_in', 'm_q_a_norm', 'm_kv_a_norm', 'm_w_uq', 'm_w_ukv', 'm_w_o_mla', 'm_conv_w', 'm_A_log', 'm_dt_bias', 'm_gdn_norm', 'm_w_o_gdn', 'm_w_o', 'm_w_gate_up', 'm_w_down', 'm_final_norm', 'v_w_ada', 'v_b_ada', 'v_norm_mix', 'v_norm_ffn', 'v_w_in', 'v_q_a_norm', 'v_kv_a_norm', 'v_w_uq', 'v_w_ukv', 'v_w_o_mla', 'v_conv_w', 'v_A_log', 'v_dt_bias', 'v_gdn_norm', 'v_w_o_gdn', 'v_w_o', 'v_w_gate_up', 'v_w_down', 'v_final_norm']
TWIN_OUTPUTS = ['loss', 'grad_x', 'grad_w_ada', 'grad_b_ada', 'grad_norm_mix', 'grad_norm_ffn', 'grad_w_in', 'grad_q_a_norm', 'grad_kv_a_norm', 'grad_w_uq', 'grad_w_ukv', 'grad_w_o_mla', 'grad_conv_w', 'grad_A_log', 'grad_dt_bias', 'grad_gdn_norm', 'grad_w_o_gdn', 'grad_w_o', 'grad_w_gate_up', 'grad_w_down', 'grad_final_norm', 'delta_w_ada', 'delta_b_ada', 'delta_norm_mix', 'delta_norm_ffn', 'delta_w_in', 'delta_q_a_norm', 'delta_kv_a_norm', 'delta_w_uq', 'delta_w_ukv', 'delta_w_o_mla', 'delta_conv_w', 'delta_A_log', 'delta_dt_bias', 'delta_gdn_norm', 'delta_w_o_gdn', 'delta_w_o', 'delta_w_gate_up', 'delta_w_down', 'delta_final_norm', 'new_m_w_ada', 'new_m_b_ada', 'new_m_norm_mix', 'new_m_norm_ffn', 'new_m_w_in', 'new_m_q_a_norm', 'new_m_kv_a_norm', 'new_m_w_uq', 'new_m_w_ukv', 'new_m_w_o_mla', 'new_m_conv_w', 'new_m_A_log', 'new_m_dt_bias', 'new_m_gdn_norm', 'new_m_w_o_gdn', 'new_m_w_o', 'new_m_w_gate_up', 'new_m_w_down', 'new_m_final_norm', 'new_v_w_ada', 'new_v_b_ada', 'new_v_norm_mix', 'new_v_norm_ffn', 'new_v_w_in', 'new_v_q_a_norm', 'new_v_kv_a_norm', 'new_v_w_uq', 'new_v_w_ukv', 'new_v_w_o_mla', 'new_v_conv_w', 'new_v_A_log', 'new_v_dt_bias', 'new_v_gdn_norm', 'new_v_w_o_gdn', 'new_v_w_o', 'new_v_w_gate_up', 'new_v_w_down', 'new_v_final_norm']
TWIN_LEAF_KINDS = {'loss': 'loss', 'grad_x': 'grad_x', 'grad_w_ada': 'grad_w', 'grad_b_ada': 'grad_w', 'grad_norm_mix': 'grad_w', 'grad_norm_ffn': 'grad_w', 'grad_w_in': 'grad_w', 'grad_q_a_norm': 'grad_w', 'grad_kv_a_norm': 'grad_w', 'grad_w_uq': 'grad_w', 'grad_w_ukv': 'grad_w', 'grad_w_o_mla': 'grad_w', 'grad_conv_w': 'grad_w', 'grad_A_log': 'grad_w', 'grad_dt_bias': 'grad_w', 'grad_gdn_norm': 'grad_w', 'grad_w_o_gdn': 'grad_w', 'grad_w_o': 'grad_w', 'grad_w_gate_up': 'grad_w', 'grad_w_down': 'grad_w', 'grad_final_norm': 'grad_w', 'delta_w_ada': 'delta_w', 'delta_b_ada': 'delta_w', 'delta_norm_mix': 'delta_w', 'delta_norm_ffn': 'delta_w', 'delta_w_in': 'delta_w', 'delta_q_a_norm': 'delta_w', 'delta_kv_a_norm': 'delta_w', 'delta_w_uq': 'delta_w', 'delta_w_ukv': 'delta_w', 'delta_w_o_mla': 'delta_w', 'delta_conv_w': 'delta_w', 'delta_A_log': 'delta_w', 'delta_dt_bias': 'delta_w', 'delta_gdn_norm': 'delta_w', 'delta_w_o_gdn': 'delta_w', 'delta_w_o': 'delta_w', 'delta_w_gate_up': 'delta_w', 'delta_w_down': 'delta_w', 'delta_final_norm': 'delta_w', 'new_m_w_ada': 'new_m', 'new_m_b_ada': 'new_m', 'new_m_norm_mix': 'new_m', 'new_m_norm_ffn': 'new_m', 'new_m_w_in': 'new_m', 'new_m_q_a_norm': 'new_m', 'new_m_kv_a_norm': 'new_m', 'new_m_w_uq': 'new_m', 'new_m_w_ukv': 'new_m', 'new_m_w_o_mla': 'new_m', 'new_m_conv_w': 'new_m', 'new_m_A_log': 'new_m', 'new_m_dt_bias': 'new_m', 'new_m_gdn_norm': 'new_m', 'new_m_w_o_gdn': 'new_m', 'new_m_w_o': 'new_m', 'new_m_w_gate_up': 'new_m', 'new_m_w_down': 'new_m', 'new_m_final_norm': 'new_m', 'new_v_w_ada': 'new_v', 'new_v_b_ada': 'new_v', 'new_v_norm_mix': 'new_v', 'new_v_norm_ffn': 'new_v', 'new_v_w_in': 'new_v', 'new_v_q_a_norm': 'new_v', 'new_v_kv_a_norm': 'new_v', 'new_v_w_uq': 'new_v', 'new_v_w_ukv': 'new_v', 'new_v_w_o_mla': 'new_v', 'new_v_conv_w': 'new_v', 'new_v_A_log': 'new_v', 'new_v_dt_bias': 'new_v', 'new_v_gdn_norm': 'new_v', 'new_v_w_o_gdn': 'new_v', 'new_v_w_o': 'new_v', 'new_v_w_gate_up': 'new_v', 'new_v_w_down': 'new_v', 'new_v_final_norm': 'new_v'}


def _forward(args):
    return _fwd_reference(*[args[k] for k in FWD_PARAMS])


def _output_shape():
    out = _jax.eval_shape(lambda: _forward(_fwd_setup_inputs(0)))
    return out.shape, out.dtype

N_MICROBATCH = 1
ADAM_LR = 0.001
ADAM_B1 = 0.9
ADAM_B2 = 0.999
ADAM_EPS = 1e-08
ADAM_WD = 0.01
ADAM_STEP = 10
PER_EXAMPLE_BATCH_AXIS = {'x': 0, 'c': 0, 'positions': 0, 'loss_target': 0}
SHARED_INPUTS = []
_WEIGHT_DTYPES = {'w_ada': _jnp.float32, 'b_ada': _jnp.float32, 'norm_mix': _jnp.float32, 'norm_ffn': _jnp.float32, 'w_in': _jnp.float32, 'q_a_norm': _jnp.float32, 'kv_a_norm': _jnp.float32, 'w_uq': _jnp.float32, 'w_ukv': _jnp.float32, 'w_o_mla': _jnp.float32, 'conv_w': _jnp.float32, 'A_log': _jnp.float32, 'dt_bias': _jnp.float32, 'gdn_norm': _jnp.float32, 'w_o_gdn': _jnp.float32, 'w_o': _jnp.float32, 'w_gate_up': _jnp.float32, 'w_down': _jnp.float32, 'final_norm': _jnp.float32}
MOMENT_SCALE = {'w_ada': 1.524433e-02, 'b_ada': 2.585075e-02, 'norm_mix': 1.143730e-02, 'norm_ffn': 1.832538e-02, 'w_in': 5.833526e-03, 'q_a_norm': 3.685971e-03, 'kv_a_norm': 8.040958e-03, 'w_uq': 2.068437e-03, 'w_ukv': 4.022176e-03, 'w_o_mla': 3.696968e-03, 'conv_w': 7.164805e-03, 'A_log': 3.848506e-02, 'dt_bias': 3.842281e-02, 'gdn_norm': 2.823911e-02, 'w_o_gdn': 6.821749e-03, 'w_o': 7.718621e-03, 'w_gate_up': 8.085250e-03, 'w_down': 1.319651e-02, 'final_norm': 8.002788e+00}


def _to_microbatches(a, axis):
    t = _jnp.moveaxis(a, axis, 0)
    t = t.reshape((N_MICROBATCH, t.shape[0] // N_MICROBATCH) + t.shape[1:])
    return _jnp.moveaxis(t, 1, axis + 1)


def setup_inputs(seed: int = 0) -> dict:
    inp = _fwd_setup_inputs(seed)
    key = _jax.random.fold_in(_jax.random.key(seed), 7919)
    shape, _ = _output_shape()
    out = dict(inp)
    out["loss_target"] = _jax.random.normal(_jax.random.fold_in(key, 0), shape, _jnp.float32)
    for i, name in enumerate(TWIN_WEIGHTS):
        w = inp[name].astype(_jnp.float32)
        if MOMENT_SCALE is None:
            s = _jnp.sqrt(_jnp.mean(_jnp.square(w)) + 1e-30)
        else:
            s = MOMENT_SCALE[name]
        km, kv = _jax.random.split(_jax.random.fold_in(key, i + 1))
        out[name] = w
        out["m_" + name] = s * _jax.random.normal(km, w.shape, _jnp.float32)
        out["v_" + name] = (s * s) * _jax.random.uniform(kv, w.shape, _jnp.float32, 0.5, 1.5)
    if N_MICROBATCH > 1:
        for name, axis in PER_EXAMPLE_BATCH_AXIS.items():
            out[name] = _to_microbatches(out[name], axis)
    return {'x': out['x'], 'c': out['c'], 'positions': out['positions'], 'w_ada': out['w_ada'], 'b_ada': out['b_ada'], 'norm_mix': out['norm_mix'], 'norm_ffn': out['norm_ffn'], 'w_in': out['w_in'], 'q_a_norm': out['q_a_norm'], 'kv_a_norm': out['kv_a_norm'], 'w_uq': out['w_uq'], 'w_ukv': out['w_ukv'], 'w_o_mla': out['w_o_mla'], 'conv_w': out['conv_w'], 'A_log': out['A_log'], 'dt_bias': out['dt_bias'], 'gdn_norm': out['gdn_norm'], 'w_o_gdn': out['w_o_gdn'], 'w_o': out['w_o'], 'w_gate_up': out['w_gate_up'], 'w_down': out['w_down'], 'final_norm': out['final_norm'], 'loss_target': out['loss_target'], 'm_w_ada': out['m_w_ada'], 'm_b_ada': out['m_b_ada'], 'm_norm_mix': out['m_norm_mix'], 'm_norm_ffn': out['m_norm_ffn'], 'm_w_in': out['m_w_in'], 'm_q_a_norm': out['m_q_a_norm'], 'm_kv_a_norm': out['m_kv_a_norm'], 'm_w_uq': out['m_w_uq'], 'm_w_ukv': out['m_w_ukv'], 'm_w_o_mla': out['m_w_o_mla'], 'm_conv_w': out['m_conv_w'], 'm_A_log': out['m_A_log'], 'm_dt_bias': out['m_dt_bias'], 'm_gdn_norm': out['m_gdn_norm'], 'm_w_o_gdn': out['m_w_o_gdn'], 'm_w_o': out['m_w_o'], 'm_w_gate_up': out['m_w_gate_up'], 'm_w_down': out['m_w_down'], 'm_final_norm': out['m_final_norm'], 'v_w_ada': out['v_w_ada'], 'v_b_ada': out['v_b_ada'], 'v_norm_mix': out['v_norm_mix'], 'v_norm_ffn': out['v_norm_ffn'], 'v_w_in': out['v_w_in'], 'v_q_a_norm': out['v_q_a_norm'], 'v_kv_a_norm': out['v_kv_a_norm'], 'v_w_uq': out['v_w_uq'], 'v_w_ukv': out['v_w_ukv'], 'v_w_o_mla': out['v_w_o_mla'], 'v_conv_w': out['v_conv_w'], 'v_A_log': out['v_A_log'], 'v_dt_bias': out['v_dt_bias'], 'v_gdn_norm': out['v_gdn_norm'], 'v_w_o_gdn': out['v_w_o_gdn'], 'v_w_o': out['v_w_o'], 'v_w_gate_up': out['v_w_gate_up'], 'v_w_down': out['v_w_down'], 'v_final_norm': out['v_final_norm']}


def _loss(weights, diff, rest, loss_target):
    with _jax.named_scope("forward"):
        args = {**rest, TWIN_DIFF_INPUT: diff, **{k: w.astype(_WEIGHT_DTYPES[k]) for k, w in weights.items()}}
        y = _forward(args)
    with _jax.named_scope("loss_head"):
        err = _jnp.square(y.astype(_jnp.float32) - loss_target)
        return 0.5 * _jnp.sum(_jnp.mean(err, axis=-1)) if err.ndim else 0.5 * err


def _adamw(w, g, m, v):
    m = ADAM_B1 * m + (1.0 - ADAM_B1) * g
    v = ADAM_B2 * v + (1.0 - ADAM_B2) * _jnp.square(g)
    m_hat = m / (1.0 - ADAM_B1 ** ADAM_STEP)
    v_hat = v / (1.0 - ADAM_B2 ** ADAM_STEP)
    delta = -ADAM_LR * (m_hat / (_jnp.sqrt(v_hat) + ADAM_EPS) + ADAM_WD * w)
    return delta, m, v


def reference(x, c, positions, w_ada, b_ada, norm_mix, norm_ffn, w_in, q_a_norm, kv_a_norm, w_uq, w_ukv, w_o_mla, conv_w, A_log, dt_bias, gdn_norm, w_o_gdn, w_o, w_gate_up, w_down, final_norm, loss_target, m_w_ada, m_b_ada, m_norm_mix, m_norm_ffn, m_w_in, m_q_a_norm, m_kv_a_norm, m_w_uq, m_w_ukv, m_w_o_mla, m_conv_w, m_A_log, m_dt_bias, m_gdn_norm, m_w_o_gdn, m_w_o, m_w_gate_up, m_w_down, m_final_norm, v_w_ada, v_b_ada, v_norm_mix, v_norm_ffn, v_w_in, v_q_a_norm, v_kv_a_norm, v_w_uq, v_w_ukv, v_w_o_mla, v_conv_w, v_A_log, v_dt_bias, v_gdn_norm, v_w_o_gdn, v_w_o, v_w_gate_up, v_w_down, v_final_norm):
    given = dict(x=x, c=c, positions=positions, w_ada=w_ada, b_ada=b_ada, norm_mix=norm_mix, norm_ffn=norm_ffn, w_in=w_in, q_a_norm=q_a_norm, kv_a_norm=kv_a_norm, w_uq=w_uq, w_ukv=w_ukv, w_o_mla=w_o_mla, conv_w=conv_w, A_log=A_log, dt_bias=dt_bias, gdn_norm=gdn_norm, w_o_gdn=w_o_gdn, w_o=w_o, w_gate_up=w_gate_up, w_down=w_down, final_norm=final_norm, loss_target=loss_target, m_w_ada=m_w_ada, m_b_ada=m_b_ada, m_norm_mix=m_norm_mix, m_norm_ffn=m_norm_ffn, m_w_in=m_w_in, m_q_a_norm=m_q_a_norm, m_kv_a_norm=m_kv_a_norm, m_w_uq=m_w_uq, m_w_ukv=m_w_ukv, m_w_o_mla=m_w_o_mla, m_conv_w=m_conv_w, m_A_log=m_A_log, m_dt_bias=m_dt_bias, m_gdn_norm=m_gdn_norm, m_w_o_gdn=m_w_o_gdn, m_w_o=m_w_o, m_w_gate_up=m_w_gate_up, m_w_down=m_w_down, m_final_norm=m_final_norm, v_w_ada=v_w_ada, v_b_ada=v_b_ada, v_norm_mix=v_norm_mix, v_norm_ffn=v_norm_ffn, v_w_in=v_w_in, v_q_a_norm=v_q_a_norm, v_kv_a_norm=v_kv_a_norm, v_w_uq=v_w_uq, v_w_ukv=v_w_ukv, v_w_o_mla=v_w_o_mla, v_conv_w=v_conv_w, v_A_log=v_A_log, v_dt_bias=v_dt_bias, v_gdn_norm=v_gdn_norm, v_w_o_gdn=v_w_o_gdn, v_w_o=v_w_o, v_w_gate_up=v_w_gate_up, v_w_down=v_w_down, v_final_norm=v_final_norm)
    weights = {n: given[n] for n in TWIN_WEIGHTS}
    shared = {n: given[n] for n in SHARED_INPUTS}
    per_example = {n: given[n] for n in ['x', 'c', 'positions']}
    grad_fn = _jax.value_and_grad(_loss, argnums=(0, 1))

    def one_microbatch(ex, loss_target):
        ex = dict(ex)
        diff = ex.pop(TWIN_DIFF_INPUT)
        return grad_fn(weights, diff, {**shared, **ex}, loss_target)

    if N_MICROBATCH == 1:
        loss, (grad_w, grad_x) = one_microbatch(per_example, given["loss_target"])
    else:
        def body(carry, xs):
            loss_sum, grad_sum = carry
            l_k, (gw_k, gx_k) = one_microbatch(xs[0], xs[1])
            with _jax.named_scope("update"):
                return (loss_sum + l_k, _jax.tree.map(_jnp.add, grad_sum, gw_k)), gx_k

        init = (_jnp.zeros((), _jnp.float32), _jax.tree.map(_jnp.zeros_like, weights))
        (loss, grad_w), grad_x = _jax.lax.scan(body, init, (per_example, given["loss_target"]))
    with _jax.named_scope("update"):
        delta_w, new_m, new_v = {}, {}, {}
        for n in TWIN_WEIGHTS:
            delta_w[n], new_m[n], new_v[n] = _adamw(weights[n], grad_w[n], given["m_" + n], given["v_" + n])
    return (loss, grad_x, *[grad_w[n] for n in TWIN_WEIGHTS], *[delta_w[n] for n in TWIN_WEIGHTS],
            *[new_m[n] for n in TWIN_WEIGHTS], *[new_v[n] for n in TWIN_WEIGHTS])
```

```python
import functools
import math

import jax
import jax.numpy as jnp
from jax import lax
from jax.experimental import pallas as pl
from jax.experimental.pallas import tpu as pltpu

F32 = jnp.float32
BF16 = jnp.bfloat16

MLA_HEADS = 8
QK_NOPE = 128
QK_ROPE = 64
V_HEAD = 128
Q_LORA = 512
KV_LORA = 512
ROPE_THETA = 10000.0
GDN_HEADS = 8
GDN_DK = 128
GDN_DV = 128
CONV_WIDTH = 4
CHUNK = 64
DEPTH = 2
EPS = 1e-6
ADAM_LR = 0.001
ADAM_B1 = 0.9
ADAM_B2 = 0.999
ADAM_EPS = 1e-08
ADAM_WD = 0.01
ADAM_STEP = 10

N_DEV = 8
AXES = ("x", "y", "c")
LANE = 128
VMEM_LIMIT = 48 * 1024 * 1024
HIGHEST = lax.Precision.HIGHEST

NN = (((1,), (0,)), ((), ()))
NT = (((1,), (1,)), ((), ()))
TN = (((0,), (0,)), ((), ()))


def _cp(sem=None):
    return pltpu.CompilerParams(dimension_semantics=sem, vmem_limit_bytes=VMEM_LIMIT)


def _tile(n, cap):
    if n <= cap:
        return n
    for t in range(cap - cap % LANE, 0, -LANE):
        if n % t == 0:
            return t
    return n


def _rows(t, cap=256):
    return cap if t % cap == 0 else t


def _pad_lanes(n):
    return -(-n // LANE) * LANE


def _sigmoid(x):
    return 1.0 / (1.0 + jnp.exp(-x))


def _softplus(x):
    return jnp.maximum(x, 0.0) + jnp.log(1.0 + jnp.exp(-jnp.abs(x)))


def _mm(a, b, dims, out_dtype, name, a_act=None):
    if dims == "nn":
        (m, k), (k2, n) = a.shape, b.shape
    elif dims == "nt":
        (m, k), (n, k2) = a.shape, b.shape
    else:
        (k, m), (k2, n) = a.shape, b.shape
    assert k == k2, (dims, a.shape, b.shape)
    tm, tn, tk = _tile(m, 1024), _tile(n, 512), _tile(k, 1024)
    nk = k // tk
    if dims == "tn":
        a_spec = pl.BlockSpec((tk, tm), lambda i, j, kk: (kk, i))
    else:
        a_spec = pl.BlockSpec((tm, tk), lambda i, j, kk: (i, kk))
    if dims == "nt":
        b_spec = pl.BlockSpec((tn, tk), lambda i, j, kk: (j, kk))
    else:
        b_spec = pl.BlockSpec((tk, tn), lambda i, j, kk: (kk, j))
    dn = {"nn": NN, "nt": NT, "tn": TN}[dims]

    def body(a_ref, b_ref, o_ref, acc_ref):
        kk = pl.program_id(2)

        @pl.when(kk == 0)
        def _():
            acc_ref[...] = jnp.zeros_like(acc_ref)

        av = a_ref[...]
        if a_act == "silu":
            av = av * _sigmoid(av)
        acc_ref[...] += lax.dot_general(av.astype(BF16), b_ref[...].astype(BF16), dn,
                                        preferred_element_type=F32)

        @pl.when(kk == nk - 1)
        def _():
            o_ref[...] = acc_ref[...].astype(o_ref.dtype)

    return pl.pallas_call(
        body, name=name, grid=(m // tm, n // tn, nk),
        in_specs=[a_spec, b_spec],
        out_specs=pl.BlockSpec((tm, tn), lambda i, j, kk: (i, j)),
        out_shape=jax.ShapeDtypeStruct((m, n), out_dtype),
        scratch_shapes=[pltpu.VMEM((tm, tn), F32)],
        compiler_params=_cp(("parallel", "parallel", "arbitrary")),
    )(a, b)


@functools.partial(jax.custom_vjp, nondiff_argnums=(2, 3))
def mm(a, b, tag, out_dtype):
    return _mm(a, b, "nn", out_dtype, "mm_" + tag)


def _mm_f(a, b, tag, out_dtype):
    return _mm(a, b, "nn", out_dtype, "mm_" + tag), (a, b)


def _mm_b(tag, out_dtype, res, g):
    a, b = res
    da = _mm(g, b, "nt", a.dtype, "mm_" + tag + "_da")
    db = _mm(a, g, "tn", b.dtype, "mm_" + tag + "_db")
    return da, db


mm.defvjp(_mm_f, _mm_b)


def _norm_fwd_call(x, nw, sc, sh, name):
    t, d = x.shape
    tr = _rows(t)
    mod = sc is not None
    row = pl.BlockSpec((tr, d), lambda i: (i, 0))
    vec = pl.BlockSpec((1, d), lambda i: (0, 0))

    def body(*refs):
        if mod:
            x_ref, nw_ref, sc_ref, sh_ref, o_ref = refs
        else:
            x_ref, nw_ref, o_ref = refs
        xv = x_ref[...]
        r = lax.rsqrt(jnp.mean(xv * xv, axis=-1, keepdims=True) + EPS)
        y = (xv * r) * nw_ref[...]
        if mod:
            y = y * (1.0 + sc_ref[...]) + sh_ref[...]
        o_ref[...] = y.astype(o_ref.dtype)

    args = (x, nw, sc, sh) if mod else (x, nw)
    return pl.pallas_call(
        body, name=name, grid=(t // tr,),
        in_specs=[row] + [vec] * (len(args) - 1), out_specs=row,
        out_shape=jax.ShapeDtypeStruct((t, d), BF16),
        compiler_params=_cp(("parallel",)),
    )(*args)


def _norm_bwd_call(x, nw, sc, dh, name):
    t, d = x.shape
    tr = _rows(t)
    mod = sc is not None
    row = pl.BlockSpec((tr, d), lambda i: (i, 0))
    vec = pl.BlockSpec((1, d), lambda i: (0, 0))

    def body(*refs):
        if mod:
            x_ref, nw_ref, sc_ref, dh_ref, dx_ref, dnw_ref, dsc_ref, dsh_ref = refs
        else:
            x_ref, nw_ref, dh_ref, dx_ref, dnw_ref = refs
        i = pl.program_id(0)
        xv = x_ref[...]
        dh = dh_ref[...].astype(F32)
        r = lax.rsqrt(jnp.mean(xv * xv, axis=-1, keepdims=True) + EPS)
        y = xv * r
        a = nw_ref[...] * (1.0 + sc_ref[...]) if mod else nw_ref[...]
        dy = dh * a
        dx_ref[...] = r * (dy - y * jnp.mean(dy * y, axis=-1, keepdims=True))
        da = jnp.sum(dh * y, axis=0, keepdims=True)

        @pl.when(i == 0)
        def _():
            dnw_ref[...] = jnp.zeros_like(dnw_ref)
            if mod:
                dsc_ref[...] = jnp.zeros_like(dsc_ref)
                dsh_ref[...] = jnp.zeros_like(dsh_ref)

        if mod:
            dnw_ref[...] += da * (1.0 + sc_ref[...])
            dsc_ref[...] += da * nw_ref[...]
            dsh_ref[...] += jnp.sum(dh, axis=0, keepdims=True)
        else:
            dnw_ref[...] += da

    args = (x, nw, sc, dh) if mod else (x, nw, dh)
    n_vec = 3 if mod else 1
    return pl.pallas_call(
        body, name=name, grid=(t // tr,),
        in_specs=[row] + [vec] * (len(args) - 2) + [row],
        out_specs=[row] + [vec] * n_vec,
        out_shape=[jax.ShapeDtypeStruct((t, d), F32)] + [jax.ShapeDtypeStruct((1, d), F32)] * n_vec,
        compiler_params=_cp(("arbitrary",)),
    )(*args)


@functools.partial(jax.custom_vjp, nondiff_argnums=(4,))
def ada_norm(x, nw, sc, sh, tag):
    return _norm_fwd_call(x, nw, sc, sh, "adanorm_" + tag)


def _ada_norm_f(x, nw, sc, sh, tag):
    return _norm_fwd_call(x, nw, sc, sh, "adanorm_" + tag), (x, nw, sc)


def _ada_norm_b(tag, res, dh):
    x, nw, sc = res
    dx, dnw, dsc, dsh = _norm_bwd_call(x, nw, sc, dh, "adanorm_" + tag + "_bwd")
    return dx, dnw, dsc, dsh


ada_norm.defvjp(_ada_norm_f, _ada_norm_b)


@functools.partial(jax.custom_vjp, nondiff_argnums=(2,))
def rms_norm(x, nw, tag):
    return _norm_fwd_call(x, nw, None, None, "rms_" + tag)


def _rms_norm_f(x, nw, tag):
    return _norm_fwd_call(x, nw, None, None, "rms_" + tag), (x, nw)


def _rms_norm_b(tag, res, dh):
    x, nw = res
    dx, dnw = _norm_bwd_call(x, nw, None, dh, "rms_" + tag + "_bwd")
    return dx, dnw


rms_norm.defvjp(_rms_norm_f, _rms_norm_b)


def _gate_mix_fwd_call(gl, ya, yb, name):
    t, d = ya.shape
    tr = _rows(t)
    row = pl.BlockSpec((tr, d), lambda i: (i, 0))

    def body(ga_ref, gb_ref, ya_ref, yb_ref, o_ref):
        o_ref[...] = (_sigmoid(ga_ref[...]) * ya_ref[...]
                      + _sigmoid(gb_ref[...]) * yb_ref[...]).astype(o_ref.dtype)

    return pl.pallas_call(
        body, name=name, grid=(t // tr,),
        in_specs=[row, pl.BlockSpec((tr, d), lambda i: (i, 1)), row, row], out_specs=row,
        out_shape=jax.ShapeDtypeStruct((t, d), BF16),
        compiler_params=_cp(("parallel",)),
    )(gl, gl, ya, yb)


def _gate_mix_bwd_call(gl, ya, yb, dm, name):
    t, d = ya.shape
    tr = _rows(t)
    row = pl.BlockSpec((tr, d), lambda i: (i, 0))
    wide = pl.BlockSpec((tr, 2 * d), lambda i: (i, 0))

    def body(gl_ref, ya_ref, yb_ref, dm_ref, dgl_ref, dya_ref, dyb_ref):
        dm = dm_ref[...].astype(F32)
        ga = _sigmoid(gl_ref[:, :d])
        gb = _sigmoid(gl_ref[:, d:])
        dya_ref[...] = dm * ga
        dyb_ref[...] = dm * gb
        dgl_ref[:, :d] = dm * ya_ref[...] * ga * (1.0 - ga)
        dgl_ref[:, d:] = dm * yb_ref[...] * gb * (1.0 - gb)

    return pl.pallas_call(
        body, name=name, grid=(t // tr,),
        in_specs=[wide, row, row, row], out_specs=[wide, row, row],
        out_shape=[jax.ShapeDtypeStruct((t, 2 * d), F32), jax.ShapeDtypeStruct((t, d), F32),
                   jax.ShapeDtypeStruct((t, d), F32)],
        compiler_params=_cp(("parallel",)),
    )(gl, ya, yb, dm)


@functools.partial(jax.custom_vjp, nondiff_argnums=(3,))
def gate_mix(gl, ya, yb, tag):
    return _gate_mix_fwd_call(gl, ya, yb, "gatemix_" + tag)


def _gate_mix_f(gl, ya, yb, tag):
    return _gate_mix_fwd_call(gl, ya, yb, "gatemix_" + tag), (gl, ya, yb)


def _gate_mix_b(tag, res, dm):
    return tuple(_gate_mix_bwd_call(*res, dm, "gatemix_" + tag + "_bwd"))


gate_mix.defvjp(_gate_mix_f, _gate_mix_b)


def _resid_fwd_call(x, gt, m, name):
    t, d = x.shape
    tr = _rows(t)
    row = pl.BlockSpec((tr, d), lambda i: (i, 0))
    vec = pl.BlockSpec((1, d), lambda i: (0, 0))

    def body(x_ref, gt_ref, m_ref, o_ref):
        o_ref[...] = x_ref[...] + gt_ref[...] * m_ref[...]

    return pl.pallas_call(
        body, name=name, grid=(t // tr,), in_specs=[row, vec, row], out_specs=row,
        out_shape=jax.ShapeDtypeStruct((t, d), F32), compiler_params=_cp(("parallel",)),
    )(x, gt, m)


def _resid_bwd_call(gt, m, g, name):
    t, d = m.shape
    tr = _rows(t)
    row = pl.BlockSpec((tr, d), lambda i: (i, 0))
    vec = pl.BlockSpec((1, d), lambda i: (0, 0))

    def body(gt_ref, m_ref, g_ref, dm_ref, dgt_ref):
        i = pl.program_id(0)
        g = g_ref[...]
        dm_ref[...] = g * gt_ref[...]

        @pl.when(i == 0)
        def _():
            dgt_ref[...] = jnp.zeros_like(dgt_ref)

        dgt_ref[...] += jnp.sum(g * m_ref[...], axis=0, keepdims=True)

    return pl.pallas_call(
        body, name=name, grid=(t // tr,), in_specs=[vec, row, row], out_specs=[row, vec],
        out_shape=[jax.ShapeDtypeStruct((t, d), F32), jax.ShapeDtypeStruct((1, d), F32)],
        compiler_params=_cp(("arbitrary",)),
    )(gt, m, g)


@functools.partial(jax.custom_vjp, nondiff_argnums=(3,))
def resid(x, gt, m, tag):
    return _resid_fwd_call(x, gt, m, "resid_" + tag)


def _resid_f(x, gt, m, tag):
    return _resid_fwd_call(x, gt, m, "resid_" + tag), (gt, m)


def _resid_b(tag, res, g):
    gt, m = res
    dm, dgt = _resid_bwd_call(gt, m, g, "resid_" + tag + "_bwd")
    return g, dgt, dm


resid.defvjp(_resid_f, _resid_b)


def _swiglu_fwd_call(gu, name):
    t, f2 = gu.shape
    f = f2 // 2
    tr = _rows(t, 128)
    half = pl.BlockSpec((tr, f), lambda i: (i, 0))

    def body(g_ref, u_ref, o_ref):
        g = g_ref[...]
        o_ref[...] = (g * _sigmoid(g) * u_ref[...]).astype(o_ref.dtype)

    return pl.pallas_call(
        body, name=name, grid=(t // tr,),
        in_specs=[half, pl.BlockSpec((tr, f), lambda i: (i, 1))], out_specs=half,
        out_shape=jax.ShapeDtypeStruct((t, f), BF16), compiler_params=_cp(("parallel",)),
    )(gu, gu)


def _swiglu_bwd_call(gu, da, name):
    t, f2 = gu.shape
    f = f2 // 2
    tr = _rows(t, 128)
    wide = pl.BlockSpec((tr, f2), lambda i: (i, 0))

    def body(gu_ref, da_ref, dgu_ref):
        g = gu_ref[:, :f]
        u = gu_ref[:, f:]
        da = da_ref[...].astype(F32)
        s = _sigmoid(g)
        dgu_ref[:, :f] = da * u * s * (1.0 + g * (1.0 - s))
        dgu_ref[:, f:] = da * g * s

    return pl.pallas_call(
        body, name=name, grid=(t // tr,),
        in_specs=[wide, pl.BlockSpec((tr, f), lambda i: (i, 0))], out_specs=wide,
        out_shape=jax.ShapeDtypeStruct((t, f2), F32), compiler_params=_cp(("parallel",)),
    )(gu, da)


@functools.partial(jax.custom_vjp, nondiff_argnums=(1,))
def swiglu(gu, tag):
    return _swiglu_fwd_call(gu, "swiglu_" + tag)


def _swiglu_f(gu, tag):
    return _swiglu_fwd_call(gu, "swiglu_" + tag), (gu,)


def _swiglu_b(tag, res, da):
    return (_swiglu_bwd_call(res[0], da, "swiglu_" + tag + "_bwd"),)


swiglu.defvjp(_swiglu_f, _swiglu_b)


def loss_head(x, fw, tgt):
    t, d = x.shape
    tr = _rows(t)
    row = pl.BlockSpec((tr, d), lambda i: (i, 0))
    vec = pl.BlockSpec((1, d), lambda i: (0, 0))
    tile = pl.BlockSpec((8, LANE), lambda i: (0, 0))

    def body(x_ref, fw_ref, tgt_ref, loss_ref, dx_ref, dfw_ref):
        i = pl.program_id(0)
        xv = x_ref[...]
        fw = fw_ref[...]
        r = lax.rsqrt(jnp.mean(xv * xv, axis=-1, keepdims=True) + EPS)
        yh = xv * r
        e = yh * fw - tgt_ref[...]
        dy = e * (1.0 / d)
        dyw = dy * fw
        dx_ref[...] = r * (dyw - yh * jnp.mean(dyw * yh, axis=-1, keepdims=True))

        @pl.when(i == 0)
        def _():
            loss_ref[...] = jnp.zeros_like(loss_ref)
            dfw_ref[...] = jnp.zeros_like(dfw_ref)

        loss_ref[...] += 0.5 * jnp.sum(jnp.mean(e * e, axis=-1, keepdims=True))
        dfw_ref[...] += jnp.sum(dy * yh, axis=0, keepdims=True)

    return pl.pallas_call(
        body, name="loss_head", grid=(t // tr,), in_specs=[row, vec, row],
        out_specs=[tile, row, vec],
        out_shape=[jax.ShapeDtypeStruct((8, LANE), F32), jax.ShapeDtypeStruct((t, d), F32),
                   jax.ShapeDtypeStruct((1, d), F32)],
        compiler_params=_cp(("arbitrary",)),
    )(x, fw, tgt)


def _attn_fwd_call(qn, qr, kv, kr, name):
    t = qn.shape[0]
    h_n = MLA_HEADS
    tq = _rows(t)
    nq = t // tq
    scale = (QK_NOPE + QK_ROPE) ** -0.5

    def body(qn_ref, qr_ref, kn_ref, v_ref, kr_ref, o_ref, lse_ref, m_scr, l_scr, acc_scr):
        i, j = pl.program_id(1), pl.program_id(2)

        @pl.when(j == 0)
        def _():
            m_scr[...] = jnp.full_like(m_scr, -1e30)
            l_scr[...] = jnp.zeros_like(l_scr)
            acc_scr[...] = jnp.zeros_like(acc_scr)

        @pl.when(j <= i)
        def _():
            s = lax.dot_general(qn_ref[...].astype(BF16), kn_ref[...].astype(BF16), NT,
                                preferred_element_type=F32)
            s += lax.dot_general(qr_ref[...].astype(BF16), kr_ref[...].astype(BF16), NT,
                                 preferred_element_type=F32)
            s = s * scale
            rows = i * tq + lax.broadcasted_iota(jnp.int32, (tq, tq), 0)
            cols = j * tq + lax.broadcasted_iota(jnp.int32, (tq, tq), 1)
            s = jnp.where(cols <= rows, s, -1e30)
            m_old = m_scr[...]
            m_new = jnp.maximum(m_old, jnp.max(s, axis=-1, keepdims=True))
            p = jnp.exp(s - m_new)
            alpha = jnp.exp(m_old - m_new)
            l_scr[...] = alpha * l_scr[...] + jnp.sum(p, axis=-1, keepdims=True)
            acc_scr[...] = alpha * acc_scr[...] + jnp.dot(p.astype(BF16), v_ref[...].astype(BF16),
                                                           preferred_element_type=F32)
            m_scr[...] = m_new

        @pl.when(j == i)
        def _():
            o_ref[...] = (acc_scr[...] / l_scr[...]).astype(o_ref.dtype)
            lse_ref[...] = m_scr[...] + jnp.log(l_scr[...])

    return pl.pallas_call(
        body, name=name, grid=(h_n, nq, nq),
        in_specs=[
            pl.BlockSpec((tq, QK_NOPE), lambda h, i, j: (i, h)),
            pl.BlockSpec((None, tq, QK_ROPE), lambda h, i, j: (h, i, 0)),
            pl.BlockSpec((tq, QK_NOPE), lambda h, i, j: (jnp.minimum(j, i), 2 * h)),
            pl.BlockSpec((tq, V_HEAD), lambda h, i, j: (jnp.minimum(j, i), 2 * h + 1)),
            pl.BlockSpec((tq, QK_ROPE), lambda h, i, j: (jnp.minimum(j, i), 0)),
        ],
        out_specs=[
            pl.BlockSpec((tq, V_HEAD), lambda h, i, j: (i, h)),
            pl.BlockSpec((None, tq, 1), lambda h, i, j: (h, i, 0)),
        ],
        out_shape=[jax.ShapeDtypeStruct((t, h_n * V_HEAD), BF16),
                   jax.ShapeDtypeStruct((h_n, t, 1), F32)],
        scratch_shapes=[pltpu.VMEM((tq, 1), F32), pltpu.VMEM((tq, 1), F32),
                        pltpu.VMEM((tq, V_HEAD), F32)],
        compiler_params=_cp(("parallel", "parallel", "arbitrary")),
    )(qn, qr, kv, kv, kr)


def _attn_bwd_call(qn, qr, kv, kr, o, lse, do, name):
    t = qn.shape[0]
    h_n = MLA_HEADS
    tq = _rows(t)
    nq = t // tq
    scale = (QK_NOPE + QK_ROPE) ** -0.5

    def body(qn_ref, qr_ref, kn_ref, v_ref, kr_ref, o_ref, lse_ref, do_ref,
             dqn_ref, dqr_ref, dkv_ref, dkr_ref, dkn_scr, dv_scr, dkr_scr):
        j, i = pl.program_id(1), pl.program_id(2)

        @pl.when(jnp.logical_and(j == 0, i == 0))
        def _():
            dqn_ref[...] = jnp.zeros_like(dqn_ref)
            dqr_ref[...] = jnp.zeros_like(dqr_ref)

        @pl.when(i == 0)
        def _():
            dkn_scr[...] = jnp.zeros_like(dkn_scr)
            dv_scr[...] = jnp.zeros_like(dv_scr)
            dkr_scr[...] = jnp.zeros_like(dkr_scr)

        @pl.when(i >= j)
        def _():
            qn_b = qn_ref[...].astype(BF16)
            qr_b = qr_ref[...].astype(BF16)
            kn_b = kn_ref[...].astype(BF16)
            kr_b = kr_ref[...].astype(BF16)
            do_b = do_ref[...]
            s = lax.dot_general(qn_b, kn_b, NT, preferred_element_type=F32)
            s += lax.dot_general(qr_b, kr_b, NT, preferred_element_type=F32)
            s = s * scale
            rows = i * tq + lax.broadcasted_iota(jnp.int32, (tq, tq), 0)
            cols = j * tq + lax.broadcasted_iota(jnp.int32, (tq, tq), 1)
            p = jnp.where(cols <= rows, jnp.exp(s - lse_ref[...]), 0.0)
            delta = jnp.sum(do_b.astype(F32) * o_ref[...].astype(F32), axis=-1, keepdims=True)
            dp = lax.dot_general(do_b, v_ref[...].astype(BF16), NT, preferred_element_type=F32)
            ds = (p * (dp - delta) * scale).astype(BF16)
            p_b = p.astype(BF16)
            dv_scr[...] += lax.dot_general(p_b, do_b, TN, preferred_element_type=F32)
            dkn_scr[...] += lax.dot_general(ds, qn_b, TN, preferred_element_type=F32)
            dkr_scr[...] += lax.dot_general(ds, qr_b, TN, preferred_element_type=F32)
            sl = pl.ds(pl.multiple_of(i * tq, tq), tq)
            dqn_ref[sl, :] += jnp.dot(ds, kn_b, preferred_element_type=F32)
            dqr_ref[sl, :] += jnp.dot(ds, kr_b, preferred_element_type=F32)

        @pl.when(i == nq - 1)
        def _():
            dkv_ref[:, :QK_NOPE] = dkn_scr[...]
            dkv_ref[:, QK_NOPE:] = dv_scr[...]
            dkr_ref[...] = dkr_scr[...]

    qi = lambda j, i: jnp.maximum(i, j)
    return pl.pallas_call(
        body, name=name, grid=(h_n, nq, nq),
        in_specs=[
            pl.BlockSpec((tq, QK_NOPE), lambda h, j, i: (qi(j, i), h)),
            pl.BlockSpec((None, tq, QK_ROPE), lambda h, j, i: (h, qi(j, i), 0)),
            pl.BlockSpec((tq, QK_NOPE), lambda h, j, i: (j, 2 * h)),
            pl.BlockSpec((tq, V_HEAD), lambda h, j, i: (j, 2 * h + 1)),
            pl.BlockSpec((tq, QK_ROPE), lambda h, j, i: (j, 0)),
            pl.BlockSpec((tq, V_HEAD), lambda h, j, i: (qi(j, i), h)),
            pl.BlockSpec((None, tq, 1), lambda h, j, i: (h, qi(j, i), 0)),
            pl.BlockSpec((tq, V_HEAD), lambda h, j, i: (qi(j, i), h)),
        ],
        out_specs=[
            pl.BlockSpec((t, QK_NOPE), lambda h, j, i: (0, h)),
            pl.BlockSpec((None, t, QK_ROPE), lambda h, j, i: (h, 0, 0)),
            pl.BlockSpec((tq, QK_NOPE + V_HEAD), lambda h, j, i: (j, h)),
            pl.BlockSpec((None, tq, QK_ROPE), lambda h, j, i: (h, j, 0)),
        ],
        out_shape=[jax.ShapeDtypeStruct((t, h_n * QK_NOPE), F32),
                   jax.ShapeDtypeStruct((h_n, t, QK_ROPE), F32),
                   jax.ShapeDtypeStruct((t, h_n * (QK_NOPE + V_HEAD)), F32),
                   jax.ShapeDtypeStruct((h_n, t, QK_ROPE), F32)],
        scratch_shapes=[pltpu.VMEM((tq, QK_NOPE), F32), pltpu.VMEM((tq, V_HEAD), F32),
                        pltpu.VMEM((tq, QK_ROPE), F32)],
        compiler_params=_cp(("parallel", "arbitrary", "arbitrary")),
    )(qn, qr, kv, kv, kr, o, lse, do)


@functools.partial(jax.custom_vjp, nondiff_argnums=(4,))
def attention(qn, qr, kv, kr, tag):
    return _attn_fwd_call(qn, qr, kv, kr, "attn_" + tag)[0]


def _attention_f(qn, qr, kv, kr, tag):
    o, lse = _attn_fwd_call(qn, qr, kv, kr, "attn_" + tag)
    return o, (qn, qr, kv, kr, o, lse)


def _attention_b(tag, res, do):
    dqn, dqr, dkv, dkr_h = _attn_bwd_call(*res, do, "attn_" + tag + "_bwd")
    return dqn, dqr, dkv, jnp.sum(dkr_h, axis=0)


attention.defvjp(_attention_f, _attention_b)


def _shift_down(u, s):
    if s == 0:
        return u
    t = u.shape[0]
    rolled = pltpu.roll(u, s, 0)
    return jnp.where(lax.broadcasted_iota(jnp.int32, u.shape, 0) >= s, rolled, 0.0)


def _shift_up(u, s):
    if s == 0:
        return u
    t = u.shape[0]
    rolled = pltpu.roll(u, t - s, 0)
    return jnp.where(lax.broadcasted_iota(jnp.int32, u.shape, 0) < t - s, rolled, 0.0)


def _conv_blocks(t, c3):
    p = c3 // 3
    tc = _tile(p, 512)
    per = p // tc
    return p, tc, per


def _conv_fwd_call(u, w, name):
    t, c3 = u.shape
    p, tc, per = _conv_blocks(t, c3)

    def body(u_ref, w_ref, o_ref):
        u = u_ref[...]
        y = jnp.zeros_like(u)
        for j in range(CONV_WIDTH):
            y = y + w_ref[j:j + 1, :] * _shift_down(u, CONV_WIDTH - 1 - j)
        o_ref[...] = y * _sigmoid(y)

    return pl.pallas_call(
        body, name=name, grid=(c3 // tc,),
        in_specs=[pl.BlockSpec((t, tc), lambda cb: (0, cb)),
                  pl.BlockSpec((CONV_WIDTH, tc), lambda cb: (0, cb))],
        out_specs=pl.BlockSpec((None, t, tc), lambda cb: (cb // per, 0, cb % per)),
        out_shape=jax.ShapeDtypeStruct((3, t, p), F32),
        compiler_params=_cp(("parallel",)),
    )(u, w)


def _conv_bwd_call(u, w, do, name):
    t, c3 = u.shape
    p, tc, per = _conv_blocks(t, c3)

    def body(u_ref, w_ref, do_ref, du_ref, dw_ref):
        u = u_ref[...]
        shifted = [_shift_down(u, CONV_WIDTH - 1 - j) for j in range(CONV_WIDTH)]
        y = jnp.zeros_like(u)
        for j in range(CONV_WIDTH):
            y = y + w_ref[j:j + 1, :] * shifted[j]
        s = _sigmoid(y)
        dy = do_ref[...] * s * (1.0 + y * (1.0 - s))
        du = jnp.zeros_like(u)
        for j in range(CONV_WIDTH):
            du = du + w_ref[j:j + 1, :] * _shift_up(dy, CONV_WIDTH - 1 - j)
            dw_ref[j:j + 1, :] = jnp.sum(dy * shifted[j], axis=0, keepdims=True)
        du_ref[...] = du

    return pl.pallas_call(
        body, name=name, grid=(c3 // tc,),
        in_specs=[pl.BlockSpec((t, tc), lambda cb: (0, cb)),
                  pl.BlockSpec((CONV_WIDTH, tc), lambda cb: (0, cb)),
                  pl.BlockSpec((None, t, tc), lambda cb: (cb // per, 0, cb % per))],
        out_specs=[pl.BlockSpec((t, tc), lambda cb: (0, cb)),
                   pl.BlockSpec((CONV_WIDTH, tc), lambda cb: (0, cb))],
        out_shape=[jax.ShapeDtypeStruct((t, c3), F32), jax.ShapeDtypeStruct((CONV_WIDTH, c3), F32)],
        compiler_params=_cp(("parallel",)),
    )(u, w, do)


@functools.partial(jax.custom_vjp, nondiff_argnums=(2,))
def conv_silu(u, w, tag):
    return _conv_fwd_call(u, w, "conv_" + tag)


def _conv_silu_f(u, w, tag):
    return _conv_fwd_call(u, w, "conv_" + tag), (u, w)


def _conv_silu_b(tag, res, do):
    return tuple(_conv_bwd_call(*res, do, "conv_" + tag + "_bwd"))


conv_silu.defvjp(_conv_silu_f, _conv_silu_b)


def _hdot(a, b, dn=NN):
    return lax.dot_general(a, b, dn, precision=HIGHEST, preferred_element_type=F32)


def _gdn_chunk(q, k, v, z, bl, al, a_log, dtb, gn, s):
    c = q.shape[0]
    ri = lax.broadcasted_iota(jnp.int32, (c, c), 0)
    ci = lax.broadcasted_iota(jnp.int32, (c, c), 1)
    lower = ri >= ci
    strict = ri > ci
    low_incl = lower.astype(F32)
    up_incl = (ri <= ci).astype(F32)
    eye = (ri == ci).astype(F32)

    q = q * lax.rsqrt(jnp.sum(q * q, axis=-1, keepdims=True) + EPS) * (GDN_DK ** -0.5)
    k = k * lax.rsqrt(jnp.sum(k * k, axis=-1, keepdims=True) + EPS)
    beta = _sigmoid(bl)
    g = -jnp.exp(a_log) * _softplus(al + dtb)
    g_w = jnp.broadcast_to(g, (c, LANE))
    gc = _hdot(low_incl, g_w)
    gr = _hdot(g_w[:, :c], up_incl, TN)
    diff = gc[:, :c] - gr
    decay = jnp.where(lower, jnp.exp(jnp.where(lower, diff, 0.0)), 0.0)
    kb = k * beta
    lmat = jnp.where(strict, _hdot(kb, k, NT) * decay, 0.0)
    inv = eye - lmat
    pw = lmat
    for _ in range(int(math.log2(c)) - 1):
        pw = _hdot(pw, pw)
        inv = _hdot(inv, eye + pw)
    eg = jnp.exp(gc)
    u = _hdot(inv, v * beta)
    w = _hdot(inv, kb * eg)
    attn = jnp.where(lower, _hdot(q, k, NT) * decay, 0.0)
    v_new = u - _hdot(w, s)
    o = _hdot(q * eg, s) + _hdot(attn, v_new)
    g_last = jnp.sum(g_w, axis=0, keepdims=True)
    k_dec = k * jnp.exp(g_last - gc)
    s_new = s * jnp.exp(g_last) + _hdot(k_dec, v_new, TN)
    on = o * lax.rsqrt(jnp.mean(o * o, axis=-1, keepdims=True) + EPS) * gn
    return on * (z * _sigmoid(z)), s_new


def _gdn_specs(n_chunks, rev):
    c = CHUNK
    nn = (lambda n: n_chunks - 1 - n) if rev else (lambda n: n)
    plane = lambda pidx: pl.BlockSpec((None, c, GDN_DK), lambda h, n: (pidx, nn(n), h))
    col = pl.BlockSpec((None, c, 1), lambda h, n: (h, nn(n), 0))
    scal = pl.BlockSpec((None, 1, 1), lambda h, n: (h, 0, 0))
    zspec = pl.BlockSpec((c, GDN_DV), lambda h, n: (nn(n), h))
    gnspec = pl.BlockSpec((1, GDN_DV), lambda h, n: (0, 0))
    sspec = pl.BlockSpec((None, None, GDN_DK, GDN_DV), lambda h, n: (h, nn(n), 0, 0))
    return plane, col, scal, zspec, gnspec, sspec


def _gdn_fwd_call(qkv, z, bl, al, a_log, dtb, gn, name):
    t = z.shape[0]
    h_n = GDN_HEADS
    n_chunks = t // CHUNK
    plane, col, scal, zspec, gnspec, sspec = _gdn_specs(n_chunks, False)

    def body(q_ref, k_ref, v_ref, z_ref, bl_ref, al_ref, a_ref, dtb_ref, gn_ref, o_ref, sall_ref, s_scr):
        n = pl.program_id(1)

        @pl.when(n == 0)
        def _():
            s_scr[...] = jnp.zeros_like(s_scr)

        s = s_scr[...]
        sall_ref[...] = s
        o, s_new = _gdn_chunk(q_ref[...], k_ref[...], v_ref[...], z_ref[...], bl_ref[...], al_ref[...],
                              a_ref[...], dtb_ref[...], gn_ref[...], s)
        o_ref[...] = o.astype(o_ref.dtype)
        s_scr[...] = s_new

    return pl.pallas_call(
        body, name=name, grid=(h_n, n_chunks),
        in_specs=[plane(0), plane(1), plane(2), zspec, col, col, scal, scal, gnspec],
        out_specs=[zspec, sspec],
        out_shape=[jax.ShapeDtypeStruct((t, h_n * GDN_DV), BF16),
                   jax.ShapeDtypeStruct((h_n, n_chunks, GDN_DK, GDN_DV), F32)],
        scratch_shapes=[pltpu.VMEM((GDN_DK, GDN_DV), F32)],
        compiler_params=_cp(("parallel", "arbitrary")),
    )(qkv, qkv, qkv, z, bl, al, a_log, dtb, gn)


def _gdn_bwd_call(qkv, z, bl, al, a_log, dtb, gn, sall, do, name):
    t = z.shape[0]
    h_n = GDN_HEADS
    n_chunks = t // CHUNK
    c = CHUNK
    plane, col, scal, zspec, gnspec, sspec = _gdn_specs(n_chunks, True)
    dplanes = pl.BlockSpec((3, c, GDN_DK), lambda h, n: (0, n_chunks - 1 - n, h))
    gnh = pl.BlockSpec((None, 1, GDN_DV), lambda h, n: (h, 0, 0))

    def body(q_ref, k_ref, v_ref, z_ref, bl_ref, al_ref, a_ref, dtb_ref, gn_ref, s_ref, do_ref,
             dqkv_ref, dz_ref, dbl_ref, dal_ref, da_ref, ddtb_ref, dgn_ref, ds_scr):
        n = pl.program_id(1)

        @pl.when(n == 0)
        def _():
            ds_scr[...] = jnp.zeros_like(ds_scr)
            da_ref[...] = jnp.zeros_like(da_ref)
            ddtb_ref[...] = jnp.zeros_like(ddtb_ref)
            dgn_ref[...] = jnp.zeros_like(dgn_ref)

        _, vjp = jax.vjp(_gdn_chunk, q_ref[...], k_ref[...], v_ref[...], z_ref[...], bl_ref[...],
                         al_ref[...], a_ref[...], dtb_ref[...], gn_ref[...], s_ref[...])
        dq, dk, dv, dz, dbl, dal, da, ddtb, dgn, ds = vjp((do_ref[...].astype(F32), ds_scr[...]))
        dqkv_ref[0] = dq
        dqkv_ref[1] = dk
        dqkv_ref[2] = dv
        dz_ref[...] = dz
        dbl_ref[...] = dbl
        dal_ref[...] = dal
        da_ref[...] += da
        ddtb_ref[...] += ddtb
        dgn_ref[...] += dgn
        ds_scr[...] = ds

    return pl.pallas_call(
        body, name=name, grid=(h_n, n_chunks),
        in_specs=[plane(0), plane(1), plane(2), zspec, col, col, scal, scal, gnspec, sspec, zspec],
        out_specs=[dplanes, zspec, col, col, scal, scal, gnh],
        out_shape=[jax.ShapeDtypeStruct((3, t, h_n * GDN_DK), F32),
                   jax.ShapeDtypeStruct((t, h_n * GDN_DV), F32),
                   jax.ShapeDtypeStruct((h_n, t, 1), F32), jax.ShapeDtypeStruct((h_n, t, 1), F32),
                   jax.ShapeDtypeStruct((h_n, 1, 1), F32), jax.ShapeDtypeStruct((h_n, 1, 1), F32),
                   jax.ShapeDtypeStruct((h_n, 1, GDN_DV), F32)],
        scratch_shapes=[pltpu.VMEM((GDN_DK, GDN_DV), F32)],
        compiler_params=_cp(("parallel", "arbitrary")),
    )(qkv, qkv, qkv, z, bl, al, a_log, dtb, gn, sall, do)


@functools.partial(jax.custom_vjp, nondiff_argnums=(7,))
def gdn(qkv, z, bl, al, a_log, dtb, gn, tag):
    return _gdn_fwd_call(qkv, z, bl, al, a_log, dtb, gn, "gdn_" + tag)[0]


def _gdn_f(qkv, z, bl, al, a_log, dtb, gn, tag):
    o, sall = _gdn_fwd_call(qkv, z, bl, al, a_log, dtb, gn, "gdn_" + tag)
    return o, (qkv, z, bl, al, a_log, dtb, gn, sall)


def _gdn_b(tag, res, do):
    dqkv, dz, dbl, dal, da, ddtb, dgn_h = _gdn_bwd_call(*res, do, "gdn_" + tag + "_bwd")
    return dqkv, dz, dbl, dal, da, ddtb, jnp.sum(dgn_h, axis=0)


gdn.defvjp(_gdn_f, _gdn_b)


def adamw(w, parts, m, v, name):
    r, c = w.shape
    n_parts = parts.shape[0]
    tr = r
    for cand in (512, 256, 128, 64, 32, 16, 8):
        if r % cand == 0 and cand * c <= 256 * 1024:
            tr = cand
            break
    blk = pl.BlockSpec((tr, c), lambda i: (i, 0))
    bc1 = 1.0 - ADAM_B1 ** ADAM_STEP
    bc2 = 1.0 - ADAM_B2 ** ADAM_STEP

    def body(w_ref, p_ref, m_ref, v_ref, g_ref, d_ref, mo_ref, vo_ref):
        g = p_ref[0].astype(F32)
        for i in range(1, n_parts):
            g = g + p_ref[i].astype(F32)
        m2 = ADAM_B1 * m_ref[...] + (1.0 - ADAM_B1) * g
        v2 = ADAM_B2 * v_ref[...] + (1.0 - ADAM_B2) * (g * g)
        g_ref[...] = g
        mo_ref[...] = m2
        vo_ref[...] = v2
        d_ref[...] = -ADAM_LR * ((m2 / bc1) / (jnp.sqrt(v2 / bc2) + ADAM_EPS) + ADAM_WD * w_ref[...])

    return pl.pallas_call(
        body, name=name, grid=(r // tr,),
        in_specs=[blk, pl.BlockSpec((n_parts, tr, c), lambda i: (0, i, 0)), blk, blk],
        out_specs=[blk] * 4, out_shape=[jax.ShapeDtypeStruct((r, c), F32)] * 4,
        compiler_params=_cp(("parallel",)),
    )(w, parts, m, v)


def exchange(arrays, modes, name):
    n = len(arrays)
    hbm = pl.BlockSpec(memory_space=pltpu.HBM)
    out_shape = [jax.ShapeDtypeStruct(a.shape if md == "scatter" else (N_DEV,) + a.shape, a.dtype)
                 for a, md in zip(arrays, modes)]

    def body(*refs):
        ins, outs = refs[:n], refs[n:2 * n]
        send_sems, recv_sems, local_sems = refs[2 * n:]
        x, y, c = (lax.axis_index(a) for a in AXES)
        me = 4 * x + 2 * y + c

        def src(k, p):
            return ins[k].at[p] if modes[k] == "scatter" else ins[k]

        local = [pltpu.make_async_copy(src(k, me), outs[k].at[me], local_sems.at[k]) for k in range(n)]
        for cp in local:
            cp.start()
        started = []
        for d in range(1, N_DEV):
            px = 1 - x if d & 4 else x
            py = 1 - y if d & 2 else y
            pc = 1 - c if d & 1 else c
            pid = 4 * px + 2 * py + pc
            for k in range(n):
                pltpu.make_async_remote_copy(
                    src_ref=src(k, pid), dst_ref=outs[k].at[me],
                    send_sem=send_sems.at[k, d - 1], recv_sem=recv_sems.at[k, d - 1],
                    device_id=(px, py, pc), device_id_type=pl.DeviceIdType.MESH).start()
                started.append((k, d, pid, (px, py, pc)))
        for k, d, pid, peer in started:
            pltpu.make_async_remote_copy(
                src_ref=src(k, pid), dst_ref=outs[k].at[pid],
                send_sem=send_sems.at[k, d - 1], recv_sem=recv_sems.at[k, d - 1],
                device_id=peer, device_id_type=pl.DeviceIdType.MESH).wait()
        for cp in local:
            cp.wait()

    outs = pl.pallas_call(
        body, name=name, in_specs=[hbm] * n, out_specs=[hbm] * n, out_shape=out_shape,
        scratch_shapes=[pltpu.SemaphoreType.DMA((n, N_DEV - 1)), pltpu.SemaphoreType.DMA((n, N_DEV - 1)),
                        pltpu.SemaphoreType.DMA((n,))],
        compiler_params=pltpu.CompilerParams(has_side_effects=True),
    )(*arrays)
    return list(outs)


BIG = ("w_in", "w_uq", "w_ukv", "w_o_mla", "w_o_gdn", "w_o", "w_gate_up", "w_down")
ROW_SHARDED = ("w_o", "w_down")
SMALL = ("b_ada", "norm_mix", "norm_ffn", "q_a_norm", "kv_a_norm", "A_log", "dt_bias", "gdn_norm",
         "final_norm")
WEIGHTS = ("w_ada", "b_ada", "norm_mix", "norm_ffn", "w_in", "q_a_norm", "kv_a_norm", "w_uq", "w_ukv",
           "w_o_mla", "conv_w", "A_log", "dt_bias", "gdn_norm", "w_o_gdn", "w_o", "w_gate_up", "w_down",
           "final_norm")


def _full(g, name, l):
    a = g[:, l]
    if name in ROW_SHARDED:
        return a.reshape(-1, a.shape[-1])
    return a.transpose(1, 0, 2).reshape(a.shape[1], -1)


def _pad_cols(a):
    return jnp.pad(a, ((0, 0), (0, _pad_lanes(a.shape[1]) - a.shape[1])))


def _rope(xv, cos, sin):
    x1, x2 = jnp.split(xv, 2, axis=-1)
    return jnp.concatenate([x1 * cos - x2 * sin, x2 * cos + x1 * sin], axis=-1)


def _layer_weights(big, conv_g, l):
    hq, hg = MLA_HEADS, GDN_HEADS
    w_in = _full(big["w_in"], "w_in", l)
    d_model = w_in.shape[0]
    o1 = Q_LORA + KV_LORA + QK_ROPE
    o2 = o1 + 2 * hg * GDN_DK + hg * GDN_DV
    o3 = o2 + hg * GDN_DV
    o4 = o3 + 2 * hg
    w_uq = _full(big["w_uq"], "w_uq", l).reshape(Q_LORA, hq, QK_NOPE + QK_ROPE)
    w_gu = _full(big["w_gate_up"], "w_gate_up", l)
    return dict(
        w_a=_pad_cols(w_in[:, :o1]), w_qkv=w_in[:, o1:o2], w_z=w_in[:, o2:o3], w_ba=_pad_cols(w_in[:, o3:o4]),
        w_g=w_in[:, o4:o4 + 2 * d_model],
        w_uq=jnp.concatenate([w_uq[:, :, :QK_NOPE].reshape(Q_LORA, hq * QK_NOPE),
                              w_uq[:, :, QK_NOPE:].reshape(Q_LORA, hq * QK_ROPE)], axis=1),
        w_ukv=_full(big["w_ukv"], "w_ukv", l), w_o_mla=_full(big["w_o_mla"], "w_o_mla", l),
        w_o_gdn=_full(big["w_o_gdn"], "w_o_gdn", l), w_o=_full(big["w_o"], "w_o", l),
        w_gu=w_gu, w_down=_full(big["w_down"], "w_down", l),
        conv_w=conv_g[:, l].transpose(1, 0, 2).reshape(CONV_WIDTH, -1),
    )


def _forward(x, mods, big, conv_g, small, cos, sin):
    t, d = x.shape
    hq, hg = MLA_HEADS, GDN_HEADS
    for l in range(DEPTH):
        w = _layer_weights(big, conv_g, l)
        sh_a, sc_a, gt_a, sh_f, sc_f, gt_f = [mods[l:l + 1, i * d:(i + 1) * d] for i in range(6)]
        tg = str(l)
        h = ada_norm(x, small["norm_mix"][l:l + 1], sc_a, sh_a, "mix" + tg)
        seg_a = mm(h, w["w_a"], "in_a" + tg, F32)
        qkv = mm(h, w["w_qkv"], "in_qkv" + tg, F32)
        z = mm(h, w["w_z"], "in_z" + tg, F32)
        ba = mm(h, w["w_ba"], "in_ba" + tg, F32)
        gl = mm(h, w["w_g"], "in_g" + tg, F32)
        c_q = seg_a[:, :Q_LORA]
        c_kv = seg_a[:, Q_LORA:Q_LORA + KV_LORA]
        k_pe = seg_a[:, Q_LORA + KV_LORA:Q_LORA + KV_LORA + QK_ROPE]
        qf = mm(rms_norm(c_q, small["q_a_norm"][l:l + 1], "qa" + tg), w["w_uq"], "uq" + tg, F32)
        kvf = mm(rms_norm(c_kv, small["kv_a_norm"][l:l + 1], "kva" + tg), w["w_ukv"], "ukv" + tg, F32)
        qn = qf[:, :hq * QK_NOPE]
        q_pe = qf[:, hq * QK_NOPE:].reshape(t, hq, QK_ROPE)
        qr = _rope(q_pe, cos[:, None, :], sin[:, None, :]).transpose(1, 0, 2)
        kr = _rope(k_pe, cos, sin)
        o_mla = attention(qn, qr, kvf, kr, tg)
        y_a = mm(o_mla, w["w_o_mla"], "o_mla" + tg, F32)
        qkv_c = conv_silu(qkv, w["conv_w"], tg)
        bl = ba[:, :hg].T[:, :, None]
        al = ba[:, hg:2 * hg].T[:, :, None]
        o_gdn = gdn(qkv_c, z, bl, al, small["A_log"][l].reshape(hg, 1, 1),
                    small["dt_bias"][l].reshape(hg, 1, 1), small["gdn_norm"][l:l + 1], tg)
        y_b = mm(o_gdn, w["w_o_gdn"], "o_gdn" + tg, F32)
        mix = mm(gate_mix(gl, y_a, y_b, tg), w["w_o"], "w_o" + tg, F32)
        x = resid(x, gt_a, mix, "mix" + tg)
        h2 = ada_norm(x, small["norm_ffn"][l:l + 1], sc_f, sh_f, "ffn" + tg)
        gu = mm(h2, w["w_gu"], "gu" + tg, F32)
        dn = mm(swiglu(gu, tg), w["w_down"], "down" + tg, F32)
        x = resid(x, gt_f, dn, "ffn" + tg)
    return x


def _device_step(x, tgt, mods, big, conv_g, small, cos, sin):
    fwd = lambda x_, mods_, big_, conv_, small_: _forward(x_, mods_, big_, conv_, small_, cos, sin)
    xf, vjp = jax.vjp(fwd, x, mods, big, conv_g, small)
    loss_t, dxf, dfn = loss_head(xf, small["final_norm"][None, :], tgt)
    dx, dmods, dbig, dconv, dsmall = vjp(dxf)
    dsmall = dict(dsmall)
    dsmall["final_norm"] = dfn[0]
    return loss_t[0, 0], dx, dmods, dbig, dconv, dsmall


def _flat_row(arrs):
    v = jnp.concatenate([a.reshape(-1) for a in arrs])
    return jnp.pad(v, (0, _pad_lanes(v.shape[0]) - v.shape[0]))[None, :]


def kernel(x, c, positions, w_ada, b_ada, norm_mix, norm_ffn, w_in, q_a_norm, kv_a_norm, w_uq, w_ukv, w_o_mla, conv_w, A_log, dt_bias, gdn_norm, w_o_gdn, w_o, w_gate_up, w_down, final_norm, loss_target, m_w_ada, m_b_ada, m_norm_mix, m_norm_ffn, m_w_in, m_q_a_norm, m_kv_a_norm, m_w_uq, m_w_ukv, m_w_o_mla, m_conv_w, m_A_log, m_dt_bias, m_gdn_norm, m_w_o_gdn, m_w_o, m_w_gate_up, m_w_down, m_final_norm, v_w_ada, v_b_ada, v_norm_mix, v_norm_ffn, v_w_in, v_q_a_norm, v_kv_a_norm, v_w_uq, v_w_ukv, v_w_o_mla, v_conv_w, v_A_log, v_dt_bias, v_gdn_norm, v_w_o_gdn, v_w_o, v_w_gate_up, v_w_down, v_final_norm):
    given = dict(locals())
    t, d = x.shape[1], x.shape[2]
    n_ada = w_ada.shape[2]

    got = exchange([c, conv_w] + [given[n].astype(BF16) for n in BIG], ["gather"] * (2 + len(BIG)),
                   "gather_weights")
    c_all, conv_g = got[0].reshape(N_DEV, d), got[1]
    big = dict(zip(BIG, got[2:]))

    c_rows = jnp.pad(c_all, ((0, 16 - N_DEV), (0, 0)))
    mod_cols = jnp.stack([_mm(c_rows, w_ada[l], "nn", F32, "ada_mod%d" % l, a_act="silu")[:N_DEV]
                          for l in range(DEPTH)], axis=1)
    mod_mine = exchange([mod_cols], ["scatter"], "scatter_mod")[0]
    mods = mod_mine.transpose(1, 0, 2).reshape(DEPTH, N_DEV * n_ada) + b_ada

    inv_freq = 1.0 / (ROPE_THETA ** (jnp.arange(0, QK_ROPE, 2, dtype=F32) / QK_ROPE))
    ang = positions[0].astype(F32)[:, None] * inv_freq
    small = {n: given[n] for n in SMALL}
    loss_mine, dx, dmods, dbig, dconv, dsmall = _device_step(
        x[0], loss_target[0], mods, big, conv_g, small, jnp.cos(ang), jnp.sin(ang))
    dsmall["b_ada"] = dmods
    loss = lax.psum(loss_mine, AXES)

    dmod_cols = dmods.reshape(DEPTH, N_DEV, n_ada).transpose(1, 0, 2)
    got = exchange([dbig[n] for n in BIG] + [dconv, dmod_cols, _flat_row([dsmall[n] for n in SMALL])],
                   ["scatter"] * (len(BIG) + 2) + ["gather"], "exchange_grads")
    parts = dict(zip(BIG, got[:len(BIG)]))
    conv_parts, dmod_all, small_parts = got[len(BIG):]

    res = {}
    for n in BIG:
        w = given[n]
        r2 = (w.shape[0] * w.shape[1], w.shape[2])
        outs = adamw(w.reshape(r2), parts[n].reshape((N_DEV,) + r2), given["m_" + n].reshape(r2),
                     given["v_" + n].reshape(r2), "adamw_" + n)
        res[n] = [o.reshape(w.shape) for o in outs]
    dm_rows = jnp.pad(dmod_all, ((0, 16 - N_DEV), (0, 0), (0, 0)))
    g_ada = jnp.stack([_mm(c_rows, dm_rows[:, l], "tn", F32, "ada_dw%d" % l, a_act="silu")
                       for l in range(DEPTH)])
    r2 = (DEPTH * d, n_ada)
    outs = adamw(w_ada.reshape(r2), g_ada.reshape((1,) + r2), m_w_ada.reshape(r2), v_w_ada.reshape(r2),
                 "adamw_w_ada")
    res["w_ada"] = [o.reshape(w_ada.shape) for o in outs]
    packed = SMALL + ("conv_w",)
    p_all = jnp.concatenate([small_parts, conv_parts.reshape(N_DEV, 1, -1)], axis=2)
    pack = lambda pre: jnp.concatenate([_flat_row([given[pre + n] for n in SMALL]),
                                        given[pre + "conv_w"].reshape(1, -1)], axis=1)
    outs = adamw(pack(""), p_all, pack("m_"), pack("v_"), "adamw_small")
    off = 0
    for n in packed:
        if n == "conv_w":
            off = small_parts.shape[2]
        size = math.prod(given[n].shape)
        res[n] = [o[0, off:off + size].reshape(given[n].shape) for o in outs]
        off += size

    return (loss, dx[None]) + tuple(res[n][i] for i in range(4) for n in WEIGHTS)
```

```python
import functools
import math

import jax
import jax.numpy as jnp
from jax import lax
from jax.experimental import pallas as pl
from jax.experimental.pallas import tpu as pltpu

F32 = jnp.float32
BF16 = jnp.bfloat16

MLA_HEADS = 8
QK_NOPE = 128
QK_ROPE = 64
V_HEAD = 128
Q_LORA = 512
KV_LORA = 512
ROPE_THETA = 10000.0
GDN_HEADS = 8
GDN_DK = 128
GDN_DV = 128
CONV_WIDTH = 4
CHUNK = 64
DEPTH = 2
EPS = 1e-6
ADAM_LR = 0.001
ADAM_B1 = 0.9
ADAM_B2 = 0.999
ADAM_EPS = 1e-08
ADAM_WD = 0.01
ADAM_STEP = 10

N_DEV = 8
AXES = ("x", "y", "c")
LANE = 128
VMEM_LIMIT = 48 * 1024 * 1024
HIGHEST = lax.Precision.HIGHEST

NN = (((1,), (0,)), ((), ()))
NT = (((1,), (1,)), ((), ()))
TN = (((0,), (0,)), ((), ()))


def _cp(sem=None):
    return pltpu.CompilerParams(dimension_semantics=sem, vmem_limit_bytes=VMEM_LIMIT)


def _tile(n, cap):
    if n <= cap:
        return n
    for t in range(cap - cap % LANE, 0, -LANE):
        if n % t == 0:
            return t
    return n


def _rows(t, cap=256):
    return cap if t % cap == 0 else t


def _pad_lanes(n):
    return -(-n // LANE) * LANE


def _sigmoid(x):
    return 1.0 / (1.0 + jnp.exp(-x))


def _softplus(x):
    return jnp.maximum(x, 0.0) + jnp.log(1.0 + jnp.exp(-jnp.abs(x)))


def _tile_slot(n, cap):
    t = _tile(n, cap)
    return n if t < 256 < n <= 1536 else t


def _mm(a, b, dims, out_dtype, name, a_act=None, slots=False):
    if dims == "nn":
        m, k = a.shape
        n = b.shape[-1] * (N_DEV if slots else 1)
    elif dims == "nt":
        m, k = a.shape
        n = b.shape[-2]
    else:
        k, m = a.shape
        n = b.shape[-1]
    tm = _tile(m, 1024)
    tn = _tile_slot(n // N_DEV, 512) if slots and dims != "nt" else _tile(n, 512)
    tk = _tile_slot(k // N_DEV, 1024) if slots and dims == "nt" else _tile(k, 1024)
    nk = k // tk
    per_n = (n // N_DEV) // tn if slots else 1
    per_k = (k // N_DEV) // tk if slots else 1
    if dims == "tn":
        a_spec = pl.BlockSpec((tk, tm), lambda i, j, kk: (kk, i))
    else:
        a_spec = pl.BlockSpec((tm, tk), lambda i, j, kk: (i, kk))
    if dims == "nt":
        if slots:
            b_spec = pl.BlockSpec((None, tn, tk), lambda i, j, kk: (kk // per_k, j, kk % per_k))
        else:
            b_spec = pl.BlockSpec((tn, tk), lambda i, j, kk: (j, kk))
    elif dims == "nn" and slots:
        b_spec = pl.BlockSpec((None, tk, tn), lambda i, j, kk: (j // per_n, kk, j % per_n))
    else:
        b_spec = pl.BlockSpec((tk, tn), lambda i, j, kk: (kk, j))
    if dims == "tn" and slots:
        out_spec = pl.BlockSpec((None, tm, tn), lambda i, j, kk: (j // per_n, i, j % per_n))
        out_shape = jax.ShapeDtypeStruct((N_DEV, m, n // N_DEV), out_dtype)
    else:
        out_spec = pl.BlockSpec((tm, tn), lambda i, j, kk: (i, j))
        out_shape = jax.ShapeDtypeStruct((m, n), out_dtype)
    dn = {"nn": NN, "nt": NT, "tn": TN}[dims]

    def body(a_ref, b_ref, o_ref, acc_ref):
        kk = pl.program_id(2)

        @pl.when(kk == 0)
        def _():
            acc_ref[...] = jnp.zeros_like(acc_ref)

        av = a_ref[...]
        if a_act == "silu":
            av = av * _sigmoid(av)
        acc_ref[...] += lax.dot_general(av.astype(BF16), b_ref[...].astype(BF16), dn,
                                        preferred_element_type=F32)

        @pl.when(kk == nk - 1)
        def _():
            o_ref[...] = acc_ref[...].astype(o_ref.dtype)

    return pl.pallas_call(
        body, name=name, grid=(m // tm, n // tn, nk),
        in_specs=[a_spec, b_spec], out_specs=out_spec, out_shape=out_shape,
        scratch_shapes=[pltpu.VMEM((tm, tn), F32)],
        compiler_params=_cp(("parallel", "parallel", "arbitrary")),
    )(a, b)


@functools.partial(jax.custom_vjp, nondiff_argnums=(2, 3))
def mm(a, b, tag, out_dtype):
    return _mm(a, b, "nn", out_dtype, "mm_" + tag, slots=b.ndim == 3)


def _mm_f(a, b, tag, out_dtype):
    return mm(a, b, tag, out_dtype), (a, b)


def _mm_b(tag, out_dtype, res, g):
    a, b = res
    slots = b.ndim == 3
    da = _mm(g, b, "nt", a.dtype, "mm_" + tag + "_da", slots=slots)
    db = _mm(a, g, "tn", b.dtype, "mm_" + tag + "_db", slots=slots)
    return da, db


mm.defvjp(_mm_f, _mm_b)


def _norm_fwd_call(x, nw, sc, sh, name):
    t, d = x.shape
    tr = _rows(t)
    mod = sc is not None
    row = pl.BlockSpec((tr, d), lambda i: (i, 0))
    vec = pl.BlockSpec((1, d), lambda i: (0, 0))

    def body(*refs):
        if mod:
            x_ref, nw_ref, sc_ref, sh_ref, o_ref = refs
        else:
            x_ref, nw_ref, o_ref = refs
        xv = x_ref[...]
        r = lax.rsqrt(jnp.mean(xv * xv, axis=-1, keepdims=True) + EPS)
        y = (xv * r) * nw_ref[...]
        if mod:
            y = y * (1.0 + sc_ref[...]) + sh_ref[...]
        o_ref[...] = y.astype(o_ref.dtype)

    args = (x, nw, sc, sh) if mod else (x, nw)
    return pl.pallas_call(
        body, name=name, grid=(t // tr,),
        in_specs=[row] + [vec] * (len(args) - 1), out_specs=row,
        out_shape=jax.ShapeDtypeStruct((t, d), BF16),
        compiler_params=_cp(("parallel",)),
    )(*args)


def _norm_bwd_call(x, nw, sc, dh, name):
    t, d = x.shape
    tr = _rows(t)
    mod = sc is not None
    row = pl.BlockSpec((tr, d), lambda i: (i, 0))
    vec = pl.BlockSpec((1, d), lambda i: (0, 0))

    def body(*refs):
        if mod:
            x_ref, nw_ref, sc_ref, dh_ref, dx_ref, dnw_ref, dsc_ref, dsh_ref = refs
        else:
            x_ref, nw_ref, dh_ref, dx_ref, dnw_ref = refs
        i = pl.program_id(0)
        xv = x_ref[...]
        dh = dh_ref[...].astype(F32)
        r = lax.rsqrt(jnp.mean(xv * xv, axis=-1, keepdims=True) + EPS)
        y = xv * r
        a = nw_ref[...] * (1.0 + sc_ref[...]) if mod else nw_ref[...]
        dy = dh * a
        dx_ref[...] = r * (dy - y * jnp.mean(dy * y, axis=-1, keepdims=True))
        da = jnp.sum(dh * y, axis=0, keepdims=True)

        @pl.when(i == 0)
        def _():
            dnw_ref[...] = jnp.zeros_like(dnw_ref)
            if mod:
                dsc_ref[...] = jnp.zeros_like(dsc_ref)
                dsh_ref[...] = jnp.zeros_like(dsh_ref)

        if mod:
            dnw_ref[...] += da * (1.0 + sc_ref[...])
            dsc_ref[...] += da * nw_ref[...]
            dsh_ref[...] += jnp.sum(dh, axis=0, keepdims=True)
        else:
            dnw_ref[...] += da

    args = (x, nw, sc, dh) if mod else (x, nw, dh)
    n_vec = 3 if mod else 1
    return pl.pallas_call(
        body, name=name, grid=(t // tr,),
        in_specs=[row] + [vec] * (len(args) - 2) + [row],
        out_specs=[row] + [vec] * n_vec,
        out_shape=[jax.ShapeDtypeStruct((t, d), F32)] + [jax.ShapeDtypeStruct((1, d), F32)] * n_vec,
        compiler_params=_cp(("arbitrary",)),
    )(*args)


@functools.partial(jax.custom_vjp, nondiff_argnums=(4,))
def ada_norm(x, nw, sc, sh, tag):
    return _norm_fwd_call(x, nw, sc, sh, "adanorm_" + tag)


def _ada_norm_f(x, nw, sc, sh, tag):
    return _norm_fwd_call(x, nw, sc, sh, "adanorm_" + tag), (x, nw, sc)


def _ada_norm_b(tag, res, dh):
    x, nw, sc = res
    dx, dnw, dsc, dsh = _norm_bwd_call(x, nw, sc, dh, "adanorm_" + tag + "_bwd")
    return dx, dnw, dsc, dsh


ada_norm.defvjp(_ada_norm_f, _ada_norm_b)


@functools.partial(jax.custom_vjp, nondiff_argnums=(2,))
def rms_norm(x, nw, tag):
    return _norm_fwd_call(x, nw, None, None, "rms_" + tag)


def _rms_norm_f(x, nw, tag):
    return _norm_fwd_call(x, nw, None, None, "rms_" + tag), (x, nw)


def _rms_norm_b(tag, res, dh):
    x, nw = res
    dx, dnw = _norm_bwd_call(x, nw, None, dh, "rms_" + tag + "_bwd")
    return dx, dnw


rms_norm.defvjp(_rms_norm_f, _rms_norm_b)


def _gate_mix_fwd_call(gl, ya, yb, name):
    t, d = ya.shape
    tr = _rows(t)
    row = pl.BlockSpec((tr, d), lambda i: (i, 0))

    def body(ga_ref, gb_ref, ya_ref, yb_ref, o_ref):
        o_ref[...] = (_sigmoid(ga_ref[...]) * ya_ref[...]
                      + _sigmoid(gb_ref[...]) * yb_ref[...]).astype(o_ref.dtype)

    return pl.pallas_call(
        body, name=name, grid=(t // tr,),
        in_specs=[row, pl.BlockSpec((tr, d), lambda i: (i, 1)), row, row], out_specs=row,
        out_shape=jax.ShapeDtypeStruct((t, d), BF16),
        compiler_params=_cp(("parallel",)),
    )(gl, gl, ya, yb)


def _gate_mix_bwd_call(gl, ya, yb, dm, name):
    t, d = ya.shape
    tr = _rows(t)
    row = pl.BlockSpec((tr, d), lambda i: (i, 0))
    wide = pl.BlockSpec((tr, 2 * d), lambda i: (i, 0))

    def body(gl_ref, ya_ref, yb_ref, dm_ref, dgl_ref, dya_ref, dyb_ref):
        dm = dm_ref[...].astype(F32)
        ga = _sigmoid(gl_ref[:, :d])
        gb = _sigmoid(gl_ref[:, d:])
        dya_ref[...] = dm * ga
        dyb_ref[...] = dm * gb
        dgl_ref[:, :d] = dm * ya_ref[...] * ga * (1.0 - ga)
        dgl_ref[:, d:] = dm * yb_ref[...] * gb * (1.0 - gb)

    return pl.pallas_call(
        body, name=name, grid=(t // tr,),
        in_specs=[wide, row, row, row], out_specs=[wide, row, row],
        out_shape=[jax.ShapeDtypeStruct((t, 2 * d), F32), jax.ShapeDtypeStruct((t, d), F32),
                   jax.ShapeDtypeStruct((t, d), F32)],
        compiler_params=_cp(("parallel",)),
    )(gl, ya, yb, dm)


@functools.partial(jax.custom_vjp, nondiff_argnums=(3,))
def gate_mix(gl, ya, yb, tag):
    return _gate_mix_fwd_call(gl, ya, yb, "gatemix_" + tag)


def _gate_mix_f(gl, ya, yb, tag):
    return _gate_mix_fwd_call(gl, ya, yb, "gatemix_" + tag), (gl, ya, yb)


def _gate_mix_b(tag, res, dm):
    return tuple(_gate_mix_bwd_call(*res, dm, "gatemix_" + tag + "_bwd"))


gate_mix.defvjp(_gate_mix_f, _gate_mix_b)


def _resid_fwd_call(x, gt, m, name):
    t, d = x.shape
    tr = _rows(t)
    row = pl.BlockSpec((tr, d), lambda i: (i, 0))
    vec = pl.BlockSpec((1, d), lambda i: (0, 0))

    def body(x_ref, gt_ref, m_ref, o_ref):
        o_ref[...] = x_ref[...] + gt_ref[...] * m_ref[...]

    return pl.pallas_call(
        body, name=name, grid=(t // tr,), in_specs=[row, vec, row], out_specs=row,
        out_shape=jax.ShapeDtypeStruct((t, d), F32), compiler_params=_cp(("parallel",)),
    )(x, gt, m)


def _resid_bwd_call(gt, m, g, name):
    t, d = m.shape
    tr = _rows(t)
    row = pl.BlockSpec((tr, d), lambda i: (i, 0))
    vec = pl.BlockSpec((1, d), lambda i: (0, 0))

    def body(gt_ref, m_ref, g_ref, dm_ref, dgt_ref):
        i = pl.program_id(0)
        g = g_ref[...]
        dm_ref[...] = g * gt_ref[...]

        @pl.when(i == 0)
        def _():
            dgt_ref[...] = jnp.zeros_like(dgt_ref)

        dgt_ref[...] += jnp.sum(g * m_ref[...], axis=0, keepdims=True)

    return pl.pallas_call(
        body, name=name, grid=(t // tr,), in_specs=[vec, row, row], out_specs=[row, vec],
        out_shape=[jax.ShapeDtypeStruct((t, d), F32), jax.ShapeDtypeStruct((1, d), F32)],
        compiler_params=_cp(("arbitrary",)),
    )(gt, m, g)


@functools.partial(jax.custom_vjp, nondiff_argnums=(3,))
def resid(x, gt, m, tag):
    return _resid_fwd_call(x, gt, m, "resid_" + tag)


def _resid_f(x, gt, m, tag):
    return _resid_fwd_call(x, gt, m, "resid_" + tag), (gt, m)


def _resid_b(tag, res, g):
    gt, m = res
    dm, dgt = _resid_bwd_call(gt, m, g, "resid_" + tag + "_bwd")
    return g, dgt, dm


resid.defvjp(_resid_f, _resid_b)


def _swiglu_fwd_call(gu, name):
    t, f2 = gu.shape
    f = f2 // 2
    tr = _rows(t, 128)
    half = pl.BlockSpec((tr, f), lambda i: (i, 0))

    def body(g_ref, u_ref, o_ref):
        g = g_ref[...]
        o_ref[...] = (g * _sigmoid(g) * u_ref[...]).astype(o_ref.dtype)

    return pl.pallas_call(
        body, name=name, grid=(t // tr,),
        in_specs=[half, pl.BlockSpec((tr, f), lambda i: (i, 1))], out_specs=half,
        out_shape=jax.ShapeDtypeStruct((t, f), BF16), compiler_params=_cp(("parallel",)),
    )(gu, gu)


def _swiglu_bwd_call(gu, da, name):
    t, f2 = gu.shape
    f = f2 // 2
    tr = _rows(t, 128)
    wide = pl.BlockSpec((tr, f2), lambda i: (i, 0))

    def body(gu_ref, da_ref, dgu_ref):
        g = gu_ref[:, :f]
        u = gu_ref[:, f:]
        da = da_ref[...].astype(F32)
        s = _sigmoid(g)
        dgu_ref[:, :f] = da * u * s * (1.0 + g * (1.0 - s))
        dgu_ref[:, f:] = da * g * s

    return pl.pallas_call(
        body, name=name, grid=(t // tr,),
        in_specs=[wide, pl.BlockSpec((tr, f), lambda i: (i, 0))], out_specs=wide,
        out_shape=jax.ShapeDtypeStruct((t, f2), F32), compiler_params=_cp(("parallel",)),
    )(gu, da)


@functools.partial(jax.custom_vjp, nondiff_argnums=(1,))
def swiglu(gu, tag):
    return _swiglu_fwd_call(gu, "swiglu_" + tag)


def _swiglu_f(gu, tag):
    return _swiglu_fwd_call(gu, "swiglu_" + tag), (gu,)


def _swiglu_b(tag, res, da):
    return (_swiglu_bwd_call(res[0], da, "swiglu_" + tag + "_bwd"),)


swiglu.defvjp(_swiglu_f, _swiglu_b)


def loss_head(x, fw, tgt):
    t, d = x.shape
    tr = _rows(t)
    row = pl.BlockSpec((tr, d), lambda i: (i, 0))
    vec = pl.BlockSpec((1, d), lambda i: (0, 0))
    tile = pl.BlockSpec((8, LANE), lambda i: (0, 0))

    def body(x_ref, fw_ref, tgt_ref, loss_ref, dx_ref, dfw_ref):
        i = pl.program_id(0)
        xv = x_ref[...]
        fw = fw_ref[...]
        r = lax.rsqrt(jnp.mean(xv * xv, axis=-1, keepdims=True) + EPS)
        yh = xv * r
        e = yh * fw - tgt_ref[...]
        dy = e * (1.0 / d)
        dyw = dy * fw
        dx_ref[...] = r * (dyw - yh * jnp.mean(dyw * yh, axis=-1, keepdims=True))

        @pl.when(i == 0)
        def _():
            loss_ref[...] = jnp.zeros_like(loss_ref)
            dfw_ref[...] = jnp.zeros_like(dfw_ref)

        loss_ref[...] += 0.5 * jnp.sum(jnp.mean(e * e, axis=-1, keepdims=True))
        dfw_ref[...] += jnp.sum(dy * yh, axis=0, keepdims=True)

    return pl.pallas_call(
        body, name="loss_head", grid=(t // tr,), in_specs=[row, vec, row],
        out_specs=[tile, row, vec],
        out_shape=[jax.ShapeDtypeStruct((8, LANE), F32), jax.ShapeDtypeStruct((t, d), F32),
                   jax.ShapeDtypeStruct((1, d), F32)],
        compiler_params=_cp(("arbitrary",)),
    )(x, fw, tgt)


def _attn_fwd_call(qn, qr, kv, kr, name):
    t = qn.shape[0]
    h_n = MLA_HEADS
    tq = _rows(t)
    nq = t // tq
    scale = (QK_NOPE + QK_ROPE) ** -0.5

    def body(qn_ref, qr_ref, kn_ref, v_ref, kr_ref, o_ref, lse_ref, m_scr, l_scr, acc_scr):
        i, j = pl.program_id(1), pl.program_id(2)

        @pl.when(j == 0)
        def _():
            m_scr[...] = jnp.full_like(m_scr, -1e30)
            l_scr[...] = jnp.zeros_like(l_scr)
            acc_scr[...] = jnp.zeros_like(acc_scr)

        @pl.when(j <= i)
        def _():
            s = lax.dot_general(qn_ref[...].astype(BF16), kn_ref[...].astype(BF16), NT,
                                preferred_element_type=F32)
            s += lax.dot_general(qr_ref[...].astype(BF16), kr_ref[...].astype(BF16), NT,
                                 preferred_element_type=F32)
            s = s * scale
            rows = i * tq + lax.broadcasted_iota(jnp.int32, (tq, tq), 0)
            cols = j * tq + lax.broadcasted_iota(jnp.int32, (tq, tq), 1)
            s = jnp.where(cols <= rows, s, -1e30)
            m_old = m_scr[...]
            m_new = jnp.maximum(m_old, jnp.max(s, axis=-1, keepdims=True))
            p = jnp.exp(s - m_new)
            alpha = jnp.exp(m_old - m_new)
            l_scr[...] = alpha * l_scr[...] + jnp.sum(p, axis=-1, keepdims=True)
            acc_scr[...] = alpha * acc_scr[...] + jnp.dot(p.astype(BF16), v_ref[...].astype(BF16),
                                                           preferred_element_type=F32)
            m_scr[...] = m_new

        @pl.when(j == i)
        def _():
            o_ref[...] = (acc_scr[...] / l_scr[...]).astype(o_ref.dtype)
            lse_ref[...] = m_scr[...] + jnp.log(l_scr[...])

    return pl.pallas_call(
        body, name=name, grid=(h_n, nq, nq),
        in_specs=[
            pl.BlockSpec((tq, QK_NOPE), lambda h, i, j: (i, h)),
            pl.BlockSpec((None, tq, QK_ROPE), lambda h, i, j: (h, i, 0)),
            pl.BlockSpec((tq, QK_NOPE), lambda h, i, j: (jnp.minimum(j, i), 2 * h)),
            pl.BlockSpec((tq, V_HEAD), lambda h, i, j: (jnp.minimum(j, i), 2 * h + 1)),
            pl.BlockSpec((tq, QK_ROPE), lambda h, i, j: (jnp.minimum(j, i), 0)),
        ],
        out_specs=[
            pl.BlockSpec((tq, V_HEAD), lambda h, i, j: (i, h)),
            pl.BlockSpec((None, tq, 1), lambda h, i, j: (h, i, 0)),
        ],
        out_shape=[jax.ShapeDtypeStruct((t, h_n * V_HEAD), BF16),
                   jax.ShapeDtypeStruct((h_n, t, 1), F32)],
        scratch_shapes=[pltpu.VMEM((tq, 1), F32), pltpu.VMEM((tq, 1), F32),
                        pltpu.VMEM((tq, V_HEAD), F32)],
        compiler_params=_cp(("parallel", "parallel", "arbitrary")),
    )(qn, qr, kv, kv, kr)


def _attn_bwd_call(qn, qr, kv, kr, o, lse, do, name):
    t = qn.shape[0]
    h_n = MLA_HEADS
    tq = _rows(t)
    nq = t // tq
    scale = (QK_NOPE + QK_ROPE) ** -0.5

    def body(qn_ref, qr_ref, kn_ref, v_ref, kr_ref, o_ref, lse_ref, do_ref,
             dqn_ref, dqr_ref, dkv_ref, dkr_ref, dkn_scr, dv_scr, dkr_scr):
        j, i = pl.program_id(1), pl.program_id(2)

        @pl.when(jnp.logical_and(j == 0, i == 0))
        def _():
            dqn_ref[...] = jnp.zeros_like(dqn_ref)
            dqr_ref[...] = jnp.zeros_like(dqr_ref)

        @pl.when(i == 0)
        def _():
            dkn_scr[...] = jnp.zeros_like(dkn_scr)
            dv_scr[...] = jnp.zeros_like(dv_scr)
            dkr_scr[...] = jnp.zeros_like(dkr_scr)

        @pl.when(i >= j)
        def _():
            qn_b = qn_ref[...].astype(BF16)
            qr_b = qr_ref[...].astype(BF16)
            kn_b = kn_ref[...].astype(BF16)
            kr_b = kr_ref[...].astype(BF16)
            do_b = do_ref[...]
            s = lax.dot_general(qn_b, kn_b, NT, preferred_element_type=F32)
            s += lax.dot_general(qr_b, kr_b, NT, preferred_element_type=F32)
            s = s * scale
            rows = i * tq + lax.broadcasted_iota(jnp.int32, (tq, tq), 0)
            cols = j * tq + lax.broadcasted_iota(jnp.int32, (tq, tq), 1)
            p = jnp.where(cols <= rows, jnp.exp(s - lse_ref[...]), 0.0)
            delta = jnp.sum(do_b.astype(F32) * o_ref[...].astype(F32), axis=-1, keepdims=True)
            dp = lax.dot_general(do_b, v_ref[...].astype(BF16), NT, preferred_element_type=F32)
            ds = (p * (dp - delta) * scale).astype(BF16)
            p_b = p.astype(BF16)
            dv_scr[...] += lax.dot_general(p_b, do_b, TN, preferred_element_type=F32)
            dkn_scr[...] += lax.dot_general(ds, qn_b, TN, preferred_element_type=F32)
            dkr_scr[...] += lax.dot_general(ds, qr_b, TN, preferred_element_type=F32)
            sl = pl.ds(pl.multiple_of(i * tq, tq), tq)
            dqn_ref[sl, :] += jnp.dot(ds, kn_b, preferred_element_type=F32)
            dqr_ref[sl, :] += jnp.dot(ds, kr_b, preferred_element_type=F32)

        @pl.when(i == nq - 1)
        def _():
            dkv_ref[:, :QK_NOPE] = dkn_scr[...]
            dkv_ref[:, QK_NOPE:] = dv_scr[...]
            dkr_ref[...] = dkr_scr[...]

    qi = lambda j, i: jnp.maximum(i, j)
    return pl.pallas_call(
        body, name=name, grid=(h_n, nq, nq),
        in_specs=[
            pl.BlockSpec((tq, QK_NOPE), lambda h, j, i: (qi(j, i), h)),
            pl.BlockSpec((None, tq, QK_ROPE), lambda h, j, i: (h, qi(j, i), 0)),
            pl.BlockSpec((tq, QK_NOPE), lambda h, j, i: (j, 2 * h)),
            pl.BlockSpec((tq, V_HEAD), lambda h, j, i: (j, 2 * h + 1)),
            pl.BlockSpec((tq, QK_ROPE), lambda h, j, i: (j, 0)),
            pl.BlockSpec((tq, V_HEAD), lambda h, j, i: (qi(j, i), h)),
            pl.BlockSpec((None, tq, 1), lambda h, j, i: (h, qi(j, i), 0)),
            pl.BlockSpec((tq, V_HEAD), lambda h, j, i: (qi(j, i), h)),
        ],
        out_specs=[
            pl.BlockSpec((t, QK_NOPE), lambda h, j, i: (0, h)),
            pl.BlockSpec((None, t, QK_ROPE), lambda h, j, i: (h, 0, 0)),
            pl.BlockSpec((tq, QK_NOPE + V_HEAD), lambda h, j, i: (j, h)),
            pl.BlockSpec((None, tq, QK_ROPE), lambda h, j, i: (h, j, 0)),
        ],
        out_shape=[jax.ShapeDtypeStruct((t, h_n * QK_NOPE), F32),
                   jax.ShapeDtypeStruct((h_n, t, QK_ROPE), F32),
                   jax.ShapeDtypeStruct((t, h_n * (QK_NOPE + V_HEAD)), F32),
                   jax.ShapeDtypeStruct((h_n, t, QK_ROPE), F32)],
        scratch_shapes=[pltpu.VMEM((tq, QK_NOPE), F32), pltpu.VMEM((tq, V_HEAD), F32),
                        pltpu.VMEM((tq, QK_ROPE), F32)],
        compiler_params=_cp(("parallel", "arbitrary", "arbitrary")),
    )(qn, qr, kv, kv, kr, o, lse, do)


@functools.partial(jax.custom_vjp, nondiff_argnums=(4,))
def attention(qn, qr, kv, kr, tag):
    return _attn_fwd_call(qn, qr, kv, kr, "attn_" + tag)[0]


def _attention_f(qn, qr, kv, kr, tag):
    o, lse = _attn_fwd_call(qn, qr, kv, kr, "attn_" + tag)
    return o, (qn, qr, kv, kr, o, lse)


def _attention_b(tag, res, do):
    dqn, dqr, dkv, dkr_h = _attn_bwd_call(*res, do, "attn_" + tag + "_bwd")
    return dqn, dqr, dkv, jnp.sum(dkr_h, axis=0)


attention.defvjp(_attention_f, _attention_b)


def _shift_down(u, s):
    if s == 0:
        return u
    t = u.shape[0]
    rolled = pltpu.roll(u, s, 0)
    return jnp.where(lax.broadcasted_iota(jnp.int32, u.shape, 0) >= s, rolled, 0.0)


def _shift_up(u, s):
    if s == 0:
        return u
    t = u.shape[0]
    rolled = pltpu.roll(u, t - s, 0)
    return jnp.where(lax.broadcasted_iota(jnp.int32, u.shape, 0) < t - s, rolled, 0.0)


def _conv_blocks(t, c3):
    p = c3 // 3
    tc = _tile(p, 512)
    per = p // tc
    return p, tc, per


def _conv_fwd_call(u, w, name):
    t, c3 = u.shape
    p, tc, per = _conv_blocks(t, c3)

    def body(u_ref, w_ref, o_ref):
        u = u_ref[...]
        y = jnp.zeros_like(u)
        for j in range(CONV_WIDTH):
            y = y + w_ref[j:j + 1, :] * _shift_down(u, CONV_WIDTH - 1 - j)
        o_ref[...] = y * _sigmoid(y)

    return pl.pallas_call(
        body, name=name, grid=(c3 // tc,),
        in_specs=[pl.BlockSpec((t, tc), lambda cb: (0, cb)),
                  pl.BlockSpec((CONV_WIDTH, tc), lambda cb: (0, cb))],
        out_specs=pl.BlockSpec((None, t, tc), lambda cb: (cb // per, 0, cb % per)),
        out_shape=jax.ShapeDtypeStruct((3, t, p), F32),
        compiler_params=_cp(("parallel",)),
    )(u, w)


def _conv_bwd_call(u, w, do, name):
    t, c3 = u.shape
    p, tc, per = _conv_blocks(t, c3)

    def body(u_ref, w_ref, do_ref, du_ref, dw_ref):
        u = u_ref[...]
        shifted = [_shift_down(u, CONV_WIDTH - 1 - j) for j in range(CONV_WIDTH)]
        y = jnp.zeros_like(u)
        for j in range(CONV_WIDTH):
            y = y + w_ref[j:j + 1, :] * shifted[j]
        s = _sigmoid(y)
        dy = do_ref[...] * s * (1.0 + y * (1.0 - s))
        du = jnp.zeros_like(u)
        for j in range(CONV_WIDTH):
            du = du + w_ref[j:j + 1, :] * _shift_up(dy, CONV_WIDTH - 1 - j)
            dw_ref[j:j + 1, :] = jnp.sum(dy * shifted[j], axis=0, keepdims=True)
        du_ref[...] = du

    return pl.pallas_call(
        body, name=name, grid=(c3 // tc,),
        in_specs=[pl.BlockSpec((t, tc), lambda cb: (0, cb)),
                  pl.BlockSpec((CONV_WIDTH, tc), lambda cb: (0, cb)),
                  pl.BlockSpec((None, t, tc), lambda cb: (cb // per, 0, cb % per))],
        out_specs=[pl.BlockSpec((t, tc), lambda cb: (0, cb)),
                   pl.BlockSpec((CONV_WIDTH, tc), lambda cb: (0, cb))],
        out_shape=[jax.ShapeDtypeStruct((t, c3), F32), jax.ShapeDtypeStruct((CONV_WIDTH, c3), F32)],
        compiler_params=_cp(("parallel",)),
    )(u, w, do)


@functools.partial(jax.custom_vjp, nondiff_argnums=(2,))
def conv_silu(u, w, tag):
    return _conv_fwd_call(u, w, "conv_" + tag)


def _conv_silu_f(u, w, tag):
    return _conv_fwd_call(u, w, "conv_" + tag), (u, w)


def _conv_silu_b(tag, res, do):
    return tuple(_conv_bwd_call(*res, do, "conv_" + tag + "_bwd"))


conv_silu.defvjp(_conv_silu_f, _conv_silu_b)


def _hdot(a, b, dn=NN):
    return lax.dot_general(a, b, dn, precision=HIGHEST, preferred_element_type=F32)


def _gdn_chunk(q, k, v, z, bl, al, a_log, dtb, gn, s):
    c = q.shape[0]
    ri = lax.broadcasted_iota(jnp.int32, (c, c), 0)
    ci = lax.broadcasted_iota(jnp.int32, (c, c), 1)
    lower = ri >= ci
    strict = ri > ci
    low_incl = lower.astype(F32)
    up_incl = (ri <= ci).astype(F32)
    eye = (ri == ci).astype(F32)

    q = q * lax.rsqrt(jnp.sum(q * q, axis=-1, keepdims=True) + EPS) * (GDN_DK ** -0.5)
    k = k * lax.rsqrt(jnp.sum(k * k, axis=-1, keepdims=True) + EPS)
    beta = _sigmoid(bl)
    g = -jnp.exp(a_log) * _softplus(al + dtb)
    g_w = jnp.broadcast_to(g, (c, LANE))
    gc = _hdot(low_incl, g_w)
    gr = _hdot(g_w[:, :c], up_incl, TN)
    diff = gc[:, :c] - gr
    decay = jnp.where(lower, jnp.exp(jnp.where(lower, diff, 0.0)), 0.0)
    kb = k * beta
    lmat = jnp.where(strict, _hdot(kb, k, NT) * decay, 0.0)
    inv = eye - lmat
    pw = lmat
    for _ in range(int(math.log2(c)) - 1):
        pw = _hdot(pw, pw)
        inv = _hdot(inv, eye + pw)
    eg = jnp.exp(gc)
    u = _hdot(inv, v * beta)
    w = _hdot(inv, kb * eg)
    attn = jnp.where(lower, _hdot(q, k, NT) * decay, 0.0)
    v_new = u - _hdot(w, s)
    o = _hdot(q * eg, s) + _hdot(attn, v_new)
    g_last = jnp.sum(g_w, axis=0, keepdims=True)
    k_dec = k * jnp.exp(g_last - gc)
    s_new = s * jnp.exp(g_last) + _hdot(k_dec, v_new, TN)
    on = o * lax.rsqrt(jnp.mean(o * o, axis=-1, keepdims=True) + EPS) * gn
    return on * (z * _sigmoid(z)), s_new


def _gdn_specs(n_chunks, rev):
    c = CHUNK
    nn = (lambda n: n_chunks - 1 - n) if rev else (lambda n: n)
    plane = lambda pidx: pl.BlockSpec((None, c, GDN_DK), lambda h, n: (pidx, nn(n), h))
    col = pl.BlockSpec((None, c, 1), lambda h, n: (h, nn(n), 0))
    scal = pl.BlockSpec((None, 1, 1), lambda h, n: (h, 0, 0))
    zspec = pl.BlockSpec((c, GDN_DV), lambda h, n: (nn(n), h))
    gnspec = pl.BlockSpec((1, GDN_DV), lambda h, n: (0, 0))
    sspec = pl.BlockSpec((None, None, GDN_DK, GDN_DV), lambda h, n: (h, nn(n), 0, 0))
    return plane, col, scal, zspec, gnspec, sspec


def _gdn_fwd_call(qkv, z, bl, al, a_log, dtb, gn, name):
    t = z.shape[0]
    h_n = GDN_HEADS
    n_chunks = t // CHUNK
    plane, col, scal, zspec, gnspec, sspec = _gdn_specs(n_chunks, False)

    def body(q_ref, k_ref, v_ref, z_ref, bl_ref, al_ref, a_ref, dtb_ref, gn_ref, o_ref, sall_ref, s_scr):
        n = pl.program_id(1)

        @pl.when(n == 0)
        def _():
            s_scr[...] = jnp.zeros_like(s_scr)

        s = s_scr[...]
        sall_ref[...] = s
        o, s_new = _gdn_chunk(q_ref[...], k_ref[...], v_ref[...], z_ref[...], bl_ref[...], al_ref[...],
                              a_ref[...], dtb_ref[...], gn_ref[...], s)
        o_ref[...] = o.astype(o_ref.dtype)
        s_scr[...] = s_new

    return pl.pallas_call(
        body, name=name, grid=(h_n, n_chunks),
        in_specs=[plane(0), plane(1), plane(2), zspec, col, col, scal, scal, gnspec],
        out_specs=[zspec, sspec],
        out_shape=[jax.ShapeDtypeStruct((t, h_n * GDN_DV), BF16),
                   jax.ShapeDtypeStruct((h_n, n_chunks, GDN_DK, GDN_DV), F32)],
        scratch_shapes=[pltpu.VMEM((GDN_DK, GDN_DV), F32)],
        compiler_params=_cp(("parallel", "arbitrary")),
    )(qkv, qkv, qkv, z, bl, al, a_log, dtb, gn)


def _gdn_bwd_call(qkv, z, bl, al, a_log, dtb, gn, sall, do, name):
    t = z.shape[0]
    h_n = GDN_HEADS
    n_chunks = t // CHUNK
    c = CHUNK
    plane, col, scal, zspec, gnspec, sspec = _gdn_specs(n_chunks, True)
    dplanes = pl.BlockSpec((3, c, GDN_DK), lambda h, n: (0, n_chunks - 1 - n, h))
    gnh = pl.BlockSpec((None, 1, GDN_DV), lambda h, n: (h, 0, 0))

    def body(q_ref, k_ref, v_ref, z_ref, bl_ref, al_ref, a_ref, dtb_ref, gn_ref, s_ref, do_ref,
             dqkv_ref, dz_ref, dbl_ref, dal_ref, da_ref, ddtb_ref, dgn_ref, ds_scr):
        n = pl.program_id(1)

        @pl.when(n == 0)
        def _():
            ds_scr[...] = jnp.zeros_like(ds_scr)
            da_ref[...] = jnp.zeros_like(da_ref)
            ddtb_ref[...] = jnp.zeros_like(ddtb_ref)
            dgn_ref[...] = jnp.zeros_like(dgn_ref)

        _, vjp = jax.vjp(_gdn_chunk, q_ref[...], k_ref[...], v_ref[...], z_ref[...], bl_ref[...],
                         al_ref[...], a_ref[...], dtb_ref[...], gn_ref[...], s_ref[...])
        dq, dk, dv, dz, dbl, dal, da, ddtb, dgn, ds = vjp((do_ref[...].astype(F32), ds_scr[...]))
        dqkv_ref[0] = dq
        dqkv_ref[1] = dk
        dqkv_ref[2] = dv
        dz_ref[...] = dz
        dbl_ref[...] = dbl
        dal_ref[...] = dal
        da_ref[...] += da
        ddtb_ref[...] += ddtb
        dgn_ref[...] += dgn
        ds_scr[...] = ds

    return pl.pallas_call(
        body, name=name, grid=(h_n, n_chunks),
        in_specs=[plane(0), plane(1), plane(2), zspec, col, col, scal, scal, gnspec, sspec, zspec],
        out_specs=[dplanes, zspec, col, col, scal, scal, gnh],
        out_shape=[jax.ShapeDtypeStruct((3, t, h_n * GDN_DK), F32),
                   jax.ShapeDtypeStruct((t, h_n * GDN_DV), F32),
                   jax.ShapeDtypeStruct((h_n, t, 1), F32), jax.ShapeDtypeStruct((h_n, t, 1), F32),
                   jax.ShapeDtypeStruct((h_n, 1, 1), F32), jax.ShapeDtypeStruct((h_n, 1, 1), F32),
                   jax.ShapeDtypeStruct((h_n, 1, GDN_DV), F32)],
        scratch_shapes=[pltpu.VMEM((GDN_DK, GDN_DV), F32)],
        compiler_params=_cp(("parallel", "arbitrary")),
    )(qkv, qkv, qkv, z, bl, al, a_log, dtb, gn, sall, do)


@functools.partial(jax.custom_vjp, nondiff_argnums=(7,))
def gdn(qkv, z, bl, al, a_log, dtb, gn, tag):
    return _gdn_fwd_call(qkv, z, bl, al, a_log, dtb, gn, "gdn_" + tag)[0]


def _gdn_f(qkv, z, bl, al, a_log, dtb, gn, tag):
    o, sall = _gdn_fwd_call(qkv, z, bl, al, a_log, dtb, gn, "gdn_" + tag)
    return o, (qkv, z, bl, al, a_log, dtb, gn, sall)


def _gdn_b(tag, res, do):
    dqkv, dz, dbl, dal, da, ddtb, dgn_h = _gdn_bwd_call(*res, do, "gdn_" + tag + "_bwd")
    return dqkv, dz, dbl, dal, da, ddtb, jnp.sum(dgn_h, axis=0)


gdn.defvjp(_gdn_f, _gdn_b)


def adamw(w, parts, m, v, name):
    n_layers = len(parts)
    n_parts, r, c = parts[0].shape
    assert w.shape == (n_layers * r, c), (w.shape, parts[0].shape)
    tr = r
    for cand in (512, 256, 128, 64, 32, 16, 8):
        if r % cand == 0 and cand * c <= 256 * 1024:
            tr = cand
            break
    nb = r // tr
    blk = pl.BlockSpec((tr, c), lambda l, i: (l * nb + i, 0))
    bc1 = 1.0 - ADAM_B1 ** ADAM_STEP
    bc2 = 1.0 - ADAM_B2 ** ADAM_STEP

    def part_spec(li):
        return pl.BlockSpec((n_parts, tr, c),
                            lambda l, i: (0, jnp.where(l == li, i, jnp.where(l < li, 0, nb - 1)), 0))

    def body(*refs):
        w_ref, p_refs = refs[0], refs[1:1 + n_layers]
        m_ref, v_ref, g_ref, d_ref, mo_ref, vo_ref = refs[1 + n_layers:]
        for li in range(n_layers):
            @pl.when(pl.program_id(0) == li)
            def _(p_ref=p_refs[li]):
                g = p_ref[0].astype(F32)
                for i in range(1, n_parts):
                    g = g + p_ref[i].astype(F32)
                m2 = ADAM_B1 * m_ref[...] + (1.0 - ADAM_B1) * g
                v2 = ADAM_B2 * v_ref[...] + (1.0 - ADAM_B2) * (g * g)
                g_ref[...] = g
                mo_ref[...] = m2
                vo_ref[...] = v2
                d_ref[...] = -ADAM_LR * ((m2 / bc1) / (jnp.sqrt(v2 / bc2) + ADAM_EPS)
                                         + ADAM_WD * w_ref[...])

    return pl.pallas_call(
        body, name=name, grid=(n_layers, nb),
        in_specs=[blk] + [part_spec(li) for li in range(n_layers)] + [blk, blk],
        out_specs=[blk] * 4, out_shape=[jax.ShapeDtypeStruct(w.shape, F32)] * 4,
        compiler_params=_cp(("arbitrary", "arbitrary")),
    )(w, *parts, m, v)


_HBM = pl.BlockSpec(memory_space=pltpu.HBM)
_SEM = pl.BlockSpec(memory_space=pltpu.SEMAPHORE)
_EFFECT = pltpu.SideEffectType.DATAFLOW_SIDE_EFFECTING


def _peer(x, y, c, d):
    px = 1 - x if d & 4 else x
    py = 1 - y if d & 2 else y
    pc = 1 - c if d & 1 else c
    return (px, py, pc), 4 * px + 2 * py + pc


def copy_start(arrays, scatter, carry, name):
    n = len(arrays)
    lands = [lax.empty(a.shape if scatter else (N_DEV,) + a.shape, a.dtype) for a in arrays]

    def body(*refs):
        srcs, dsts = refs[:n], refs[n:2 * n]
        sems = refs[2 * n + 1:4 * n + 1]
        x, y, c = (lax.axis_index(a) for a in AXES)
        me = 4 * x + 2 * y + c
        for k in range(n):
            for d in range(1, N_DEV):
                peer, pid = _peer(x, y, c, d)
                pltpu.make_async_remote_copy(
                    src_ref=srcs[k].at[pid] if scatter else srcs[k], dst_ref=dsts[k].at[me],
                    send_sem=sems[2 * k], recv_sem=sems[2 * k + 1],
                    device_id=peer, device_id_type=pl.DeviceIdType.MESH).start()

    operands = list(arrays) + lands + [carry]
    outs = pl.pallas_call(
        body, name=name,
        out_shape=tuple([pltpu.SemaphoreType.DMA(())] * (2 * n)
                        + [pltpu.HBM(a.shape, a.dtype) for a in operands]),
        in_specs=[_HBM] * (2 * n + 1),
        out_specs=tuple([_SEM] * (2 * n) + [_HBM] * (2 * n + 1)),
        input_output_aliases={i: 2 * n + i for i in range(2 * n + 1)},
        compiler_params=pltpu.CompilerParams(has_side_effects=_EFFECT),
    )(*[pltpu.with_memory_space_constraint(a, pltpu.HBM) for a in operands])
    sems, thru = outs[:2 * n], outs[2 * n:4 * n]
    handles = [(sems[2 * k], sems[2 * k + 1], thru[k], thru[n + k]) for k in range(n)]
    return outs[-1], handles


def copy_wait(handles, after, name):
    n = len(handles)
    sems = [s for h in handles for s in h[:2]]
    srcs = [h[2] for h in handles]
    lands = [h[3] for h in handles]

    def body(*refs):
        dsts = refs[n:2 * n]
        sem_refs = refs[2 * n:4 * n]
        x, y, c = (lax.axis_index(a) for a in AXES)
        for k in range(n):
            seven = dsts[k].at[pl.ds(0, N_DEV - 1)]
            pltpu.make_async_remote_copy(
                src_ref=seven, dst_ref=seven, send_sem=sem_refs[2 * k], recv_sem=sem_refs[2 * k + 1],
                device_id=(x, y, c), device_id_type=pl.DeviceIdType.MESH).wait()

    outs = pl.pallas_call(
        body, name=name,
        out_shape=tuple([pltpu.HBM(a.shape, a.dtype) for a in srcs + lands]),
        in_specs=[_HBM] * (2 * n) + [_SEM] * (2 * n) + [pl.BlockSpec(memory_space=pl.ANY)],
        out_specs=tuple([_HBM] * (2 * n)),
        input_output_aliases={i: i for i in range(2 * n)},
        compiler_params=pltpu.CompilerParams(has_side_effects=_EFFECT),
    )(*srcs, *lands, *sems, after)
    return list(outs[:n]), list(outs[n:])


def exchange(arrays, modes, name):
    n = len(arrays)
    hbm = pl.BlockSpec(memory_space=pltpu.HBM)
    out_shape = [jax.ShapeDtypeStruct(a.shape if md == "scatter" else (N_DEV,) + a.shape, a.dtype)
                 for a, md in zip(arrays, modes)]

    def body(*refs):
        ins, outs = refs[:n], refs[n:2 * n]
        send_sems, recv_sems, local_sems = refs[2 * n:]
        x, y, c = (lax.axis_index(a) for a in AXES)
        me = 4 * x + 2 * y + c

        def src(k, p):
            return ins[k].at[p] if modes[k] == "scatter" else ins[k]

        local = [pltpu.make_async_copy(src(k, me), outs[k].at[me], local_sems.at[k]) for k in range(n)]
        for cp in local:
            cp.start()
        started = []
        for d in range(1, N_DEV):
            px = 1 - x if d & 4 else x
            py = 1 - y if d & 2 else y
            pc = 1 - c if d & 1 else c
            pid = 4 * px + 2 * py + pc
            for k in range(n):
                pltpu.make_async_remote_copy(
                    src_ref=src(k, pid), dst_ref=outs[k].at[me],
                    send_sem=send_sems.at[k, d - 1], recv_sem=recv_sems.at[k, d - 1],
                    device_id=(px, py, pc), device_id_type=pl.DeviceIdType.MESH).start()
                started.append((k, d, pid, (px, py, pc)))
        for k, d, pid, peer in started:
            pltpu.make_async_remote_copy(
                src_ref=src(k, pid), dst_ref=outs[k].at[pid],
                send_sem=send_sems.at[k, d - 1], recv_sem=recv_sems.at[k, d - 1],
                device_id=peer, device_id_type=pl.DeviceIdType.MESH).wait()
        for cp in local:
            cp.wait()

    outs = pl.pallas_call(
        body, name=name, in_specs=[hbm] * n, out_specs=[hbm] * n, out_shape=out_shape,
        scratch_shapes=[pltpu.SemaphoreType.DMA((n, N_DEV - 1)), pltpu.SemaphoreType.DMA((n, N_DEV - 1)),
                        pltpu.SemaphoreType.DMA((n,))],
        compiler_params=pltpu.CompilerParams(has_side_effects=True),
    )(*arrays)
    return list(outs)


BIG = ("w_in", "w_uq", "w_ukv", "w_o_mla", "w_o_gdn", "w_o", "w_gate_up", "w_down")
ROW_SHARDED = ("w_o", "w_down")
SMALL = ("b_ada", "norm_mix", "norm_ffn", "q_a_norm", "kv_a_norm", "A_log", "dt_bias", "gdn_norm",
         "final_norm")
WEIGHTS = ("w_ada", "b_ada", "norm_mix", "norm_ffn", "w_in", "q_a_norm", "kv_a_norm", "w_uq", "w_ukv",
           "w_o_mla", "conv_w", "A_log", "dt_bias", "gdn_norm", "w_o_gdn", "w_o", "w_gate_up", "w_down",
           "final_norm")


def _unslot(g):
    return g.transpose(1, 0, 2).reshape(g.shape[1], -1)


def _cols(g):
    return g if g.shape[-1] % LANE == 0 else _unslot(g)


def _stack_rows(g):
    return g.reshape(-1, g.shape[-1])


def _pad_cols(a):
    return jnp.pad(a, ((0, 0), (0, _pad_lanes(a.shape[1]) - a.shape[1])))


def _rope(xv, cos, sin):
    x1, x2 = jnp.split(xv, 2, axis=-1)
    return jnp.concatenate([x1 * cos - x2 * sin, x2 * cos + x1 * sin], axis=-1)


MIX_WEIGHTS = ("w_uq", "w_ukv", "w_o_mla", "w_o_gdn", "w_o")
FFN_WEIGHTS = ("w_gate_up", "w_down")


def _stage_in(x, mod, nm, w_in_s, tg):
    d = x.shape[1]
    hg = GDN_HEADS
    w_in = _unslot(w_in_s)
    o1 = Q_LORA + KV_LORA + QK_ROPE
    o2 = o1 + 2 * hg * GDN_DK + hg * GDN_DV
    o3 = o2 + hg * GDN_DV
    o4 = o3 + 2 * hg
    h = ada_norm(x, nm, mod[:, d:2 * d], mod[:, :d], "mix" + tg)
    return (mm(h, _pad_cols(w_in[:, :o1]), "in_a" + tg, F32), mm(h, w_in[:, o1:o2], "in_qkv" + tg, F32),
            mm(h, w_in[:, o2:o3], "in_z" + tg, F32), mm(h, _pad_cols(w_in[:, o3:o4]), "in_ba" + tg, F32),
            mm(h, w_in[:, o4:o4 + 2 * d], "in_g" + tg, F32))


def _stage_mix(x, mod, seg_a, qkv, z, ba, gl, w_uq_s, w_ukv_s, w_o_mla_s, w_o_gdn_s, w_o_s, conv_s,
               qan, kvan, a_log, dtb, gn, cos, sin, tg):
    t, d = x.shape
    hq, hg = MLA_HEADS, GDN_HEADS
    w_uq = _unslot(w_uq_s).reshape(Q_LORA, hq, QK_NOPE + QK_ROPE)
    w_uq = jnp.concatenate([w_uq[:, :, :QK_NOPE].reshape(Q_LORA, hq * QK_NOPE),
                            w_uq[:, :, QK_NOPE:].reshape(Q_LORA, hq * QK_ROPE)], axis=1)
    c_q = seg_a[:, :Q_LORA]
    c_kv = seg_a[:, Q_LORA:Q_LORA + KV_LORA]
    k_pe = seg_a[:, Q_LORA + KV_LORA:Q_LORA + KV_LORA + QK_ROPE]
    qf = mm(rms_norm(c_q, qan, "qa" + tg), w_uq, "uq" + tg, F32)
    kvf = mm(rms_norm(c_kv, kvan, "kva" + tg), _cols(w_ukv_s), "ukv" + tg, F32)
    qn = qf[:, :hq * QK_NOPE]
    q_pe = qf[:, hq * QK_NOPE:].reshape(t, hq, QK_ROPE)
    qr = _rope(q_pe, cos[:, None, :], sin[:, None, :]).transpose(1, 0, 2)
    kr = _rope(k_pe, cos, sin)
    y_a = mm(attention(qn, qr, kvf, kr, tg), _cols(w_o_mla_s), "o_mla" + tg, F32)
    conv_w = conv_s.transpose(1, 0, 2).reshape(CONV_WIDTH, -1)
    qkv_c = conv_silu(qkv, conv_w, tg)
    bl = ba[:, :hg].T[:, :, None]
    al = ba[:, hg:2 * hg].T[:, :, None]
    o_gdn = gdn(qkv_c, z, bl, al, a_log.reshape(hg, 1, 1), dtb.reshape(hg, 1, 1), gn, tg)
    y_b = mm(o_gdn, _cols(w_o_gdn_s), "o_gdn" + tg, F32)
    mix = mm(gate_mix(gl, y_a, y_b, tg), _stack_rows(w_o_s), "w_o" + tg, F32)
    return resid(x, mod[:, 2 * d:3 * d], mix, "mix" + tg)


def _stage_ffn(x, mod, nf, w_gu_s, w_down_s, tg):
    d = x.shape[1]
    h = ada_norm(x, nf, mod[:, 4 * d:5 * d], mod[:, 3 * d:4 * d], "ffn" + tg)
    gu = mm(h, _cols(w_gu_s), "gu" + tg, F32)
    dn = mm(swiglu(gu, tg), _stack_rows(w_down_s), "down" + tg, F32)
    return resid(x, mod[:, 5 * d:6 * d], dn, "ffn" + tg)


def _flat_row(arrs):
    v = jnp.concatenate([a.reshape(-1) for a in arrs])
    return jnp.pad(v, (0, _pad_lanes(v.shape[0]) - v.shape[0]))[None, :]


def kernel(x, c, positions, w_ada, b_ada, norm_mix, norm_ffn, w_in, q_a_norm, kv_a_norm, w_uq, w_ukv, w_o_mla, conv_w, A_log, dt_bias, gdn_norm, w_o_gdn, w_o, w_gate_up, w_down, final_norm, loss_target, m_w_ada, m_b_ada, m_norm_mix, m_norm_ffn, m_w_in, m_q_a_norm, m_kv_a_norm, m_w_uq, m_w_ukv, m_w_o_mla, m_conv_w, m_A_log, m_dt_bias, m_gdn_norm, m_w_o_gdn, m_w_o, m_w_gate_up, m_w_down, m_final_norm, v_w_ada, v_b_ada, v_norm_mix, v_norm_ffn, v_w_in, v_q_a_norm, v_kv_a_norm, v_w_uq, v_w_ukv, v_w_o_mla, v_conv_w, v_A_log, v_dt_bias, v_gdn_norm, v_w_o_gdn, v_w_o, v_w_gate_up, v_w_down, v_final_norm):
    given = dict(locals())
    t, d = x.shape[1], x.shape[2]
    n_ada = w_ada.shape[2]
    me = 4 * lax.axis_index("x") + 2 * lax.axis_index("y") + lax.axis_index("c")

    def with_own(land, own):
        return lax.dynamic_update_slice(land, own[None], (me,) + (0,) * own.ndim)

    keys = [(n, l) for l in range(DEPTH) for n in BIG]
    c_after, handles = copy_start([given[n][l].astype(BF16) for n, l in keys], False, c, "gather_start")
    handles = dict(zip(keys, handles))

    def landed(ks, after, name):
        srcs, lands = copy_wait([handles[k] for k in ks], after, name)
        return [with_own(land, src) for src, land in zip(srcs, lands)]

    got = exchange([c_after, conv_w], ["gather", "gather"], "gather_small")
    c_all, conv_g = got[0].reshape(N_DEV, d), got[1]
    c_rows = jnp.pad(c_all, ((0, 16 - N_DEV), (0, 0)))
    mod_cols = jnp.stack([_mm(c_rows, w_ada[l], "nn", F32, "ada_mod%d" % l, a_act="silu")[:N_DEV]
                          for l in range(DEPTH)], axis=1)
    mod_mine = exchange([mod_cols], ["scatter"], "scatter_mod")[0]
    mods = mod_mine.transpose(1, 0, 2).reshape(DEPTH, N_DEV * n_ada) + b_ada

    inv_freq = 1.0 / (ROPE_THETA ** (jnp.arange(0, QK_ROPE, 2, dtype=F32) / QK_ROPE))
    ang = positions[0].astype(F32)[:, None] * inv_freq
    cos, sin = jnp.cos(ang), jnp.sin(ang)
    xl = x[0]
    after = mods
    vjps = []
    for l in range(DEPTH):
        tg = str(l)
        mod = mods[l:l + 1]
        (w_in_s,) = landed([("w_in", l)], after, "wait_in" + tg)
        seg, vjp_in = jax.vjp(lambda *a, tg=tg: _stage_in(*a, tg), xl, mod, norm_mix[l:l + 1], w_in_s)
        w_mix = landed([(n, l) for n in MIX_WEIGHTS], seg[0], "wait_mix" + tg)
        xm, vjp_mix = jax.vjp(lambda *a, tg=tg: _stage_mix(*a, cos, sin, tg), xl, mod, *seg, *w_mix,
                              conv_g[:, l], q_a_norm[l:l + 1], kv_a_norm[l:l + 1], A_log[l], dt_bias[l],
                              gdn_norm[l:l + 1])
        w_ffn = landed([(n, l) for n in FFN_WEIGHTS], xm, "wait_ffn" + tg)
        xl, vjp_ffn = jax.vjp(lambda *a, tg=tg: _stage_ffn(*a, tg), xm, mod, norm_ffn[l:l + 1], *w_ffn)
        after = xl
        vjps.append((vjp_in, vjp_mix, vjp_ffn))

    loss_t, g, dfn = loss_head(xl, final_norm[None, :], loss_target[0])
    loss = lax.psum(loss_t[0, 0], AXES)
    dsmall = {n: [None] * DEPTH for n in SMALL + ("conv_w",)}
    dmods = [None] * DEPTH
    sent = {}

    def send(ks, grads, carry, name):
        carry, hs = copy_start(list(grads), True, carry, name)
        sent.update(zip(ks, hs))
        return carry

    for l in reversed(range(DEPTH)):
        tg = str(l)
        vjp_in, vjp_mix, vjp_ffn = vjps[l]
        dxm, dmod_f, dsmall["norm_ffn"][l], *dw = vjp_ffn(g)
        dxm = send([(n, l) for n in FFN_WEIGHTS], dw, dxm, "scatter_ffn" + tg)
        dx_m, dmod_m, *rest = vjp_mix(dxm)
        dseg, dw, rest = rest[:5], rest[5:5 + len(MIX_WEIGHTS)], rest[5 + len(MIX_WEIGHTS):]
        dseg[0] = send([(n, l) for n in MIX_WEIGHTS], dw, dseg[0], "scatter_mix" + tg)
        for n, gr in zip(("conv_w", "q_a_norm", "kv_a_norm", "A_log", "dt_bias", "gdn_norm"), rest):
            dsmall[n][l] = gr
        dx_i, dmod_i, dsmall["norm_mix"][l], dw_in = vjp_in(tuple(dseg))
        g = dx_i + dx_m
        if l > 0:
            g = send([("w_in", l)], [dw_in], g, "scatter_in" + tg)
        else:
            dmod_i = send([("w_in", l)], [dw_in], dmod_i, "scatter_in" + tg)
        dmods[l] = dmod_f + dmod_m + dmod_i
    dx = g
    dmods = jnp.concatenate(dmods, axis=0)
    dconv = jnp.stack(dsmall.pop("conv_w"), axis=1)
    dsmall = {n: jnp.concatenate(v, axis=0) if v[0].ndim == 2 else jnp.stack(v)
              for n, v in dsmall.items() if v[0] is not None}
    dsmall["b_ada"] = dmods
    dsmall["final_norm"] = dfn[0]

    dmod_cols = dmods.reshape(DEPTH, N_DEV, n_ada).transpose(1, 0, 2)
    conv_parts, dmod_all, small_parts = exchange(
        [dconv, dmod_cols, _flat_row([dsmall[n] for n in SMALL])], ["scatter", "scatter", "gather"],
        "exchange_small")

    res = {}
    dm_rows = jnp.pad(dmod_all, ((0, 16 - N_DEV), (0, 0), (0, 0)))
    g_ada = jnp.stack([_mm(c_rows, dm_rows[:, l], "tn", F32, "ada_dw%d" % l, a_act="silu")
                       for l in range(DEPTH)])
    r2 = (DEPTH * d, n_ada)
    outs = adamw(w_ada.reshape(r2), [g_ada.reshape((1,) + r2)], m_w_ada.reshape(r2), v_w_ada.reshape(r2),
                 "adamw_w_ada")
    res["w_ada"] = [o.reshape(w_ada.shape) for o in outs]
    after = g_ada
    for group, gname in ((FFN_WEIGHTS, "ffn"), (MIX_WEIGHTS, "mix"), (("w_in",), "in")):
        ks = [(n, l) for l in reversed(range(DEPTH)) for n in group]
        srcs, lands = copy_wait([sent[k] for k in ks], after, "scatter_wait_" + gname)
        parts = {k: with_own(land, lax.dynamic_index_in_dim(src, me, 0, keepdims=False))
                 for k, src, land in zip(ks, srcs, lands)}
        for n in group:
            w = given[n]
            r2 = (w.shape[0] * w.shape[1], w.shape[2])
            outs = adamw(w.reshape(r2), [parts[(n, l)] for l in range(DEPTH)], given["m_" + n].reshape(r2),
                         given["v_" + n].reshape(r2), "adamw_" + n)
            res[n] = [o.reshape(w.shape) for o in outs]
        after = res[group[-1]][0]
    packed = SMALL + ("conv_w",)
    p_all = jnp.concatenate([small_parts, conv_parts.reshape(N_DEV, 1, -1)], axis=2)
    pack = lambda pre: jnp.concatenate([_flat_row([given[pre + n] for n in SMALL]),
                                        given[pre + "conv_w"].reshape(1, -1)], axis=1)
    outs = adamw(pack(""), [p_all], pack("m_"), pack("v_"), "adamw_small")
    off = 0
    for n in packed:
        if n == "conv_w":
            off = small_parts.shape[2]
        size = math.prod(given[n].shape)
        res[n] = [o[0, off:off + size].reshape(given[n].shape) for o in outs]
        off += size

    return (loss, dx[None]) + tuple(res[n][i] for i in range(4) for n in WEIGHTS)
```

```python
import functools
import math

import jax
import jax.numpy as jnp
from jax import lax
from jax.experimental import pallas as pl
from jax.experimental.pallas import tpu as pltpu

F32 = jnp.float32
BF16 = jnp.bfloat16

MLA_HEADS = 8
QK_NOPE = 128
QK_ROPE = 64
V_HEAD = 128
Q_LORA = 512
KV_LORA = 512
ROPE_THETA = 10000.0
GDN_HEADS = 8
GDN_DK = 128
GDN_DV = 128
CONV_WIDTH = 4
CHUNK = 64
DEPTH = 2
EPS = 1e-6
ADAM_LR = 0.001
ADAM_B1 = 0.9
ADAM_B2 = 0.999
ADAM_EPS = 1e-08
ADAM_WD = 0.01
ADAM_STEP = 10

N_DEV = 8
AXES = ("x", "y", "c")
LANE = 128
VMEM_LIMIT = 48 * 1024 * 1024
HIGHEST = lax.Precision.HIGHEST

NN = (((1,), (0,)), ((), ()))
NT = (((1,), (1,)), ((), ()))
TN = (((0,), (0,)), ((), ()))


def _cp(sem=None):
    return pltpu.CompilerParams(dimension_semantics=sem, vmem_limit_bytes=VMEM_LIMIT)


def _tile(n, cap):
    if n <= cap:
        return n
    for t in range(cap - cap % LANE, 0, -LANE):
        if n % t == 0:
            return t
    return n


def _rows(t, cap=256):
    return cap if t % cap == 0 else t


def _pad_lanes(n):
    return -(-n // LANE) * LANE


def _sigmoid(x):
    return 1.0 / (1.0 + jnp.exp(-x))


def _softplus(x):
    return jnp.maximum(x, 0.0) + jnp.log(1.0 + jnp.exp(-jnp.abs(x)))


def _tile_slot(n, cap):
    t = _tile(n, cap)
    return n if t < 256 < n <= 1536 else t


def _mm(a, b, dims, out_dtype, name, a_act=None, slots=False):
    if dims == "nn":
        m, k = a.shape
        n = b.shape[-1] * (N_DEV if slots else 1)
    elif dims == "nt":
        m, k = a.shape
        n = b.shape[-2]
    else:
        k, m = a.shape
        n = b.shape[-1]
    tm = _tile(m, 1024)
    tn = _tile_slot(n // N_DEV, 512) if slots and dims != "nt" else _tile(n, 512)
    tk = _tile_slot(k // N_DEV, 1024) if slots and dims == "nt" else _tile(k, 1024)
    nk = k // tk
    per_n = (n // N_DEV) // tn if slots else 1
    per_k = (k // N_DEV) // tk if slots else 1
    if dims == "tn":
        a_spec = pl.BlockSpec((tk, tm), lambda i, j, kk: (kk, i))
    else:
        a_spec = pl.BlockSpec((tm, tk), lambda i, j, kk: (i, kk))
    if dims == "nt":
        if slots:
            b_spec = pl.BlockSpec((None, tn, tk), lambda i, j, kk: (kk // per_k, j, kk % per_k))
        else:
            b_spec = pl.BlockSpec((tn, tk), lambda i, j, kk: (j, kk))
    elif dims == "nn" and slots:
        b_spec = pl.BlockSpec((None, tk, tn), lambda i, j, kk: (j // per_n, kk, j % per_n))
    else:
        b_spec = pl.BlockSpec((tk, tn), lambda i, j, kk: (kk, j))
    if dims == "tn" and slots:
        out_spec = pl.BlockSpec((None, tm, tn), lambda i, j, kk: (j // per_n, i, j % per_n))
        out_shape = jax.ShapeDtypeStruct((N_DEV, m, n // N_DEV), out_dtype)
    else:
        out_spec = pl.BlockSpec((tm, tn), lambda i, j, kk: (i, j))
        out_shape = jax.ShapeDtypeStruct((m, n), out_dtype)
    dn = {"nn": NN, "nt": NT, "tn": TN}[dims]

    def body(a_ref, b_ref, o_ref, acc_ref):
        kk = pl.program_id(2)

        @pl.when(kk == 0)
        def _():
            acc_ref[...] = jnp.zeros_like(acc_ref)

        av = a_ref[...]
        if a_act == "silu":
            av = av * _sigmoid(av)
        acc_ref[...] += lax.dot_general(av.astype(BF16), b_ref[...].astype(BF16), dn,
                                        preferred_element_type=F32)

        @pl.when(kk == nk - 1)
        def _():
            o_ref[...] = acc_ref[...].astype(o_ref.dtype)

    return pl.pallas_call(
        body, name=name, grid=(m // tm, n // tn, nk),
        in_specs=[a_spec, b_spec], out_specs=out_spec, out_shape=out_shape,
        scratch_shapes=[pltpu.VMEM((tm, tn), F32)],
        compiler_params=_cp(("parallel", "parallel", "arbitrary")),
    )(a, b)


@functools.partial(jax.custom_vjp, nondiff_argnums=(2, 3))
def mm(a, b, tag, out_dtype):
    return _mm(a, b, "nn", out_dtype, "mm_" + tag, slots=b.ndim == 3)


def _mm_f(a, b, tag, out_dtype):
    return mm(a, b, tag, out_dtype), (a, b)


def _mm_b(tag, out_dtype, res, g):
    a, b = res
    slots = b.ndim == 3
    da = _mm(g, b, "nt", a.dtype, "mm_" + tag + "_da", slots=slots)
    db = _mm(a, g, "tn", b.dtype, "mm_" + tag + "_db", slots=slots)
    return da, db


mm.defvjp(_mm_f, _mm_b)


def _norm_fwd_call(x, nw, sc, sh, name):
    t, d = x.shape
    tr = _rows(t)
    mod = sc is not None
    row = pl.BlockSpec((tr, d), lambda i: (i, 0))
    vec = pl.BlockSpec((1, d), lambda i: (0, 0))

    def body(*refs):
        if mod:
            x_ref, nw_ref, sc_ref, sh_ref, o_ref = refs
        else:
            x_ref, nw_ref, o_ref = refs
        xv = x_ref[...]
        r = lax.rsqrt(jnp.mean(xv * xv, axis=-1, keepdims=True) + EPS)
        y = (xv * r) * nw_ref[...]
        if mod:
            y = y * (1.0 + sc_ref[...]) + sh_ref[...]
        o_ref[...] = y.astype(o_ref.dtype)

    args = (x, nw, sc, sh) if mod else (x, nw)
    return pl.pallas_call(
        body, name=name, grid=(t // tr,),
        in_specs=[row] + [vec] * (len(args) - 1), out_specs=row,
        out_shape=jax.ShapeDtypeStruct((t, d), BF16),
        compiler_params=_cp(("parallel",)),
    )(*args)


def _norm_bwd_call(x, nw, sc, dh, name):
    t, d = x.shape
    tr = _rows(t)
    mod = sc is not None
    row = pl.BlockSpec((tr, d), lambda i: (i, 0))
    vec = pl.BlockSpec((1, d), lambda i: (0, 0))

    def body(*refs):
        if mod:
            x_ref, nw_ref, sc_ref, dh_ref, dx_ref, dnw_ref, dsc_ref, dsh_ref = refs
        else:
            x_ref, nw_ref, dh_ref, dx_ref, dnw_ref = refs
        i = pl.program_id(0)
        xv = x_ref[...]
        dh = dh_ref[...].astype(F32)
        r = lax.rsqrt(jnp.mean(xv * xv, axis=-1, keepdims=True) + EPS)
        y = xv * r
        a = nw_ref[...] * (1.0 + sc_ref[...]) if mod else nw_ref[...]
        dy = dh * a
        dx_ref[...] = r * (dy - y * jnp.mean(dy * y, axis=-1, keepdims=True))
        da = jnp.sum(dh * y, axis=0, keepdims=True)

        @pl.when(i == 0)
        def _():
            dnw_ref[...] = jnp.zeros_like(dnw_ref)
            if mod:
                dsc_ref[...] = jnp.zeros_like(dsc_ref)
                dsh_ref[...] = jnp.zeros_like(dsh_ref)

        if mod:
            dnw_ref[...] += da * (1.0 + sc_ref[...])
            dsc_ref[...] += da * nw_ref[...]
            dsh_ref[...] += jnp.sum(dh, axis=0, keepdims=True)
        else:
            dnw_ref[...] += da

    args = (x, nw, sc, dh) if mod else (x, nw, dh)
    n_vec = 3 if mod else 1
    return pl.pallas_call(
        body, name=name, grid=(t // tr,),
        in_specs=[row] + [vec] * (len(args) - 2) + [row],
        out_specs=[row] + [vec] * n_vec,
        out_shape=[jax.ShapeDtypeStruct((t, d), F32)] + [jax.ShapeDtypeStruct((1, d), F32)] * n_vec,
        compiler_params=_cp(("arbitrary",)),
    )(*args)


@functools.partial(jax.custom_vjp, nondiff_argnums=(4,))
def ada_norm(x, nw, sc, sh, tag):
    return _norm_fwd_call(x, nw, sc, sh, "adanorm_" + tag)


def _ada_norm_f(x, nw, sc, sh, tag):
    return _norm_fwd_call(x, nw, sc, sh, "adanorm_" + tag), (x, nw, sc)


def _ada_norm_b(tag, res, dh):
    x, nw, sc = res
    dx, dnw, dsc, dsh = _norm_bwd_call(x, nw, sc, dh, "adanorm_" + tag + "_bwd")
    return dx, dnw, dsc, dsh


ada_norm.defvjp(_ada_norm_f, _ada_norm_b)


@functools.partial(jax.custom_vjp, nondiff_argnums=(2,))
def rms_norm(x, nw, tag):
    return _norm_fwd_call(x, nw, None, None, "rms_" + tag)


def _rms_norm_f(x, nw, tag):
    return _norm_fwd_call(x, nw, None, None, "rms_" + tag), (x, nw)


def _rms_norm_b(tag, res, dh):
    x, nw = res
    dx, dnw = _norm_bwd_call(x, nw, None, dh, "rms_" + tag + "_bwd")
    return dx, dnw


rms_norm.defvjp(_rms_norm_f, _rms_norm_b)


def _gate_mix_fwd_call(gl, ya, yb, name):
    t, d = ya.shape
    tr = _rows(t)
    row = pl.BlockSpec((tr, d), lambda i: (i, 0))

    def body(ga_ref, gb_ref, ya_ref, yb_ref, o_ref):
        o_ref[...] = (_sigmoid(ga_ref[...]) * ya_ref[...]
                      + _sigmoid(gb_ref[...]) * yb_ref[...]).astype(o_ref.dtype)

    return pl.pallas_call(
        body, name=name, grid=(t // tr,),
        in_specs=[row, pl.BlockSpec((tr, d), lambda i: (i, 1)), row, row], out_specs=row,
        out_shape=jax.ShapeDtypeStruct((t, d), BF16),
        compiler_params=_cp(("parallel",)),
    )(gl, gl, ya, yb)


def _gate_mix_bwd_call(gl, ya, yb, dm, name):
    t, d = ya.shape
    tr = _rows(t)
    row = pl.BlockSpec((tr, d), lambda i: (i, 0))
    wide = pl.BlockSpec((tr, 2 * d), lambda i: (i, 0))

    def body(gl_ref, ya_ref, yb_ref, dm_ref, dgl_ref, dya_ref, dyb_ref):
        dm = dm_ref[...].astype(F32)
        ga = _sigmoid(gl_ref[:, :d])
        gb = _sigmoid(gl_ref[:, d:])
        dya_ref[...] = dm * ga
        dyb_ref[...] = dm * gb
        dgl_ref[:, :d] = dm * ya_ref[...] * ga * (1.0 - ga)
        dgl_ref[:, d:] = dm * yb_ref[...] * gb * (1.0 - gb)

    return pl.pallas_call(
        body, name=name, grid=(t // tr,),
        in_specs=[wide, row, row, row], out_specs=[wide, row, row],
        out_shape=[jax.ShapeDtypeStruct((t, 2 * d), F32), jax.ShapeDtypeStruct((t, d), F32),
                   jax.ShapeDtypeStruct((t, d), F32)],
        compiler_params=_cp(("parallel",)),
    )(gl, ya, yb, dm)


@functools.partial(jax.custom_vjp, nondiff_argnums=(3,))
def gate_mix(gl, ya, yb, tag):
    return _gate_mix_fwd_call(gl, ya, yb, "gatemix_" + tag)


def _gate_mix_f(gl, ya, yb, tag):
    return _gate_mix_fwd_call(gl, ya, yb, "gatemix_" + tag), (gl, ya, yb)


def _gate_mix_b(tag, res, dm):
    return tuple(_gate_mix_bwd_call(*res, dm, "gatemix_" + tag + "_bwd"))


gate_mix.defvjp(_gate_mix_f, _gate_mix_b)


def _resid_fwd_call(x, gt, m, name):
    t, d = x.shape
    tr = _rows(t)
    row = pl.BlockSpec((tr, d), lambda i: (i, 0))
    vec = pl.BlockSpec((1, d), lambda i: (0, 0))

    def body(x_ref, gt_ref, m_ref, o_ref):
        o_ref[...] = x_ref[...] + gt_ref[...] * m_ref[...]

    return pl.pallas_call(
        body, name=name, grid=(t // tr,), in_specs=[row, vec, row], out_specs=row,
        out_shape=jax.ShapeDtypeStruct((t, d), F32), compiler_params=_cp(("parallel",)),
    )(x, gt, m)


def _resid_bwd_call(gt, m, g, name):
    t, d = m.shape
    tr = _rows(t)
    row = pl.BlockSpec((tr, d), lambda i: (i, 0))
    vec = pl.BlockSpec((1, d), lambda i: (0, 0))

    def body(gt_ref, m_ref, g_ref, dm_ref, dgt_ref):
        i = pl.program_id(0)
        g = g_ref[...]
        dm_ref[...] = g * gt_ref[...]

        @pl.when(i == 0)
        def _():
            dgt_ref[...] = jnp.zeros_like(dgt_ref)

        dgt_ref[...] += jnp.sum(g * m_ref[...], axis=0, keepdims=True)

    return pl.pallas_call(
        body, name=name, grid=(t // tr,), in_specs=[vec, row, row], out_specs=[row, vec],
        out_shape=[jax.ShapeDtypeStruct((t, d), F32), jax.ShapeDtypeStruct((1, d), F32)],
        compiler_params=_cp(("arbitrary",)),
    )(gt, m, g)


@functools.partial(jax.custom_vjp, nondiff_argnums=(3,))
def resid(x, gt, m, tag):
    return _resid_fwd_call(x, gt, m, "resid_" + tag)


def _resid_f(x, gt, m, tag):
    return _resid_fwd_call(x, gt, m, "resid_" + tag), (gt, m)


def _resid_b(tag, res, g):
    gt, m = res
    dm, dgt = _resid_bwd_call(gt, m, g, "resid_" + tag + "_bwd")
    return g, dgt, dm


resid.defvjp(_resid_f, _resid_b)


def _swiglu_fwd_call(gu, name):
    t, f2 = gu.shape
    f = f2 // 2
    tr = _rows(t, 128)
    half = pl.BlockSpec((tr, f), lambda i: (i, 0))

    def body(g_ref, u_ref, o_ref):
        g = g_ref[...]
        o_ref[...] = (g * _sigmoid(g) * u_ref[...]).astype(o_ref.dtype)

    return pl.pallas_call(
        body, name=name, grid=(t // tr,),
        in_specs=[half, pl.BlockSpec((tr, f), lambda i: (i, 1))], out_specs=half,
        out_shape=jax.ShapeDtypeStruct((t, f), BF16), compiler_params=_cp(("parallel",)),
    )(gu, gu)


def _swiglu_bwd_call(gu, da, name):
    t, f2 = gu.shape
    f = f2 // 2
    tr = _rows(t, 128)
    wide = pl.BlockSpec((tr, f2), lambda i: (i, 0))

    def body(gu_ref, da_ref, dgu_ref):
        g = gu_ref[:, :f]
        u = gu_ref[:, f:]
        da = da_ref[...].astype(F32)
        s = _sigmoid(g)
        dgu_ref[:, :f] = da * u * s * (1.0 + g * (1.0 - s))
        dgu_ref[:, f:] = da * g * s

    return pl.pallas_call(
        body, name=name, grid=(t // tr,),
        in_specs=[wide, pl.BlockSpec((tr, f), lambda i: (i, 0))], out_specs=wide,
        out_shape=jax.ShapeDtypeStruct((t, f2), F32), compiler_params=_cp(("parallel",)),
    )(gu, da)


@functools.partial(jax.custom_vjp, nondiff_argnums=(1,))
def swiglu(gu, tag):
    return _swiglu_fwd_call(gu, "swiglu_" + tag)


def _swiglu_f(gu, tag):
    return _swiglu_fwd_call(gu, "swiglu_" + tag), (gu,)


def _swiglu_b(tag, res, da):
    return (_swiglu_bwd_call(res[0], da, "swiglu_" + tag + "_bwd"),)


swiglu.defvjp(_swiglu_f, _swiglu_b)


def loss_head(x, fw, tgt):
    t, d = x.shape
    tr = _rows(t)
    row = pl.BlockSpec((tr, d), lambda i: (i, 0))
    vec = pl.BlockSpec((1, d), lambda i: (0, 0))
    tile = pl.BlockSpec((8, LANE), lambda i: (0, 0))

    def body(x_ref, fw_ref, tgt_ref, loss_ref, dx_ref, dfw_ref):
        i = pl.program_id(0)
        xv = x_ref[...]
        fw = fw_ref[...]
        r = lax.rsqrt(jnp.mean(xv * xv, axis=-1, keepdims=True) + EPS)
        yh = xv * r
        e = yh * fw - tgt_ref[...]
        dy = e * (1.0 / d)
        dyw = dy * fw
        dx_ref[...] = r * (dyw - yh * jnp.mean(dyw * yh, axis=-1, keepdims=True))

        @pl.when(i == 0)
        def _():
            loss_ref[...] = jnp.zeros_like(loss_ref)
            dfw_ref[...] = jnp.zeros_like(dfw_ref)

        loss_ref[...] += 0.5 * jnp.sum(jnp.mean(e * e, axis=-1, keepdims=True))
        dfw_ref[...] += jnp.sum(dy * yh, axis=0, keepdims=True)

    return pl.pallas_call(
        body, name="loss_head", grid=(t // tr,), in_specs=[row, vec, row],
        out_specs=[tile, row, vec],
        out_shape=[jax.ShapeDtypeStruct((8, LANE), F32), jax.ShapeDtypeStruct((t, d), F32),
                   jax.ShapeDtypeStruct((1, d), F32)],
        compiler_params=_cp(("arbitrary",)),
    )(x, fw, tgt)


def _attn_fwd_call(qn, qr, kv, kr, name):
    t = qn.shape[0]
    h_n = MLA_HEADS
    tq = _rows(t)
    nq = t // tq
    scale = (QK_NOPE + QK_ROPE) ** -0.5

    def body(qn_ref, qr_ref, kn_ref, v_ref, kr_ref, o_ref, lse_ref, m_scr, l_scr, acc_scr):
        i, j = pl.program_id(1), pl.program_id(2)

        @pl.when(j == 0)
        def _():
            m_scr[...] = jnp.full_like(m_scr, -1e30)
            l_scr[...] = jnp.zeros_like(l_scr)
            acc_scr[...] = jnp.zeros_like(acc_scr)

        @pl.when(j <= i)
        def _():
            s = lax.dot_general(qn_ref[...].astype(BF16), kn_ref[...].astype(BF16), NT,
                                preferred_element_type=F32)
            s += lax.dot_general(qr_ref[...].astype(BF16), kr_ref[...].astype(BF16), NT,
                                 preferred_element_type=F32)
            s = s * scale
            rows = i * tq + lax.broadcasted_iota(jnp.int32, (tq, tq), 0)
            cols = j * tq + lax.broadcasted_iota(jnp.int32, (tq, tq), 1)
            s = jnp.where(cols <= rows, s, -1e30)
            m_old = m_scr[...]
            m_new = jnp.maximum(m_old, jnp.max(s, axis=-1, keepdims=True))
            p = jnp.exp(s - m_new)
            alpha = jnp.exp(m_old - m_new)
            l_scr[...] = alpha * l_scr[...] + jnp.sum(p, axis=-1, keepdims=True)
            acc_scr[...] = alpha * acc_scr[...] + jnp.dot(p.astype(BF16), v_ref[...].astype(BF16),
                                                           preferred_element_type=F32)
            m_scr[...] = m_new

        @pl.when(j == i)
        def _():
            o_ref[...] = (acc_scr[...] / l_scr[...]).astype(o_ref.dtype)
            lse_ref[...] = m_scr[...] + jnp.log(l_scr[...])

    return pl.pallas_call(
        body, name=name, grid=(h_n, nq, nq),
        in_specs=[
            pl.BlockSpec((tq, QK_NOPE), lambda h, i, j: (i, h)),
            pl.BlockSpec((None, tq, QK_ROPE), lambda h, i, j: (h, i, 0)),
            pl.BlockSpec((tq, QK_NOPE), lambda h, i, j: (jnp.minimum(j, i), 2 * h)),
            pl.BlockSpec((tq, V_HEAD), lambda h, i, j: (jnp.minimum(j, i), 2 * h + 1)),
            pl.BlockSpec((tq, QK_ROPE), lambda h, i, j: (jnp.minimum(j, i), 0)),
        ],
        out_specs=[
            pl.BlockSpec((tq, V_HEAD), lambda h, i, j: (i, h)),
            pl.BlockSpec((None, tq, 1), lambda h, i, j: (h, i, 0)),
        ],
        out_shape=[jax.ShapeDtypeStruct((t, h_n * V_HEAD), BF16),
                   jax.ShapeDtypeStruct((h_n, t, 1), F32)],
        scratch_shapes=[pltpu.VMEM((tq, 1), F32), pltpu.VMEM((tq, 1), F32),
                        pltpu.VMEM((tq, V_HEAD), F32)],
        compiler_params=_cp(("parallel", "parallel", "arbitrary")),
    )(qn, qr, kv, kv, kr)


def _attn_bwd_call(qn, qr, kv, kr, o, lse, do, name):
    t = qn.shape[0]
    h_n = MLA_HEADS
    tq = _rows(t)
    nq = t // tq
    scale = (QK_NOPE + QK_ROPE) ** -0.5

    def body(qn_ref, qr_ref, kn_ref, v_ref, kr_ref, o_ref, lse_ref, do_ref,
             dqn_ref, dqr_ref, dkv_ref, dkr_ref, dkn_scr, dv_scr, dkr_scr):
        j, i = pl.program_id(1), pl.program_id(2)

        @pl.when(jnp.logical_and(j == 0, i == 0))
        def _():
            dqn_ref[...] = jnp.zeros_like(dqn_ref)
            dqr_ref[...] = jnp.zeros_like(dqr_ref)

        @pl.when(i == 0)
        def _():
            dkn_scr[...] = jnp.zeros_like(dkn_scr)
            dv_scr[...] = jnp.zeros_like(dv_scr)
            dkr_scr[...] = jnp.zeros_like(dkr_scr)

        @pl.when(i >= j)
        def _():
            qn_b = qn_ref[...].astype(BF16)
            qr_b = qr_ref[...].astype(BF16)
            kn_b = kn_ref[...].astype(BF16)
            kr_b = kr_ref[...].astype(BF16)
            do_b = do_ref[...]
            s = lax.dot_general(qn_b, kn_b, NT, preferred_element_type=F32)
            s += lax.dot_general(qr_b, kr_b, NT, preferred_element_type=F32)
            s = s * scale
            rows = i * tq + lax.broadcasted_iota(jnp.int32, (tq, tq), 0)
            cols = j * tq + lax.broadcasted_iota(jnp.int32, (tq, tq), 1)
            p = jnp.where(cols <= rows, jnp.exp(s - lse_ref[...]), 0.0)
            delta = jnp.sum(do_b.astype(F32) * o_ref[...].astype(F32), axis=-1, keepdims=True)
            dp = lax.dot_general(do_b, v_ref[...].astype(BF16), NT, preferred_element_type=F32)
            ds = (p * (dp - delta) * scale).astype(BF16)
            p_b = p.astype(BF16)
            dv_scr[...] += lax.dot_general(p_b, do_b, TN, preferred_element_type=F32)
            dkn_scr[...] += lax.dot_general(ds, qn_b, TN, preferred_element_type=F32)
            dkr_scr[...] += lax.dot_general(ds, qr_b, TN, preferred_element_type=F32)
            sl = pl.ds(pl.multiple_of(i * tq, tq), tq)
            dqn_ref[sl, :] += jnp.dot(ds, kn_b, preferred_element_type=F32)
            dqr_ref[sl, :] += jnp.dot(ds, kr_b, preferred_element_type=F32)

        @pl.when(i == nq - 1)
        def _():
            dkv_ref[:, :QK_NOPE] = dkn_scr[...]
            dkv_ref[:, QK_NOPE:] = dv_scr[...]
            dkr_ref[...] = dkr_scr[...]

    qi = lambda j, i: jnp.maximum(i, j)
    return pl.pallas_call(
        body, name=name, grid=(h_n, nq, nq),
        in_specs=[
            pl.BlockSpec((tq, QK_NOPE), lambda h, j, i: (qi(j, i), h)),
            pl.BlockSpec((None, tq, QK_ROPE), lambda h, j, i: (h, qi(j, i), 0)),
            pl.BlockSpec((tq, QK_NOPE), lambda h, j, i: (j, 2 * h)),
            pl.BlockSpec((tq, V_HEAD), lambda h, j, i: (j, 2 * h + 1)),
            pl.BlockSpec((tq, QK_ROPE), lambda h, j, i: (j, 0)),
            pl.BlockSpec((tq, V_HEAD), lambda h, j, i: (qi(j, i), h)),
            pl.BlockSpec((None, tq, 1), lambda h, j, i: (h, qi(j, i), 0)),
            pl.BlockSpec((tq, V_HEAD), lambda h, j, i: (qi(j, i), h)),
        ],
        out_specs=[
            pl.BlockSpec((t, QK_NOPE), lambda h, j, i: (0, h)),
            pl.BlockSpec((None, t, QK_ROPE), lambda h, j, i: (h, 0, 0)),
            pl.BlockSpec((tq, QK_NOPE + V_HEAD), lambda h, j, i: (j, h)),
            pl.BlockSpec((None, tq, QK_ROPE), lambda h, j, i: (h, j, 0)),
        ],
        out_shape=[jax.ShapeDtypeStruct((t, h_n * QK_NOPE), F32),
                   jax.ShapeDtypeStruct((h_n, t, QK_ROPE), F32),
                   jax.ShapeDtypeStruct((t, h_n * (QK_NOPE + V_HEAD)), F32),
                   jax.ShapeDtypeStruct((h_n, t, QK_ROPE), F32)],
        scratch_shapes=[pltpu.VMEM((tq, QK_NOPE), F32), pltpu.VMEM((tq, V_HEAD), F32),
                        pltpu.VMEM((tq, QK_ROPE), F32)],
        compiler_params=_cp(("parallel", "arbitrary", "arbitrary")),
    )(qn, qr, kv, kv, kr, o, lse, do)


@functools.partial(jax.custom_vjp, nondiff_argnums=(4,))
def attention(qn, qr, kv, kr, tag):
    return _attn_fwd_call(qn, qr, kv, kr, "attn_" + tag)[0]


def _attention_f(qn, qr, kv, kr, tag):
    o, lse = _attn_fwd_call(qn, qr, kv, kr, "attn_" + tag)
    return o, (qn, qr, kv, kr, o, lse)


def _attention_b(tag, res, do):
    dqn, dqr, dkv, dkr_h = _attn_bwd_call(*res, do, "attn_" + tag + "_bwd")
    return dqn, dqr, dkv, jnp.sum(dkr_h, axis=0)


attention.defvjp(_attention_f, _attention_b)


def _shift_down(u, s):
    if s == 0:
        return u
    t = u.shape[0]
    rolled = pltpu.roll(u, s, 0)
    return jnp.where(lax.broadcasted_iota(jnp.int32, u.shape, 0) >= s, rolled, 0.0)


def _shift_up(u, s):
    if s == 0:
        return u
    t = u.shape[0]
    rolled = pltpu.roll(u, t - s, 0)
    return jnp.where(lax.broadcasted_iota(jnp.int32, u.shape, 0) < t - s, rolled, 0.0)


def _conv_blocks(t, c3):
    p = c3 // 3
    tc = _tile(p, 512)
    per = p // tc
    return p, tc, per


def _conv_fwd_call(u, w, name):
    t, c3 = u.shape
    p, tc, per = _conv_blocks(t, c3)

    def body(u_ref, w_ref, o_ref):
        u = u_ref[...]
        y = jnp.zeros_like(u)
        for j in range(CONV_WIDTH):
            y = y + w_ref[j:j + 1, :] * _shift_down(u, CONV_WIDTH - 1 - j)
        o_ref[...] = y * _sigmoid(y)

    return pl.pallas_call(
        body, name=name, grid=(c3 // tc,),
        in_specs=[pl.BlockSpec((t, tc), lambda cb: (0, cb)),
                  pl.BlockSpec((CONV_WIDTH, tc), lambda cb: (0, cb))],
        out_specs=pl.BlockSpec((None, t, tc), lambda cb: (cb // per, 0, cb % per)),
        out_shape=jax.ShapeDtypeStruct((3, t, p), F32),
        compiler_params=_cp(("parallel",)),
    )(u, w)


def _conv_bwd_call(u, w, do, name):
    t, c3 = u.shape
    p, tc, per = _conv_blocks(t, c3)

    def body(u_ref, w_ref, do_ref, du_ref, dw_ref):
        u = u_ref[...]
        shifted = [_shift_down(u, CONV_WIDTH - 1 - j) for j in range(CONV_WIDTH)]
        y = jnp.zeros_like(u)
        for j in range(CONV_WIDTH):
            y = y + w_ref[j:j + 1, :] * shifted[j]
        s = _sigmoid(y)
        dy = do_ref[...] * s * (1.0 + y * (1.0 - s))
        du = jnp.zeros_like(u)
        for j in range(CONV_WIDTH):
            du = du + w_ref[j:j + 1, :] * _shift_up(dy, CONV_WIDTH - 1 - j)
            dw_ref[j:j + 1, :] = jnp.sum(dy * shifted[j], axis=0, keepdims=True)
        du_ref[...] = du

    return pl.pallas_call(
        body, name=name, grid=(c3 // tc,),
        in_specs=[pl.BlockSpec((t, tc), lambda cb: (0, cb)),
                  pl.BlockSpec((CONV_WIDTH, tc), lambda cb: (0, cb)),
                  pl.BlockSpec((None, t, tc), lambda cb: (cb // per, 0, cb % per))],
        out_specs=[pl.BlockSpec((t, tc), lambda cb: (0, cb)),
                   pl.BlockSpec((CONV_WIDTH, tc), lambda cb: (0, cb))],
        out_shape=[jax.ShapeDtypeStruct((t, c3), F32), jax.ShapeDtypeStruct((CONV_WIDTH, c3), F32)],
        compiler_params=_cp(("parallel",)),
    )(u, w, do)


@functools.partial(jax.custom_vjp, nondiff_argnums=(2,))
def conv_silu(u, w, tag):
    return _conv_fwd_call(u, w, "conv_" + tag)


def _conv_silu_f(u, w, tag):
    return _conv_fwd_call(u, w, "conv_" + tag), (u, w)


def _conv_silu_b(tag, res, do):
    return tuple(_conv_bwd_call(*res, do, "conv_" + tag + "_bwd"))


conv_silu.defvjp(_conv_silu_f, _conv_silu_b)


def _hdot(a, b, dn=NN):
    return lax.dot_general(a, b, dn, precision=HIGHEST, preferred_element_type=F32)


def _bf16_dot(a, b, dn):
    return lax.dot_general(a.astype(BF16), b.astype(BF16), dn, preferred_element_type=F32)


@functools.partial(jax.custom_vjp, nondiff_argnums=(2,))
def _bdot(a, b, dn=NN):
    return _bf16_dot(a, b, dn)


def _bdot_f(a, b, dn):
    return _bf16_dot(a, b, dn), (a, b)


def _bdot_b(dn, res, g):
    a, b = res
    if dn == NN:
        return _bf16_dot(g, b, NT), _bf16_dot(a, g, TN)
    if dn == NT:
        return _bf16_dot(g, b, NN), _bf16_dot(g, a, TN)
    return _bf16_dot(b, g, NT), _bf16_dot(a, g, NN)


_bdot.defvjp(_bdot_f, _bdot_b)


GDN_HEADS_PER_STEP = 4
GDN_HEADS_PER_STEP_BWD = 2


def _gdn_chunk(q, k, v, z, bl, al, a_log, dtb, gn, s):
    c = q.shape[0]
    ri = lax.broadcasted_iota(jnp.int32, (c, c), 0)
    ci = lax.broadcasted_iota(jnp.int32, (c, c), 1)
    lower = ri >= ci
    strict = ri > ci
    low_incl = lower.astype(F32)
    up_incl = (ri <= ci).astype(F32)
    eye = (ri == ci).astype(F32)

    q = q * lax.rsqrt(jnp.sum(q * q, axis=-1, keepdims=True) + EPS) * (GDN_DK ** -0.5)
    k = k * lax.rsqrt(jnp.sum(k * k, axis=-1, keepdims=True) + EPS)
    beta = _sigmoid(bl)
    g = -jnp.exp(a_log) * _softplus(al + dtb)
    g_w = jnp.broadcast_to(g, (c, LANE))
    gc = _hdot(low_incl, g_w)
    gr = _hdot(g_w[:, :c], up_incl, TN)
    diff = gc[:, :c] - gr
    decay = jnp.where(lower, jnp.exp(jnp.where(lower, diff, 0.0)), 0.0)
    kb = k * beta
    lmat = jnp.where(strict, _bdot(kb, k, NT) * decay, 0.0)
    inv = eye - lmat
    pw = lmat
    for _ in range(int(math.log2(c)) - 1):
        pw = _hdot(pw, pw)
        inv = _hdot(inv, eye + pw)
    eg = jnp.exp(gc)
    u = _hdot(inv, v * beta)
    w = _hdot(inv, kb * eg)
    attn = jnp.where(lower, _bdot(q, k, NT) * decay, 0.0)
    v_new = u - _bdot(w, s)
    o = _bdot(q * eg, s) + _bdot(attn, v_new)
    g_last = jnp.sum(g_w, axis=0, keepdims=True)
    k_dec = k * jnp.exp(g_last - gc)
    s_new = s * jnp.exp(g_last) + _bdot(k_dec, v_new, TN)
    on = o * lax.rsqrt(jnp.mean(o * o, axis=-1, keepdims=True) + EPS) * gn
    return on * (z * _sigmoid(z)), s_new


def _gdn_specs(n_chunks, hb, rev):
    c = CHUNK
    nn = (lambda n: n_chunks - 1 - n) if rev else (lambda n: n)
    plane = lambda pidx: pl.BlockSpec((None, c, hb * GDN_DK), lambda hg, n: (pidx, nn(n), hg))
    col = pl.BlockSpec((hb, c, 1), lambda hg, n: (hg, nn(n), 0))
    scal = pl.BlockSpec((hb, 1, 1), lambda hg, n: (hg, 0, 0))
    zspec = pl.BlockSpec((c, hb * GDN_DV), lambda hg, n: (nn(n), hg))
    gnspec = pl.BlockSpec((1, GDN_DV), lambda hg, n: (0, 0))
    sspec = pl.BlockSpec((hb, None, GDN_DK, GDN_DV), lambda hg, n: (hg, nn(n), 0, 0))
    return plane, col, scal, zspec, gnspec, sspec


def _heads_per_step(want):
    return want if GDN_HEADS % want == 0 else 1


def _gdn_fwd_call(qkv, z, bl, al, a_log, dtb, gn, name):
    t = z.shape[0]
    h_n = GDN_HEADS
    hb = _heads_per_step(GDN_HEADS_PER_STEP)
    n_chunks = t // CHUNK
    plane, col, scal, zspec, gnspec, sspec = _gdn_specs(n_chunks, hb, False)

    def body(q_ref, k_ref, v_ref, z_ref, bl_ref, al_ref, a_ref, dtb_ref, gn_ref, o_ref, sall_ref, s_scr):
        n = pl.program_id(1)

        @pl.when(n == 0)
        def _():
            s_scr[...] = jnp.zeros_like(s_scr)

        for j in range(hb):
            hs = slice(j * GDN_DK, (j + 1) * GDN_DK)
            s = s_scr[j]
            sall_ref[j] = s
            o, s_new = _gdn_chunk(q_ref[:, hs], k_ref[:, hs], v_ref[:, hs], z_ref[:, hs], bl_ref[j],
                                  al_ref[j], a_ref[j], dtb_ref[j], gn_ref[...], s)
            o_ref[:, hs] = o.astype(o_ref.dtype)
            s_scr[j] = s_new

    return pl.pallas_call(
        body, name=name, grid=(h_n // hb, n_chunks),
        in_specs=[plane(0), plane(1), plane(2), zspec, col, col, scal, scal, gnspec],
        out_specs=[zspec, sspec],
        out_shape=[jax.ShapeDtypeStruct((t, h_n * GDN_DV), BF16),
                   jax.ShapeDtypeStruct((h_n, n_chunks, GDN_DK, GDN_DV), F32)],
        scratch_shapes=[pltpu.VMEM((hb, GDN_DK, GDN_DV), F32)],
        compiler_params=_cp(("parallel", "arbitrary")),
    )(qkv, qkv, qkv, z, bl, al, a_log, dtb, gn)


def _gdn_bwd_call(qkv, z, bl, al, a_log, dtb, gn, sall, do, name):
    t = z.shape[0]
    h_n = GDN_HEADS
    hb = _heads_per_step(GDN_HEADS_PER_STEP_BWD)
    n_chunks = t // CHUNK
    c = CHUNK
    plane, col, scal, zspec, gnspec, sspec = _gdn_specs(n_chunks, hb, True)
    dplanes = pl.BlockSpec((3, c, hb * GDN_DK), lambda hg, n: (0, n_chunks - 1 - n, hg))
    gnh = pl.BlockSpec((None, 1, GDN_DV), lambda hg, n: (hg, 0, 0))

    def body(q_ref, k_ref, v_ref, z_ref, bl_ref, al_ref, a_ref, dtb_ref, gn_ref, s_ref, do_ref,
             dqkv_ref, dz_ref, dbl_ref, dal_ref, da_ref, ddtb_ref, dgn_ref, ds_scr):
        n = pl.program_id(1)

        @pl.when(n == 0)
        def _():
            ds_scr[...] = jnp.zeros_like(ds_scr)
            da_ref[...] = jnp.zeros_like(da_ref)
            ddtb_ref[...] = jnp.zeros_like(ddtb_ref)
            dgn_ref[...] = jnp.zeros_like(dgn_ref)

        for j in range(hb):
            hs = slice(j * GDN_DK, (j + 1) * GDN_DK)
            _, vjp = jax.vjp(_gdn_chunk, q_ref[:, hs], k_ref[:, hs], v_ref[:, hs], z_ref[:, hs], bl_ref[j],
                             al_ref[j], a_ref[j], dtb_ref[j], gn_ref[...], s_ref[j])
            dq, dk, dv, dz, dbl, dal, da, ddtb, dgn, ds = vjp((do_ref[:, hs].astype(F32), ds_scr[j]))
            dqkv_ref[0, :, hs] = dq
            dqkv_ref[1, :, hs] = dk
            dqkv_ref[2, :, hs] = dv
            dz_ref[:, hs] = dz
            dbl_ref[j] = dbl
            dal_ref[j] = dal
            da_ref[j] += da
            ddtb_ref[j] += ddtb
            dgn_ref[...] += dgn
            ds_scr[j] = ds

    return pl.pallas_call(
        body, name=name, grid=(h_n // hb, n_chunks),
        in_specs=[plane(0), plane(1), plane(2), zspec, col, col, scal, scal, gnspec, sspec, zspec],
        out_specs=[dplanes, zspec, col, col, scal, scal, gnh],
        out_shape=[jax.ShapeDtypeStruct((3, t, h_n * GDN_DK), F32),
                   jax.ShapeDtypeStruct((t, h_n * GDN_DV), F32),
                   jax.ShapeDtypeStruct((h_n, t, 1), F32), jax.ShapeDtypeStruct((h_n, t, 1), F32),
                   jax.ShapeDtypeStruct((h_n, 1, 1), F32), jax.ShapeDtypeStruct((h_n, 1, 1), F32),
                   jax.ShapeDtypeStruct((h_n // hb, 1, GDN_DV), F32)],
        scratch_shapes=[pltpu.VMEM((hb, GDN_DK, GDN_DV), F32)],
        compiler_params=_cp(("parallel", "arbitrary")),
    )(qkv, qkv, qkv, z, bl, al, a_log, dtb, gn, sall, do)


@functools.partial(jax.custom_vjp, nondiff_argnums=(7,))
def gdn(qkv, z, bl, al, a_log, dtb, gn, tag):
    return _gdn_fwd_call(qkv, z, bl, al, a_log, dtb, gn, "gdn_" + tag)[0]


def _gdn_f(qkv, z, bl, al, a_log, dtb, gn, tag):
    o, sall = _gdn_fwd_call(qkv, z, bl, al, a_log, dtb, gn, "gdn_" + tag)
    return o, (qkv, z, bl, al, a_log, dtb, gn, sall)


def _gdn_b(tag, res, do):
    dqkv, dz, dbl, dal, da, ddtb, dgn_h = _gdn_bwd_call(*res, do, "gdn_" + tag + "_bwd")
    return dqkv, dz, dbl, dal, da, ddtb, jnp.sum(dgn_h, axis=0)


gdn.defvjp(_gdn_f, _gdn_b)


def adamw(w, parts, m, v, name):
    n_layers = len(parts)
    n_parts, r, c = parts[0].shape
    assert w.shape == (n_layers * r, c), (w.shape, parts[0].shape)
    tr = r
    for cand in (512, 256, 128, 64, 32, 16, 8):
        if r % cand == 0 and cand * c <= 256 * 1024:
            tr = cand
            break
    nb = r // tr
    blk = pl.BlockSpec((tr, c), lambda l, i: (l * nb + i, 0))
    bc1 = 1.0 - ADAM_B1 ** ADAM_STEP
    bc2 = 1.0 - ADAM_B2 ** ADAM_STEP

    def part_spec(li):
        return pl.BlockSpec((n_parts, tr, c),
                            lambda l, i: (0, jnp.where(l == li, i, jnp.where(l < li, 0, nb - 1)), 0))

    def body(*refs):
        w_ref, p_refs = refs[0], refs[1:1 + n_layers]
        m_ref, v_ref, g_ref, d_ref, mo_ref, vo_ref = refs[1 + n_layers:]
        for li in range(n_layers):
            @pl.when(pl.program_id(0) == li)
            def _(p_ref=p_refs[li]):
                g = p_ref[0].astype(F32)
                for i in range(1, n_parts):
                    g = g + p_ref[i].astype(F32)
                m2 = ADAM_B1 * m_ref[...] + (1.0 - ADAM_B1) * g
                v2 = ADAM_B2 * v_ref[...] + (1.0 - ADAM_B2) * (g * g)
                g_ref[...] = g
                mo_ref[...] = m2
                vo_ref[...] = v2
                d_ref[...] = -ADAM_LR * ((m2 / bc1) / (jnp.sqrt(v2 / bc2) + ADAM_EPS)
                                         + ADAM_WD * w_ref[...])

    return pl.pallas_call(
        body, name=name, grid=(n_layers, nb),
        in_specs=[blk] + [part_spec(li) for li in range(n_layers)] + [blk, blk],
        out_specs=[blk] * 4, out_shape=[jax.ShapeDtypeStruct(w.shape, F32)] * 4,
        compiler_params=_cp(("arbitrary", "arbitrary")),
    )(w, *parts, m, v)


_HBM = pl.BlockSpec(memory_space=pltpu.HBM)
_SEM = pl.BlockSpec(memory_space=pltpu.SEMAPHORE)
_EFFECT = pltpu.SideEffectType.DATAFLOW_SIDE_EFFECTING


def _peer(x, y, c, d):
    px = 1 - x if d & 4 else x
    py = 1 - y if d & 2 else y
    pc = 1 - c if d & 1 else c
    return (px, py, pc), 4 * px + 2 * py + pc


def copy_start(arrays, scatter, carry, name):
    n = len(arrays)
    lands = [lax.empty(a.shape if scatter else (N_DEV,) + a.shape, a.dtype) for a in arrays]

    def body(*refs):
        srcs, dsts = refs[:n], refs[n:2 * n]
        sems = refs[2 * n + 1:4 * n + 1]
        x, y, c = (lax.axis_index(a) for a in AXES)
        me = 4 * x + 2 * y + c
        for k in range(n):
            for d in range(1, N_DEV):
                peer, pid = _peer(x, y, c, d)
                pltpu.make_async_remote_copy(
                    src_ref=srcs[k].at[pid] if scatter else srcs[k], dst_ref=dsts[k].at[me],
                    send_sem=sems[2 * k], recv_sem=sems[2 * k + 1],
                    device_id=peer, device_id_type=pl.DeviceIdType.MESH).start()

    operands = list(arrays) + lands + [carry]
    outs = pl.pallas_call(
        body, name=name,
        out_shape=tuple([pltpu.SemaphoreType.DMA(())] * (2 * n)
                        + [pltpu.HBM(a.shape, a.dtype) for a in operands]),
        in_specs=[_HBM] * (2 * n + 1),
        out_specs=tuple([_SEM] * (2 * n) + [_HBM] * (2 * n + 1)),
        input_output_aliases={i: 2 * n + i for i in range(2 * n + 1)},
        compiler_params=pltpu.CompilerParams(has_side_effects=_EFFECT),
    )(*[pltpu.with_memory_space_constraint(a, pltpu.HBM) for a in operands])
    sems, thru = outs[:2 * n], outs[2 * n:4 * n]
    handles = [(sems[2 * k], sems[2 * k + 1], thru[k], thru[n + k]) for k in range(n)]
    return outs[-1], handles


def copy_wait(handles, after, name):
    n = len(handles)
    sems = [s for h in handles for s in h[:2]]
    srcs = [h[2] for h in handles]
    lands = [h[3] for h in handles]

    def body(*refs):
        dsts = refs[n:2 * n]
        sem_refs = refs[2 * n:4 * n]
        x, y, c = (lax.axis_index(a) for a in AXES)
        for k in range(n):
            seven = dsts[k].at[pl.ds(0, N_DEV - 1)]
            pltpu.make_async_remote_copy(
                src_ref=seven, dst_ref=seven, send_sem=sem_refs[2 * k], recv_sem=sem_refs[2 * k + 1],
                device_id=(x, y, c), device_id_type=pl.DeviceIdType.MESH).wait()

    outs = pl.pallas_call(
        body, name=name,
        out_shape=tuple([pltpu.HBM(a.shape, a.dtype) for a in srcs + lands]),
        in_specs=[_HBM] * (2 * n) + [_SEM] * (2 * n) + [pl.BlockSpec(memory_space=pl.ANY)],
        out_specs=tuple([_HBM] * (2 * n)),
        input_output_aliases={i: i for i in range(2 * n)},
        compiler_params=pltpu.CompilerParams(has_side_effects=_EFFECT),
    )(*srcs, *lands, *sems, after)
    return list(outs[:n]), list(outs[n:])


def exchange(arrays, modes, name):
    n = len(arrays)
    hbm = pl.BlockSpec(memory_space=pltpu.HBM)
    out_shape = [jax.ShapeDtypeStruct(a.shape if md == "scatter" else (N_DEV,) + a.shape, a.dtype)
                 for a, md in zip(arrays, modes)]

    def body(*refs):
        ins, outs = refs[:n], refs[n:2 * n]
        send_sems, recv_sems, local_sems = refs[2 * n:]
        x, y, c = (lax.axis_index(a) for a in AXES)
        me = 4 * x + 2 * y + c

        def src(k, p):
            return ins[k].at[p] if modes[k] == "scatter" else ins[k]

        local = [pltpu.make_async_copy(src(k, me), outs[k].at[me], local_sems.at[k]) for k in range(n)]
        for cp in local:
            cp.start()
        started = []
        for d in range(1, N_DEV):
            px = 1 - x if d & 4 else x
            py = 1 - y if d & 2 else y
            pc = 1 - c if d & 1 else c
            pid = 4 * px + 2 * py + pc
            for k in range(n):
                pltpu.make_async_remote_copy(
                    src_ref=src(k, pid), dst_ref=outs[k].at[me],
                    send_sem=send_sems.at[k, d - 1], recv_sem=recv_sems.at[k, d - 1],
                    device_id=(px, py, pc), device_id_type=pl.DeviceIdType.MESH).start()
                started.append((k, d, pid, (px, py, pc)))
        for k, d, pid, peer in started:
            pltpu.make_async_remote_copy(
                src_ref=src(k, pid), dst_ref=outs[k].at[pid],
                send_sem=send_sems.at[k, d - 1], recv_sem=recv_sems.at[k, d - 1],
                device_id=peer, device_id_type=pl.DeviceIdType.MESH).wait()
        for cp in local:
            cp.wait()

    outs = pl.pallas_call(
        body, name=name, in_specs=[hbm] * n, out_specs=[hbm] * n, out_shape=out_shape,
        scratch_shapes=[pltpu.SemaphoreType.DMA((n, N_DEV - 1)), pltpu.SemaphoreType.DMA((n, N_DEV - 1)),
                        pltpu.SemaphoreType.DMA((n,))],
        compiler_params=pltpu.CompilerParams(has_side_effects=True),
    )(*arrays)
    return list(outs)


BIG = ("w_in", "w_uq", "w_ukv", "w_o_mla", "w_o_gdn", "w_o", "w_gate_up", "w_down")
ROW_SHARDED = ("w_o", "w_down")
SMALL = ("b_ada", "norm_mix", "norm_ffn", "q_a_norm", "kv_a_norm", "A_log", "dt_bias", "gdn_norm",
         "final_norm")
WEIGHTS = ("w_ada", "b_ada", "norm_mix", "norm_ffn", "w_in", "q_a_norm", "kv_a_norm", "w_uq", "w_ukv",
           "w_o_mla", "conv_w", "A_log", "dt_bias", "gdn_norm", "w_o_gdn", "w_o", "w_gate_up", "w_down",
           "final_norm")


def _unslot(g):
    return g.transpose(1, 0, 2).reshape(g.shape[1], -1)


def _cols(g):
    return g if g.shape[-1] % LANE == 0 else _unslot(g)


def _stack_rows(g):
    return g.reshape(-1, g.shape[-1])


def _pad_cols(a):
    return jnp.pad(a, ((0, 0), (0, _pad_lanes(a.shape[1]) - a.shape[1])))


def _rope(xv, cos, sin):
    x1, x2 = jnp.split(xv, 2, axis=-1)
    return jnp.concatenate([x1 * cos - x2 * sin, x2 * cos + x1 * sin], axis=-1)


MIX_WEIGHTS = ("w_uq", "w_ukv", "w_o_mla", "w_o_gdn", "w_o")
FFN_WEIGHTS = ("w_gate_up", "w_down")


def _stage_in(x, mod, nm, w_in_s, tg):
    d = x.shape[1]
    hg = GDN_HEADS
    w_in = _unslot(w_in_s)
    o1 = Q_LORA + KV_LORA + QK_ROPE
    o2 = o1 + 2 * hg * GDN_DK + hg * GDN_DV
    o3 = o2 + hg * GDN_DV
    o4 = o3 + 2 * hg
    h = ada_norm(x, nm, mod[:, d:2 * d], mod[:, :d], "mix" + tg)
    return (mm(h, _pad_cols(w_in[:, :o1]), "in_a" + tg, F32), mm(h, w_in[:, o1:o2], "in_qkv" + tg, F32),
            mm(h, w_in[:, o2:o3], "in_z" + tg, F32), mm(h, _pad_cols(w_in[:, o3:o4]), "in_ba" + tg, F32),
            mm(h, w_in[:, o4:o4 + 2 * d], "in_g" + tg, F32))


def _stage_mix(x, mod, seg_a, qkv, z, ba, gl, w_uq_s, w_ukv_s, w_o_mla_s, w_o_gdn_s, w_o_s, conv_s,
               qan, kvan, a_log, dtb, gn, cos, sin, tg):
    t, d = x.shape
    hq, hg = MLA_HEADS, GDN_HEADS
    w_uq = _unslot(w_uq_s).reshape(Q_LORA, hq, QK_NOPE + QK_ROPE)
    w_uq = jnp.concatenate([w_uq[:, :, :QK_NOPE].reshape(Q_LORA, hq * QK_NOPE),
                            w_uq[:, :, QK_NOPE:].reshape(Q_LORA, hq * QK_ROPE)], axis=1)
    c_q = seg_a[:, :Q_LORA]
    c_kv = seg_a[:, Q_LORA:Q_LORA + KV_LORA]
    k_pe = seg_a[:, Q_LORA + KV_LORA:Q_LORA + KV_LORA + QK_ROPE]
    qf = mm(rms_norm(c_q, qan, "qa" + tg), w_uq, "uq" + tg, F32)
    kvf = mm(rms_norm(c_kv, kvan, "kva" + tg), _cols(w_ukv_s), "ukv" + tg, F32)
    qn = qf[:, :hq * QK_NOPE]
    q_pe = qf[:, hq * QK_NOPE:].reshape(t, hq, QK_ROPE)
    qr = _rope(q_pe, cos[:, None, :], sin[:, None, :]).transpose(1, 0, 2)
    kr = _rope(k_pe, cos, sin)
    y_a = mm(attention(qn, qr, kvf, kr, tg), _cols(w_o_mla_s), "o_mla" + tg, F32)
    conv_w = conv_s.transpose(1, 0, 2).reshape(CONV_WIDTH, -1)
    qkv_c = conv_silu(qkv, conv_w, tg)
    bl = ba[:, :hg].T[:, :, None]
    al = ba[:, hg:2 * hg].T[:, :, None]
    o_gdn = gdn(qkv_c, z, bl, al, a_log.reshape(hg, 1, 1), dtb.reshape(hg, 1, 1), gn, tg)
    y_b = mm(o_gdn, _cols(w_o_gdn_s), "o_gdn" + tg, F32)
    mix = mm(gate_mix(gl, y_a, y_b, tg), _stack_rows(w_o_s), "w_o" + tg, F32)
    return resid(x, mod[:, 2 * d:3 * d], mix, "mix" + tg)


def _stage_ffn(x, mod, nf, w_gu_s, w_down_s, tg):
    d = x.shape[1]
    h = ada_norm(x, nf, mod[:, 4 * d:5 * d], mod[:, 3 * d:4 * d], "ffn" + tg)
    gu = mm(h, _cols(w_gu_s), "gu" + tg, F32)
    dn = mm(swiglu(gu, tg), _stack_rows(w_down_s), "down" + tg, F32)
    return resid(x, mod[:, 5 * d:6 * d], dn, "ffn" + tg)


def _flat_row(arrs):
    v = jnp.concatenate([a.reshape(-1) for a in arrs])
    return jnp.pad(v, (0, _pad_lanes(v.shape[0]) - v.shape[0]))[None, :]


def kernel(x, c, positions, w_ada, b_ada, norm_mix, norm_ffn, w_in, q_a_norm, kv_a_norm, w_uq, w_ukv, w_o_mla, conv_w, A_log, dt_bias, gdn_norm, w_o_gdn, w_o, w_gate_up, w_down, final_norm, loss_target, m_w_ada, m_b_ada, m_norm_mix, m_norm_ffn, m_w_in, m_q_a_norm, m_kv_a_norm, m_w_uq, m_w_ukv, m_w_o_mla, m_conv_w, m_A_log, m_dt_bias, m_gdn_norm, m_w_o_gdn, m_w_o, m_w_gate_up, m_w_down, m_final_norm, v_w_ada, v_b_ada, v_norm_mix, v_norm_ffn, v_w_in, v_q_a_norm, v_kv_a_norm, v_w_uq, v_w_ukv, v_w_o_mla, v_conv_w, v_A_log, v_dt_bias, v_gdn_norm, v_w_o_gdn, v_w_o, v_w_gate_up, v_w_down, v_final_norm):
    given = dict(locals())
    t, d = x.shape[1], x.shape[2]
    n_ada = w_ada.shape[2]
    me = 4 * lax.axis_index("x") + 2 * lax.axis_index("y") + lax.axis_index("c")

    def with_own(land, own):
        return lax.dynamic_update_slice(land, own[None], (me,) + (0,) * own.ndim)

    got = exchange([c, conv_w], ["gather", "gather"], "gather_small")
    c_all, conv_g = got[0].reshape(N_DEV, d), got[1]
    c_rows = jnp.pad(c_all, ((0, 16 - N_DEV), (0, 0)))
    mod_cols = jnp.stack([_mm(c_rows, w_ada[l], "nn", F32, "ada_mod%d" % l, a_act="silu")[:N_DEV]
                          for l in range(DEPTH)], axis=1)
    mod_mine = exchange([mod_cols], ["scatter"], "scatter_mod")[0]
    mods = mod_mine.transpose(1, 0, 2).reshape(DEPTH, N_DEV * n_ada) + b_ada

    keys = [(n, l) for l in range(DEPTH) for n in BIG]
    mods, handles = copy_start([given[n][l].astype(BF16) for n, l in keys], False, mods, "gather_start")
    handles = dict(zip(keys, handles))

    def landed(ks, after, name):
        srcs, lands = copy_wait([handles[k] for k in ks], after, name)
        return [with_own(land, src) for src, land in zip(srcs, lands)]

    inv_freq = 1.0 / (ROPE_THETA ** (jnp.arange(0, QK_ROPE, 2, dtype=F32) / QK_ROPE))
    ang = positions[0].astype(F32)[:, None] * inv_freq
    cos, sin = jnp.cos(ang), jnp.sin(ang)
    xl = x[0]
    after = mods
    vjps = []
    for l in range(DEPTH):
        tg = str(l)
        mod = mods[l:l + 1]
        (w_in_s,) = landed([("w_in", l)], after, "wait_in" + tg)
        seg, vjp_in = jax.vjp(lambda *a, tg=tg: _stage_in(*a, tg), xl, mod, norm_mix[l:l + 1], w_in_s)
        w_mix = landed([(n, l) for n in MIX_WEIGHTS], seg[0], "wait_mix" + tg)
        xm, vjp_mix = jax.vjp(lambda *a, tg=tg: _stage_mix(*a, cos, sin, tg), xl, mod, *seg, *w_mix,
                              conv_g[:, l], q_a_norm[l:l + 1], kv_a_norm[l:l + 1], A_log[l], dt_bias[l],
                              gdn_norm[l:l + 1])
        w_ffn = landed([(n, l) for n in FFN_WEIGHTS], xm, "wait_ffn" + tg)
        xl, vjp_ffn = jax.vjp(lambda *a, tg=tg: _stage_ffn(*a, tg), xm, mod, norm_ffn[l:l + 1], *w_ffn)
        after = xl
        vjps.append((vjp_in, vjp_mix, vjp_ffn))

    loss_t, g, dfn = loss_head(xl, final_norm[None, :], loss_target[0])
    loss = lax.psum(loss_t[0, 0], AXES)
    dsmall = {n: [None] * DEPTH for n in SMALL + ("conv_w",)}
    dmods = [None] * DEPTH
    sent = {}

    def send(ks, grads, carry, name):
        carry, hs = copy_start(list(grads), True, carry, name)
        sent.update(zip(ks, hs))
        return carry

    for l in reversed(range(DEPTH)):
        tg = str(l)
        vjp_in, vjp_mix, vjp_ffn = vjps[l]
        dxm, dmod_f, dsmall["norm_ffn"][l], *dw = vjp_ffn(g)
        dxm = send([(n, l) for n in FFN_WEIGHTS], dw, dxm, "scatter_ffn" + tg)
        dx_m, dmod_m, *rest = vjp_mix(dxm)
        dseg, dw, rest = rest[:5], rest[5:5 + len(MIX_WEIGHTS)], rest[5 + len(MIX_WEIGHTS):]
        dseg[0] = send([(n, l) for n in MIX_WEIGHTS], dw, dseg[0], "scatter_mix" + tg)
        for n, gr in zip(("conv_w", "q_a_norm", "kv_a_norm", "A_log", "dt_bias", "gdn_norm"), rest):
            dsmall[n][l] = gr
        dx_i, dmod_i, dsmall["norm_mix"][l], dw_in = vjp_in(tuple(dseg))
        g = dx_i + dx_m
        if l > 0:
            g = send([("w_in", l)], [dw_in], g, "scatter_in" + tg)
        dmods[l] = dmod_f + dmod_m + dmod_i
    dx = g
    dmods = jnp.concatenate(dmods, axis=0)
    dconv = jnp.stack(dsmall.pop("conv_w"), axis=1)
    dsmall = {n: jnp.concatenate(v, axis=0) if v[0].ndim == 2 else jnp.stack(v)
              for n, v in dsmall.items() if v[0] is not None}
    dsmall["b_ada"] = dmods
    dsmall["final_norm"] = dfn[0]

    dmod_cols = dmods.reshape(DEPTH, N_DEV, n_ada).transpose(1, 0, 2)
    conv_parts, dmod_all, small_parts = exchange(
        [dconv, dmod_cols, _flat_row([dsmall[n] for n in SMALL])], ["scatter", "scatter", "gather"],
        "exchange_small")
    dmod_all = send([("w_in", 0)], [dw_in], dmod_all, "scatter_in0")

    res = {}
    dm_rows = jnp.pad(dmod_all, ((0, 16 - N_DEV), (0, 0), (0, 0)))
    g_ada = jnp.stack([_mm(c_rows, dm_rows[:, l], "tn", F32, "ada_dw%d" % l, a_act="silu")
                       for l in range(DEPTH)])
    r2 = (DEPTH * d, n_ada)
    outs = adamw(w_ada.reshape(r2), [g_ada.reshape((1,) + r2)], m_w_ada.reshape(r2), v_w_ada.reshape(r2),
                 "adamw_w_ada")
    res["w_ada"] = [o.reshape(w_ada.shape) for o in outs]
    packed = SMALL + ("conv_w",)
    p_all = jnp.concatenate([small_parts, conv_parts.reshape(N_DEV, 1, -1)], axis=2)
    pack = lambda pre: jnp.concatenate([_flat_row([given[pre + n] for n in SMALL]),
                                        given[pre + "conv_w"].reshape(1, -1)], axis=1)
    outs = adamw(pack(""), [p_all], pack("m_"), pack("v_"), "adamw_small")
    done = [res["w_ada"][1], outs[1]]
    for group, gname in ((FFN_WEIGHTS, "ffn"), (MIX_WEIGHTS, "mix"), (("w_in",), "in")):
        ks = [(n, l) for l in reversed(range(DEPTH)) for n in group]
        after = sum(lax.slice(a, (0,) * a.ndim, (1,) * a.ndim).reshape(1, 1) for a in done)
        srcs, lands = copy_wait([sent[k] for k in ks], after, "scatter_wait_" + gname)
        parts = {k: with_own(land, lax.dynamic_index_in_dim(src, me, 0, keepdims=False))
                 for k, src, land in zip(ks, srcs, lands)}
        for n in group:
            w = given[n]
            r2 = (w.shape[0] * w.shape[1], w.shape[2])
            res[n] = [o.reshape(w.shape) for o in
                      adamw(w.reshape(r2), [parts[(n, l)] for l in range(DEPTH)], given["m_" + n].reshape(r2),
                            given["v_" + n].reshape(r2), "adamw_" + n)]
            done.append(res[n][1])
    off = 0
    for n in packed:
        if n == "conv_w":
            off = small_parts.shape[2]
        size = math.prod(given[n].shape)
        res[n] = [o[0, off:off + size].reshape(given[n].shape) for o in outs]
        off += size

    return (loss, dx[None]) + tuple(res[n][i] for i in range(4) for n in WEIGHTS)
```

```python
import functools
import math

import jax
import jax.numpy as jnp
from jax import lax
from jax.experimental import pallas as pl
from jax.experimental.pallas import tpu as pltpu

F32 = jnp.float32
BF16 = jnp.bfloat16

MLA_HEADS = 8
QK_NOPE = 128
QK_ROPE = 64
V_HEAD = 128
Q_LORA = 512
KV_LORA = 512
ROPE_THETA = 10000.0
GDN_HEADS = 8
GDN_DK = 128
GDN_DV = 128
CONV_WIDTH = 4
CHUNK = 64
DEPTH = 2
EPS = 1e-6
ADAM_LR = 0.001
ADAM_B1 = 0.9
ADAM_B2 = 0.999
ADAM_EPS = 1e-08
ADAM_WD = 0.01
ADAM_STEP = 10

N_DEV = 8
AXES = ("x", "y", "c")
LANE = 128
VMEM_LIMIT = 48 * 1024 * 1024
MM_VMEM_BUDGET = 36 * 1024 * 1024
HIGHEST = lax.Precision.HIGHEST

NN = (((1,), (0,)), ((), ()))
NT = (((1,), (1,)), ((), ()))
TN = (((0,), (0,)), ((), ()))


def _cp(sem=None):
    return pltpu.CompilerParams(dimension_semantics=sem, vmem_limit_bytes=VMEM_LIMIT)


def _tile(n, cap):
    if n <= cap:
        return n
    for t in range(cap - cap % LANE, 0, -LANE):
        if n % t == 0:
            return t
    return n


def _rows(t, cap=256):
    return cap if t % cap == 0 else t


def _pad_lanes(n):
    return -(-n // LANE) * LANE


def _sigmoid(x):
    return 1.0 / (1.0 + jnp.exp(-x))


def _softplus(x):
    return jnp.maximum(x, 0.0) + jnp.log(1.0 + jnp.exp(-jnp.abs(x)))


def _tile_slot(n, cap):
    t = _tile(n, cap)
    return n if t < 256 < n <= 1536 else t


def _mm(a, b, dims, out_dtype, name, a_act=None, slots=False):
    if dims == "nn":
        m, k = a.shape
        n = b.shape[-1] * (N_DEV if slots else 1)
    elif dims == "nt":
        m, k = a.shape
        n = b.shape[-2]
    else:
        k, m = a.shape
        n = b.shape[-1]
    tm = _tile(m, 1024)
    tn = _tile_slot(n // N_DEV, 512) if slots and dims != "nt" else _tile(n, 512)
    k_slot = k // N_DEV if slots and dims == "nt" else k

    def vmem_bytes(tk_):
        a_b, b_b = tm * tk_ * a.dtype.itemsize, tk_ * tn * b.dtype.itemsize
        casts = (tm * tk_ * 2 if a.dtype != BF16 else 0) + (tk_ * tn * 2 if b.dtype != BF16 else 0)
        return 2 * (a_b + b_b + tm * tn * jnp.dtype(out_dtype).itemsize) + 2 * tm * tn * 4 + casts

    tk = _tile_slot(k_slot, 1536) if slots and dims == "nt" else _tile(k, 2048)
    while vmem_bytes(tk) > MM_VMEM_BUDGET and tk % (2 * LANE) == 0:
        tk //= 2
    nk = k // tk
    per_n = (n // N_DEV) // tn if slots else 1
    per_k = (k // N_DEV) // tk if slots else 1
    if dims == "tn":
        a_spec = pl.BlockSpec((tk, tm), lambda i, j, kk: (kk, i))
    else:
        a_spec = pl.BlockSpec((tm, tk), lambda i, j, kk: (i, kk))
    if dims == "nt":
        if slots:
            b_spec = pl.BlockSpec((None, tn, tk), lambda i, j, kk: (kk // per_k, j, kk % per_k))
        else:
            b_spec = pl.BlockSpec((tn, tk), lambda i, j, kk: (j, kk))
    elif dims == "nn" and slots:
        b_spec = pl.BlockSpec((None, tk, tn), lambda i, j, kk: (j // per_n, kk, j % per_n))
    else:
        b_spec = pl.BlockSpec((tk, tn), lambda i, j, kk: (kk, j))
    if dims == "tn" and slots:
        out_spec = pl.BlockSpec((None, tm, tn), lambda i, j, kk: (j // per_n, i, j % per_n))
        out_shape = jax.ShapeDtypeStruct((N_DEV, m, n // N_DEV), out_dtype)
    else:
        out_spec = pl.BlockSpec((tm, tn), lambda i, j, kk: (i, j))
        out_shape = jax.ShapeDtypeStruct((m, n), out_dtype)
    dn = {"nn": NN, "nt": NT, "tn": TN}[dims]

    def product(a_ref, b_ref):
        av = a_ref[...]
        if a_act == "silu":
            av = av * _sigmoid(av)
        return lax.dot_general(av.astype(BF16), b_ref[...].astype(BF16), dn, preferred_element_type=F32)

    def body_one(a_ref, b_ref, o_ref):
        o_ref[...] = product(a_ref, b_ref).astype(o_ref.dtype)

    def body_acc(a_ref, b_ref, o_ref, acc_ref):
        kk = pl.program_id(2)

        @pl.when(kk == 0)
        def _():
            acc_ref[...] = jnp.zeros_like(acc_ref)

        acc_ref[...] += product(a_ref, b_ref)

        @pl.when(kk == nk - 1)
        def _():
            o_ref[...] = acc_ref[...].astype(o_ref.dtype)

    return pl.pallas_call(
        body_one if nk == 1 else body_acc, name=name, grid=(m // tm, n // tn, nk),
        in_specs=[a_spec, b_spec], out_specs=out_spec, out_shape=out_shape,
        scratch_shapes=[] if nk == 1 else [pltpu.VMEM((tm, tn), F32)],
        compiler_params=_cp(("parallel", "parallel", "arbitrary")),
    )(a, b)


@functools.partial(jax.custom_vjp, nondiff_argnums=(2, 3))
def mm(a, b, tag, out_dtype):
    return _mm(a, b, "nn", out_dtype, "mm_" + tag, slots=b.ndim == 3)


def _mm_f(a, b, tag, out_dtype):
    return mm(a, b, tag, out_dtype), (a, b)


def _mm_b(tag, out_dtype, res, g):
    a, b = res
    slots = b.ndim == 3
    da = _mm(g, b, "nt", a.dtype, "mm_" + tag + "_da", slots=slots)
    db = _mm(a, g, "tn", b.dtype, "mm_" + tag + "_db", slots=slots)
    return da, db


mm.defvjp(_mm_f, _mm_b)


def _norm_fwd_call(x, nw, sc, sh, name):
    t, d = x.shape
    tr = _rows(t)
    mod = sc is not None
    row = pl.BlockSpec((tr, d), lambda i: (i, 0))
    vec = pl.BlockSpec((1, d), lambda i: (0, 0))

    def body(*refs):
        if mod:
            x_ref, nw_ref, sc_ref, sh_ref, o_ref = refs
        else:
            x_ref, nw_ref, o_ref = refs
        xv = x_ref[...]
        r = lax.rsqrt(jnp.mean(xv * xv, axis=-1, keepdims=True) + EPS)
        y = (xv * r) * nw_ref[...]
        if mod:
            y = y * (1.0 + sc_ref[...]) + sh_ref[...]
        o_ref[...] = y.astype(o_ref.dtype)

    args = (x, nw, sc, sh) if mod else (x, nw)
    return pl.pallas_call(
        body, name=name, grid=(t // tr,),
        in_specs=[row] + [vec] * (len(args) - 1), out_specs=row,
        out_shape=jax.ShapeDtypeStruct((t, d), BF16),
        compiler_params=_cp(("parallel",)),
    )(*args)


def _norm_bwd_call(x, nw, sc, dh, name):
    t, d = x.shape
    tr = _rows(t)
    mod = sc is not None
    row = pl.BlockSpec((tr, d), lambda i: (i, 0))
    vec = pl.BlockSpec((1, d), lambda i: (0, 0))

    def body(*refs):
        if mod:
            x_ref, nw_ref, sc_ref, dh_ref, dx_ref, dnw_ref, dsc_ref, dsh_ref = refs
        else:
            x_ref, nw_ref, dh_ref, dx_ref, dnw_ref = refs
        i = pl.program_id(0)
        xv = x_ref[...]
        dh = dh_ref[...].astype(F32)
        r = lax.rsqrt(jnp.mean(xv * xv, axis=-1, keepdims=True) + EPS)
        y = xv * r
        a = nw_ref[...] * (1.0 + sc_ref[...]) if mod else nw_ref[...]
        dy = dh * a
        dx_ref[...] = r * (dy - y * jnp.mean(dy * y, axis=-1, keepdims=True))
        da = jnp.sum(dh * y, axis=0, keepdims=True)

        @pl.when(i == 0)
        def _():
            dnw_ref[...] = jnp.zeros_like(dnw_ref)
            if mod:
                dsc_ref[...] = jnp.zeros_like(dsc_ref)
                dsh_ref[...] = jnp.zeros_like(dsh_ref)

        if mod:
            dnw_ref[...] += da * (1.0 + sc_ref[...])
            dsc_ref[...] += da * nw_ref[...]
            dsh_ref[...] += jnp.sum(dh, axis=0, keepdims=True)
        else:
            dnw_ref[...] += da

    args = (x, nw, sc, dh) if mod else (x, nw, dh)
    n_vec = 3 if mod else 1
    return pl.pallas_call(
        body, name=name, grid=(t // tr,),
        in_specs=[row] + [vec] * (len(args) - 2) + [row],
        out_specs=[row] + [vec] * n_vec,
        out_shape=[jax.ShapeDtypeStruct((t, d), F32)] + [jax.ShapeDtypeStruct((1, d), F32)] * n_vec,
        compiler_params=_cp(("arbitrary",)),
    )(*args)


@functools.partial(jax.custom_vjp, nondiff_argnums=(4,))
def ada_norm(x, nw, sc, sh, tag):
    return _norm_fwd_call(x, nw, sc, sh, "adanorm_" + tag)


def _ada_norm_f(x, nw, sc, sh, tag):
    return _norm_fwd_call(x, nw, sc, sh, "adanorm_" + tag), (x, nw, sc)


def _ada_norm_b(tag, res, dh):
    x, nw, sc = res
    dx, dnw, dsc, dsh = _norm_bwd_call(x, nw, sc, dh, "adanorm_" + tag + "_bwd")
    return dx, dnw, dsc, dsh


ada_norm.defvjp(_ada_norm_f, _ada_norm_b)


@functools.partial(jax.custom_vjp, nondiff_argnums=(2,))
def rms_norm(x, nw, tag):
    return _norm_fwd_call(x, nw, None, None, "rms_" + tag)


def _rms_norm_f(x, nw, tag):
    return _norm_fwd_call(x, nw, None, None, "rms_" + tag), (x, nw)


def _rms_norm_b(tag, res, dh):
    x, nw = res
    dx, dnw = _norm_bwd_call(x, nw, None, dh, "rms_" + tag + "_bwd")
    return dx, dnw


rms_norm.defvjp(_rms_norm_f, _rms_norm_b)


def _gate_mix_fwd_call(gl, ya, yb, name):
    t, d = ya.shape
    tr = _rows(t)
    row = pl.BlockSpec((tr, d), lambda i: (i, 0))

    def body(ga_ref, gb_ref, ya_ref, yb_ref, o_ref):
        o_ref[...] = (_sigmoid(ga_ref[...]) * ya_ref[...]
                      + _sigmoid(gb_ref[...]) * yb_ref[...]).astype(o_ref.dtype)

    return pl.pallas_call(
        body, name=name, grid=(t // tr,),
        in_specs=[row, pl.BlockSpec((tr, d), lambda i: (i, 1)), row, row], out_specs=row,
        out_shape=jax.ShapeDtypeStruct((t, d), BF16),
        compiler_params=_cp(("parallel",)),
    )(gl, gl, ya, yb)


def _gate_mix_bwd_call(gl, ya, yb, dm, name):
    t, d = ya.shape
    tr = _rows(t)
    row = pl.BlockSpec((tr, d), lambda i: (i, 0))
    wide = pl.BlockSpec((tr, 2 * d), lambda i: (i, 0))

    def body(gl_ref, ya_ref, yb_ref, dm_ref, dgl_ref, dya_ref, dyb_ref):
        dm = dm_ref[...].astype(F32)
        ga = _sigmoid(gl_ref[:, :d])
        gb = _sigmoid(gl_ref[:, d:])
        dya_ref[...] = dm * ga
        dyb_ref[...] = dm * gb
        dgl_ref[:, :d] = dm * ya_ref[...] * ga * (1.0 - ga)
        dgl_ref[:, d:] = dm * yb_ref[...] * gb * (1.0 - gb)

    return pl.pallas_call(
        body, name=name, grid=(t // tr,),
        in_specs=[wide, row, row, row], out_specs=[wide, row, row],
        out_shape=[jax.ShapeDtypeStruct((t, 2 * d), F32), jax.ShapeDtypeStruct((t, d), F32),
                   jax.ShapeDtypeStruct((t, d), F32)],
        compiler_params=_cp(("parallel",)),
    )(gl, ya, yb, dm)


@functools.partial(jax.custom_vjp, nondiff_argnums=(3,))
def gate_mix(gl, ya, yb, tag):
    return _gate_mix_fwd_call(gl, ya, yb, "gatemix_" + tag)


def _gate_mix_f(gl, ya, yb, tag):
    return _gate_mix_fwd_call(gl, ya, yb, "gatemix_" + tag), (gl, ya, yb)


def _gate_mix_b(tag, res, dm):
    return tuple(_gate_mix_bwd_call(*res, dm, "gatemix_" + tag + "_bwd"))


gate_mix.defvjp(_gate_mix_f, _gate_mix_b)


def _resid_fwd_call(x, gt, m, name):
    t, d = x.shape
    tr = _rows(t)
    row = pl.BlockSpec((tr, d), lambda i: (i, 0))
    vec = pl.BlockSpec((1, d), lambda i: (0, 0))

    def body(x_ref, gt_ref, m_ref, o_ref):
        o_ref[...] = x_ref[...] + gt_ref[...] * m_ref[...]

    return pl.pallas_call(
        body, name=name, grid=(t // tr,), in_specs=[row, vec, row], out_specs=row,
        out_shape=jax.ShapeDtypeStruct((t, d), F32), compiler_params=_cp(("parallel",)),
    )(x, gt, m)


def _resid_bwd_call(gt, m, g, name):
    t, d = m.shape
    tr = _rows(t)
    row = pl.BlockSpec((tr, d), lambda i: (i, 0))
    vec = pl.BlockSpec((1, d), lambda i: (0, 0))

    def body(gt_ref, m_ref, g_ref, dm_ref, dgt_ref):
        i = pl.program_id(0)
        g = g_ref[...]
        dm_ref[...] = g * gt_ref[...]

        @pl.when(i == 0)
        def _():
            dgt_ref[...] = jnp.zeros_like(dgt_ref)

        dgt_ref[...] += jnp.sum(g * m_ref[...], axis=0, keepdims=True)

    return pl.pallas_call(
        body, name=name, grid=(t // tr,), in_specs=[vec, row, row], out_specs=[row, vec],
        out_shape=[jax.ShapeDtypeStruct((t, d), F32), jax.ShapeDtypeStruct((1, d), F32)],
        compiler_params=_cp(("arbitrary",)),
    )(gt, m, g)


@functools.partial(jax.custom_vjp, nondiff_argnums=(3,))
def resid(x, gt, m, tag):
    return _resid_fwd_call(x, gt, m, "resid_" + tag)


def _resid_f(x, gt, m, tag):
    return _resid_fwd_call(x, gt, m, "resid_" + tag), (gt, m)


def _resid_b(tag, res, g):
    gt, m = res
    dm, dgt = _resid_bwd_call(gt, m, g, "resid_" + tag + "_bwd")
    return g, dgt, dm


resid.defvjp(_resid_f, _resid_b)


def _swiglu_fwd_call(gu, name):
    t, f2 = gu.shape
    f = f2 // 2
    tr = _rows(t, 128)
    half = pl.BlockSpec((tr, f), lambda i: (i, 0))

    def body(g_ref, u_ref, o_ref):
        g = g_ref[...]
        o_ref[...] = (g * _sigmoid(g) * u_ref[...]).astype(o_ref.dtype)

    return pl.pallas_call(
        body, name=name, grid=(t // tr,),
        in_specs=[half, pl.BlockSpec((tr, f), lambda i: (i, 1))], out_specs=half,
        out_shape=jax.ShapeDtypeStruct((t, f), BF16), compiler_params=_cp(("parallel",)),
    )(gu, gu)


def _swiglu_bwd_call(gu, da, name):
    t, f2 = gu.shape
    f = f2 // 2
    tr = _rows(t, 128)
    wide = pl.BlockSpec((tr, f2), lambda i: (i, 0))

    def body(gu_ref, da_ref, dgu_ref):
        g = gu_ref[:, :f]
        u = gu_ref[:, f:]
        da = da_ref[...].astype(F32)
        s = _sigmoid(g)
        dgu_ref[:, :f] = da * u * s * (1.0 + g * (1.0 - s))
        dgu_ref[:, f:] = da * g * s

    return pl.pallas_call(
        body, name=name, grid=(t // tr,),
        in_specs=[wide, pl.BlockSpec((tr, f), lambda i: (i, 0))], out_specs=wide,
        out_shape=jax.ShapeDtypeStruct((t, f2), F32), compiler_params=_cp(("parallel",)),
    )(gu, da)


@functools.partial(jax.custom_vjp, nondiff_argnums=(1,))
def swiglu(gu, tag):
    return _swiglu_fwd_call(gu, "swiglu_" + tag)


def _swiglu_f(gu, tag):
    return _swiglu_fwd_call(gu, "swiglu_" + tag), (gu,)


def _swiglu_b(tag, res, da):
    return (_swiglu_bwd_call(res[0], da, "swiglu_" + tag + "_bwd"),)


swiglu.defvjp(_swiglu_f, _swiglu_b)


def loss_head(x, fw, tgt):
    t, d = x.shape
    tr = _rows(t)
    row = pl.BlockSpec((tr, d), lambda i: (i, 0))
    vec = pl.BlockSpec((1, d), lambda i: (0, 0))
    tile = pl.BlockSpec((8, LANE), lambda i: (0, 0))

    def body(x_ref, fw_ref, tgt_ref, loss_ref, dx_ref, dfw_ref):
        i = pl.program_id(0)
        xv = x_ref[...]
        fw = fw_ref[...]
        r = lax.rsqrt(jnp.mean(xv * xv, axis=-1, keepdims=True) + EPS)
        yh = xv * r
        e = yh * fw - tgt_ref[...]
        dy = e * (1.0 / d)
        dyw = dy * fw
        dx_ref[...] = r * (dyw - yh * jnp.mean(dyw * yh, axis=-1, keepdims=True))

        @pl.when(i == 0)
        def _():
            loss_ref[...] = jnp.zeros_like(loss_ref)
            dfw_ref[...] = jnp.zeros_like(dfw_ref)

        loss_ref[...] += 0.5 * jnp.sum(jnp.mean(e * e, axis=-1, keepdims=True))
        dfw_ref[...] += jnp.sum(dy * yh, axis=0, keepdims=True)

    return pl.pallas_call(
        body, name="loss_head", grid=(t // tr,), in_specs=[row, vec, row],
        out_specs=[tile, row, vec],
        out_shape=[jax.ShapeDtypeStruct((8, LANE), F32), jax.ShapeDtypeStruct((t, d), F32),
                   jax.ShapeDtypeStruct((1, d), F32)],
        compiler_params=_cp(("arbitrary",)),
    )(x, fw, tgt)


def _attn_scores(qn_ref, qr_ref, kn_ref, kr_ref, diag):
    tq = qn_ref.shape[0]
    s = lax.dot_general(qn_ref[...].astype(BF16), kn_ref[...].astype(BF16), NT, preferred_element_type=F32)
    s += lax.dot_general(qr_ref[...].astype(BF16), kr_ref[...].astype(BF16), NT, preferred_element_type=F32)
    s = s * (QK_NOPE + QK_ROPE) ** -0.5
    if diag:
        rows = lax.broadcasted_iota(jnp.int32, (tq, tq), 0)
        cols = lax.broadcasted_iota(jnp.int32, (tq, tq), 1)
        s = jnp.where(cols <= rows, s, -1e30)
    return s


def _attn_fwd_call(qn, qr, kv, kr, name):
    t = qn.shape[0]
    h_n = MLA_HEADS
    tq = _rows(t, 512)
    nq = t // tq
    assert V_HEAD == LANE and tq % LANE == 0

    def body(qn_ref, qr_ref, kn_ref, v_ref, kr_ref, o_ref, lse_ref, m_scr, l_scr, acc_scr):
        i, j = pl.program_id(1), pl.program_id(2)

        @pl.when(j == 0)
        def _():
            m_scr[...] = jnp.full_like(m_scr, -1e30)
            l_scr[...] = jnp.zeros_like(l_scr)
            acc_scr[...] = jnp.zeros_like(acc_scr)

        def step(diag):
            s = _attn_scores(qn_ref, qr_ref, kn_ref, kr_ref, diag)
            m_old = m_scr[...]
            m_new = jnp.maximum(m_old, jnp.max(s, axis=-1, keepdims=True))
            p = jnp.exp(s - jnp.tile(m_new, (1, tq // LANE)))
            alpha = jnp.exp(m_old - m_new)
            l_scr[...] = alpha * l_scr[...] + jnp.sum(p, axis=-1, keepdims=True)
            acc_scr[...] = alpha * acc_scr[...] + jnp.dot(p.astype(BF16), v_ref[...].astype(BF16),
                                                           preferred_element_type=F32)
            m_scr[...] = m_new

        @pl.when(j < i)
        def _():
            step(False)

        @pl.when(j == i)
        def _():
            step(True)
            o_ref[...] = (acc_scr[...] / l_scr[...]).astype(o_ref.dtype)
            lse_ref[...] = (m_scr[...] + jnp.log(l_scr[...]))[:, :1]

    return pl.pallas_call(
        body, name=name, grid=(h_n, nq, nq),
        in_specs=[
            pl.BlockSpec((tq, QK_NOPE), lambda h, i, j: (i, h)),
            pl.BlockSpec((None, tq, QK_ROPE), lambda h, i, j: (h, i, 0)),
            pl.BlockSpec((tq, QK_NOPE), lambda h, i, j: (jnp.minimum(j, i), 2 * h)),
            pl.BlockSpec((tq, V_HEAD), lambda h, i, j: (jnp.minimum(j, i), 2 * h + 1)),
            pl.BlockSpec((tq, QK_ROPE), lambda h, i, j: (jnp.minimum(j, i), 0)),
        ],
        out_specs=[
            pl.BlockSpec((tq, V_HEAD), lambda h, i, j: (i, h)),
            pl.BlockSpec((None, tq, 1), lambda h, i, j: (h, i, 0)),
        ],
        out_shape=[jax.ShapeDtypeStruct((t, h_n * V_HEAD), BF16),
                   jax.ShapeDtypeStruct((h_n, t, 1), F32)],
        scratch_shapes=[pltpu.VMEM((tq, LANE), F32), pltpu.VMEM((tq, LANE), F32),
                        pltpu.VMEM((tq, V_HEAD), F32)],
        compiler_params=_cp(("parallel", "parallel", "arbitrary")),
    )(qn, qr, kv, kv, kr)


def _attn_bwd_call(qn, qr, kv, kr, o, lse, do, name):
    t = qn.shape[0]
    h_n = MLA_HEADS
    tq = _rows(t, 512)
    nq = t // tq
    scale = (QK_NOPE + QK_ROPE) ** -0.5

    def body(qn_ref, qr_ref, kn_ref, v_ref, kr_ref, o_ref, lse_ref, do_ref,
             dqn_ref, dqr_ref, dkv_ref, dkr_ref, dqn_scr, dqr_scr, dkn_scr, dv_scr, dkr_scr):
        j, i = pl.program_id(1), pl.program_id(2)

        @pl.when(jnp.logical_and(j == 0, i == 0))
        def _():
            dqn_scr[...] = jnp.zeros_like(dqn_scr)
            dqr_scr[...] = jnp.zeros_like(dqr_scr)

        @pl.when(i == 0)
        def _():
            dkn_scr[...] = jnp.zeros_like(dkn_scr)
            dv_scr[...] = jnp.zeros_like(dv_scr)
            dkr_scr[...] = jnp.zeros_like(dkr_scr)

        def step(diag):
            qn_b = qn_ref[...].astype(BF16)
            qr_b = qr_ref[...].astype(BF16)
            kn_b = kn_ref[...].astype(BF16)
            kr_b = kr_ref[...].astype(BF16)
            do_b = do_ref[...]
            p = jnp.exp(_attn_scores(qn_ref, qr_ref, kn_ref, kr_ref, diag) - lse_ref[...])
            delta = jnp.sum(do_b.astype(F32) * o_ref[...].astype(F32), axis=-1, keepdims=True)
            dp = lax.dot_general(do_b, v_ref[...].astype(BF16), NT, preferred_element_type=F32)
            ds = (p * (dp - delta) * scale).astype(BF16)
            p_b = p.astype(BF16)
            dv_scr[...] += lax.dot_general(p_b, do_b, TN, preferred_element_type=F32)
            dkn_scr[...] += lax.dot_general(ds, qn_b, TN, preferred_element_type=F32)
            dkr_scr[...] += lax.dot_general(ds, qr_b, TN, preferred_element_type=F32)
            sl = pl.ds(pl.multiple_of(i * tq, tq), tq)
            dqn_scr[sl, :] += jnp.dot(ds, kn_b, preferred_element_type=F32)
            dqr_scr[sl, :] += jnp.dot(ds, kr_b, preferred_element_type=F32)

        @pl.when(i > j)
        def _():
            step(False)

        @pl.when(i == j)
        def _():
            step(True)

        @pl.when(i == nq - 1)
        def _():
            dkv_ref[:, :QK_NOPE] = dkn_scr[...].astype(dkv_ref.dtype)
            dkv_ref[:, QK_NOPE:] = dv_scr[...].astype(dkv_ref.dtype)
            dkr_ref[...] = dkr_scr[...]

        @pl.when(jnp.logical_and(j == nq - 1, i == nq - 1))
        def _():
            dqn_ref[...] = dqn_scr[...].astype(dqn_ref.dtype)
            dqr_ref[...] = dqr_scr[...].astype(dqr_ref.dtype)

    qi = lambda j, i: jnp.maximum(i, j)
    return pl.pallas_call(
        body, name=name, grid=(h_n, nq, nq),
        in_specs=[
            pl.BlockSpec((tq, QK_NOPE), lambda h, j, i: (qi(j, i), h)),
            pl.BlockSpec((None, tq, QK_ROPE), lambda h, j, i: (h, qi(j, i), 0)),
            pl.BlockSpec((tq, QK_NOPE), lambda h, j, i: (j, 2 * h)),
            pl.BlockSpec((tq, V_HEAD), lambda h, j, i: (j, 2 * h + 1)),
            pl.BlockSpec((tq, QK_ROPE), lambda h, j, i: (j, 0)),
            pl.BlockSpec((tq, V_HEAD), lambda h, j, i: (qi(j, i), h)),
            pl.BlockSpec((None, tq, 1), lambda h, j, i: (h, qi(j, i), 0)),
            pl.BlockSpec((tq, V_HEAD), lambda h, j, i: (qi(j, i), h)),
        ],
        out_specs=[
            pl.BlockSpec((t, QK_NOPE), lambda h, j, i: (0, h)),
            pl.BlockSpec((None, t, QK_ROPE), lambda h, j, i: (h, 0, 0)),
            pl.BlockSpec((tq, QK_NOPE + V_HEAD), lambda h, j, i: (j, h)),
            pl.BlockSpec((None, tq, QK_ROPE), lambda h, j, i: (h, j, 0)),
        ],
        out_shape=[jax.ShapeDtypeStruct((t, h_n * QK_NOPE), qn.dtype),
                   jax.ShapeDtypeStruct((h_n, t, QK_ROPE), qr.dtype),
                   jax.ShapeDtypeStruct((t, h_n * (QK_NOPE + V_HEAD)), kv.dtype),
                   jax.ShapeDtypeStruct((h_n, t, QK_ROPE), F32)],
        scratch_shapes=[pltpu.VMEM((t, QK_NOPE), F32), pltpu.VMEM((t, QK_ROPE), F32),
                        pltpu.VMEM((tq, QK_NOPE), F32), pltpu.VMEM((tq, V_HEAD), F32),
                        pltpu.VMEM((tq, QK_ROPE), F32)],
        compiler_params=_cp(("parallel", "arbitrary", "arbitrary")),
    )(qn, qr, kv, kv, kr, o, lse, do)


@functools.partial(jax.custom_vjp, nondiff_argnums=(4,))
def attention(qn, qr, kv, kr, tag):
    return _attn_fwd_call(qn, qr, kv, kr, "attn_" + tag)[0]


def _attention_f(qn, qr, kv, kr, tag):
    o, lse = _attn_fwd_call(qn, qr, kv, kr, "attn_" + tag)
    return o, (qn, qr, kv, kr, o, lse)


def _attention_b(tag, res, do):
    dqn, dqr, dkv, dkr_h = _attn_bwd_call(*res, do, "attn_" + tag + "_bwd")
    return dqn, dqr, dkv, jnp.sum(dkr_h, axis=0).astype(res[3].dtype)


attention.defvjp(_attention_f, _attention_b)


def _shift_down(u, s):
    if s == 0:
        return u
    t = u.shape[0]
    rolled = pltpu.roll(u, s, 0)
    return jnp.where(lax.broadcasted_iota(jnp.int32, u.shape, 0) >= s, rolled, 0.0)


def _shift_up(u, s):
    if s == 0:
        return u
    t = u.shape[0]
    rolled = pltpu.roll(u, t - s, 0)
    return jnp.where(lax.broadcasted_iota(jnp.int32, u.shape, 0) < t - s, rolled, 0.0)


def _conv_blocks(t, c3):
    p = c3 // 3
    tc = _tile(p, 512)
    per = p // tc
    return p, tc, per


def _conv_fwd_call(u, w, name):
    t, c3 = u.shape
    p, tc, per = _conv_blocks(t, c3)

    def body(u_ref, w_ref, o_ref):
        u = u_ref[...]
        y = jnp.zeros_like(u)
        for j in range(CONV_WIDTH):
            y = y + w_ref[j:j + 1, :] * _shift_down(u, CONV_WIDTH - 1 - j)
        o_ref[...] = y * _sigmoid(y)

    return pl.pallas_call(
        body, name=name, grid=(c3 // tc,),
        in_specs=[pl.BlockSpec((t, tc), lambda cb: (0, cb)),
                  pl.BlockSpec((CONV_WIDTH, tc), lambda cb: (0, cb))],
        out_specs=pl.BlockSpec((None, t, tc), lambda cb: (cb // per, 0, cb % per)),
        out_shape=jax.ShapeDtypeStruct((3, t, p), F32),
        compiler_params=_cp(("parallel",)),
    )(u, w)


def _conv_bwd_call(u, w, do, name):
    t, c3 = u.shape
    p, tc, per = _conv_blocks(t, c3)

    def body(u_ref, w_ref, do_ref, du_ref, dw_ref):
        u = u_ref[...]
        shifted = [_shift_down(u, CONV_WIDTH - 1 - j) for j in range(CONV_WIDTH)]
        y = jnp.zeros_like(u)
        for j in range(CONV_WIDTH):
            y = y + w_ref[j:j + 1, :] * shifted[j]
        s = _sigmoid(y)
        dy = do_ref[...] * s * (1.0 + y * (1.0 - s))
        du = jnp.zeros_like(u)
        for j in range(CONV_WIDTH):
            du = du + w_ref[j:j + 1, :] * _shift_up(dy, CONV_WIDTH - 1 - j)
            dw_ref[j:j + 1, :] = jnp.sum(dy * shifted[j], axis=0, keepdims=True)
        du_ref[...] = du

    return pl.pallas_call(
        body, name=name, grid=(c3 // tc,),
        in_specs=[pl.BlockSpec((t, tc), lambda cb: (0, cb)),
                  pl.BlockSpec((CONV_WIDTH, tc), lambda cb: (0, cb)),
                  pl.BlockSpec((None, t, tc), lambda cb: (cb // per, 0, cb % per))],
        out_specs=[pl.BlockSpec((t, tc), lambda cb: (0, cb)),
                   pl.BlockSpec((CONV_WIDTH, tc), lambda cb: (0, cb))],
        out_shape=[jax.ShapeDtypeStruct((t, c3), F32), jax.ShapeDtypeStruct((CONV_WIDTH, c3), F32)],
        compiler_params=_cp(("parallel",)),
    )(u, w, do)


@functools.partial(jax.custom_vjp, nondiff_argnums=(2,))
def conv_silu(u, w, tag):
    return _conv_fwd_call(u, w, "conv_" + tag)


def _conv_silu_f(u, w, tag):
    return _conv_fwd_call(u, w, "conv_" + tag), (u, w)


def _conv_silu_b(tag, res, do):
    return tuple(_conv_bwd_call(*res, do, "conv_" + tag + "_bwd"))


conv_silu.defvjp(_conv_silu_f, _conv_silu_b)


BNN = (((2,), (1,)), ((0,), (0,)))
BNT = (((2,), (2,)), ((0,), (0,)))
BTN = (((1,), (1,)), ((0,), (0,)))


def _hdot(a, b, dn=BNN):
    return lax.dot_general(a, b, dn, precision=HIGHEST, preferred_element_type=F32)


def _bf16_dot(a, b, dn):
    return lax.dot_general(a.astype(BF16), b.astype(BF16), dn, preferred_element_type=F32)


@functools.partial(jax.custom_vjp, nondiff_argnums=(2,))
def _bdot(a, b, dn=BNN):
    return _bf16_dot(a, b, dn)


def _bdot_f(a, b, dn):
    return _bf16_dot(a, b, dn), (a, b)


def _bdot_b(dn, res, g):
    a, b = res
    if dn == BNN:
        return _bf16_dot(g, b, BNT), _bf16_dot(a, g, BTN)
    if dn == BNT:
        return _bf16_dot(g, b, BNN), _bf16_dot(g, a, BTN)
    return _bf16_dot(b, g, BNT), _bf16_dot(a, g, BNN)


_bdot.defvjp(_bdot_f, _bdot_b)


GDN_HEADS_PER_STEP = 8
GDN_HEADS_PER_STEP_BWD = 4


def _gdn_chunk(q, k, v, z, bl, al, a_log, dtb, gn, s):
    b, c = q.shape[0], q.shape[1]
    ri = lax.broadcasted_iota(jnp.int32, (c, c), 0)
    ci = lax.broadcasted_iota(jnp.int32, (c, c), 1)
    lower = (ri >= ci)[None]
    strict = (ri > ci)[None]
    low_incl = jnp.broadcast_to((ri >= ci).astype(F32), (b, c, c))
    up_incl = jnp.broadcast_to((ri <= ci).astype(F32), (b, c, c))
    eye = (ri == ci).astype(F32)[None]

    q = q * lax.rsqrt(jnp.sum(q * q, axis=-1, keepdims=True) + EPS) * (GDN_DK ** -0.5)
    k = k * lax.rsqrt(jnp.sum(k * k, axis=-1, keepdims=True) + EPS)
    beta = _sigmoid(bl)
    g = -jnp.exp(a_log) * _softplus(al + dtb)
    g_w = jnp.broadcast_to(g, (b, c, LANE))
    gc = _hdot(low_incl, g_w)
    gr = _hdot(g_w[:, :, :c], up_incl, BTN)
    diff = gc[:, :, :c] - gr
    decay = jnp.where(lower, jnp.exp(jnp.where(lower, diff, 0.0)), 0.0)
    kb = k * beta
    lmat = jnp.where(strict, _bdot(kb, k, BNT) * decay, 0.0)
    inv = eye - lmat
    pw = lmat
    for _ in range(int(math.log2(c)) - 1):
        pw = _hdot(pw, pw)
        inv = _hdot(inv, eye + pw)
    eg = jnp.exp(gc)
    u = _hdot(inv, v * beta)
    w = _hdot(inv, kb * eg)
    attn = jnp.where(lower, _bdot(q, k, BNT) * decay, 0.0)
    v_new = u - _bdot(w, s)
    o = _bdot(q * eg, s) + _bdot(attn, v_new)
    g_last = jnp.sum(g_w, axis=1, keepdims=True)
    k_dec = k * jnp.exp(g_last - gc)
    s_new = s * jnp.exp(g_last) + _bdot(k_dec, v_new, BTN)
    on = o * lax.rsqrt(jnp.mean(o * o, axis=-1, keepdims=True) + EPS) * gn
    return on * (z * _sigmoid(z)), s_new


def _gdn_specs(n_chunks, hb, rev):
    c = CHUNK
    nn = (lambda n: n_chunks - 1 - n) if rev else (lambda n: n)
    plane = lambda pidx: pl.BlockSpec((None, c, hb * GDN_DK), lambda hg, n: (pidx, nn(n), hg))
    col = pl.BlockSpec((hb, c, 1), lambda hg, n: (hg, nn(n), 0))
    scal = pl.BlockSpec((hb, 1, 1), lambda hg, n: (hg, 0, 0))
    zspec = pl.BlockSpec((c, hb * GDN_DV), lambda hg, n: (nn(n), hg))
    gnspec = pl.BlockSpec((1, GDN_DV), lambda hg, n: (0, 0))
    sspec = pl.BlockSpec((hb, None, GDN_DK, GDN_DV), lambda hg, n: (hg, nn(n), 0, 0))
    return plane, col, scal, zspec, gnspec, sspec


def _heads_per_step(want):
    return math.gcd(want, GDN_HEADS)


def _heads(ref, hb):
    return jnp.stack([ref[:, j * GDN_DK:(j + 1) * GDN_DK] for j in range(hb)])


def _gdn_fwd_call(qkv, z, bl, al, a_log, dtb, gn, name):
    t = z.shape[0]
    h_n = GDN_HEADS
    hb = _heads_per_step(GDN_HEADS_PER_STEP)
    n_chunks = t // CHUNK
    plane, col, scal, zspec, gnspec, sspec = _gdn_specs(n_chunks, hb, False)

    def body(q_ref, k_ref, v_ref, z_ref, bl_ref, al_ref, a_ref, dtb_ref, gn_ref, o_ref, sall_ref, s_scr):
        n = pl.program_id(1)

        @pl.when(n == 0)
        def _():
            s_scr[...] = jnp.zeros_like(s_scr)

        s = s_scr[...]
        sall_ref[...] = s
        o, s_new = _gdn_chunk(_heads(q_ref, hb), _heads(k_ref, hb), _heads(v_ref, hb), _heads(z_ref, hb),
                              bl_ref[...], al_ref[...], a_ref[...], dtb_ref[...], gn_ref[...], s)
        for j in range(hb):
            o_ref[:, j * GDN_DV:(j + 1) * GDN_DV] = o[j].astype(o_ref.dtype)
        s_scr[...] = s_new

    return pl.pallas_call(
        body, name=name, grid=(h_n // hb, n_chunks),
        in_specs=[plane(0), plane(1), plane(2), zspec, col, col, scal, scal, gnspec],
        out_specs=[zspec, sspec],
        out_shape=[jax.ShapeDtypeStruct((t, h_n * GDN_DV), BF16),
                   jax.ShapeDtypeStruct((h_n, n_chunks, GDN_DK, GDN_DV), F32)],
        scratch_shapes=[pltpu.VMEM((hb, GDN_DK, GDN_DV), F32)],
        compiler_params=_cp(("parallel", "arbitrary")),
    )(qkv, qkv, qkv, z, bl, al, a_log, dtb, gn)


def _gdn_bwd_call(qkv, z, bl, al, a_log, dtb, gn, sall, do, name):
    t = z.shape[0]
    h_n = GDN_HEADS
    hb = _heads_per_step(GDN_HEADS_PER_STEP_BWD)
    n_chunks = t // CHUNK
    c = CHUNK
    plane, col, scal, zspec, gnspec, sspec = _gdn_specs(n_chunks, hb, True)
    dplanes = pl.BlockSpec((3, c, hb * GDN_DK), lambda hg, n: (0, n_chunks - 1 - n, hg))
    gnh = pl.BlockSpec((None, 1, GDN_DV), lambda hg, n: (hg, 0, 0))

    def body(q_ref, k_ref, v_ref, z_ref, bl_ref, al_ref, a_ref, dtb_ref, gn_ref, s_ref, do_ref,
             dqkv_ref, dz_ref, dbl_ref, dal_ref, da_ref, ddtb_ref, dgn_ref, ds_scr):
        n = pl.program_id(1)

        @pl.when(n == 0)
        def _():
            ds_scr[...] = jnp.zeros_like(ds_scr)
            da_ref[...] = jnp.zeros_like(da_ref)
            ddtb_ref[...] = jnp.zeros_like(ddtb_ref)
            dgn_ref[...] = jnp.zeros_like(dgn_ref)

        _, vjp = jax.vjp(_gdn_chunk, _heads(q_ref, hb), _heads(k_ref, hb), _heads(v_ref, hb), _heads(z_ref, hb),
                         bl_ref[...], al_ref[...], a_ref[...], dtb_ref[...], gn_ref[...], s_ref[...])
        dq, dk, dv, dz, dbl, dal, da, ddtb, dgn, ds = vjp((_heads(do_ref, hb).astype(F32), ds_scr[...]))
        for j in range(hb):
            hs = slice(j * GDN_DK, (j + 1) * GDN_DK)
            dqkv_ref[0, :, hs] = dq[j]
            dqkv_ref[1, :, hs] = dk[j]
            dqkv_ref[2, :, hs] = dv[j]
            dz_ref[:, hs] = dz[j]
        dbl_ref[...] = dbl
        dal_ref[...] = dal
        da_ref[...] += da
        ddtb_ref[...] += ddtb
        dgn_ref[...] += dgn
        ds_scr[...] = ds

    return pl.pallas_call(
        body, name=name, grid=(h_n // hb, n_chunks),
        in_specs=[plane(0), plane(1), plane(2), zspec, col, col, scal, scal, gnspec, sspec, zspec],
        out_specs=[dplanes, zspec, col, col, scal, scal, gnh],
        out_shape=[jax.ShapeDtypeStruct((3, t, h_n * GDN_DK), F32),
                   jax.ShapeDtypeStruct((t, h_n * GDN_DV), F32),
                   jax.ShapeDtypeStruct((h_n, t, 1), F32), jax.ShapeDtypeStruct((h_n, t, 1), F32),
                   jax.ShapeDtypeStruct((h_n, 1, 1), F32), jax.ShapeDtypeStruct((h_n, 1, 1), F32),
                   jax.ShapeDtypeStruct((h_n // hb, 1, GDN_DV), F32)],
        scratch_shapes=[pltpu.VMEM((hb, GDN_DK, GDN_DV), F32)],
        compiler_params=_cp(("parallel", "arbitrary")),
    )(qkv, qkv, qkv, z, bl, al, a_log, dtb, gn, sall, do)


@functools.partial(jax.custom_vjp, nondiff_argnums=(7,))
def gdn(qkv, z, bl, al, a_log, dtb, gn, tag):
    return _gdn_fwd_call(qkv, z, bl, al, a_log, dtb, gn, "gdn_" + tag)[0]


def _gdn_f(qkv, z, bl, al, a_log, dtb, gn, tag):
    o, sall = _gdn_fwd_call(qkv, z, bl, al, a_log, dtb, gn, "gdn_" + tag)
    return o, (qkv, z, bl, al, a_log, dtb, gn, sall)


def _gdn_b(tag, res, do):
    dqkv, dz, dbl, dal, da, ddtb, dgn_h = _gdn_bwd_call(*res, do, "gdn_" + tag + "_bwd")
    return dqkv, dz, dbl, dal, da, ddtb, jnp.sum(dgn_h, axis=0)


gdn.defvjp(_gdn_f, _gdn_b)


def adamw(w, parts, m, v, name):
    n_layers = len(parts)
    n_parts, r, c = parts[0].shape
    assert w.shape == (n_layers * r, c), (w.shape, parts[0].shape)
    tr = r
    for cand in (512, 256, 128, 64, 32, 16, 8):
        if r % cand == 0 and cand * c <= 256 * 1024:
            tr = cand
            break
    nb = r // tr
    blk = pl.BlockSpec((tr, c), lambda l, i: (l * nb + i, 0))
    bc1 = 1.0 - ADAM_B1 ** ADAM_STEP
    bc2 = 1.0 - ADAM_B2 ** ADAM_STEP

    def part_spec(li):
        return pl.BlockSpec((n_parts, tr, c),
                            lambda l, i: (0, jnp.where(l == li, i, jnp.where(l < li, 0, nb - 1)), 0))

    def body(*refs):
        w_ref, p_refs = refs[0], refs[1:1 + n_layers]
        m_ref, v_ref, g_ref, d_ref, mo_ref, vo_ref = refs[1 + n_layers:]
        for li in range(n_layers):
            @pl.when(pl.program_id(0) == li)
            def _(p_ref=p_refs[li]):
                g = p_ref[0].astype(F32)
                for i in range(1, n_parts):
                    g = g + p_ref[i].astype(F32)
                m2 = ADAM_B1 * m_ref[...] + (1.0 - ADAM_B1) * g
                v2 = ADAM_B2 * v_ref[...] + (1.0 - ADAM_B2) * (g * g)
                g_ref[...] = g
                mo_ref[...] = m2
                vo_ref[...] = v2
                d_ref[...] = -ADAM_LR * ((m2 / bc1) / (jnp.sqrt(v2 / bc2) + ADAM_EPS)
                                         + ADAM_WD * w_ref[...])

    return pl.pallas_call(
        body, name=name, grid=(n_layers, nb),
        in_specs=[blk] + [part_spec(li) for li in range(n_layers)] + [blk, blk],
        out_specs=[blk] * 4, out_shape=[jax.ShapeDtypeStruct(w.shape, F32)] * 4,
        compiler_params=_cp(("arbitrary", "arbitrary")),
    )(w, *parts, m, v)


_HBM = pl.BlockSpec(memory_space=pltpu.HBM)
_SEM = pl.BlockSpec(memory_space=pltpu.SEMAPHORE)
_EFFECT = pltpu.SideEffectType.DATAFLOW_SIDE_EFFECTING


def _peer(x, y, c, d):
    px = 1 - x if d & 4 else x
    py = 1 - y if d & 2 else y
    pc = 1 - c if d & 1 else c
    return (px, py, pc), 4 * px + 2 * py + pc


def copy_start(arrays, scatter, carry, name):
    n = len(arrays)
    lands = [lax.empty(a.shape if scatter else (N_DEV,) + a.shape, a.dtype) for a in arrays]

    def body(*refs):
        srcs, dsts = refs[:n], refs[n:2 * n]
        sems = refs[2 * n + 1:4 * n + 1]
        x, y, c = (lax.axis_index(a) for a in AXES)
        me = 4 * x + 2 * y + c
        for k in range(n):
            for d in range(1, N_DEV):
                peer, pid = _peer(x, y, c, d)
                pltpu.make_async_remote_copy(
                    src_ref=srcs[k].at[pid] if scatter else srcs[k], dst_ref=dsts[k].at[me],
                    send_sem=sems[2 * k], recv_sem=sems[2 * k + 1],
                    device_id=peer, device_id_type=pl.DeviceIdType.MESH).start()

    operands = list(arrays) + lands + [carry]
    outs = pl.pallas_call(
        body, name=name,
        out_shape=tuple([pltpu.SemaphoreType.DMA(())] * (2 * n)
                        + [pltpu.HBM(a.shape, a.dtype) for a in operands]),
        in_specs=[_HBM] * (2 * n + 1),
        out_specs=tuple([_SEM] * (2 * n) + [_HBM] * (2 * n + 1)),
        input_output_aliases={i: 2 * n + i for i in range(2 * n + 1)},
        compiler_params=pltpu.CompilerParams(has_side_effects=_EFFECT),
    )(*[pltpu.with_memory_space_constraint(a, pltpu.HBM) for a in operands])
    sems, thru = outs[:2 * n], outs[2 * n:4 * n]
    handles = [(sems[2 * k], sems[2 * k + 1], thru[k], thru[n + k]) for k in range(n)]
    return outs[-1], handles


def copy_wait(handles, after, name):
    n = len(handles)
    sems = [s for h in handles for s in h[:2]]
    srcs = [h[2] for h in handles]
    lands = [h[3] for h in handles]

    def body(*refs):
        dsts = refs[n:2 * n]
        sem_refs = refs[2 * n:4 * n]
        x, y, c = (lax.axis_index(a) for a in AXES)
        for k in range(n):
            seven = dsts[k].at[pl.ds(0, N_DEV - 1)]
            pltpu.make_async_remote_copy(
                src_ref=seven, dst_ref=seven, send_sem=sem_refs[2 * k], recv_sem=sem_refs[2 * k + 1],
                device_id=(x, y, c), device_id_type=pl.DeviceIdType.MESH).wait()

    outs = pl.pallas_call(
        body, name=name,
        out_shape=tuple([pltpu.HBM(a.shape, a.dtype) for a in srcs + lands]),
        in_specs=[_HBM] * (2 * n) + [_SEM] * (2 * n) + [pl.BlockSpec(memory_space=pl.ANY)],
        out_specs=tuple([_HBM] * (2 * n)),
        input_output_aliases={i: i for i in range(2 * n)},
        compiler_params=pltpu.CompilerParams(has_side_effects=_EFFECT),
    )(*srcs, *lands, *sems, after)
    return list(outs[:n]), list(outs[n:])


def exchange(arrays, modes, name):
    n = len(arrays)
    hbm = pl.BlockSpec(memory_space=pltpu.HBM)
    out_shape = [jax.ShapeDtypeStruct(a.shape if md == "scatter" else (N_DEV,) + a.shape, a.dtype)
                 for a, md in zip(arrays, modes)]

    def body(*refs):
        ins, outs = refs[:n], refs[n:2 * n]
        send_sems, recv_sems, local_sems = refs[2 * n:]
        x, y, c = (lax.axis_index(a) for a in AXES)
        me = 4 * x + 2 * y + c

        def src(k, p):
            return ins[k].at[p] if modes[k] == "scatter" else ins[k]

        local = [pltpu.make_async_copy(src(k, me), outs[k].at[me], local_sems.at[k]) for k in range(n)]
        for cp in local:
            cp.start()
        started = []
        for d in range(1, N_DEV):
            px = 1 - x if d & 4 else x
            py = 1 - y if d & 2 else y
            pc = 1 - c if d & 1 else c
            pid = 4 * px + 2 * py + pc
            for k in range(n):
                pltpu.make_async_remote_copy(
                    src_ref=src(k, pid), dst_ref=outs[k].at[me],
                    send_sem=send_sems.at[k, d - 1], recv_sem=recv_sems.at[k, d - 1],
                    device_id=(px, py, pc), device_id_type=pl.DeviceIdType.MESH).start()
                started.append((k, d, pid, (px, py, pc)))
        for k, d, pid, peer in started:
            pltpu.make_async_remote_copy(
                src_ref=src(k, pid), dst_ref=outs[k].at[pid],
                send_sem=send_sems.at[k, d - 1], recv_sem=recv_sems.at[k, d - 1],
                device_id=peer, device_id_type=pl.DeviceIdType.MESH).wait()
        for cp in local:
            cp.wait()

    outs = pl.pallas_call(
        body, name=name, in_specs=[hbm] * n, out_specs=[hbm] * n, out_shape=out_shape,
        scratch_shapes=[pltpu.SemaphoreType.DMA((n, N_DEV - 1)), pltpu.SemaphoreType.DMA((n, N_DEV - 1)),
                        pltpu.SemaphoreType.DMA((n,))],
        compiler_params=pltpu.CompilerParams(has_side_effects=True),
    )(*arrays)
    return list(outs)


BIG = ("w_in", "w_uq", "w_ukv", "w_o_mla", "w_o_gdn", "w_o", "w_gate_up", "w_down")
ROW_SHARDED = ("w_o", "w_down")
SMALL = ("b_ada", "norm_mix", "norm_ffn", "q_a_norm", "kv_a_norm", "A_log", "dt_bias", "gdn_norm",
         "final_norm")
WEIGHTS = ("w_ada", "b_ada", "norm_mix", "norm_ffn", "w_in", "q_a_norm", "kv_a_norm", "w_uq", "w_ukv",
           "w_o_mla", "conv_w", "A_log", "dt_bias", "gdn_norm", "w_o_gdn", "w_o", "w_gate_up", "w_down",
           "final_norm")


def _unslot(g):
    return g.transpose(1, 0, 2).reshape(g.shape[1], -1)


def _cols(g):
    return g if g.shape[-1] % LANE == 0 else _unslot(g)


def _stack_rows(g):
    return g.reshape(-1, g.shape[-1])


def _pad_cols(a):
    return jnp.pad(a, ((0, 0), (0, _pad_lanes(a.shape[1]) - a.shape[1])))


def _rope(xv, cos, sin):
    x1, x2 = jnp.split(xv, 2, axis=-1)
    return jnp.concatenate([x1 * cos - x2 * sin, x2 * cos + x1 * sin], axis=-1)


MIX_WEIGHTS = ("w_uq", "w_ukv", "w_o_mla", "w_o_gdn", "w_o")
FFN_WEIGHTS = ("w_gate_up", "w_down")


def _stage_in(x, mod, nm, w_in_s, tg):
    d = x.shape[1]
    hg = GDN_HEADS
    w_in = _unslot(w_in_s)
    o1 = Q_LORA + KV_LORA + QK_ROPE
    o2 = o1 + 2 * hg * GDN_DK + hg * GDN_DV
    o3 = o2 + hg * GDN_DV
    o4 = o3 + 2 * hg
    h = ada_norm(x, nm, mod[:, d:2 * d], mod[:, :d], "mix" + tg)
    return (mm(h, _pad_cols(w_in[:, :o1]), "in_a" + tg, F32), mm(h, w_in[:, o1:o2], "in_qkv" + tg, F32),
            mm(h, w_in[:, o2:o3], "in_z" + tg, F32), mm(h, _pad_cols(w_in[:, o3:o4]), "in_ba" + tg, F32),
            mm(h, w_in[:, o4:o4 + 2 * d], "in_g" + tg, F32))


def _stage_mix(x, mod, seg_a, qkv, z, ba, gl, w_uq_s, w_ukv_s, w_o_mla_s, w_o_gdn_s, w_o_s, conv_s,
               qan, kvan, a_log, dtb, gn, cos, sin, tg):
    t, d = x.shape
    hq, hg = MLA_HEADS, GDN_HEADS
    w_uq = _unslot(w_uq_s).reshape(Q_LORA, hq, QK_NOPE + QK_ROPE)
    w_uq = jnp.concatenate([w_uq[:, :, :QK_NOPE].reshape(Q_LORA, hq * QK_NOPE),
                            w_uq[:, :, QK_NOPE:].reshape(Q_LORA, hq * QK_ROPE)], axis=1)
    c_q = seg_a[:, :Q_LORA]
    c_kv = seg_a[:, Q_LORA:Q_LORA + KV_LORA]
    k_pe = seg_a[:, Q_LORA + KV_LORA:Q_LORA + KV_LORA + QK_ROPE]
    qf = mm(rms_norm(c_q, qan, "qa" + tg), w_uq, "uq" + tg, BF16)
    kvf = mm(rms_norm(c_kv, kvan, "kva" + tg), _cols(w_ukv_s), "ukv" + tg, BF16)
    qn = qf[:, :hq * QK_NOPE]
    q_pe = qf[:, hq * QK_NOPE:].astype(F32).reshape(t, hq, QK_ROPE)
    qr = _rope(q_pe, cos[:, None, :], sin[:, None, :]).transpose(1, 0, 2).astype(BF16)
    kr = _rope(k_pe, cos, sin).astype(BF16)
    y_a = mm(attention(qn, qr, kvf, kr, tg), _cols(w_o_mla_s), "o_mla" + tg, F32)
    conv_w = conv_s.transpose(1, 0, 2).reshape(CONV_WIDTH, -1)
    qkv_c = conv_silu(qkv, conv_w, tg)
    bl = ba[:, :hg].T[:, :, None]
    al = ba[:, hg:2 * hg].T[:, :, None]
    o_gdn = gdn(qkv_c, z, bl, al, a_log.reshape(hg, 1, 1), dtb.reshape(hg, 1, 1), gn, tg)
    y_b = mm(o_gdn, _cols(w_o_gdn_s), "o_gdn" + tg, F32)
    mix = mm(gate_mix(gl, y_a, y_b, tg), _stack_rows(w_o_s), "w_o" + tg, F32)
    return resid(x, mod[:, 2 * d:3 * d], mix, "mix" + tg)


def _stage_ffn(x, mod, nf, w_gu_s, w_down_s, tg):
    d = x.shape[1]
    h = ada_norm(x, nf, mod[:, 4 * d:5 * d], mod[:, 3 * d:4 * d], "ffn" + tg)
    gu = mm(h, _cols(w_gu_s), "gu" + tg, F32)
    dn = mm(swiglu(gu, tg), _stack_rows(w_down_s), "down" + tg, F32)
    return resid(x, mod[:, 5 * d:6 * d], dn, "ffn" + tg)


def _flat_row(arrs):
    v = jnp.concatenate([a.reshape(-1) for a in arrs])
    return jnp.pad(v, (0, _pad_lanes(v.shape[0]) - v.shape[0]))[None, :]


def kernel(x, c, positions, w_ada, b_ada, norm_mix, norm_ffn, w_in, q_a_norm, kv_a_norm, w_uq, w_ukv, w_o_mla, conv_w, A_log, dt_bias, gdn_norm, w_o_gdn, w_o, w_gate_up, w_down, final_norm, loss_target, m_w_ada, m_b_ada, m_norm_mix, m_norm_ffn, m_w_in, m_q_a_norm, m_kv_a_norm, m_w_uq, m_w_ukv, m_w_o_mla, m_conv_w, m_A_log, m_dt_bias, m_gdn_norm, m_w_o_gdn, m_w_o, m_w_gate_up, m_w_down, m_final_norm, v_w_ada, v_b_ada, v_norm_mix, v_norm_ffn, v_w_in, v_q_a_norm, v_kv_a_norm, v_w_uq, v_w_ukv, v_w_o_mla, v_conv_w, v_A_log, v_dt_bias, v_gdn_norm, v_w_o_gdn, v_w_o, v_w_gate_up, v_w_down, v_final_norm):
    given = dict(locals())
    t, d = x.shape[1], x.shape[2]
    n_ada = w_ada.shape[2]
    me = 4 * lax.axis_index("x") + 2 * lax.axis_index("y") + lax.axis_index("c")

    def with_own(land, own):
        return lax.dynamic_update_slice(land, own[None], (me,) + (0,) * own.ndim)

    got = exchange([c, conv_w], ["gather", "gather"], "gather_small")
    c_all, conv_g = got[0].reshape(N_DEV, d), got[1]
    c_rows = jnp.pad(c_all, ((0, 16 - N_DEV), (0, 0)))
    mod_cols = jnp.stack([_mm(c_rows, w_ada[l], "nn", F32, "ada_mod%d" % l, a_act="silu")[:N_DEV]
                          for l in range(DEPTH)], axis=1)
    mod_mine = exchange([mod_cols], ["scatter"], "scatter_mod")[0]
    mods = mod_mine.transpose(1, 0, 2).reshape(DEPTH, N_DEV * n_ada) + b_ada

    keys = [(n, l) for l in range(DEPTH) for n in BIG]
    mods, handles = copy_start([given[n][l].astype(BF16) for n, l in keys], False, mods, "gather_start")
    handles = dict(zip(keys, handles))

    def landed(ks, after, name):
        srcs, lands = copy_wait([handles[k] for k in ks], after, name)
        return [with_own(land, src) for src, land in zip(srcs, lands)]

    inv_freq = 1.0 / (ROPE_THETA ** (jnp.arange(0, QK_ROPE, 2, dtype=F32) / QK_ROPE))
    ang = positions[0].astype(F32)[:, None] * inv_freq
    cos, sin = jnp.cos(ang), jnp.sin(ang)
    xl = x[0]
    after = mods
    vjps = []
    for l in range(DEPTH):
        tg = str(l)
        mod = mods[l:l + 1]
        (w_in_s,) = landed([("w_in", l)], after, "wait_in" + tg)
        seg, vjp_in = jax.vjp(lambda *a, tg=tg: _stage_in(*a, tg), xl, mod, norm_mix[l:l + 1], w_in_s)
        w_mix = landed([(n, l) for n in MIX_WEIGHTS], seg[0], "wait_mix" + tg)
        xm, vjp_mix = jax.vjp(lambda *a, tg=tg: _stage_mix(*a, cos, sin, tg), xl, mod, *seg, *w_mix,
                              conv_g[:, l], q_a_norm[l:l + 1], kv_a_norm[l:l + 1], A_log[l], dt_bias[l],
                              gdn_norm[l:l + 1])
        w_ffn = landed([(n, l) for n in FFN_WEIGHTS], xm, "wait_ffn" + tg)
        xl, vjp_ffn = jax.vjp(lambda *a, tg=tg: _stage_ffn(*a, tg), xm, mod, norm_ffn[l:l + 1], *w_ffn)
        after = xl
        vjps.append((vjp_in, vjp_mix, vjp_ffn))

    loss_t, g, dfn = loss_head(xl, final_norm[None, :], loss_target[0])
    loss = lax.psum(loss_t[0, 0], AXES)
    dsmall = {n: [None] * DEPTH for n in SMALL + ("conv_w",)}
    dmods = [None] * DEPTH
    sent = {}

    def send(ks, grads, carry, name):
        carry, hs = copy_start(list(grads), True, carry, name)
        sent.update(zip(ks, hs))
        return carry

    for l in reversed(range(DEPTH)):
        tg = str(l)
        vjp_in, vjp_mix, vjp_ffn = vjps[l]
        dxm, dmod_f, dsmall["norm_ffn"][l], *dw = vjp_ffn(g)
        dxm = send([(n, l) for n in FFN_WEIGHTS], dw, dxm, "scatter_ffn" + tg)
        dx_m, dmod_m, *rest = vjp_mix(dxm)
        dseg, dw, rest = rest[:5], rest[5:5 + len(MIX_WEIGHTS)], rest[5 + len(MIX_WEIGHTS):]
        dseg[0] = send([(n, l) for n in MIX_WEIGHTS], dw, dseg[0], "scatter_mix" + tg)
        for n, gr in zip(("conv_w", "q_a_norm", "kv_a_norm", "A_log", "dt_bias", "gdn_norm"), rest):
            dsmall[n][l] = gr
        dx_i, dmod_i, dsmall["norm_mix"][l], dw_in = vjp_in(tuple(dseg))
        g = dx_i + dx_m
        if l > 0:
            g = send([("w_in", l)], [dw_in], g, "scatter_in" + tg)
        dmods[l] = dmod_f + dmod_m + dmod_i
    dx = g
    dmods = jnp.concatenate(dmods, axis=0)
    dconv = jnp.stack(dsmall.pop("conv_w"), axis=1)
    dsmall = {n: jnp.concatenate(v, axis=0) if v[0].ndim == 2 else jnp.stack(v)
              for n, v in dsmall.items() if v[0] is not None}
    dsmall["b_ada"] = dmods
    dsmall["final_norm"] = dfn[0]

    dmod_cols = dmods.reshape(DEPTH, N_DEV, n_ada).transpose(1, 0, 2)
    conv_parts, dmod_all, small_parts = exchange(
        [dconv, dmod_cols, _flat_row([dsmall[n] for n in SMALL])], ["scatter", "scatter", "gather"],
        "exchange_small")
    dmod_all = send([("w_in", 0)], [dw_in], dmod_all, "scatter_in0")

    res = {}
    dm_rows = jnp.pad(dmod_all, ((0, 16 - N_DEV), (0, 0), (0, 0)))
    g_ada = jnp.stack([_mm(c_rows, dm_rows[:, l], "tn", F32, "ada_dw%d" % l, a_act="silu")
                       for l in range(DEPTH)])
    r2 = (DEPTH * d, n_ada)
    outs = adamw(w_ada.reshape(r2), [g_ada.reshape((1,) + r2)], m_w_ada.reshape(r2), v_w_ada.reshape(r2),
                 "adamw_w_ada")
    res["w_ada"] = [o.reshape(w_ada.shape) for o in outs]
    packed = SMALL + ("conv_w",)
    p_all = jnp.concatenate([small_parts, conv_parts.reshape(N_DEV, 1, -1)], axis=2)
    pack = lambda pre: jnp.concatenate([_flat_row([given[pre + n] for n in SMALL]),
                                        given[pre + "conv_w"].reshape(1, -1)], axis=1)
    outs = adamw(pack(""), [p_all], pack("m_"), pack("v_"), "adamw_small")
    done = [res["w_ada"][1], outs[1]]
    for group, gname in ((FFN_WEIGHTS, "ffn"), (MIX_WEIGHTS, "mix"), (("w_in",), "in")):
        ks = [(n, l) for l in reversed(range(DEPTH)) for n in group]
        after = sum(lax.slice(a, (0,) * a.ndim, (1,) * a.ndim).reshape(1, 1) for a in done)
        srcs, lands = copy_wait([sent[k] for k in ks], after, "scatter_wait_" + gname)
        parts = {k: with_own(land, lax.dynamic_index_in_dim(src, me, 0, keepdims=False))
                 for k, src, land in zip(ks, srcs, lands)}
        for n in group:
            w = given[n]
            r2 = (w.shape[0] * w.shape[1], w.shape[2])
            res[n] = [o.reshape(w.shape) for o in
                      adamw(w.reshape(r2), [parts[(n, l)] for l in range(DEPTH)], given["m_" + n].reshape(r2),
                            given["v_" + n].reshape(r2), "adamw_" + n)]
            done.append(res[n][1])
    off = 0
    for n in packed:
        if n == "conv_w":
            off = small_parts.shape[2]
        size = math.prod(given[n].shape)
        res[n] = [o[0, off:off + size].reshape(given[n].shape) for o in outs]
        off += size

    return (loss, dx[None]) + tuple(res[n][i] for i in range(4) for n in WEIGHTS)
```

```python
import functools
import math

import jax
import jax.numpy as jnp
from jax import lax
from jax.experimental import pallas as pl
from jax.experimental.pallas import tpu as pltpu

F32 = jnp.float32
BF16 = jnp.bfloat16

MLA_HEADS = 8
QK_NOPE = 128
QK_ROPE = 64
V_HEAD = 128
Q_LORA = 512
KV_LORA = 512
ROPE_THETA = 10000.0
GDN_HEADS = 8
GDN_DK = 128
GDN_DV = 128
CONV_WIDTH = 4
CHUNK = 64
DEPTH = 2
EPS = 1e-6
ADAM_LR = 0.001
ADAM_B1 = 0.9
ADAM_B2 = 0.999
ADAM_EPS = 1e-08
ADAM_WD = 0.01
ADAM_STEP = 10

N_DEV = 8
AXES = ("x", "y", "c")
LANE = 128
VMEM_LIMIT = 48 * 1024 * 1024
MM_VMEM_BUDGET = 36 * 1024 * 1024
HIGHEST = lax.Precision.HIGHEST

NN = (((1,), (0,)), ((), ()))
NT = (((1,), (1,)), ((), ()))
TN = (((0,), (0,)), ((), ()))


def _cp(sem=None):
    return pltpu.CompilerParams(dimension_semantics=sem, vmem_limit_bytes=VMEM_LIMIT)


def _tile(n, cap):
    if n <= cap:
        return n
    for t in range(cap - cap % LANE, 0, -LANE):
        if n % t == 0:
            return t
    return n


def _rows(t, cap=256):
    return cap if t % cap == 0 else t


def _pad_lanes(n):
    return -(-n // LANE) * LANE


def _sigmoid(x):
    return 1.0 / (1.0 + jnp.exp(-x))


def _softplus(x):
    return jnp.maximum(x, 0.0) + jnp.log(1.0 + jnp.exp(-jnp.abs(x)))


def _tile_slot(n, cap):
    t = _tile(n, cap)
    return n if t < 256 < n <= 1536 else t


def _mm(a, b, dims, out_dtype, name, a_act=None, slots=False):
    if dims == "nn":
        m, k = a.shape
        n = b.shape[-1] * (N_DEV if slots else 1)
    elif dims == "nt":
        m, k = a.shape
        n = b.shape[-2]
    else:
        k, m = a.shape
        n = b.shape[-1]
    tm = _tile(m, 1024)
    tn = _tile_slot(n // N_DEV, 512) if slots and dims != "nt" else _tile(n, 512)
    k_slot = k // N_DEV if slots and dims == "nt" else k

    def vmem_bytes(tk_):
        a_b, b_b = tm * tk_ * a.dtype.itemsize, tk_ * tn * b.dtype.itemsize
        casts = (tm * tk_ * 2 if a.dtype != BF16 else 0) + (tk_ * tn * 2 if b.dtype != BF16 else 0)
        return 2 * (a_b + b_b + tm * tn * jnp.dtype(out_dtype).itemsize) + 2 * tm * tn * 4 + casts

    tk = _tile_slot(k_slot, 1536) if slots and dims == "nt" else _tile(k, 2048)
    while vmem_bytes(tk) > MM_VMEM_BUDGET and tk % (2 * LANE) == 0:
        tk //= 2
    nk = k // tk
    per_n = (n // N_DEV) // tn if slots else 1
    per_k = (k // N_DEV) // tk if slots else 1
    if dims == "tn":
        a_spec = pl.BlockSpec((tk, tm), lambda i, j, kk: (kk, i))
    else:
        a_spec = pl.BlockSpec((tm, tk), lambda i, j, kk: (i, kk))
    if dims == "nt":
        if slots:
            b_spec = pl.BlockSpec((None, tn, tk), lambda i, j, kk: (kk // per_k, j, kk % per_k))
        else:
            b_spec = pl.BlockSpec((tn, tk), lambda i, j, kk: (j, kk))
    elif dims == "nn" and slots:
        b_spec = pl.BlockSpec((None, tk, tn), lambda i, j, kk: (j // per_n, kk, j % per_n))
    else:
        b_spec = pl.BlockSpec((tk, tn), lambda i, j, kk: (kk, j))
    if dims == "tn" and slots:
        out_spec = pl.BlockSpec((None, tm, tn), lambda i, j, kk: (j // per_n, i, j % per_n))
        out_shape = jax.ShapeDtypeStruct((N_DEV, m, n // N_DEV), out_dtype)
    else:
        out_spec = pl.BlockSpec((tm, tn), lambda i, j, kk: (i, j))
        out_shape = jax.ShapeDtypeStruct((m, n), out_dtype)
    dn = {"nn": NN, "nt": NT, "tn": TN}[dims]

    def product(a_ref, b_ref):
        av = a_ref[...]
        if a_act == "silu":
            av = av * _sigmoid(av)
        return lax.dot_general(av.astype(BF16), b_ref[...].astype(BF16), dn, preferred_element_type=F32)

    def body_one(a_ref, b_ref, o_ref):
        o_ref[...] = product(a_ref, b_ref).astype(o_ref.dtype)

    def body_acc(a_ref, b_ref, o_ref, acc_ref):
        kk = pl.program_id(2)

        @pl.when(kk == 0)
        def _():
            acc_ref[...] = jnp.zeros_like(acc_ref)

        acc_ref[...] += product(a_ref, b_ref)

        @pl.when(kk == nk - 1)
        def _():
            o_ref[...] = acc_ref[...].astype(o_ref.dtype)

    return pl.pallas_call(
        body_one if nk == 1 else body_acc, name=name, grid=(m // tm, n // tn, nk),
        in_specs=[a_spec, b_spec], out_specs=out_spec, out_shape=out_shape,
        scratch_shapes=[] if nk == 1 else [pltpu.VMEM((tm, tn), F32)],
        compiler_params=_cp(("parallel", "parallel", "arbitrary")),
    )(a, b)


@functools.partial(jax.custom_vjp, nondiff_argnums=(2, 3))
def mm(a, b, tag, out_dtype):
    return _mm(a, b, "nn", out_dtype, "mm_" + tag, slots=b.ndim == 3)


def _mm_f(a, b, tag, out_dtype):
    return mm(a, b, tag, out_dtype), (a, b)


def _mm_b(tag, out_dtype, res, g):
    a, b = res
    slots = b.ndim == 3
    da = _mm(g, b, "nt", a.dtype, "mm_" + tag + "_da", slots=slots)
    db = _mm(a, g, "tn", b.dtype, "mm_" + tag + "_db", slots=slots)
    return da, db


mm.defvjp(_mm_f, _mm_b)


def _norm_fwd_call(x, nw, sc, sh, name):
    t, d = x.shape
    tr = _rows(t)
    mod = sc is not None
    row = pl.BlockSpec((tr, d), lambda i: (i, 0))
    vec = pl.BlockSpec((1, d), lambda i: (0, 0))

    def body(*refs):
        if mod:
            x_ref, nw_ref, sc_ref, sh_ref, o_ref = refs
        else:
            x_ref, nw_ref, o_ref = refs
        xv = x_ref[...].astype(F32)
        r = lax.rsqrt(jnp.mean(xv * xv, axis=-1, keepdims=True) + EPS)
        y = (xv * r) * nw_ref[...]
        if mod:
            y = y * (1.0 + sc_ref[...]) + sh_ref[...]
        o_ref[...] = y.astype(o_ref.dtype)

    args = (x, nw, sc, sh) if mod else (x, nw)
    return pl.pallas_call(
        body, name=name, grid=(t // tr,),
        in_specs=[row] + [vec] * (len(args) - 1), out_specs=row,
        out_shape=jax.ShapeDtypeStruct((t, d), BF16),
        compiler_params=_cp(("parallel",)),
    )(*args)


def _norm_bwd_call(x, nw, sc, dh, name):
    t, d = x.shape
    tr = _rows(t)
    mod = sc is not None
    row = pl.BlockSpec((tr, d), lambda i: (i, 0))
    vec = pl.BlockSpec((1, d), lambda i: (0, 0))

    def body(*refs):
        if mod:
            x_ref, nw_ref, sc_ref, dh_ref, dx_ref, dnw_ref, dsc_ref, dsh_ref = refs
        else:
            x_ref, nw_ref, dh_ref, dx_ref, dnw_ref = refs
        i = pl.program_id(0)
        xv = x_ref[...].astype(F32)
        dh = dh_ref[...].astype(F32)
        r = lax.rsqrt(jnp.mean(xv * xv, axis=-1, keepdims=True) + EPS)
        y = xv * r
        a = nw_ref[...] * (1.0 + sc_ref[...]) if mod else nw_ref[...]
        dy = dh * a
        dx_ref[...] = (r * (dy - y * jnp.mean(dy * y, axis=-1, keepdims=True))).astype(dx_ref.dtype)
        da = jnp.sum(dh * y, axis=0, keepdims=True)

        @pl.when(i == 0)
        def _():
            dnw_ref[...] = jnp.zeros_like(dnw_ref)
            if mod:
                dsc_ref[...] = jnp.zeros_like(dsc_ref)
                dsh_ref[...] = jnp.zeros_like(dsh_ref)

        if mod:
            dnw_ref[...] += da * (1.0 + sc_ref[...])
            dsc_ref[...] += da * nw_ref[...]
            dsh_ref[...] += jnp.sum(dh, axis=0, keepdims=True)
        else:
            dnw_ref[...] += da

    args = (x, nw, sc, dh) if mod else (x, nw, dh)
    n_vec = 3 if mod else 1
    return pl.pallas_call(
        body, name=name, grid=(t // tr,),
        in_specs=[row] + [vec] * (len(args) - 2) + [row],
        out_specs=[row] + [vec] * n_vec,
        out_shape=[jax.ShapeDtypeStruct((t, d), x.dtype)] + [jax.ShapeDtypeStruct((1, d), F32)] * n_vec,
        compiler_params=_cp(("arbitrary",)),
    )(*args)


@functools.partial(jax.custom_vjp, nondiff_argnums=(4,))
def ada_norm(x, nw, sc, sh, tag):
    return _norm_fwd_call(x, nw, sc, sh, "adanorm_" + tag)


def _ada_norm_f(x, nw, sc, sh, tag):
    return _norm_fwd_call(x, nw, sc, sh, "adanorm_" + tag), (x, nw, sc)


def _ada_norm_b(tag, res, dh):
    x, nw, sc = res
    dx, dnw, dsc, dsh = _norm_bwd_call(x, nw, sc, dh, "adanorm_" + tag + "_bwd")
    return dx, dnw, dsc, dsh


ada_norm.defvjp(_ada_norm_f, _ada_norm_b)


@functools.partial(jax.custom_vjp, nondiff_argnums=(2,))
def rms_norm(x, nw, tag):
    return _norm_fwd_call(x, nw, None, None, "rms_" + tag)


def _rms_norm_f(x, nw, tag):
    return _norm_fwd_call(x, nw, None, None, "rms_" + tag), (x, nw)


def _rms_norm_b(tag, res, dh):
    x, nw = res
    dx, dnw = _norm_bwd_call(x, nw, None, dh, "rms_" + tag + "_bwd")
    return dx, dnw


rms_norm.defvjp(_rms_norm_f, _rms_norm_b)


def _gate_mix_fwd_call(gl, ya, yb, name):
    t, d = ya.shape
    tr = _rows(t)
    row = pl.BlockSpec((tr, d), lambda i: (i, 0))

    def body(ga_ref, gb_ref, ya_ref, yb_ref, o_ref):
        o_ref[...] = (_sigmoid(ga_ref[...].astype(F32)) * ya_ref[...].astype(F32)
                      + _sigmoid(gb_ref[...].astype(F32)) * yb_ref[...].astype(F32)).astype(o_ref.dtype)

    return pl.pallas_call(
        body, name=name, grid=(t // tr,),
        in_specs=[row, pl.BlockSpec((tr, d), lambda i: (i, 1)), row, row], out_specs=row,
        out_shape=jax.ShapeDtypeStruct((t, d), BF16),
        compiler_params=_cp(("parallel",)),
    )(gl, gl, ya, yb)


def _gate_mix_bwd_call(gl, ya, yb, dm, name):
    t, d = ya.shape
    tr = _rows(t)
    row = pl.BlockSpec((tr, d), lambda i: (i, 0))
    wide = pl.BlockSpec((tr, 2 * d), lambda i: (i, 0))

    def body(gl_ref, ya_ref, yb_ref, dm_ref, dgl_ref, dya_ref, dyb_ref):
        dm = dm_ref[...].astype(F32)
        ga = _sigmoid(gl_ref[:, :d].astype(F32))
        gb = _sigmoid(gl_ref[:, d:].astype(F32))
        dya_ref[...] = (dm * ga).astype(dya_ref.dtype)
        dyb_ref[...] = (dm * gb).astype(dyb_ref.dtype)
        dgl_ref[:, :d] = (dm * ya_ref[...].astype(F32) * ga * (1.0 - ga)).astype(dgl_ref.dtype)
        dgl_ref[:, d:] = (dm * yb_ref[...].astype(F32) * gb * (1.0 - gb)).astype(dgl_ref.dtype)

    return pl.pallas_call(
        body, name=name, grid=(t // tr,),
        in_specs=[wide, row, row, row], out_specs=[wide, row, row],
        out_shape=[jax.ShapeDtypeStruct((t, 2 * d), gl.dtype), jax.ShapeDtypeStruct((t, d), ya.dtype),
                   jax.ShapeDtypeStruct((t, d), yb.dtype)],
        compiler_params=_cp(("parallel",)),
    )(gl, ya, yb, dm)


@functools.partial(jax.custom_vjp, nondiff_argnums=(3,))
def gate_mix(gl, ya, yb, tag):
    return _gate_mix_fwd_call(gl, ya, yb, "gatemix_" + tag)


def _gate_mix_f(gl, ya, yb, tag):
    return _gate_mix_fwd_call(gl, ya, yb, "gatemix_" + tag), (gl, ya, yb)


def _gate_mix_b(tag, res, dm):
    return tuple(_gate_mix_bwd_call(*res, dm, "gatemix_" + tag + "_bwd"))


gate_mix.defvjp(_gate_mix_f, _gate_mix_b)


def _resid_fwd_call(x, gt, m, name):
    t, d = x.shape
    tr = _rows(t)
    row = pl.BlockSpec((tr, d), lambda i: (i, 0))
    vec = pl.BlockSpec((1, d), lambda i: (0, 0))

    def body(x_ref, gt_ref, m_ref, o_ref):
        o_ref[...] = x_ref[...] + gt_ref[...] * m_ref[...]

    return pl.pallas_call(
        body, name=name, grid=(t // tr,), in_specs=[row, vec, row], out_specs=row,
        out_shape=jax.ShapeDtypeStruct((t, d), F32), compiler_params=_cp(("parallel",)),
    )(x, gt, m)


def _resid_bwd_call(gt, m, g, name):
    t, d = m.shape
    tr = _rows(t)
    row = pl.BlockSpec((tr, d), lambda i: (i, 0))
    vec = pl.BlockSpec((1, d), lambda i: (0, 0))

    def body(gt_ref, m_ref, g_ref, dm_ref, dgt_ref):
        i = pl.program_id(0)
        g = g_ref[...]
        dm_ref[...] = g * gt_ref[...]

        @pl.when(i == 0)
        def _():
            dgt_ref[...] = jnp.zeros_like(dgt_ref)

        dgt_ref[...] += jnp.sum(g * m_ref[...], axis=0, keepdims=True)

    return pl.pallas_call(
        body, name=name, grid=(t // tr,), in_specs=[vec, row, row], out_specs=[row, vec],
        out_shape=[jax.ShapeDtypeStruct((t, d), F32), jax.ShapeDtypeStruct((1, d), F32)],
        compiler_params=_cp(("arbitrary",)),
    )(gt, m, g)


@functools.partial(jax.custom_vjp, nondiff_argnums=(3,))
def resid(x, gt, m, tag):
    return _resid_fwd_call(x, gt, m, "resid_" + tag)


def _resid_f(x, gt, m, tag):
    return _resid_fwd_call(x, gt, m, "resid_" + tag), (gt, m)


def _resid_b(tag, res, g):
    gt, m = res
    dm, dgt = _resid_bwd_call(gt, m, g, "resid_" + tag + "_bwd")
    return g, dgt, dm


resid.defvjp(_resid_f, _resid_b)


def _swiglu_fwd_call(gu, name):
    t, f2 = gu.shape
    f = f2 // 2
    tr = _rows(t, 128)
    half = pl.BlockSpec((tr, f), lambda i: (i, 0))

    def body(g_ref, u_ref, o_ref):
        g = g_ref[...].astype(F32)
        o_ref[...] = (g * _sigmoid(g) * u_ref[...].astype(F32)).astype(o_ref.dtype)

    return pl.pallas_call(
        body, name=name, grid=(t // tr,),
        in_specs=[half, pl.BlockSpec((tr, f), lambda i: (i, 1))], out_specs=half,
        out_shape=jax.ShapeDtypeStruct((t, f), BF16), compiler_params=_cp(("parallel",)),
    )(gu, gu)


def _swiglu_bwd_call(gu, da, name):
    t, f2 = gu.shape
    f = f2 // 2
    tr = _rows(t, 128)
    wide = pl.BlockSpec((tr, f2), lambda i: (i, 0))

    def body(gu_ref, da_ref, dgu_ref):
        g = gu_ref[:, :f].astype(F32)
        u = gu_ref[:, f:].astype(F32)
        da = da_ref[...].astype(F32)
        s = _sigmoid(g)
        dgu_ref[:, :f] = (da * u * s * (1.0 + g * (1.0 - s))).astype(dgu_ref.dtype)
        dgu_ref[:, f:] = (da * g * s).astype(dgu_ref.dtype)

    return pl.pallas_call(
        body, name=name, grid=(t // tr,),
        in_specs=[wide, pl.BlockSpec((tr, f), lambda i: (i, 0))], out_specs=wide,
        out_shape=jax.ShapeDtypeStruct((t, f2), gu.dtype), compiler_params=_cp(("parallel",)),
    )(gu, da)


@functools.partial(jax.custom_vjp, nondiff_argnums=(1,))
def swiglu(gu, tag):
    return _swiglu_fwd_call(gu, "swiglu_" + tag)


def _swiglu_f(gu, tag):
    return _swiglu_fwd_call(gu, "swiglu_" + tag), (gu,)


def _swiglu_b(tag, res, da):
    return (_swiglu_bwd_call(res[0], da, "swiglu_" + tag + "_bwd"),)


swiglu.defvjp(_swiglu_f, _swiglu_b)


def loss_head(x, fw, tgt):
    t, d = x.shape
    tr = _rows(t)
    row = pl.BlockSpec((tr, d), lambda i: (i, 0))
    vec = pl.BlockSpec((1, d), lambda i: (0, 0))
    tile = pl.BlockSpec((8, LANE), lambda i: (0, 0))

    def body(x_ref, fw_ref, tgt_ref, loss_ref, dx_ref, dfw_ref):
        i = pl.program_id(0)
        xv = x_ref[...]
        fw = fw_ref[...]
        r = lax.rsqrt(jnp.mean(xv * xv, axis=-1, keepdims=True) + EPS)
        yh = xv * r
        e = yh * fw - tgt_ref[...]
        dy = e * (1.0 / d)
        dyw = dy * fw
        dx_ref[...] = r * (dyw - yh * jnp.mean(dyw * yh, axis=-1, keepdims=True))

        @pl.when(i == 0)
        def _():
            loss_ref[...] = jnp.zeros_like(loss_ref)
            dfw_ref[...] = jnp.zeros_like(dfw_ref)

        loss_ref[...] += 0.5 * jnp.sum(jnp.mean(e * e, axis=-1, keepdims=True))
        dfw_ref[...] += jnp.sum(dy * yh, axis=0, keepdims=True)

    return pl.pallas_call(
        body, name="loss_head", grid=(t // tr,), in_specs=[row, vec, row],
        out_specs=[tile, row, vec],
        out_shape=[jax.ShapeDtypeStruct((8, LANE), F32), jax.ShapeDtypeStruct((t, d), F32),
                   jax.ShapeDtypeStruct((1, d), F32)],
        compiler_params=_cp(("arbitrary",)),
    )(x, fw, tgt)


def _attn_scores(qn_ref, qr_ref, kn_ref, kr_ref, diag):
    tq = qn_ref.shape[0]
    s = lax.dot_general(qn_ref[...].astype(BF16), kn_ref[...].astype(BF16), NT, preferred_element_type=F32)
    s += lax.dot_general(qr_ref[...].astype(BF16), kr_ref[...].astype(BF16), NT, preferred_element_type=F32)
    s = s * (QK_NOPE + QK_ROPE) ** -0.5
    if diag:
        rows = lax.broadcasted_iota(jnp.int32, (tq, tq), 0)
        cols = lax.broadcasted_iota(jnp.int32, (tq, tq), 1)
        s = jnp.where(cols <= rows, s, -1e30)
    return s


def _attn_fwd_call(qn, qr, kv, kr, name):
    t = qn.shape[0]
    h_n = MLA_HEADS
    tq = _rows(t, 512)
    nq = t // tq
    assert V_HEAD == LANE and tq % LANE == 0

    def body(qn_ref, qr_ref, kn_ref, v_ref, kr_ref, o_ref, lse_ref, m_scr, l_scr, acc_scr):
        i, j = pl.program_id(1), pl.program_id(2)

        @pl.when(j == 0)
        def _():
            m_scr[...] = jnp.full_like(m_scr, -1e30)
            l_scr[...] = jnp.zeros_like(l_scr)
            acc_scr[...] = jnp.zeros_like(acc_scr)

        def step(diag):
            s = _attn_scores(qn_ref, qr_ref, kn_ref, kr_ref, diag)
            m_old = m_scr[...]
            m_new = jnp.maximum(m_old, jnp.max(s, axis=-1, keepdims=True))
            p = jnp.exp(s - jnp.tile(m_new, (1, tq // LANE)))
            alpha = jnp.exp(m_old - m_new)
            l_scr[...] = alpha * l_scr[...] + jnp.sum(p, axis=-1, keepdims=True)
            acc_scr[...] = alpha * acc_scr[...] + jnp.dot(p.astype(BF16), v_ref[...].astype(BF16),
                                                           preferred_element_type=F32)
            m_scr[...] = m_new

        @pl.when(j < i)
        def _():
            step(False)

        @pl.when(j == i)
        def _():
            step(True)
            o_ref[...] = (acc_scr[...] / l_scr[...]).astype(o_ref.dtype)
            lse_ref[...] = (m_scr[...] + jnp.log(l_scr[...]))[:, :1]

    return pl.pallas_call(
        body, name=name, grid=(h_n, nq, nq),
        in_specs=[
            pl.BlockSpec((tq, QK_NOPE), lambda h, i, j: (i, h)),
            pl.BlockSpec((None, tq, QK_ROPE), lambda h, i, j: (h, i, 0)),
            pl.BlockSpec((tq, QK_NOPE), lambda h, i, j: (jnp.minimum(j, i), 2 * h)),
            pl.BlockSpec((tq, V_HEAD), lambda h, i, j: (jnp.minimum(j, i), 2 * h + 1)),
            pl.BlockSpec((tq, QK_ROPE), lambda h, i, j: (jnp.minimum(j, i), 0)),
        ],
        out_specs=[
            pl.BlockSpec((tq, V_HEAD), lambda h, i, j: (i, h)),
            pl.BlockSpec((None, tq, 1), lambda h, i, j: (h, i, 0)),
        ],
        out_shape=[jax.ShapeDtypeStruct((t, h_n * V_HEAD), BF16),
                   jax.ShapeDtypeStruct((h_n, t, 1), F32)],
        scratch_shapes=[pltpu.VMEM((tq, LANE), F32), pltpu.VMEM((tq, LANE), F32),
                        pltpu.VMEM((tq, V_HEAD), F32)],
        compiler_params=_cp(("parallel", "parallel", "arbitrary")),
    )(qn, qr, kv, kv, kr)


def _attn_bwd_call(qn, qr, kv, kr, o, lse, do, name):
    t = qn.shape[0]
    h_n = MLA_HEADS
    tq = _rows(t, 512)
    nq = t // tq
    scale = (QK_NOPE + QK_ROPE) ** -0.5

    def body(qn_ref, qr_ref, kn_ref, v_ref, kr_ref, o_ref, lse_ref, do_ref,
             dqn_ref, dqr_ref, dkv_ref, dkr_ref, dqn_scr, dqr_scr, dkn_scr, dv_scr, dkr_scr):
        j, i = pl.program_id(1), pl.program_id(2)

        @pl.when(jnp.logical_and(j == 0, i == 0))
        def _():
            dqn_scr[...] = jnp.zeros_like(dqn_scr)
            dqr_scr[...] = jnp.zeros_like(dqr_scr)

        @pl.when(i == 0)
        def _():
            dkn_scr[...] = jnp.zeros_like(dkn_scr)
            dv_scr[...] = jnp.zeros_like(dv_scr)
            dkr_scr[...] = jnp.zeros_like(dkr_scr)

        def step(diag):
            qn_b = qn_ref[...].astype(BF16)
            qr_b = qr_ref[...].astype(BF16)
            kn_b = kn_ref[...].astype(BF16)
            kr_b = kr_ref[...].astype(BF16)
            do_b = do_ref[...]
            p = jnp.exp(_attn_scores(qn_ref, qr_ref, kn_ref, kr_ref, diag) - lse_ref[...])
            delta = jnp.sum(do_b.astype(F32) * o_ref[...].astype(F32), axis=-1, keepdims=True)
            dp = lax.dot_general(do_b, v_ref[...].astype(BF16), NT, preferred_element_type=F32)
            ds = (p * (dp - delta) * scale).astype(BF16)
            p_b = p.astype(BF16)
            dv_scr[...] += lax.dot_general(p_b, do_b, TN, preferred_element_type=F32)
            dkn_scr[...] += lax.dot_general(ds, qn_b, TN, preferred_element_type=F32)
            dkr_scr[...] += lax.dot_general(ds, qr_b, TN, preferred_element_type=F32)
            sl = pl.ds(pl.multiple_of(i * tq, tq), tq)
            dqn_scr[sl, :] += jnp.dot(ds, kn_b, preferred_element_type=F32)
            dqr_scr[sl, :] += jnp.dot(ds, kr_b, preferred_element_type=F32)

        @pl.when(i > j)
        def _():
            step(False)

        @pl.when(i == j)
        def _():
            step(True)

        @pl.when(i == nq - 1)
        def _():
            dkv_ref[:, :QK_NOPE] = dkn_scr[...].astype(dkv_ref.dtype)
            dkv_ref[:, QK_NOPE:] = dv_scr[...].astype(dkv_ref.dtype)
            dkr_ref[...] = dkr_scr[...]

        @pl.when(jnp.logical_and(j == nq - 1, i == nq - 1))
        def _():
            dqn_ref[...] = dqn_scr[...].astype(dqn_ref.dtype)
            dqr_ref[...] = dqr_scr[...].astype(dqr_ref.dtype)

    qi = lambda j, i: jnp.maximum(i, j)
    return pl.pallas_call(
        body, name=name, grid=(h_n, nq, nq),
        in_specs=[
            pl.BlockSpec((tq, QK_NOPE), lambda h, j, i: (qi(j, i), h)),
            pl.BlockSpec((None, tq, QK_ROPE), lambda h, j, i: (h, qi(j, i), 0)),
            pl.BlockSpec((tq, QK_NOPE), lambda h, j, i: (j, 2 * h)),
            pl.BlockSpec((tq, V_HEAD), lambda h, j, i: (j, 2 * h + 1)),
            pl.BlockSpec((tq, QK_ROPE), lambda h, j, i: (j, 0)),
            pl.BlockSpec((tq, V_HEAD), lambda h, j, i: (qi(j, i), h)),
            pl.BlockSpec((None, tq, 1), lambda h, j, i: (h, qi(j, i), 0)),
            pl.BlockSpec((tq, V_HEAD), lambda h, j, i: (qi(j, i), h)),
        ],
        out_specs=[
            pl.BlockSpec((t, QK_NOPE), lambda h, j, i: (0, h)),
            pl.BlockSpec((None, t, QK_ROPE), lambda h, j, i: (h, 0, 0)),
            pl.BlockSpec((tq, QK_NOPE + V_HEAD), lambda h, j, i: (j, h)),
            pl.BlockSpec((None, tq, QK_ROPE), lambda h, j, i: (h, j, 0)),
        ],
        out_shape=[jax.ShapeDtypeStruct((t, h_n * QK_NOPE), qn.dtype),
                   jax.ShapeDtypeStruct((h_n, t, QK_ROPE), qr.dtype),
                   jax.ShapeDtypeStruct((t, h_n * (QK_NOPE + V_HEAD)), kv.dtype),
                   jax.ShapeDtypeStruct((h_n, t, QK_ROPE), F32)],
        scratch_shapes=[pltpu.VMEM((t, QK_NOPE), F32), pltpu.VMEM((t, QK_ROPE), F32),
                        pltpu.VMEM((tq, QK_NOPE), F32), pltpu.VMEM((tq, V_HEAD), F32),
                        pltpu.VMEM((tq, QK_ROPE), F32)],
        compiler_params=_cp(("parallel", "arbitrary", "arbitrary")),
    )(qn, qr, kv, kv, kr, o, lse, do)


@functools.partial(jax.custom_vjp, nondiff_argnums=(4,))
def attention(qn, qr, kv, kr, tag):
    return _attn_fwd_call(qn, qr, kv, kr, "attn_" + tag)[0]


def _attention_f(qn, qr, kv, kr, tag):
    o, lse = _attn_fwd_call(qn, qr, kv, kr, "attn_" + tag)
    return o, (qn, qr, kv, kr, o, lse)


def _attention_b(tag, res, do):
    dqn, dqr, dkv, dkr_h = _attn_bwd_call(*res, do, "attn_" + tag + "_bwd")
    return dqn, dqr, dkv, jnp.sum(dkr_h, axis=0).astype(res[3].dtype)


attention.defvjp(_attention_f, _attention_b)


def _shift_down(u, s):
    if s == 0:
        return u
    t = u.shape[0]
    rolled = pltpu.roll(u, s, 0)
    return jnp.where(lax.broadcasted_iota(jnp.int32, u.shape, 0) >= s, rolled, 0.0)


def _shift_up(u, s):
    if s == 0:
        return u
    t = u.shape[0]
    rolled = pltpu.roll(u, t - s, 0)
    return jnp.where(lax.broadcasted_iota(jnp.int32, u.shape, 0) < t - s, rolled, 0.0)


def _conv_blocks(t, c3):
    p = c3 // 3
    tc = _tile(p, 512)
    per = p // tc
    return p, tc, per


def _conv_fwd_call(u, w, name):
    t, c3 = u.shape
    p, tc, per = _conv_blocks(t, c3)

    def body(u_ref, w_ref, o_ref):
        u = u_ref[...].astype(F32)
        y = jnp.zeros_like(u)
        for j in range(CONV_WIDTH):
            y = y + w_ref[j:j + 1, :] * _shift_down(u, CONV_WIDTH - 1 - j)
        o_ref[...] = y * _sigmoid(y)

    return pl.pallas_call(
        body, name=name, grid=(c3 // tc,),
        in_specs=[pl.BlockSpec((t, tc), lambda cb: (0, cb)),
                  pl.BlockSpec((CONV_WIDTH, tc), lambda cb: (0, cb))],
        out_specs=pl.BlockSpec((None, t, tc), lambda cb: (cb // per, 0, cb % per)),
        out_shape=jax.ShapeDtypeStruct((3, t, p), F32),
        compiler_params=_cp(("parallel",)),
    )(u, w)


def _conv_bwd_call(u, w, do, name):
    t, c3 = u.shape
    p, tc, per = _conv_blocks(t, c3)

    def body(u_ref, w_ref, do_ref, du_ref, dw_ref):
        u = u_ref[...].astype(F32)
        shifted = [_shift_down(u, CONV_WIDTH - 1 - j) for j in range(CONV_WIDTH)]
        y = jnp.zeros_like(u)
        for j in range(CONV_WIDTH):
            y = y + w_ref[j:j + 1, :] * shifted[j]
        s = _sigmoid(y)
        dy = do_ref[...] * s * (1.0 + y * (1.0 - s))
        du = jnp.zeros_like(u)
        for j in range(CONV_WIDTH):
            du = du + w_ref[j:j + 1, :] * _shift_up(dy, CONV_WIDTH - 1 - j)
            dw_ref[j:j + 1, :] = jnp.sum(dy * shifted[j], axis=0, keepdims=True)
        du_ref[...] = du.astype(du_ref.dtype)

    return pl.pallas_call(
        body, name=name, grid=(c3 // tc,),
        in_specs=[pl.BlockSpec((t, tc), lambda cb: (0, cb)),
                  pl.BlockSpec((CONV_WIDTH, tc), lambda cb: (0, cb)),
                  pl.BlockSpec((None, t, tc), lambda cb: (cb // per, 0, cb % per))],
        out_specs=[pl.BlockSpec((t, tc), lambda cb: (0, cb)),
                   pl.BlockSpec((CONV_WIDTH, tc), lambda cb: (0, cb))],
        out_shape=[jax.ShapeDtypeStruct((t, c3), u.dtype), jax.ShapeDtypeStruct((CONV_WIDTH, c3), F32)],
        compiler_params=_cp(("parallel",)),
    )(u, w, do)


@functools.partial(jax.custom_vjp, nondiff_argnums=(2,))
def conv_silu(u, w, tag):
    return _conv_fwd_call(u, w, "conv_" + tag)


def _conv_silu_f(u, w, tag):
    return _conv_fwd_call(u, w, "conv_" + tag), (u, w)


def _conv_silu_b(tag, res, do):
    return tuple(_conv_bwd_call(*res, do, "conv_" + tag + "_bwd"))


conv_silu.defvjp(_conv_silu_f, _conv_silu_b)


BNN = (((2,), (1,)), ((0,), (0,)))
BNT = (((2,), (2,)), ((0,), (0,)))
BTN = (((1,), (1,)), ((0,), (0,)))


def _hdot(a, b, dn=BNN):
    return lax.dot_general(a, b, dn, precision=HIGHEST, preferred_element_type=F32)


def _bf16_dot(a, b, dn):
    return lax.dot_general(a.astype(BF16), b.astype(BF16), dn, preferred_element_type=F32)


@functools.partial(jax.custom_vjp, nondiff_argnums=(2,))
def _bdot(a, b, dn=BNN):
    return _bf16_dot(a, b, dn)


def _bdot_f(a, b, dn):
    return _bf16_dot(a, b, dn), (a, b)


def _bdot_b(dn, res, g):
    a, b = res
    if dn == BNN:
        return _bf16_dot(g, b, BNT), _bf16_dot(a, g, BTN)
    if dn == BNT:
        return _bf16_dot(g, b, BNN), _bf16_dot(g, a, BTN)
    return _bf16_dot(b, g, BNT), _bf16_dot(a, g, BNN)


_bdot.defvjp(_bdot_f, _bdot_b)


GDN_HEADS_PER_STEP = 8
GDN_HEADS_PER_STEP_BWD = 4


def _gdn_chunk(q, k, v, z, bl, al, a_log, dtb, gn, s):
    b, c = q.shape[0], q.shape[1]
    ri = lax.broadcasted_iota(jnp.int32, (c, c), 0)
    ci = lax.broadcasted_iota(jnp.int32, (c, c), 1)
    lower = (ri >= ci)[None]
    strict = (ri > ci)[None]
    low_incl = jnp.broadcast_to((ri >= ci).astype(F32), (b, c, c))
    up_incl = jnp.broadcast_to((ri <= ci).astype(F32), (b, c, c))
    eye = (ri == ci).astype(F32)[None]

    q = q * lax.rsqrt(jnp.sum(q * q, axis=-1, keepdims=True) + EPS) * (GDN_DK ** -0.5)
    k = k * lax.rsqrt(jnp.sum(k * k, axis=-1, keepdims=True) + EPS)
    beta = _sigmoid(bl)
    g = -jnp.exp(a_log) * _softplus(al + dtb)
    g_w = jnp.broadcast_to(g, (b, c, LANE))
    gc = _hdot(low_incl, g_w)
    gr = _hdot(g_w[:, :, :c], up_incl, BTN)
    diff = gc[:, :, :c] - gr
    decay = jnp.where(lower, jnp.exp(jnp.where(lower, diff, 0.0)), 0.0)
    kb = k * beta
    lmat = jnp.where(strict, _bdot(kb, k, BNT) * decay, 0.0)
    inv = eye - lmat
    pw = lmat
    for _ in range(int(math.log2(c)) - 1):
        pw = _hdot(pw, pw)
        inv = _hdot(inv, eye + pw)
    eg = jnp.exp(gc)
    u = _hdot(inv, v * beta)
    w = _hdot(inv, kb * eg)
    attn = jnp.where(lower, _bdot(q, k, BNT) * decay, 0.0)
    v_new = u - _bdot(w, s)
    o = _bdot(q * eg, s) + _bdot(attn, v_new)
    g_last = jnp.sum(g_w, axis=1, keepdims=True)
    k_dec = k * jnp.exp(g_last - gc)
    s_new = s * jnp.exp(g_last) + _bdot(k_dec, v_new, BTN)
    on = o * lax.rsqrt(jnp.mean(o * o, axis=-1, keepdims=True) + EPS) * gn
    return on * (z * _sigmoid(z)), s_new


def _gdn_specs(n_chunks, hb, rev):
    c = CHUNK
    nn = (lambda n: n_chunks - 1 - n) if rev else (lambda n: n)
    plane = lambda pidx: pl.BlockSpec((None, c, hb * GDN_DK), lambda hg, n: (pidx, nn(n), hg))
    col = pl.BlockSpec((hb, c, 1), lambda hg, n: (hg, nn(n), 0))
    scal = pl.BlockSpec((hb, 1, 1), lambda hg, n: (hg, 0, 0))
    zspec = pl.BlockSpec((c, hb * GDN_DV), lambda hg, n: (nn(n), hg))
    gnspec = pl.BlockSpec((1, GDN_DV), lambda hg, n: (0, 0))
    sspec = pl.BlockSpec((hb, None, GDN_DK, GDN_DV), lambda hg, n: (hg, nn(n), 0, 0))
    return plane, col, scal, zspec, gnspec, sspec


def _heads_per_step(want):
    return math.gcd(want, GDN_HEADS)


def _heads(ref, hb):
    return jnp.stack([ref[:, j * GDN_DK:(j + 1) * GDN_DK] for j in range(hb)])


def _gdn_fwd_call(qkv, z, bl, al, a_log, dtb, gn, name):
    t = z.shape[0]
    h_n = GDN_HEADS
    hb = _heads_per_step(GDN_HEADS_PER_STEP)
    n_chunks = t // CHUNK
    plane, col, scal, zspec, gnspec, sspec = _gdn_specs(n_chunks, hb, False)

    def body(q_ref, k_ref, v_ref, z_ref, bl_ref, al_ref, a_ref, dtb_ref, gn_ref, o_ref, sall_ref, s_scr):
        n = pl.program_id(1)

        @pl.when(n == 0)
        def _():
            s_scr[...] = jnp.zeros_like(s_scr)

        s = s_scr[...]
        sall_ref[...] = s
        o, s_new = _gdn_chunk(_heads(q_ref, hb), _heads(k_ref, hb), _heads(v_ref, hb),
                              _heads(z_ref, hb).astype(F32),
                              bl_ref[...], al_ref[...], a_ref[...], dtb_ref[...], gn_ref[...], s)
        for j in range(hb):
            o_ref[:, j * GDN_DV:(j + 1) * GDN_DV] = o[j].astype(o_ref.dtype)
        s_scr[...] = s_new

    return pl.pallas_call(
        body, name=name, grid=(h_n // hb, n_chunks),
        in_specs=[plane(0), plane(1), plane(2), zspec, col, col, scal, scal, gnspec],
        out_specs=[zspec, sspec],
        out_shape=[jax.ShapeDtypeStruct((t, h_n * GDN_DV), BF16),
                   jax.ShapeDtypeStruct((h_n, n_chunks, GDN_DK, GDN_DV), F32)],
        scratch_shapes=[pltpu.VMEM((hb, GDN_DK, GDN_DV), F32)],
        compiler_params=_cp(("parallel", "arbitrary")),
    )(qkv, qkv, qkv, z, bl, al, a_log, dtb, gn)


def _gdn_bwd_call(qkv, z, bl, al, a_log, dtb, gn, sall, do, name):
    t = z.shape[0]
    h_n = GDN_HEADS
    hb = _heads_per_step(GDN_HEADS_PER_STEP_BWD)
    n_chunks = t // CHUNK
    c = CHUNK
    plane, col, scal, zspec, gnspec, sspec = _gdn_specs(n_chunks, hb, True)
    dplanes = pl.BlockSpec((3, c, hb * GDN_DK), lambda hg, n: (0, n_chunks - 1 - n, hg))
    gnh = pl.BlockSpec((None, 1, GDN_DV), lambda hg, n: (hg, 0, 0))

    def body(q_ref, k_ref, v_ref, z_ref, bl_ref, al_ref, a_ref, dtb_ref, gn_ref, s_ref, do_ref,
             dqkv_ref, dz_ref, dbl_ref, dal_ref, da_ref, ddtb_ref, dgn_ref, ds_scr):
        n = pl.program_id(1)

        @pl.when(n == 0)
        def _():
            ds_scr[...] = jnp.zeros_like(ds_scr)
            da_ref[...] = jnp.zeros_like(da_ref)
            ddtb_ref[...] = jnp.zeros_like(ddtb_ref)
            dgn_ref[...] = jnp.zeros_like(dgn_ref)

        _, vjp = jax.vjp(_gdn_chunk, _heads(q_ref, hb), _heads(k_ref, hb), _heads(v_ref, hb),
                              _heads(z_ref, hb).astype(F32),
                         bl_ref[...], al_ref[...], a_ref[...], dtb_ref[...], gn_ref[...], s_ref[...])
        dq, dk, dv, dz, dbl, dal, da, ddtb, dgn, ds = vjp((_heads(do_ref, hb).astype(F32), ds_scr[...]))
        for j in range(hb):
            hs = slice(j * GDN_DK, (j + 1) * GDN_DK)
            dqkv_ref[0, :, hs] = dq[j]
            dqkv_ref[1, :, hs] = dk[j]
            dqkv_ref[2, :, hs] = dv[j]
            dz_ref[:, hs] = dz[j].astype(dz_ref.dtype)
        dbl_ref[...] = dbl
        dal_ref[...] = dal
        da_ref[...] += da
        ddtb_ref[...] += ddtb
        dgn_ref[...] += dgn
        ds_scr[...] = ds

    return pl.pallas_call(
        body, name=name, grid=(h_n // hb, n_chunks),
        in_specs=[plane(0), plane(1), plane(2), zspec, col, col, scal, scal, gnspec, sspec, zspec],
        out_specs=[dplanes, zspec, col, col, scal, scal, gnh],
        out_shape=[jax.ShapeDtypeStruct((3, t, h_n * GDN_DK), F32),
                   jax.ShapeDtypeStruct((t, h_n * GDN_DV), z.dtype),
                   jax.ShapeDtypeStruct((h_n, t, 1), F32), jax.ShapeDtypeStruct((h_n, t, 1), F32),
                   jax.ShapeDtypeStruct((h_n, 1, 1), F32), jax.ShapeDtypeStruct((h_n, 1, 1), F32),
                   jax.ShapeDtypeStruct((h_n // hb, 1, GDN_DV), F32)],
        scratch_shapes=[pltpu.VMEM((hb, GDN_DK, GDN_DV), F32)],
        compiler_params=_cp(("parallel", "arbitrary")),
    )(qkv, qkv, qkv, z, bl, al, a_log, dtb, gn, sall, do)


@functools.partial(jax.custom_vjp, nondiff_argnums=(7,))
def gdn(qkv, z, bl, al, a_log, dtb, gn, tag):
    return _gdn_fwd_call(qkv, z, bl, al, a_log, dtb, gn, "gdn_" + tag)[0]


def _gdn_f(qkv, z, bl, al, a_log, dtb, gn, tag):
    o, sall = _gdn_fwd_call(qkv, z, bl, al, a_log, dtb, gn, "gdn_" + tag)
    return o, (qkv, z, bl, al, a_log, dtb, gn, sall)


def _gdn_b(tag, res, do):
    dqkv, dz, dbl, dal, da, ddtb, dgn_h = _gdn_bwd_call(*res, do, "gdn_" + tag + "_bwd")
    return dqkv, dz, dbl, dal, da, ddtb, jnp.sum(dgn_h, axis=0)


gdn.defvjp(_gdn_f, _gdn_b)


def adamw(w, parts, m, v, name):
    n_layers = len(parts)
    n_parts, r, c = parts[0].shape
    assert w.shape == (n_layers * r, c), (w.shape, parts[0].shape)
    tr = r
    for cand in (512, 256, 128, 64, 32, 16, 8):
        if r % cand == 0 and cand * c <= 256 * 1024:
            tr = cand
            break
    nb = r // tr
    blk = pl.BlockSpec((tr, c), lambda l, i: (l * nb + i, 0))
    bc1 = 1.0 - ADAM_B1 ** ADAM_STEP
    bc2 = 1.0 - ADAM_B2 ** ADAM_STEP

    def part_spec(li):
        return pl.BlockSpec((n_parts, tr, c),
                            lambda l, i: (0, jnp.where(l == li, i, jnp.where(l < li, 0, nb - 1)), 0))

    def body(*refs):
        w_ref, p_refs = refs[0], refs[1:1 + n_layers]
        m_ref, v_ref, g_ref, d_ref, mo_ref, vo_ref = refs[1 + n_layers:]
        for li in range(n_layers):
            @pl.when(pl.program_id(0) == li)
            def _(p_ref=p_refs[li]):
                g = p_ref[0].astype(F32)
                for i in range(1, n_parts):
                    g = g + p_ref[i].astype(F32)
                m2 = ADAM_B1 * m_ref[...] + (1.0 - ADAM_B1) * g
                v2 = ADAM_B2 * v_ref[...] + (1.0 - ADAM_B2) * (g * g)
                g_ref[...] = g
                mo_ref[...] = m2
                vo_ref[...] = v2
                d_ref[...] = -ADAM_LR * ((m2 / bc1) / (jnp.sqrt(v2 / bc2) + ADAM_EPS)
                                         + ADAM_WD * w_ref[...])

    return pl.pallas_call(
        body, name=name, grid=(n_layers, nb),
        in_specs=[blk] + [part_spec(li) for li in range(n_layers)] + [blk, blk],
        out_specs=[blk] * 4, out_shape=[jax.ShapeDtypeStruct(w.shape, F32)] * 4,
        compiler_params=_cp(("arbitrary", "arbitrary")),
    )(w, *parts, m, v)


_HBM = pl.BlockSpec(memory_space=pltpu.HBM)
_SEM = pl.BlockSpec(memory_space=pltpu.SEMAPHORE)
_EFFECT = pltpu.SideEffectType.DATAFLOW_SIDE_EFFECTING


def _peer(x, y, c, d):
    px = 1 - x if d & 4 else x
    py = 1 - y if d & 2 else y
    pc = 1 - c if d & 1 else c
    return (px, py, pc), 4 * px + 2 * py + pc


def copy_start(arrays, scatter, carry, name):
    n = len(arrays)
    lands = [lax.empty(a.shape if scatter else (N_DEV,) + a.shape, a.dtype) for a in arrays]

    def body(*refs):
        srcs, dsts = refs[:n], refs[n:2 * n]
        sems = refs[2 * n + 1:4 * n + 1]
        x, y, c = (lax.axis_index(a) for a in AXES)
        me = 4 * x + 2 * y + c
        for k in range(n):
            for d in range(1, N_DEV):
                peer, pid = _peer(x, y, c, d)
                pltpu.make_async_remote_copy(
                    src_ref=srcs[k].at[pid] if scatter else srcs[k], dst_ref=dsts[k].at[me],
                    send_sem=sems[2 * k], recv_sem=sems[2 * k + 1],
                    device_id=peer, device_id_type=pl.DeviceIdType.MESH).start()

    operands = list(arrays) + lands + [carry]
    outs = pl.pallas_call(
        body, name=name,
        out_shape=tuple([pltpu.SemaphoreType.DMA(())] * (2 * n)
                        + [pltpu.HBM(a.shape, a.dtype) for a in operands]),
        in_specs=[_HBM] * (2 * n + 1),
        out_specs=tuple([_SEM] * (2 * n) + [_HBM] * (2 * n + 1)),
        input_output_aliases={i: 2 * n + i for i in range(2 * n + 1)},
        compiler_params=pltpu.CompilerParams(has_side_effects=_EFFECT),
    )(*[pltpu.with_memory_space_constraint(a, pltpu.HBM) for a in operands])
    sems, thru = outs[:2 * n], outs[2 * n:4 * n]
    handles = [(sems[2 * k], sems[2 * k + 1], thru[k], thru[n + k]) for k in range(n)]
    return outs[-1], handles


def copy_wait(handles, after, name):
    n = len(handles)
    sems = [s for h in handles for s in h[:2]]
    srcs = [h[2] for h in handles]
    lands = [h[3] for h in handles]

    def body(*refs):
        dsts = refs[n:2 * n]
        sem_refs = refs[2 * n:4 * n]
        x, y, c = (lax.axis_index(a) for a in AXES)
        for k in range(n):
            seven = dsts[k].at[pl.ds(0, N_DEV - 1)]
            pltpu.make_async_remote_copy(
                src_ref=seven, dst_ref=seven, send_sem=sem_refs[2 * k], recv_sem=sem_refs[2 * k + 1],
                device_id=(x, y, c), device_id_type=pl.DeviceIdType.MESH).wait()

    outs = pl.pallas_call(
        body, name=name,
        out_shape=tuple([pltpu.HBM(a.shape, a.dtype) for a in srcs + lands]),
        in_specs=[_HBM] * (2 * n) + [_SEM] * (2 * n) + [pl.BlockSpec(memory_space=pl.ANY)],
        out_specs=tuple([_HBM] * (2 * n)),
        input_output_aliases={i: i for i in range(2 * n)},
        compiler_params=pltpu.CompilerParams(has_side_effects=_EFFECT),
    )(*srcs, *lands, *sems, after)
    return list(outs[:n]), list(outs[n:])


def exchange(arrays, modes, name):
    n = len(arrays)
    hbm = pl.BlockSpec(memory_space=pltpu.HBM)
    out_shape = [jax.ShapeDtypeStruct(a.shape if md == "scatter" else (N_DEV,) + a.shape, a.dtype)
                 for a, md in zip(arrays, modes)]

    def body(*refs):
        ins, outs = refs[:n], refs[n:2 * n]
        send_sems, recv_sems, local_sems = refs[2 * n:]
        x, y, c = (lax.axis_index(a) for a in AXES)
        me = 4 * x + 2 * y + c

        def src(k, p):
            return ins[k].at[p] if modes[k] == "scatter" else ins[k]

        local = [pltpu.make_async_copy(src(k, me), outs[k].at[me], local_sems.at[k]) for k in range(n)]
        for cp in local:
            cp.start()
        started = []
        for d in range(1, N_DEV):
            px = 1 - x if d & 4 else x
            py = 1 - y if d & 2 else y
            pc = 1 - c if d & 1 else c
            pid = 4 * px + 2 * py + pc
            for k in range(n):
                pltpu.make_async_remote_copy(
                    src_ref=src(k, pid), dst_ref=outs[k].at[me],
                    send_sem=send_sems.at[k, d - 1], recv_sem=recv_sems.at[k, d - 1],
                    device_id=(px, py, pc), device_id_type=pl.DeviceIdType.MESH).start()
                started.append((k, d, pid, (px, py, pc)))
        for k, d, pid, peer in started:
            pltpu.make_async_remote_copy(
                src_ref=src(k, pid), dst_ref=outs[k].at[pid],
                send_sem=send_sems.at[k, d - 1], recv_sem=recv_sems.at[k, d - 1],
                device_id=peer, device_id_type=pl.DeviceIdType.MESH).wait()
        for cp in local:
            cp.wait()

    outs = pl.pallas_call(
        body, name=name, in_specs=[hbm] * n, out_specs=[hbm] * n, out_shape=out_shape,
        scratch_shapes=[pltpu.SemaphoreType.DMA((n, N_DEV - 1)), pltpu.SemaphoreType.DMA((n, N_DEV - 1)),
                        pltpu.SemaphoreType.DMA((n,))],
        compiler_params=pltpu.CompilerParams(has_side_effects=True),
    )(*arrays)
    return list(outs)


BIG = ("w_in", "w_uq", "w_ukv", "w_o_mla", "w_o_gdn", "w_o", "w_gate_up", "w_down")
ROW_SHARDED = ("w_o", "w_down")
SMALL = ("b_ada", "norm_mix", "norm_ffn", "q_a_norm", "kv_a_norm", "A_log", "dt_bias", "gdn_norm",
         "final_norm")
WEIGHTS = ("w_ada", "b_ada", "norm_mix", "norm_ffn", "w_in", "q_a_norm", "kv_a_norm", "w_uq", "w_ukv",
           "w_o_mla", "conv_w", "A_log", "dt_bias", "gdn_norm", "w_o_gdn", "w_o", "w_gate_up", "w_down",
           "final_norm")


def _unslot(g):
    return g.transpose(1, 0, 2).reshape(g.shape[1], -1)


def _cols(g):
    return g if g.shape[-1] % LANE == 0 else _unslot(g)


def _stack_rows(g):
    return g.reshape(-1, g.shape[-1])


def _pad_cols(a):
    return jnp.pad(a, ((0, 0), (0, _pad_lanes(a.shape[1]) - a.shape[1])))


def _rope(xv, cos, sin):
    x1, x2 = jnp.split(xv, 2, axis=-1)
    return jnp.concatenate([x1 * cos - x2 * sin, x2 * cos + x1 * sin], axis=-1)


MIX_WEIGHTS = ("w_uq", "w_ukv", "w_o_mla", "w_o_gdn", "w_o")
FFN_WEIGHTS = ("w_gate_up", "w_down")


def _stage_in(x, mod, nm, w_in_s, tg):
    d = x.shape[1]
    hg = GDN_HEADS
    w_in = _unslot(w_in_s)
    o1 = Q_LORA + KV_LORA + QK_ROPE
    o2 = o1 + 2 * hg * GDN_DK + hg * GDN_DV
    o3 = o2 + hg * GDN_DV
    o4 = o3 + 2 * hg
    h = ada_norm(x, nm, mod[:, d:2 * d], mod[:, :d], "mix" + tg)
    return (mm(h, _pad_cols(w_in[:, :o1]), "in_a" + tg, BF16), mm(h, w_in[:, o1:o2], "in_qkv" + tg, BF16),
            mm(h, w_in[:, o2:o3], "in_z" + tg, BF16), mm(h, _pad_cols(w_in[:, o3:o4]), "in_ba" + tg, F32),
            mm(h, w_in[:, o4:o4 + 2 * d], "in_g" + tg, BF16))


def _stage_mix(x, mod, seg_a, qkv, z, ba, gl, w_uq_s, w_ukv_s, w_o_mla_s, w_o_gdn_s, w_o_s, conv_s,
               qan, kvan, a_log, dtb, gn, cos, sin, tg):
    t, d = x.shape
    hq, hg = MLA_HEADS, GDN_HEADS
    w_uq = _unslot(w_uq_s).reshape(Q_LORA, hq, QK_NOPE + QK_ROPE)
    w_uq = jnp.concatenate([w_uq[:, :, :QK_NOPE].reshape(Q_LORA, hq * QK_NOPE),
                            w_uq[:, :, QK_NOPE:].reshape(Q_LORA, hq * QK_ROPE)], axis=1)
    c_q = seg_a[:, :Q_LORA]
    c_kv = seg_a[:, Q_LORA:Q_LORA + KV_LORA]
    k_pe = seg_a[:, Q_LORA + KV_LORA:Q_LORA + KV_LORA + QK_ROPE]
    qf = mm(rms_norm(c_q, qan, "qa" + tg), w_uq, "uq" + tg, BF16)
    kvf = mm(rms_norm(c_kv, kvan, "kva" + tg), _cols(w_ukv_s), "ukv" + tg, BF16)
    qn = qf[:, :hq * QK_NOPE]
    q_pe = qf[:, hq * QK_NOPE:].astype(F32).reshape(t, hq, QK_ROPE)
    qr = _rope(q_pe, cos[:, None, :], sin[:, None, :]).transpose(1, 0, 2).astype(BF16)
    kr = _rope(k_pe.astype(F32), cos, sin).astype(BF16)
    y_a = mm(attention(qn, qr, kvf, kr, tg), _cols(w_o_mla_s), "o_mla" + tg, BF16)
    conv_w = conv_s.transpose(1, 0, 2).reshape(CONV_WIDTH, -1)
    qkv_c = conv_silu(qkv, conv_w, tg)
    bl = ba[:, :hg].T[:, :, None]
    al = ba[:, hg:2 * hg].T[:, :, None]
    o_gdn = gdn(qkv_c, z, bl, al, a_log.reshape(hg, 1, 1), dtb.reshape(hg, 1, 1), gn, tg)
    y_b = mm(o_gdn, _cols(w_o_gdn_s), "o_gdn" + tg, BF16)
    mix = mm(gate_mix(gl, y_a, y_b, tg), _stack_rows(w_o_s), "w_o" + tg, F32)
    return resid(x, mod[:, 2 * d:3 * d], mix, "mix" + tg)


def _stage_ffn(x, mod, nf, w_gu_s, w_down_s, tg):
    d = x.shape[1]
    h = ada_norm(x, nf, mod[:, 4 * d:5 * d], mod[:, 3 * d:4 * d], "ffn" + tg)
    gu = mm(h, _cols(w_gu_s), "gu" + tg, BF16)
    dn = mm(swiglu(gu, tg), _stack_rows(w_down_s), "down" + tg, F32)
    return resid(x, mod[:, 5 * d:6 * d], dn, "ffn" + tg)


def _flat_row(arrs):
    v = jnp.concatenate([a.reshape(-1) for a in arrs])
    return jnp.pad(v, (0, _pad_lanes(v.shape[0]) - v.shape[0]))[None, :]


def kernel(x, c, positions, w_ada, b_ada, norm_mix, norm_ffn, w_in, q_a_norm, kv_a_norm, w_uq, w_ukv, w_o_mla, conv_w, A_log, dt_bias, gdn_norm, w_o_gdn, w_o, w_gate_up, w_down, final_norm, loss_target, m_w_ada, m_b_ada, m_norm_mix, m_norm_ffn, m_w_in, m_q_a_norm, m_kv_a_norm, m_w_uq, m_w_ukv, m_w_o_mla, m_conv_w, m_A_log, m_dt_bias, m_gdn_norm, m_w_o_gdn, m_w_o, m_w_gate_up, m_w_down, m_final_norm, v_w_ada, v_b_ada, v_norm_mix, v_norm_ffn, v_w_in, v_q_a_norm, v_kv_a_norm, v_w_uq, v_w_ukv, v_w_o_mla, v_conv_w, v_A_log, v_dt_bias, v_gdn_norm, v_w_o_gdn, v_w_o, v_w_gate_up, v_w_down, v_final_norm):
    given = dict(locals())
    t, d = x.shape[1], x.shape[2]
    n_ada = w_ada.shape[2]
    me = 4 * lax.axis_index("x") + 2 * lax.axis_index("y") + lax.axis_index("c")

    def with_own(land, own):
        return lax.dynamic_update_slice(land, own[None], (me,) + (0,) * own.ndim)

    got = exchange([c, conv_w], ["gather", "gather"], "gather_small")
    c_all, conv_g = got[0].reshape(N_DEV, d), got[1]
    c_rows = jnp.pad(c_all, ((0, 16 - N_DEV), (0, 0)))
    mod_cols = jnp.stack([_mm(c_rows, w_ada[l], "nn", F32, "ada_mod%d" % l, a_act="silu")[:N_DEV]
                          for l in range(DEPTH)], axis=1)
    mod_mine = exchange([mod_cols], ["scatter"], "scatter_mod")[0]
    mods = mod_mine.transpose(1, 0, 2).reshape(DEPTH, N_DEV * n_ada) + b_ada

    keys = [(n, l) for l in range(DEPTH) for n in BIG]
    mods, handles = copy_start([given[n][l].astype(BF16) for n, l in keys], False, mods, "gather_start")
    handles = dict(zip(keys, handles))

    def landed(ks, after, name):
        srcs, lands = copy_wait([handles[k] for k in ks], after, name)
        return [with_own(land, src) for src, land in zip(srcs, lands)]

    inv_freq = 1.0 / (ROPE_THETA ** (jnp.arange(0, QK_ROPE, 2, dtype=F32) / QK_ROPE))
    ang = positions[0].astype(F32)[:, None] * inv_freq
    cos, sin = jnp.cos(ang), jnp.sin(ang)
    xl = x[0]
    after = mods
    vjps = []
    for l in range(DEPTH):
        tg = str(l)
        mod = mods[l:l + 1]
        (w_in_s,) = landed([("w_in", l)], after, "wait_in" + tg)
        seg, vjp_in = jax.vjp(lambda *a, tg=tg: _stage_in(*a, tg), xl, mod, norm_mix[l:l + 1], w_in_s)
        w_mix = landed([(n, l) for n in MIX_WEIGHTS], seg[0], "wait_mix" + tg)
        xm, vjp_mix = jax.vjp(lambda *a, tg=tg: _stage_mix(*a, cos, sin, tg), xl, mod, *seg, *w_mix,
                              conv_g[:, l], q_a_norm[l:l + 1], kv_a_norm[l:l + 1], A_log[l], dt_bias[l],
                              gdn_norm[l:l + 1])
        w_ffn = landed([(n, l) for n in FFN_WEIGHTS], xm, "wait_ffn" + tg)
        xl, vjp_ffn = jax.vjp(lambda *a, tg=tg: _stage_ffn(*a, tg), xm, mod, norm_ffn[l:l + 1], *w_ffn)
        after = xl
        vjps.append((vjp_in, vjp_mix, vjp_ffn))

    loss_t, g, dfn = loss_head(xl, final_norm[None, :], loss_target[0])
    loss = lax.psum(loss_t[0, 0], AXES)
    dsmall = {n: [None] * DEPTH for n in SMALL + ("conv_w",)}
    dmods = [None] * DEPTH
    sent = {}

    def send(ks, grads, carry, name):
        carry, hs = copy_start(list(grads), True, carry, name)
        sent.update(zip(ks, hs))
        return carry

    for l in reversed(range(DEPTH)):
        tg = str(l)
        vjp_in, vjp_mix, vjp_ffn = vjps[l]
        dxm, dmod_f, dsmall["norm_ffn"][l], *dw = vjp_ffn(g)
        dxm = send([(n, l) for n in FFN_WEIGHTS], dw, dxm, "scatter_ffn" + tg)
        dx_m, dmod_m, *rest = vjp_mix(dxm)
        dseg, dw, rest = rest[:5], rest[5:5 + len(MIX_WEIGHTS)], rest[5 + len(MIX_WEIGHTS):]
        dseg[0] = send([(n, l) for n in MIX_WEIGHTS], dw, dseg[0], "scatter_mix" + tg)
        for n, gr in zip(("conv_w", "q_a_norm", "kv_a_norm", "A_log", "dt_bias", "gdn_norm"), rest):
            dsmall[n][l] = gr
        dx_i, dmod_i, dsmall["norm_mix"][l], dw_in = vjp_in(tuple(dseg))
        g = dx_i + dx_m
        if l > 0:
            g = send([("w_in", l)], [dw_in], g, "scatter_in" + tg)
        dmods[l] = dmod_f + dmod_m + dmod_i
    dx = g
    dmods = jnp.concatenate(dmods, axis=0)
    dconv = jnp.stack(dsmall.pop("conv_w"), axis=1)
    dsmall = {n: jnp.concatenate(v, axis=0) if v[0].ndim == 2 else jnp.stack(v)
              for n, v in dsmall.items() if v[0] is not None}
    dsmall["b_ada"] = dmods
    dsmall["final_norm"] = dfn[0]

    dmod_cols = dmods.reshape(DEPTH, N_DEV, n_ada).transpose(1, 0, 2)
    conv_parts, dmod_all, small_parts = exchange(
        [dconv, dmod_cols, _flat_row([dsmall[n] for n in SMALL])], ["scatter", "scatter", "gather"],
        "exchange_small")
    dmod_all = send([("w_in", 0)], [dw_in], dmod_all, "scatter_in0")

    res = {}
    dm_rows = jnp.pad(dmod_all, ((0, 16 - N_DEV), (0, 0), (0, 0)))
    g_ada = jnp.stack([_mm(c_rows, dm_rows[:, l], "tn", F32, "ada_dw%d" % l, a_act="silu")
                       for l in range(DEPTH)])
    r2 = (DEPTH * d, n_ada)
    outs = adamw(w_ada.reshape(r2), [g_ada.reshape((1,) + r2)], m_w_ada.reshape(r2), v_w_ada.reshape(r2),
                 "adamw_w_ada")
    res["w_ada"] = [o.reshape(w_ada.shape) for o in outs]
    packed = SMALL + ("conv_w",)
    p_all = jnp.concatenate([small_parts, conv_parts.reshape(N_DEV, 1, -1)], axis=2)
    pack = lambda pre: jnp.concatenate([_flat_row([given[pre + n] for n in SMALL]),
                                        given[pre + "conv_w"].reshape(1, -1)], axis=1)
    outs = adamw(pack(""), [p_all], pack("m_"), pack("v_"), "adamw_small")
    done = [res["w_ada"][1], outs[1]]
    for group, gname in ((FFN_WEIGHTS, "ffn"), (MIX_WEIGHTS, "mix"), (("w_in",), "in")):
        ks = [(n, l) for l in reversed(range(DEPTH)) for n in group]
        after = sum(lax.slice(a, (0,) * a.ndim, (1,) * a.ndim).reshape(1, 1) for a in done)
        srcs, lands = copy_wait([sent[k] for k in ks], after, "scatter_wait_" + gname)
        parts = {k: with_own(land, lax.dynamic_index_in_dim(src, me, 0, keepdims=False))
                 for k, src, land in zip(ks, srcs, lands)}
        for n in group:
            w = given[n]
            r2 = (w.shape[0] * w.shape[1], w.shape[2])
            res[n] = [o.reshape(w.shape) for o in
                      adamw(w.reshape(r2), [parts[(n, l)] for l in range(DEPTH)], given["m_" + n].reshape(r2),
                            given["v_" + n].reshape(r2), "adamw_" + n)]
            done.append(res[n][1])
    off = 0
    for n in packed:
        if n == "conv_w":
            off = small_parts.shape[2]
        size = math.prod(given[n].shape)
        res[n] = [o[0, off:off + size].reshape(given[n].shape) for o in outs]
        off += size

    return (loss, dx[None]) + tuple(res[n][i] for i in range(4) for n in WEIGHTS)
```

```python
import functools
import math

import jax
import jax.numpy as jnp
from jax import lax
from jax.experimental import pallas as pl
from jax.experimental.pallas import tpu as pltpu

F32 = jnp.float32
BF16 = jnp.bfloat16

MLA_HEADS = 8
QK_NOPE = 128
QK_ROPE = 64
V_HEAD = 128
Q_LORA = 512
KV_LORA = 512
ROPE_THETA = 10000.0
GDN_HEADS = 8
GDN_DK = 128
GDN_DV = 128
CONV_WIDTH = 4
CHUNK = 64
DEPTH = 2
EPS = 1e-6
ADAM_LR = 0.001
ADAM_B1 = 0.9
ADAM_B2 = 0.999
ADAM_EPS = 1e-08
ADAM_WD = 0.01
ADAM_STEP = 10

N_DEV = 8
AXES = ("x", "y", "c")
LANE = 128
VMEM_LIMIT = 48 * 1024 * 1024
MM_VMEM_BUDGET = 36 * 1024 * 1024
HIGHEST = lax.Precision.HIGHEST

NN = (((1,), (0,)), ((), ()))
NT = (((1,), (1,)), ((), ()))
TN = (((0,), (0,)), ((), ()))


def _cp(sem=None):
    return pltpu.CompilerParams(dimension_semantics=sem, vmem_limit_bytes=VMEM_LIMIT)


def _tile(n, cap):
    if n <= cap:
        return n
    for t in range(cap - cap % LANE, 0, -LANE):
        if n % t == 0:
            return t
    return n


def _rows(t, cap=256):
    return cap if t % cap == 0 else t


def _pad_lanes(n):
    return -(-n // LANE) * LANE


def _sigmoid(x):
    return 1.0 / (1.0 + jnp.exp(-x))


def _softplus(x):
    return jnp.maximum(x, 0.0) + jnp.log(1.0 + jnp.exp(-jnp.abs(x)))


def _tile_slot(n, cap):
    t = _tile(n, cap)
    return n if t < 256 < n <= 1536 else t


def _mm(a, b, dims, out_dtype, name, a_act=None, slots=False):
    if dims == "nn":
        m, k = a.shape
        n = b.shape[-1] * (N_DEV if slots else 1)
    elif dims == "nt":
        m, k = a.shape
        n = b.shape[-2]
    else:
        k, m = a.shape
        n = b.shape[-1]
    tm = _tile(m, 1024)
    tn = _tile_slot(n // N_DEV, 512) if slots and dims != "nt" else _tile(n, 512)
    k_slot = k // N_DEV if slots and dims == "nt" else k

    def vmem_bytes(tk_):
        a_b, b_b = tm * tk_ * a.dtype.itemsize, tk_ * tn * b.dtype.itemsize
        casts = (tm * tk_ * 2 if a.dtype != BF16 else 0) + (tk_ * tn * 2 if b.dtype != BF16 else 0)
        return 2 * (a_b + b_b + tm * tn * jnp.dtype(out_dtype).itemsize) + 2 * tm * tn * 4 + casts

    tk = _tile_slot(k_slot, 1536) if slots and dims == "nt" else _tile(k, 2048)
    while vmem_bytes(tk) > MM_VMEM_BUDGET and tk % (2 * LANE) == 0:
        tk //= 2
    nk = k // tk
    per_n = (n // N_DEV) // tn if slots else 1
    per_k = (k // N_DEV) // tk if slots else 1
    if dims == "tn":
        a_spec = pl.BlockSpec((tk, tm), lambda i, j, kk: (kk, i))
    else:
        a_spec = pl.BlockSpec((tm, tk), lambda i, j, kk: (i, kk))
    if dims == "nt":
        if slots:
            b_spec = pl.BlockSpec((None, tn, tk), lambda i, j, kk: (kk // per_k, j, kk % per_k))
        else:
            b_spec = pl.BlockSpec((tn, tk), lambda i, j, kk: (j, kk))
    elif dims == "nn" and slots:
        b_spec = pl.BlockSpec((None, tk, tn), lambda i, j, kk: (j // per_n, kk, j % per_n))
    else:
        b_spec = pl.BlockSpec((tk, tn), lambda i, j, kk: (kk, j))
    if dims == "tn" and slots:
        out_spec = pl.BlockSpec((None, tm, tn), lambda i, j, kk: (j // per_n, i, j % per_n))
        out_shape = jax.ShapeDtypeStruct((N_DEV, m, n // N_DEV), out_dtype)
    else:
        out_spec = pl.BlockSpec((tm, tn), lambda i, j, kk: (i, j))
        out_shape = jax.ShapeDtypeStruct((m, n), out_dtype)
    dn = {"nn": NN, "nt": NT, "tn": TN}[dims]

    def product(a_ref, b_ref):
        av = a_ref[...]
        if a_act == "silu":
            av = av * _sigmoid(av)
        return lax.dot_general(av.astype(BF16), b_ref[...].astype(BF16), dn, preferred_element_type=F32)

    def body_one(a_ref, b_ref, o_ref):
        o_ref[...] = product(a_ref, b_ref).astype(o_ref.dtype)

    def body_acc(a_ref, b_ref, o_ref, acc_ref):
        kk = pl.program_id(2)

        @pl.when(kk == 0)
        def _():
            acc_ref[...] = jnp.zeros_like(acc_ref)

        acc_ref[...] += product(a_ref, b_ref)

        @pl.when(kk == nk - 1)
        def _():
            o_ref[...] = acc_ref[...].astype(o_ref.dtype)

    return pl.pallas_call(
        body_one if nk == 1 else body_acc, name=name, grid=(m // tm, n // tn, nk),
        in_specs=[a_spec, b_spec], out_specs=out_spec, out_shape=out_shape,
        scratch_shapes=[] if nk == 1 else [pltpu.VMEM((tm, tn), F32)],
        compiler_params=_cp(("parallel", "parallel", "arbitrary")),
    )(a, b)


@functools.partial(jax.custom_vjp, nondiff_argnums=(2, 3))
def mm(a, b, tag, out_dtype):
    return _mm(a, b, "nn", out_dtype, "mm_" + tag, slots=b.ndim == 3)


def _mm_f(a, b, tag, out_dtype):
    return mm(a, b, tag, out_dtype), (a, b)


def _mm_b(tag, out_dtype, res, g):
    a, b = res
    slots = b.ndim == 3
    da = _mm(g, b, "nt", a.dtype, "mm_" + tag + "_da", slots=slots)
    db = _mm(a, g, "tn", b.dtype, "mm_" + tag + "_db", slots=slots)
    return da, db


mm.defvjp(_mm_f, _mm_b)


def _norm_fwd_call(x, nw, sc, sh, name):
    t, d = x.shape
    tr = _rows(t)
    mod = sc is not None
    row = pl.BlockSpec((tr, d), lambda i: (i, 0))
    vec = pl.BlockSpec((1, d), lambda i: (0, 0))

    def body(*refs):
        if mod:
            x_ref, nw_ref, sc_ref, sh_ref, o_ref = refs
        else:
            x_ref, nw_ref, o_ref = refs
        xv = x_ref[...].astype(F32)
        r = lax.rsqrt(jnp.mean(xv * xv, axis=-1, keepdims=True) + EPS)
        y = (xv * r) * nw_ref[...]
        if mod:
            y = y * (1.0 + sc_ref[...]) + sh_ref[...]
        o_ref[...] = y.astype(o_ref.dtype)

    args = (x, nw, sc, sh) if mod else (x, nw)
    return pl.pallas_call(
        body, name=name, grid=(t // tr,),
        in_specs=[row] + [vec] * (len(args) - 1), out_specs=row,
        out_shape=jax.ShapeDtypeStruct((t, d), BF16),
        compiler_params=_cp(("parallel",)),
    )(*args)


def _norm_bwd_call(x, nw, sc, dh, name):
    t, d = x.shape
    tr = _rows(t)
    mod = sc is not None
    row = pl.BlockSpec((tr, d), lambda i: (i, 0))
    vec = pl.BlockSpec((1, d), lambda i: (0, 0))

    def body(*refs):
        if mod:
            x_ref, nw_ref, sc_ref, dh_ref, dx_ref, dnw_ref, dsc_ref, dsh_ref = refs
        else:
            x_ref, nw_ref, dh_ref, dx_ref, dnw_ref = refs
        i = pl.program_id(0)
        xv = x_ref[...].astype(F32)
        dh = dh_ref[...].astype(F32)
        r = lax.rsqrt(jnp.mean(xv * xv, axis=-1, keepdims=True) + EPS)
        y = xv * r
        a = nw_ref[...] * (1.0 + sc_ref[...]) if mod else nw_ref[...]
        dy = dh * a
        dx_ref[...] = (r * (dy - y * jnp.mean(dy * y, axis=-1, keepdims=True))).astype(dx_ref.dtype)
        da = jnp.sum(dh * y, axis=0, keepdims=True)

        @pl.when(i == 0)
        def _():
            dnw_ref[...] = jnp.zeros_like(dnw_ref)
            if mod:
                dsc_ref[...] = jnp.zeros_like(dsc_ref)
                dsh_ref[...] = jnp.zeros_like(dsh_ref)

        if mod:
            dnw_ref[...] += da * (1.0 + sc_ref[...])
            dsc_ref[...] += da * nw_ref[...]
            dsh_ref[...] += jnp.sum(dh, axis=0, keepdims=True)
        else:
            dnw_ref[...] += da

    args = (x, nw, sc, dh) if mod else (x, nw, dh)
    n_vec = 3 if mod else 1
    return pl.pallas_call(
        body, name=name, grid=(t // tr,),
        in_specs=[row] + [vec] * (len(args) - 2) + [row],
        out_specs=[row] + [vec] * n_vec,
        out_shape=[jax.ShapeDtypeStruct((t, d), x.dtype)] + [jax.ShapeDtypeStruct((1, d), F32)] * n_vec,
        compiler_params=_cp(("arbitrary",)),
    )(*args)


@functools.partial(jax.custom_vjp, nondiff_argnums=(4,))
def ada_norm(x, nw, sc, sh, tag):
    return _norm_fwd_call(x, nw, sc, sh, "adanorm_" + tag)


def _ada_norm_f(x, nw, sc, sh, tag):
    return _norm_fwd_call(x, nw, sc, sh, "adanorm_" + tag), (x, nw, sc)


def _ada_norm_b(tag, res, dh):
    x, nw, sc = res
    dx, dnw, dsc, dsh = _norm_bwd_call(x, nw, sc, dh, "adanorm_" + tag + "_bwd")
    return dx, dnw, dsc, dsh


ada_norm.defvjp(_ada_norm_f, _ada_norm_b)


@functools.partial(jax.custom_vjp, nondiff_argnums=(2,))
def rms_norm(x, nw, tag):
    return _norm_fwd_call(x, nw, None, None, "rms_" + tag)


def _rms_norm_f(x, nw, tag):
    return _norm_fwd_call(x, nw, None, None, "rms_" + tag), (x, nw)


def _rms_norm_b(tag, res, dh):
    x, nw = res
    dx, dnw = _norm_bwd_call(x, nw, None, dh, "rms_" + tag + "_bwd")
    return dx, dnw


rms_norm.defvjp(_rms_norm_f, _rms_norm_b)


def _gate_mix_fwd_call(gl, ya, yb, name):
    t, d = ya.shape
    tr = _rows(t)
    row = pl.BlockSpec((tr, d), lambda i: (i, 0))

    def body(ga_ref, gb_ref, ya_ref, yb_ref, o_ref):
        o_ref[...] = (_sigmoid(ga_ref[...].astype(F32)) * ya_ref[...].astype(F32)
                      + _sigmoid(gb_ref[...].astype(F32)) * yb_ref[...].astype(F32)).astype(o_ref.dtype)

    return pl.pallas_call(
        body, name=name, grid=(t // tr,),
        in_specs=[row, pl.BlockSpec((tr, d), lambda i: (i, 1)), row, row], out_specs=row,
        out_shape=jax.ShapeDtypeStruct((t, d), BF16),
        compiler_params=_cp(("parallel",)),
    )(gl, gl, ya, yb)


def _gate_mix_bwd_call(gl, ya, yb, dm, name):
    t, d = ya.shape
    tr = _rows(t)
    row = pl.BlockSpec((tr, d), lambda i: (i, 0))
    wide = pl.BlockSpec((tr, 2 * d), lambda i: (i, 0))

    def body(gl_ref, ya_ref, yb_ref, dm_ref, dgl_ref, dya_ref, dyb_ref):
        dm = dm_ref[...].astype(F32)
        ga = _sigmoid(gl_ref[:, :d].astype(F32))
        gb = _sigmoid(gl_ref[:, d:].astype(F32))
        dya_ref[...] = (dm * ga).astype(dya_ref.dtype)
        dyb_ref[...] = (dm * gb).astype(dyb_ref.dtype)
        dgl_ref[:, :d] = (dm * ya_ref[...].astype(F32) * ga * (1.0 - ga)).astype(dgl_ref.dtype)
        dgl_ref[:, d:] = (dm * yb_ref[...].astype(F32) * gb * (1.0 - gb)).astype(dgl_ref.dtype)

    return pl.pallas_call(
        body, name=name, grid=(t // tr,),
        in_specs=[wide, row, row, row], out_specs=[wide, row, row],
        out_shape=[jax.ShapeDtypeStruct((t, 2 * d), gl.dtype), jax.ShapeDtypeStruct((t, d), ya.dtype),
                   jax.ShapeDtypeStruct((t, d), yb.dtype)],
        compiler_params=_cp(("parallel",)),
    )(gl, ya, yb, dm)


@functools.partial(jax.custom_vjp, nondiff_argnums=(3,))
def gate_mix(gl, ya, yb, tag):
    return _gate_mix_fwd_call(gl, ya, yb, "gatemix_" + tag)


def _gate_mix_f(gl, ya, yb, tag):
    return _gate_mix_fwd_call(gl, ya, yb, "gatemix_" + tag), (gl, ya, yb)


def _gate_mix_b(tag, res, dm):
    return tuple(_gate_mix_bwd_call(*res, dm, "gatemix_" + tag + "_bwd"))


gate_mix.defvjp(_gate_mix_f, _gate_mix_b)


def _resid_fwd_call(x, gt, m, name):
    t, d = x.shape
    tr = _rows(t)
    row = pl.BlockSpec((tr, d), lambda i: (i, 0))
    vec = pl.BlockSpec((1, d), lambda i: (0, 0))

    def body(x_ref, gt_ref, m_ref, o_ref):
        o_ref[...] = x_ref[...] + gt_ref[...] * m_ref[...]

    return pl.pallas_call(
        body, name=name, grid=(t // tr,), in_specs=[row, vec, row], out_specs=row,
        out_shape=jax.ShapeDtypeStruct((t, d), F32), compiler_params=_cp(("parallel",)),
    )(x, gt, m)


def _resid_bwd_call(gt, m, g, name):
    t, d = m.shape
    tr = _rows(t)
    row = pl.BlockSpec((tr, d), lambda i: (i, 0))
    vec = pl.BlockSpec((1, d), lambda i: (0, 0))

    def body(gt_ref, m_ref, g_ref, dm_ref, dgt_ref):
        i = pl.program_id(0)
        g = g_ref[...]
        dm_ref[...] = g * gt_ref[...]

        @pl.when(i == 0)
        def _():
            dgt_ref[...] = jnp.zeros_like(dgt_ref)

        dgt_ref[...] += jnp.sum(g * m_ref[...], axis=0, keepdims=True)

    return pl.pallas_call(
        body, name=name, grid=(t // tr,), in_specs=[vec, row, row], out_specs=[row, vec],
        out_shape=[jax.ShapeDtypeStruct((t, d), F32), jax.ShapeDtypeStruct((1, d), F32)],
        compiler_params=_cp(("arbitrary",)),
    )(gt, m, g)


@functools.partial(jax.custom_vjp, nondiff_argnums=(3,))
def resid(x, gt, m, tag):
    return _resid_fwd_call(x, gt, m, "resid_" + tag)


def _resid_f(x, gt, m, tag):
    return _resid_fwd_call(x, gt, m, "resid_" + tag), (gt, m)


def _resid_b(tag, res, g):
    gt, m = res
    dm, dgt = _resid_bwd_call(gt, m, g, "resid_" + tag + "_bwd")
    return g, dgt, dm


resid.defvjp(_resid_f, _resid_b)


def _swiglu_fwd_call(gu, name):
    t, f2 = gu.shape
    f = f2 // 2
    tr = _rows(t, 128)
    half = pl.BlockSpec((tr, f), lambda i: (i, 0))

    def body(g_ref, u_ref, o_ref):
        g = g_ref[...].astype(F32)
        o_ref[...] = (g * _sigmoid(g) * u_ref[...].astype(F32)).astype(o_ref.dtype)

    return pl.pallas_call(
        body, name=name, grid=(t // tr,),
        in_specs=[half, pl.BlockSpec((tr, f), lambda i: (i, 1))], out_specs=half,
        out_shape=jax.ShapeDtypeStruct((t, f), BF16), compiler_params=_cp(("parallel",)),
    )(gu, gu)


def _swiglu_bwd_call(gu, da, name):
    t, f2 = gu.shape
    f = f2 // 2
    tr = _rows(t, 128)
    wide = pl.BlockSpec((tr, f2), lambda i: (i, 0))

    def body(gu_ref, da_ref, dgu_ref):
        g = gu_ref[:, :f].astype(F32)
        u = gu_ref[:, f:].astype(F32)
        da = da_ref[...].astype(F32)
        s = _sigmoid(g)
        dgu_ref[:, :f] = (da * u * s * (1.0 + g * (1.0 - s))).astype(dgu_ref.dtype)
        dgu_ref[:, f:] = (da * g * s).astype(dgu_ref.dtype)

    return pl.pallas_call(
        body, name=name, grid=(t // tr,),
        in_specs=[wide, pl.BlockSpec((tr, f), lambda i: (i, 0))], out_specs=wide,
        out_shape=jax.ShapeDtypeStruct((t, f2), gu.dtype), compiler_params=_cp(("parallel",)),
    )(gu, da)


@functools.partial(jax.custom_vjp, nondiff_argnums=(1,))
def swiglu(gu, tag):
    return _swiglu_fwd_call(gu, "swiglu_" + tag)


def _swiglu_f(gu, tag):
    return _swiglu_fwd_call(gu, "swiglu_" + tag), (gu,)


def _swiglu_b(tag, res, da):
    return (_swiglu_bwd_call(res[0], da, "swiglu_" + tag + "_bwd"),)


swiglu.defvjp(_swiglu_f, _swiglu_b)


def loss_head(x, fw, tgt):
    t, d = x.shape
    tr = _rows(t)
    row = pl.BlockSpec((tr, d), lambda i: (i, 0))
    vec = pl.BlockSpec((1, d), lambda i: (0, 0))
    tile = pl.BlockSpec((8, LANE), lambda i: (0, 0))

    def body(x_ref, fw_ref, tgt_ref, loss_ref, dx_ref, dfw_ref):
        i = pl.program_id(0)
        xv = x_ref[...]
        fw = fw_ref[...]
        r = lax.rsqrt(jnp.mean(xv * xv, axis=-1, keepdims=True) + EPS)
        yh = xv * r
        e = yh * fw - tgt_ref[...]
        dy = e * (1.0 / d)
        dyw = dy * fw
        dx_ref[...] = r * (dyw - yh * jnp.mean(dyw * yh, axis=-1, keepdims=True))

        @pl.when(i == 0)
        def _():
            loss_ref[...] = jnp.zeros_like(loss_ref)
            dfw_ref[...] = jnp.zeros_like(dfw_ref)

        loss_ref[...] += 0.5 * jnp.sum(jnp.mean(e * e, axis=-1, keepdims=True))
        dfw_ref[...] += jnp.sum(dy * yh, axis=0, keepdims=True)

    return pl.pallas_call(
        body, name="loss_head", grid=(t // tr,), in_specs=[row, vec, row],
        out_specs=[tile, row, vec],
        out_shape=[jax.ShapeDtypeStruct((8, LANE), F32), jax.ShapeDtypeStruct((t, d), F32),
                   jax.ShapeDtypeStruct((1, d), F32)],
        compiler_params=_cp(("arbitrary",)),
    )(x, fw, tgt)


def _attn_scores(qn_ref, qr_ref, kn_ref, kr_ref, diag):
    tq = qn_ref.shape[0]
    s = lax.dot_general(qn_ref[...].astype(BF16), kn_ref[...].astype(BF16), NT, preferred_element_type=F32)
    s += lax.dot_general(qr_ref[...].astype(BF16), kr_ref[...].astype(BF16), NT, preferred_element_type=F32)
    s = s * (QK_NOPE + QK_ROPE) ** -0.5
    if diag:
        rows = lax.broadcasted_iota(jnp.int32, (tq, tq), 0)
        cols = lax.broadcasted_iota(jnp.int32, (tq, tq), 1)
        s = jnp.where(cols <= rows, s, -1e30)
    return s


def _attn_fwd_call(qn, qr, kv, kr, name):
    t = qn.shape[0]
    h_n = MLA_HEADS
    tq = _rows(t, 512)
    nq = t // tq
    assert V_HEAD == LANE and tq % LANE == 0

    def body(qn_ref, qr_ref, kn_ref, v_ref, kr_ref, o_ref, lse_ref, m_scr, l_scr, acc_scr):
        i, j = pl.program_id(1), pl.program_id(2)

        @pl.when(j == 0)
        def _():
            m_scr[...] = jnp.full_like(m_scr, -1e30)
            l_scr[...] = jnp.zeros_like(l_scr)
            acc_scr[...] = jnp.zeros_like(acc_scr)

        def step(diag):
            s = _attn_scores(qn_ref, qr_ref, kn_ref, kr_ref, diag)
            m_old = m_scr[...]
            m_new = jnp.maximum(m_old, jnp.max(s, axis=-1, keepdims=True))
            p = jnp.exp(s - jnp.tile(m_new, (1, tq // LANE)))
            alpha = jnp.exp(m_old - m_new)
            l_scr[...] = alpha * l_scr[...] + jnp.sum(p, axis=-1, keepdims=True)
            acc_scr[...] = alpha * acc_scr[...] + jnp.dot(p.astype(BF16), v_ref[...].astype(BF16),
                                                           preferred_element_type=F32)
            m_scr[...] = m_new

        @pl.when(j < i)
        def _():
            step(False)

        @pl.when(j == i)
        def _():
            step(True)
            o_ref[...] = (acc_scr[...] / l_scr[...]).astype(o_ref.dtype)
            lse_ref[...] = (m_scr[...] + jnp.log(l_scr[...]))[:, :1]

    return pl.pallas_call(
        body, name=name, grid=(h_n, nq, nq),
        in_specs=[
            pl.BlockSpec((tq, QK_NOPE), lambda h, i, j: (i, h)),
            pl.BlockSpec((None, tq, QK_ROPE), lambda h, i, j: (h, i, 0)),
            pl.BlockSpec((tq, QK_NOPE), lambda h, i, j: (jnp.minimum(j, i), 2 * h)),
            pl.BlockSpec((tq, V_HEAD), lambda h, i, j: (jnp.minimum(j, i), 2 * h + 1)),
            pl.BlockSpec((tq, QK_ROPE), lambda h, i, j: (jnp.minimum(j, i), 0)),
        ],
        out_specs=[
            pl.BlockSpec((tq, V_HEAD), lambda h, i, j: (i, h)),
            pl.BlockSpec((None, tq, 1), lambda h, i, j: (h, i, 0)),
        ],
        out_shape=[jax.ShapeDtypeStruct((t, h_n * V_HEAD), BF16),
                   jax.ShapeDtypeStruct((h_n, t, 1), F32)],
        scratch_shapes=[pltpu.VMEM((tq, LANE), F32), pltpu.VMEM((tq, LANE), F32),
                        pltpu.VMEM((tq, V_HEAD), F32)],
        compiler_params=_cp(("parallel", "parallel", "arbitrary")),
    )(qn, qr, kv, kv, kr)


def _attn_bwd_call(qn, qr, kv, kr, o, lse, do, name):
    t = qn.shape[0]
    h_n = MLA_HEADS
    tq = _rows(t, 512)
    nq = t // tq
    scale = (QK_NOPE + QK_ROPE) ** -0.5

    def body(qn_ref, qr_ref, kn_ref, v_ref, kr_ref, o_ref, lse_ref, do_ref,
             dqn_ref, dqr_ref, dkv_ref, dkr_ref, dqn_scr, dqr_scr, dkn_scr, dv_scr, dkr_scr):
        j, i = pl.program_id(1), pl.program_id(2)

        @pl.when(jnp.logical_and(j == 0, i == 0))
        def _():
            dqn_scr[...] = jnp.zeros_like(dqn_scr)
            dqr_scr[...] = jnp.zeros_like(dqr_scr)

        @pl.when(i == 0)
        def _():
            dkn_scr[...] = jnp.zeros_like(dkn_scr)
            dv_scr[...] = jnp.zeros_like(dv_scr)
            dkr_scr[...] = jnp.zeros_like(dkr_scr)

        def step(diag):
            qn_b = qn_ref[...].astype(BF16)
            qr_b = qr_ref[...].astype(BF16)
            kn_b = kn_ref[...].astype(BF16)
            kr_b = kr_ref[...].astype(BF16)
            do_b = do_ref[...]
            p = jnp.exp(_attn_scores(qn_ref, qr_ref, kn_ref, kr_ref, diag) - lse_ref[...])
            delta = jnp.sum(do_b.astype(F32) * o_ref[...].astype(F32), axis=-1, keepdims=True)
            dp = lax.dot_general(do_b, v_ref[...].astype(BF16), NT, preferred_element_type=F32)
            ds = (p * (dp - delta) * scale).astype(BF16)
            p_b = p.astype(BF16)
            dv_scr[...] += lax.dot_general(p_b, do_b, TN, preferred_element_type=F32)
            dkn_scr[...] += lax.dot_general(ds, qn_b, TN, preferred_element_type=F32)
            dkr_scr[...] += lax.dot_general(ds, qr_b, TN, preferred_element_type=F32)
            sl = pl.ds(pl.multiple_of(i * tq, tq), tq)
            dqn_scr[sl, :] += jnp.dot(ds, kn_b, preferred_element_type=F32)
            dqr_scr[sl, :] += jnp.dot(ds, kr_b, preferred_element_type=F32)

        @pl.when(i > j)
        def _():
            step(False)

        @pl.when(i == j)
        def _():
            step(True)

        @pl.when(i == nq - 1)
        def _():
            dkv_ref[:, :QK_NOPE] = dkn_scr[...].astype(dkv_ref.dtype)
            dkv_ref[:, QK_NOPE:] = dv_scr[...].astype(dkv_ref.dtype)
            dkr_ref[...] = dkr_scr[...]

        @pl.when(jnp.logical_and(j == nq - 1, i == nq - 1))
        def _():
            dqn_ref[...] = dqn_scr[...].astype(dqn_ref.dtype)
            dqr_ref[...] = dqr_scr[...].astype(dqr_ref.dtype)

    qi = lambda j, i: jnp.maximum(i, j)
    return pl.pallas_call(
        body, name=name, grid=(h_n, nq, nq),
        in_specs=[
            pl.BlockSpec((tq, QK_NOPE), lambda h, j, i: (qi(j, i), h)),
            pl.BlockSpec((None, tq, QK_ROPE), lambda h, j, i: (h, qi(j, i), 0)),
            pl.BlockSpec((tq, QK_NOPE), lambda h, j, i: (j, 2 * h)),
            pl.BlockSpec((tq, V_HEAD), lambda h, j, i: (j, 2 * h + 1)),
            pl.BlockSpec((tq, QK_ROPE), lambda h, j, i: (j, 0)),
            pl.BlockSpec((tq, V_HEAD), lambda h, j, i: (qi(j, i), h)),
            pl.BlockSpec((None, tq, 1), lambda h, j, i: (h, qi(j, i), 0)),
            pl.BlockSpec((tq, V_HEAD), lambda h, j, i: (qi(j, i), h)),
        ],
        out_specs=[
            pl.BlockSpec((t, QK_NOPE), lambda h, j, i: (0, h)),
            pl.BlockSpec((None, t, QK_ROPE), lambda h, j, i: (h, 0, 0)),
            pl.BlockSpec((tq, QK_NOPE + V_HEAD), lambda h, j, i: (j, h)),
            pl.BlockSpec((None, tq, QK_ROPE), lambda h, j, i: (h, j, 0)),
        ],
        out_shape=[jax.ShapeDtypeStruct((t, h_n * QK_NOPE), qn.dtype),
                   jax.ShapeDtypeStruct((h_n, t, QK_ROPE), qr.dtype),
                   jax.ShapeDtypeStruct((t, h_n * (QK_NOPE + V_HEAD)), kv.dtype),
                   jax.ShapeDtypeStruct((h_n, t, QK_ROPE), F32)],
        scratch_shapes=[pltpu.VMEM((t, QK_NOPE), F32), pltpu.VMEM((t, QK_ROPE), F32),
                        pltpu.VMEM((tq, QK_NOPE), F32), pltpu.VMEM((tq, V_HEAD), F32),
                        pltpu.VMEM((tq, QK_ROPE), F32)],
        compiler_params=_cp(("parallel", "arbitrary", "arbitrary")),
    )(qn, qr, kv, kv, kr, o, lse, do)


@functools.partial(jax.custom_vjp, nondiff_argnums=(4,))
def attention(qn, qr, kv, kr, tag):
    return _attn_fwd_call(qn, qr, kv, kr, "attn_" + tag)[0]


def _attention_f(qn, qr, kv, kr, tag):
    o, lse = _attn_fwd_call(qn, qr, kv, kr, "attn_" + tag)
    return o, (qn, qr, kv, kr, o, lse)


def _attention_b(tag, res, do):
    dqn, dqr, dkv, dkr_h = _attn_bwd_call(*res, do, "attn_" + tag + "_bwd")
    return dqn, dqr, dkv, jnp.sum(dkr_h, axis=0).astype(res[3].dtype)


attention.defvjp(_attention_f, _attention_b)


def _shift_down(u, s):
    if s == 0:
        return u
    t = u.shape[0]
    rolled = pltpu.roll(u, s, 0)
    return jnp.where(lax.broadcasted_iota(jnp.int32, u.shape, 0) >= s, rolled, 0.0)


def _shift_up(u, s):
    if s == 0:
        return u
    t = u.shape[0]
    rolled = pltpu.roll(u, t - s, 0)
    return jnp.where(lax.broadcasted_iota(jnp.int32, u.shape, 0) < t - s, rolled, 0.0)


def _conv_blocks(t, c3):
    p = c3 // 3
    tc = _tile(p, 512)
    per = p // tc
    return p, tc, per


def _conv_fwd_call(u, w, name):
    t, c3 = u.shape
    p, tc, per = _conv_blocks(t, c3)

    def body(u_ref, w_ref, o_ref):
        u = u_ref[...].astype(F32)
        y = jnp.zeros_like(u)
        for j in range(CONV_WIDTH):
            y = y + w_ref[j:j + 1, :] * _shift_down(u, CONV_WIDTH - 1 - j)
        o_ref[...] = y * _sigmoid(y)

    return pl.pallas_call(
        body, name=name, grid=(c3 // tc,),
        in_specs=[pl.BlockSpec((t, tc), lambda cb: (0, cb)),
                  pl.BlockSpec((CONV_WIDTH, tc), lambda cb: (0, cb))],
        out_specs=pl.BlockSpec((None, t, tc), lambda cb: (cb // per, 0, cb % per)),
        out_shape=jax.ShapeDtypeStruct((3, t, p), F32),
        compiler_params=_cp(("parallel",)),
    )(u, w)


def _conv_bwd_call(u, w, do, name):
    t, c3 = u.shape
    p, tc, per = _conv_blocks(t, c3)

    def body(u_ref, w_ref, do_ref, du_ref, dw_ref):
        u = u_ref[...].astype(F32)
        shifted = [_shift_down(u, CONV_WIDTH - 1 - j) for j in range(CONV_WIDTH)]
        y = jnp.zeros_like(u)
        for j in range(CONV_WIDTH):
            y = y + w_ref[j:j + 1, :] * shifted[j]
        s = _sigmoid(y)
        dy = do_ref[...] * s * (1.0 + y * (1.0 - s))
        du = jnp.zeros_like(u)
        for j in range(CONV_WIDTH):
            du = du + w_ref[j:j + 1, :] * _shift_up(dy, CONV_WIDTH - 1 - j)
            dw_ref[j:j + 1, :] = jnp.sum(dy * shifted[j], axis=0, keepdims=True)
        du_ref[...] = du.astype(du_ref.dtype)

    return pl.pallas_call(
        body, name=name, grid=(c3 // tc,),
        in_specs=[pl.BlockSpec((t, tc), lambda cb: (0, cb)),
                  pl.BlockSpec((CONV_WIDTH, tc), lambda cb: (0, cb)),
                  pl.BlockSpec((None, t, tc), lambda cb: (cb // per, 0, cb % per))],
        out_specs=[pl.BlockSpec((t, tc), lambda cb: (0, cb)),
                   pl.BlockSpec((CONV_WIDTH, tc), lambda cb: (0, cb))],
        out_shape=[jax.ShapeDtypeStruct((t, c3), u.dtype), jax.ShapeDtypeStruct((CONV_WIDTH, c3), F32)],
        compiler_params=_cp(("parallel",)),
    )(u, w, do)


@functools.partial(jax.custom_vjp, nondiff_argnums=(2,))
def conv_silu(u, w, tag):
    return _conv_fwd_call(u, w, "conv_" + tag)


def _conv_silu_f(u, w, tag):
    return _conv_fwd_call(u, w, "conv_" + tag), (u, w)


def _conv_silu_b(tag, res, do):
    return tuple(_conv_bwd_call(*res, do, "conv_" + tag + "_bwd"))


conv_silu.defvjp(_conv_silu_f, _conv_silu_b)


BNN = (((2,), (1,)), ((0,), (0,)))
BNT = (((2,), (2,)), ((0,), (0,)))
BTN = (((1,), (1,)), ((0,), (0,)))


def _hdot(a, b, dn=BNN):
    return lax.dot_general(a, b, dn, precision=HIGHEST, preferred_element_type=F32)


def _bf16_dot(a, b, dn):
    return lax.dot_general(a.astype(BF16), b.astype(BF16), dn, preferred_element_type=F32)


@functools.partial(jax.custom_vjp, nondiff_argnums=(2,))
def _bdot(a, b, dn=BNN):
    return _bf16_dot(a, b, dn)


def _bdot_f(a, b, dn):
    return _bf16_dot(a, b, dn), (a, b)


def _bdot_b(dn, res, g):
    a, b = res
    if dn == BNN:
        return _bf16_dot(g, b, BNT), _bf16_dot(a, g, BTN)
    if dn == BNT:
        return _bf16_dot(g, b, BNN), _bf16_dot(g, a, BTN)
    return _bf16_dot(b, g, BNT), _bf16_dot(a, g, BNN)


_bdot.defvjp(_bdot_f, _bdot_b)


GDN_HEADS_PER_STEP = 8
GDN_HEADS_PER_STEP_BWD = 4


def _gdn_chunk(q, k, v, z, bl, al, a_log, dtb, gn, s):
    b, c = q.shape[0], q.shape[1]
    ri = lax.broadcasted_iota(jnp.int32, (c, c), 0)
    ci = lax.broadcasted_iota(jnp.int32, (c, c), 1)
    lower = (ri >= ci)[None]
    strict = (ri > ci)[None]
    low_incl = jnp.broadcast_to((ri >= ci).astype(F32), (b, c, c))
    up_incl = jnp.broadcast_to((ri <= ci).astype(F32), (b, c, c))
    eye = (ri == ci).astype(F32)[None]

    q = q * lax.rsqrt(jnp.sum(q * q, axis=-1, keepdims=True) + EPS) * (GDN_DK ** -0.5)
    k = k * lax.rsqrt(jnp.sum(k * k, axis=-1, keepdims=True) + EPS)
    beta = _sigmoid(bl)
    g = -jnp.exp(a_log) * _softplus(al + dtb)
    g_w = jnp.broadcast_to(g, (b, c, LANE))
    gc = _hdot(low_incl, g_w)
    gr = _hdot(g_w[:, :, :c], up_incl, BTN)
    diff = gc[:, :, :c] - gr
    decay = jnp.where(lower, jnp.exp(jnp.where(lower, diff, 0.0)), 0.0)
    kb = k * beta
    lmat = jnp.where(strict, _bdot(kb, k, BNT) * decay, 0.0)
    inv = eye - lmat
    pw = lmat
    for _ in range(int(math.log2(c)) - 1):
        pw = _hdot(pw, pw)
        inv = _hdot(inv, eye + pw)
    eg = jnp.exp(gc)
    u = _hdot(inv, v * beta)
    w = _hdot(inv, kb * eg)
    attn = jnp.where(lower, _bdot(q, k, BNT) * decay, 0.0)
    v_new = u - _bdot(w, s)
    o = _bdot(q * eg, s) + _bdot(attn, v_new)
    g_last = jnp.sum(g_w, axis=1, keepdims=True)
    k_dec = k * jnp.exp(g_last - gc)
    s_new = s * jnp.exp(g_last) + _bdot(k_dec, v_new, BTN)
    on = o * lax.rsqrt(jnp.mean(o * o, axis=-1, keepdims=True) + EPS) * gn
    return on * (z * _sigmoid(z)), s_new


def _gdn_specs(n_chunks, hb, rev):
    c = CHUNK
    nn = (lambda n: n_chunks - 1 - n) if rev else (lambda n: n)
    plane = lambda pidx: pl.BlockSpec((None, c, hb * GDN_DK), lambda hg, n: (pidx, nn(n), hg))
    col = pl.BlockSpec((hb, c, 1), lambda hg, n: (hg, nn(n), 0))
    scal = pl.BlockSpec((hb, 1, 1), lambda hg, n: (hg, 0, 0))
    zspec = pl.BlockSpec((c, hb * GDN_DV), lambda hg, n: (nn(n), hg))
    gnspec = pl.BlockSpec((1, GDN_DV), lambda hg, n: (0, 0))
    sspec = pl.BlockSpec((hb, None, GDN_DK, GDN_DV), lambda hg, n: (hg, nn(n), 0, 0))
    return plane, col, scal, zspec, gnspec, sspec


def _heads_per_step(want):
    return math.gcd(want, GDN_HEADS)


def _heads(ref, hb):
    return jnp.stack([ref[:, j * GDN_DK:(j + 1) * GDN_DK] for j in range(hb)])


def _gdn_fwd_call(qkv, z, bl, al, a_log, dtb, gn, name):
    t = z.shape[0]
    h_n = GDN_HEADS
    hb = _heads_per_step(GDN_HEADS_PER_STEP)
    n_chunks = t // CHUNK
    plane, col, scal, zspec, gnspec, sspec = _gdn_specs(n_chunks, hb, False)

    def body(q_ref, k_ref, v_ref, z_ref, bl_ref, al_ref, a_ref, dtb_ref, gn_ref, o_ref, sall_ref, s_scr):
        n = pl.program_id(1)

        @pl.when(n == 0)
        def _():
            s_scr[...] = jnp.zeros_like(s_scr)

        s = s_scr[...]
        sall_ref[...] = s
        o, s_new = _gdn_chunk(_heads(q_ref, hb), _heads(k_ref, hb), _heads(v_ref, hb),
                              _heads(z_ref, hb).astype(F32),
                              bl_ref[...], al_ref[...], a_ref[...], dtb_ref[...], gn_ref[...], s)
        for j in range(hb):
            o_ref[:, j * GDN_DV:(j + 1) * GDN_DV] = o[j].astype(o_ref.dtype)
        s_scr[...] = s_new

    return pl.pallas_call(
        body, name=name, grid=(h_n // hb, n_chunks),
        in_specs=[plane(0), plane(1), plane(2), zspec, col, col, scal, scal, gnspec],
        out_specs=[zspec, sspec],
        out_shape=[jax.ShapeDtypeStruct((t, h_n * GDN_DV), BF16),
                   jax.ShapeDtypeStruct((h_n, n_chunks, GDN_DK, GDN_DV), F32)],
        scratch_shapes=[pltpu.VMEM((hb, GDN_DK, GDN_DV), F32)],
        compiler_params=_cp(("parallel", "arbitrary")),
    )(qkv, qkv, qkv, z, bl, al, a_log, dtb, gn)


def _gdn_bwd_call(qkv, z, bl, al, a_log, dtb, gn, sall, do, name):
    t = z.shape[0]
    h_n = GDN_HEADS
    hb = _heads_per_step(GDN_HEADS_PER_STEP_BWD)
    n_chunks = t // CHUNK
    c = CHUNK
    plane, col, scal, zspec, gnspec, sspec = _gdn_specs(n_chunks, hb, True)
    dplanes = pl.BlockSpec((3, c, hb * GDN_DK), lambda hg, n: (0, n_chunks - 1 - n, hg))
    gnh = pl.BlockSpec((None, 1, GDN_DV), lambda hg, n: (hg, 0, 0))

    def body(q_ref, k_ref, v_ref, z_ref, bl_ref, al_ref, a_ref, dtb_ref, gn_ref, s_ref, do_ref,
             dqkv_ref, dz_ref, dbl_ref, dal_ref, da_ref, ddtb_ref, dgn_ref, ds_scr):
        n = pl.program_id(1)

        @pl.when(n == 0)
        def _():
            ds_scr[...] = jnp.zeros_like(ds_scr)
            da_ref[...] = jnp.zeros_like(da_ref)
            ddtb_ref[...] = jnp.zeros_like(ddtb_ref)
            dgn_ref[...] = jnp.zeros_like(dgn_ref)

        _, vjp = jax.vjp(_gdn_chunk, _heads(q_ref, hb), _heads(k_ref, hb), _heads(v_ref, hb),
                              _heads(z_ref, hb).astype(F32),
                         bl_ref[...], al_ref[...], a_ref[...], dtb_ref[...], gn_ref[...], s_ref[...])
        dq, dk, dv, dz, dbl, dal, da, ddtb, dgn, ds = vjp((_heads(do_ref, hb).astype(F32), ds_scr[...]))
        for j in range(hb):
            hs = slice(j * GDN_DK, (j + 1) * GDN_DK)
            dqkv_ref[0, :, hs] = dq[j]
            dqkv_ref[1, :, hs] = dk[j]
            dqkv_ref[2, :, hs] = dv[j]
            dz_ref[:, hs] = dz[j].astype(dz_ref.dtype)
        dbl_ref[...] = dbl
        dal_ref[...] = dal
        da_ref[...] += da
        ddtb_ref[...] += ddtb
        dgn_ref[...] += dgn
        ds_scr[...] = ds

    return pl.pallas_call(
        body, name=name, grid=(h_n // hb, n_chunks),
        in_specs=[plane(0), plane(1), plane(2), zspec, col, col, scal, scal, gnspec, sspec, zspec],
        out_specs=[dplanes, zspec, col, col, scal, scal, gnh],
        out_shape=[jax.ShapeDtypeStruct((3, t, h_n * GDN_DK), F32),
                   jax.ShapeDtypeStruct((t, h_n * GDN_DV), z.dtype),
                   jax.ShapeDtypeStruct((h_n, t, 1), F32), jax.ShapeDtypeStruct((h_n, t, 1), F32),
                   jax.ShapeDtypeStruct((h_n, 1, 1), F32), jax.ShapeDtypeStruct((h_n, 1, 1), F32),
                   jax.ShapeDtypeStruct((h_n // hb, 1, GDN_DV), F32)],
        scratch_shapes=[pltpu.VMEM((hb, GDN_DK, GDN_DV), F32)],
        compiler_params=_cp(("parallel", "arbitrary")),
    )(qkv, qkv, qkv, z, bl, al, a_log, dtb, gn, sall, do)


@functools.partial(jax.custom_vjp, nondiff_argnums=(7,))
def gdn(qkv, z, bl, al, a_log, dtb, gn, tag):
    return _gdn_fwd_call(qkv, z, bl, al, a_log, dtb, gn, "gdn_" + tag)[0]


def _gdn_f(qkv, z, bl, al, a_log, dtb, gn, tag):
    o, sall = _gdn_fwd_call(qkv, z, bl, al, a_log, dtb, gn, "gdn_" + tag)
    return o, (qkv, z, bl, al, a_log, dtb, gn, sall)


def _gdn_b(tag, res, do):
    dqkv, dz, dbl, dal, da, ddtb, dgn_h = _gdn_bwd_call(*res, do, "gdn_" + tag + "_bwd")
    return dqkv, dz, dbl, dal, da, ddtb, jnp.sum(dgn_h, axis=0)


gdn.defvjp(_gdn_f, _gdn_b)


def adamw(w, parts, m, v, name):
    n_layers = len(parts)
    n_parts, r, c = parts[0].shape
    assert w.shape == (n_layers * r, c), (w.shape, parts[0].shape)
    tr = r
    for cand in (512, 256, 128, 64, 32, 16, 8):
        if r % cand == 0 and cand * c <= 256 * 1024:
            tr = cand
            break
    nb = r // tr
    blk = pl.BlockSpec((tr, c), lambda l, i: (l * nb + i, 0))
    bc1 = 1.0 - ADAM_B1 ** ADAM_STEP
    bc2 = 1.0 - ADAM_B2 ** ADAM_STEP

    def part_spec(li):
        return pl.BlockSpec((n_parts, tr, c),
                            lambda l, i: (0, jnp.where(l == li, i, jnp.where(l < li, 0, nb - 1)), 0))

    def body(*refs):
        w_ref, p_refs = refs[0], refs[1:1 + n_layers]
        m_ref, v_ref, g_ref, d_ref, mo_ref, vo_ref = refs[1 + n_layers:]
        for li in range(n_layers):
            @pl.when(pl.program_id(0) == li)
            def _(p_ref=p_refs[li]):
                g = p_ref[0].astype(F32)
                for i in range(1, n_parts):
                    g = g + p_ref[i].astype(F32)
                m2 = ADAM_B1 * m_ref[...] + (1.0 - ADAM_B1) * g
                v2 = ADAM_B2 * v_ref[...] + (1.0 - ADAM_B2) * (g * g)
                g_ref[...] = g
                mo_ref[...] = m2
                vo_ref[...] = v2
                d_ref[...] = -ADAM_LR * ((m2 / bc1) / (jnp.sqrt(v2 / bc2) + ADAM_EPS)
                                         + ADAM_WD * w_ref[...])

    return pl.pallas_call(
        body, name=name, grid=(n_layers, nb),
        in_specs=[blk] + [part_spec(li) for li in range(n_layers)] + [blk, blk],
        out_specs=[blk] * 4, out_shape=[jax.ShapeDtypeStruct(w.shape, F32)] * 4,
        compiler_params=_cp(("arbitrary", "arbitrary")),
    )(w, *parts, m, v)


_HBM = pl.BlockSpec(memory_space=pltpu.HBM)
_SEM = pl.BlockSpec(memory_space=pltpu.SEMAPHORE)
_EFFECT = pltpu.SideEffectType.DATAFLOW_SIDE_EFFECTING


def _peer(x, y, c, d):
    px = 1 - x if d & 4 else x
    py = 1 - y if d & 2 else y
    pc = 1 - c if d & 1 else c
    return (px, py, pc), 4 * px + 2 * py + pc


ALL_PEERS = (1, 2, 3, 4, 5, 6, 7)
SIBLING = 1
SAME_CORE_REMOTE = (2, 4, 6)


def copy_start(arrays, mode, carry, name):
    n = len(arrays)
    if mode == "forward":
        lands = []
    else:
        lands = [lax.empty(a.shape if mode == "scatter" else (N_DEV,) + a.shape, a.dtype) for a in arrays]
    n_in = n + len(lands) + 1

    def body(*refs):
        srcs = refs[:n]
        dsts = refs[n:2 * n] if lands else srcs
        sems = refs[n_in:n_in + 2 * n]
        x, y, c = (lax.axis_index(a) for a in AXES)
        me = 4 * x + 2 * y + c
        for k in range(n):
            if mode == "forward":
                sibling, _ = _peer(x, y, c, SIBLING)
                copies = [(srcs[k].at[_peer(x, y, c, d)[1]], dsts[k].at[_peer(x, y, c, d)[1]], sibling)
                          for d in SAME_CORE_REMOTE]
            elif mode == "gather":
                copies = [(srcs[k], dsts[k].at[me], _peer(x, y, c, d)[0]) for d in (SIBLING,) + SAME_CORE_REMOTE]
            else:
                copies = [(srcs[k].at[_peer(x, y, c, d)[1]], dsts[k].at[me], _peer(x, y, c, d)[0])
                          for d in ALL_PEERS]
            for src, dst, peer in copies:
                pltpu.make_async_remote_copy(src_ref=src, dst_ref=dst, send_sem=sems[2 * k],
                                             recv_sem=sems[2 * k + 1], device_id=peer,
                                             device_id_type=pl.DeviceIdType.MESH).start()

    operands = list(arrays) + lands + [carry]
    outs = pl.pallas_call(
        body, name=name,
        out_shape=tuple([pltpu.SemaphoreType.DMA(())] * (2 * n)
                        + [pltpu.HBM(a.shape, a.dtype) for a in operands]),
        in_specs=[_HBM] * n_in,
        out_specs=tuple([_SEM] * (2 * n) + [_HBM] * n_in),
        input_output_aliases={i: 2 * n + i for i in range(n_in)},
        compiler_params=pltpu.CompilerParams(has_side_effects=_EFFECT),
    )(*[pltpu.with_memory_space_constraint(a, pltpu.HBM) for a in operands])
    sems, thru = outs[:2 * n], outs[2 * n:-1]
    handles = [(sems[2 * k], sems[2 * k + 1], thru[k] if lands else None, thru[n + k] if lands else thru[k])
               for k in range(n)]
    return outs[-1], handles


def copy_wait(handles, n_blocks, after, name):
    n = len(handles)
    sems = [s for h in handles for s in h[:2]]
    srcs = [h[2] for h in handles if h[2] is not None]
    lands = [h[3] for h in handles]
    ns = len(srcs)

    def body(*refs):
        dsts = refs[ns:ns + n]
        sem_refs = refs[ns + n:ns + 3 * n]
        x, y, c = (lax.axis_index(a) for a in AXES)
        for k in range(n):
            blocks = dsts[k].at[pl.ds(0, n_blocks)]
            pltpu.make_async_remote_copy(
                src_ref=blocks, dst_ref=blocks, send_sem=sem_refs[2 * k], recv_sem=sem_refs[2 * k + 1],
                device_id=(x, y, c), device_id_type=pl.DeviceIdType.MESH).wait()

    outs = pl.pallas_call(
        body, name=name,
        out_shape=tuple([pltpu.HBM(a.shape, a.dtype) for a in srcs + lands]),
        in_specs=[_HBM] * (ns + n) + [_SEM] * (2 * n) + [pl.BlockSpec(memory_space=pl.ANY)],
        out_specs=tuple([_HBM] * (ns + n)),
        input_output_aliases={i: i for i in range(ns + n)},
        compiler_params=pltpu.CompilerParams(has_side_effects=_EFFECT),
    )(*srcs, *lands, *sems, after)
    return (list(outs[:ns]) if ns else [None] * n), list(outs[ns:])


def exchange(arrays, modes, name):
    n = len(arrays)
    hbm = pl.BlockSpec(memory_space=pltpu.HBM)
    out_shape = [jax.ShapeDtypeStruct(a.shape if md == "scatter" else (N_DEV,) + a.shape, a.dtype)
                 for a, md in zip(arrays, modes)]

    def body(*refs):
        ins, outs = refs[:n], refs[n:2 * n]
        send_sems, recv_sems, local_sems = refs[2 * n:]
        x, y, c = (lax.axis_index(a) for a in AXES)
        me = 4 * x + 2 * y + c

        def src(k, p):
            return ins[k].at[p] if modes[k] == "scatter" else ins[k]

        local = [pltpu.make_async_copy(src(k, me), outs[k].at[me], local_sems.at[k]) for k in range(n)]
        for cp in local:
            cp.start()
        started = []
        for d in range(1, N_DEV):
            px = 1 - x if d & 4 else x
            py = 1 - y if d & 2 else y
            pc = 1 - c if d & 1 else c
            pid = 4 * px + 2 * py + pc
            for k in range(n):
                pltpu.make_async_remote_copy(
                    src_ref=src(k, pid), dst_ref=outs[k].at[me],
                    send_sem=send_sems.at[k, d - 1], recv_sem=recv_sems.at[k, d - 1],
                    device_id=(px, py, pc), device_id_type=pl.DeviceIdType.MESH).start()
                started.append((k, d, pid, (px, py, pc)))
        for k, d, pid, peer in started:
            pltpu.make_async_remote_copy(
                src_ref=src(k, pid), dst_ref=outs[k].at[pid],
                send_sem=send_sems.at[k, d - 1], recv_sem=recv_sems.at[k, d - 1],
                device_id=peer, device_id_type=pl.DeviceIdType.MESH).wait()
        for cp in local:
            cp.wait()

    outs = pl.pallas_call(
        body, name=name, in_specs=[hbm] * n, out_specs=[hbm] * n, out_shape=out_shape,
        scratch_shapes=[pltpu.SemaphoreType.DMA((n, N_DEV - 1)), pltpu.SemaphoreType.DMA((n, N_DEV - 1)),
                        pltpu.SemaphoreType.DMA((n,))],
        compiler_params=pltpu.CompilerParams(has_side_effects=True),
    )(*arrays)
    return list(outs)


BIG = ("w_in", "w_uq", "w_ukv", "w_o_mla", "w_o_gdn", "w_o", "w_gate_up", "w_down")
ROW_SHARDED = ("w_o", "w_down")
SMALL = ("b_ada", "norm_mix", "norm_ffn", "q_a_norm", "kv_a_norm", "A_log", "dt_bias", "gdn_norm",
         "final_norm")
WEIGHTS = ("w_ada", "b_ada", "norm_mix", "norm_ffn", "w_in", "q_a_norm", "kv_a_norm", "w_uq", "w_ukv",
           "w_o_mla", "conv_w", "A_log", "dt_bias", "gdn_norm", "w_o_gdn", "w_o", "w_gate_up", "w_down",
           "final_norm")


def _unslot(g):
    return g.transpose(1, 0, 2).reshape(g.shape[1], -1)


def _cols(g):
    return g if g.shape[-1] % LANE == 0 else _unslot(g)


def _stack_rows(g):
    return g.reshape(-1, g.shape[-1])


def _pad_cols(a):
    return jnp.pad(a, ((0, 0), (0, _pad_lanes(a.shape[1]) - a.shape[1])))


def _rope(xv, cos, sin):
    x1, x2 = jnp.split(xv, 2, axis=-1)
    return jnp.concatenate([x1 * cos - x2 * sin, x2 * cos + x1 * sin], axis=-1)


MIX_WEIGHTS = ("w_uq", "w_ukv", "w_o_mla", "w_o_gdn", "w_o")
FFN_WEIGHTS = ("w_gate_up", "w_down")


def _stage_in(x, mod, nm, w_in_s, tg):
    d = x.shape[1]
    hg = GDN_HEADS
    w_in = _unslot(w_in_s)
    o1 = Q_LORA + KV_LORA + QK_ROPE
    o2 = o1 + 2 * hg * GDN_DK + hg * GDN_DV
    o3 = o2 + hg * GDN_DV
    o4 = o3 + 2 * hg
    h = ada_norm(x, nm, mod[:, d:2 * d], mod[:, :d], "mix" + tg)
    return (mm(h, _pad_cols(w_in[:, :o1]), "in_a" + tg, BF16), mm(h, w_in[:, o1:o2], "in_qkv" + tg, BF16),
            mm(h, w_in[:, o2:o3], "in_z" + tg, BF16), mm(h, _pad_cols(w_in[:, o3:o4]), "in_ba" + tg, F32),
            mm(h, w_in[:, o4:o4 + 2 * d], "in_g" + tg, BF16))


def _stage_mix(x, mod, seg_a, qkv, z, ba, gl, w_uq_s, w_ukv_s, w_o_mla_s, w_o_gdn_s, w_o_s, conv_s,
               qan, kvan, a_log, dtb, gn, cos, sin, tg):
    t, d = x.shape
    hq, hg = MLA_HEADS, GDN_HEADS
    w_uq = _unslot(w_uq_s).reshape(Q_LORA, hq, QK_NOPE + QK_ROPE)
    w_uq = jnp.concatenate([w_uq[:, :, :QK_NOPE].reshape(Q_LORA, hq * QK_NOPE),
                            w_uq[:, :, QK_NOPE:].reshape(Q_LORA, hq * QK_ROPE)], axis=1)
    c_q = seg_a[:, :Q_LORA]
    c_kv = seg_a[:, Q_LORA:Q_LORA + KV_LORA]
    k_pe = seg_a[:, Q_LORA + KV_LORA:Q_LORA + KV_LORA + QK_ROPE]
    qf = mm(rms_norm(c_q, qan, "qa" + tg), w_uq, "uq" + tg, BF16)
    kvf = mm(rms_norm(c_kv, kvan, "kva" + tg), _cols(w_ukv_s), "ukv" + tg, BF16)
    qn = qf[:, :hq * QK_NOPE]
    q_pe = qf[:, hq * QK_NOPE:].astype(F32).reshape(t, hq, QK_ROPE)
    qr = _rope(q_pe, cos[:, None, :], sin[:, None, :]).transpose(1, 0, 2).astype(BF16)
    kr = _rope(k_pe.astype(F32), cos, sin).astype(BF16)
    y_a = mm(attention(qn, qr, kvf, kr, tg), _cols(w_o_mla_s), "o_mla" + tg, BF16)
    conv_w = conv_s.transpose(1, 0, 2).reshape(CONV_WIDTH, -1)
    qkv_c = conv_silu(qkv, conv_w, tg)
    bl = ba[:, :hg].T[:, :, None]
    al = ba[:, hg:2 * hg].T[:, :, None]
    o_gdn = gdn(qkv_c, z, bl, al, a_log.reshape(hg, 1, 1), dtb.reshape(hg, 1, 1), gn, tg)
    y_b = mm(o_gdn, _cols(w_o_gdn_s), "o_gdn" + tg, BF16)
    mix = mm(gate_mix(gl, y_a, y_b, tg), _stack_rows(w_o_s), "w_o" + tg, F32)
    return resid(x, mod[:, 2 * d:3 * d], mix, "mix" + tg)


def _stage_ffn(x, mod, nf, w_gu_s, w_down_s, tg):
    d = x.shape[1]
    h = ada_norm(x, nf, mod[:, 4 * d:5 * d], mod[:, 3 * d:4 * d], "ffn" + tg)
    gu = mm(h, _cols(w_gu_s), "gu" + tg, BF16)
    dn = mm(swiglu(gu, tg), _stack_rows(w_down_s), "down" + tg, F32)
    return resid(x, mod[:, 5 * d:6 * d], dn, "ffn" + tg)


def _flat_row(arrs):
    v = jnp.concatenate([a.reshape(-1) for a in arrs])
    return jnp.pad(v, (0, _pad_lanes(v.shape[0]) - v.shape[0]))[None, :]


def kernel(x, c, positions, w_ada, b_ada, norm_mix, norm_ffn, w_in, q_a_norm, kv_a_norm, w_uq, w_ukv, w_o_mla, conv_w, A_log, dt_bias, gdn_norm, w_o_gdn, w_o, w_gate_up, w_down, final_norm, loss_target, m_w_ada, m_b_ada, m_norm_mix, m_norm_ffn, m_w_in, m_q_a_norm, m_kv_a_norm, m_w_uq, m_w_ukv, m_w_o_mla, m_conv_w, m_A_log, m_dt_bias, m_gdn_norm, m_w_o_gdn, m_w_o, m_w_gate_up, m_w_down, m_final_norm, v_w_ada, v_b_ada, v_norm_mix, v_norm_ffn, v_w_in, v_q_a_norm, v_kv_a_norm, v_w_uq, v_w_ukv, v_w_o_mla, v_conv_w, v_A_log, v_dt_bias, v_gdn_norm, v_w_o_gdn, v_w_o, v_w_gate_up, v_w_down, v_final_norm):
    given = dict(locals())
    t, d = x.shape[1], x.shape[2]
    n_ada = w_ada.shape[2]
    me = 4 * lax.axis_index("x") + 2 * lax.axis_index("y") + lax.axis_index("c")

    def with_own(land, own):
        return lax.dynamic_update_slice(land, own[None], (me,) + (0,) * own.ndim)

    got = exchange([c, conv_w], ["gather", "gather"], "gather_small")
    c_all, conv_g = got[0].reshape(N_DEV, d), got[1]
    c_rows = jnp.pad(c_all, ((0, 16 - N_DEV), (0, 0)))
    mod_cols = jnp.stack([_mm(c_rows, w_ada[l], "nn", F32, "ada_mod%d" % l, a_act="silu")[:N_DEV]
                          for l in range(DEPTH)], axis=1)
    mod_mine = exchange([mod_cols], ["scatter"], "scatter_mod")[0]
    mods = mod_mine.transpose(1, 0, 2).reshape(DEPTH, N_DEV * n_ada) + b_ada

    groups = [[(n, l) for n in names] for l in range(DEPTH) for names in (("w_in",), MIX_WEIGHTS, FFN_WEIGHTS)]
    gtags = [s + str(l) for l in range(DEPTH) for s in ("in", "mix", "ffn")]
    keys = [k for ks in groups for k in ks]
    mods, handles = copy_start([given[n][l].astype(BF16) for n, l in keys], "gather", mods, "gather_start")
    handles = dict(zip(keys, handles))
    own, relayed = {}, {}

    def relay(gi, carry):
        ks = groups[gi]
        srcs, lands = copy_wait([handles[k] for k in ks], 1 + len(SAME_CORE_REMOTE), carry,
                                "wait_ici_" + gtags[gi])
        own.update(zip(ks, srcs))
        carry, hs = copy_start(lands, "forward", carry, "relay_" + gtags[gi])
        relayed.update(zip(ks, hs))
        return carry

    def landed(gi, after):
        ks = groups[gi]
        _, lands = copy_wait([relayed[k] for k in ks], len(SAME_CORE_REMOTE), after, "wait_" + gtags[gi])
        return [with_own(land, own[k]) for k, land in zip(ks, lands)]

    inv_freq = 1.0 / (ROPE_THETA ** (jnp.arange(0, QK_ROPE, 2, dtype=F32) / QK_ROPE))
    ang = positions[0].astype(F32)[:, None] * inv_freq
    cos, sin = jnp.cos(ang), jnp.sin(ang)
    xl = x[0]
    mods = relay(0, mods)
    vjps = []
    for l in range(DEPTH):
        tg = str(l)
        gi = 3 * l
        mod = mods[l:l + 1]
        xl = relay(gi + 1, xl)
        (w_in_s,) = landed(gi, xl)
        seg, vjp_in = jax.vjp(lambda *a, tg=tg: _stage_in(*a, tg), xl, mod, norm_mix[l:l + 1], w_in_s)
        seg = (relay(gi + 2, seg[0]),) + tuple(seg[1:])
        w_mix = landed(gi + 1, seg[0])
        xm, vjp_mix = jax.vjp(lambda *a, tg=tg: _stage_mix(*a, cos, sin, tg), xl, mod, *seg, *w_mix,
                              conv_g[:, l], q_a_norm[l:l + 1], kv_a_norm[l:l + 1], A_log[l], dt_bias[l],
                              gdn_norm[l:l + 1])
        if gi + 3 < len(groups):
            xm = relay(gi + 3, xm)
        w_ffn = landed(gi + 2, xm)
        xl, vjp_ffn = jax.vjp(lambda *a, tg=tg: _stage_ffn(*a, tg), xm, mod, norm_ffn[l:l + 1], *w_ffn)
        vjps.append((vjp_in, vjp_mix, vjp_ffn))

    loss_t, g, dfn = loss_head(xl, final_norm[None, :], loss_target[0])
    loss = lax.psum(loss_t[0, 0], AXES)
    dsmall = {n: [None] * DEPTH for n in SMALL + ("conv_w",)}
    dmods = [None] * DEPTH
    sent = {}

    def send(ks, grads, carry, name):
        carry, hs = copy_start(list(grads), "scatter", carry, name)
        sent.update(zip(ks, hs))
        return carry

    for l in reversed(range(DEPTH)):
        tg = str(l)
        vjp_in, vjp_mix, vjp_ffn = vjps[l]
        dxm, dmod_f, dsmall["norm_ffn"][l], *dw = vjp_ffn(g)
        dxm = send([(n, l) for n in FFN_WEIGHTS], dw, dxm, "scatter_ffn" + tg)
        dx_m, dmod_m, *rest = vjp_mix(dxm)
        dseg, dw, rest = rest[:5], rest[5:5 + len(MIX_WEIGHTS)], rest[5 + len(MIX_WEIGHTS):]
        dseg[0] = send([(n, l) for n in MIX_WEIGHTS], dw, dseg[0], "scatter_mix" + tg)
        for n, gr in zip(("conv_w", "q_a_norm", "kv_a_norm", "A_log", "dt_bias", "gdn_norm"), rest):
            dsmall[n][l] = gr
        dx_i, dmod_i, dsmall["norm_mix"][l], dw_in = vjp_in(tuple(dseg))
        g = dx_i + dx_m
        if l > 0:
            g = send([("w_in", l)], [dw_in], g, "scatter_in" + tg)
        dmods[l] = dmod_f + dmod_m + dmod_i
    dx = g
    dmods = jnp.concatenate(dmods, axis=0)
    dconv = jnp.stack(dsmall.pop("conv_w"), axis=1)
    dsmall = {n: jnp.concatenate(v, axis=0) if v[0].ndim == 2 else jnp.stack(v)
              for n, v in dsmall.items() if v[0] is not None}
    dsmall["b_ada"] = dmods
    dsmall["final_norm"] = dfn[0]

    dmod_cols = dmods.reshape(DEPTH, N_DEV, n_ada).transpose(1, 0, 2)
    conv_parts, dmod_all, small_parts = exchange(
        [dconv, dmod_cols, _flat_row([dsmall[n] for n in SMALL])], ["scatter", "scatter", "gather"],
        "exchange_small")
    dmod_all = send([("w_in", 0)], [dw_in], dmod_all, "scatter_in0")

    res = {}
    dm_rows = jnp.pad(dmod_all, ((0, 16 - N_DEV), (0, 0), (0, 0)))
    g_ada = jnp.stack([_mm(c_rows, dm_rows[:, l], "tn", F32, "ada_dw%d" % l, a_act="silu")
                       for l in range(DEPTH)])
    r2 = (DEPTH * d, n_ada)
    outs = adamw(w_ada.reshape(r2), [g_ada.reshape((1,) + r2)], m_w_ada.reshape(r2), v_w_ada.reshape(r2),
                 "adamw_w_ada")
    res["w_ada"] = [o.reshape(w_ada.shape) for o in outs]
    packed = SMALL + ("conv_w",)
    p_all = jnp.concatenate([small_parts, conv_parts.reshape(N_DEV, 1, -1)], axis=2)
    pack = lambda pre: jnp.concatenate([_flat_row([given[pre + n] for n in SMALL]),
                                        given[pre + "conv_w"].reshape(1, -1)], axis=1)
    outs = adamw(pack(""), [p_all], pack("m_"), pack("v_"), "adamw_small")
    done = [res["w_ada"][1], outs[1]]
    for group, gname in ((FFN_WEIGHTS, "ffn"), (MIX_WEIGHTS, "mix"), (("w_in",), "in")):
        ks = [(n, l) for l in reversed(range(DEPTH)) for n in group]
        after = sum(lax.slice(a, (0,) * a.ndim, (1,) * a.ndim).reshape(1, 1) for a in done)
        srcs, lands = copy_wait([sent[k] for k in ks], len(ALL_PEERS), after, "scatter_wait_" + gname)
        parts = {k: with_own(land, lax.dynamic_index_in_dim(src, me, 0, keepdims=False))
                 for k, src, land in zip(ks, srcs, lands)}
        for n in group:
            w = given[n]
            r2 = (w.shape[0] * w.shape[1], w.shape[2])
            res[n] = [o.reshape(w.shape) for o in
                      adamw(w.reshape(r2), [parts[(n, l)] for l in range(DEPTH)], given["m_" + n].reshape(r2),
                            given["v_" + n].reshape(r2), "adamw_" + n)]
            done.append(res[n][1])
    off = 0
    for n in packed:
        if n == "conv_w":
            off = small_parts.shape[2]
        size = math.prod(given[n].shape)
        res[n] = [o[0, off:off + size].reshape(given[n].shape) for o in outs]
        off += size

    return (loss, dx[None]) + tuple(res[n][i] for i in range(4) for n in WEIGHTS)
```

```python
import functools
import math

import jax
import jax.numpy as jnp
from jax import lax
from jax.experimental import pallas as pl
from jax.experimental.pallas import tpu as pltpu

F32 = jnp.float32
BF16 = jnp.bfloat16

MLA_HEADS = 8
QK_NOPE = 128
QK_ROPE = 64
V_HEAD = 128
Q_LORA = 512
KV_LORA = 512
ROPE_THETA = 10000.0
GDN_HEADS = 8
GDN_DK = 128
GDN_DV = 128
CONV_WIDTH = 4
CHUNK = 64
DEPTH = 2
EPS = 1e-6
ADAM_LR = 0.001
ADAM_B1 = 0.9
ADAM_B2 = 0.999
ADAM_EPS = 1e-08
ADAM_WD = 0.01
ADAM_STEP = 10

N_DEV = 8
AXES = ("x", "y", "c")
LANE = 128
VMEM_LIMIT = 48 * 1024 * 1024
MM_VMEM_BUDGET = 36 * 1024 * 1024

NN = (((1,), (0,)), ((), ()))
NT = (((1,), (1,)), ((), ()))
TN = (((0,), (0,)), ((), ()))


def _cp(sem=None):
    return pltpu.CompilerParams(dimension_semantics=sem, vmem_limit_bytes=VMEM_LIMIT)


def _tile(n, cap):
    if n <= cap:
        return n
    for t in range(cap - cap % LANE, 0, -LANE):
        if n % t == 0:
            return t
    return n


def _rows(t, cap=256):
    return cap if t % cap == 0 else t


def _pad_lanes(n):
    return -(-n // LANE) * LANE


def _sigmoid(x):
    return 1.0 / (1.0 + jnp.exp(-x))


def _softplus(x):
    return jnp.maximum(x, 0.0) + jnp.log(1.0 + jnp.exp(-jnp.abs(x)))


def _tile_slot(n, cap):
    t = _tile(n, cap)
    return n if t < 256 < n <= 1536 else t


def _mm(a, b, dims, out_dtype, name, a_act=None, slots=False):
    if dims == "nn":
        m, k = a.shape
        n = b.shape[-1] * (N_DEV if slots else 1)
    elif dims == "nt":
        m, k = a.shape
        n = b.shape[-2]
    else:
        k, m = a.shape
        n = b.shape[-1]
    tm = _tile(m, 1536)
    tn = _tile_slot(n // N_DEV, 512) if slots and dims != "nt" else _tile(n, 1024)
    k_slot = k // N_DEV if slots and dims == "nt" else k

    def vmem_bytes(tk_):
        a_b, b_b = tm * tk_ * a.dtype.itemsize, tk_ * tn * b.dtype.itemsize
        casts = (tm * tk_ * 2 if a.dtype != BF16 else 0) + (tk_ * tn * 2 if b.dtype != BF16 else 0)
        return 2 * (a_b + b_b + tm * tn * jnp.dtype(out_dtype).itemsize) + 2 * tm * tn * 4 + casts

    tk = _tile_slot(k_slot, 1536) if slots and dims == "nt" else _tile(k, 2048)
    while vmem_bytes(tk) > MM_VMEM_BUDGET and tk % (2 * LANE) == 0:
        tk //= 2
    nk = k // tk
    per_n = (n // N_DEV) // tn if slots else 1
    per_k = (k // N_DEV) // tk if slots else 1
    if dims == "tn":
        a_spec = pl.BlockSpec((tk, tm), lambda i, j, kk: (kk, i))
    else:
        a_spec = pl.BlockSpec((tm, tk), lambda i, j, kk: (i, kk))
    if dims == "nt":
        if slots:
            b_spec = pl.BlockSpec((None, tn, tk), lambda i, j, kk: (kk // per_k, j, kk % per_k))
        else:
            b_spec = pl.BlockSpec((tn, tk), lambda i, j, kk: (j, kk))
    elif dims == "nn" and slots:
        b_spec = pl.BlockSpec((None, tk, tn), lambda i, j, kk: (j // per_n, kk, j % per_n))
    else:
        b_spec = pl.BlockSpec((tk, tn), lambda i, j, kk: (kk, j))
    if dims == "tn" and slots:
        out_spec = pl.BlockSpec((None, tm, tn), lambda i, j, kk: (j // per_n, i, j % per_n))
        out_shape = jax.ShapeDtypeStruct((N_DEV, m, n // N_DEV), out_dtype)
    else:
        out_spec = pl.BlockSpec((tm, tn), lambda i, j, kk: (i, j))
        out_shape = jax.ShapeDtypeStruct((m, n), out_dtype)
    dn = {"nn": NN, "nt": NT, "tn": TN}[dims]

    def product(a_ref, b_ref):
        av = a_ref[...]
        if a_act == "silu":
            av = av * _sigmoid(av)
        return lax.dot_general(av.astype(BF16), b_ref[...].astype(BF16), dn, preferred_element_type=F32)

    def body_one(a_ref, b_ref, o_ref):
        o_ref[...] = product(a_ref, b_ref).astype(o_ref.dtype)

    def body_acc(a_ref, b_ref, o_ref, acc_ref):
        kk = pl.program_id(2)

        @pl.when(kk == 0)
        def _():
            acc_ref[...] = jnp.zeros_like(acc_ref)

        acc_ref[...] += product(a_ref, b_ref)

        @pl.when(kk == nk - 1)
        def _():
            o_ref[...] = acc_ref[...].astype(o_ref.dtype)

    return pl.pallas_call(
        body_one if nk == 1 else body_acc, name=name, grid=(m // tm, n // tn, nk),
        in_specs=[a_spec, b_spec], out_specs=out_spec, out_shape=out_shape,
        scratch_shapes=[] if nk == 1 else [pltpu.VMEM((tm, tn), F32)],
        compiler_params=_cp(("parallel", "parallel", "arbitrary")),
    )(a, b)


@functools.partial(jax.custom_vjp, nondiff_argnums=(2, 3))
def mm(a, b, tag, out_dtype):
    return _mm(a, b, "nn", out_dtype, "mm_" + tag, slots=b.ndim == 3)


def _mm_f(a, b, tag, out_dtype):
    return mm(a, b, tag, out_dtype), (a, b)


def _mm_b(tag, out_dtype, res, g):
    a, b = res
    slots = b.ndim == 3
    da = _mm(g, b, "nt", a.dtype, "mm_" + tag + "_da", slots=slots)
    db = _mm(a, g, "tn", b.dtype, "mm_" + tag + "_db", slots=slots)
    return da, db


mm.defvjp(_mm_f, _mm_b)


def _norm_fwd_call(x, nw, sc, sh, name):
    t, d = x.shape
    tr = _rows(t)
    mod = sc is not None
    row = pl.BlockSpec((tr, d), lambda i: (i, 0))
    vec = pl.BlockSpec((1, d), lambda i: (0, 0))

    def body(*refs):
        if mod:
            x_ref, nw_ref, sc_ref, sh_ref, o_ref = refs
        else:
            x_ref, nw_ref, o_ref = refs
        xv = x_ref[...].astype(F32)
        r = lax.rsqrt(jnp.mean(xv * xv, axis=-1, keepdims=True) + EPS)
        y = (xv * r) * nw_ref[...]
        if mod:
            y = y * (1.0 + sc_ref[...]) + sh_ref[...]
        o_ref[...] = y.astype(o_ref.dtype)

    args = (x, nw, sc, sh) if mod else (x, nw)
    return pl.pallas_call(
        body, name=name, grid=(t // tr,),
        in_specs=[row] + [vec] * (len(args) - 1), out_specs=row,
        out_shape=jax.ShapeDtypeStruct((t, d), BF16),
        compiler_params=_cp(("parallel",)),
    )(*args)


def _norm_bwd_call(x, nw, sc, dh, name):
    t, d = x.shape
    tr = _rows(t)
    mod = sc is not None
    row = pl.BlockSpec((tr, d), lambda i: (i, 0))
    vec = pl.BlockSpec((1, d), lambda i: (0, 0))

    def body(*refs):
        if mod:
            x_ref, nw_ref, sc_ref, dh_ref, dx_ref, dnw_ref, dsc_ref, dsh_ref = refs
        else:
            x_ref, nw_ref, dh_ref, dx_ref, dnw_ref = refs
        i = pl.program_id(0)
        xv = x_ref[...].astype(F32)
        dh = dh_ref[...].astype(F32)
        r = lax.rsqrt(jnp.mean(xv * xv, axis=-1, keepdims=True) + EPS)
        y = xv * r
        a = nw_ref[...] * (1.0 + sc_ref[...]) if mod else nw_ref[...]
        dy = dh * a
        dx_ref[...] = (r * (dy - y * jnp.mean(dy * y, axis=-1, keepdims=True))).astype(dx_ref.dtype)
        da = jnp.sum(dh * y, axis=0, keepdims=True)

        @pl.when(i == 0)
        def _():
            dnw_ref[...] = jnp.zeros_like(dnw_ref)
            if mod:
                dsc_ref[...] = jnp.zeros_like(dsc_ref)
                dsh_ref[...] = jnp.zeros_like(dsh_ref)

        if mod:
            dnw_ref[...] += da * (1.0 + sc_ref[...])
            dsc_ref[...] += da * nw_ref[...]
            dsh_ref[...] += jnp.sum(dh, axis=0, keepdims=True)
        else:
            dnw_ref[...] += da

    args = (x, nw, sc, dh) if mod else (x, nw, dh)
    n_vec = 3 if mod else 1
    return pl.pallas_call(
        body, name=name, grid=(t // tr,),
        in_specs=[row] + [vec] * (len(args) - 2) + [row],
        out_specs=[row] + [vec] * n_vec,
        out_shape=[jax.ShapeDtypeStruct((t, d), x.dtype)] + [jax.ShapeDtypeStruct((1, d), F32)] * n_vec,
        compiler_params=_cp(("arbitrary",)),
    )(*args)


@functools.partial(jax.custom_vjp, nondiff_argnums=(4,))
def ada_norm(x, nw, sc, sh, tag):
    return _norm_fwd_call(x, nw, sc, sh, "adanorm_" + tag)


def _ada_norm_f(x, nw, sc, sh, tag):
    return _norm_fwd_call(x, nw, sc, sh, "adanorm_" + tag), (x, nw, sc)


def _ada_norm_b(tag, res, dh):
    x, nw, sc = res
    dx, dnw, dsc, dsh = _norm_bwd_call(x, nw, sc, dh, "adanorm_" + tag + "_bwd")
    return dx, dnw, dsc, dsh


ada_norm.defvjp(_ada_norm_f, _ada_norm_b)


@functools.partial(jax.custom_vjp, nondiff_argnums=(2,))
def rms_norm(x, nw, tag):
    return _norm_fwd_call(x, nw, None, None, "rms_" + tag)


def _rms_norm_f(x, nw, tag):
    return _norm_fwd_call(x, nw, None, None, "rms_" + tag), (x, nw)


def _rms_norm_b(tag, res, dh):
    x, nw = res
    dx, dnw = _norm_bwd_call(x, nw, None, dh, "rms_" + tag + "_bwd")
    return dx, dnw


rms_norm.defvjp(_rms_norm_f, _rms_norm_b)


def _gate_mix_fwd_call(gl, ya, yb, name):
    t, d = ya.shape
    tr = _rows(t)
    row = pl.BlockSpec((tr, d), lambda i: (i, 0))

    def body(ga_ref, gb_ref, ya_ref, yb_ref, o_ref):
        o_ref[...] = (_sigmoid(ga_ref[...].astype(F32)) * ya_ref[...].astype(F32)
                      + _sigmoid(gb_ref[...].astype(F32)) * yb_ref[...].astype(F32)).astype(o_ref.dtype)

    return pl.pallas_call(
        body, name=name, grid=(t // tr,),
        in_specs=[row, pl.BlockSpec((tr, d), lambda i: (i, 1)), row, row], out_specs=row,
        out_shape=jax.ShapeDtypeStruct((t, d), BF16),
        compiler_params=_cp(("parallel",)),
    )(gl, gl, ya, yb)


def _gate_mix_bwd_call(gl, ya, yb, dm, name):
    t, d = ya.shape
    tr = _rows(t)
    row = pl.BlockSpec((tr, d), lambda i: (i, 0))
    wide = pl.BlockSpec((tr, 2 * d), lambda i: (i, 0))

    def body(gl_ref, ya_ref, yb_ref, dm_ref, dgl_ref, dya_ref, dyb_ref):
        dm = dm_ref[...].astype(F32)
        ga = _sigmoid(gl_ref[:, :d].astype(F32))
        gb = _sigmoid(gl_ref[:, d:].astype(F32))
        dya_ref[...] = (dm * ga).astype(dya_ref.dtype)
        dyb_ref[...] = (dm * gb).astype(dyb_ref.dtype)
        dgl_ref[:, :d] = (dm * ya_ref[...].astype(F32) * ga * (1.0 - ga)).astype(dgl_ref.dtype)
        dgl_ref[:, d:] = (dm * yb_ref[...].astype(F32) * gb * (1.0 - gb)).astype(dgl_ref.dtype)

    return pl.pallas_call(
        body, name=name, grid=(t // tr,),
        in_specs=[wide, row, row, row], out_specs=[wide, row, row],
        out_shape=[jax.ShapeDtypeStruct((t, 2 * d), gl.dtype), jax.ShapeDtypeStruct((t, d), ya.dtype),
                   jax.ShapeDtypeStruct((t, d), yb.dtype)],
        compiler_params=_cp(("parallel",)),
    )(gl, ya, yb, dm)


@functools.partial(jax.custom_vjp, nondiff_argnums=(3,))
def gate_mix(gl, ya, yb, tag):
    return _gate_mix_fwd_call(gl, ya, yb, "gatemix_" + tag)


def _gate_mix_f(gl, ya, yb, tag):
    return _gate_mix_fwd_call(gl, ya, yb, "gatemix_" + tag), (gl, ya, yb)


def _gate_mix_b(tag, res, dm):
    return tuple(_gate_mix_bwd_call(*res, dm, "gatemix_" + tag + "_bwd"))


gate_mix.defvjp(_gate_mix_f, _gate_mix_b)


def _resid_fwd_call(x, gt, m, name):
    t, d = x.shape
    tr = _rows(t)
    row = pl.BlockSpec((tr, d), lambda i: (i, 0))
    vec = pl.BlockSpec((1, d), lambda i: (0, 0))

    def body(x_ref, gt_ref, m_ref, o_ref):
        o_ref[...] = x_ref[...] + gt_ref[...] * m_ref[...]

    return pl.pallas_call(
        body, name=name, grid=(t // tr,), in_specs=[row, vec, row], out_specs=row,
        out_shape=jax.ShapeDtypeStruct((t, d), F32), compiler_params=_cp(("parallel",)),
    )(x, gt, m)


def _resid_bwd_call(gt, m, g, name):
    t, d = m.shape
    tr = _rows(t)
    row = pl.BlockSpec((tr, d), lambda i: (i, 0))
    vec = pl.BlockSpec((1, d), lambda i: (0, 0))

    def body(gt_ref, m_ref, g_ref, dm_ref, dgt_ref):
        i = pl.program_id(0)
        g = g_ref[...]
        dm_ref[...] = g * gt_ref[...]

        @pl.when(i == 0)
        def _():
            dgt_ref[...] = jnp.zeros_like(dgt_ref)

        dgt_ref[...] += jnp.sum(g * m_ref[...], axis=0, keepdims=True)

    return pl.pallas_call(
        body, name=name, grid=(t // tr,), in_specs=[vec, row, row], out_specs=[row, vec],
        out_shape=[jax.ShapeDtypeStruct((t, d), F32), jax.ShapeDtypeStruct((1, d), F32)],
        compiler_params=_cp(("arbitrary",)),
    )(gt, m, g)


@functools.partial(jax.custom_vjp, nondiff_argnums=(3,))
def resid(x, gt, m, tag):
    return _resid_fwd_call(x, gt, m, "resid_" + tag)


def _resid_f(x, gt, m, tag):
    return _resid_fwd_call(x, gt, m, "resid_" + tag), (gt, m)


def _resid_b(tag, res, g):
    gt, m = res
    dm, dgt = _resid_bwd_call(gt, m, g, "resid_" + tag + "_bwd")
    return g, dgt, dm


resid.defvjp(_resid_f, _resid_b)


def _swiglu_fwd_call(gu, name):
    t, f2 = gu.shape
    f = f2 // 2
    tr = _rows(t, 128)
    half = pl.BlockSpec((tr, f), lambda i: (i, 0))

    def body(g_ref, u_ref, o_ref):
        g = g_ref[...].astype(F32)
        o_ref[...] = (g * _sigmoid(g) * u_ref[...].astype(F32)).astype(o_ref.dtype)

    return pl.pallas_call(
        body, name=name, grid=(t // tr,),
        in_specs=[half, pl.BlockSpec((tr, f), lambda i: (i, 1))], out_specs=half,
        out_shape=jax.ShapeDtypeStruct((t, f), BF16), compiler_params=_cp(("parallel",)),
    )(gu, gu)


def _swiglu_bwd_call(gu, da, name):
    t, f2 = gu.shape
    f = f2 // 2
    tr = _rows(t, 128)
    wide = pl.BlockSpec((tr, f2), lambda i: (i, 0))

    def body(gu_ref, da_ref, dgu_ref):
        g = gu_ref[:, :f].astype(F32)
        u = gu_ref[:, f:].astype(F32)
        da = da_ref[...].astype(F32)
        s = _sigmoid(g)
        dgu_ref[:, :f] = (da * u * s * (1.0 + g * (1.0 - s))).astype(dgu_ref.dtype)
        dgu_ref[:, f:] = (da * g * s).astype(dgu_ref.dtype)

    return pl.pallas_call(
        body, name=name, grid=(t // tr,),
        in_specs=[wide, pl.BlockSpec((tr, f), lambda i: (i, 0))], out_specs=wide,
        out_shape=jax.ShapeDtypeStruct((t, f2), gu.dtype), compiler_params=_cp(("parallel",)),
    )(gu, da)


@functools.partial(jax.custom_vjp, nondiff_argnums=(1,))
def swiglu(gu, tag):
    return _swiglu_fwd_call(gu, "swiglu_" + tag)


def _swiglu_f(gu, tag):
    return _swiglu_fwd_call(gu, "swiglu_" + tag), (gu,)


def _swiglu_b(tag, res, da):
    return (_swiglu_bwd_call(res[0], da, "swiglu_" + tag + "_bwd"),)


swiglu.defvjp(_swiglu_f, _swiglu_b)


def loss_head(x, fw, tgt):
    t, d = x.shape
    tr = _rows(t)
    row = pl.BlockSpec((tr, d), lambda i: (i, 0))
    vec = pl.BlockSpec((1, d), lambda i: (0, 0))
    tile = pl.BlockSpec((8, LANE), lambda i: (0, 0))

    def body(x_ref, fw_ref, tgt_ref, loss_ref, dx_ref, dfw_ref):
        i = pl.program_id(0)
        xv = x_ref[...]
        fw = fw_ref[...]
        r = lax.rsqrt(jnp.mean(xv * xv, axis=-1, keepdims=True) + EPS)
        yh = xv * r
        e = yh * fw - tgt_ref[...]
        dy = e * (1.0 / d)
        dyw = dy * fw
        dx_ref[...] = r * (dyw - yh * jnp.mean(dyw * yh, axis=-1, keepdims=True))

        @pl.when(i == 0)
        def _():
            loss_ref[...] = jnp.zeros_like(loss_ref)
            dfw_ref[...] = jnp.zeros_like(dfw_ref)

        loss_ref[...] += 0.5 * jnp.sum(jnp.mean(e * e, axis=-1, keepdims=True))
        dfw_ref[...] += jnp.sum(dy * yh, axis=0, keepdims=True)

    return pl.pallas_call(
        body, name="loss_head", grid=(t // tr,), in_specs=[row, vec, row],
        out_specs=[tile, row, vec],
        out_shape=[jax.ShapeDtypeStruct((8, LANE), F32), jax.ShapeDtypeStruct((t, d), F32),
                   jax.ShapeDtypeStruct((1, d), F32)],
        compiler_params=_cp(("arbitrary",)),
    )(x, fw, tgt)


def _attn_scores(qn_ref, qr_ref, kn_ref, kr_ref, diag):
    tq = qn_ref.shape[0]
    s = lax.dot_general(qn_ref[...].astype(BF16), kn_ref[...].astype(BF16), NT, preferred_element_type=F32)
    s += lax.dot_general(qr_ref[...].astype(BF16), kr_ref[...].astype(BF16), NT, preferred_element_type=F32)
    s = s * (QK_NOPE + QK_ROPE) ** -0.5
    if diag:
        rows = lax.broadcasted_iota(jnp.int32, (tq, tq), 0)
        cols = lax.broadcasted_iota(jnp.int32, (tq, tq), 1)
        s = jnp.where(cols <= rows, s, -1e30)
    return s


def _attn_fwd_call(qn, qr, kv, kr, name):
    t = qn.shape[0]
    h_n = MLA_HEADS
    tq = _rows(t, 512)
    nq = t // tq
    assert V_HEAD == LANE and tq % LANE == 0

    def body(qn_ref, qr_ref, kn_ref, v_ref, kr_ref, o_ref, lse_ref, m_scr, l_scr, acc_scr):
        i, j = pl.program_id(1), pl.program_id(2)

        @pl.when(j == 0)
        def _():
            m_scr[...] = jnp.full_like(m_scr, -1e30)
            l_scr[...] = jnp.zeros_like(l_scr)
            acc_scr[...] = jnp.zeros_like(acc_scr)

        def step(diag):
            s = _attn_scores(qn_ref, qr_ref, kn_ref, kr_ref, diag)
            m_old = m_scr[...]
            m_new = jnp.maximum(m_old, jnp.max(s, axis=-1, keepdims=True))
            p = jnp.exp(s - jnp.tile(m_new, (1, tq // LANE)))
            alpha = jnp.exp(m_old - m_new)
            l_scr[...] = alpha * l_scr[...] + jnp.sum(p, axis=-1, keepdims=True)
            acc_scr[...] = alpha * acc_scr[...] + jnp.dot(p.astype(BF16), v_ref[...].astype(BF16),
                                                           preferred_element_type=F32)
            m_scr[...] = m_new

        @pl.when(j < i)
        def _():
            step(False)

        @pl.when(j == i)
        def _():
            step(True)
            o_ref[...] = (acc_scr[...] / l_scr[...]).astype(o_ref.dtype)
            lse_ref[...] = (m_scr[...] + jnp.log(l_scr[...]))[:, :1]

    return pl.pallas_call(
        body, name=name, grid=(h_n, nq, nq),
        in_specs=[
            pl.BlockSpec((tq, QK_NOPE), lambda h, i, j: (i, h)),
            pl.BlockSpec((None, tq, QK_ROPE), lambda h, i, j: (h, i, 0)),
            pl.BlockSpec((tq, QK_NOPE), lambda h, i, j: (jnp.minimum(j, i), 2 * h)),
            pl.BlockSpec((tq, V_HEAD), lambda h, i, j: (jnp.minimum(j, i), 2 * h + 1)),
            pl.BlockSpec((tq, QK_ROPE), lambda h, i, j: (jnp.minimum(j, i), 0)),
        ],
        out_specs=[
            pl.BlockSpec((tq, V_HEAD), lambda h, i, j: (i, h)),
            pl.BlockSpec((None, tq, 1), lambda h, i, j: (h, i, 0)),
        ],
        out_shape=[jax.ShapeDtypeStruct((t, h_n * V_HEAD), BF16),
                   jax.ShapeDtypeStruct((h_n, t, 1), F32)],
        scratch_shapes=[pltpu.VMEM((tq, LANE), F32), pltpu.VMEM((tq, LANE), F32),
                        pltpu.VMEM((tq, V_HEAD), F32)],
        compiler_params=_cp(("parallel", "parallel", "arbitrary")),
    )(qn, qr, kv, kv, kr)


def _attn_bwd_call(qn, qr, kv, kr, o, lse, do, name):
    t = qn.shape[0]
    h_n = MLA_HEADS
    tq = _rows(t, 512)
    nq = t // tq
    scale = (QK_NOPE + QK_ROPE) ** -0.5

    def body(qn_ref, qr_ref, kn_ref, v_ref, kr_ref, o_ref, lse_ref, do_ref,
             dqn_ref, dqr_ref, dkv_ref, dkr_ref, dqn_scr, dqr_scr, dkn_scr, dv_scr, dkr_scr):
        j, i = pl.program_id(1), pl.program_id(2)

        @pl.when(jnp.logical_and(j == 0, i == 0))
        def _():
            dqn_scr[...] = jnp.zeros_like(dqn_scr)
            dqr_scr[...] = jnp.zeros_like(dqr_scr)

        @pl.when(i == 0)
        def _():
            dkn_scr[...] = jnp.zeros_like(dkn_scr)
            dv_scr[...] = jnp.zeros_like(dv_scr)
            dkr_scr[...] = jnp.zeros_like(dkr_scr)

        def step(diag):
            qn_b = qn_ref[...].astype(BF16)
            qr_b = qr_ref[...].astype(BF16)
            kn_b = kn_ref[...].astype(BF16)
            kr_b = kr_ref[...].astype(BF16)
            do_b = do_ref[...]
            p = jnp.exp(_attn_scores(qn_ref, qr_ref, kn_ref, kr_ref, diag) - lse_ref[...])
            delta = jnp.sum(do_b.astype(F32) * o_ref[...].astype(F32), axis=-1, keepdims=True)
            dp = lax.dot_general(do_b, v_ref[...].astype(BF16), NT, preferred_element_type=F32)
            ds = (p * (dp - delta) * scale).astype(BF16)
            p_b = p.astype(BF16)
            dv_scr[...] += lax.dot_general(p_b, do_b, TN, preferred_element_type=F32)
            dkn_scr[...] += lax.dot_general(ds, qn_b, TN, preferred_element_type=F32)
            dkr_scr[...] += lax.dot_general(ds, qr_b, TN, preferred_element_type=F32)
            sl = pl.ds(pl.multiple_of(i * tq, tq), tq)
            dqn_scr[sl, :] += jnp.dot(ds, kn_b, preferred_element_type=F32)
            dqr_scr[sl, :] += jnp.dot(ds, kr_b, preferred_element_type=F32)

        @pl.when(i > j)
        def _():
            step(False)

        @pl.when(i == j)
        def _():
            step(True)

        @pl.when(i == nq - 1)
        def _():
            dkv_ref[:, :QK_NOPE] = dkn_scr[...].astype(dkv_ref.dtype)
            dkv_ref[:, QK_NOPE:] = dv_scr[...].astype(dkv_ref.dtype)
            dkr_ref[...] = dkr_scr[...]

        @pl.when(jnp.logical_and(j == nq - 1, i == nq - 1))
        def _():
            dqn_ref[...] = dqn_scr[...].astype(dqn_ref.dtype)
            dqr_ref[...] = dqr_scr[...].astype(dqr_ref.dtype)

    qi = lambda j, i: jnp.maximum(i, j)
    return pl.pallas_call(
        body, name=name, grid=(h_n, nq, nq),
        in_specs=[
            pl.BlockSpec((tq, QK_NOPE), lambda h, j, i: (qi(j, i), h)),
            pl.BlockSpec((None, tq, QK_ROPE), lambda h, j, i: (h, qi(j, i), 0)),
            pl.BlockSpec((tq, QK_NOPE), lambda h, j, i: (j, 2 * h)),
            pl.BlockSpec((tq, V_HEAD), lambda h, j, i: (j, 2 * h + 1)),
            pl.BlockSpec((tq, QK_ROPE), lambda h, j, i: (j, 0)),
            pl.BlockSpec((tq, V_HEAD), lambda h, j, i: (qi(j, i), h)),
            pl.BlockSpec((None, tq, 1), lambda h, j, i: (h, qi(j, i), 0)),
            pl.BlockSpec((tq, V_HEAD), lambda h, j, i: (qi(j, i), h)),
        ],
        out_specs=[
            pl.BlockSpec((t, QK_NOPE), lambda h, j, i: (0, h)),
            pl.BlockSpec((None, t, QK_ROPE), lambda h, j, i: (h, 0, 0)),
            pl.BlockSpec((tq, QK_NOPE + V_HEAD), lambda h, j, i: (j, h)),
            pl.BlockSpec((None, tq, QK_ROPE), lambda h, j, i: (h, j, 0)),
        ],
        out_shape=[jax.ShapeDtypeStruct((t, h_n * QK_NOPE), qn.dtype),
                   jax.ShapeDtypeStruct((h_n, t, QK_ROPE), qr.dtype),
                   jax.ShapeDtypeStruct((t, h_n * (QK_NOPE + V_HEAD)), kv.dtype),
                   jax.ShapeDtypeStruct((h_n, t, QK_ROPE), F32)],
        scratch_shapes=[pltpu.VMEM((t, QK_NOPE), F32), pltpu.VMEM((t, QK_ROPE), F32),
                        pltpu.VMEM((tq, QK_NOPE), F32), pltpu.VMEM((tq, V_HEAD), F32),
                        pltpu.VMEM((tq, QK_ROPE), F32)],
        compiler_params=_cp(("parallel", "arbitrary", "arbitrary")),
    )(qn, qr, kv, kv, kr, o, lse, do)


@functools.partial(jax.custom_vjp, nondiff_argnums=(4,))
def attention(qn, qr, kv, kr, tag):
    return _attn_fwd_call(qn, qr, kv, kr, "attn_" + tag)[0]


def _attention_f(qn, qr, kv, kr, tag):
    o, lse = _attn_fwd_call(qn, qr, kv, kr, "attn_" + tag)
    return o, (qn, qr, kv, kr, o, lse)


def _attention_b(tag, res, do):
    dqn, dqr, dkv, dkr_h = _attn_bwd_call(*res, do, "attn_" + tag + "_bwd")
    return dqn, dqr, dkv, jnp.sum(dkr_h, axis=0).astype(res[3].dtype)


attention.defvjp(_attention_f, _attention_b)


def _shift_down(u, s):
    if s == 0:
        return u
    t = u.shape[0]
    rolled = pltpu.roll(u, s, 0)
    return jnp.where(lax.broadcasted_iota(jnp.int32, u.shape, 0) >= s, rolled, 0.0)


def _shift_up(u, s):
    if s == 0:
        return u
    t = u.shape[0]
    rolled = pltpu.roll(u, t - s, 0)
    return jnp.where(lax.broadcasted_iota(jnp.int32, u.shape, 0) < t - s, rolled, 0.0)


def _conv_blocks(t, c3):
    p = c3 // 3
    tc = _tile(p, 512)
    per = p // tc
    return p, tc, per


def _conv_fwd_call(u, w, name):
    t, c3 = u.shape
    p, tc, per = _conv_blocks(t, c3)

    def body(u_ref, w_ref, o_ref):
        u = u_ref[...].astype(F32)
        y = jnp.zeros_like(u)
        for j in range(CONV_WIDTH):
            y = y + w_ref[j:j + 1, :] * _shift_down(u, CONV_WIDTH - 1 - j)
        o_ref[...] = y * _sigmoid(y)

    return pl.pallas_call(
        body, name=name, grid=(c3 // tc,),
        in_specs=[pl.BlockSpec((t, tc), lambda cb: (0, cb)),
                  pl.BlockSpec((CONV_WIDTH, tc), lambda cb: (0, cb))],
        out_specs=pl.BlockSpec((None, t, tc), lambda cb: (cb // per, 0, cb % per)),
        out_shape=jax.ShapeDtypeStruct((3, t, p), F32),
        compiler_params=_cp(("parallel",)),
    )(u, w)


def _conv_bwd_call(u, w, do, name):
    t, c3 = u.shape
    p, tc, per = _conv_blocks(t, c3)

    def body(u_ref, w_ref, do_ref, du_ref, dw_ref):
        u = u_ref[...].astype(F32)
        shifted = [_shift_down(u, CONV_WIDTH - 1 - j) for j in range(CONV_WIDTH)]
        y = jnp.zeros_like(u)
        for j in range(CONV_WIDTH):
            y = y + w_ref[j:j + 1, :] * shifted[j]
        s = _sigmoid(y)
        dy = do_ref[...] * s * (1.0 + y * (1.0 - s))
        du = jnp.zeros_like(u)
        for j in range(CONV_WIDTH):
            du = du + w_ref[j:j + 1, :] * _shift_up(dy, CONV_WIDTH - 1 - j)
            dw_ref[j:j + 1, :] = jnp.sum(dy * shifted[j], axis=0, keepdims=True)
        du_ref[...] = du.astype(du_ref.dtype)

    return pl.pallas_call(
        body, name=name, grid=(c3 // tc,),
        in_specs=[pl.BlockSpec((t, tc), lambda cb: (0, cb)),
                  pl.BlockSpec((CONV_WIDTH, tc), lambda cb: (0, cb)),
                  pl.BlockSpec((None, t, tc), lambda cb: (cb // per, 0, cb % per))],
        out_specs=[pl.BlockSpec((t, tc), lambda cb: (0, cb)),
                   pl.BlockSpec((CONV_WIDTH, tc), lambda cb: (0, cb))],
        out_shape=[jax.ShapeDtypeStruct((t, c3), u.dtype), jax.ShapeDtypeStruct((CONV_WIDTH, c3), F32)],
        compiler_params=_cp(("parallel",)),
    )(u, w, do)


@functools.partial(jax.custom_vjp, nondiff_argnums=(2,))
def conv_silu(u, w, tag):
    return _conv_fwd_call(u, w, "conv_" + tag)


def _conv_silu_f(u, w, tag):
    return _conv_fwd_call(u, w, "conv_" + tag), (u, w)


def _conv_silu_b(tag, res, do):
    return tuple(_conv_bwd_call(*res, do, "conv_" + tag + "_bwd"))


conv_silu.defvjp(_conv_silu_f, _conv_silu_b)


BNN = (((2,), (1,)), ((0,), (0,)))
BNT = (((2,), (2,)), ((0,), (0,)))
BTN = (((1,), (1,)), ((0,), (0,)))


def _xdot(a, b, dn=BNN):
    return lax.dot_general(a, b, dn, precision=lax.Precision.HIGHEST, preferred_element_type=F32)


def _bf16_dot(a, b, dn):
    return lax.dot_general(a.astype(BF16), b.astype(BF16), dn, preferred_element_type=F32)


def _dot3(a, b, dn):
    ah, bh = a.astype(BF16), b.astype(BF16)
    al, bl = a - ah.astype(F32), b - bh.astype(F32)
    return _bf16_dot(ah, bh, dn) + (_bf16_dot(ah, bl, dn) + _bf16_dot(al, bh, dn))


def _transposed(dn, a, b, g):
    if dn == BNN:
        return (g, b, BNT), (a, g, BTN)
    if dn == BNT:
        return (g, b, BNN), (g, a, BTN)
    return (b, g, BNT), (a, g, BNN)


@functools.partial(jax.custom_vjp, nondiff_argnums=(2,))
def _hdot(a, b, dn=BNN):
    return _dot3(a, b, dn)


def _hdot_f(a, b, dn):
    return _dot3(a, b, dn), (a, b)


def _hdot_b(dn, res, g):
    da, db = _transposed(dn, *res, g)
    return _dot3(*da), _dot3(*db)


_hdot.defvjp(_hdot_f, _hdot_b)


@functools.partial(jax.custom_vjp, nondiff_argnums=(2,))
def _bdot(a, b, dn=BNN):
    return _bf16_dot(a, b, dn)


def _bdot_f(a, b, dn):
    return _bf16_dot(a, b, dn), (a, b)


def _bdot_b(dn, res, g):
    da, db = _transposed(dn, *res, g)
    return _bf16_dot(*da), _bf16_dot(*db)


_bdot.defvjp(_bdot_f, _bdot_b)


GDN_HEADS_PER_STEP = 8
GDN_HEADS_PER_STEP_BWD = 4


def _gdn_chunk(q, k, v, z, bl, al, a_log, dtb, gn, s):
    b, c = q.shape[0], q.shape[1]
    ri = lax.broadcasted_iota(jnp.int32, (c, c), 0)
    ci = lax.broadcasted_iota(jnp.int32, (c, c), 1)
    lower = (ri >= ci)[None]
    strict = (ri > ci)[None]
    low_incl = jnp.broadcast_to((ri >= ci).astype(F32), (b, c, c))
    up_incl = jnp.broadcast_to((ri <= ci).astype(F32), (b, c, c))
    eye = (ri == ci).astype(F32)[None]

    q = q * lax.rsqrt(jnp.sum(q * q, axis=-1, keepdims=True) + EPS) * (GDN_DK ** -0.5)
    k = k * lax.rsqrt(jnp.sum(k * k, axis=-1, keepdims=True) + EPS)
    beta = _sigmoid(bl)
    g = -jnp.exp(a_log) * _softplus(al + dtb)
    g_w = jnp.broadcast_to(g, (b, c, LANE))
    gc = _xdot(low_incl, g_w)
    gr = _xdot(g_w[:, :, :c], up_incl, BTN)
    diff = gc[:, :, :c] - gr
    decay = jnp.where(lower, jnp.exp(jnp.where(lower, diff, 0.0)), 0.0)
    kb = k * beta
    lmat = jnp.where(strict, _bdot(kb, k, BNT) * decay, 0.0)
    inv = eye - lmat
    pw = lmat
    for _ in range(int(math.log2(c)) - 1):
        pw = _hdot(pw, pw)
        inv = _hdot(inv, eye + pw)
    eg = jnp.exp(gc)
    u = _hdot(inv, v * beta)
    w = _hdot(inv, kb * eg)
    attn = jnp.where(lower, _bdot(q, k, BNT) * decay, 0.0)
    v_new = u - _bdot(w, s)
    o = _bdot(q * eg, s) + _bdot(attn, v_new)
    g_last = jnp.sum(g_w, axis=1, keepdims=True)
    k_dec = k * jnp.exp(g_last - gc)
    s_new = s * jnp.exp(g_last) + _bdot(k_dec, v_new, BTN)
    on = o * lax.rsqrt(jnp.mean(o * o, axis=-1, keepdims=True) + EPS) * gn
    return on * (z * _sigmoid(z)), s_new


def _gdn_specs(n_chunks, hb, rev):
    c = CHUNK
    nn = (lambda n: n_chunks - 1 - n) if rev else (lambda n: n)
    plane = lambda pidx: pl.BlockSpec((None, c, hb * GDN_DK), lambda hg, n: (pidx, nn(n), hg))
    col = pl.BlockSpec((hb, c, 1), lambda hg, n: (hg, nn(n), 0))
    scal = pl.BlockSpec((hb, 1, 1), lambda hg, n: (hg, 0, 0))
    zspec = pl.BlockSpec((c, hb * GDN_DV), lambda hg, n: (nn(n), hg))
    gnspec = pl.BlockSpec((1, GDN_DV), lambda hg, n: (0, 0))
    sspec = pl.BlockSpec((hb, None, GDN_DK, GDN_DV), lambda hg, n: (hg, nn(n), 0, 0))
    return plane, col, scal, zspec, gnspec, sspec


def _heads_per_step(want):
    return math.gcd(want, GDN_HEADS)


def _heads(ref, hb):
    return jnp.stack([ref[:, j * GDN_DK:(j + 1) * GDN_DK] for j in range(hb)])


def _gdn_fwd_call(qkv, z, bl, al, a_log, dtb, gn, name):
    t = z.shape[0]
    h_n = GDN_HEADS
    hb = _heads_per_step(GDN_HEADS_PER_STEP)
    n_chunks = t // CHUNK
    plane, col, scal, zspec, gnspec, sspec = _gdn_specs(n_chunks, hb, False)

    def body(q_ref, k_ref, v_ref, z_ref, bl_ref, al_ref, a_ref, dtb_ref, gn_ref, o_ref, sall_ref, s_scr):
        n = pl.program_id(1)

        @pl.when(n == 0)
        def _():
            s_scr[...] = jnp.zeros_like(s_scr)

        s = s_scr[...]
        sall_ref[...] = s
        o, s_new = _gdn_chunk(_heads(q_ref, hb), _heads(k_ref, hb), _heads(v_ref, hb),
                              _heads(z_ref, hb).astype(F32),
                              bl_ref[...], al_ref[...], a_ref[...], dtb_ref[...], gn_ref[...], s)
        for j in range(hb):
            o_ref[:, j * GDN_DV:(j + 1) * GDN_DV] = o[j].astype(o_ref.dtype)
        s_scr[...] = s_new

    return pl.pallas_call(
        body, name=name, grid=(h_n // hb, n_chunks),
        in_specs=[plane(0), plane(1), plane(2), zspec, col, col, scal, scal, gnspec],
        out_specs=[zspec, sspec],
        out_shape=[jax.ShapeDtypeStruct((t, h_n * GDN_DV), BF16),
                   jax.ShapeDtypeStruct((h_n, n_chunks, GDN_DK, GDN_DV), F32)],
        scratch_shapes=[pltpu.VMEM((hb, GDN_DK, GDN_DV), F32)],
        compiler_params=_cp(("parallel", "arbitrary")),
    )(qkv, qkv, qkv, z, bl, al, a_log, dtb, gn)


def _gdn_bwd_call(qkv, z, bl, al, a_log, dtb, gn, sall, do, name):
    t = z.shape[0]
    h_n = GDN_HEADS
    hb = _heads_per_step(GDN_HEADS_PER_STEP_BWD)
    n_chunks = t // CHUNK
    c = CHUNK
    plane, col, scal, zspec, gnspec, sspec = _gdn_specs(n_chunks, hb, True)
    dplanes = pl.BlockSpec((3, c, hb * GDN_DK), lambda hg, n: (0, n_chunks - 1 - n, hg))
    gnh = pl.BlockSpec((None, 1, GDN_DV), lambda hg, n: (hg, 0, 0))

    def body(q_ref, k_ref, v_ref, z_ref, bl_ref, al_ref, a_ref, dtb_ref, gn_ref, s_ref, do_ref,
             dqkv_ref, dz_ref, dbl_ref, dal_ref, da_ref, ddtb_ref, dgn_ref, ds_scr):
        n = pl.program_id(1)

        @pl.when(n == 0)
        def _():
            ds_scr[...] = jnp.zeros_like(ds_scr)
            da_ref[...] = jnp.zeros_like(da_ref)
            ddtb_ref[...] = jnp.zeros_like(ddtb_ref)
            dgn_ref[...] = jnp.zeros_like(dgn_ref)

        _, vjp = jax.vjp(_gdn_chunk, _heads(q_ref, hb), _heads(k_ref, hb), _heads(v_ref, hb),
                              _heads(z_ref, hb).astype(F32),
                         bl_ref[...], al_ref[...], a_ref[...], dtb_ref[...], gn_ref[...], s_ref[...])
        dq, dk, dv, dz, dbl, dal, da, ddtb, dgn, ds = vjp((_heads(do_ref, hb).astype(F32), ds_scr[...]))
        for j in range(hb):
            hs = slice(j * GDN_DK, (j + 1) * GDN_DK)
            dqkv_ref[0, :, hs] = dq[j]
            dqkv_ref[1, :, hs] = dk[j]
            dqkv_ref[2, :, hs] = dv[j]
            dz_ref[:, hs] = dz[j].astype(dz_ref.dtype)
        dbl_ref[...] = dbl
        dal_ref[...] = dal
        da_ref[...] += da
        ddtb_ref[...] += ddtb
        dgn_ref[...] += dgn
        ds_scr[...] = ds

    return pl.pallas_call(
        body, name=name, grid=(h_n // hb, n_chunks),
        in_specs=[plane(0), plane(1), plane(2), zspec, col, col, scal, scal, gnspec, sspec, zspec],
        out_specs=[dplanes, zspec, col, col, scal, scal, gnh],
        out_shape=[jax.ShapeDtypeStruct((3, t, h_n * GDN_DK), F32),
                   jax.ShapeDtypeStruct((t, h_n * GDN_DV), z.dtype),
                   jax.ShapeDtypeStruct((h_n, t, 1), F32), jax.ShapeDtypeStruct((h_n, t, 1), F32),
                   jax.ShapeDtypeStruct((h_n, 1, 1), F32), jax.ShapeDtypeStruct((h_n, 1, 1), F32),
                   jax.ShapeDtypeStruct((h_n // hb, 1, GDN_DV), F32)],
        scratch_shapes=[pltpu.VMEM((hb, GDN_DK, GDN_DV), F32)],
        compiler_params=_cp(("parallel", "arbitrary")),
    )(qkv, qkv, qkv, z, bl, al, a_log, dtb, gn, sall, do)


@functools.partial(jax.custom_vjp, nondiff_argnums=(7,))
def gdn(qkv, z, bl, al, a_log, dtb, gn, tag):
    return _gdn_fwd_call(qkv, z, bl, al, a_log, dtb, gn, "gdn_" + tag)[0]


def _gdn_f(qkv, z, bl, al, a_log, dtb, gn, tag):
    o, sall = _gdn_fwd_call(qkv, z, bl, al, a_log, dtb, gn, "gdn_" + tag)
    return o, (qkv, z, bl, al, a_log, dtb, gn, sall)


def _gdn_b(tag, res, do):
    dqkv, dz, dbl, dal, da, ddtb, dgn_h = _gdn_bwd_call(*res, do, "gdn_" + tag + "_bwd")
    return dqkv, dz, dbl, dal, da, ddtb, jnp.sum(dgn_h, axis=0)


gdn.defvjp(_gdn_f, _gdn_b)


def adamw(w, parts, m, v, name):
    n_layers = len(parts)
    n_parts, r, c = parts[0].shape
    assert w.shape == (n_layers * r, c), (w.shape, parts[0].shape)
    tr = r
    for cand in (512, 256, 128, 64, 32, 16, 8):
        if r % cand == 0 and cand * c <= 256 * 1024:
            tr = cand
            break
    nb = r // tr
    blk = pl.BlockSpec((tr, c), lambda l, i: (l * nb + i, 0))
    bc1 = 1.0 - ADAM_B1 ** ADAM_STEP
    bc2 = 1.0 - ADAM_B2 ** ADAM_STEP

    def part_spec(li):
        return pl.BlockSpec((n_parts, tr, c),
                            lambda l, i: (0, jnp.where(l == li, i, jnp.where(l < li, 0, nb - 1)), 0))

    def body(*refs):
        w_ref, p_refs = refs[0], refs[1:1 + n_layers]
        m_ref, v_ref, g_ref, d_ref, mo_ref, vo_ref = refs[1 + n_layers:]
        for li in range(n_layers):
            @pl.when(pl.program_id(0) == li)
            def _(p_ref=p_refs[li]):
                g = p_ref[0].astype(F32)
                for i in range(1, n_parts):
                    g = g + p_ref[i].astype(F32)
                m2 = ADAM_B1 * m_ref[...] + (1.0 - ADAM_B1) * g
                v2 = ADAM_B2 * v_ref[...] + (1.0 - ADAM_B2) * (g * g)
                g_ref[...] = g
                mo_ref[...] = m2
                vo_ref[...] = v2
                d_ref[...] = -ADAM_LR * ((m2 / bc1) / (jnp.sqrt(v2 / bc2) + ADAM_EPS)
                                         + ADAM_WD * w_ref[...])

    return pl.pallas_call(
        body, name=name, grid=(n_layers, nb),
        in_specs=[blk] + [part_spec(li) for li in range(n_layers)] + [blk, blk],
        out_specs=[blk] * 4, out_shape=[jax.ShapeDtypeStruct(w.shape, F32)] * 4,
        compiler_params=_cp(("arbitrary", "arbitrary")),
    )(w, *parts, m, v)


_HBM = pl.BlockSpec(memory_space=pltpu.HBM)
_SEM = pl.BlockSpec(memory_space=pltpu.SEMAPHORE)
_EFFECT = pltpu.SideEffectType.DATAFLOW_SIDE_EFFECTING


def _peer(x, y, c, d):
    px = 1 - x if d & 4 else x
    py = 1 - y if d & 2 else y
    pc = 1 - c if d & 1 else c
    return (px, py, pc), 4 * px + 2 * py + pc


ALL_PEERS = (1, 2, 3, 4, 5, 6, 7)
SIBLING = 1
SAME_CORE_REMOTE = (2, 4, 6)


def copy_start(arrays, mode, carry, name):
    n = len(arrays)
    if mode == "forward":
        lands = []
    else:
        lands = [lax.empty(a.shape if mode == "scatter" else (N_DEV,) + a.shape, a.dtype) for a in arrays]
    n_in = n + len(lands) + 1

    def body(*refs):
        srcs = refs[:n]
        dsts = refs[n:2 * n] if lands else srcs
        sems = refs[n_in:n_in + 2 * n]
        x, y, c = (lax.axis_index(a) for a in AXES)
        me = 4 * x + 2 * y + c
        for k in range(n):
            if mode == "forward":
                sibling, _ = _peer(x, y, c, SIBLING)
                copies = [(srcs[k].at[_peer(x, y, c, d)[1]], dsts[k].at[_peer(x, y, c, d)[1]], sibling)
                          for d in SAME_CORE_REMOTE]
            elif mode == "gather":
                copies = [(srcs[k], dsts[k].at[me], _peer(x, y, c, d)[0]) for d in (SIBLING,) + SAME_CORE_REMOTE]
            else:
                copies = [(srcs[k].at[_peer(x, y, c, d)[1]], dsts[k].at[me], _peer(x, y, c, d)[0])
                          for d in ALL_PEERS]
            for src, dst, peer in copies:
                pltpu.make_async_remote_copy(src_ref=src, dst_ref=dst, send_sem=sems[2 * k],
                                             recv_sem=sems[2 * k + 1], device_id=peer,
                                             device_id_type=pl.DeviceIdType.MESH).start()

    operands = list(arrays) + lands + [carry]
    outs = pl.pallas_call(
        body, name=name,
        out_shape=tuple([pltpu.SemaphoreType.DMA(())] * (2 * n)
                        + [pltpu.HBM(a.shape, a.dtype) for a in operands]),
        in_specs=[_HBM] * n_in,
        out_specs=tuple([_SEM] * (2 * n) + [_HBM] * n_in),
        input_output_aliases={i: 2 * n + i for i in range(n_in)},
        compiler_params=pltpu.CompilerParams(has_side_effects=_EFFECT),
    )(*[pltpu.with_memory_space_constraint(a, pltpu.HBM) for a in operands])
    sems, thru = outs[:2 * n], outs[2 * n:-1]
    handles = [(sems[2 * k], sems[2 * k + 1], thru[k] if lands else None, thru[n + k] if lands else thru[k])
               for k in range(n)]
    return outs[-1], handles


def copy_wait(handles, n_blocks, after, name):
    n = len(handles)
    sems = [s for h in handles for s in h[:2]]
    srcs = [h[2] for h in handles if h[2] is not None]
    lands = [h[3] for h in handles]
    ns = len(srcs)

    def body(*refs):
        dsts = refs[ns:ns + n]
        sem_refs = refs[ns + n:ns + 3 * n]
        x, y, c = (lax.axis_index(a) for a in AXES)
        for k in range(n):
            blocks = dsts[k].at[pl.ds(0, n_blocks)]
            pltpu.make_async_remote_copy(
                src_ref=blocks, dst_ref=blocks, send_sem=sem_refs[2 * k], recv_sem=sem_refs[2 * k + 1],
                device_id=(x, y, c), device_id_type=pl.DeviceIdType.MESH).wait()

    outs = pl.pallas_call(
        body, name=name,
        out_shape=tuple([pltpu.HBM(a.shape, a.dtype) for a in srcs + lands]),
        in_specs=[_HBM] * (ns + n) + [_SEM] * (2 * n) + [pl.BlockSpec(memory_space=pl.ANY)],
        out_specs=tuple([_HBM] * (ns + n)),
        input_output_aliases={i: i for i in range(ns + n)},
        compiler_params=pltpu.CompilerParams(has_side_effects=_EFFECT),
    )(*srcs, *lands, *sems, after)
    return (list(outs[:ns]) if ns else [None] * n), list(outs[ns:])


def exchange(arrays, modes, name):
    n = len(arrays)
    hbm = pl.BlockSpec(memory_space=pltpu.HBM)
    out_shape = [jax.ShapeDtypeStruct(a.shape if md == "scatter" else (N_DEV,) + a.shape, a.dtype)
                 for a, md in zip(arrays, modes)]

    def body(*refs):
        ins, outs = refs[:n], refs[n:2 * n]
        send_sems, recv_sems, local_sems = refs[2 * n:]
        x, y, c = (lax.axis_index(a) for a in AXES)
        me = 4 * x + 2 * y + c

        def src(k, p):
            return ins[k].at[p] if modes[k] == "scatter" else ins[k]

        local = [pltpu.make_async_copy(src(k, me), outs[k].at[me], local_sems.at[k]) for k in range(n)]
        for cp in local:
            cp.start()
        started = []
        for d in range(1, N_DEV):
            px = 1 - x if d & 4 else x
            py = 1 - y if d & 2 else y
            pc = 1 - c if d & 1 else c
            pid = 4 * px + 2 * py + pc
            for k in range(n):
                pltpu.make_async_remote_copy(
                    src_ref=src(k, pid), dst_ref=outs[k].at[me],
                    send_sem=send_sems.at[k, d - 1], recv_sem=recv_sems.at[k, d - 1],
                    device_id=(px, py, pc), device_id_type=pl.DeviceIdType.MESH).start()
                started.append((k, d, pid, (px, py, pc)))
        for k, d, pid, peer in started:
            pltpu.make_async_remote_copy(
                src_ref=src(k, pid), dst_ref=outs[k].at[pid],
                send_sem=send_sems.at[k, d - 1], recv_sem=recv_sems.at[k, d - 1],
                device_id=peer, device_id_type=pl.DeviceIdType.MESH).wait()
        for cp in local:
            cp.wait()

    outs = pl.pallas_call(
        body, name=name, in_specs=[hbm] * n, out_specs=[hbm] * n, out_shape=out_shape,
        scratch_shapes=[pltpu.SemaphoreType.DMA((n, N_DEV - 1)), pltpu.SemaphoreType.DMA((n, N_DEV - 1)),
                        pltpu.SemaphoreType.DMA((n,))],
        compiler_params=pltpu.CompilerParams(has_side_effects=True),
    )(*arrays)
    return list(outs)


BIG = ("w_in", "w_uq", "w_ukv", "w_o_mla", "w_o_gdn", "w_o", "w_gate_up", "w_down")
ROW_SHARDED = ("w_o", "w_down")
SMALL = ("b_ada", "norm_mix", "norm_ffn", "q_a_norm", "kv_a_norm", "A_log", "dt_bias", "gdn_norm",
         "final_norm")
WEIGHTS = ("w_ada", "b_ada", "norm_mix", "norm_ffn", "w_in", "q_a_norm", "kv_a_norm", "w_uq", "w_ukv",
           "w_o_mla", "conv_w", "A_log", "dt_bias", "gdn_norm", "w_o_gdn", "w_o", "w_gate_up", "w_down",
           "final_norm")


def _unslot(g):
    return g.transpose(1, 0, 2).reshape(g.shape[1], -1)


def _cols(g):
    return g if g.shape[-1] % LANE == 0 else _unslot(g)


def _stack_rows(g):
    return g.reshape(-1, g.shape[-1])


def _pad_cols(a):
    return jnp.pad(a, ((0, 0), (0, _pad_lanes(a.shape[1]) - a.shape[1])))


def _rope(xv, cos, sin):
    x1, x2 = jnp.split(xv, 2, axis=-1)
    return jnp.concatenate([x1 * cos - x2 * sin, x2 * cos + x1 * sin], axis=-1)


MIX_WEIGHTS = ("w_uq", "w_ukv", "w_o_mla", "w_o_gdn", "w_o")
FFN_WEIGHTS = ("w_gate_up", "w_down")


def _stage_in(x, mod, nm, w_in_s, tg):
    d = x.shape[1]
    hg = GDN_HEADS
    w_in = _unslot(w_in_s)
    o1 = Q_LORA + KV_LORA + QK_ROPE
    o2 = o1 + 2 * hg * GDN_DK + hg * GDN_DV
    o3 = o2 + hg * GDN_DV
    o4 = o3 + 2 * hg
    h = ada_norm(x, nm, mod[:, d:2 * d], mod[:, :d], "mix" + tg)
    return (mm(h, _pad_cols(w_in[:, :o1]), "in_a" + tg, BF16), mm(h, w_in[:, o1:o2], "in_qkv" + tg, BF16),
            mm(h, w_in[:, o2:o3], "in_z" + tg, BF16), mm(h, _pad_cols(w_in[:, o3:o4]), "in_ba" + tg, F32),
            mm(h, w_in[:, o4:o4 + 2 * d], "in_g" + tg, BF16))


def _stage_mix(x, mod, seg_a, qkv, z, ba, gl, w_uq_s, w_ukv_s, w_o_mla_s, w_o_gdn_s, w_o_s, conv_s,
               qan, kvan, a_log, dtb, gn, cos, sin, tg):
    t, d = x.shape
    hq, hg = MLA_HEADS, GDN_HEADS
    w_uq = _unslot(w_uq_s).reshape(Q_LORA, hq, QK_NOPE + QK_ROPE)
    w_uq = jnp.concatenate([w_uq[:, :, :QK_NOPE].reshape(Q_LORA, hq * QK_NOPE),
                            w_uq[:, :, QK_NOPE:].reshape(Q_LORA, hq * QK_ROPE)], axis=1)
    c_q = seg_a[:, :Q_LORA]
    c_kv = seg_a[:, Q_LORA:Q_LORA + KV_LORA]
    k_pe = seg_a[:, Q_LORA + KV_LORA:Q_LORA + KV_LORA + QK_ROPE]
    qf = mm(rms_norm(c_q, qan, "qa" + tg), w_uq, "uq" + tg, BF16)
    kvf = mm(rms_norm(c_kv, kvan, "kva" + tg), _cols(w_ukv_s), "ukv" + tg, BF16)
    qn = qf[:, :hq * QK_NOPE]
    q_pe = qf[:, hq * QK_NOPE:].astype(F32).reshape(t, hq, QK_ROPE)
    qr = _rope(q_pe, cos[:, None, :], sin[:, None, :]).transpose(1, 0, 2).astype(BF16)
    kr = _rope(k_pe.astype(F32), cos, sin).astype(BF16)
    y_a = mm(attention(qn, qr, kvf, kr, tg), _cols(w_o_mla_s), "o_mla" + tg, BF16)
    conv_w = conv_s.transpose(1, 0, 2).reshape(CONV_WIDTH, -1)
    qkv_c = conv_silu(qkv, conv_w, tg)
    bl = ba[:, :hg].T[:, :, None]
    al = ba[:, hg:2 * hg].T[:, :, None]
    o_gdn = gdn(qkv_c, z, bl, al, a_log.reshape(hg, 1, 1), dtb.reshape(hg, 1, 1), gn, tg)
    y_b = mm(o_gdn, _cols(w_o_gdn_s), "o_gdn" + tg, BF16)
    mix = mm(gate_mix(gl, y_a, y_b, tg), _stack_rows(w_o_s), "w_o" + tg, F32)
    return resid(x, mod[:, 2 * d:3 * d], mix, "mix" + tg)


def _stage_ffn(x, mod, nf, w_gu_s, w_down_s, tg):
    d = x.shape[1]
    h = ada_norm(x, nf, mod[:, 4 * d:5 * d], mod[:, 3 * d:4 * d], "ffn" + tg)
    gu = mm(h, _cols(w_gu_s), "gu" + tg, BF16)
    dn = mm(swiglu(gu, tg), _stack_rows(w_down_s), "down" + tg, F32)
    return resid(x, mod[:, 5 * d:6 * d], dn, "ffn" + tg)


def _flat_row(arrs):
    v = jnp.concatenate([a.reshape(-1) for a in arrs])
    return jnp.pad(v, (0, _pad_lanes(v.shape[0]) - v.shape[0]))[None, :]


def kernel(x, c, positions, w_ada, b_ada, norm_mix, norm_ffn, w_in, q_a_norm, kv_a_norm, w_uq, w_ukv, w_o_mla, conv_w, A_log, dt_bias, gdn_norm, w_o_gdn, w_o, w_gate_up, w_down, final_norm, loss_target, m_w_ada, m_b_ada, m_norm_mix, m_norm_ffn, m_w_in, m_q_a_norm, m_kv_a_norm, m_w_uq, m_w_ukv, m_w_o_mla, m_conv_w, m_A_log, m_dt_bias, m_gdn_norm, m_w_o_gdn, m_w_o, m_w_gate_up, m_w_down, m_final_norm, v_w_ada, v_b_ada, v_norm_mix, v_norm_ffn, v_w_in, v_q_a_norm, v_kv_a_norm, v_w_uq, v_w_ukv, v_w_o_mla, v_conv_w, v_A_log, v_dt_bias, v_gdn_norm, v_w_o_gdn, v_w_o, v_w_gate_up, v_w_down, v_final_norm):
    given = dict(locals())
    t, d = x.shape[1], x.shape[2]
    n_ada = w_ada.shape[2]
    me = 4 * lax.axis_index("x") + 2 * lax.axis_index("y") + lax.axis_index("c")

    def with_own(land, own):
        return lax.dynamic_update_slice(land, own[None], (me,) + (0,) * own.ndim)

    got = exchange([c, conv_w], ["gather", "gather"], "gather_small")
    c_all, conv_g = got[0].reshape(N_DEV, d), got[1]
    c_rows = jnp.pad(c_all, ((0, 16 - N_DEV), (0, 0)))
    mod_cols = jnp.stack([_mm(c_rows, w_ada[l], "nn", F32, "ada_mod%d" % l, a_act="silu")[:N_DEV]
                          for l in range(DEPTH)], axis=1)
    mod_mine = exchange([mod_cols], ["scatter"], "scatter_mod")[0]
    mods = mod_mine.transpose(1, 0, 2).reshape(DEPTH, N_DEV * n_ada) + b_ada

    groups = [[(n, l) for n in names] for l in range(DEPTH) for names in (("w_in",), MIX_WEIGHTS, FFN_WEIGHTS)]
    gtags = [s + str(l) for l in range(DEPTH) for s in ("in", "mix", "ffn")]
    keys = [k for ks in groups for k in ks]
    mods, handles = copy_start([given[n][l].astype(BF16) for n, l in keys], "gather", mods, "gather_start")
    handles = dict(zip(keys, handles))
    own, relayed = {}, {}

    def relay(gi, carry):
        ks = groups[gi]
        srcs, lands = copy_wait([handles[k] for k in ks], 1 + len(SAME_CORE_REMOTE), carry,
                                "wait_ici_" + gtags[gi])
        own.update(zip(ks, srcs))
        carry, hs = copy_start(lands, "forward", carry, "relay_" + gtags[gi])
        relayed.update(zip(ks, hs))
        return carry

    def landed(gi, after):
        ks = groups[gi]
        _, lands = copy_wait([relayed[k] for k in ks], len(SAME_CORE_REMOTE), after, "wait_" + gtags[gi])
        return [with_own(land, own[k]) for k, land in zip(ks, lands)]

    inv_freq = 1.0 / (ROPE_THETA ** (jnp.arange(0, QK_ROPE, 2, dtype=F32) / QK_ROPE))
    ang = positions[0].astype(F32)[:, None] * inv_freq
    cos, sin = jnp.cos(ang), jnp.sin(ang)
    relay_before = {0: [0], 1: [1], 2: [2, 3], 3: [4], 4: [5], 5: []}

    def weights_for(stage, carry):
        for gi in relay_before[stage]:
            carry = relay(gi, carry)
        return carry, landed(stage, carry)

    xl = x[0]
    vjps = []
    for l in range(DEPTH):
        tg = str(l)
        mod = mods[l:l + 1]
        xl, (w_in_s,) = weights_for(3 * l, xl)
        seg, vjp_in = jax.vjp(lambda *a, tg=tg: _stage_in(*a, tg), xl, mod, norm_mix[l:l + 1], w_in_s)
        seg0, w_mix = weights_for(3 * l + 1, seg[0])
        seg = (seg0,) + tuple(seg[1:])
        xm, vjp_mix = jax.vjp(lambda *a, tg=tg: _stage_mix(*a, cos, sin, tg), xl, mod, *seg, *w_mix,
                              conv_g[:, l], q_a_norm[l:l + 1], kv_a_norm[l:l + 1], A_log[l], dt_bias[l],
                              gdn_norm[l:l + 1])
        xm, w_ffn = weights_for(3 * l + 2, xm)
        xl, vjp_ffn = jax.vjp(lambda *a, tg=tg: _stage_ffn(*a, tg), xm, mod, norm_ffn[l:l + 1], *w_ffn)
        vjps.append((vjp_in, vjp_mix, vjp_ffn))

    loss_t, g, dfn = loss_head(xl, final_norm[None, :], loss_target[0])
    loss = lax.psum(loss_t[0, 0], AXES)
    dsmall = {n: [None] * DEPTH for n in SMALL + ("conv_w",)}
    dmods = [None] * DEPTH
    sent = {}

    def send(ks, grads, carry, name):
        carry, hs = copy_start(list(grads), "scatter", carry, name)
        sent.update(zip(ks, hs))
        return carry

    for l in reversed(range(DEPTH)):
        tg = str(l)
        vjp_in, vjp_mix, vjp_ffn = vjps[l]
        dxm, dmod_f, dsmall["norm_ffn"][l], *dw = vjp_ffn(g)
        dxm = send([(n, l) for n in FFN_WEIGHTS], dw, dxm, "scatter_ffn" + tg)
        dx_m, dmod_m, *rest = vjp_mix(dxm)
        dseg, dw, rest = rest[:5], rest[5:5 + len(MIX_WEIGHTS)], rest[5 + len(MIX_WEIGHTS):]
        dseg[0] = send([(n, l) for n in MIX_WEIGHTS], dw, dseg[0], "scatter_mix" + tg)
        for n, gr in zip(("conv_w", "q_a_norm", "kv_a_norm", "A_log", "dt_bias", "gdn_norm"), rest):
            dsmall[n][l] = gr
        dx_i, dmod_i, dsmall["norm_mix"][l], dw_in = vjp_in(tuple(dseg))
        g = dx_i + dx_m
        if l > 0:
            g = send([("w_in", l)], [dw_in], g, "scatter_in" + tg)
        dmods[l] = dmod_f + dmod_m + dmod_i
    dx = g
    dmods = jnp.concatenate(dmods, axis=0)
    dconv = jnp.stack(dsmall.pop("conv_w"), axis=1)
    dsmall = {n: jnp.concatenate(v, axis=0) if v[0].ndim == 2 else jnp.stack(v)
              for n, v in dsmall.items() if v[0] is not None}
    dsmall["b_ada"] = dmods
    dsmall["final_norm"] = dfn[0]

    dmod_cols = dmods.reshape(DEPTH, N_DEV, n_ada).transpose(1, 0, 2)
    conv_parts, dmod_all, small_parts = exchange(
        [dconv, dmod_cols, _flat_row([dsmall[n] for n in SMALL])], ["scatter", "scatter", "gather"],
        "exchange_small")
    dmod_all = send([("w_in", 0)], [dw_in], dmod_all, "scatter_in0")

    res = {}
    dm_rows = jnp.pad(dmod_all, ((0, 16 - N_DEV), (0, 0), (0, 0)))
    g_ada = jnp.stack([_mm(c_rows, dm_rows[:, l], "tn", F32, "ada_dw%d" % l, a_act="silu")
                       for l in range(DEPTH)])
    r2 = (DEPTH * d, n_ada)
    outs = adamw(w_ada.reshape(r2), [g_ada.reshape((1,) + r2)], m_w_ada.reshape(r2), v_w_ada.reshape(r2),
                 "adamw_w_ada")
    res["w_ada"] = [o.reshape(w_ada.shape) for o in outs]
    packed = SMALL + ("conv_w",)
    p_all = jnp.concatenate([small_parts, conv_parts.reshape(N_DEV, 1, -1)], axis=2)
    pack = lambda pre: jnp.concatenate([_flat_row([given[pre + n] for n in SMALL]),
                                        given[pre + "conv_w"].reshape(1, -1)], axis=1)
    outs = adamw(pack(""), [p_all], pack("m_"), pack("v_"), "adamw_small")
    done = [res["w_ada"][1], outs[1]]
    for group, gname in ((FFN_WEIGHTS, "ffn"), (MIX_WEIGHTS, "mix"), (("w_in",), "in")):
        ks = [(n, l) for l in reversed(range(DEPTH)) for n in group]
        after = sum(lax.slice(a, (0,) * a.ndim, (1,) * a.ndim).reshape(1, 1) for a in done)
        srcs, lands = copy_wait([sent[k] for k in ks], len(ALL_PEERS), after, "scatter_wait_" + gname)
        parts = {k: with_own(land, lax.dynamic_index_in_dim(src, me, 0, keepdims=False))
                 for k, src, land in zip(ks, srcs, lands)}
        for n in group:
            w = given[n]
            r2 = (w.shape[0] * w.shape[1], w.shape[2])
            res[n] = [o.reshape(w.shape) for o in
                      adamw(w.reshape(r2), [parts[(n, l)] for l in range(DEPTH)], given["m_" + n].reshape(r2),
                            given["v_" + n].reshape(r2), "adamw_" + n)]
            done.append(res[n][1])
    off = 0
    for n in packed:
        if n == "conv_w":
            off = small_parts.shape[2]
        size = math.prod(given[n].shape)
        res[n] = [o[0, off:off + size].reshape(given[n].shape) for o in outs]
        off += size

    return (loss, dx[None]) + tuple(res[n][i] for i in range(4) for n in WEIGHTS)
```

```python
import functools
import math

import jax
import jax.numpy as jnp
from jax import lax
from jax.experimental import pallas as pl
from jax.experimental.pallas import tpu as pltpu

F32 = jnp.float32
BF16 = jnp.bfloat16

MLA_HEADS = 8
QK_NOPE = 128
QK_ROPE = 64
V_HEAD = 128
Q_LORA = 512
KV_LORA = 512
ROPE_THETA = 10000.0
GDN_HEADS = 8
GDN_DK = 128
GDN_DV = 128
CONV_WIDTH = 4
CHUNK = 64
DEPTH = 2
EPS = 1e-6
ADAM_LR = 0.001
ADAM_B1 = 0.9
ADAM_B2 = 0.999
ADAM_EPS = 1e-08
ADAM_WD = 0.01
ADAM_STEP = 10

N_DEV = 8
AXES = ("x", "y", "c")
LANE = 128
VMEM_LIMIT = 48 * 1024 * 1024
MM_VMEM_BUDGET = 36 * 1024 * 1024

NN = (((1,), (0,)), ((), ()))
NT = (((1,), (1,)), ((), ()))
TN = (((0,), (0,)), ((), ()))


def _cp(sem=None):
    return pltpu.CompilerParams(dimension_semantics=sem, vmem_limit_bytes=VMEM_LIMIT)


def _tile(n, cap):
    if n <= cap:
        return n
    for t in range(cap - cap % LANE, 0, -LANE):
        if n % t == 0:
            return t
    return n


def _rows(t, cap=256):
    return cap if t % cap == 0 else t


def _pad_lanes(n):
    return -(-n // LANE) * LANE


def _sigmoid(x):
    return 1.0 / (1.0 + jnp.exp(-x))


def _softplus(x):
    return jnp.maximum(x, 0.0) + jnp.log(1.0 + jnp.exp(-jnp.abs(x)))


def _tile_slot(n, cap):
    t = _tile(n, cap)
    return n if t < 256 < n <= 1536 else t


def _mm(a, b, dims, out_dtype, name, a_act=None, slots=False):
    if dims == "nn":
        m, k = a.shape
        n = b.shape[-1] * (N_DEV if slots else 1)
    elif dims == "nt":
        m, k = a.shape
        n = b.shape[-2]
    else:
        k, m = a.shape
        n = b.shape[-1]
    tm = _tile(m, 1536)
    tn = _tile_slot(n // N_DEV, 512) if slots and dims != "nt" else _tile(n, 1024)
    k_slot = k // N_DEV if slots and dims == "nt" else k

    def vmem_bytes(tk_):
        a_b, b_b = tm * tk_ * a.dtype.itemsize, tk_ * tn * b.dtype.itemsize
        casts = (tm * tk_ * 2 if a.dtype != BF16 else 0) + (tk_ * tn * 2 if b.dtype != BF16 else 0)
        return 2 * (a_b + b_b + tm * tn * jnp.dtype(out_dtype).itemsize) + 2 * tm * tn * 4 + casts

    tk = _tile_slot(k_slot, 1536) if slots and dims == "nt" else _tile(k, 2048)
    while vmem_bytes(tk) > MM_VMEM_BUDGET and tk % (2 * LANE) == 0:
        tk //= 2
    nk = k // tk
    per_n = (n // N_DEV) // tn if slots else 1
    per_k = (k // N_DEV) // tk if slots else 1
    if dims == "tn":
        a_spec = pl.BlockSpec((tk, tm), lambda i, j, kk: (kk, i))
    else:
        a_spec = pl.BlockSpec((tm, tk), lambda i, j, kk: (i, kk))
    if dims == "nt":
        if slots:
            b_spec = pl.BlockSpec((None, tn, tk), lambda i, j, kk: (kk // per_k, j, kk % per_k))
        else:
            b_spec = pl.BlockSpec((tn, tk), lambda i, j, kk: (j, kk))
    elif dims == "nn" and slots:
        b_spec = pl.BlockSpec((None, tk, tn), lambda i, j, kk: (j // per_n, kk, j % per_n))
    else:
        b_spec = pl.BlockSpec((tk, tn), lambda i, j, kk: (kk, j))
    if dims == "tn" and slots:
        out_spec = pl.BlockSpec((None, tm, tn), lambda i, j, kk: (j // per_n, i, j % per_n))
        out_shape = jax.ShapeDtypeStruct((N_DEV, m, n // N_DEV), out_dtype)
    else:
        out_spec = pl.BlockSpec((tm, tn), lambda i, j, kk: (i, j))
        out_shape = jax.ShapeDtypeStruct((m, n), out_dtype)
    dn = {"nn": NN, "nt": NT, "tn": TN}[dims]

    def product(a_ref, b_ref):
        av = a_ref[...]
        if a_act == "silu":
            av = av * _sigmoid(av)
        return lax.dot_general(av.astype(BF16), b_ref[...].astype(BF16), dn, preferred_element_type=F32)

    def body_one(a_ref, b_ref, o_ref):
        o_ref[...] = product(a_ref, b_ref).astype(o_ref.dtype)

    def body_acc(a_ref, b_ref, o_ref, acc_ref):
        kk = pl.program_id(2)

        @pl.when(kk == 0)
        def _():
            acc_ref[...] = jnp.zeros_like(acc_ref)

        acc_ref[...] += product(a_ref, b_ref)

        @pl.when(kk == nk - 1)
        def _():
            o_ref[...] = acc_ref[...].astype(o_ref.dtype)

    return pl.pallas_call(
        body_one if nk == 1 else body_acc, name=name, grid=(m // tm, n // tn, nk),
        in_specs=[a_spec, b_spec], out_specs=out_spec, out_shape=out_shape,
        scratch_shapes=[] if nk == 1 else [pltpu.VMEM((tm, tn), F32)],
        compiler_params=_cp(("parallel", "parallel", "arbitrary")),
    )(a, b)


@functools.partial(jax.custom_vjp, nondiff_argnums=(2, 3))
def mm(a, b, tag, out_dtype):
    return _mm(a, b, "nn", out_dtype, "mm_" + tag, slots=b.ndim == 3)


def _mm_f(a, b, tag, out_dtype):
    return mm(a, b, tag, out_dtype), (a, b)


def _mm_b(tag, out_dtype, res, g):
    a, b = res
    slots = b.ndim == 3
    da = _mm(g, b, "nt", a.dtype, "mm_" + tag + "_da", slots=slots)
    db = _mm(a, g, "tn", b.dtype, "mm_" + tag + "_db", slots=slots)
    return da, db


mm.defvjp(_mm_f, _mm_b)


@functools.partial(jax.custom_vjp, nondiff_argnums=(2, 3))
def mmt(a, bt, tag, out_dtype):
    return _mm(a, bt, "nt", out_dtype, "mmt_" + tag)


def _mmt_f(a, bt, tag, out_dtype):
    return mmt(a, bt, tag, out_dtype), (a, bt)


def _mmt_b(tag, out_dtype, res, g):
    a, bt = res
    da = _mm(g, bt, "nn", a.dtype, "mmt_" + tag + "_da")
    dbt = _mm(g, a, "tn", bt.dtype, "mmt_" + tag + "_db")
    return da, dbt


mmt.defvjp(_mmt_f, _mmt_b)


def _norm_fwd_call(x, nw, sc, sh, name):
    t, d = x.shape
    tr = _rows(t)
    mod = sc is not None
    row = pl.BlockSpec((tr, d), lambda i: (i, 0))
    vec = pl.BlockSpec((1, d), lambda i: (0, 0))

    def body(*refs):
        if mod:
            x_ref, nw_ref, sc_ref, sh_ref, o_ref = refs
        else:
            x_ref, nw_ref, o_ref = refs
        xv = x_ref[...].astype(F32)
        r = lax.rsqrt(jnp.mean(xv * xv, axis=-1, keepdims=True) + EPS)
        y = (xv * r) * nw_ref[...]
        if mod:
            y = y * (1.0 + sc_ref[...]) + sh_ref[...]
        o_ref[...] = y.astype(o_ref.dtype)

    args = (x, nw, sc, sh) if mod else (x, nw)
    return pl.pallas_call(
        body, name=name, grid=(t // tr,),
        in_specs=[row] + [vec] * (len(args) - 1), out_specs=row,
        out_shape=jax.ShapeDtypeStruct((t, d), BF16),
        compiler_params=_cp(("parallel",)),
    )(*args)


def _norm_bwd_call(x, nw, sc, dh, name):
    t, d = x.shape
    tr = _rows(t)
    mod = sc is not None
    row = pl.BlockSpec((tr, d), lambda i: (i, 0))
    vec = pl.BlockSpec((1, d), lambda i: (0, 0))

    def body(*refs):
        if mod:
            x_ref, nw_ref, sc_ref, dh_ref, dx_ref, dnw_ref, dsc_ref, dsh_ref = refs
        else:
            x_ref, nw_ref, dh_ref, dx_ref, dnw_ref = refs
        i = pl.program_id(0)
        xv = x_ref[...].astype(F32)
        dh = dh_ref[...].astype(F32)
        r = lax.rsqrt(jnp.mean(xv * xv, axis=-1, keepdims=True) + EPS)
        y = xv * r
        a = nw_ref[...] * (1.0 + sc_ref[...]) if mod else nw_ref[...]
        dy = dh * a
        dx_ref[...] = (r * (dy - y * jnp.mean(dy * y, axis=-1, keepdims=True))).astype(dx_ref.dtype)
        da = jnp.sum(dh * y, axis=0, keepdims=True)

        @pl.when(i == 0)
        def _():
            dnw_ref[...] = jnp.zeros_like(dnw_ref)
            if mod:
                dsc_ref[...] = jnp.zeros_like(dsc_ref)
                dsh_ref[...] = jnp.zeros_like(dsh_ref)

        if mod:
            dnw_ref[...] += da * (1.0 + sc_ref[...])
            dsc_ref[...] += da * nw_ref[...]
            dsh_ref[...] += jnp.sum(dh, axis=0, keepdims=True)
        else:
            dnw_ref[...] += da

    args = (x, nw, sc, dh) if mod else (x, nw, dh)
    n_vec = 3 if mod else 1
    return pl.pallas_call(
        body, name=name, grid=(t // tr,),
        in_specs=[row] + [vec] * (len(args) - 2) + [row],
        out_specs=[row] + [vec] * n_vec,
        out_shape=[jax.ShapeDtypeStruct((t, d), x.dtype)] + [jax.ShapeDtypeStruct((1, d), F32)] * n_vec,
        compiler_params=_cp(("arbitrary",)),
    )(*args)


@functools.partial(jax.custom_vjp, nondiff_argnums=(4,))
def ada_norm(x, nw, sc, sh, tag):
    return _norm_fwd_call(x, nw, sc, sh, "adanorm_" + tag)


def _ada_norm_f(x, nw, sc, sh, tag):
    return _norm_fwd_call(x, nw, sc, sh, "adanorm_" + tag), (x, nw, sc)


def _ada_norm_b(tag, res, dh):
    x, nw, sc = res
    dx, dnw, dsc, dsh = _norm_bwd_call(x, nw, sc, dh, "adanorm_" + tag + "_bwd")
    return dx, dnw, dsc, dsh


ada_norm.defvjp(_ada_norm_f, _ada_norm_b)


@functools.partial(jax.custom_vjp, nondiff_argnums=(2,))
def rms_norm(x, nw, tag):
    return _norm_fwd_call(x, nw, None, None, "rms_" + tag)


def _rms_norm_f(x, nw, tag):
    return _norm_fwd_call(x, nw, None, None, "rms_" + tag), (x, nw)


def _rms_norm_b(tag, res, dh):
    x, nw = res
    dx, dnw = _norm_bwd_call(x, nw, None, dh, "rms_" + tag + "_bwd")
    return dx, dnw


rms_norm.defvjp(_rms_norm_f, _rms_norm_b)


def _gate_mix_fwd_call(gl, ya, yb, name):
    t, d = ya.shape
    tr = _rows(t)
    row = pl.BlockSpec((tr, d), lambda i: (i, 0))

    def body(ga_ref, gb_ref, ya_ref, yb_ref, o_ref):
        o_ref[...] = (_sigmoid(ga_ref[...].astype(F32)) * ya_ref[...].astype(F32)
                      + _sigmoid(gb_ref[...].astype(F32)) * yb_ref[...].astype(F32)).astype(o_ref.dtype)

    return pl.pallas_call(
        body, name=name, grid=(t // tr,),
        in_specs=[row, pl.BlockSpec((tr, d), lambda i: (i, 1)), row, row], out_specs=row,
        out_shape=jax.ShapeDtypeStruct((t, d), BF16),
        compiler_params=_cp(("parallel",)),
    )(gl, gl, ya, yb)


def _gate_mix_bwd_call(gl, ya, yb, dm, name):
    t, d = ya.shape
    tr = _rows(t)
    row = pl.BlockSpec((tr, d), lambda i: (i, 0))
    wide = pl.BlockSpec((tr, 2 * d), lambda i: (i, 0))

    def body(gl_ref, ya_ref, yb_ref, dm_ref, dgl_ref, dya_ref, dyb_ref):
        dm = dm_ref[...].astype(F32)
        ga = _sigmoid(gl_ref[:, :d].astype(F32))
        gb = _sigmoid(gl_ref[:, d:].astype(F32))
        dya_ref[...] = (dm * ga).astype(dya_ref.dtype)
        dyb_ref[...] = (dm * gb).astype(dyb_ref.dtype)
        dgl_ref[:, :d] = (dm * ya_ref[...].astype(F32) * ga * (1.0 - ga)).astype(dgl_ref.dtype)
        dgl_ref[:, d:] = (dm * yb_ref[...].astype(F32) * gb * (1.0 - gb)).astype(dgl_ref.dtype)

    return pl.pallas_call(
        body, name=name, grid=(t // tr,),
        in_specs=[wide, row, row, row], out_specs=[wide, row, row],
        out_shape=[jax.ShapeDtypeStruct((t, 2 * d), gl.dtype), jax.ShapeDtypeStruct((t, d), ya.dtype),
                   jax.ShapeDtypeStruct((t, d), yb.dtype)],
        compiler_params=_cp(("parallel",)),
    )(gl, ya, yb, dm)


@functools.partial(jax.custom_vjp, nondiff_argnums=(3,))
def gate_mix(gl, ya, yb, tag):
    return _gate_mix_fwd_call(gl, ya, yb, "gatemix_" + tag)


def _gate_mix_f(gl, ya, yb, tag):
    return _gate_mix_fwd_call(gl, ya, yb, "gatemix_" + tag), (gl, ya, yb)


def _gate_mix_b(tag, res, dm):
    return tuple(_gate_mix_bwd_call(*res, dm, "gatemix_" + tag + "_bwd"))


gate_mix.defvjp(_gate_mix_f, _gate_mix_b)


def _resid_fwd_call(x, gt, m, name):
    t, d = x.shape
    tr = _rows(t)
    row = pl.BlockSpec((tr, d), lambda i: (i, 0))
    vec = pl.BlockSpec((1, d), lambda i: (0, 0))

    def body(x_ref, gt_ref, m_ref, o_ref):
        o_ref[...] = x_ref[...] + gt_ref[...] * m_ref[...]

    return pl.pallas_call(
        body, name=name, grid=(t // tr,), in_specs=[row, vec, row], out_specs=row,
        out_shape=jax.ShapeDtypeStruct((t, d), F32), compiler_params=_cp(("parallel",)),
    )(x, gt, m)


def _resid_bwd_call(gt, m, g, name):
    t, d = m.shape
    tr = _rows(t)
    row = pl.BlockSpec((tr, d), lambda i: (i, 0))
    vec = pl.BlockSpec((1, d), lambda i: (0, 0))

    def body(gt_ref, m_ref, g_ref, dm_ref, dgt_ref):
        i = pl.program_id(0)
        g = g_ref[...]
        dm_ref[...] = g * gt_ref[...]

        @pl.when(i == 0)
        def _():
            dgt_ref[...] = jnp.zeros_like(dgt_ref)

        dgt_ref[...] += jnp.sum(g * m_ref[...], axis=0, keepdims=True)

    return pl.pallas_call(
        body, name=name, grid=(t // tr,), in_specs=[vec, row, row], out_specs=[row, vec],
        out_shape=[jax.ShapeDtypeStruct((t, d), F32), jax.ShapeDtypeStruct((1, d), F32)],
        compiler_params=_cp(("arbitrary",)),
    )(gt, m, g)


@functools.partial(jax.custom_vjp, nondiff_argnums=(3,))
def resid(x, gt, m, tag):
    return _resid_fwd_call(x, gt, m, "resid_" + tag)


def _resid_f(x, gt, m, tag):
    return _resid_fwd_call(x, gt, m, "resid_" + tag), (gt, m)


def _resid_b(tag, res, g):
    gt, m = res
    dm, dgt = _resid_bwd_call(gt, m, g, "resid_" + tag + "_bwd")
    return g, dgt, dm


resid.defvjp(_resid_f, _resid_b)


def _swiglu_fwd_call(gu, name):
    t, f2 = gu.shape
    f = f2 // 2
    tr = _rows(t, 128)
    half = pl.BlockSpec((tr, f), lambda i: (i, 0))

    def body(g_ref, u_ref, o_ref):
        g = g_ref[...].astype(F32)
        o_ref[...] = (g * _sigmoid(g) * u_ref[...].astype(F32)).astype(o_ref.dtype)

    return pl.pallas_call(
        body, name=name, grid=(t // tr,),
        in_specs=[half, pl.BlockSpec((tr, f), lambda i: (i, 1))], out_specs=half,
        out_shape=jax.ShapeDtypeStruct((t, f), BF16), compiler_params=_cp(("parallel",)),
    )(gu, gu)


def _swiglu_bwd_call(gu, da, name):
    t, f2 = gu.shape
    f = f2 // 2
    tr = _rows(t, 128)
    wide = pl.BlockSpec((tr, f2), lambda i: (i, 0))

    def body(gu_ref, da_ref, dgu_ref):
        g = gu_ref[:, :f].astype(F32)
        u = gu_ref[:, f:].astype(F32)
        da = da_ref[...].astype(F32)
        s = _sigmoid(g)
        dgu_ref[:, :f] = (da * u * s * (1.0 + g * (1.0 - s))).astype(dgu_ref.dtype)
        dgu_ref[:, f:] = (da * g * s).astype(dgu_ref.dtype)

    return pl.pallas_call(
        body, name=name, grid=(t // tr,),
        in_specs=[wide, pl.BlockSpec((tr, f), lambda i: (i, 0))], out_specs=wide,
        out_shape=jax.ShapeDtypeStruct((t, f2), gu.dtype), compiler_params=_cp(("parallel",)),
    )(gu, da)


@functools.partial(jax.custom_vjp, nondiff_argnums=(1,))
def swiglu(gu, tag):
    return _swiglu_fwd_call(gu, "swiglu_" + tag)


def _swiglu_f(gu, tag):
    return _swiglu_fwd_call(gu, "swiglu_" + tag), (gu,)


def _swiglu_b(tag, res, da):
    return (_swiglu_bwd_call(res[0], da, "swiglu_" + tag + "_bwd"),)


swiglu.defvjp(_swiglu_f, _swiglu_b)


def loss_head(x, fw, tgt):
    t, d = x.shape
    tr = _rows(t)
    row = pl.BlockSpec((tr, d), lambda i: (i, 0))
    vec = pl.BlockSpec((1, d), lambda i: (0, 0))
    tile = pl.BlockSpec((8, LANE), lambda i: (0, 0))

    def body(x_ref, fw_ref, tgt_ref, loss_ref, dx_ref, dfw_ref):
        i = pl.program_id(0)
        xv = x_ref[...]
        fw = fw_ref[...]
        r = lax.rsqrt(jnp.mean(xv * xv, axis=-1, keepdims=True) + EPS)
        yh = xv * r
        e = yh * fw - tgt_ref[...]
        dy = e * (1.0 / d)
        dyw = dy * fw
        dx_ref[...] = r * (dyw - yh * jnp.mean(dyw * yh, axis=-1, keepdims=True))

        @pl.when(i == 0)
        def _():
            loss_ref[...] = jnp.zeros_like(loss_ref)
            dfw_ref[...] = jnp.zeros_like(dfw_ref)

        loss_ref[...] += 0.5 * jnp.sum(jnp.mean(e * e, axis=-1, keepdims=True))
        dfw_ref[...] += jnp.sum(dy * yh, axis=0, keepdims=True)

    return pl.pallas_call(
        body, name="loss_head", grid=(t // tr,), in_specs=[row, vec, row],
        out_specs=[tile, row, vec],
        out_shape=[jax.ShapeDtypeStruct((8, LANE), F32), jax.ShapeDtypeStruct((t, d), F32),
                   jax.ShapeDtypeStruct((1, d), F32)],
        compiler_params=_cp(("arbitrary",)),
    )(x, fw, tgt)


def _attn_scores(qn_ref, qr_ref, kn_ref, kr_ref, diag):
    tq = qn_ref.shape[0]
    s = lax.dot_general(qn_ref[...].astype(BF16), kn_ref[...].astype(BF16), NT, preferred_element_type=F32)
    s += lax.dot_general(qr_ref[...].astype(BF16), kr_ref[...].astype(BF16), NT, preferred_element_type=F32)
    s = s * (QK_NOPE + QK_ROPE) ** -0.5
    if diag:
        rows = lax.broadcasted_iota(jnp.int32, (tq, tq), 0)
        cols = lax.broadcasted_iota(jnp.int32, (tq, tq), 1)
        s = jnp.where(cols <= rows, s, -1e30)
    return s


def _attn_fwd_call(qn, qr, kv, kr, name):
    t = qn.shape[0]
    h_n = MLA_HEADS
    tq = _rows(t, 512)
    nq = t // tq
    assert V_HEAD == LANE and tq % LANE == 0

    def body(qn_ref, qr_ref, kn_ref, v_ref, kr_ref, o_ref, lse_ref, m_scr, l_scr, acc_scr):
        i, j = pl.program_id(1), pl.program_id(2)

        @pl.when(j == 0)
        def _():
            m_scr[...] = jnp.full_like(m_scr, -1e30)
            l_scr[...] = jnp.zeros_like(l_scr)
            acc_scr[...] = jnp.zeros_like(acc_scr)

        def step(diag):
            s = _attn_scores(qn_ref, qr_ref, kn_ref, kr_ref, diag)
            m_old = m_scr[...]
            m_new = jnp.maximum(m_old, jnp.max(s, axis=-1, keepdims=True))
            p = jnp.exp(s - jnp.tile(m_new, (1, tq // LANE)))
            alpha = jnp.exp(m_old - m_new)
            l_scr[...] = alpha * l_scr[...] + jnp.sum(p, axis=-1, keepdims=True)
            acc_scr[...] = alpha * acc_scr[...] + jnp.dot(p.astype(BF16), v_ref[...].astype(BF16),
                                                           preferred_element_type=F32)
            m_scr[...] = m_new

        @pl.when(j < i)
        def _():
            step(False)

        @pl.when(j == i)
        def _():
            step(True)
            o_ref[...] = (acc_scr[...] / l_scr[...]).astype(o_ref.dtype)
            lse_ref[...] = (m_scr[...] + jnp.log(l_scr[...]))[:, :1]

    return pl.pallas_call(
        body, name=name, grid=(h_n, nq, nq),
        in_specs=[
            pl.BlockSpec((tq, QK_NOPE), lambda h, i, j: (i, h)),
            pl.BlockSpec((None, tq, QK_ROPE), lambda h, i, j: (h, i, 0)),
            pl.BlockSpec((tq, QK_NOPE), lambda h, i, j: (jnp.minimum(j, i), 2 * h)),
            pl.BlockSpec((tq, V_HEAD), lambda h, i, j: (jnp.minimum(j, i), 2 * h + 1)),
            pl.BlockSpec((tq, QK_ROPE), lambda h, i, j: (jnp.minimum(j, i), 0)),
        ],
        out_specs=[
            pl.BlockSpec((tq, V_HEAD), lambda h, i, j: (i, h)),
            pl.BlockSpec((None, tq, 1), lambda h, i, j: (h, i, 0)),
        ],
        out_shape=[jax.ShapeDtypeStruct((t, h_n * V_HEAD), BF16),
                   jax.ShapeDtypeStruct((h_n, t, 1), F32)],
        scratch_shapes=[pltpu.VMEM((tq, LANE), F32), pltpu.VMEM((tq, LANE), F32),
                        pltpu.VMEM((tq, V_HEAD), F32)],
        compiler_params=_cp(("parallel", "parallel", "arbitrary")),
    )(qn, qr, kv, kv, kr)


def _attn_bwd_call(qn, qr, kv, kr, o, lse, do, name):
    t = qn.shape[0]
    h_n = MLA_HEADS
    tq = _rows(t, 512)
    nq = t // tq
    scale = (QK_NOPE + QK_ROPE) ** -0.5

    def body(qn_ref, qr_ref, kn_ref, v_ref, kr_ref, o_ref, lse_ref, do_ref,
             dqn_ref, dqr_ref, dkv_ref, dkr_ref, dqn_scr, dqr_scr, dkn_scr, dv_scr, dkr_scr):
        j, i = pl.program_id(1), pl.program_id(2)

        @pl.when(jnp.logical_and(j == 0, i == 0))
        def _():
            dqn_scr[...] = jnp.zeros_like(dqn_scr)
            dqr_scr[...] = jnp.zeros_like(dqr_scr)

        @pl.when(i == 0)
        def _():
            dkn_scr[...] = jnp.zeros_like(dkn_scr)
            dv_scr[...] = jnp.zeros_like(dv_scr)
            dkr_scr[...] = jnp.zeros_like(dkr_scr)

        def step(diag):
            qn_b = qn_ref[...].astype(BF16)
            qr_b = qr_ref[...].astype(BF16)
            kn_b = kn_ref[...].astype(BF16)
            kr_b = kr_ref[...].astype(BF16)
            do_b = do_ref[...]
            p = jnp.exp(_attn_scores(qn_ref, qr_ref, kn_ref, kr_ref, diag) - lse_ref[...])
            delta = jnp.sum(do_b.astype(F32) * o_ref[...].astype(F32), axis=-1, keepdims=True)
            dp = lax.dot_general(do_b, v_ref[...].astype(BF16), NT, preferred_element_type=F32)
            ds = (p * (dp - delta) * scale).astype(BF16)
            p_b = p.astype(BF16)
            dv_scr[...] += lax.dot_general(p_b, do_b, TN, preferred_element_type=F32)
            dkn_scr[...] += lax.dot_general(ds, qn_b, TN, preferred_element_type=F32)
            dkr_scr[...] += lax.dot_general(ds, qr_b, TN, preferred_element_type=F32)
            sl = pl.ds(pl.multiple_of(i * tq, tq), tq)
            dqn_scr[sl, :] += jnp.dot(ds, kn_b, preferred_element_type=F32)
            dqr_scr[sl, :] += jnp.dot(ds, kr_b, preferred_element_type=F32)

        @pl.when(i > j)
        def _():
            step(False)

        @pl.when(i == j)
        def _():
            step(True)

        @pl.when(i == nq - 1)
        def _():
            dkv_ref[:, :QK_NOPE] = dkn_scr[...].astype(dkv_ref.dtype)
            dkv_ref[:, QK_NOPE:] = dv_scr[...].astype(dkv_ref.dtype)
            dkr_ref[...] = dkr_scr[...]

        @pl.when(jnp.logical_and(j == nq - 1, i == nq - 1))
        def _():
            dqn_ref[...] = dqn_scr[...].astype(dqn_ref.dtype)
            dqr_ref[...] = dqr_scr[...].astype(dqr_ref.dtype)

    qi = lambda j, i: jnp.maximum(i, j)
    return pl.pallas_call(
        body, name=name, grid=(h_n, nq, nq),
        in_specs=[
            pl.BlockSpec((tq, QK_NOPE), lambda h, j, i: (qi(j, i), h)),
            pl.BlockSpec((None, tq, QK_ROPE), lambda h, j, i: (h, qi(j, i), 0)),
            pl.BlockSpec((tq, QK_NOPE), lambda h, j, i: (j, 2 * h)),
            pl.BlockSpec((tq, V_HEAD), lambda h, j, i: (j, 2 * h + 1)),
            pl.BlockSpec((tq, QK_ROPE), lambda h, j, i: (j, 0)),
            pl.BlockSpec((tq, V_HEAD), lambda h, j, i: (qi(j, i), h)),
            pl.BlockSpec((None, tq, 1), lambda h, j, i: (h, qi(j, i), 0)),
            pl.BlockSpec((tq, V_HEAD), lambda h, j, i: (qi(j, i), h)),
        ],
        out_specs=[
            pl.BlockSpec((t, QK_NOPE), lambda h, j, i: (0, h)),
            pl.BlockSpec((None, t, QK_ROPE), lambda h, j, i: (h, 0, 0)),
            pl.BlockSpec((tq, QK_NOPE + V_HEAD), lambda h, j, i: (j, h)),
            pl.BlockSpec((None, tq, QK_ROPE), lambda h, j, i: (h, j, 0)),
        ],
        out_shape=[jax.ShapeDtypeStruct((t, h_n * QK_NOPE), qn.dtype),
                   jax.ShapeDtypeStruct((h_n, t, QK_ROPE), qr.dtype),
                   jax.ShapeDtypeStruct((t, h_n * (QK_NOPE + V_HEAD)), kv.dtype),
                   jax.ShapeDtypeStruct((h_n, t, QK_ROPE), F32)],
        scratch_shapes=[pltpu.VMEM((t, QK_NOPE), F32), pltpu.VMEM((t, QK_ROPE), F32),
                        pltpu.VMEM((tq, QK_NOPE), F32), pltpu.VMEM((tq, V_HEAD), F32),
                        pltpu.VMEM((tq, QK_ROPE), F32)],
        compiler_params=_cp(("parallel", "arbitrary", "arbitrary")),
    )(qn, qr, kv, kv, kr, o, lse, do)


@functools.partial(jax.custom_vjp, nondiff_argnums=(4,))
def attention(qn, qr, kv, kr, tag):
    return _attn_fwd_call(qn, qr, kv, kr, "attn_" + tag)[0]


def _attention_f(qn, qr, kv, kr, tag):
    o, lse = _attn_fwd_call(qn, qr, kv, kr, "attn_" + tag)
    return o, (qn, qr, kv, kr, o, lse)


def _attention_b(tag, res, do):
    dqn, dqr, dkv, dkr_h = _attn_bwd_call(*res, do, "attn_" + tag + "_bwd")
    return dqn, dqr, dkv, jnp.sum(dkr_h, axis=0).astype(res[3].dtype)


attention.defvjp(_attention_f, _attention_b)


def _shift_down(u, s):
    if s == 0:
        return u
    t = u.shape[0]
    rolled = pltpu.roll(u, s, 0)
    return jnp.where(lax.broadcasted_iota(jnp.int32, u.shape, 0) >= s, rolled, 0.0)


def _shift_up(u, s):
    if s == 0:
        return u
    t = u.shape[0]
    rolled = pltpu.roll(u, t - s, 0)
    return jnp.where(lax.broadcasted_iota(jnp.int32, u.shape, 0) < t - s, rolled, 0.0)


def _conv_blocks(t, c3):
    p = c3 // 3
    tc = _tile(p, 512)
    per = p // tc
    return p, tc, per


def _conv_fwd_call(u, w, name):
    t, c3 = u.shape
    p, tc, per = _conv_blocks(t, c3)

    def body(u_ref, w_ref, o_ref):
        u = u_ref[...].astype(F32)
        y = jnp.zeros_like(u)
        for j in range(CONV_WIDTH):
            y = y + w_ref[j:j + 1, :] * _shift_down(u, CONV_WIDTH - 1 - j)
        o_ref[...] = y * _sigmoid(y)

    return pl.pallas_call(
        body, name=name, grid=(c3 // tc,),
        in_specs=[pl.BlockSpec((t, tc), lambda cb: (0, cb)),
                  pl.BlockSpec((CONV_WIDTH, tc), lambda cb: (0, cb))],
        out_specs=pl.BlockSpec((None, t, tc), lambda cb: (cb // per, 0, cb % per)),
        out_shape=jax.ShapeDtypeStruct((3, t, p), F32),
        compiler_params=_cp(("parallel",)),
    )(u, w)


def _conv_bwd_call(u, w, do, name):
    t, c3 = u.shape
    p, tc, per = _conv_blocks(t, c3)

    def body(u_ref, w_ref, do_ref, du_ref, dw_ref):
        u = u_ref[...].astype(F32)
        shifted = [_shift_down(u, CONV_WIDTH - 1 - j) for j in range(CONV_WIDTH)]
        y = jnp.zeros_like(u)
        for j in range(CONV_WIDTH):
            y = y + w_ref[j:j + 1, :] * shifted[j]
        s = _sigmoid(y)
        dy = do_ref[...] * s * (1.0 + y * (1.0 - s))
        du = jnp.zeros_like(u)
        for j in range(CONV_WIDTH):
            du = du + w_ref[j:j + 1, :] * _shift_up(dy, CONV_WIDTH - 1 - j)
            dw_ref[j:j + 1, :] = jnp.sum(dy * shifted[j], axis=0, keepdims=True)
        du_ref[...] = du.astype(du_ref.dtype)

    return pl.pallas_call(
        body, name=name, grid=(c3 // tc,),
        in_specs=[pl.BlockSpec((t, tc), lambda cb: (0, cb)),
                  pl.BlockSpec((CONV_WIDTH, tc), lambda cb: (0, cb)),
                  pl.BlockSpec((None, t, tc), lambda cb: (cb // per, 0, cb % per))],
        out_specs=[pl.BlockSpec((t, tc), lambda cb: (0, cb)),
                   pl.BlockSpec((CONV_WIDTH, tc), lambda cb: (0, cb))],
        out_shape=[jax.ShapeDtypeStruct((t, c3), u.dtype), jax.ShapeDtypeStruct((CONV_WIDTH, c3), F32)],
        compiler_params=_cp(("parallel",)),
    )(u, w, do)


@functools.partial(jax.custom_vjp, nondiff_argnums=(2,))
def conv_silu(u, w, tag):
    return _conv_fwd_call(u, w, "conv_" + tag)


def _conv_silu_f(u, w, tag):
    return _conv_fwd_call(u, w, "conv_" + tag), (u, w)


def _conv_silu_b(tag, res, do):
    return tuple(_conv_bwd_call(*res, do, "conv_" + tag + "_bwd"))


conv_silu.defvjp(_conv_silu_f, _conv_silu_b)


BNN = (((2,), (1,)), ((0,), (0,)))
BNT = (((2,), (2,)), ((0,), (0,)))
BTN = (((1,), (1,)), ((0,), (0,)))


def _xdot(a, b, dn=BNN):
    return lax.dot_general(a, b, dn, precision=lax.Precision.HIGHEST, preferred_element_type=F32)


def _bf16_dot(a, b, dn):
    return lax.dot_general(a.astype(BF16), b.astype(BF16), dn, preferred_element_type=F32)


def _dot3(a, b, dn):
    ah, bh = a.astype(BF16), b.astype(BF16)
    al, bl = a - ah.astype(F32), b - bh.astype(F32)
    return _bf16_dot(ah, bh, dn) + (_bf16_dot(ah, bl, dn) + _bf16_dot(al, bh, dn))


def _transposed(dn, a, b, g):
    if dn == BNN:
        return (g, b, BNT), (a, g, BTN)
    if dn == BNT:
        return (g, b, BNN), (g, a, BTN)
    return (b, g, BNT), (a, g, BNN)


@functools.partial(jax.custom_vjp, nondiff_argnums=(2,))
def _hdot(a, b, dn=BNN):
    return _dot3(a, b, dn)


def _hdot_f(a, b, dn):
    return _dot3(a, b, dn), (a, b)


def _hdot_b(dn, res, g):
    da, db = _transposed(dn, *res, g)
    return _dot3(*da), _dot3(*db)


_hdot.defvjp(_hdot_f, _hdot_b)


@functools.partial(jax.custom_vjp, nondiff_argnums=(2,))
def _bdot(a, b, dn=BNN):
    return _bf16_dot(a, b, dn)


def _bdot_f(a, b, dn):
    return _bf16_dot(a, b, dn), (a, b)


def _bdot_b(dn, res, g):
    da, db = _transposed(dn, *res, g)
    return _bf16_dot(*da), _bf16_dot(*db)


_bdot.defvjp(_bdot_f, _bdot_b)


GDN_HEADS_PER_STEP = 8
GDN_HEADS_PER_STEP_BWD = 4


def _gdn_chunk(q, k, v, z, bl, al, a_log, dtb, gn, s):
    b, c = q.shape[0], q.shape[1]
    ri = lax.broadcasted_iota(jnp.int32, (c, c), 0)
    ci = lax.broadcasted_iota(jnp.int32, (c, c), 1)
    lower = (ri >= ci)[None]
    strict = (ri > ci)[None]
    low_incl = jnp.broadcast_to((ri >= ci).astype(F32), (b, c, c))
    up_incl = jnp.broadcast_to((ri <= ci).astype(F32), (b, c, c))
    eye = (ri == ci).astype(F32)[None]

    q = q * lax.rsqrt(jnp.sum(q * q, axis=-1, keepdims=True) + EPS) * (GDN_DK ** -0.5)
    k = k * lax.rsqrt(jnp.sum(k * k, axis=-1, keepdims=True) + EPS)
    beta = _sigmoid(bl)
    g = -jnp.exp(a_log) * _softplus(al + dtb)
    g_w = jnp.broadcast_to(g, (b, c, LANE))
    gc = _xdot(low_incl, g_w)
    gr = _xdot(g_w[:, :, :c], up_incl, BTN)
    diff = gc[:, :, :c] - gr
    decay = jnp.where(lower, jnp.exp(jnp.where(lower, diff, 0.0)), 0.0)
    kb = k * beta
    lmat = jnp.where(strict, _bdot(kb, k, BNT) * decay, 0.0)
    inv = eye - lmat
    pw = lmat
    for _ in range(int(math.log2(c)) - 1):
        pw = _hdot(pw, pw)
        inv = _hdot(inv, eye + pw)
    eg = jnp.exp(gc)
    u = _hdot(inv, v * beta)
    w = _hdot(inv, kb * eg)
    attn = jnp.where(lower, _bdot(q, k, BNT) * decay, 0.0)
    v_new = u - _bdot(w, s)
    o = _bdot(q * eg, s) + _bdot(attn, v_new)
    g_last = jnp.sum(g_w, axis=1, keepdims=True)
    k_dec = k * jnp.exp(g_last - gc)
    s_new = s * jnp.exp(g_last) + _bdot(k_dec, v_new, BTN)
    on = o * lax.rsqrt(jnp.mean(o * o, axis=-1, keepdims=True) + EPS) * gn
    return on * (z * _sigmoid(z)), s_new


def _gdn_specs(n_chunks, hb, rev):
    c = CHUNK
    nn = (lambda n: n_chunks - 1 - n) if rev else (lambda n: n)
    plane = lambda pidx: pl.BlockSpec((None, c, hb * GDN_DK), lambda hg, n: (pidx, nn(n), hg))
    col = pl.BlockSpec((hb, c, 1), lambda hg, n: (hg, nn(n), 0))
    scal = pl.BlockSpec((hb, 1, 1), lambda hg, n: (hg, 0, 0))
    zspec = pl.BlockSpec((c, hb * GDN_DV), lambda hg, n: (nn(n), hg))
    gnspec = pl.BlockSpec((1, GDN_DV), lambda hg, n: (0, 0))
    sspec = pl.BlockSpec((hb, None, GDN_DK, GDN_DV), lambda hg, n: (hg, nn(n), 0, 0))
    return plane, col, scal, zspec, gnspec, sspec


def _heads_per_step(want):
    return math.gcd(want, GDN_HEADS)


def _heads(ref, hb):
    return jnp.stack([ref[:, j * GDN_DK:(j + 1) * GDN_DK] for j in range(hb)])


def _gdn_fwd_call(qkv, z, bl, al, a_log, dtb, gn, name):
    t = z.shape[0]
    h_n = GDN_HEADS
    hb = _heads_per_step(GDN_HEADS_PER_STEP)
    n_chunks = t // CHUNK
    plane, col, scal, zspec, gnspec, sspec = _gdn_specs(n_chunks, hb, False)

    def body(q_ref, k_ref, v_ref, z_ref, bl_ref, al_ref, a_ref, dtb_ref, gn_ref, o_ref, sall_ref, s_scr):
        n = pl.program_id(1)

        @pl.when(n == 0)
        def _():
            s_scr[...] = jnp.zeros_like(s_scr)

        s = s_scr[...]
        sall_ref[...] = s
        o, s_new = _gdn_chunk(_heads(q_ref, hb), _heads(k_ref, hb), _heads(v_ref, hb),
                              _heads(z_ref, hb).astype(F32),
                              bl_ref[...], al_ref[...], a_ref[...], dtb_ref[...], gn_ref[...], s)
        for j in range(hb):
            o_ref[:, j * GDN_DV:(j + 1) * GDN_DV] = o[j].astype(o_ref.dtype)
        s_scr[...] = s_new

    return pl.pallas_call(
        body, name=name, grid=(h_n // hb, n_chunks),
        in_specs=[plane(0), plane(1), plane(2), zspec, col, col, scal, scal, gnspec],
        out_specs=[zspec, sspec],
        out_shape=[jax.ShapeDtypeStruct((t, h_n * GDN_DV), BF16),
                   jax.ShapeDtypeStruct((h_n, n_chunks, GDN_DK, GDN_DV), F32)],
        scratch_shapes=[pltpu.VMEM((hb, GDN_DK, GDN_DV), F32)],
        compiler_params=_cp(("parallel", "arbitrary")),
    )(qkv, qkv, qkv, z, bl, al, a_log, dtb, gn)


def _gdn_bwd_call(qkv, z, bl, al, a_log, dtb, gn, sall, do, name):
    t = z.shape[0]
    h_n = GDN_HEADS
    hb = _heads_per_step(GDN_HEADS_PER_STEP_BWD)
    n_chunks = t // CHUNK
    c = CHUNK
    plane, col, scal, zspec, gnspec, sspec = _gdn_specs(n_chunks, hb, True)
    dplanes = pl.BlockSpec((3, c, hb * GDN_DK), lambda hg, n: (0, n_chunks - 1 - n, hg))
    gnh = pl.BlockSpec((None, 1, GDN_DV), lambda hg, n: (hg, 0, 0))

    def body(q_ref, k_ref, v_ref, z_ref, bl_ref, al_ref, a_ref, dtb_ref, gn_ref, s_ref, do_ref,
             dqkv_ref, dz_ref, dbl_ref, dal_ref, da_ref, ddtb_ref, dgn_ref, ds_scr):
        n = pl.program_id(1)

        @pl.when(n == 0)
        def _():
            ds_scr[...] = jnp.zeros_like(ds_scr)
            da_ref[...] = jnp.zeros_like(da_ref)
            ddtb_ref[...] = jnp.zeros_like(ddtb_ref)
            dgn_ref[...] = jnp.zeros_like(dgn_ref)

        _, vjp = jax.vjp(_gdn_chunk, _heads(q_ref, hb), _heads(k_ref, hb), _heads(v_ref, hb),
                              _heads(z_ref, hb).astype(F32),
                         bl_ref[...], al_ref[...], a_ref[...], dtb_ref[...], gn_ref[...], s_ref[...])
        dq, dk, dv, dz, dbl, dal, da, ddtb, dgn, ds = vjp((_heads(do_ref, hb).astype(F32), ds_scr[...]))
        for j in range(hb):
            hs = slice(j * GDN_DK, (j + 1) * GDN_DK)
            dqkv_ref[0, :, hs] = dq[j]
            dqkv_ref[1, :, hs] = dk[j]
            dqkv_ref[2, :, hs] = dv[j]
            dz_ref[:, hs] = dz[j].astype(dz_ref.dtype)
        dbl_ref[...] = dbl
        dal_ref[...] = dal
        da_ref[...] += da
        ddtb_ref[...] += ddtb
        dgn_ref[...] += dgn
        ds_scr[...] = ds

    return pl.pallas_call(
        body, name=name, grid=(h_n // hb, n_chunks),
        in_specs=[plane(0), plane(1), plane(2), zspec, col, col, scal, scal, gnspec, sspec, zspec],
        out_specs=[dplanes, zspec, col, col, scal, scal, gnh],
        out_shape=[jax.ShapeDtypeStruct((3, t, h_n * GDN_DK), F32),
                   jax.ShapeDtypeStruct((t, h_n * GDN_DV), z.dtype),
                   jax.ShapeDtypeStruct((h_n, t, 1), F32), jax.ShapeDtypeStruct((h_n, t, 1), F32),
                   jax.ShapeDtypeStruct((h_n, 1, 1), F32), jax.ShapeDtypeStruct((h_n, 1, 1), F32),
                   jax.ShapeDtypeStruct((h_n // hb, 1, GDN_DV), F32)],
        scratch_shapes=[pltpu.VMEM((hb, GDN_DK, GDN_DV), F32)],
        compiler_params=_cp(("parallel", "arbitrary")),
    )(qkv, qkv, qkv, z, bl, al, a_log, dtb, gn, sall, do)


@functools.partial(jax.custom_vjp, nondiff_argnums=(7,))
def gdn(qkv, z, bl, al, a_log, dtb, gn, tag):
    return _gdn_fwd_call(qkv, z, bl, al, a_log, dtb, gn, "gdn_" + tag)[0]


def _gdn_f(qkv, z, bl, al, a_log, dtb, gn, tag):
    o, sall = _gdn_fwd_call(qkv, z, bl, al, a_log, dtb, gn, "gdn_" + tag)
    return o, (qkv, z, bl, al, a_log, dtb, gn, sall)


def _gdn_b(tag, res, do):
    dqkv, dz, dbl, dal, da, ddtb, dgn_h = _gdn_bwd_call(*res, do, "gdn_" + tag + "_bwd")
    return dqkv, dz, dbl, dal, da, ddtb, jnp.sum(dgn_h, axis=0)


gdn.defvjp(_gdn_f, _gdn_b)


ADAMW_BLOCK = 256 * 1024


def adamw(w, m, v, *, parts, name):
    n_layers = len(parts)
    n_parts, r, c = parts[0].shape
    assert w.shape == (n_layers * r, c), (w.shape, parts[0].shape)
    tr, tc = r, c
    for cand in (512, 256, 128, 64, 32, 16, 8):
        if r % cand == 0 and cand * c <= ADAMW_BLOCK:
            tr = cand
            break
    if tr * c > 2 * ADAMW_BLOCK:
        assert n_layers == 1
        tc = _tile(c, max(LANE, 2 * ADAMW_BLOCK // r // LANE * LANE))
    nb = (r // tr) * (c // tc)
    cols = c // tc
    blk = pl.BlockSpec((tr, tc), lambda l, i: (l * (r // tr) + i // cols, i % cols))
    bc1 = 1.0 - ADAM_B1 ** ADAM_STEP
    bc2 = 1.0 - ADAM_B2 ** ADAM_STEP

    def part_spec(li):
        def index(l, i):
            j = jnp.where(l == li, i, jnp.where(l < li, 0, nb - 1))
            return 0, j // cols, j % cols
        return pl.BlockSpec((n_parts, tr, tc), index)

    def body(*refs):
        w_ref, p_refs = refs[0], refs[1:1 + n_layers]
        m_ref, v_ref, g_ref, d_ref, mo_ref, vo_ref = refs[1 + n_layers:]
        for li in range(n_layers):
            @pl.when(pl.program_id(0) == li)
            def _(p_ref=p_refs[li]):
                g = p_ref[0].astype(F32)
                for i in range(1, n_parts):
                    g = g + p_ref[i].astype(F32)
                m2 = ADAM_B1 * m_ref[...] + (1.0 - ADAM_B1) * g
                v2 = ADAM_B2 * v_ref[...] + (1.0 - ADAM_B2) * (g * g)
                g_ref[...] = g
                mo_ref[...] = m2
                vo_ref[...] = v2
                d_ref[...] = -ADAM_LR * ((m2 / bc1) / (jnp.sqrt(v2 / bc2) + ADAM_EPS)
                                         + ADAM_WD * w_ref[...])

    return pl.pallas_call(
        body, name=name, grid=(n_layers, nb),
        in_specs=[blk] + [part_spec(li) for li in range(n_layers)] + [blk, blk],
        out_specs=[blk] * 4, out_shape=[jax.ShapeDtypeStruct(w.shape, F32)] * 4,
        compiler_params=_cp(("arbitrary", "arbitrary")),
    )(w, *parts, m, v)


_HBM = pl.BlockSpec(memory_space=pltpu.HBM)
_SEM = pl.BlockSpec(memory_space=pltpu.SEMAPHORE)
_EFFECT = pltpu.SideEffectType.DATAFLOW_SIDE_EFFECTING


def _peer(x, y, c, d):
    px = 1 - x if d & 4 else x
    py = 1 - y if d & 2 else y
    pc = 1 - c if d & 1 else c
    return (px, py, pc), 4 * px + 2 * py + pc


ALL_PEERS = (1, 2, 3, 4, 5, 6, 7)
SIBLING = 1
SAME_CORE_REMOTE = (2, 4, 6)


def copy_start(arrays, mode, carry, name):
    n = len(arrays)
    if mode == "forward":
        lands = []
    else:
        lands = [lax.empty(a.shape if mode == "scatter" else (N_DEV,) + a.shape, a.dtype) for a in arrays]
    n_in = n + len(lands) + 1

    def body(*refs):
        srcs = refs[:n]
        dsts = refs[n:2 * n] if lands else srcs
        sems = refs[n_in:n_in + 2 * n]
        x, y, c = (lax.axis_index(a) for a in AXES)
        me = 4 * x + 2 * y + c
        for k in range(n):
            if mode == "forward":
                sibling, _ = _peer(x, y, c, SIBLING)
                copies = [(srcs[k].at[_peer(x, y, c, d)[1]], dsts[k].at[_peer(x, y, c, d)[1]], sibling)
                          for d in SAME_CORE_REMOTE]
            elif mode == "gather":
                copies = [(srcs[k], dsts[k].at[me], _peer(x, y, c, d)[0]) for d in (SIBLING,) + SAME_CORE_REMOTE]
            else:
                copies = [(srcs[k].at[_peer(x, y, c, d)[1]], dsts[k].at[me], _peer(x, y, c, d)[0])
                          for d in ALL_PEERS]
            for src, dst, peer in copies:
                pltpu.make_async_remote_copy(src_ref=src, dst_ref=dst, send_sem=sems[2 * k],
                                             recv_sem=sems[2 * k + 1], device_id=peer,
                                             device_id_type=pl.DeviceIdType.MESH).start()

    operands = list(arrays) + lands + [carry]
    outs = pl.pallas_call(
        body, name=name,
        out_shape=tuple([pltpu.SemaphoreType.DMA(())] * (2 * n)
                        + [pltpu.HBM(a.shape, a.dtype) for a in operands]),
        in_specs=[_HBM] * n_in,
        out_specs=tuple([_SEM] * (2 * n) + [_HBM] * n_in),
        input_output_aliases={i: 2 * n + i for i in range(n_in)},
        compiler_params=pltpu.CompilerParams(has_side_effects=_EFFECT),
    )(*[pltpu.with_memory_space_constraint(a, pltpu.HBM) for a in operands])
    sems, thru = outs[:2 * n], outs[2 * n:-1]
    handles = [(sems[2 * k], sems[2 * k + 1], thru[k] if lands else None, thru[n + k] if lands else thru[k])
               for k in range(n)]
    return outs[-1], handles


def copy_wait(handles, n_blocks, after, name):
    n = len(handles)
    sems = [s for h in handles for s in h[:2]]
    srcs = [h[2] for h in handles if h[2] is not None]
    lands = [h[3] for h in handles]
    ns = len(srcs)

    def body(*refs):
        dsts = refs[ns:ns + n]
        sem_refs = refs[ns + n:ns + 3 * n]
        x, y, c = (lax.axis_index(a) for a in AXES)
        for k in range(n):
            blocks = dsts[k].at[pl.ds(0, n_blocks)]
            pltpu.make_async_remote_copy(
                src_ref=blocks, dst_ref=blocks, send_sem=sem_refs[2 * k], recv_sem=sem_refs[2 * k + 1],
                device_id=(x, y, c), device_id_type=pl.DeviceIdType.MESH).wait()

    outs = pl.pallas_call(
        body, name=name,
        out_shape=tuple([pltpu.HBM(a.shape, a.dtype) for a in srcs + lands]),
        in_specs=[_HBM] * (ns + n) + [_SEM] * (2 * n) + [pl.BlockSpec(memory_space=pl.ANY)],
        out_specs=tuple([_HBM] * (ns + n)),
        input_output_aliases={i: i for i in range(ns + n)},
        compiler_params=pltpu.CompilerParams(has_side_effects=_EFFECT),
    )(*srcs, *lands, *sems, after)
    return (list(outs[:ns]) if ns else [None] * n), list(outs[ns:])


def exchange(arrays, modes, name):
    n = len(arrays)
    hbm = pl.BlockSpec(memory_space=pltpu.HBM)
    out_shape = [jax.ShapeDtypeStruct(a.shape if md == "scatter" else (N_DEV,) + a.shape, a.dtype)
                 for a, md in zip(arrays, modes)]

    def body(*refs):
        ins, outs = refs[:n], refs[n:2 * n]
        send_sems, recv_sems, local_sems = refs[2 * n:]
        x, y, c = (lax.axis_index(a) for a in AXES)
        me = 4 * x + 2 * y + c

        def src(k, p):
            return ins[k].at[p] if modes[k] == "scatter" else ins[k]

        local = [pltpu.make_async_copy(src(k, me), outs[k].at[me], local_sems.at[k]) for k in range(n)]
        for cp in local:
            cp.start()
        started = []
        for d in range(1, N_DEV):
            px = 1 - x if d & 4 else x
            py = 1 - y if d & 2 else y
            pc = 1 - c if d & 1 else c
            pid = 4 * px + 2 * py + pc
            for k in range(n):
                pltpu.make_async_remote_copy(
                    src_ref=src(k, pid), dst_ref=outs[k].at[me],
                    send_sem=send_sems.at[k, d - 1], recv_sem=recv_sems.at[k, d - 1],
                    device_id=(px, py, pc), device_id_type=pl.DeviceIdType.MESH).start()
                started.append((k, d, pid, (px, py, pc)))
        for k, d, pid, peer in started:
            pltpu.make_async_remote_copy(
                src_ref=src(k, pid), dst_ref=outs[k].at[pid],
                send_sem=send_sems.at[k, d - 1], recv_sem=recv_sems.at[k, d - 1],
                device_id=peer, device_id_type=pl.DeviceIdType.MESH).wait()
        for cp in local:
            cp.wait()

    outs = pl.pallas_call(
        body, name=name, in_specs=[hbm] * n, out_specs=[hbm] * n, out_shape=out_shape,
        scratch_shapes=[pltpu.SemaphoreType.DMA((n, N_DEV - 1)), pltpu.SemaphoreType.DMA((n, N_DEV - 1)),
                        pltpu.SemaphoreType.DMA((n,))],
        compiler_params=pltpu.CompilerParams(has_side_effects=True),
    )(*arrays)
    return list(outs)


BIG = ("w_in", "w_uq", "w_ukv", "w_o_mla", "w_o_gdn", "w_o", "w_gate_up", "w_down")
ROW_SHARDED = ("w_o", "w_down")
SMALL = ("b_ada", "norm_mix", "norm_ffn", "q_a_norm", "kv_a_norm", "A_log", "dt_bias", "gdn_norm",
         "final_norm")
WEIGHTS = ("w_ada", "b_ada", "norm_mix", "norm_ffn", "w_in", "q_a_norm", "kv_a_norm", "w_uq", "w_ukv",
           "w_o_mla", "conv_w", "A_log", "dt_bias", "gdn_norm", "w_o_gdn", "w_o", "w_gate_up", "w_down",
           "final_norm")


def _unslot(g):
    return g.transpose(1, 0, 2).reshape(g.shape[1], -1)


def _cols(g):
    return g if g.shape[-1] % LANE == 0 else _unslot(g)


def _stack_rows(g):
    return g.reshape(-1, g.shape[-1])


def _rope(xv, cos, sin):
    x1, x2 = jnp.split(xv, 2, axis=-1)
    return jnp.concatenate([x1 * cos - x2 * sin, x2 * cos + x1 * sin], axis=-1)


MIX_WEIGHTS = ("w_uq", "w_ukv", "w_o_mla", "w_o_gdn", "w_o")
FFN_WEIGHTS = ("w_gate_up", "w_down")


def _stage_in(x, mod, nm, w_in_ts, tg):
    d = x.shape[1]
    hg = GDN_HEADS
    wt = _stack_rows(w_in_ts)
    o1 = Q_LORA + KV_LORA + QK_ROPE
    o2 = o1 + 2 * hg * GDN_DK + hg * GDN_DV
    o3 = o2 + hg * GDN_DV
    o4 = o3 + 2 * hg
    h = ada_norm(x, nm, mod[:, d:2 * d], mod[:, :d], "mix" + tg)
    return (mmt(h, wt[:o1], "in_a" + tg, BF16), mmt(h, wt[o1:o2], "in_qkv" + tg, BF16),
            mmt(h, wt[o2:o3], "in_z" + tg, BF16), mmt(h, wt[o3:o4], "in_ba" + tg, F32),
            mmt(h, wt[o4:o4 + 2 * d], "in_g" + tg, BF16))


def _stage_mix(x, mod, seg_a, qkv, z, ba, gl, w_uq_s, w_ukv_s, w_o_mla_s, w_o_gdn_s, w_o_s, conv_s,
               qan, kvan, a_log, dtb, gn, cos, sin, tg):
    t, d = x.shape
    hq, hg = MLA_HEADS, GDN_HEADS
    w_uq = _unslot(w_uq_s).reshape(Q_LORA, hq, QK_NOPE + QK_ROPE)
    w_uq = jnp.concatenate([w_uq[:, :, :QK_NOPE].reshape(Q_LORA, hq * QK_NOPE),
                            w_uq[:, :, QK_NOPE:].reshape(Q_LORA, hq * QK_ROPE)], axis=1)
    c_q = seg_a[:, :Q_LORA]
    c_kv = seg_a[:, Q_LORA:Q_LORA + KV_LORA]
    k_pe = seg_a[:, Q_LORA + KV_LORA:Q_LORA + KV_LORA + QK_ROPE]
    qf = mm(rms_norm(c_q, qan, "qa" + tg), w_uq, "uq" + tg, BF16)
    kvf = mm(rms_norm(c_kv, kvan, "kva" + tg), _cols(w_ukv_s), "ukv" + tg, BF16)
    qn = qf[:, :hq * QK_NOPE]
    q_pe = qf[:, hq * QK_NOPE:].astype(F32).reshape(t, hq, QK_ROPE)
    qr = _rope(q_pe, cos[:, None, :], sin[:, None, :]).transpose(1, 0, 2).astype(BF16)
    kr = _rope(k_pe.astype(F32), cos, sin).astype(BF16)
    y_a = mm(attention(qn, qr, kvf, kr, tg), _cols(w_o_mla_s), "o_mla" + tg, BF16)
    conv_w = conv_s.transpose(1, 0, 2).reshape(CONV_WIDTH, -1)
    qkv_c = conv_silu(qkv, conv_w, tg)
    bl = ba[:, :hg].T[:, :, None]
    al = ba[:, hg:2 * hg].T[:, :, None]
    o_gdn = gdn(qkv_c, z, bl, al, a_log.reshape(hg, 1, 1), dtb.reshape(hg, 1, 1), gn, tg)
    y_b = mm(o_gdn, _cols(w_o_gdn_s), "o_gdn" + tg, BF16)
    mix = mm(gate_mix(gl, y_a, y_b, tg), _stack_rows(w_o_s), "w_o" + tg, F32)
    return resid(x, mod[:, 2 * d:3 * d], mix, "mix" + tg)


def _stage_ffn(x, mod, nf, w_gu_s, w_down_s, tg):
    d = x.shape[1]
    h = ada_norm(x, nf, mod[:, 4 * d:5 * d], mod[:, 3 * d:4 * d], "ffn" + tg)
    gu = mm(h, _cols(w_gu_s), "gu" + tg, BF16)
    dn = mm(swiglu(gu, tg), _stack_rows(w_down_s), "down" + tg, F32)
    return resid(x, mod[:, 5 * d:6 * d], dn, "ffn" + tg)


def _flat_row(arrs):
    v = jnp.concatenate([a.reshape(-1) for a in arrs])
    return jnp.pad(v, (0, _pad_lanes(v.shape[0]) - v.shape[0]))[None, :]


def kernel(x, c, positions, w_ada, b_ada, norm_mix, norm_ffn, w_in, q_a_norm, kv_a_norm, w_uq, w_ukv, w_o_mla, conv_w, A_log, dt_bias, gdn_norm, w_o_gdn, w_o, w_gate_up, w_down, final_norm, loss_target, m_w_ada, m_b_ada, m_norm_mix, m_norm_ffn, m_w_in, m_q_a_norm, m_kv_a_norm, m_w_uq, m_w_ukv, m_w_o_mla, m_conv_w, m_A_log, m_dt_bias, m_gdn_norm, m_w_o_gdn, m_w_o, m_w_gate_up, m_w_down, m_final_norm, v_w_ada, v_b_ada, v_norm_mix, v_norm_ffn, v_w_in, v_q_a_norm, v_kv_a_norm, v_w_uq, v_w_ukv, v_w_o_mla, v_conv_w, v_A_log, v_dt_bias, v_gdn_norm, v_w_o_gdn, v_w_o, v_w_gate_up, v_w_down, v_final_norm):
    given = dict(locals())
    t, d = x.shape[1], x.shape[2]
    n_ada = w_ada.shape[2]
    me = 4 * lax.axis_index("x") + 2 * lax.axis_index("y") + lax.axis_index("c")

    def with_own(land, own):
        return lax.dynamic_update_slice(land, own[None], (me,) + (0,) * own.ndim)

    got = exchange([c, conv_w], ["gather", "gather"], "gather_small")
    c_all, conv_g = got[0].reshape(N_DEV, d), got[1]
    c_rows = jnp.pad(c_all, ((0, 16 - N_DEV), (0, 0)))
    mod_cols = jnp.stack([_mm(c_rows, w_ada[l], "nn", F32, "ada_mod%d" % l, a_act="silu")[:N_DEV]
                          for l in range(DEPTH)], axis=1)
    mod_mine = exchange([mod_cols], ["scatter"], "scatter_mod")[0]
    mods = mod_mine.transpose(1, 0, 2).reshape(DEPTH, N_DEV * n_ada) + b_ada

    groups = [[(n, l) for n in names] for l in range(DEPTH) for names in (("w_in",), MIX_WEIGHTS, FFN_WEIGHTS)]
    gtags = [s + str(l) for l in range(DEPTH) for s in ("in", "mix", "ffn")]
    keys = [k for ks in groups for k in ks]
    shard = lambda n, l: (jnp.swapaxes(given[n][l], 0, 1) if n == "w_in" else given[n][l]).astype(BF16)
    mods, handles = copy_start([shard(n, l) for n, l in keys], "gather", mods, "gather_start")
    handles = dict(zip(keys, handles))
    own, relayed = {}, {}

    def relay(gi, carry):
        ks = groups[gi]
        srcs, lands = copy_wait([handles[k] for k in ks], 1 + len(SAME_CORE_REMOTE), carry,
                                "wait_ici_" + gtags[gi])
        own.update(zip(ks, srcs))
        carry, hs = copy_start(lands, "forward", carry, "relay_" + gtags[gi])
        relayed.update(zip(ks, hs))
        return carry

    def landed(gi, after):
        ks = groups[gi]
        _, lands = copy_wait([relayed[k] for k in ks], len(SAME_CORE_REMOTE), after, "wait_" + gtags[gi])
        return [with_own(land, own[k]) for k, land in zip(ks, lands)]

    inv_freq = 1.0 / (ROPE_THETA ** (jnp.arange(0, QK_ROPE, 2, dtype=F32) / QK_ROPE))
    ang = positions[0].astype(F32)[:, None] * inv_freq
    cos, sin = jnp.cos(ang), jnp.sin(ang)
    relay_before = {0: [0], 1: [1], 2: [2, 3], 3: [4], 4: [5], 5: []}

    def weights_for(stage, carry):
        for gi in relay_before[stage]:
            carry = relay(gi, carry)
        return carry, landed(stage, carry)

    xl = x[0]
    vjps = []
    for l in range(DEPTH):
        tg = str(l)
        mod = mods[l:l + 1]
        xl, (w_in_s,) = weights_for(3 * l, xl)
        seg, vjp_in = jax.vjp(lambda *a, tg=tg: _stage_in(*a, tg), xl, mod, norm_mix[l:l + 1], w_in_s)
        seg0, w_mix = weights_for(3 * l + 1, seg[0])
        seg = (seg0,) + tuple(seg[1:])
        xm, vjp_mix = jax.vjp(lambda *a, tg=tg: _stage_mix(*a, cos, sin, tg), xl, mod, *seg, *w_mix,
                              conv_g[:, l], q_a_norm[l:l + 1], kv_a_norm[l:l + 1], A_log[l], dt_bias[l],
                              gdn_norm[l:l + 1])
        xm, w_ffn = weights_for(3 * l + 2, xm)
        xl, vjp_ffn = jax.vjp(lambda *a, tg=tg: _stage_ffn(*a, tg), xm, mod, norm_ffn[l:l + 1], *w_ffn)
        vjps.append((vjp_in, vjp_mix, vjp_ffn))

    loss_t, g, dfn = loss_head(xl, final_norm[None, :], loss_target[0])
    loss = lax.psum(loss_t[0, 0], AXES)
    dsmall = {n: [None] * DEPTH for n in SMALL + ("conv_w",)}
    dmods = [None] * DEPTH
    sent = {}

    def send(ks, grads, carry, name):
        carry, hs = copy_start(list(grads), "scatter", carry, name)
        sent.update(zip(ks, hs))
        return carry

    for l in reversed(range(DEPTH)):
        tg = str(l)
        vjp_in, vjp_mix, vjp_ffn = vjps[l]
        dxm, dmod_f, dsmall["norm_ffn"][l], *dw = vjp_ffn(g)
        dxm = send([(n, l) for n in FFN_WEIGHTS], dw, dxm, "scatter_ffn" + tg)
        dx_m, dmod_m, *rest = vjp_mix(dxm)
        dseg, dw, rest = rest[:5], rest[5:5 + len(MIX_WEIGHTS)], rest[5 + len(MIX_WEIGHTS):]
        dseg[0] = send([(n, l) for n in MIX_WEIGHTS], dw, dseg[0], "scatter_mix" + tg)
        for n, gr in zip(("conv_w", "q_a_norm", "kv_a_norm", "A_log", "dt_bias", "gdn_norm"), rest):
            dsmall[n][l] = gr
        dx_i, dmod_i, dsmall["norm_mix"][l], dw_in = vjp_in(tuple(dseg))
        g = dx_i + dx_m
        if l > 0:
            g = send([("w_in", l)], [dw_in], g, "scatter_in" + tg)
        dmods[l] = dmod_f + dmod_m + dmod_i
    dx = g
    dmods = jnp.concatenate(dmods, axis=0)
    dconv = jnp.stack(dsmall.pop("conv_w"), axis=1)
    dsmall = {n: jnp.concatenate(v, axis=0) if v[0].ndim == 2 else jnp.stack(v)
              for n, v in dsmall.items() if v[0] is not None}
    dsmall["b_ada"] = dmods
    dsmall["final_norm"] = dfn[0]

    dmod_cols = dmods.reshape(DEPTH, N_DEV, n_ada).transpose(1, 0, 2)
    conv_parts, dmod_all, small_parts = exchange(
        [dconv, dmod_cols, _flat_row([dsmall[n] for n in SMALL])], ["scatter", "scatter", "gather"],
        "exchange_small")
    dmod_all = send([("w_in", 0)], [dw_in], dmod_all, "scatter_in0")

    res = {}
    dm_rows = jnp.pad(dmod_all, ((0, 16 - N_DEV), (0, 0), (0, 0)))
    g_ada = [_mm(c_rows, dm_rows[:, l], "tn", F32, "ada_dw%d" % l, a_act="silu")[None] for l in range(DEPTH)]
    r2 = (DEPTH * d, n_ada)
    outs = adamw(w_ada.reshape(r2), m_w_ada.reshape(r2), v_w_ada.reshape(r2), parts=g_ada, name="adamw_w_ada")
    res["w_ada"] = [o.reshape(w_ada.shape) for o in outs]
    packed = SMALL + ("conv_w",)
    p_all = jnp.concatenate([small_parts, conv_parts.reshape(N_DEV, 1, -1)], axis=2)
    pack = lambda pre: jnp.concatenate([_flat_row([given[pre + n] for n in SMALL]),
                                        given[pre + "conv_w"].reshape(1, -1)], axis=1)
    outs = adamw(pack(""), pack("m_"), pack("v_"), parts=[p_all], name="adamw_small")
    done = [res["w_ada"][1], outs[1]]
    for group, gname in ((FFN_WEIGHTS, "ffn"), (MIX_WEIGHTS, "mix"), (("w_in",), "in")):
        ks = [(n, l) for l in reversed(range(DEPTH)) for n in group]
        after = sum(lax.slice(a, (0,) * a.ndim, (1,) * a.ndim).reshape(1, 1) for a in done)
        srcs, lands = copy_wait([sent[k] for k in ks], len(ALL_PEERS), after, "scatter_wait_" + gname)
        parts = {k: with_own(land, lax.dynamic_index_in_dim(src, me, 0, keepdims=False))
                 for k, src, land in zip(ks, srcs, lands)}
        for n in group:
            w = given[n]
            pl_ = [jnp.swapaxes(parts[(n, l)], 1, 2) if n == "w_in" else parts[(n, l)] for l in range(DEPTH)]
            r2 = (w.shape[0] * w.shape[1], w.shape[2])
            res[n] = [o.reshape(w.shape) for o in
                      adamw(w.reshape(r2), given["m_" + n].reshape(r2), given["v_" + n].reshape(r2),
                            parts=pl_, name="adamw_" + n)]
            done.append(res[n][1])
    off = 0
    for n in packed:
        if n == "conv_w":
            off = small_parts.shape[2]
        size = math.prod(given[n].shape)
        res[n] = [o[0, off:off + size].reshape(given[n].shape) for o in outs]
        off += size

    return (loss, dx[None]) + tuple(res[n][i] for i in range(4) for n in WEIGHTS)
```

```python
import functools
import math

import jax
import jax.numpy as jnp
from jax import lax
from jax.experimental import pallas as pl
from jax.experimental.pallas import tpu as pltpu

F32 = jnp.float32
BF16 = jnp.bfloat16

MLA_HEADS = 8
QK_NOPE = 128
QK_ROPE = 64
V_HEAD = 128
Q_LORA = 512
KV_LORA = 512
ROPE_THETA = 10000.0
GDN_HEADS = 8
GDN_DK = 128
GDN_DV = 128
CONV_WIDTH = 4
CHUNK = 64
DEPTH = 2
EPS = 1e-6
ADAM_LR = 0.001
ADAM_B1 = 0.9
ADAM_B2 = 0.999
ADAM_EPS = 1e-08
ADAM_WD = 0.01
ADAM_STEP = 10

N_DEV = 8
AXES = ("x", "y", "c")
LANE = 128
VMEM_LIMIT = 48 * 1024 * 1024
MM_VMEM_BUDGET = 36 * 1024 * 1024

NN = (((1,), (0,)), ((), ()))
NT = (((1,), (1,)), ((), ()))
TN = (((0,), (0,)), ((), ()))


def _cp(sem=None):
    return pltpu.CompilerParams(dimension_semantics=sem, vmem_limit_bytes=VMEM_LIMIT)


def _tile(n, cap):
    if n <= cap:
        return n
    for t in range(cap - cap % LANE, 0, -LANE):
        if n % t == 0:
            return t
    return n


def _rows(t, cap=256):
    return cap if t % cap == 0 else t


def _pad_lanes(n):
    return -(-n // LANE) * LANE


def _sigmoid(x):
    return 1.0 / (1.0 + jnp.exp(-x))


def _softplus(x):
    return jnp.maximum(x, 0.0) + jnp.log(1.0 + jnp.exp(-jnp.abs(x)))


def _tile_slot(n, cap):
    t = _tile(n, cap)
    return n if t < 256 < n <= 1536 else t


def _mm(a, b, dims, out_dtype, name, a_act=None, slots=False, resid=None):
    if dims == "nn":
        m, k = a.shape
        n = b.shape[-1] * (N_DEV if slots else 1)
    elif dims == "nt":
        m, k = a.shape
        n = b.shape[-2]
    else:
        k, m = a.shape
        n = b.shape[-1]
    tm = _tile(m, 1536)
    tn = _tile_slot(n // N_DEV, 512) if slots and dims != "nt" else _tile(n, 512 if resid else 1024)
    k_slot = k // N_DEV if slots and dims == "nt" else k

    def vmem_bytes(tk_):
        a_b, b_b = tm * tk_ * a.dtype.itemsize, tk_ * tn * b.dtype.itemsize
        casts = (tm * tk_ * 2 if a.dtype != BF16 else 0) + (tk_ * tn * 2 if b.dtype != BF16 else 0)
        return 2 * (a_b + b_b + tm * tn * jnp.dtype(out_dtype).itemsize) + 2 * tm * tn * 4 + casts

    tk = _tile_slot(k_slot, 1536) if slots and dims == "nt" else _tile(k, 2048)
    while vmem_bytes(tk) > MM_VMEM_BUDGET and tk % (2 * LANE) == 0:
        tk //= 2
    nk = k // tk
    per_n = (n // N_DEV) // tn if slots else 1
    per_k = (k // N_DEV) // tk if slots else 1
    if dims == "tn":
        a_spec = pl.BlockSpec((tk, tm), lambda i, j, kk: (kk, i))
    else:
        a_spec = pl.BlockSpec((tm, tk), lambda i, j, kk: (i, kk))
    if dims == "nt":
        if slots:
            b_spec = pl.BlockSpec((None, tn, tk), lambda i, j, kk: (kk // per_k, j, kk % per_k))
        else:
            b_spec = pl.BlockSpec((tn, tk), lambda i, j, kk: (j, kk))
    elif dims == "nn" and slots:
        b_spec = pl.BlockSpec((None, tk, tn), lambda i, j, kk: (j // per_n, kk, j % per_n))
    else:
        b_spec = pl.BlockSpec((tk, tn), lambda i, j, kk: (kk, j))
    if dims == "tn" and slots:
        out_spec = pl.BlockSpec((None, tm, tn), lambda i, j, kk: (j // per_n, i, j % per_n))
        out_shape = jax.ShapeDtypeStruct((N_DEV, m, n // N_DEV), out_dtype)
    else:
        out_spec = pl.BlockSpec((tm, tn), lambda i, j, kk: (i, j))
        out_shape = jax.ShapeDtypeStruct((m, n), out_dtype)
    dn = {"nn": NN, "nt": NT, "tn": TN}[dims]

    def product(a_ref, b_ref):
        av = a_ref[...]
        if a_act == "silu":
            av = av * _sigmoid(av)
        return lax.dot_general(av.astype(BF16), b_ref[...].astype(BF16), dn, preferred_element_type=F32)

    def finish(acc, rest):
        if resid is None:
            (o_ref,) = rest
            o_ref[...] = acc.astype(o_ref.dtype)
        else:
            x_ref, gt_ref, o_ref, p_ref = rest
            o_ref[...] = x_ref[...] + gt_ref[...] * acc
            p_ref[...] = acc.astype(p_ref.dtype)

    def body_one(a_ref, b_ref, *rest):
        finish(product(a_ref, b_ref), rest)

    def body_acc(a_ref, b_ref, *rest):
        acc_ref = rest[-1]
        kk = pl.program_id(2)

        @pl.when(kk == 0)
        def _():
            acc_ref[...] = jnp.zeros_like(acc_ref)

        acc_ref[...] += product(a_ref, b_ref)

        @pl.when(kk == nk - 1)
        def _():
            finish(acc_ref[...], rest[:-1])

    in_specs, operands = [a_spec, b_spec], [a, b]
    if resid is not None:
        in_specs += [out_spec, pl.BlockSpec((1, tn), lambda i, j, kk: (0, j))]
        operands += list(resid)
        out_spec, out_shape = [out_spec, out_spec], [out_shape, jax.ShapeDtypeStruct((m, n), BF16)]
    return pl.pallas_call(
        body_one if nk == 1 else body_acc, name=name, grid=(m // tm, n // tn, nk),
        in_specs=in_specs, out_specs=out_spec, out_shape=out_shape,
        scratch_shapes=[] if nk == 1 else [pltpu.VMEM((tm, tn), F32)],
        compiler_params=_cp(("parallel", "parallel", "arbitrary")),
    )(*operands)


@functools.partial(jax.custom_vjp, nondiff_argnums=(2, 3))
def mm(a, b, tag, out_dtype):
    return _mm(a, b, "nn", out_dtype, "mm_" + tag, slots=b.ndim == 3)


def _mm_f(a, b, tag, out_dtype):
    return mm(a, b, tag, out_dtype), (a, b)


def _mm_b(tag, out_dtype, res, g):
    a, b = res
    slots = b.ndim == 3
    da = _mm(g, b, "nt", a.dtype, "mm_" + tag + "_da", slots=slots)
    db = _mm(a, g, "tn", b.dtype, "mm_" + tag + "_db", slots=slots)
    return da, db


mm.defvjp(_mm_f, _mm_b)


def _norm_fwd_call(x, nw, sc, sh, name):
    t, d = x.shape
    tr = _rows(t)
    mod = sc is not None
    row = pl.BlockSpec((tr, d), lambda i: (i, 0))
    vec = pl.BlockSpec((1, d), lambda i: (0, 0))

    def body(*refs):
        if mod:
            x_ref, nw_ref, sc_ref, sh_ref, o_ref = refs
        else:
            x_ref, nw_ref, o_ref = refs
        xv = x_ref[...].astype(F32)
        r = lax.rsqrt(jnp.mean(xv * xv, axis=-1, keepdims=True) + EPS)
        y = (xv * r) * nw_ref[...]
        if mod:
            y = y * (1.0 + sc_ref[...]) + sh_ref[...]
        o_ref[...] = y.astype(o_ref.dtype)

    args = (x, nw, sc, sh) if mod else (x, nw)
    return pl.pallas_call(
        body, name=name, grid=(t // tr,),
        in_specs=[row] + [vec] * (len(args) - 1), out_specs=row,
        out_shape=jax.ShapeDtypeStruct((t, d), BF16),
        compiler_params=_cp(("parallel",)),
    )(*args)


def _norm_bwd_call(x, nw, sc, dh, name):
    t, d = x.shape
    tr = _rows(t)
    mod = sc is not None
    row = pl.BlockSpec((tr, d), lambda i: (i, 0))
    vec = pl.BlockSpec((1, d), lambda i: (0, 0))

    def body(*refs):
        if mod:
            x_ref, nw_ref, sc_ref, dh_ref, dx_ref, dnw_ref, dsc_ref, dsh_ref = refs
        else:
            x_ref, nw_ref, dh_ref, dx_ref, dnw_ref = refs
        i = pl.program_id(0)
        xv = x_ref[...].astype(F32)
        dh = dh_ref[...].astype(F32)
        r = lax.rsqrt(jnp.mean(xv * xv, axis=-1, keepdims=True) + EPS)
        y = xv * r
        a = nw_ref[...] * (1.0 + sc_ref[...]) if mod else nw_ref[...]
        dy = dh * a
        dx_ref[...] = (r * (dy - y * jnp.mean(dy * y, axis=-1, keepdims=True))).astype(dx_ref.dtype)
        da = jnp.sum(dh * y, axis=0, keepdims=True)

        @pl.when(i == 0)
        def _():
            dnw_ref[...] = jnp.zeros_like(dnw_ref)
            if mod:
                dsc_ref[...] = jnp.zeros_like(dsc_ref)
                dsh_ref[...] = jnp.zeros_like(dsh_ref)

        if mod:
            dnw_ref[...] += da * (1.0 + sc_ref[...])
            dsc_ref[...] += da * nw_ref[...]
            dsh_ref[...] += jnp.sum(dh, axis=0, keepdims=True)
        else:
            dnw_ref[...] += da

    args = (x, nw, sc, dh) if mod else (x, nw, dh)
    n_vec = 3 if mod else 1
    return pl.pallas_call(
        body, name=name, grid=(t // tr,),
        in_specs=[row] + [vec] * (len(args) - 2) + [row],
        out_specs=[row] + [vec] * n_vec,
        out_shape=[jax.ShapeDtypeStruct((t, d), x.dtype)] + [jax.ShapeDtypeStruct((1, d), F32)] * n_vec,
        compiler_params=_cp(("arbitrary",)),
    )(*args)


@functools.partial(jax.custom_vjp, nondiff_argnums=(4,))
def ada_norm(x, nw, sc, sh, tag):
    return _norm_fwd_call(x, nw, sc, sh, "adanorm_" + tag)


def _ada_norm_f(x, nw, sc, sh, tag):
    return _norm_fwd_call(x, nw, sc, sh, "adanorm_" + tag), (x, nw, sc)


def _ada_norm_b(tag, res, dh):
    x, nw, sc = res
    dx, dnw, dsc, dsh = _norm_bwd_call(x, nw, sc, dh, "adanorm_" + tag + "_bwd")
    return dx, dnw, dsc, dsh


ada_norm.defvjp(_ada_norm_f, _ada_norm_b)


@functools.partial(jax.custom_vjp, nondiff_argnums=(2,))
def rms_norm(x, nw, tag):
    return _norm_fwd_call(x, nw, None, None, "rms_" + tag)


def _rms_norm_f(x, nw, tag):
    return _norm_fwd_call(x, nw, None, None, "rms_" + tag), (x, nw)


def _rms_norm_b(tag, res, dh):
    x, nw = res
    dx, dnw = _norm_bwd_call(x, nw, None, dh, "rms_" + tag + "_bwd")
    return dx, dnw


rms_norm.defvjp(_rms_norm_f, _rms_norm_b)


def _gate_mix_fwd_call(gl, ya, yb, name):
    t, d = ya.shape
    tr = _rows(t)
    row = pl.BlockSpec((tr, d), lambda i: (i, 0))

    def body(ga_ref, gb_ref, ya_ref, yb_ref, o_ref):
        o_ref[...] = (_sigmoid(ga_ref[...].astype(F32)) * ya_ref[...].astype(F32)
                      + _sigmoid(gb_ref[...].astype(F32)) * yb_ref[...].astype(F32)).astype(o_ref.dtype)

    return pl.pallas_call(
        body, name=name, grid=(t // tr,),
        in_specs=[row, pl.BlockSpec((tr, d), lambda i: (i, 1)), row, row], out_specs=row,
        out_shape=jax.ShapeDtypeStruct((t, d), BF16),
        compiler_params=_cp(("parallel",)),
    )(gl, gl, ya, yb)


def _gate_mix_bwd_call(gl, ya, yb, dm, name):
    t, d = ya.shape
    tr = _rows(t)
    row = pl.BlockSpec((tr, d), lambda i: (i, 0))
    wide = pl.BlockSpec((tr, 2 * d), lambda i: (i, 0))

    def body(gl_ref, ya_ref, yb_ref, dm_ref, dgl_ref, dya_ref, dyb_ref):
        dm = dm_ref[...].astype(F32)
        ga = _sigmoid(gl_ref[:, :d].astype(F32))
        gb = _sigmoid(gl_ref[:, d:].astype(F32))
        dya_ref[...] = (dm * ga).astype(dya_ref.dtype)
        dyb_ref[...] = (dm * gb).astype(dyb_ref.dtype)
        dgl_ref[:, :d] = (dm * ya_ref[...].astype(F32) * ga * (1.0 - ga)).astype(dgl_ref.dtype)
        dgl_ref[:, d:] = (dm * yb_ref[...].astype(F32) * gb * (1.0 - gb)).astype(dgl_ref.dtype)

    return pl.pallas_call(
        body, name=name, grid=(t // tr,),
        in_specs=[wide, row, row, row], out_specs=[wide, row, row],
        out_shape=[jax.ShapeDtypeStruct((t, 2 * d), gl.dtype), jax.ShapeDtypeStruct((t, d), ya.dtype),
                   jax.ShapeDtypeStruct((t, d), yb.dtype)],
        compiler_params=_cp(("parallel",)),
    )(gl, ya, yb, dm)


@functools.partial(jax.custom_vjp, nondiff_argnums=(3,))
def gate_mix(gl, ya, yb, tag):
    return _gate_mix_fwd_call(gl, ya, yb, "gatemix_" + tag)


def _gate_mix_f(gl, ya, yb, tag):
    return _gate_mix_fwd_call(gl, ya, yb, "gatemix_" + tag), (gl, ya, yb)


def _gate_mix_b(tag, res, dm):
    return tuple(_gate_mix_bwd_call(*res, dm, "gatemix_" + tag + "_bwd"))


gate_mix.defvjp(_gate_mix_f, _gate_mix_b)


def _resid_bwd_call(gt, p, g, name):
    t, d = p.shape
    tr = _rows(t)
    row = pl.BlockSpec((tr, d), lambda i: (i, 0))
    vec = pl.BlockSpec((1, d), lambda i: (0, 0))

    def body(gt_ref, p_ref, g_ref, dp_ref, dgt_ref):
        i = pl.program_id(0)
        g = g_ref[...]
        dp_ref[...] = (g * gt_ref[...]).astype(dp_ref.dtype)

        @pl.when(i == 0)
        def _():
            dgt_ref[...] = jnp.zeros_like(dgt_ref)

        dgt_ref[...] += jnp.sum(g * p_ref[...].astype(F32), axis=0, keepdims=True)

    return pl.pallas_call(
        body, name=name, grid=(t // tr,), in_specs=[vec, row, row], out_specs=[row, vec],
        out_shape=[jax.ShapeDtypeStruct((t, d), BF16), jax.ShapeDtypeStruct((1, d), F32)],
        compiler_params=_cp(("arbitrary",)),
    )(gt, p, g)


@functools.partial(jax.custom_vjp, nondiff_argnums=(4,))
def mm_resid(x, gt, a, b, tag):
    return _mm(a, b, "nn", F32, "mmres_" + tag, resid=(x, gt))[0]


def _mm_resid_f(x, gt, a, b, tag):
    o, p = _mm(a, b, "nn", F32, "mmres_" + tag, resid=(x, gt))
    return o, (gt, a, b, p)


def _mm_resid_b(tag, res, g):
    gt, a, b, p = res
    dp, dgt = _resid_bwd_call(gt, p, g, "mmres_" + tag + "_gate")
    da = _mm(dp, b, "nt", a.dtype, "mmres_" + tag + "_da")
    db = _mm(a, dp, "tn", b.dtype, "mmres_" + tag + "_db")
    return g, dgt, da, db


mm_resid.defvjp(_mm_resid_f, _mm_resid_b)


def _swiglu_fwd_call(gu, name):
    t, f2 = gu.shape
    f = f2 // 2
    tr = _rows(t, 128)
    half = pl.BlockSpec((tr, f), lambda i: (i, 0))

    def body(g_ref, u_ref, o_ref):
        g = g_ref[...].astype(F32)
        o_ref[...] = (g * _sigmoid(g) * u_ref[...].astype(F32)).astype(o_ref.dtype)

    return pl.pallas_call(
        body, name=name, grid=(t // tr,),
        in_specs=[half, pl.BlockSpec((tr, f), lambda i: (i, 1))], out_specs=half,
        out_shape=jax.ShapeDtypeStruct((t, f), BF16), compiler_params=_cp(("parallel",)),
    )(gu, gu)


def _swiglu_bwd_call(gu, da, name):
    t, f2 = gu.shape
    f = f2 // 2
    tr = _rows(t, 128)
    wide = pl.BlockSpec((tr, f2), lambda i: (i, 0))

    def body(gu_ref, da_ref, dgu_ref):
        g = gu_ref[:, :f].astype(F32)
        u = gu_ref[:, f:].astype(F32)
        da = da_ref[...].astype(F32)
        s = _sigmoid(g)
        dgu_ref[:, :f] = (da * u * s * (1.0 + g * (1.0 - s))).astype(dgu_ref.dtype)
        dgu_ref[:, f:] = (da * g * s).astype(dgu_ref.dtype)

    return pl.pallas_call(
        body, name=name, grid=(t // tr,),
        in_specs=[wide, pl.BlockSpec((tr, f), lambda i: (i, 0))], out_specs=wide,
        out_shape=jax.ShapeDtypeStruct((t, f2), gu.dtype), compiler_params=_cp(("parallel",)),
    )(gu, da)


@functools.partial(jax.custom_vjp, nondiff_argnums=(1,))
def swiglu(gu, tag):
    return _swiglu_fwd_call(gu, "swiglu_" + tag)


def _swiglu_f(gu, tag):
    return _swiglu_fwd_call(gu, "swiglu_" + tag), (gu,)


def _swiglu_b(tag, res, da):
    return (_swiglu_bwd_call(res[0], da, "swiglu_" + tag + "_bwd"),)


swiglu.defvjp(_swiglu_f, _swiglu_b)


def loss_head(x, fw, tgt):
    t, d = x.shape
    tr = _rows(t)
    row = pl.BlockSpec((tr, d), lambda i: (i, 0))
    vec = pl.BlockSpec((1, d), lambda i: (0, 0))
    tile = pl.BlockSpec((8, LANE), lambda i: (0, 0))

    def body(x_ref, fw_ref, tgt_ref, loss_ref, dx_ref, dfw_ref):
        i = pl.program_id(0)
        xv = x_ref[...]
        fw = fw_ref[...]
        r = lax.rsqrt(jnp.mean(xv * xv, axis=-1, keepdims=True) + EPS)
        yh = xv * r
        e = yh * fw - tgt_ref[...]
        dy = e * (1.0 / d)
        dyw = dy * fw
        dx_ref[...] = r * (dyw - yh * jnp.mean(dyw * yh, axis=-1, keepdims=True))

        @pl.when(i == 0)
        def _():
            loss_ref[...] = jnp.zeros_like(loss_ref)
            dfw_ref[...] = jnp.zeros_like(dfw_ref)

        loss_ref[...] += 0.5 * jnp.sum(jnp.mean(e * e, axis=-1, keepdims=True))
        dfw_ref[...] += jnp.sum(dy * yh, axis=0, keepdims=True)

    return pl.pallas_call(
        body, name="loss_head", grid=(t // tr,), in_specs=[row, vec, row],
        out_specs=[tile, row, vec],
        out_shape=[jax.ShapeDtypeStruct((8, LANE), F32), jax.ShapeDtypeStruct((t, d), F32),
                   jax.ShapeDtypeStruct((1, d), F32)],
        compiler_params=_cp(("arbitrary",)),
    )(x, fw, tgt)


def _attn_scores(qn_ref, qr_ref, kn_ref, kr_ref, diag):
    tq = qn_ref.shape[0]
    s = lax.dot_general(qn_ref[...].astype(BF16), kn_ref[...].astype(BF16), NT, preferred_element_type=F32)
    s += lax.dot_general(qr_ref[...].astype(BF16), kr_ref[...].astype(BF16), NT, preferred_element_type=F32)
    s = s * (QK_NOPE + QK_ROPE) ** -0.5
    if diag:
        rows = lax.broadcasted_iota(jnp.int32, (tq, tq), 0)
        cols = lax.broadcasted_iota(jnp.int32, (tq, tq), 1)
        s = jnp.where(cols <= rows, s, -1e30)
    return s


def _attn_fwd_call(qn, qr, kv, kr, name):
    t = qn.shape[0]
    h_n = MLA_HEADS
    tq = _rows(t, 512)
    nq = t // tq
    assert V_HEAD == LANE and tq % LANE == 0

    def body(qn_ref, qr_ref, kn_ref, v_ref, kr_ref, o_ref, lse_ref, m_scr, l_scr, acc_scr):
        i, j = pl.program_id(1), pl.program_id(2)

        @pl.when(j == 0)
        def _():
            m_scr[...] = jnp.full_like(m_scr, -1e30)
            l_scr[...] = jnp.zeros_like(l_scr)
            acc_scr[...] = jnp.zeros_like(acc_scr)

        def step(diag):
            s = _attn_scores(qn_ref, qr_ref, kn_ref, kr_ref, diag)
            m_old = m_scr[...]
            m_new = jnp.maximum(m_old, jnp.max(s, axis=-1, keepdims=True))
            p = jnp.exp(s - jnp.tile(m_new, (1, tq // LANE)))
            alpha = jnp.exp(m_old - m_new)
            l_scr[...] = alpha * l_scr[...] + jnp.sum(p, axis=-1, keepdims=True)
            acc_scr[...] = alpha * acc_scr[...] + jnp.dot(p.astype(BF16), v_ref[...].astype(BF16),
                                                           preferred_element_type=F32)
            m_scr[...] = m_new

        @pl.when(j < i)
        def _():
            step(False)

        @pl.when(j == i)
        def _():
            step(True)
            o_ref[...] = (acc_scr[...] / l_scr[...]).astype(o_ref.dtype)
            lse_ref[...] = (m_scr[...] + jnp.log(l_scr[...]))[:, :1]

    return pl.pallas_call(
        body, name=name, grid=(h_n, nq, nq),
        in_specs=[
            pl.BlockSpec((tq, QK_NOPE), lambda h, i, j: (i, h)),
            pl.BlockSpec((None, tq, QK_ROPE), lambda h, i, j: (h, i, 0)),
            pl.BlockSpec((tq, QK_NOPE), lambda h, i, j: (jnp.minimum(j, i), 2 * h)),
            pl.BlockSpec((tq, V_HEAD), lambda h, i, j: (jnp.minimum(j, i), 2 * h + 1)),
            pl.BlockSpec((tq, QK_ROPE), lambda h, i, j: (jnp.minimum(j, i), 0)),
        ],
        out_specs=[
            pl.BlockSpec((tq, V_HEAD), lambda h, i, j: (i, h)),
            pl.BlockSpec((None, tq, 1), lambda h, i, j: (h, i, 0)),
        ],
        out_shape=[jax.ShapeDtypeStruct((t, h_n * V_HEAD), BF16),
                   jax.ShapeDtypeStruct((h_n, t, 1), F32)],
        scratch_shapes=[pltpu.VMEM((tq, LANE), F32), pltpu.VMEM((tq, LANE), F32),
                        pltpu.VMEM((tq, V_HEAD), F32)],
        compiler_params=_cp(("parallel", "parallel", "arbitrary")),
    )(qn, qr, kv, kv, kr)


def _attn_bwd_call(qn, qr, kv, kr, o, lse, do, name):
    t = qn.shape[0]
    h_n = MLA_HEADS
    tq = _rows(t, 512)
    nq = t // tq
    scale = (QK_NOPE + QK_ROPE) ** -0.5

    def body(qn_ref, qr_ref, kn_ref, v_ref, kr_ref, o_ref, lse_ref, do_ref,
             dqn_ref, dqr_ref, dkv_ref, dkr_ref, dqn_scr, dqr_scr, dkn_scr, dv_scr, dkr_scr):
        j, i = pl.program_id(1), pl.program_id(2)

        @pl.when(jnp.logical_and(j == 0, i == 0))
        def _():
            dqn_scr[...] = jnp.zeros_like(dqn_scr)
            dqr_scr[...] = jnp.zeros_like(dqr_scr)

        @pl.when(i == 0)
        def _():
            dkn_scr[...] = jnp.zeros_like(dkn_scr)
            dv_scr[...] = jnp.zeros_like(dv_scr)
            dkr_scr[...] = jnp.zeros_like(dkr_scr)

        def step(diag):
            qn_b = qn_ref[...].astype(BF16)
            qr_b = qr_ref[...].astype(BF16)
            kn_b = kn_ref[...].astype(BF16)
            kr_b = kr_ref[...].astype(BF16)
            do_b = do_ref[...]
            p = jnp.exp(_attn_scores(qn_ref, qr_ref, kn_ref, kr_ref, diag) - lse_ref[...])
            delta = jnp.sum(do_b.astype(F32) * o_ref[...].astype(F32), axis=-1, keepdims=True)
            dp = lax.dot_general(do_b, v_ref[...].astype(BF16), NT, preferred_element_type=F32)
            ds = (p * (dp - delta) * scale).astype(BF16)
            p_b = p.astype(BF16)
            dv_scr[...] += lax.dot_general(p_b, do_b, TN, preferred_element_type=F32)
            dkn_scr[...] += lax.dot_general(ds, qn_b, TN, preferred_element_type=F32)
            dkr_scr[...] += lax.dot_general(ds, qr_b, TN, preferred_element_type=F32)
            sl = pl.ds(pl.multiple_of(i * tq, tq), tq)
            dqn_scr[sl, :] += jnp.dot(ds, kn_b, preferred_element_type=F32)
            dqr_scr[sl, :] += jnp.dot(ds, kr_b, preferred_element_type=F32)

        @pl.when(i > j)
        def _():
            step(False)

        @pl.when(i == j)
        def _():
            step(True)

        @pl.when(i == nq - 1)
        def _():
            dkv_ref[:, :QK_NOPE] = dkn_scr[...].astype(dkv_ref.dtype)
            dkv_ref[:, QK_NOPE:] = dv_scr[...].astype(dkv_ref.dtype)
            dkr_ref[...] = dkr_scr[...]

        @pl.when(jnp.logical_and(j == nq - 1, i == nq - 1))
        def _():
            dqn_ref[...] = dqn_scr[...].astype(dqn_ref.dtype)
            dqr_ref[...] = dqr_scr[...].astype(dqr_ref.dtype)

    qi = lambda j, i: jnp.maximum(i, j)
    return pl.pallas_call(
        body, name=name, grid=(h_n, nq, nq),
        in_specs=[
            pl.BlockSpec((tq, QK_NOPE), lambda h, j, i: (qi(j, i), h)),
            pl.BlockSpec((None, tq, QK_ROPE), lambda h, j, i: (h, qi(j, i), 0)),
            pl.BlockSpec((tq, QK_NOPE), lambda h, j, i: (j, 2 * h)),
            pl.BlockSpec((tq, V_HEAD), lambda h, j, i: (j, 2 * h + 1)),
            pl.BlockSpec((tq, QK_ROPE), lambda h, j, i: (j, 0)),
            pl.BlockSpec((tq, V_HEAD), lambda h, j, i: (qi(j, i), h)),
            pl.BlockSpec((None, tq, 1), lambda h, j, i: (h, qi(j, i), 0)),
            pl.BlockSpec((tq, V_HEAD), lambda h, j, i: (qi(j, i), h)),
        ],
        out_specs=[
            pl.BlockSpec((t, QK_NOPE), lambda h, j, i: (0, h)),
            pl.BlockSpec((None, t, QK_ROPE), lambda h, j, i: (h, 0, 0)),
            pl.BlockSpec((tq, QK_NOPE + V_HEAD), lambda h, j, i: (j, h)),
            pl.BlockSpec((None, tq, QK_ROPE), lambda h, j, i: (h, j, 0)),
        ],
        out_shape=[jax.ShapeDtypeStruct((t, h_n * QK_NOPE), qn.dtype),
                   jax.ShapeDtypeStruct((h_n, t, QK_ROPE), qr.dtype),
                   jax.ShapeDtypeStruct((t, h_n * (QK_NOPE + V_HEAD)), kv.dtype),
                   jax.ShapeDtypeStruct((h_n, t, QK_ROPE), F32)],
        scratch_shapes=[pltpu.VMEM((t, QK_NOPE), F32), pltpu.VMEM((t, QK_ROPE), F32),
                        pltpu.VMEM((tq, QK_NOPE), F32), pltpu.VMEM((tq, V_HEAD), F32),
                        pltpu.VMEM((tq, QK_ROPE), F32)],
        compiler_params=_cp(("parallel", "arbitrary", "arbitrary")),
    )(qn, qr, kv, kv, kr, o, lse, do)


@functools.partial(jax.custom_vjp, nondiff_argnums=(4,))
def attention(qn, qr, kv, kr, tag):
    return _attn_fwd_call(qn, qr, kv, kr, "attn_" + tag)[0]


def _attention_f(qn, qr, kv, kr, tag):
    o, lse = _attn_fwd_call(qn, qr, kv, kr, "attn_" + tag)
    return o, (qn, qr, kv, kr, o, lse)


def _attention_b(tag, res, do):
    dqn, dqr, dkv, dkr_h = _attn_bwd_call(*res, do, "attn_" + tag + "_bwd")
    return dqn, dqr, dkv, jnp.sum(dkr_h, axis=0).astype(res[3].dtype)


attention.defvjp(_attention_f, _attention_b)


def _shift_down(u, s):
    if s == 0:
        return u
    t = u.shape[0]
    rolled = pltpu.roll(u, s, 0)
    return jnp.where(lax.broadcasted_iota(jnp.int32, u.shape, 0) >= s, rolled, 0.0)


def _shift_up(u, s):
    if s == 0:
        return u
    t = u.shape[0]
    rolled = pltpu.roll(u, t - s, 0)
    return jnp.where(lax.broadcasted_iota(jnp.int32, u.shape, 0) < t - s, rolled, 0.0)


def _conv_blocks(t, c3):
    p = c3 // 3
    tc = _tile(p, 512)
    per = p // tc
    return p, tc, per


def _conv_fwd_call(u, w, name):
    t, c3 = u.shape
    p, tc, per = _conv_blocks(t, c3)

    def body(u_ref, w_ref, o_ref):
        u = u_ref[...].astype(F32)
        y = jnp.zeros_like(u)
        for j in range(CONV_WIDTH):
            y = y + w_ref[j:j + 1, :] * _shift_down(u, CONV_WIDTH - 1 - j)
        o_ref[...] = y * _sigmoid(y)

    return pl.pallas_call(
        body, name=name, grid=(c3 // tc,),
        in_specs=[pl.BlockSpec((t, tc), lambda cb: (0, cb)),
                  pl.BlockSpec((CONV_WIDTH, tc), lambda cb: (0, cb))],
        out_specs=pl.BlockSpec((None, t, tc), lambda cb: (cb // per, 0, cb % per)),
        out_shape=jax.ShapeDtypeStruct((3, t, p), F32),
        compiler_params=_cp(("parallel",)),
    )(u, w)


def _conv_bwd_call(u, w, do, name):
    t, c3 = u.shape
    p, tc, per = _conv_blocks(t, c3)

    def body(u_ref, w_ref, do_ref, du_ref, dw_ref):
        u = u_ref[...].astype(F32)
        shifted = [_shift_down(u, CONV_WIDTH - 1 - j) for j in range(CONV_WIDTH)]
        y = jnp.zeros_like(u)
        for j in range(CONV_WIDTH):
            y = y + w_ref[j:j + 1, :] * shifted[j]
        s = _sigmoid(y)
        dy = do_ref[...] * s * (1.0 + y * (1.0 - s))
        du = jnp.zeros_like(u)
        for j in range(CONV_WIDTH):
            du = du + w_ref[j:j + 1, :] * _shift_up(dy, CONV_WIDTH - 1 - j)
            dw_ref[j:j + 1, :] = jnp.sum(dy * shifted[j], axis=0, keepdims=True)
        du_ref[...] = du.astype(du_ref.dtype)

    return pl.pallas_call(
        body, name=name, grid=(c3 // tc,),
        in_specs=[pl.BlockSpec((t, tc), lambda cb: (0, cb)),
                  pl.BlockSpec((CONV_WIDTH, tc), lambda cb: (0, cb)),
                  pl.BlockSpec((None, t, tc), lambda cb: (cb // per, 0, cb % per))],
        out_specs=[pl.BlockSpec((t, tc), lambda cb: (0, cb)),
                   pl.BlockSpec((CONV_WIDTH, tc), lambda cb: (0, cb))],
        out_shape=[jax.ShapeDtypeStruct((t, c3), u.dtype), jax.ShapeDtypeStruct((CONV_WIDTH, c3), F32)],
        compiler_params=_cp(("parallel",)),
    )(u, w, do)


@functools.partial(jax.custom_vjp, nondiff_argnums=(2,))
def conv_silu(u, w, tag):
    return _conv_fwd_call(u, w, "conv_" + tag)


def _conv_silu_f(u, w, tag):
    return _conv_fwd_call(u, w, "conv_" + tag), (u, w)


def _conv_silu_b(tag, res, do):
    return tuple(_conv_bwd_call(*res, do, "conv_" + tag + "_bwd"))


conv_silu.defvjp(_conv_silu_f, _conv_silu_b)


BNN = (((2,), (1,)), ((0,), (0,)))
BNT = (((2,), (2,)), ((0,), (0,)))
BTN = (((1,), (1,)), ((0,), (0,)))


def _xdot(a, b, dn=BNN):
    return lax.dot_general(a, b, dn, precision=lax.Precision.HIGHEST, preferred_element_type=F32)


def _bf16_dot(a, b, dn):
    return lax.dot_general(a.astype(BF16), b.astype(BF16), dn, preferred_element_type=F32)


def _dot3(a, b, dn):
    ah, bh = a.astype(BF16), b.astype(BF16)
    al, bl = a - ah.astype(F32), b - bh.astype(F32)
    return _bf16_dot(ah, bh, dn) + (_bf16_dot(ah, bl, dn) + _bf16_dot(al, bh, dn))


def _transposed(dn, a, b, g):
    if dn == BNN:
        return (g, b, BNT), (a, g, BTN)
    if dn == BNT:
        return (g, b, BNN), (g, a, BTN)
    return (b, g, BNT), (a, g, BNN)


@functools.partial(jax.custom_vjp, nondiff_argnums=(2,))
def _hdot(a, b, dn=BNN):
    return _dot3(a, b, dn)


def _hdot_f(a, b, dn):
    return _dot3(a, b, dn), (a, b)


def _hdot_b(dn, res, g):
    da, db = _transposed(dn, *res, g)
    return _dot3(*da), _dot3(*db)


_hdot.defvjp(_hdot_f, _hdot_b)


@functools.partial(jax.custom_vjp, nondiff_argnums=(2,))
def _bdot(a, b, dn=BNN):
    return _bf16_dot(a, b, dn)


def _bdot_f(a, b, dn):
    return _bf16_dot(a, b, dn), (a, b)


def _bdot_b(dn, res, g):
    da, db = _transposed(dn, *res, g)
    return _bf16_dot(*da), _bf16_dot(*db)


_bdot.defvjp(_bdot_f, _bdot_b)


GDN_HEADS_PER_STEP = 8
GDN_HEADS_PER_STEP_BWD = 8


def _gdn_chunk(q, k, v, z, bl, al, a_log, dtb, gn, s):
    b, c = q.shape[0], q.shape[1]
    ri = lax.broadcasted_iota(jnp.int32, (c, c), 0)
    ci = lax.broadcasted_iota(jnp.int32, (c, c), 1)
    lower = (ri >= ci)[None]
    strict = (ri > ci)[None]
    low_incl = jnp.broadcast_to((ri >= ci).astype(F32), (b, c, c))
    up_incl = jnp.broadcast_to((ri <= ci).astype(F32), (b, c, c))
    eye = (ri == ci).astype(F32)[None]

    q = q * lax.rsqrt(jnp.sum(q * q, axis=-1, keepdims=True) + EPS) * (GDN_DK ** -0.5)
    k = k * lax.rsqrt(jnp.sum(k * k, axis=-1, keepdims=True) + EPS)
    beta = _sigmoid(bl)
    g = -jnp.exp(a_log) * _softplus(al + dtb)
    g_w = jnp.broadcast_to(g, (b, c, LANE))
    gc = _xdot(low_incl, g_w)
    gr = _xdot(g_w[:, :, :c], up_incl, BTN)
    diff = gc[:, :, :c] - gr
    decay = jnp.where(lower, jnp.exp(jnp.where(lower, diff, 0.0)), 0.0)
    kb = k * beta
    lmat = jnp.where(strict, _bdot(kb, k, BNT) * decay, 0.0)
    inv = eye - lmat
    pw = lmat
    for _ in range(int(math.log2(c)) - 1):
        pw = _hdot(pw, pw)
        inv = _hdot(inv, eye + pw)
    eg = jnp.exp(gc)
    u = _hdot(inv, v * beta)
    w = _hdot(inv, kb * eg)
    attn = jnp.where(lower, _bdot(q, k, BNT) * decay, 0.0)
    v_new = u - _bdot(w, s)
    o = _bdot(q * eg, s) + _bdot(attn, v_new)
    g_last = jnp.sum(g_w, axis=1, keepdims=True)
    k_dec = k * jnp.exp(g_last - gc)
    s_new = s * jnp.exp(g_last) + _bdot(k_dec, v_new, BTN)
    on = o * lax.rsqrt(jnp.mean(o * o, axis=-1, keepdims=True) + EPS) * gn
    return on * (z * _sigmoid(z)), s_new


def _gdn_specs(n_chunks, hb, rev):
    c = CHUNK
    nn = (lambda n: n_chunks - 1 - n) if rev else (lambda n: n)
    plane = lambda pidx: pl.BlockSpec((None, c, hb * GDN_DK), lambda hg, n: (pidx, nn(n), hg))
    col = pl.BlockSpec((hb, c, 1), lambda hg, n: (hg, nn(n), 0))
    scal = pl.BlockSpec((hb, 1, 1), lambda hg, n: (hg, 0, 0))
    zspec = pl.BlockSpec((c, hb * GDN_DV), lambda hg, n: (nn(n), hg))
    gnspec = pl.BlockSpec((1, GDN_DV), lambda hg, n: (0, 0))
    sspec = pl.BlockSpec((hb, None, GDN_DK, GDN_DV), lambda hg, n: (hg, nn(n), 0, 0))
    return plane, col, scal, zspec, gnspec, sspec


def _heads_per_step(want):
    return math.gcd(want, GDN_HEADS)


def _heads(ref, hb):
    return jnp.stack([ref[:, j * GDN_DK:(j + 1) * GDN_DK] for j in range(hb)])


def _gdn_fwd_call(qkv, z, bl, al, a_log, dtb, gn, name):
    t = z.shape[0]
    h_n = GDN_HEADS
    hb = _heads_per_step(GDN_HEADS_PER_STEP)
    n_chunks = t // CHUNK
    plane, col, scal, zspec, gnspec, sspec = _gdn_specs(n_chunks, hb, False)

    def body(q_ref, k_ref, v_ref, z_ref, bl_ref, al_ref, a_ref, dtb_ref, gn_ref, o_ref, sall_ref, s_scr):
        n = pl.program_id(1)

        @pl.when(n == 0)
        def _():
            s_scr[...] = jnp.zeros_like(s_scr)

        s = s_scr[...]
        sall_ref[...] = s
        o, s_new = _gdn_chunk(_heads(q_ref, hb), _heads(k_ref, hb), _heads(v_ref, hb),
                              _heads(z_ref, hb).astype(F32),
                              bl_ref[...], al_ref[...], a_ref[...], dtb_ref[...], gn_ref[...], s)
        for j in range(hb):
            o_ref[:, j * GDN_DV:(j + 1) * GDN_DV] = o[j].astype(o_ref.dtype)
        s_scr[...] = s_new

    return pl.pallas_call(
        body, name=name, grid=(h_n // hb, n_chunks),
        in_specs=[plane(0), plane(1), plane(2), zspec, col, col, scal, scal, gnspec],
        out_specs=[zspec, sspec],
        out_shape=[jax.ShapeDtypeStruct((t, h_n * GDN_DV), BF16),
                   jax.ShapeDtypeStruct((h_n, n_chunks, GDN_DK, GDN_DV), F32)],
        scratch_shapes=[pltpu.VMEM((hb, GDN_DK, GDN_DV), F32)],
        compiler_params=_cp(("parallel", "arbitrary")),
    )(qkv, qkv, qkv, z, bl, al, a_log, dtb, gn)


def _gdn_bwd_call(qkv, z, bl, al, a_log, dtb, gn, sall, do, name):
    t = z.shape[0]
    h_n = GDN_HEADS
    hb = _heads_per_step(GDN_HEADS_PER_STEP_BWD)
    n_chunks = t // CHUNK
    c = CHUNK
    plane, col, scal, zspec, gnspec, sspec = _gdn_specs(n_chunks, hb, True)
    dplanes = pl.BlockSpec((3, c, hb * GDN_DK), lambda hg, n: (0, n_chunks - 1 - n, hg))
    gnh = pl.BlockSpec((None, 1, GDN_DV), lambda hg, n: (hg, 0, 0))

    def body(q_ref, k_ref, v_ref, z_ref, bl_ref, al_ref, a_ref, dtb_ref, gn_ref, s_ref, do_ref,
             dqkv_ref, dz_ref, dbl_ref, dal_ref, da_ref, ddtb_ref, dgn_ref, ds_scr):
        n = pl.program_id(1)

        @pl.when(n == 0)
        def _():
            ds_scr[...] = jnp.zeros_like(ds_scr)
            da_ref[...] = jnp.zeros_like(da_ref)
            ddtb_ref[...] = jnp.zeros_like(ddtb_ref)
            dgn_ref[...] = jnp.zeros_like(dgn_ref)

        _, vjp = jax.vjp(_gdn_chunk, _heads(q_ref, hb), _heads(k_ref, hb), _heads(v_ref, hb),
                              _heads(z_ref, hb).astype(F32),
                         bl_ref[...], al_ref[...], a_ref[...], dtb_ref[...], gn_ref[...], s_ref[...])
        dq, dk, dv, dz, dbl, dal, da, ddtb, dgn, ds = vjp((_heads(do_ref, hb).astype(F32), ds_scr[...]))
        for j in range(hb):
            hs = slice(j * GDN_DK, (j + 1) * GDN_DK)
            dqkv_ref[0, :, hs] = dq[j]
            dqkv_ref[1, :, hs] = dk[j]
            dqkv_ref[2, :, hs] = dv[j]
            dz_ref[:, hs] = dz[j].astype(dz_ref.dtype)
        dbl_ref[...] = dbl
        dal_ref[...] = dal
        da_ref[...] += da
        ddtb_ref[...] += ddtb
        dgn_ref[...] += dgn
        ds_scr[...] = ds

    return pl.pallas_call(
        body, name=name, grid=(h_n // hb, n_chunks),
        in_specs=[plane(0), plane(1), plane(2), zspec, col, col, scal, scal, gnspec, sspec, zspec],
        out_specs=[dplanes, zspec, col, col, scal, scal, gnh],
        out_shape=[jax.ShapeDtypeStruct((3, t, h_n * GDN_DK), F32),
                   jax.ShapeDtypeStruct((t, h_n * GDN_DV), z.dtype),
                   jax.ShapeDtypeStruct((h_n, t, 1), F32), jax.ShapeDtypeStruct((h_n, t, 1), F32),
                   jax.ShapeDtypeStruct((h_n, 1, 1), F32), jax.ShapeDtypeStruct((h_n, 1, 1), F32),
                   jax.ShapeDtypeStruct((h_n // hb, 1, GDN_DV), F32)],
        scratch_shapes=[pltpu.VMEM((hb, GDN_DK, GDN_DV), F32)],
        compiler_params=_cp(("parallel", "arbitrary")),
    )(qkv, qkv, qkv, z, bl, al, a_log, dtb, gn, sall, do)


@functools.partial(jax.custom_vjp, nondiff_argnums=(7,))
def gdn(qkv, z, bl, al, a_log, dtb, gn, tag):
    return _gdn_fwd_call(qkv, z, bl, al, a_log, dtb, gn, "gdn_" + tag)[0]


def _gdn_f(qkv, z, bl, al, a_log, dtb, gn, tag):
    o, sall = _gdn_fwd_call(qkv, z, bl, al, a_log, dtb, gn, "gdn_" + tag)
    return o, (qkv, z, bl, al, a_log, dtb, gn, sall)


def _gdn_b(tag, res, do):
    dqkv, dz, dbl, dal, da, ddtb, dgn_h = _gdn_bwd_call(*res, do, "gdn_" + tag + "_bwd")
    return dqkv, dz, dbl, dal, da, ddtb, jnp.sum(dgn_h, axis=0)


gdn.defvjp(_gdn_f, _gdn_b)


ADAMW_BLOCK = 256 * 1024


def adamw(w, m, v, *, parts, name):
    n_layers = len(parts)
    n_parts, r, c = parts[0].shape
    assert w.shape == (n_layers * r, c), (w.shape, parts[0].shape)
    tr = r
    for cand in (512, 256, 128, 64, 32, 16, 8):
        if r % cand == 0 and cand * c <= ADAMW_BLOCK:
            tr = cand
            break
    nb = r // tr
    blk = pl.BlockSpec((tr, c), lambda l, i: (l * nb + i, 0))
    bc1 = 1.0 - ADAM_B1 ** ADAM_STEP
    bc2 = 1.0 - ADAM_B2 ** ADAM_STEP

    def part_spec(li):
        return pl.BlockSpec((n_parts, tr, c),
                            lambda l, i: (0, jnp.where(l == li, i, jnp.where(l < li, 0, nb - 1)), 0))

    def body(*refs):
        w_ref, p_refs = refs[0], refs[1:1 + n_layers]
        m_ref, v_ref, g_ref, d_ref, mo_ref, vo_ref = refs[1 + n_layers:]
        for li in range(n_layers):
            @pl.when(pl.program_id(0) == li)
            def _(p_ref=p_refs[li]):
                g = p_ref[0].astype(F32)
                for i in range(1, n_parts):
                    g = g + p_ref[i].astype(F32)
                m2 = ADAM_B1 * m_ref[...] + (1.0 - ADAM_B1) * g
                v2 = ADAM_B2 * v_ref[...] + (1.0 - ADAM_B2) * (g * g)
                g_ref[...] = g
                mo_ref[...] = m2
                vo_ref[...] = v2
                d_ref[...] = -ADAM_LR * ((m2 / bc1) / (jnp.sqrt(v2 / bc2) + ADAM_EPS)
                                         + ADAM_WD * w_ref[...])

    return pl.pallas_call(
        body, name=name, grid=(n_layers, nb),
        in_specs=[blk] + [part_spec(li) for li in range(n_layers)] + [blk, blk],
        out_specs=[blk] * 4, out_shape=[jax.ShapeDtypeStruct(w.shape, F32)] * 4,
        compiler_params=_cp(("arbitrary", "arbitrary")),
    )(w, *parts, m, v)


_HBM = pl.BlockSpec(memory_space=pltpu.HBM)
_SEM = pl.BlockSpec(memory_space=pltpu.SEMAPHORE)
_EFFECT = pltpu.SideEffectType.DATAFLOW_SIDE_EFFECTING


def _peer(x, y, c, d):
    px = 1 - x if d & 4 else x
    py = 1 - y if d & 2 else y
    pc = 1 - c if d & 1 else c
    return (px, py, pc), 4 * px + 2 * py + pc


ALL_PEERS = (1, 2, 3, 4, 5, 6, 7)
SIBLING = 1
SAME_CORE_REMOTE = (2, 4, 6)


def copy_start(arrays, mode, carry, name):
    n = len(arrays)
    if mode == "forward":
        lands = []
    else:
        lands = [lax.empty(a.shape if mode == "scatter" else (N_DEV,) + a.shape, a.dtype) for a in arrays]
    n_in = n + len(lands) + 1

    def body(*refs):
        srcs = refs[:n]
        dsts = refs[n:2 * n] if lands else srcs
        sems = refs[n_in:n_in + 2 * n]
        x, y, c = (lax.axis_index(a) for a in AXES)
        me = 4 * x + 2 * y + c
        for k in range(n):
            if mode == "forward":
                sibling, _ = _peer(x, y, c, SIBLING)
                copies = [(srcs[k].at[_peer(x, y, c, d)[1]], dsts[k].at[_peer(x, y, c, d)[1]], sibling)
                          for d in SAME_CORE_REMOTE]
            elif mode == "gather":
                copies = [(srcs[k], dsts[k].at[me], _peer(x, y, c, d)[0]) for d in (SIBLING,) + SAME_CORE_REMOTE]
            else:
                copies = [(srcs[k].at[_peer(x, y, c, d)[1]], dsts[k].at[me], _peer(x, y, c, d)[0])
                          for d in ALL_PEERS]
            for src, dst, peer in copies:
                pltpu.make_async_remote_copy(src_ref=src, dst_ref=dst, send_sem=sems[2 * k],
                                             recv_sem=sems[2 * k + 1], device_id=peer,
                                             device_id_type=pl.DeviceIdType.MESH).start()

    operands = list(arrays) + lands + [carry]
    outs = pl.pallas_call(
        body, name=name,
        out_shape=tuple([pltpu.SemaphoreType.DMA(())] * (2 * n)
                        + [pltpu.HBM(a.shape, a.dtype) for a in operands]),
        in_specs=[_HBM] * n_in,
        out_specs=tuple([_SEM] * (2 * n) + [_HBM] * n_in),
        input_output_aliases={i: 2 * n + i for i in range(n_in)},
        compiler_params=pltpu.CompilerParams(has_side_effects=_EFFECT),
    )(*[pltpu.with_memory_space_constraint(a, pltpu.HBM) for a in operands])
    sems, thru = outs[:2 * n], outs[2 * n:-1]
    handles = [(sems[2 * k], sems[2 * k + 1], thru[k] if lands else None, thru[n + k] if lands else thru[k])
               for k in range(n)]
    return outs[-1], handles


def copy_wait(handles, n_blocks, after, name):
    n = len(handles)
    sems = [s for h in handles for s in h[:2]]
    srcs = [h[2] for h in handles if h[2] is not None]
    lands = [h[3] for h in handles]
    ns = len(srcs)

    def body(*refs):
        dsts = refs[ns:ns + n]
        sem_refs = refs[ns + n:ns + 3 * n]
        x, y, c = (lax.axis_index(a) for a in AXES)
        for k in range(n):
            blocks = dsts[k].at[pl.ds(0, n_blocks)]
            pltpu.make_async_remote_copy(
                src_ref=blocks, dst_ref=blocks, send_sem=sem_refs[2 * k], recv_sem=sem_refs[2 * k + 1],
                device_id=(x, y, c), device_id_type=pl.DeviceIdType.MESH).wait()

    outs = pl.pallas_call(
        body, name=name,
        out_shape=tuple([pltpu.HBM(a.shape, a.dtype) for a in srcs + lands]),
        in_specs=[_HBM] * (ns + n) + [_SEM] * (2 * n) + [pl.BlockSpec(memory_space=pl.ANY)],
        out_specs=tuple([_HBM] * (ns + n)),
        input_output_aliases={i: i for i in range(ns + n)},
        compiler_params=pltpu.CompilerParams(has_side_effects=_EFFECT),
    )(*srcs, *lands, *sems, after)
    return (list(outs[:ns]) if ns else [None] * n), list(outs[ns:])


def exchange(arrays, modes, name):
    n = len(arrays)
    hbm = pl.BlockSpec(memory_space=pltpu.HBM)
    out_shape = [jax.ShapeDtypeStruct(a.shape if md == "scatter" else (N_DEV,) + a.shape, a.dtype)
                 for a, md in zip(arrays, modes)]

    def body(*refs):
        ins, outs = refs[:n], refs[n:2 * n]
        send_sems, recv_sems, local_sems = refs[2 * n:]
        x, y, c = (lax.axis_index(a) for a in AXES)
        me = 4 * x + 2 * y + c

        def src(k, p):
            return ins[k].at[p] if modes[k] == "scatter" else ins[k]

        local = [pltpu.make_async_copy(src(k, me), outs[k].at[me], local_sems.at[k]) for k in range(n)]
        for cp in local:
            cp.start()
        started = []
        for d in range(1, N_DEV):
            px = 1 - x if d & 4 else x
            py = 1 - y if d & 2 else y
            pc = 1 - c if d & 1 else c
            pid = 4 * px + 2 * py + pc
            for k in range(n):
                pltpu.make_async_remote_copy(
                    src_ref=src(k, pid), dst_ref=outs[k].at[me],
                    send_sem=send_sems.at[k, d - 1], recv_sem=recv_sems.at[k, d - 1],
                    device_id=(px, py, pc), device_id_type=pl.DeviceIdType.MESH).start()
                started.append((k, d, pid, (px, py, pc)))
        for k, d, pid, peer in started:
            pltpu.make_async_remote_copy(
                src_ref=src(k, pid), dst_ref=outs[k].at[pid],
                send_sem=send_sems.at[k, d - 1], recv_sem=recv_sems.at[k, d - 1],
                device_id=peer, device_id_type=pl.DeviceIdType.MESH).wait()
        for cp in local:
            cp.wait()

    outs = pl.pallas_call(
        body, name=name, in_specs=[hbm] * n, out_specs=[hbm] * n, out_shape=out_shape,
        scratch_shapes=[pltpu.SemaphoreType.DMA((n, N_DEV - 1)), pltpu.SemaphoreType.DMA((n, N_DEV - 1)),
                        pltpu.SemaphoreType.DMA((n,))],
        compiler_params=pltpu.CompilerParams(has_side_effects=True),
    )(*arrays)
    return list(outs)


BIG = ("w_in", "w_uq", "w_ukv", "w_o_mla", "w_o_gdn", "w_o", "w_gate_up", "w_down")
ROW_SHARDED = ("w_o", "w_down")
SMALL = ("b_ada", "norm_mix", "norm_ffn", "q_a_norm", "kv_a_norm", "A_log", "dt_bias", "gdn_norm",
         "final_norm")
WEIGHTS = ("w_ada", "b_ada", "norm_mix", "norm_ffn", "w_in", "q_a_norm", "kv_a_norm", "w_uq", "w_ukv",
           "w_o_mla", "conv_w", "A_log", "dt_bias", "gdn_norm", "w_o_gdn", "w_o", "w_gate_up", "w_down",
           "final_norm")


def _unslot(g):
    return g.transpose(1, 0, 2).reshape(g.shape[1], -1)


def _cols(g):
    return g if g.shape[-1] % LANE == 0 else _unslot(g)


def _stack_rows(g):
    return g.reshape(-1, g.shape[-1])


def _rope(xv, cos, sin):
    x1, x2 = jnp.split(xv, 2, axis=-1)
    return jnp.concatenate([x1 * cos - x2 * sin, x2 * cos + x1 * sin], axis=-1)


MIX_WEIGHTS = ("w_uq", "w_ukv", "w_o_mla", "w_o_gdn", "w_o")
FFN_WEIGHTS = ("w_gate_up", "w_down")


def _pad_cols(a):
    return jnp.pad(a, ((0, 0), (0, _pad_lanes(a.shape[1]) - a.shape[1])))


def _stage_in(x, mod, nm, w_in_s, tg):
    d = x.shape[1]
    hg = GDN_HEADS
    w_in = _unslot(w_in_s)
    o1 = Q_LORA + KV_LORA + QK_ROPE
    o2 = o1 + 2 * hg * GDN_DK + hg * GDN_DV
    o3 = o2 + hg * GDN_DV
    o4 = o3 + 2 * hg
    h = ada_norm(x, nm, mod[:, d:2 * d], mod[:, :d], "mix" + tg)
    return (mm(h, _pad_cols(w_in[:, :o1]), "in_a" + tg, BF16), mm(h, w_in[:, o1:o2], "in_qkv" + tg, BF16),
            mm(h, w_in[:, o2:o3], "in_z" + tg, BF16), mm(h, _pad_cols(w_in[:, o3:o4]), "in_ba" + tg, F32),
            mm(h, w_in[:, o4:o4 + 2 * d], "in_g" + tg, BF16))


def _stage_mix(x, mod, seg_a, qkv, z, ba, gl, w_uq_s, w_ukv_s, w_o_mla_s, w_o_gdn_s, w_o_s, conv_s,
               qan, kvan, a_log, dtb, gn, cos, sin, tg):
    t, d = x.shape
    hq, hg = MLA_HEADS, GDN_HEADS
    w_uq = _unslot(w_uq_s).reshape(Q_LORA, hq, QK_NOPE + QK_ROPE)
    w_uq = jnp.concatenate([w_uq[:, :, :QK_NOPE].reshape(Q_LORA, hq * QK_NOPE),
                            w_uq[:, :, QK_NOPE:].reshape(Q_LORA, hq * QK_ROPE)], axis=1)
    c_q = seg_a[:, :Q_LORA]
    c_kv = seg_a[:, Q_LORA:Q_LORA + KV_LORA]
    k_pe = seg_a[:, Q_LORA + KV_LORA:Q_LORA + KV_LORA + QK_ROPE]
    qf = mm(rms_norm(c_q, qan, "qa" + tg), w_uq, "uq" + tg, BF16)
    kvf = mm(rms_norm(c_kv, kvan, "kva" + tg), _cols(w_ukv_s), "ukv" + tg, BF16)
    qn = qf[:, :hq * QK_NOPE]
    q_pe = qf[:, hq * QK_NOPE:].astype(F32).reshape(t, hq, QK_ROPE)
    qr = _rope(q_pe, cos[:, None, :], sin[:, None, :]).transpose(1, 0, 2).astype(BF16)
    kr = _rope(k_pe.astype(F32), cos, sin).astype(BF16)
    y_a = mm(attention(qn, qr, kvf, kr, tg), _cols(w_o_mla_s), "o_mla" + tg, BF16)
    conv_w = conv_s.transpose(1, 0, 2).reshape(CONV_WIDTH, -1)
    qkv_c = conv_silu(qkv, conv_w, tg)
    bl = ba[:, :hg].T[:, :, None]
    al = ba[:, hg:2 * hg].T[:, :, None]
    o_gdn = gdn(qkv_c, z, bl, al, a_log.reshape(hg, 1, 1), dtb.reshape(hg, 1, 1), gn, tg)
    y_b = mm(o_gdn, _cols(w_o_gdn_s), "o_gdn" + tg, BF16)
    return mm_resid(x, mod[:, 2 * d:3 * d], gate_mix(gl, y_a, y_b, tg), _stack_rows(w_o_s), "w_o" + tg)


def _stage_ffn(x, mod, nf, w_gu_s, w_down_s, tg):
    d = x.shape[1]
    h = ada_norm(x, nf, mod[:, 4 * d:5 * d], mod[:, 3 * d:4 * d], "ffn" + tg)
    gu = mm(h, _cols(w_gu_s), "gu" + tg, BF16)
    return mm_resid(x, mod[:, 5 * d:6 * d], swiglu(gu, tg), _stack_rows(w_down_s), "down" + tg)


def _flat_row(arrs):
    v = jnp.concatenate([a.reshape(-1) for a in arrs])
    return jnp.pad(v, (0, _pad_lanes(v.shape[0]) - v.shape[0]))[None, :]


def kernel(x, c, positions, w_ada, b_ada, norm_mix, norm_ffn, w_in, q_a_norm, kv_a_norm, w_uq, w_ukv, w_o_mla, conv_w, A_log, dt_bias, gdn_norm, w_o_gdn, w_o, w_gate_up, w_down, final_norm, loss_target, m_w_ada, m_b_ada, m_norm_mix, m_norm_ffn, m_w_in, m_q_a_norm, m_kv_a_norm, m_w_uq, m_w_ukv, m_w_o_mla, m_conv_w, m_A_log, m_dt_bias, m_gdn_norm, m_w_o_gdn, m_w_o, m_w_gate_up, m_w_down, m_final_norm, v_w_ada, v_b_ada, v_norm_mix, v_norm_ffn, v_w_in, v_q_a_norm, v_kv_a_norm, v_w_uq, v_w_ukv, v_w_o_mla, v_conv_w, v_A_log, v_dt_bias, v_gdn_norm, v_w_o_gdn, v_w_o, v_w_gate_up, v_w_down, v_final_norm):
    given = dict(locals())
    t, d = x.shape[1], x.shape[2]
    n_ada = w_ada.shape[2]
    me = 4 * lax.axis_index("x") + 2 * lax.axis_index("y") + lax.axis_index("c")

    def with_own(land, own):
        return lax.dynamic_update_slice(land, own[None], (me,) + (0,) * own.ndim)

    got = exchange([c, conv_w], ["gather", "gather"], "gather_small")
    c_all, conv_g = got[0].reshape(N_DEV, d), got[1]
    c_rows = jnp.pad(c_all, ((0, 16 - N_DEV), (0, 0)))
    mod_cols = jnp.stack([_mm(c_rows, w_ada[l], "nn", F32, "ada_mod%d" % l, a_act="silu")[:N_DEV]
                          for l in range(DEPTH)], axis=1)
    mod_mine = exchange([mod_cols], ["scatter"], "scatter_mod")[0]
    mods = mod_mine.transpose(1, 0, 2).reshape(DEPTH, N_DEV * n_ada) + b_ada

    groups = [[(n, l) for n in names] for l in range(DEPTH) for names in (("w_in",), MIX_WEIGHTS, FFN_WEIGHTS)]
    gtags = [s + str(l) for l in range(DEPTH) for s in ("in", "mix", "ffn")]
    keys = [k for ks in groups for k in ks]
    mods, handles = copy_start([given[n][l].astype(BF16) for n, l in keys], "gather", mods, "gather_start")
    handles = dict(zip(keys, handles))
    own, relayed = {}, {}

    def relay(gi, carry):
        ks = groups[gi]
        srcs, lands = copy_wait([handles[k] for k in ks], 1 + len(SAME_CORE_REMOTE), carry,
                                "wait_ici_" + gtags[gi])
        own.update(zip(ks, srcs))
        carry, hs = copy_start(lands, "forward", carry, "relay_" + gtags[gi])
        relayed.update(zip(ks, hs))
        return carry

    def landed(gi, after):
        ks = groups[gi]
        _, lands = copy_wait([relayed[k] for k in ks], len(SAME_CORE_REMOTE), after, "wait_" + gtags[gi])
        return [with_own(land, own[k]) for k, land in zip(ks, lands)]

    inv_freq = 1.0 / (ROPE_THETA ** (jnp.arange(0, QK_ROPE, 2, dtype=F32) / QK_ROPE))
    ang = positions[0].astype(F32)[:, None] * inv_freq
    cos, sin = jnp.cos(ang), jnp.sin(ang)
    relay_before = {0: [0], 1: [1], 2: [2, 3], 3: [4], 4: [5], 5: []}

    def weights_for(stage, carry):
        for gi in relay_before[stage]:
            carry = relay(gi, carry)
        return carry, landed(stage, carry)

    xl = x[0]
    vjps = []
    for l in range(DEPTH):
        tg = str(l)
        mod = mods[l:l + 1]
        xl, (w_in_s,) = weights_for(3 * l, xl)
        seg, vjp_in = jax.vjp(lambda *a, tg=tg: _stage_in(*a, tg), xl, mod, norm_mix[l:l + 1], w_in_s)
        seg0, w_mix = weights_for(3 * l + 1, seg[0])
        seg = (seg0,) + tuple(seg[1:])
        xm, vjp_mix = jax.vjp(lambda *a, tg=tg: _stage_mix(*a, cos, sin, tg), xl, mod, *seg, *w_mix,
                              conv_g[:, l], q_a_norm[l:l + 1], kv_a_norm[l:l + 1], A_log[l], dt_bias[l],
                              gdn_norm[l:l + 1])
        xm, w_ffn = weights_for(3 * l + 2, xm)
        xl, vjp_ffn = jax.vjp(lambda *a, tg=tg: _stage_ffn(*a, tg), xm, mod, norm_ffn[l:l + 1], *w_ffn)
        vjps.append((vjp_in, vjp_mix, vjp_ffn))

    loss_t, g, dfn = loss_head(xl, final_norm[None, :], loss_target[0])
    loss = lax.psum(loss_t[0, 0], AXES)
    dsmall = {n: [None] * DEPTH for n in SMALL + ("conv_w",)}
    dmods = [None] * DEPTH
    sent = {}

    def send(ks, grads, carry, name):
        carry, hs = copy_start(list(grads), "scatter", carry, name)
        sent.update(zip(ks, hs))
        return carry

    for l in reversed(range(DEPTH)):
        tg = str(l)
        vjp_in, vjp_mix, vjp_ffn = vjps[l]
        dxm, dmod_f, dsmall["norm_ffn"][l], *dw = vjp_ffn(g)
        dxm = send([(n, l) for n in FFN_WEIGHTS], dw, dxm, "scatter_ffn" + tg)
        dx_m, dmod_m, *rest = vjp_mix(dxm)
        dseg, dw, rest = rest[:5], rest[5:5 + len(MIX_WEIGHTS)], rest[5 + len(MIX_WEIGHTS):]
        dseg[0] = send([(n, l) for n in MIX_WEIGHTS], dw, dseg[0], "scatter_mix" + tg)
        for n, gr in zip(("conv_w", "q_a_norm", "kv_a_norm", "A_log", "dt_bias", "gdn_norm"), rest):
            dsmall[n][l] = gr
        dx_i, dmod_i, dsmall["norm_mix"][l], dw_in = vjp_in(tuple(dseg))
        g = dx_i + dx_m
        if l > 0:
            g = send([("w_in", l)], [dw_in], g, "scatter_in" + tg)
        dmods[l] = dmod_f + dmod_m + dmod_i
    dx = g
    dmods = jnp.concatenate(dmods, axis=0)
    dconv = jnp.stack(dsmall.pop("conv_w"), axis=1)
    dsmall = {n: jnp.concatenate(v, axis=0) if v[0].ndim == 2 else jnp.stack(v)
              for n, v in dsmall.items() if v[0] is not None}
    dsmall["b_ada"] = dmods
    dsmall["final_norm"] = dfn[0]

    dmod_cols = dmods.reshape(DEPTH, N_DEV, n_ada).transpose(1, 0, 2)
    conv_parts, dmod_all, small_parts = exchange(
        [dconv, dmod_cols, _flat_row([dsmall[n] for n in SMALL])], ["scatter", "scatter", "gather"],
        "exchange_small")
    dmod_all = send([("w_in", 0)], [dw_in], dmod_all, "scatter_in0")

    res = {}
    dm_rows = jnp.pad(dmod_all, ((0, 16 - N_DEV), (0, 0), (0, 0)))
    g_ada = [_mm(c_rows, dm_rows[:, l], "tn", F32, "ada_dw%d" % l, a_act="silu")[None] for l in range(DEPTH)]
    r2 = (DEPTH * d, n_ada)
    outs = adamw(w_ada.reshape(r2), m_w_ada.reshape(r2), v_w_ada.reshape(r2), parts=g_ada, name="adamw_w_ada")
    res["w_ada"] = [o.reshape(w_ada.shape) for o in outs]
    packed = SMALL + ("conv_w",)
    p_all = jnp.concatenate([small_parts, conv_parts.reshape(N_DEV, 1, -1)], axis=2)
    pack = lambda pre: jnp.concatenate([_flat_row([given[pre + n] for n in SMALL]),
                                        given[pre + "conv_w"].reshape(1, -1)], axis=1)
    outs = adamw(pack(""), pack("m_"), pack("v_"), parts=[p_all], name="adamw_small")
    done = [res["w_ada"][1], outs[1]]
    for group, gname in ((FFN_WEIGHTS, "ffn"), (MIX_WEIGHTS, "mix"), (("w_in",), "in")):
        ks = [(n, l) for l in reversed(range(DEPTH)) for n in group]
        after = sum(lax.slice(a, (0,) * a.ndim, (1,) * a.ndim).reshape(1, 1) for a in done)
        srcs, lands = copy_wait([sent[k] for k in ks], len(ALL_PEERS), after, "scatter_wait_" + gname)
        parts = {k: with_own(land, lax.dynamic_index_in_dim(src, me, 0, keepdims=False))
                 for k, src, land in zip(ks, srcs, lands)}
        for n in group:
            w = given[n]
            r2 = (w.shape[0] * w.shape[1], w.shape[2])
            res[n] = [o.reshape(w.shape) for o in
                      adamw(w.reshape(r2), given["m_" + n].reshape(r2), given["v_" + n].reshape(r2),
                            parts=[parts[(n, l)] for l in range(DEPTH)], name="adamw_" + n)]
            done.append(res[n][1])
    off = 0
    for n in packed:
        if n == "conv_w":
            off = small_parts.shape[2]
        size = math.prod(given[n].shape)
        res[n] = [o[0, off:off + size].reshape(given[n].shape) for o in outs]
        off += size

    return (loss, dx[None]) + tuple(res[n][i] for i in range(4) for n in WEIGHTS)
```

```python
import functools
import math

import jax
import jax.numpy as jnp
from jax import lax
from jax.experimental import pallas as pl
from jax.experimental.pallas import tpu as pltpu

F32 = jnp.float32
BF16 = jnp.bfloat16

MLA_HEADS = 8
QK_NOPE = 128
QK_ROPE = 64
V_HEAD = 128
Q_LORA = 512
KV_LORA = 512
ROPE_THETA = 10000.0
GDN_HEADS = 8
GDN_DK = 128
GDN_DV = 128
CONV_WIDTH = 4
CHUNK = 64
DEPTH = 2
EPS = 1e-6
ADAM_LR = 0.001
ADAM_B1 = 0.9
ADAM_B2 = 0.999
ADAM_EPS = 1e-08
ADAM_WD = 0.01
ADAM_STEP = 10

N_DEV = 8
AXES = ("x", "y", "c")
LANE = 128
VMEM_LIMIT = 48 * 1024 * 1024
MM_VMEM_BUDGET = 36 * 1024 * 1024

NN = (((1,), (0,)), ((), ()))
NT = (((1,), (1,)), ((), ()))
TN = (((0,), (0,)), ((), ()))


def _cp(sem=None):
    return pltpu.CompilerParams(dimension_semantics=sem, vmem_limit_bytes=VMEM_LIMIT)


def _tile(n, cap):
    if n <= cap:
        return n
    for t in range(cap - cap % LANE, 0, -LANE):
        if n % t == 0:
            return t
    return n


def _rows(t, cap=256):
    return cap if t % cap == 0 else t


def _pad_lanes(n):
    return -(-n // LANE) * LANE


def _sigmoid(x):
    return 1.0 / (1.0 + jnp.exp(-x))


def _softplus(x):
    return jnp.maximum(x, 0.0) + jnp.log(1.0 + jnp.exp(-jnp.abs(x)))


def _tile_slot(n, cap):
    t = _tile(n, cap)
    return n if t < 256 < n <= 1536 else t


def _mm(a, b, dims, out_dtype, name, a_act=None, slots=False, resid=None):
    if dims == "nn":
        m, k = a.shape
        n = b.shape[-1] * (N_DEV if slots else 1)
    elif dims == "nt":
        m, k = a.shape
        n = b.shape[-2]
    else:
        k, m = a.shape
        n = b.shape[-1]
    tm = _tile(m, 1536)
    tn = _tile_slot(n // N_DEV, 512) if slots and dims != "nt" else _tile(n, 512 if resid else 1024)
    k_slot = k // N_DEV if slots and dims == "nt" else k

    def vmem_bytes(tk_):
        a_b, b_b = tm * tk_ * a.dtype.itemsize, tk_ * tn * b.dtype.itemsize
        casts = (tm * tk_ * 2 if a.dtype != BF16 else 0) + (tk_ * tn * 2 if b.dtype != BF16 else 0)
        return 2 * (a_b + b_b + tm * tn * jnp.dtype(out_dtype).itemsize) + 2 * tm * tn * 4 + casts

    tk = _tile_slot(k_slot, 1536) if slots and dims == "nt" else _tile(k, 2048)
    while vmem_bytes(tk) > MM_VMEM_BUDGET and tk % (2 * LANE) == 0:
        tk //= 2
    nk = k // tk
    per_n = (n // N_DEV) // tn if slots else 1
    per_k = (k // N_DEV) // tk if slots else 1
    if dims == "tn":
        a_spec = pl.BlockSpec((tk, tm), lambda i, j, kk: (kk, i))
    else:
        a_spec = pl.BlockSpec((tm, tk), lambda i, j, kk: (i, kk))
    if dims == "nt":
        if slots:
            b_spec = pl.BlockSpec((None, tn, tk), lambda i, j, kk: (kk // per_k, j, kk % per_k))
        else:
            b_spec = pl.BlockSpec((tn, tk), lambda i, j, kk: (j, kk))
    elif dims == "nn" and slots:
        b_spec = pl.BlockSpec((None, tk, tn), lambda i, j, kk: (j // per_n, kk, j % per_n))
    else:
        b_spec = pl.BlockSpec((tk, tn), lambda i, j, kk: (kk, j))
    if dims == "tn" and slots:
        out_spec = pl.BlockSpec((None, tm, tn), lambda i, j, kk: (j // per_n, i, j % per_n))
        out_shape = jax.ShapeDtypeStruct((N_DEV, m, n // N_DEV), out_dtype)
    else:
        out_spec = pl.BlockSpec((tm, tn), lambda i, j, kk: (i, j))
        out_shape = jax.ShapeDtypeStruct((m, n), out_dtype)
    dn = {"nn": NN, "nt": NT, "tn": TN}[dims]

    def product(a_ref, b_ref):
        av = a_ref[...]
        if a_act == "silu":
            av = av * _sigmoid(av)
        return lax.dot_general(av.astype(BF16), b_ref[...].astype(BF16), dn, preferred_element_type=F32)

    def finish(acc, rest):
        if resid is None:
            (o_ref,) = rest
            o_ref[...] = acc.astype(o_ref.dtype)
        else:
            x_ref, gt_ref, o_ref, p_ref = rest
            o_ref[...] = x_ref[...] + gt_ref[...] * acc
            p_ref[...] = acc.astype(p_ref.dtype)

    def body_one(a_ref, b_ref, *rest):
        finish(product(a_ref, b_ref), rest)

    def body_acc(a_ref, b_ref, *rest):
        acc_ref = rest[-1]
        kk = pl.program_id(2)

        @pl.when(kk == 0)
        def _():
            acc_ref[...] = jnp.zeros_like(acc_ref)

        acc_ref[...] += product(a_ref, b_ref)

        @pl.when(kk == nk - 1)
        def _():
            finish(acc_ref[...], rest[:-1])

    in_specs, operands = [a_spec, b_spec], [a, b]
    if resid is not None:
        in_specs += [out_spec, pl.BlockSpec((1, tn), lambda i, j, kk: (0, j))]
        operands += list(resid)
        out_spec, out_shape = [out_spec, out_spec], [out_shape, jax.ShapeDtypeStruct((m, n), BF16)]
    return pl.pallas_call(
        body_one if nk == 1 else body_acc, name=name, grid=(m // tm, n // tn, nk),
        in_specs=in_specs, out_specs=out_spec, out_shape=out_shape,
        scratch_shapes=[] if nk == 1 else [pltpu.VMEM((tm, tn), F32)],
        compiler_params=_cp(("parallel", "parallel", "arbitrary")),
    )(*operands)


@functools.partial(jax.custom_vjp, nondiff_argnums=(2, 3))
def mm(a, b, tag, out_dtype):
    return _mm(a, b, "nn", out_dtype, "mm_" + tag, slots=b.ndim == 3)


def _mm_f(a, b, tag, out_dtype):
    return mm(a, b, tag, out_dtype), (a, b)


def _mm_b(tag, out_dtype, res, g):
    a, b = res
    slots = b.ndim == 3
    da = _mm(g, b, "nt", a.dtype, "mm_" + tag + "_da", slots=slots)
    db = _mm(a, g, "tn", b.dtype, "mm_" + tag + "_db", slots=slots)
    return da, db


mm.defvjp(_mm_f, _mm_b)


def _norm_fwd_call(x, nw, sc, sh, name):
    t, d = x.shape
    tr = _rows(t)
    mod = sc is not None
    row = pl.BlockSpec((tr, d), lambda i: (i, 0))
    vec = pl.BlockSpec((1, d), lambda i: (0, 0))

    def body(*refs):
        if mod:
            x_ref, nw_ref, sc_ref, sh_ref, o_ref = refs
        else:
            x_ref, nw_ref, o_ref = refs
        xv = x_ref[...].astype(F32)
        r = lax.rsqrt(jnp.mean(xv * xv, axis=-1, keepdims=True) + EPS)
        y = (xv * r) * nw_ref[...]
        if mod:
            y = y * (1.0 + sc_ref[...]) + sh_ref[...]
        o_ref[...] = y.astype(o_ref.dtype)

    args = (x, nw, sc, sh) if mod else (x, nw)
    return pl.pallas_call(
        body, name=name, grid=(t // tr,),
        in_specs=[row] + [vec] * (len(args) - 1), out_specs=row,
        out_shape=jax.ShapeDtypeStruct((t, d), BF16),
        compiler_params=_cp(("parallel",)),
    )(*args)


def _norm_bwd_call(x, nw, sc, dh, name):
    t, d = x.shape
    tr = _rows(t)
    mod = sc is not None
    row = pl.BlockSpec((tr, d), lambda i: (i, 0))
    vec = pl.BlockSpec((1, d), lambda i: (0, 0))

    def body(*refs):
        if mod:
            x_ref, nw_ref, sc_ref, dh_ref, dx_ref, dnw_ref, dsc_ref, dsh_ref = refs
        else:
            x_ref, nw_ref, dh_ref, dx_ref, dnw_ref = refs
        i = pl.program_id(0)
        xv = x_ref[...].astype(F32)
        dh = dh_ref[...].astype(F32)
        r = lax.rsqrt(jnp.mean(xv * xv, axis=-1, keepdims=True) + EPS)
        y = xv * r
        a = nw_ref[...] * (1.0 + sc_ref[...]) if mod else nw_ref[...]
        dy = dh * a
        dx_ref[...] = (r * (dy - y * jnp.mean(dy * y, axis=-1, keepdims=True))).astype(dx_ref.dtype)
        da = jnp.sum(dh * y, axis=0, keepdims=True)

        @pl.when(i == 0)
        def _():
            dnw_ref[...] = jnp.zeros_like(dnw_ref)
            if mod:
                dsc_ref[...] = jnp.zeros_like(dsc_ref)
                dsh_ref[...] = jnp.zeros_like(dsh_ref)

        if mod:
            dnw_ref[...] += da * (1.0 + sc_ref[...])
            dsc_ref[...] += da * nw_ref[...]
            dsh_ref[...] += jnp.sum(dh, axis=0, keepdims=True)
        else:
            dnw_ref[...] += da

    args = (x, nw, sc, dh) if mod else (x, nw, dh)
    n_vec = 3 if mod else 1
    return pl.pallas_call(
        body, name=name, grid=(t // tr,),
        in_specs=[row] + [vec] * (len(args) - 2) + [row],
        out_specs=[row] + [vec] * n_vec,
        out_shape=[jax.ShapeDtypeStruct((t, d), x.dtype)] + [jax.ShapeDtypeStruct((1, d), F32)] * n_vec,
        compiler_params=_cp(("arbitrary",)),
    )(*args)


@functools.partial(jax.custom_vjp, nondiff_argnums=(4,))
def ada_norm(x, nw, sc, sh, tag):
    return _norm_fwd_call(x, nw, sc, sh, "adanorm_" + tag)


def _ada_norm_f(x, nw, sc, sh, tag):
    return _norm_fwd_call(x, nw, sc, sh, "adanorm_" + tag), (x, nw, sc)


def _ada_norm_b(tag, res, dh):
    x, nw, sc = res
    dx, dnw, dsc, dsh = _norm_bwd_call(x, nw, sc, dh, "adanorm_" + tag + "_bwd")
    return dx, dnw, dsc, dsh


ada_norm.defvjp(_ada_norm_f, _ada_norm_b)


@functools.partial(jax.custom_vjp, nondiff_argnums=(2,))
def rms_norm(x, nw, tag):
    return _norm_fwd_call(x, nw, None, None, "rms_" + tag)


def _rms_norm_f(x, nw, tag):
    return _norm_fwd_call(x, nw, None, None, "rms_" + tag), (x, nw)


def _rms_norm_b(tag, res, dh):
    x, nw = res
    dx, dnw = _norm_bwd_call(x, nw, None, dh, "rms_" + tag + "_bwd")
    return dx, dnw


rms_norm.defvjp(_rms_norm_f, _rms_norm_b)


def _gate_mix_fwd_call(gl, ya, yb, name):
    t, d = ya.shape
    tr = _rows(t)
    row = pl.BlockSpec((tr, d), lambda i: (i, 0))

    def body(ga_ref, gb_ref, ya_ref, yb_ref, o_ref):
        o_ref[...] = (_sigmoid(ga_ref[...].astype(F32)) * ya_ref[...].astype(F32)
                      + _sigmoid(gb_ref[...].astype(F32)) * yb_ref[...].astype(F32)).astype(o_ref.dtype)

    return pl.pallas_call(
        body, name=name, grid=(t // tr,),
        in_specs=[row, pl.BlockSpec((tr, d), lambda i: (i, 1)), row, row], out_specs=row,
        out_shape=jax.ShapeDtypeStruct((t, d), BF16),
        compiler_params=_cp(("parallel",)),
    )(gl, gl, ya, yb)


def _gate_mix_bwd_call(gl, ya, yb, dm, name):
    t, d = ya.shape
    tr = _rows(t)
    row = pl.BlockSpec((tr, d), lambda i: (i, 0))
    wide = pl.BlockSpec((tr, 2 * d), lambda i: (i, 0))

    def body(gl_ref, ya_ref, yb_ref, dm_ref, dgl_ref, dya_ref, dyb_ref):
        dm = dm_ref[...].astype(F32)
        ga = _sigmoid(gl_ref[:, :d].astype(F32))
        gb = _sigmoid(gl_ref[:, d:].astype(F32))
        dya_ref[...] = (dm * ga).astype(dya_ref.dtype)
        dyb_ref[...] = (dm * gb).astype(dyb_ref.dtype)
        dgl_ref[:, :d] = (dm * ya_ref[...].astype(F32) * ga * (1.0 - ga)).astype(dgl_ref.dtype)
        dgl_ref[:, d:] = (dm * yb_ref[...].astype(F32) * gb * (1.0 - gb)).astype(dgl_ref.dtype)

    return pl.pallas_call(
        body, name=name, grid=(t // tr,),
        in_specs=[wide, row, row, row], out_specs=[wide, row, row],
        out_shape=[jax.ShapeDtypeStruct((t, 2 * d), gl.dtype), jax.ShapeDtypeStruct((t, d), ya.dtype),
                   jax.ShapeDtypeStruct((t, d), yb.dtype)],
        compiler_params=_cp(("parallel",)),
    )(gl, ya, yb, dm)


@functools.partial(jax.custom_vjp, nondiff_argnums=(3,))
def gate_mix(gl, ya, yb, tag):
    return _gate_mix_fwd_call(gl, ya, yb, "gatemix_" + tag)


def _gate_mix_f(gl, ya, yb, tag):
    return _gate_mix_fwd_call(gl, ya, yb, "gatemix_" + tag), (gl, ya, yb)


def _gate_mix_b(tag, res, dm):
    return tuple(_gate_mix_bwd_call(*res, dm, "gatemix_" + tag + "_bwd"))


gate_mix.defvjp(_gate_mix_f, _gate_mix_b)


def _resid_bwd_call(gt, p, g, name):
    t, d = p.shape
    tr = _rows(t)
    row = pl.BlockSpec((tr, d), lambda i: (i, 0))
    vec = pl.BlockSpec((1, d), lambda i: (0, 0))

    def body(gt_ref, p_ref, g_ref, dp_ref, dgt_ref):
        i = pl.program_id(0)
        g = g_ref[...]
        dp_ref[...] = (g * gt_ref[...]).astype(dp_ref.dtype)

        @pl.when(i == 0)
        def _():
            dgt_ref[...] = jnp.zeros_like(dgt_ref)

        dgt_ref[...] += jnp.sum(g * p_ref[...].astype(F32), axis=0, keepdims=True)

    return pl.pallas_call(
        body, name=name, grid=(t // tr,), in_specs=[vec, row, row], out_specs=[row, vec],
        out_shape=[jax.ShapeDtypeStruct((t, d), BF16), jax.ShapeDtypeStruct((1, d), F32)],
        compiler_params=_cp(("arbitrary",)),
    )(gt, p, g)


@functools.partial(jax.custom_vjp, nondiff_argnums=(4,))
def mm_resid(x, gt, a, b, tag):
    return _mm(a, b, "nn", F32, "mmres_" + tag, resid=(x, gt))[0]


def _mm_resid_f(x, gt, a, b, tag):
    o, p = _mm(a, b, "nn", F32, "mmres_" + tag, resid=(x, gt))
    return o, (gt, a, b, p)


def _mm_resid_b(tag, res, g):
    gt, a, b, p = res
    dp, dgt = _resid_bwd_call(gt, p, g, "mmres_" + tag + "_gate")
    da = _mm(dp, b, "nt", a.dtype, "mmres_" + tag + "_da")
    db = _mm(a, dp, "tn", b.dtype, "mmres_" + tag + "_db")
    return g, dgt, da, db


mm_resid.defvjp(_mm_resid_f, _mm_resid_b)


def _swiglu_fwd_call(gu, name):
    t, f2 = gu.shape
    f = f2 // 2
    tr = _rows(t, 128)
    half = pl.BlockSpec((tr, f), lambda i: (i, 0))

    def body(g_ref, u_ref, o_ref):
        g = g_ref[...].astype(F32)
        o_ref[...] = (g * _sigmoid(g) * u_ref[...].astype(F32)).astype(o_ref.dtype)

    return pl.pallas_call(
        body, name=name, grid=(t // tr,),
        in_specs=[half, pl.BlockSpec((tr, f), lambda i: (i, 1))], out_specs=half,
        out_shape=jax.ShapeDtypeStruct((t, f), BF16), compiler_params=_cp(("parallel",)),
    )(gu, gu)


def _swiglu_bwd_call(gu, da, name):
    t, f2 = gu.shape
    f = f2 // 2
    tr = _rows(t, 128)
    wide = pl.BlockSpec((tr, f2), lambda i: (i, 0))

    def body(gu_ref, da_ref, dgu_ref):
        g = gu_ref[:, :f].astype(F32)
        u = gu_ref[:, f:].astype(F32)
        da = da_ref[...].astype(F32)
        s = _sigmoid(g)
        dgu_ref[:, :f] = (da * u * s * (1.0 + g * (1.0 - s))).astype(dgu_ref.dtype)
        dgu_ref[:, f:] = (da * g * s).astype(dgu_ref.dtype)

    return pl.pallas_call(
        body, name=name, grid=(t // tr,),
        in_specs=[wide, pl.BlockSpec((tr, f), lambda i: (i, 0))], out_specs=wide,
        out_shape=jax.ShapeDtypeStruct((t, f2), gu.dtype), compiler_params=_cp(("parallel",)),
    )(gu, da)


@functools.partial(jax.custom_vjp, nondiff_argnums=(1,))
def swiglu(gu, tag):
    return _swiglu_fwd_call(gu, "swiglu_" + tag)


def _swiglu_f(gu, tag):
    return _swiglu_fwd_call(gu, "swiglu_" + tag), (gu,)


def _swiglu_b(tag, res, da):
    return (_swiglu_bwd_call(res[0], da, "swiglu_" + tag + "_bwd"),)


swiglu.defvjp(_swiglu_f, _swiglu_b)


def loss_head(x, fw, tgt):
    t, d = x.shape
    tr = _rows(t)
    row = pl.BlockSpec((tr, d), lambda i: (i, 0))
    vec = pl.BlockSpec((1, d), lambda i: (0, 0))
    tile = pl.BlockSpec((8, LANE), lambda i: (0, 0))

    def body(x_ref, fw_ref, tgt_ref, loss_ref, dx_ref, dfw_ref):
        i = pl.program_id(0)
        xv = x_ref[...]
        fw = fw_ref[...]
        r = lax.rsqrt(jnp.mean(xv * xv, axis=-1, keepdims=True) + EPS)
        yh = xv * r
        e = yh * fw - tgt_ref[...]
        dy = e * (1.0 / d)
        dyw = dy * fw
        dx_ref[...] = r * (dyw - yh * jnp.mean(dyw * yh, axis=-1, keepdims=True))

        @pl.when(i == 0)
        def _():
            loss_ref[...] = jnp.zeros_like(loss_ref)
            dfw_ref[...] = jnp.zeros_like(dfw_ref)

        loss_ref[...] += 0.5 * jnp.sum(jnp.mean(e * e, axis=-1, keepdims=True))
        dfw_ref[...] += jnp.sum(dy * yh, axis=0, keepdims=True)

    return pl.pallas_call(
        body, name="loss_head", grid=(t // tr,), in_specs=[row, vec, row],
        out_specs=[tile, row, vec],
        out_shape=[jax.ShapeDtypeStruct((8, LANE), F32), jax.ShapeDtypeStruct((t, d), F32),
                   jax.ShapeDtypeStruct((1, d), F32)],
        compiler_params=_cp(("arbitrary",)),
    )(x, fw, tgt)


def _attn_scores(qn_ref, qr_ref, kn_ref, kr_ref, diag):
    tq = qn_ref.shape[0]
    s = lax.dot_general(qn_ref[...].astype(BF16), kn_ref[...].astype(BF16), NT, preferred_element_type=F32)
    s += lax.dot_general(qr_ref[...].astype(BF16), kr_ref[...].astype(BF16), NT, preferred_element_type=F32)
    s = s * (QK_NOPE + QK_ROPE) ** -0.5
    if diag:
        rows = lax.broadcasted_iota(jnp.int32, (tq, tq), 0)
        cols = lax.broadcasted_iota(jnp.int32, (tq, tq), 1)
        s = jnp.where(cols <= rows, s, -1e30)
    return s


def _attn_fwd_call(qn, qr, kv, kr, name):
    t = qn.shape[0]
    h_n = MLA_HEADS
    tq = _rows(t, 512)
    nq = t // tq
    assert V_HEAD == LANE and tq % LANE == 0

    def body(qn_ref, qr_ref, kn_ref, v_ref, kr_ref, o_ref, lse_ref, m_scr, l_scr, acc_scr):
        i, j = pl.program_id(1), pl.program_id(2)

        @pl.when(j == 0)
        def _():
            m_scr[...] = jnp.full_like(m_scr, -1e30)
            l_scr[...] = jnp.zeros_like(l_scr)
            acc_scr[...] = jnp.zeros_like(acc_scr)

        def step(diag):
            s = _attn_scores(qn_ref, qr_ref, kn_ref, kr_ref, diag)
            m_old = m_scr[...]
            m_new = jnp.maximum(m_old, jnp.max(s, axis=-1, keepdims=True))
            p = jnp.exp(s - jnp.tile(m_new, (1, tq // LANE)))
            alpha = jnp.exp(m_old - m_new)
            l_scr[...] = alpha * l_scr[...] + jnp.sum(p, axis=-1, keepdims=True)
            acc_scr[...] = alpha * acc_scr[...] + jnp.dot(p.astype(BF16), v_ref[...].astype(BF16),
                                                           preferred_element_type=F32)
            m_scr[...] = m_new

        @pl.when(j < i)
        def _():
            step(False)

        @pl.when(j == i)
        def _():
            step(True)
            o_ref[...] = (acc_scr[...] / l_scr[...]).astype(o_ref.dtype)
            lse_ref[...] = (m_scr[...] + jnp.log(l_scr[...]))[:, :1]

    return pl.pallas_call(
        body, name=name, grid=(h_n, nq, nq),
        in_specs=[
            pl.BlockSpec((tq, QK_NOPE), lambda h, i, j: (i, h)),
            pl.BlockSpec((None, tq, QK_ROPE), lambda h, i, j: (h, i, 0)),
            pl.BlockSpec((tq, QK_NOPE), lambda h, i, j: (jnp.minimum(j, i), 2 * h)),
            pl.BlockSpec((tq, V_HEAD), lambda h, i, j: (jnp.minimum(j, i), 2 * h + 1)),
            pl.BlockSpec((tq, QK_ROPE), lambda h, i, j: (jnp.minimum(j, i), 0)),
        ],
        out_specs=[
            pl.BlockSpec((tq, V_HEAD), lambda h, i, j: (i, h)),
            pl.BlockSpec((None, tq, 1), lambda h, i, j: (h, i, 0)),
        ],
        out_shape=[jax.ShapeDtypeStruct((t, h_n * V_HEAD), BF16),
                   jax.ShapeDtypeStruct((h_n, t, 1), F32)],
        scratch_shapes=[pltpu.VMEM((tq, LANE), F32), pltpu.VMEM((tq, LANE), F32),
                        pltpu.VMEM((tq, V_HEAD), F32)],
        compiler_params=_cp(("parallel", "parallel", "arbitrary")),
    )(qn, qr, kv, kv, kr)


def _attn_bwd_call(qn, qr, kv, kr, o, lse, do, name):
    t = qn.shape[0]
    h_n = MLA_HEADS
    tq = _rows(t, 512)
    nq = t // tq
    scale = (QK_NOPE + QK_ROPE) ** -0.5

    def body(qn_ref, qr_ref, kn_ref, v_ref, kr_ref, o_ref, lse_ref, do_ref,
             dqn_ref, dqr_ref, dkv_ref, dkr_ref, dqn_scr, dqr_scr, dkn_scr, dv_scr, dkr_scr):
        j, i = pl.program_id(1), pl.program_id(2)

        @pl.when(jnp.logical_and(j == 0, i == 0))
        def _():
            dqn_scr[...] = jnp.zeros_like(dqn_scr)
            dqr_scr[...] = jnp.zeros_like(dqr_scr)

        @pl.when(i == 0)
        def _():
            dkn_scr[...] = jnp.zeros_like(dkn_scr)
            dv_scr[...] = jnp.zeros_like(dv_scr)
            dkr_scr[...] = jnp.zeros_like(dkr_scr)

        def step(diag):
            qn_b = qn_ref[...].astype(BF16)
            qr_b = qr_ref[...].astype(BF16)
            kn_b = kn_ref[...].astype(BF16)
            kr_b = kr_ref[...].astype(BF16)
            do_b = do_ref[...]
            p = jnp.exp(_attn_scores(qn_ref, qr_ref, kn_ref, kr_ref, diag) - lse_ref[...])
            delta = jnp.sum(do_b.astype(F32) * o_ref[...].astype(F32), axis=-1, keepdims=True)
            dp = lax.dot_general(do_b, v_ref[...].astype(BF16), NT, preferred_element_type=F32)
            ds = (p * (dp - delta) * scale).astype(BF16)
            p_b = p.astype(BF16)
            dv_scr[...] += lax.dot_general(p_b, do_b, TN, preferred_element_type=F32)
            dkn_scr[...] += lax.dot_general(ds, qn_b, TN, preferred_element_type=F32)
            dkr_scr[...] += lax.dot_general(ds, qr_b, TN, preferred_element_type=F32)
            sl = pl.ds(pl.multiple_of(i * tq, tq), tq)
            dqn_scr[sl, :] += jnp.dot(ds, kn_b, preferred_element_type=F32)
            dqr_scr[sl, :] += jnp.dot(ds, kr_b, preferred_element_type=F32)

        @pl.when(i > j)
        def _():
            step(False)

        @pl.when(i == j)
        def _():
            step(True)

        @pl.when(i == nq - 1)
        def _():
            dkv_ref[:, :QK_NOPE] = dkn_scr[...].astype(dkv_ref.dtype)
            dkv_ref[:, QK_NOPE:] = dv_scr[...].astype(dkv_ref.dtype)
            dkr_ref[...] = dkr_scr[...]

        @pl.when(jnp.logical_and(j == nq - 1, i == nq - 1))
        def _():
            dqn_ref[...] = dqn_scr[...].astype(dqn_ref.dtype)
            dqr_ref[...] = dqr_scr[...].astype(dqr_ref.dtype)

    qi = lambda j, i: jnp.maximum(i, j)
    return pl.pallas_call(
        body, name=name, grid=(h_n, nq, nq),
        in_specs=[
            pl.BlockSpec((tq, QK_NOPE), lambda h, j, i: (qi(j, i), h)),
            pl.BlockSpec((None, tq, QK_ROPE), lambda h, j, i: (h, qi(j, i), 0)),
            pl.BlockSpec((tq, QK_NOPE), lambda h, j, i: (j, 2 * h)),
            pl.BlockSpec((tq, V_HEAD), lambda h, j, i: (j, 2 * h + 1)),
            pl.BlockSpec((tq, QK_ROPE), lambda h, j, i: (j, 0)),
            pl.BlockSpec((tq, V_HEAD), lambda h, j, i: (qi(j, i), h)),
            pl.BlockSpec((None, tq, 1), lambda h, j, i: (h, qi(j, i), 0)),
            pl.BlockSpec((tq, V_HEAD), lambda h, j, i: (qi(j, i), h)),
        ],
        out_specs=[
            pl.BlockSpec((t, QK_NOPE), lambda h, j, i: (0, h)),
            pl.BlockSpec((None, t, QK_ROPE), lambda h, j, i: (h, 0, 0)),
            pl.BlockSpec((tq, QK_NOPE + V_HEAD), lambda h, j, i: (j, h)),
            pl.BlockSpec((None, tq, QK_ROPE), lambda h, j, i: (h, j, 0)),
        ],
        out_shape=[jax.ShapeDtypeStruct((t, h_n * QK_NOPE), qn.dtype),
                   jax.ShapeDtypeStruct((h_n, t, QK_ROPE), qr.dtype),
                   jax.ShapeDtypeStruct((t, h_n * (QK_NOPE + V_HEAD)), kv.dtype),
                   jax.ShapeDtypeStruct((h_n, t, QK_ROPE), F32)],
        scratch_shapes=[pltpu.VMEM((t, QK_NOPE), F32), pltpu.VMEM((t, QK_ROPE), F32),
                        pltpu.VMEM((tq, QK_NOPE), F32), pltpu.VMEM((tq, V_HEAD), F32),
                        pltpu.VMEM((tq, QK_ROPE), F32)],
        compiler_params=_cp(("parallel", "arbitrary", "arbitrary")),
    )(qn, qr, kv, kv, kr, o, lse, do)


@functools.partial(jax.custom_vjp, nondiff_argnums=(4,))
def attention(qn, qr, kv, kr, tag):
    return _attn_fwd_call(qn, qr, kv, kr, "attn_" + tag)[0]


def _attention_f(qn, qr, kv, kr, tag):
    o, lse = _attn_fwd_call(qn, qr, kv, kr, "attn_" + tag)
    return o, (qn, qr, kv, kr, o, lse)


def _attention_b(tag, res, do):
    dqn, dqr, dkv, dkr_h = _attn_bwd_call(*res, do, "attn_" + tag + "_bwd")
    return dqn, dqr, dkv, jnp.sum(dkr_h, axis=0).astype(res[3].dtype)


attention.defvjp(_attention_f, _attention_b)


def _shift_down(u, s):
    if s == 0:
        return u
    t = u.shape[0]
    rolled = pltpu.roll(u, s, 0)
    return jnp.where(lax.broadcasted_iota(jnp.int32, u.shape, 0) >= s, rolled, 0.0)


def _shift_up(u, s):
    if s == 0:
        return u
    t = u.shape[0]
    rolled = pltpu.roll(u, t - s, 0)
    return jnp.where(lax.broadcasted_iota(jnp.int32, u.shape, 0) < t - s, rolled, 0.0)


def _conv_blocks(t, c3):
    p = c3 // 3
    tc = _tile(p, 512)
    per = p // tc
    return p, tc, per


def _conv_fwd_call(u, w, name):
    t, c3 = u.shape
    p, tc, per = _conv_blocks(t, c3)

    def body(u_ref, w_ref, o_ref):
        u = u_ref[...].astype(F32)
        y = jnp.zeros_like(u)
        for j in range(CONV_WIDTH):
            y = y + w_ref[j:j + 1, :] * _shift_down(u, CONV_WIDTH - 1 - j)
        o_ref[...] = y * _sigmoid(y)

    return pl.pallas_call(
        body, name=name, grid=(c3 // tc,),
        in_specs=[pl.BlockSpec((t, tc), lambda cb: (0, cb)),
                  pl.BlockSpec((CONV_WIDTH, tc), lambda cb: (0, cb))],
        out_specs=pl.BlockSpec((None, t, tc), lambda cb: (cb // per, 0, cb % per)),
        out_shape=jax.ShapeDtypeStruct((3, t, p), F32),
        compiler_params=_cp(("parallel",)),
    )(u, w)


def _conv_bwd_call(u, w, do, name):
    t, c3 = u.shape
    p, tc, per = _conv_blocks(t, c3)

    def body(u_ref, w_ref, do_ref, du_ref, dw_ref):
        u = u_ref[...].astype(F32)
        shifted = [_shift_down(u, CONV_WIDTH - 1 - j) for j in range(CONV_WIDTH)]
        y = jnp.zeros_like(u)
        for j in range(CONV_WIDTH):
            y = y + w_ref[j:j + 1, :] * shifted[j]
        s = _sigmoid(y)
        dy = do_ref[...] * s * (1.0 + y * (1.0 - s))
        du = jnp.zeros_like(u)
        for j in range(CONV_WIDTH):
            du = du + w_ref[j:j + 1, :] * _shift_up(dy, CONV_WIDTH - 1 - j)
            dw_ref[j:j + 1, :] = jnp.sum(dy * shifted[j], axis=0, keepdims=True)
        du_ref[...] = du.astype(du_ref.dtype)

    return pl.pallas_call(
        body, name=name, grid=(c3 // tc,),
        in_specs=[pl.BlockSpec((t, tc), lambda cb: (0, cb)),
                  pl.BlockSpec((CONV_WIDTH, tc), lambda cb: (0, cb)),
                  pl.BlockSpec((None, t, tc), lambda cb: (cb // per, 0, cb % per))],
        out_specs=[pl.BlockSpec((t, tc), lambda cb: (0, cb)),
                   pl.BlockSpec((CONV_WIDTH, tc), lambda cb: (0, cb))],
        out_shape=[jax.ShapeDtypeStruct((t, c3), u.dtype), jax.ShapeDtypeStruct((CONV_WIDTH, c3), F32)],
        compiler_params=_cp(("parallel",)),
    )(u, w, do)


@functools.partial(jax.custom_vjp, nondiff_argnums=(2,))
def conv_silu(u, w, tag):
    return _conv_fwd_call(u, w, "conv_" + tag)


def _conv_silu_f(u, w, tag):
    return _conv_fwd_call(u, w, "conv_" + tag), (u, w)


def _conv_silu_b(tag, res, do):
    return tuple(_conv_bwd_call(*res, do, "conv_" + tag + "_bwd"))


conv_silu.defvjp(_conv_silu_f, _conv_silu_b)


BNN = (((2,), (1,)), ((0,), (0,)))
BNT = (((2,), (2,)), ((0,), (0,)))
BTN = (((1,), (1,)), ((0,), (0,)))


def _xdot(a, b, dn=BNN):
    return lax.dot_general(a, b, dn, precision=lax.Precision.HIGHEST, preferred_element_type=F32)


def _bf16_dot(a, b, dn):
    return lax.dot_general(a.astype(BF16), b.astype(BF16), dn, preferred_element_type=F32)


def _dot3(a, b, dn):
    ah, bh = a.astype(BF16), b.astype(BF16)
    al, bl = a - ah.astype(F32), b - bh.astype(F32)
    return _bf16_dot(ah, bh, dn) + (_bf16_dot(ah, bl, dn) + _bf16_dot(al, bh, dn))


def _transposed(dn, a, b, g):
    if dn == BNN:
        return (g, b, BNT), (a, g, BTN)
    if dn == BNT:
        return (g, b, BNN), (g, a, BTN)
    return (b, g, BNT), (a, g, BNN)


@functools.partial(jax.custom_vjp, nondiff_argnums=(2,))
def _hdot(a, b, dn=BNN):
    return _dot3(a, b, dn)


def _hdot_f(a, b, dn):
    return _dot3(a, b, dn), (a, b)


def _hdot_b(dn, res, g):
    da, db = _transposed(dn, *res, g)
    return _dot3(*da), _dot3(*db)


_hdot.defvjp(_hdot_f, _hdot_b)


@functools.partial(jax.custom_vjp, nondiff_argnums=(2,))
def _bdot(a, b, dn=BNN):
    return _bf16_dot(a, b, dn)


def _bdot_f(a, b, dn):
    return _bf16_dot(a, b, dn), (a, b)


def _bdot_b(dn, res, g):
    da, db = _transposed(dn, *res, g)
    return _bf16_dot(*da), _bf16_dot(*db)


_bdot.defvjp(_bdot_f, _bdot_b)


GDN_HEADS_PER_STEP = 8
GDN_HEADS_PER_STEP_BWD = 8


def _gdn_chunk(q, k, v, z, bl, al, a_log, dtb, gn, s):
    b, c = q.shape[0], q.shape[1]
    ri = lax.broadcasted_iota(jnp.int32, (c, c), 0)
    ci = lax.broadcasted_iota(jnp.int32, (c, c), 1)
    lower = (ri >= ci)[None]
    strict = (ri > ci)[None]
    low_incl = jnp.broadcast_to((ri >= ci).astype(F32), (b, c, c))
    up_incl = jnp.broadcast_to((ri <= ci).astype(F32), (b, c, c))
    eye = (ri == ci).astype(F32)[None]

    q = q * lax.rsqrt(jnp.sum(q * q, axis=-1, keepdims=True) + EPS) * (GDN_DK ** -0.5)
    k = k * lax.rsqrt(jnp.sum(k * k, axis=-1, keepdims=True) + EPS)
    beta = _sigmoid(bl)
    g = -jnp.exp(a_log) * _softplus(al + dtb)
    g_w = jnp.broadcast_to(g, (b, c, LANE))
    gc = _xdot(low_incl, g_w)
    gr = _xdot(g_w[:, :, :c], up_incl, BTN)
    diff = gc[:, :, :c] - gr
    decay = jnp.where(lower, jnp.exp(jnp.where(lower, diff, 0.0)), 0.0)
    kb = k * beta
    lmat = jnp.where(strict, _bdot(kb, k, BNT) * decay, 0.0)
    inv = eye - lmat
    pw = lmat
    for _ in range(int(math.log2(c)) - 1):
        pw = _hdot(pw, pw)
        inv = _hdot(inv, eye + pw)
    eg = jnp.exp(gc)
    u = _hdot(inv, v * beta)
    w = _hdot(inv, kb * eg)
    attn = jnp.where(lower, _bdot(q, k, BNT) * decay, 0.0)
    v_new = u - _bdot(w, s)
    o = _bdot(q * eg, s) + _bdot(attn, v_new)
    g_last = jnp.sum(g_w, axis=1, keepdims=True)
    k_dec = k * jnp.exp(g_last - gc)
    s_new = s * jnp.exp(g_last) + _bdot(k_dec, v_new, BTN)
    on = o * lax.rsqrt(jnp.mean(o * o, axis=-1, keepdims=True) + EPS) * gn
    return on * (z * _sigmoid(z)), s_new


def _head_cols(ba, first, count):
    lane = lax.broadcasted_iota(jnp.int32, ba.shape, 1)
    return jnp.stack([jnp.sum(jnp.where(lane == first + j, ba, 0.0), axis=1, keepdims=True)
                      for j in range(count)])


def _gdn_heads(q, k, v, z, ba, a_log, dtb, gn, s):
    h_n = q.shape[0]
    return _gdn_chunk(q, k, v, z, _head_cols(ba, 0, h_n), _head_cols(ba, h_n, h_n), a_log, dtb, gn, s)


def _gdn_specs(n_chunks, hb, rev):
    c = CHUNK
    nn = (lambda n: n_chunks - 1 - n) if rev else (lambda n: n)
    plane = lambda pidx: pl.BlockSpec((None, c, hb * GDN_DK), lambda hg, n: (pidx, nn(n), hg))
    assert hb == GDN_HEADS
    logits = pl.BlockSpec((c, LANE), lambda hg, n: (nn(n), 0))
    scal = pl.BlockSpec((hb, 1, 1), lambda hg, n: (hg, 0, 0))
    zspec = pl.BlockSpec((c, hb * GDN_DV), lambda hg, n: (nn(n), hg))
    gnspec = pl.BlockSpec((1, GDN_DV), lambda hg, n: (0, 0))
    sspec = pl.BlockSpec((hb, None, GDN_DK, GDN_DV), lambda hg, n: (hg, nn(n), 0, 0))
    return plane, logits, scal, zspec, gnspec, sspec


def _heads_per_step(want):
    return math.gcd(want, GDN_HEADS)


def _heads(ref, hb):
    return jnp.stack([ref[:, j * GDN_DK:(j + 1) * GDN_DK] for j in range(hb)])


def _gdn_fwd_call(qkv, z, ba, a_log, dtb, gn, name):
    t = z.shape[0]
    h_n = GDN_HEADS
    hb = _heads_per_step(GDN_HEADS_PER_STEP)
    n_chunks = t // CHUNK
    plane, logits, scal, zspec, gnspec, sspec = _gdn_specs(n_chunks, hb, False)

    def body(q_ref, k_ref, v_ref, z_ref, ba_ref, a_ref, dtb_ref, gn_ref, o_ref, sall_ref, s_scr):
        n = pl.program_id(1)

        @pl.when(n == 0)
        def _():
            s_scr[...] = jnp.zeros_like(s_scr)

        s = s_scr[...]
        sall_ref[...] = s
        o, s_new = _gdn_heads(_heads(q_ref, hb), _heads(k_ref, hb), _heads(v_ref, hb),
                              _heads(z_ref, hb).astype(F32),
                              ba_ref[...], a_ref[...], dtb_ref[...], gn_ref[...], s)
        for j in range(hb):
            o_ref[:, j * GDN_DV:(j + 1) * GDN_DV] = o[j].astype(o_ref.dtype)
        s_scr[...] = s_new

    return pl.pallas_call(
        body, name=name, grid=(h_n // hb, n_chunks),
        in_specs=[plane(0), plane(1), plane(2), zspec, logits, scal, scal, gnspec],
        out_specs=[zspec, sspec],
        out_shape=[jax.ShapeDtypeStruct((t, h_n * GDN_DV), BF16),
                   jax.ShapeDtypeStruct((h_n, n_chunks, GDN_DK, GDN_DV), F32)],
        scratch_shapes=[pltpu.VMEM((hb, GDN_DK, GDN_DV), F32)],
        compiler_params=_cp(("parallel", "arbitrary")),
    )(qkv, qkv, qkv, z, ba, a_log, dtb, gn)


def _gdn_bwd_call(qkv, z, ba, a_log, dtb, gn, sall, do, name):
    t = z.shape[0]
    h_n = GDN_HEADS
    hb = _heads_per_step(GDN_HEADS_PER_STEP_BWD)
    n_chunks = t // CHUNK
    c = CHUNK
    plane, logits, scal, zspec, gnspec, sspec = _gdn_specs(n_chunks, hb, True)
    dplanes = pl.BlockSpec((3, c, hb * GDN_DK), lambda hg, n: (0, n_chunks - 1 - n, hg))
    gnh = pl.BlockSpec((None, 1, GDN_DV), lambda hg, n: (hg, 0, 0))

    def body(q_ref, k_ref, v_ref, z_ref, ba_ref, a_ref, dtb_ref, gn_ref, s_ref, do_ref,
             dqkv_ref, dz_ref, dba_ref, da_ref, ddtb_ref, dgn_ref, ds_scr):
        n = pl.program_id(1)

        @pl.when(n == 0)
        def _():
            ds_scr[...] = jnp.zeros_like(ds_scr)
            da_ref[...] = jnp.zeros_like(da_ref)
            ddtb_ref[...] = jnp.zeros_like(ddtb_ref)
            dgn_ref[...] = jnp.zeros_like(dgn_ref)

        _, vjp = jax.vjp(_gdn_heads, _heads(q_ref, hb), _heads(k_ref, hb), _heads(v_ref, hb),
                         _heads(z_ref, hb).astype(F32),
                         ba_ref[...], a_ref[...], dtb_ref[...], gn_ref[...], s_ref[...])
        dq, dk, dv, dz, dba, da, ddtb, dgn, ds = vjp((_heads(do_ref, hb).astype(F32), ds_scr[...]))
        for j in range(hb):
            hs = slice(j * GDN_DK, (j + 1) * GDN_DK)
            dqkv_ref[0, :, hs] = dq[j]
            dqkv_ref[1, :, hs] = dk[j]
            dqkv_ref[2, :, hs] = dv[j]
            dz_ref[:, hs] = dz[j].astype(dz_ref.dtype)
        dba_ref[...] = dba
        da_ref[...] += da
        ddtb_ref[...] += ddtb
        dgn_ref[...] += dgn
        ds_scr[...] = ds

    return pl.pallas_call(
        body, name=name, grid=(h_n // hb, n_chunks),
        in_specs=[plane(0), plane(1), plane(2), zspec, logits, scal, scal, gnspec, sspec, zspec],
        out_specs=[dplanes, zspec, logits, scal, scal, gnh],
        out_shape=[jax.ShapeDtypeStruct((3, t, h_n * GDN_DK), F32),
                   jax.ShapeDtypeStruct((t, h_n * GDN_DV), z.dtype),
                   jax.ShapeDtypeStruct((t, LANE), F32),
                   jax.ShapeDtypeStruct((h_n, 1, 1), F32), jax.ShapeDtypeStruct((h_n, 1, 1), F32),
                   jax.ShapeDtypeStruct((h_n // hb, 1, GDN_DV), F32)],
        scratch_shapes=[pltpu.VMEM((hb, GDN_DK, GDN_DV), F32)],
        compiler_params=_cp(("parallel", "arbitrary")),
    )(qkv, qkv, qkv, z, ba, a_log, dtb, gn, sall, do)


@functools.partial(jax.custom_vjp, nondiff_argnums=(6,))
def gdn(qkv, z, ba, a_log, dtb, gn, tag):
    return _gdn_fwd_call(qkv, z, ba, a_log, dtb, gn, "gdn_" + tag)[0]


def _gdn_f(qkv, z, ba, a_log, dtb, gn, tag):
    o, sall = _gdn_fwd_call(qkv, z, ba, a_log, dtb, gn, "gdn_" + tag)
    return o, (qkv, z, ba, a_log, dtb, gn, sall)


def _gdn_b(tag, res, do):
    dqkv, dz, dba, da, ddtb, dgn_h = _gdn_bwd_call(*res, do, "gdn_" + tag + "_bwd")
    return dqkv, dz, dba, da, ddtb, jnp.sum(dgn_h, axis=0)


gdn.defvjp(_gdn_f, _gdn_b)


ADAMW_BLOCK = 256 * 1024


def adamw(w, m, v, *, parts, name):
    n_layers = len(parts)
    n_parts, r, c = parts[0].shape
    assert w.shape == (n_layers * r, c), (w.shape, parts[0].shape)
    tr = r
    for cand in (512, 256, 128, 64, 32, 16, 8):
        if r % cand == 0 and cand * c <= ADAMW_BLOCK:
            tr = cand
            break
    nb = r // tr
    blk = pl.BlockSpec((tr, c), lambda l, i: (l * nb + i, 0))
    bc1 = 1.0 - ADAM_B1 ** ADAM_STEP
    bc2 = 1.0 - ADAM_B2 ** ADAM_STEP

    def part_spec(li):
        return pl.BlockSpec((n_parts, tr, c),
                            lambda l, i: (0, jnp.where(l == li, i, jnp.where(l < li, 0, nb - 1)), 0))

    def body(*refs):
        w_ref, p_refs = refs[0], refs[1:1 + n_layers]
        m_ref, v_ref, g_ref, d_ref, mo_ref, vo_ref = refs[1 + n_layers:]
        for li in range(n_layers):
            @pl.when(pl.program_id(0) == li)
            def _(p_ref=p_refs[li]):
                g = p_ref[0].astype(F32)
                for i in range(1, n_parts):
                    g = g + p_ref[i].astype(F32)
                m2 = ADAM_B1 * m_ref[...] + (1.0 - ADAM_B1) * g
                v2 = ADAM_B2 * v_ref[...] + (1.0 - ADAM_B2) * (g * g)
                g_ref[...] = g
                mo_ref[...] = m2
                vo_ref[...] = v2
                d_ref[...] = -ADAM_LR * ((m2 / bc1) / (jnp.sqrt(v2 / bc2) + ADAM_EPS)
                                         + ADAM_WD * w_ref[...])

    return pl.pallas_call(
        body, name=name, grid=(n_layers, nb),
        in_specs=[blk] + [part_spec(li) for li in range(n_layers)] + [blk, blk],
        out_specs=[blk] * 4, out_shape=[jax.ShapeDtypeStruct(w.shape, F32)] * 4,
        compiler_params=_cp(("arbitrary", "arbitrary")),
    )(w, *parts, m, v)


_HBM = pl.BlockSpec(memory_space=pltpu.HBM)
_SEM = pl.BlockSpec(memory_space=pltpu.SEMAPHORE)
_EFFECT = pltpu.SideEffectType.DATAFLOW_SIDE_EFFECTING


def _peer(x, y, c, d):
    px = 1 - x if d & 4 else x
    py = 1 - y if d & 2 else y
    pc = 1 - c if d & 1 else c
    return (px, py, pc), 4 * px + 2 * py + pc


ALL_PEERS = (1, 2, 3, 4, 5, 6, 7)
SIBLING = 1
SAME_CORE_REMOTE = (2, 4, 6)


def copy_start(arrays, mode, carry, name):
    n = len(arrays)
    if mode == "forward":
        lands = []
    else:
        lands = [lax.empty(a.shape if mode == "scatter" else (N_DEV,) + a.shape, a.dtype) for a in arrays]
    n_in = n + len(lands) + 1

    def body(*refs):
        srcs = refs[:n]
        dsts = refs[n:2 * n] if lands else srcs
        sems = refs[n_in:n_in + 2 * n]
        x, y, c = (lax.axis_index(a) for a in AXES)
        me = 4 * x + 2 * y + c
        for k in range(n):
            if mode == "forward":
                sibling, _ = _peer(x, y, c, SIBLING)
                copies = [(srcs[k].at[_peer(x, y, c, d)[1]], dsts[k].at[_peer(x, y, c, d)[1]], sibling)
                          for d in SAME_CORE_REMOTE]
            elif mode == "gather":
                copies = [(srcs[k], dsts[k].at[me], _peer(x, y, c, d)[0]) for d in (SIBLING,) + SAME_CORE_REMOTE]
            else:
                copies = [(srcs[k].at[_peer(x, y, c, d)[1]], dsts[k].at[me], _peer(x, y, c, d)[0])
                          for d in ALL_PEERS]
            for src, dst, peer in copies:
                pltpu.make_async_remote_copy(src_ref=src, dst_ref=dst, send_sem=sems[2 * k],
                                             recv_sem=sems[2 * k + 1], device_id=peer,
                                             device_id_type=pl.DeviceIdType.MESH).start()

    operands = list(arrays) + lands + [carry]
    outs = pl.pallas_call(
        body, name=name,
        out_shape=tuple([pltpu.SemaphoreType.DMA(())] * (2 * n)
                        + [pltpu.HBM(a.shape, a.dtype) for a in operands]),
        in_specs=[_HBM] * n_in,
        out_specs=tuple([_SEM] * (2 * n) + [_HBM] * n_in),
        input_output_aliases={i: 2 * n + i for i in range(n_in)},
        compiler_params=pltpu.CompilerParams(has_side_effects=_EFFECT),
    )(*[pltpu.with_memory_space_constraint(a, pltpu.HBM) for a in operands])
    sems, thru = outs[:2 * n], outs[2 * n:-1]
    handles = [(sems[2 * k], sems[2 * k + 1], thru[k] if lands else None, thru[n + k] if lands else thru[k])
               for k in range(n)]
    return outs[-1], handles


def copy_wait(handles, n_blocks, after, name):
    n = len(handles)
    sems = [s for h in handles for s in h[:2]]
    srcs = [h[2] for h in handles if h[2] is not None]
    lands = [h[3] for h in handles]
    ns = len(srcs)

    def body(*refs):
        dsts = refs[ns:ns + n]
        sem_refs = refs[ns + n:ns + 3 * n]
        x, y, c = (lax.axis_index(a) for a in AXES)
        for k in range(n):
            blocks = dsts[k].at[pl.ds(0, n_blocks)]
            pltpu.make_async_remote_copy(
                src_ref=blocks, dst_ref=blocks, send_sem=sem_refs[2 * k], recv_sem=sem_refs[2 * k + 1],
                device_id=(x, y, c), device_id_type=pl.DeviceIdType.MESH).wait()

    outs = pl.pallas_call(
        body, name=name,
        out_shape=tuple([pltpu.HBM(a.shape, a.dtype) for a in srcs + lands]),
        in_specs=[_HBM] * (ns + n) + [_SEM] * (2 * n) + [pl.BlockSpec(memory_space=pl.ANY)],
        out_specs=tuple([_HBM] * (ns + n)),
        input_output_aliases={i: i for i in range(ns + n)},
        compiler_params=pltpu.CompilerParams(has_side_effects=_EFFECT),
    )(*srcs, *lands, *sems, after)
    return (list(outs[:ns]) if ns else [None] * n), list(outs[ns:])


def exchange(arrays, modes, name):
    n = len(arrays)
    hbm = pl.BlockSpec(memory_space=pltpu.HBM)
    out_shape = [jax.ShapeDtypeStruct(a.shape if md == "scatter" else (N_DEV,) + a.shape, a.dtype)
                 for a, md in zip(arrays, modes)]

    def body(*refs):
        ins, outs = refs[:n], refs[n:2 * n]
        send_sems, recv_sems, local_sems = refs[2 * n:]
        x, y, c = (lax.axis_index(a) for a in AXES)
        me = 4 * x + 2 * y + c

        def src(k, p):
            return ins[k].at[p] if modes[k] == "scatter" else ins[k]

        local = [pltpu.make_async_copy(src(k, me), outs[k].at[me], local_sems.at[k]) for k in range(n)]
        for cp in local:
            cp.start()
        started = []
        for d in range(1, N_DEV):
            px = 1 - x if d & 4 else x
            py = 1 - y if d & 2 else y
            pc = 1 - c if d & 1 else c
            pid = 4 * px + 2 * py + pc
            for k in range(n):
                pltpu.make_async_remote_copy(
                    src_ref=src(k, pid), dst_ref=outs[k].at[me],
                    send_sem=send_sems.at[k, d - 1], recv_sem=recv_sems.at[k, d - 1],
                    device_id=(px, py, pc), device_id_type=pl.DeviceIdType.MESH).start()
                started.append((k, d, pid, (px, py, pc)))
        for k, d, pid, peer in started:
            pltpu.make_async_remote_copy(
                src_ref=src(k, pid), dst_ref=outs[k].at[pid],
                send_sem=send_sems.at[k, d - 1], recv_sem=recv_sems.at[k, d - 1],
                device_id=peer, device_id_type=pl.DeviceIdType.MESH).wait()
        for cp in local:
            cp.wait()

    outs = pl.pallas_call(
        body, name=name, in_specs=[hbm] * n, out_specs=[hbm] * n, out_shape=out_shape,
        scratch_shapes=[pltpu.SemaphoreType.DMA((n, N_DEV - 1)), pltpu.SemaphoreType.DMA((n, N_DEV - 1)),
                        pltpu.SemaphoreType.DMA((n,))],
        compiler_params=pltpu.CompilerParams(has_side_effects=True),
    )(*arrays)
    return list(outs)


BIG = ("w_in", "w_uq", "w_ukv", "w_o_mla", "w_o_gdn", "w_o", "w_gate_up", "w_down")
ROW_SHARDED = ("w_o", "w_down")
SMALL = ("b_ada", "norm_mix", "norm_ffn", "q_a_norm", "kv_a_norm", "A_log", "dt_bias", "gdn_norm",
         "final_norm")
WEIGHTS = ("w_ada", "b_ada", "norm_mix", "norm_ffn", "w_in", "q_a_norm", "kv_a_norm", "w_uq", "w_ukv",
           "w_o_mla", "conv_w", "A_log", "dt_bias", "gdn_norm", "w_o_gdn", "w_o", "w_gate_up", "w_down",
           "final_norm")


def _unslot(g):
    return g.transpose(1, 0, 2).reshape(g.shape[1], -1)


def _cols(g):
    return g if g.shape[-1] % LANE == 0 else _unslot(g)


def _stack_rows(g):
    return g.reshape(-1, g.shape[-1])


def _rope(xv, cos, sin):
    x1, x2 = jnp.split(xv, 2, axis=-1)
    return jnp.concatenate([x1 * cos - x2 * sin, x2 * cos + x1 * sin], axis=-1)


MIX_WEIGHTS = ("w_uq", "w_ukv", "w_o_mla", "w_o_gdn", "w_o")
FFN_WEIGHTS = ("w_gate_up", "w_down")


def _pad_cols(a):
    return jnp.pad(a, ((0, 0), (0, _pad_lanes(a.shape[1]) - a.shape[1])))


def _stage_in(x, mod, nm, w_in_s, tg):
    d = x.shape[1]
    hg = GDN_HEADS
    w_in = _unslot(w_in_s)
    o1 = Q_LORA + KV_LORA + QK_ROPE
    o2 = o1 + 2 * hg * GDN_DK + hg * GDN_DV
    o3 = o2 + hg * GDN_DV
    o4 = o3 + 2 * hg
    h = ada_norm(x, nm, mod[:, d:2 * d], mod[:, :d], "mix" + tg)
    return (mm(h, _pad_cols(w_in[:, :o1]), "in_a" + tg, BF16), mm(h, w_in[:, o1:o2], "in_qkv" + tg, BF16),
            mm(h, w_in[:, o2:o3], "in_z" + tg, BF16), mm(h, _pad_cols(w_in[:, o3:o4]), "in_ba" + tg, F32),
            mm(h, w_in[:, o4:o4 + 2 * d], "in_g" + tg, BF16))


def _stage_mix(x, mod, seg_a, qkv, z, ba, gl, w_uq_s, w_ukv_s, w_o_mla_s, w_o_gdn_s, w_o_s, conv_s,
               qan, kvan, a_log, dtb, gn, cos, sin, tg):
    t, d = x.shape
    hq, hg = MLA_HEADS, GDN_HEADS
    w_uq = _unslot(w_uq_s).reshape(Q_LORA, hq, QK_NOPE + QK_ROPE)
    w_uq = jnp.concatenate([w_uq[:, :, :QK_NOPE].reshape(Q_LORA, hq * QK_NOPE),
                            w_uq[:, :, QK_NOPE:].reshape(Q_LORA, hq * QK_ROPE)], axis=1)
    c_q = seg_a[:, :Q_LORA]
    c_kv = seg_a[:, Q_LORA:Q_LORA + KV_LORA]
    k_pe = seg_a[:, Q_LORA + KV_LORA:Q_LORA + KV_LORA + QK_ROPE]
    qf = mm(rms_norm(c_q, qan, "qa" + tg), w_uq, "uq" + tg, BF16)
    kvf = mm(rms_norm(c_kv, kvan, "kva" + tg), _cols(w_ukv_s), "ukv" + tg, BF16)
    qn = qf[:, :hq * QK_NOPE]
    q_pe = qf[:, hq * QK_NOPE:].astype(F32).reshape(t, hq, QK_ROPE)
    qr = _rope(q_pe, cos[:, None, :], sin[:, None, :]).transpose(1, 0, 2).astype(BF16)
    kr = _rope(k_pe.astype(F32), cos, sin).astype(BF16)
    y_a = mm(attention(qn, qr, kvf, kr, tg), _cols(w_o_mla_s), "o_mla" + tg, BF16)
    conv_w = conv_s.transpose(1, 0, 2).reshape(CONV_WIDTH, -1)
    qkv_c = conv_silu(qkv, conv_w, tg)
    o_gdn = gdn(qkv_c, z, ba, a_log.reshape(hg, 1, 1), dtb.reshape(hg, 1, 1), gn, tg)
    y_b = mm(o_gdn, _cols(w_o_gdn_s), "o_gdn" + tg, BF16)
    return mm_resid(x, mod[:, 2 * d:3 * d], gate_mix(gl, y_a, y_b, tg), _stack_rows(w_o_s), "w_o" + tg)


def _stage_ffn(x, mod, nf, w_gu_s, w_down_s, tg):
    d = x.shape[1]
    h = ada_norm(x, nf, mod[:, 4 * d:5 * d], mod[:, 3 * d:4 * d], "ffn" + tg)
    gu = mm(h, _cols(w_gu_s), "gu" + tg, BF16)
    return mm_resid(x, mod[:, 5 * d:6 * d], swiglu(gu, tg), _stack_rows(w_down_s), "down" + tg)


def _flat_row(arrs):
    v = jnp.concatenate([a.reshape(-1) for a in arrs])
    return jnp.pad(v, (0, _pad_lanes(v.shape[0]) - v.shape[0]))[None, :]


def kernel(x, c, positions, w_ada, b_ada, norm_mix, norm_ffn, w_in, q_a_norm, kv_a_norm, w_uq, w_ukv, w_o_mla, conv_w, A_log, dt_bias, gdn_norm, w_o_gdn, w_o, w_gate_up, w_down, final_norm, loss_target, m_w_ada, m_b_ada, m_norm_mix, m_norm_ffn, m_w_in, m_q_a_norm, m_kv_a_norm, m_w_uq, m_w_ukv, m_w_o_mla, m_conv_w, m_A_log, m_dt_bias, m_gdn_norm, m_w_o_gdn, m_w_o, m_w_gate_up, m_w_down, m_final_norm, v_w_ada, v_b_ada, v_norm_mix, v_norm_ffn, v_w_in, v_q_a_norm, v_kv_a_norm, v_w_uq, v_w_ukv, v_w_o_mla, v_conv_w, v_A_log, v_dt_bias, v_gdn_norm, v_w_o_gdn, v_w_o, v_w_gate_up, v_w_down, v_final_norm):
    given = dict(locals())
    t, d = x.shape[1], x.shape[2]
    n_ada = w_ada.shape[2]
    me = 4 * lax.axis_index("x") + 2 * lax.axis_index("y") + lax.axis_index("c")

    def with_own(land, own):
        return lax.dynamic_update_slice(land, own[None], (me,) + (0,) * own.ndim)

    got = exchange([c, conv_w], ["gather", "gather"], "gather_small")
    c_all, conv_g = got[0].reshape(N_DEV, d), got[1]
    c_rows = jnp.pad(c_all, ((0, 16 - N_DEV), (0, 0)))
    mod_cols = jnp.stack([_mm(c_rows, w_ada[l], "nn", F32, "ada_mod%d" % l, a_act="silu")[:N_DEV]
                          for l in range(DEPTH)], axis=1)
    mod_mine = exchange([mod_cols], ["scatter"], "scatter_mod")[0]
    mods = mod_mine.transpose(1, 0, 2).reshape(DEPTH, N_DEV * n_ada) + b_ada

    groups = [[(n, l) for n in names] for l in range(DEPTH) for names in (("w_in",), MIX_WEIGHTS, FFN_WEIGHTS)]
    gtags = [s + str(l) for l in range(DEPTH) for s in ("in", "mix", "ffn")]
    keys = [k for ks in groups for k in ks]
    mods, handles = copy_start([given[n][l].astype(BF16) for n, l in keys], "gather", mods, "gather_start")
    handles = dict(zip(keys, handles))
    own, relayed = {}, {}

    def relay(gi, carry):
        ks = groups[gi]
        srcs, lands = copy_wait([handles[k] for k in ks], 1 + len(SAME_CORE_REMOTE), carry,
                                "wait_ici_" + gtags[gi])
        own.update(zip(ks, srcs))
        carry, hs = copy_start(lands, "forward", carry, "relay_" + gtags[gi])
        relayed.update(zip(ks, hs))
        return carry

    def landed(gi, after):
        ks = groups[gi]
        _, lands = copy_wait([relayed[k] for k in ks], len(SAME_CORE_REMOTE), after, "wait_" + gtags[gi])
        return [with_own(land, own[k]) for k, land in zip(ks, lands)]

    inv_freq = 1.0 / (ROPE_THETA ** (jnp.arange(0, QK_ROPE, 2, dtype=F32) / QK_ROPE))
    ang = positions[0].astype(F32)[:, None] * inv_freq
    cos, sin = jnp.cos(ang), jnp.sin(ang)
    relay_before = {0: [0], 1: [1], 2: [2, 3], 3: [4], 4: [5], 5: []}

    def weights_for(stage, carry):
        for gi in relay_before[stage]:
            carry = relay(gi, carry)
        return carry, landed(stage, carry)

    xl = x[0]
    vjps = []
    for l in range(DEPTH):
        tg = str(l)
        mod = mods[l:l + 1]
        xl, (w_in_s,) = weights_for(3 * l, xl)
        seg, vjp_in = jax.vjp(lambda *a, tg=tg: _stage_in(*a, tg), xl, mod, norm_mix[l:l + 1], w_in_s)
        seg0, w_mix = weights_for(3 * l + 1, seg[0])
        seg = (seg0,) + tuple(seg[1:])
        xm, vjp_mix = jax.vjp(lambda *a, tg=tg: _stage_mix(*a, cos, sin, tg), xl, mod, *seg, *w_mix,
                              conv_g[:, l], q_a_norm[l:l + 1], kv_a_norm[l:l + 1], A_log[l], dt_bias[l],
                              gdn_norm[l:l + 1])
        xm, w_ffn = weights_for(3 * l + 2, xm)
        xl, vjp_ffn = jax.vjp(lambda *a, tg=tg: _stage_ffn(*a, tg), xm, mod, norm_ffn[l:l + 1], *w_ffn)
        vjps.append((vjp_in, vjp_mix, vjp_ffn))

    loss_t, g, dfn = loss_head(xl, final_norm[None, :], loss_target[0])
    loss = lax.psum(loss_t[0, 0], AXES)
    dsmall = {n: [None] * DEPTH for n in SMALL + ("conv_w",)}
    dmods = [None] * DEPTH
    sent = {}

    def send(ks, grads, carry, name):
        carry, hs = copy_start(list(grads), "scatter", carry, name)
        sent.update(zip(ks, hs))
        return carry

    for l in reversed(range(DEPTH)):
        tg = str(l)
        vjp_in, vjp_mix, vjp_ffn = vjps[l]
        dxm, dmod_f, dsmall["norm_ffn"][l], *dw = vjp_ffn(g)
        dxm = send([(n, l) for n in FFN_WEIGHTS], dw, dxm, "scatter_ffn" + tg)
        dx_m, dmod_m, *rest = vjp_mix(dxm)
        dseg, dw, rest = rest[:5], rest[5:5 + len(MIX_WEIGHTS)], rest[5 + len(MIX_WEIGHTS):]
        dseg[0] = send([(n, l) for n in MIX_WEIGHTS], dw, dseg[0], "scatter_mix" + tg)
        for n, gr in zip(("conv_w", "q_a_norm", "kv_a_norm", "A_log", "dt_bias", "gdn_norm"), rest):
            dsmall[n][l] = gr
        dx_i, dmod_i, dsmall["norm_mix"][l], dw_in = vjp_in(tuple(dseg))
        g = dx_i + dx_m
        if l > 0:
            g = send([("w_in", l)], [dw_in], g, "scatter_in" + tg)
        dmods[l] = dmod_f + dmod_m + dmod_i
    dx = g
    dmods = jnp.concatenate(dmods, axis=0)
    dconv = jnp.stack(dsmall.pop("conv_w"), axis=1)
    dsmall = {n: jnp.concatenate(v, axis=0) if v[0].ndim == 2 else jnp.stack(v)
              for n, v in dsmall.items() if v[0] is not None}
    dsmall["b_ada"] = dmods
    dsmall["final_norm"] = dfn[0]

    dmod_cols = dmods.reshape(DEPTH, N_DEV, n_ada).transpose(1, 0, 2)
    conv_parts, dmod_all, small_parts = exchange(
        [dconv, dmod_cols, _flat_row([dsmall[n] for n in SMALL])], ["scatter", "scatter", "gather"],
        "exchange_small")
    dmod_all = send([("w_in", 0)], [dw_in], dmod_all, "scatter_in0")

    res = {}
    dm_rows = jnp.pad(dmod_all, ((0, 16 - N_DEV), (0, 0), (0, 0)))
    g_ada = [_mm(c_rows, dm_rows[:, l], "tn", F32, "ada_dw%d" % l, a_act="silu")[None] for l in range(DEPTH)]
    r2 = (DEPTH * d, n_ada)
    outs = adamw(w_ada.reshape(r2), m_w_ada.reshape(r2), v_w_ada.reshape(r2), parts=g_ada, name="adamw_w_ada")
    res["w_ada"] = [o.reshape(w_ada.shape) for o in outs]
    packed = SMALL + ("conv_w",)
    p_all = jnp.concatenate([small_parts, conv_parts.reshape(N_DEV, 1, -1)], axis=2)
    pack = lambda pre: jnp.concatenate([_flat_row([given[pre + n] for n in SMALL]),
                                        given[pre + "conv_w"].reshape(1, -1)], axis=1)
    outs = adamw(pack(""), pack("m_"), pack("v_"), parts=[p_all], name="adamw_small")
    done = [res["w_ada"][1], outs[1]]
    for group, gname in ((FFN_WEIGHTS, "ffn"), (MIX_WEIGHTS, "mix"), (("w_in",), "in")):
        ks = [(n, l) for l in reversed(range(DEPTH)) for n in group]
        after = sum(lax.slice(a, (0,) * a.ndim, (1,) * a.ndim).reshape(1, 1) for a in done)
        srcs, lands = copy_wait([sent[k] for k in ks], len(ALL_PEERS), after, "scatter_wait_" + gname)
        parts = {k: with_own(land, lax.dynamic_index_in_dim(src, me, 0, keepdims=False))
                 for k, src, land in zip(ks, srcs, lands)}
        for n in group:
            w = given[n]
            r2 = (w.shape[0] * w.shape[1], w.shape[2])
            res[n] = [o.reshape(w.shape) for o in
                      adamw(w.reshape(r2), given["m_" + n].reshape(r2), given["v_" + n].reshape(r2),
                            parts=[parts[(n, l)] for l in range(DEPTH)], name="adamw_" + n)]
            done.append(res[n][1])
    off = 0
    for n in packed:
        if n == "conv_w":
            off = small_parts.shape[2]
        size = math.prod(given[n].shape)
        res[n] = [o[0, off:off + size].reshape(given[n].shape) for o in outs]
        off += size

    return (loss, dx[None]) + tuple(res[n][i] for i in range(4) for n in WEIGHTS)
```

```python
import functools
import math

import jax
import jax.numpy as jnp
from jax import lax
from jax.experimental import pallas as pl
from jax.experimental.pallas import tpu as pltpu

F32 = jnp.float32
BF16 = jnp.bfloat16

MLA_HEADS = 8
QK_NOPE = 128
QK_ROPE = 64
V_HEAD = 128
Q_LORA = 512
KV_LORA = 512
ROPE_THETA = 10000.0
GDN_HEADS = 8
GDN_DK = 128
GDN_DV = 128
CONV_WIDTH = 4
CHUNK = 64
DEPTH = 2
EPS = 1e-6
ADAM_LR = 0.001
ADAM_B1 = 0.9
ADAM_B2 = 0.999
ADAM_EPS = 1e-08
ADAM_WD = 0.01
ADAM_STEP = 10

N_DEV = 8
AXES = ("x", "y", "c")
LANE = 128
VMEM_LIMIT = 48 * 1024 * 1024
MM_VMEM_BUDGET = 36 * 1024 * 1024

NN = (((1,), (0,)), ((), ()))
NT = (((1,), (1,)), ((), ()))
TN = (((0,), (0,)), ((), ()))


def _cp(sem=None):
    return pltpu.CompilerParams(dimension_semantics=sem, vmem_limit_bytes=VMEM_LIMIT)


def _tile(n, cap):
    if n <= cap:
        return n
    for t in range(cap - cap % LANE, 0, -LANE):
        if n % t == 0:
            return t
    return n


def _rows(t, cap=256):
    return cap if t % cap == 0 else t


def _pad_lanes(n):
    return -(-n // LANE) * LANE


def _sigmoid(x):
    return 1.0 / (1.0 + jnp.exp(-x))


def _softplus(x):
    return jnp.maximum(x, 0.0) + jnp.log(1.0 + jnp.exp(-jnp.abs(x)))


def _tile_slot(n, cap):
    t = _tile(n, cap)
    return n if t < 256 < n <= 1536 else t


NARROW_SLOT = 512


def _mm_narrow_slots(a, b, dims, out_dtype, name):
    bf = lambda v: v.astype(BF16)
    dot = lambda p, q, dn: lax.dot_general(p, q, dn, preferred_element_type=F32)
    if dims == "nn":
        (m, k), per = a.shape, b.shape[-1]
        spb = min(N_DEV, max(1, 1024 // per))
        tm = _tile(m, 1024)

        def body(a_ref, b_ref, o_ref):
            av = bf(a_ref[...])
            for s in range(spb):
                o_ref[:, s * per:(s + 1) * per] = dot(av, bf(b_ref[s]), NN).astype(o_ref.dtype)

        grid = (m // tm, N_DEV // spb)
        in_specs = [pl.BlockSpec((tm, k), lambda i, j: (i, 0)), pl.BlockSpec((spb, k, per), lambda i, j: (j, 0, 0))]
        out_specs = pl.BlockSpec((tm, spb * per), lambda i, j: (i, j))
        out_shape = jax.ShapeDtypeStruct((m, N_DEV * per), out_dtype)
    elif dims == "nt":
        m, (n, per) = a.shape[0], b.shape[1:]
        tm, tn = _tile(m, 1024), _tile(n, 1024)

        def body(a_ref, b_ref, o_ref):
            av = bf(a_ref[...])
            acc = dot(av[:, :per], bf(b_ref[0]), NT)
            for s in range(1, N_DEV):
                acc += dot(av[:, s * per:(s + 1) * per], bf(b_ref[s]), NT)
            o_ref[...] = acc.astype(o_ref.dtype)

        grid = (m // tm, n // tn)
        in_specs = [pl.BlockSpec((tm, N_DEV * per), lambda i, j: (i, 0)),
                    pl.BlockSpec((N_DEV, tn, per), lambda i, j: (0, j, 0))]
        out_specs = pl.BlockSpec((tm, tn), lambda i, j: (i, j))
        out_shape = jax.ShapeDtypeStruct((m, n), out_dtype)
    else:
        (k, m), per = a.shape, b.shape[1] // N_DEV
        spb = min(N_DEV, max(1, 1024 // per))
        tm = _tile(m, 1024)

        def body(a_ref, b_ref, o_ref):
            av = bf(a_ref[...])
            for s in range(spb):
                o_ref[s] = dot(av, bf(b_ref[:, s * per:(s + 1) * per]), TN).astype(o_ref.dtype)

        grid = (m // tm, N_DEV // spb)
        in_specs = [pl.BlockSpec((k, tm), lambda i, j: (0, i)), pl.BlockSpec((k, spb * per), lambda i, j: (0, j))]
        out_specs = pl.BlockSpec((spb, tm, per), lambda i, j: (j, i, 0))
        out_shape = jax.ShapeDtypeStruct((N_DEV, m, per), out_dtype)
    return pl.pallas_call(body, name=name, grid=grid, in_specs=in_specs, out_specs=out_specs,
                          out_shape=out_shape, compiler_params=_cp(("parallel", "parallel")))(a, b)


def _mm(a, b, dims, out_dtype, name, a_act=None, slots=False, resid=None):
    if dims == "nn":
        m, k = a.shape
        n = b.shape[-1] * (N_DEV if slots else 1)
    elif dims == "nt":
        m, k = a.shape
        n = b.shape[-2]
    else:
        k, m = a.shape
        n = b.shape[-1]
    per = (k if dims == "nt" else n) // N_DEV
    if slots and per % LANE == 0 and per <= NARROW_SLOT and k <= 2048 and a_act is None and resid is None:
        return _mm_narrow_slots(a, b, dims, out_dtype, name)
    tm = _tile(m, 1536)
    tn = _tile_slot(n // N_DEV, 512) if slots and dims != "nt" else _tile(n, 512 if resid else 1024)
    k_slot = k // N_DEV if slots and dims == "nt" else k

    def vmem_bytes(tk_):
        a_b, b_b = tm * tk_ * a.dtype.itemsize, tk_ * tn * b.dtype.itemsize
        casts = (tm * tk_ * 2 if a.dtype != BF16 else 0) + (tk_ * tn * 2 if b.dtype != BF16 else 0)
        return 2 * (a_b + b_b + tm * tn * jnp.dtype(out_dtype).itemsize) + 2 * tm * tn * 4 + casts

    tk = _tile_slot(k_slot, 1536) if slots and dims == "nt" else _tile(k, 2048)
    while vmem_bytes(tk) > MM_VMEM_BUDGET and tk % (2 * LANE) == 0:
        tk //= 2
    nk = k // tk
    per_n = (n // N_DEV) // tn if slots else 1
    per_k = (k // N_DEV) // tk if slots else 1
    if dims == "tn":
        a_spec = pl.BlockSpec((tk, tm), lambda i, j, kk: (kk, i))
    else:
        a_spec = pl.BlockSpec((tm, tk), lambda i, j, kk: (i, kk))
    if dims == "nt":
        if slots:
            b_spec = pl.BlockSpec((None, tn, tk), lambda i, j, kk: (kk // per_k, j, kk % per_k))
        else:
            b_spec = pl.BlockSpec((tn, tk), lambda i, j, kk: (j, kk))
    elif dims == "nn" and slots:
        b_spec = pl.BlockSpec((None, tk, tn), lambda i, j, kk: (j // per_n, kk, j % per_n))
    else:
        b_spec = pl.BlockSpec((tk, tn), lambda i, j, kk: (kk, j))
    if dims == "tn" and slots:
        out_spec = pl.BlockSpec((None, tm, tn), lambda i, j, kk: (j // per_n, i, j % per_n))
        out_shape = jax.ShapeDtypeStruct((N_DEV, m, n // N_DEV), out_dtype)
    else:
        out_spec = pl.BlockSpec((tm, tn), lambda i, j, kk: (i, j))
        out_shape = jax.ShapeDtypeStruct((m, n), out_dtype)
    dn = {"nn": NN, "nt": NT, "tn": TN}[dims]

    def product(a_ref, b_ref):
        av = a_ref[...]
        if a_act == "silu":
            av = av * _sigmoid(av)
        return lax.dot_general(av.astype(BF16), b_ref[...].astype(BF16), dn, preferred_element_type=F32)

    def finish(acc, rest):
        if resid is None:
            (o_ref,) = rest
            o_ref[...] = acc.astype(o_ref.dtype)
        else:
            x_ref, gt_ref, o_ref, p_ref = rest
            o_ref[...] = x_ref[...] + gt_ref[...] * acc
            p_ref[...] = acc.astype(p_ref.dtype)

    def body_one(a_ref, b_ref, *rest):
        finish(product(a_ref, b_ref), rest)

    def body_acc(a_ref, b_ref, *rest):
        acc_ref = rest[-1]
        kk = pl.program_id(2)

        @pl.when(kk == 0)
        def _():
            acc_ref[...] = jnp.zeros_like(acc_ref)

        acc_ref[...] += product(a_ref, b_ref)

        @pl.when(kk == nk - 1)
        def _():
            finish(acc_ref[...], rest[:-1])

    in_specs, operands = [a_spec, b_spec], [a, b]
    if resid is not None:
        in_specs += [out_spec, pl.BlockSpec((1, tn), lambda i, j, kk: (0, j))]
        operands += list(resid)
        out_spec, out_shape = [out_spec, out_spec], [out_shape, jax.ShapeDtypeStruct((m, n), BF16)]
    return pl.pallas_call(
        body_one if nk == 1 else body_acc, name=name, grid=(m // tm, n // tn, nk),
        in_specs=in_specs, out_specs=out_spec, out_shape=out_shape,
        scratch_shapes=[] if nk == 1 else [pltpu.VMEM((tm, tn), F32)],
        compiler_params=_cp(("parallel", "parallel", "arbitrary")),
    )(*operands)


@functools.partial(jax.custom_vjp, nondiff_argnums=(2, 3))
def mm(a, b, tag, out_dtype):
    return _mm(a, b, "nn", out_dtype, "mm_" + tag, slots=b.ndim == 3)


def _mm_f(a, b, tag, out_dtype):
    return mm(a, b, tag, out_dtype), (a, b)


def _mm_b(tag, out_dtype, res, g):
    a, b = res
    slots = b.ndim == 3
    da = _mm(g, b, "nt", a.dtype, "mm_" + tag + "_da", slots=slots)
    db = _mm(a, g, "tn", b.dtype, "mm_" + tag + "_db", slots=slots)
    return da, db


mm.defvjp(_mm_f, _mm_b)


def _norm_fwd_call(x, nw, sc, sh, name):
    t, d = x.shape
    tr = _rows(t)
    mod = sc is not None
    row = pl.BlockSpec((tr, d), lambda i: (i, 0))
    vec = pl.BlockSpec((1, d), lambda i: (0, 0))

    def body(*refs):
        if mod:
            x_ref, nw_ref, sc_ref, sh_ref, o_ref = refs
        else:
            x_ref, nw_ref, o_ref = refs
        xv = x_ref[...].astype(F32)
        r = lax.rsqrt(jnp.mean(xv * xv, axis=-1, keepdims=True) + EPS)
        y = (xv * r) * nw_ref[...]
        if mod:
            y = y * (1.0 + sc_ref[...]) + sh_ref[...]
        o_ref[...] = y.astype(o_ref.dtype)

    args = (x, nw, sc, sh) if mod else (x, nw)
    return pl.pallas_call(
        body, name=name, grid=(t // tr,),
        in_specs=[row] + [vec] * (len(args) - 1), out_specs=row,
        out_shape=jax.ShapeDtypeStruct((t, d), BF16),
        compiler_params=_cp(("parallel",)),
    )(*args)


def _norm_bwd_call(x, nw, sc, dh, name):
    t, d = x.shape
    tr = _rows(t)
    mod = sc is not None
    row = pl.BlockSpec((tr, d), lambda i: (i, 0))
    vec = pl.BlockSpec((1, d), lambda i: (0, 0))

    def body(*refs):
        if mod:
            x_ref, nw_ref, sc_ref, dh_ref, dx_ref, dnw_ref, dsc_ref, dsh_ref = refs
        else:
            x_ref, nw_ref, dh_ref, dx_ref, dnw_ref = refs
        i = pl.program_id(0)
        xv = x_ref[...].astype(F32)
        dh = dh_ref[...].astype(F32)
        r = lax.rsqrt(jnp.mean(xv * xv, axis=-1, keepdims=True) + EPS)
        y = xv * r
        a = nw_ref[...] * (1.0 + sc_ref[...]) if mod else nw_ref[...]
        dy = dh * a
        dx_ref[...] = (r * (dy - y * jnp.mean(dy * y, axis=-1, keepdims=True))).astype(dx_ref.dtype)
        da = jnp.sum(dh * y, axis=0, keepdims=True)

        @pl.when(i == 0)
        def _():
            dnw_ref[...] = jnp.zeros_like(dnw_ref)
            if mod:
                dsc_ref[...] = jnp.zeros_like(dsc_ref)
                dsh_ref[...] = jnp.zeros_like(dsh_ref)

        if mod:
            dnw_ref[...] += da * (1.0 + sc_ref[...])
            dsc_ref[...] += da * nw_ref[...]
            dsh_ref[...] += jnp.sum(dh, axis=0, keepdims=True)
        else:
            dnw_ref[...] += da

    args = (x, nw, sc, dh) if mod else (x, nw, dh)
    n_vec = 3 if mod else 1
    return pl.pallas_call(
        body, name=name, grid=(t // tr,),
        in_specs=[row] + [vec] * (len(args) - 2) + [row],
        out_specs=[row] + [vec] * n_vec,
        out_shape=[jax.ShapeDtypeStruct((t, d), x.dtype)] + [jax.ShapeDtypeStruct((1, d), F32)] * n_vec,
        compiler_params=_cp(("arbitrary",)),
    )(*args)


@functools.partial(jax.custom_vjp, nondiff_argnums=(4,))
def ada_norm(x, nw, sc, sh, tag):
    return _norm_fwd_call(x, nw, sc, sh, "adanorm_" + tag)


def _ada_norm_f(x, nw, sc, sh, tag):
    return _norm_fwd_call(x, nw, sc, sh, "adanorm_" + tag), (x, nw, sc)


def _ada_norm_b(tag, res, dh):
    x, nw, sc = res
    dx, dnw, dsc, dsh = _norm_bwd_call(x, nw, sc, dh, "adanorm_" + tag + "_bwd")
    return dx, dnw, dsc, dsh


ada_norm.defvjp(_ada_norm_f, _ada_norm_b)


@functools.partial(jax.custom_vjp, nondiff_argnums=(2,))
def rms_norm(x, nw, tag):
    return _norm_fwd_call(x, nw, None, None, "rms_" + tag)


def _rms_norm_f(x, nw, tag):
    return _norm_fwd_call(x, nw, None, None, "rms_" + tag), (x, nw)


def _rms_norm_b(tag, res, dh):
    x, nw = res
    dx, dnw = _norm_bwd_call(x, nw, None, dh, "rms_" + tag + "_bwd")
    return dx, dnw


rms_norm.defvjp(_rms_norm_f, _rms_norm_b)


def _gate_mix_fwd_call(gl, ya, yb, name):
    t, d = ya.shape
    tr = _rows(t)
    row = pl.BlockSpec((tr, d), lambda i: (i, 0))

    def body(ga_ref, gb_ref, ya_ref, yb_ref, o_ref):
        o_ref[...] = (_sigmoid(ga_ref[...].astype(F32)) * ya_ref[...].astype(F32)
                      + _sigmoid(gb_ref[...].astype(F32)) * yb_ref[...].astype(F32)).astype(o_ref.dtype)

    return pl.pallas_call(
        body, name=name, grid=(t // tr,),
        in_specs=[row, pl.BlockSpec((tr, d), lambda i: (i, 1)), row, row], out_specs=row,
        out_shape=jax.ShapeDtypeStruct((t, d), BF16),
        compiler_params=_cp(("parallel",)),
    )(gl, gl, ya, yb)


def _gate_mix_bwd_call(gl, ya, yb, dm, name):
    t, d = ya.shape
    tr = _rows(t)
    row = pl.BlockSpec((tr, d), lambda i: (i, 0))
    wide = pl.BlockSpec((tr, 2 * d), lambda i: (i, 0))

    def body(gl_ref, ya_ref, yb_ref, dm_ref, dgl_ref, dya_ref, dyb_ref):
        dm = dm_ref[...].astype(F32)
        ga = _sigmoid(gl_ref[:, :d].astype(F32))
        gb = _sigmoid(gl_ref[:, d:].astype(F32))
        dya_ref[...] = (dm * ga).astype(dya_ref.dtype)
        dyb_ref[...] = (dm * gb).astype(dyb_ref.dtype)
        dgl_ref[:, :d] = (dm * ya_ref[...].astype(F32) * ga * (1.0 - ga)).astype(dgl_ref.dtype)
        dgl_ref[:, d:] = (dm * yb_ref[...].astype(F32) * gb * (1.0 - gb)).astype(dgl_ref.dtype)

    return pl.pallas_call(
        body, name=name, grid=(t // tr,),
        in_specs=[wide, row, row, row], out_specs=[wide, row, row],
        out_shape=[jax.ShapeDtypeStruct((t, 2 * d), gl.dtype), jax.ShapeDtypeStruct((t, d), ya.dtype),
                   jax.ShapeDtypeStruct((t, d), yb.dtype)],
        compiler_params=_cp(("parallel",)),
    )(gl, ya, yb, dm)


@functools.partial(jax.custom_vjp, nondiff_argnums=(3,))
def gate_mix(gl, ya, yb, tag):
    return _gate_mix_fwd_call(gl, ya, yb, "gatemix_" + tag)


def _gate_mix_f(gl, ya, yb, tag):
    return _gate_mix_fwd_call(gl, ya, yb, "gatemix_" + tag), (gl, ya, yb)


def _gate_mix_b(tag, res, dm):
    return tuple(_gate_mix_bwd_call(*res, dm, "gatemix_" + tag + "_bwd"))


gate_mix.defvjp(_gate_mix_f, _gate_mix_b)


def _resid_bwd_call(gt, p, g, name):
    t, d = p.shape
    tr = _rows(t)
    row = pl.BlockSpec((tr, d), lambda i: (i, 0))
    vec = pl.BlockSpec((1, d), lambda i: (0, 0))

    def body(gt_ref, p_ref, g_ref, dp_ref, dgt_ref):
        i = pl.program_id(0)
        g = g_ref[...]
        dp_ref[...] = (g * gt_ref[...]).astype(dp_ref.dtype)

        @pl.when(i == 0)
        def _():
            dgt_ref[...] = jnp.zeros_like(dgt_ref)

        dgt_ref[...] += jnp.sum(g * p_ref[...].astype(F32), axis=0, keepdims=True)

    return pl.pallas_call(
        body, name=name, grid=(t // tr,), in_specs=[vec, row, row], out_specs=[row, vec],
        out_shape=[jax.ShapeDtypeStruct((t, d), BF16), jax.ShapeDtypeStruct((1, d), F32)],
        compiler_params=_cp(("arbitrary",)),
    )(gt, p, g)


@functools.partial(jax.custom_vjp, nondiff_argnums=(4,))
def mm_resid(x, gt, a, b, tag):
    return _mm(a, b, "nn", F32, "mmres_" + tag, resid=(x, gt))[0]


def _mm_resid_f(x, gt, a, b, tag):
    o, p = _mm(a, b, "nn", F32, "mmres_" + tag, resid=(x, gt))
    return o, (gt, a, b, p)


def _mm_resid_b(tag, res, g):
    gt, a, b, p = res
    dp, dgt = _resid_bwd_call(gt, p, g, "mmres_" + tag + "_gate")
    da = _mm(dp, b, "nt", a.dtype, "mmres_" + tag + "_da")
    db = _mm(a, dp, "tn", b.dtype, "mmres_" + tag + "_db")
    return g, dgt, da, db


mm_resid.defvjp(_mm_resid_f, _mm_resid_b)


def _swiglu_fwd_call(gu, name):
    t, f2 = gu.shape
    f = f2 // 2
    tr = _rows(t, 128)
    half = pl.BlockSpec((tr, f), lambda i: (i, 0))

    def body(g_ref, u_ref, o_ref):
        g = g_ref[...].astype(F32)
        o_ref[...] = (g * _sigmoid(g) * u_ref[...].astype(F32)).astype(o_ref.dtype)

    return pl.pallas_call(
        body, name=name, grid=(t // tr,),
        in_specs=[half, pl.BlockSpec((tr, f), lambda i: (i, 1))], out_specs=half,
        out_shape=jax.ShapeDtypeStruct((t, f), BF16), compiler_params=_cp(("parallel",)),
    )(gu, gu)


def _swiglu_bwd_call(gu, da, name):
    t, f2 = gu.shape
    f = f2 // 2
    tr = _rows(t, 128)
    wide = pl.BlockSpec((tr, f2), lambda i: (i, 0))

    def body(gu_ref, da_ref, dgu_ref):
        g = gu_ref[:, :f].astype(F32)
        u = gu_ref[:, f:].astype(F32)
        da = da_ref[...].astype(F32)
        s = _sigmoid(g)
        dgu_ref[:, :f] = (da * u * s * (1.0 + g * (1.0 - s))).astype(dgu_ref.dtype)
        dgu_ref[:, f:] = (da * g * s).astype(dgu_ref.dtype)

    return pl.pallas_call(
        body, name=name, grid=(t // tr,),
        in_specs=[wide, pl.BlockSpec((tr, f), lambda i: (i, 0))], out_specs=wide,
        out_shape=jax.ShapeDtypeStruct((t, f2), gu.dtype), compiler_params=_cp(("parallel",)),
    )(gu, da)


@functools.partial(jax.custom_vjp, nondiff_argnums=(1,))
def swiglu(gu, tag):
    return _swiglu_fwd_call(gu, "swiglu_" + tag)


def _swiglu_f(gu, tag):
    return _swiglu_fwd_call(gu, "swiglu_" + tag), (gu,)


def _swiglu_b(tag, res, da):
    return (_swiglu_bwd_call(res[0], da, "swiglu_" + tag + "_bwd"),)


swiglu.defvjp(_swiglu_f, _swiglu_b)


def loss_head(x, fw, tgt):
    t, d = x.shape
    tr = _rows(t)
    row = pl.BlockSpec((tr, d), lambda i: (i, 0))
    vec = pl.BlockSpec((1, d), lambda i: (0, 0))
    tile = pl.BlockSpec((8, LANE), lambda i: (0, 0))

    def body(x_ref, fw_ref, tgt_ref, loss_ref, dx_ref, dfw_ref):
        i = pl.program_id(0)
        xv = x_ref[...]
        fw = fw_ref[...]
        r = lax.rsqrt(jnp.mean(xv * xv, axis=-1, keepdims=True) + EPS)
        yh = xv * r
        e = yh * fw - tgt_ref[...]
        dy = e * (1.0 / d)
        dyw = dy * fw
        dx_ref[...] = r * (dyw - yh * jnp.mean(dyw * yh, axis=-1, keepdims=True))

        @pl.when(i == 0)
        def _():
            loss_ref[...] = jnp.zeros_like(loss_ref)
            dfw_ref[...] = jnp.zeros_like(dfw_ref)

        loss_ref[...] += 0.5 * jnp.sum(jnp.mean(e * e, axis=-1, keepdims=True))
        dfw_ref[...] += jnp.sum(dy * yh, axis=0, keepdims=True)

    return pl.pallas_call(
        body, name="loss_head", grid=(t // tr,), in_specs=[row, vec, row],
        out_specs=[tile, row, vec],
        out_shape=[jax.ShapeDtypeStruct((8, LANE), F32), jax.ShapeDtypeStruct((t, d), F32),
                   jax.ShapeDtypeStruct((1, d), F32)],
        compiler_params=_cp(("arbitrary",)),
    )(x, fw, tgt)


def _attn_scores(qn_ref, qr_ref, kn_ref, kr_ref, diag):
    tq = qn_ref.shape[0]
    s = lax.dot_general(qn_ref[...].astype(BF16), kn_ref[...].astype(BF16), NT, preferred_element_type=F32)
    s += lax.dot_general(qr_ref[...].astype(BF16), kr_ref[...].astype(BF16), NT, preferred_element_type=F32)
    s = s * (QK_NOPE + QK_ROPE) ** -0.5
    if diag:
        rows = lax.broadcasted_iota(jnp.int32, (tq, tq), 0)
        cols = lax.broadcasted_iota(jnp.int32, (tq, tq), 1)
        s = jnp.where(cols <= rows, s, -1e30)
    return s


def _attn_fwd_call(qn, qr, kv, kr, name):
    t = qn.shape[0]
    h_n = MLA_HEADS
    tq = _rows(t, 512)
    nq = t // tq
    assert V_HEAD == LANE and tq % LANE == 0

    def body(qn_ref, qr_ref, kn_ref, v_ref, kr_ref, o_ref, lse_ref, m_scr, l_scr, acc_scr):
        i, j = pl.program_id(1), pl.program_id(2)

        @pl.when(j == 0)
        def _():
            m_scr[...] = jnp.full_like(m_scr, -1e30)
            l_scr[...] = jnp.zeros_like(l_scr)
            acc_scr[...] = jnp.zeros_like(acc_scr)

        def step(diag):
            s = _attn_scores(qn_ref, qr_ref, kn_ref, kr_ref, diag)
            m_old = m_scr[...]
            m_new = jnp.maximum(m_old, jnp.max(s, axis=-1, keepdims=True))
            p = jnp.exp(s - jnp.tile(m_new, (1, tq // LANE)))
            alpha = jnp.exp(m_old - m_new)
            l_scr[...] = alpha * l_scr[...] + jnp.sum(p, axis=-1, keepdims=True)
            acc_scr[...] = alpha * acc_scr[...] + jnp.dot(p.astype(BF16), v_ref[...].astype(BF16),
                                                           preferred_element_type=F32)
            m_scr[...] = m_new

        @pl.when(j < i)
        def _():
            step(False)

        @pl.when(j == i)
        def _():
            step(True)
            o_ref[...] = (acc_scr[...] / l_scr[...]).astype(o_ref.dtype)
            lse_ref[...] = (m_scr[...] + jnp.log(l_scr[...]))[:, :1]

    return pl.pallas_call(
        body, name=name, grid=(h_n, nq, nq),
        in_specs=[
            pl.BlockSpec((tq, QK_NOPE), lambda h, i, j: (i, h)),
            pl.BlockSpec((None, tq, QK_ROPE), lambda h, i, j: (h, i, 0)),
            pl.BlockSpec((tq, QK_NOPE), lambda h, i, j: (jnp.minimum(j, i), 2 * h)),
            pl.BlockSpec((tq, V_HEAD), lambda h, i, j: (jnp.minimum(j, i), 2 * h + 1)),
            pl.BlockSpec((tq, QK_ROPE), lambda h, i, j: (jnp.minimum(j, i), 0)),
        ],
        out_specs=[
            pl.BlockSpec((tq, V_HEAD), lambda h, i, j: (i, h)),
            pl.BlockSpec((None, tq, 1), lambda h, i, j: (h, i, 0)),
        ],
        out_shape=[jax.ShapeDtypeStruct((t, h_n * V_HEAD), BF16),
                   jax.ShapeDtypeStruct((h_n, t, 1), F32)],
        scratch_shapes=[pltpu.VMEM((tq, LANE), F32), pltpu.VMEM((tq, LANE), F32),
                        pltpu.VMEM((tq, V_HEAD), F32)],
        compiler_params=_cp(("parallel", "parallel", "arbitrary")),
    )(qn, qr, kv, kv, kr)


def _attn_bwd_call(qn, qr, kv, kr, o, lse, do, name):
    t = qn.shape[0]
    h_n = MLA_HEADS
    tq = _rows(t, 512)
    nq = t // tq
    scale = (QK_NOPE + QK_ROPE) ** -0.5

    def body(qn_ref, qr_ref, kn_ref, v_ref, kr_ref, o_ref, lse_ref, do_ref,
             dqn_ref, dqr_ref, dkv_ref, dkr_ref, dqn_scr, dqr_scr, dkn_scr, dv_scr, dkr_scr):
        j, i = pl.program_id(1), pl.program_id(2)

        @pl.when(jnp.logical_and(j == 0, i == 0))
        def _():
            dqn_scr[...] = jnp.zeros_like(dqn_scr)
            dqr_scr[...] = jnp.zeros_like(dqr_scr)

        @pl.when(i == 0)
        def _():
            dkn_scr[...] = jnp.zeros_like(dkn_scr)
            dv_scr[...] = jnp.zeros_like(dv_scr)
            dkr_scr[...] = jnp.zeros_like(dkr_scr)

        def step(diag):
            qn_b = qn_ref[...].astype(BF16)
            qr_b = qr_ref[...].astype(BF16)
            kn_b = kn_ref[...].astype(BF16)
            kr_b = kr_ref[...].astype(BF16)
            do_b = do_ref[...]
            p = jnp.exp(_attn_scores(qn_ref, qr_ref, kn_ref, kr_ref, diag) - lse_ref[...])
            delta = jnp.sum(do_b.astype(F32) * o_ref[...].astype(F32), axis=-1, keepdims=True)
            dp = lax.dot_general(do_b, v_ref[...].astype(BF16), NT, preferred_element_type=F32)
            ds = (p * (dp - delta) * scale).astype(BF16)
            p_b = p.astype(BF16)
            dv_scr[...] += lax.dot_general(p_b, do_b, TN, preferred_element_type=F32)
            dkn_scr[...] += lax.dot_general(ds, qn_b, TN, preferred_element_type=F32)
            dkr_scr[...] += lax.dot_general(ds, qr_b, TN, preferred_element_type=F32)
            sl = pl.ds(pl.multiple_of(i * tq, tq), tq)
            dqn_scr[sl, :] += jnp.dot(ds, kn_b, preferred_element_type=F32)
            dqr_scr[sl, :] += jnp.dot(ds, kr_b, preferred_element_type=F32)

        @pl.when(i > j)
        def _():
            step(False)

        @pl.when(i == j)
        def _():
            step(True)

        @pl.when(i == nq - 1)
        def _():
            dkv_ref[:, :QK_NOPE] = dkn_scr[...].astype(dkv_ref.dtype)
            dkv_ref[:, QK_NOPE:] = dv_scr[...].astype(dkv_ref.dtype)
            dkr_ref[...] = dkr_scr[...]

        @pl.when(jnp.logical_and(j == nq - 1, i == nq - 1))
        def _():
            dqn_ref[...] = dqn_scr[...].astype(dqn_ref.dtype)
            dqr_ref[...] = dqr_scr[...].astype(dqr_ref.dtype)

    qi = lambda j, i: jnp.maximum(i, j)
    return pl.pallas_call(
        body, name=name, grid=(h_n, nq, nq),
        in_specs=[
            pl.BlockSpec((tq, QK_NOPE), lambda h, j, i: (qi(j, i), h)),
            pl.BlockSpec((None, tq, QK_ROPE), lambda h, j, i: (h, qi(j, i), 0)),
            pl.BlockSpec((tq, QK_NOPE), lambda h, j, i: (j, 2 * h)),
            pl.BlockSpec((tq, V_HEAD), lambda h, j, i: (j, 2 * h + 1)),
            pl.BlockSpec((tq, QK_ROPE), lambda h, j, i: (j, 0)),
            pl.BlockSpec((tq, V_HEAD), lambda h, j, i: (qi(j, i), h)),
            pl.BlockSpec((None, tq, 1), lambda h, j, i: (h, qi(j, i), 0)),
            pl.BlockSpec((tq, V_HEAD), lambda h, j, i: (qi(j, i), h)),
        ],
        out_specs=[
            pl.BlockSpec((t, QK_NOPE), lambda h, j, i: (0, h)),
            pl.BlockSpec((None, t, QK_ROPE), lambda h, j, i: (h, 0, 0)),
            pl.BlockSpec((tq, QK_NOPE + V_HEAD), lambda h, j, i: (j, h)),
            pl.BlockSpec((None, tq, QK_ROPE), lambda h, j, i: (h, j, 0)),
        ],
        out_shape=[jax.ShapeDtypeStruct((t, h_n * QK_NOPE), qn.dtype),
                   jax.ShapeDtypeStruct((h_n, t, QK_ROPE), qr.dtype),
                   jax.ShapeDtypeStruct((t, h_n * (QK_NOPE + V_HEAD)), kv.dtype),
                   jax.ShapeDtypeStruct((h_n, t, QK_ROPE), F32)],
        scratch_shapes=[pltpu.VMEM((t, QK_NOPE), F32), pltpu.VMEM((t, QK_ROPE), F32),
                        pltpu.VMEM((tq, QK_NOPE), F32), pltpu.VMEM((tq, V_HEAD), F32),
                        pltpu.VMEM((tq, QK_ROPE), F32)],
        compiler_params=_cp(("parallel", "arbitrary", "arbitrary")),
    )(qn, qr, kv, kv, kr, o, lse, do)


@functools.partial(jax.custom_vjp, nondiff_argnums=(4,))
def attention(qn, qr, kv, kr, tag):
    return _attn_fwd_call(qn, qr, kv, kr, "attn_" + tag)[0]


def _attention_f(qn, qr, kv, kr, tag):
    o, lse = _attn_fwd_call(qn, qr, kv, kr, "attn_" + tag)
    return o, (qn, qr, kv, kr, o, lse)


def _attention_b(tag, res, do):
    dqn, dqr, dkv, dkr_h = _attn_bwd_call(*res, do, "attn_" + tag + "_bwd")
    return dqn, dqr, dkv, jnp.sum(dkr_h, axis=0).astype(res[3].dtype)


attention.defvjp(_attention_f, _attention_b)


def _shift_down(u, s):
    if s == 0:
        return u
    t = u.shape[0]
    rolled = pltpu.roll(u, s, 0)
    return jnp.where(lax.broadcasted_iota(jnp.int32, u.shape, 0) >= s, rolled, 0.0)


def _shift_up(u, s):
    if s == 0:
        return u
    t = u.shape[0]
    rolled = pltpu.roll(u, t - s, 0)
    return jnp.where(lax.broadcasted_iota(jnp.int32, u.shape, 0) < t - s, rolled, 0.0)


def _conv_blocks(t, c3):
    p = c3 // 3
    tc = _tile(p, 512)
    per = p // tc
    return p, tc, per


def _conv_fwd_call(u, w, name):
    t, c3 = u.shape
    p, tc, per = _conv_blocks(t, c3)

    def body(u_ref, w_ref, o_ref):
        u = u_ref[...].astype(F32)
        y = jnp.zeros_like(u)
        for j in range(CONV_WIDTH):
            y = y + w_ref[j:j + 1, :] * _shift_down(u, CONV_WIDTH - 1 - j)
        o_ref[...] = y * _sigmoid(y)

    return pl.pallas_call(
        body, name=name, grid=(c3 // tc,),
        in_specs=[pl.BlockSpec((t, tc), lambda cb: (0, cb)),
                  pl.BlockSpec((CONV_WIDTH, tc), lambda cb: (0, cb))],
        out_specs=pl.BlockSpec((None, t, tc), lambda cb: (cb // per, 0, cb % per)),
        out_shape=jax.ShapeDtypeStruct((3, t, p), F32),
        compiler_params=_cp(("parallel",)),
    )(u, w)


def _conv_bwd_call(u, w, do, name):
    t, c3 = u.shape
    p, tc, per = _conv_blocks(t, c3)

    def body(u_ref, w_ref, do_ref, du_ref, dw_ref):
        u = u_ref[...].astype(F32)
        shifted = [_shift_down(u, CONV_WIDTH - 1 - j) for j in range(CONV_WIDTH)]
        y = jnp.zeros_like(u)
        for j in range(CONV_WIDTH):
            y = y + w_ref[j:j + 1, :] * shifted[j]
        s = _sigmoid(y)
        dy = do_ref[...] * s * (1.0 + y * (1.0 - s))
        du = jnp.zeros_like(u)
        for j in range(CONV_WIDTH):
            du = du + w_ref[j:j + 1, :] * _shift_up(dy, CONV_WIDTH - 1 - j)
            dw_ref[j:j + 1, :] = jnp.sum(dy * shifted[j], axis=0, keepdims=True)
        du_ref[...] = du.astype(du_ref.dtype)

    return pl.pallas_call(
        body, name=name, grid=(c3 // tc,),
        in_specs=[pl.BlockSpec((t, tc), lambda cb: (0, cb)),
                  pl.BlockSpec((CONV_WIDTH, tc), lambda cb: (0, cb)),
                  pl.BlockSpec((None, t, tc), lambda cb: (cb // per, 0, cb % per))],
        out_specs=[pl.BlockSpec((t, tc), lambda cb: (0, cb)),
                   pl.BlockSpec((CONV_WIDTH, tc), lambda cb: (0, cb))],
        out_shape=[jax.ShapeDtypeStruct((t, c3), u.dtype), jax.ShapeDtypeStruct((CONV_WIDTH, c3), F32)],
        compiler_params=_cp(("parallel",)),
    )(u, w, do)


@functools.partial(jax.custom_vjp, nondiff_argnums=(2,))
def conv_silu(u, w, tag):
    return _conv_fwd_call(u, w, "conv_" + tag)


def _conv_silu_f(u, w, tag):
    return _conv_fwd_call(u, w, "conv_" + tag), (u, w)


def _conv_silu_b(tag, res, do):
    return tuple(_conv_bwd_call(*res, do, "conv_" + tag + "_bwd"))


conv_silu.defvjp(_conv_silu_f, _conv_silu_b)


BNN = (((2,), (1,)), ((0,), (0,)))
BNT = (((2,), (2,)), ((0,), (0,)))
BTN = (((1,), (1,)), ((0,), (0,)))


def _xdot(a, b, dn=BNN):
    return lax.dot_general(a, b, dn, precision=lax.Precision.HIGHEST, preferred_element_type=F32)


def _bf16_dot(a, b, dn):
    return lax.dot_general(a.astype(BF16), b.astype(BF16), dn, preferred_element_type=F32)


def _dot3(a, b, dn):
    ah, bh = a.astype(BF16), b.astype(BF16)
    al, bl = a - ah.astype(F32), b - bh.astype(F32)
    return _bf16_dot(ah, bh, dn) + (_bf16_dot(ah, bl, dn) + _bf16_dot(al, bh, dn))


def _transposed(dn, a, b, g):
    if dn == BNN:
        return (g, b, BNT), (a, g, BTN)
    if dn == BNT:
        return (g, b, BNN), (g, a, BTN)
    return (b, g, BNT), (a, g, BNN)


@functools.partial(jax.custom_vjp, nondiff_argnums=(2,))
def _hdot(a, b, dn=BNN):
    return _dot3(a, b, dn)


def _hdot_f(a, b, dn):
    return _dot3(a, b, dn), (a, b)


def _hdot_b(dn, res, g):
    da, db = _transposed(dn, *res, g)
    return _dot3(*da), _dot3(*db)


_hdot.defvjp(_hdot_f, _hdot_b)


@functools.partial(jax.custom_vjp, nondiff_argnums=(2,))
def _bdot(a, b, dn=BNN):
    return _bf16_dot(a, b, dn)


def _bdot_f(a, b, dn):
    return _bf16_dot(a, b, dn), (a, b)


def _bdot_b(dn, res, g):
    da, db = _transposed(dn, *res, g)
    return _bf16_dot(*da), _bf16_dot(*db)


_bdot.defvjp(_bdot_f, _bdot_b)


GDN_HEADS_PER_STEP = 8
GDN_HEADS_PER_STEP_BWD = 8


def _gdn_chunk(q, k, v, z, bl, al, a_log, dtb, gn, s):
    b, c = q.shape[0], q.shape[1]
    ri = lax.broadcasted_iota(jnp.int32, (c, c), 0)
    ci = lax.broadcasted_iota(jnp.int32, (c, c), 1)
    lower = (ri >= ci)[None]
    strict = (ri > ci)[None]
    low_incl = jnp.broadcast_to((ri >= ci).astype(F32), (b, c, c))
    up_incl = jnp.broadcast_to((ri <= ci).astype(F32), (b, c, c))
    eye = (ri == ci).astype(F32)[None]

    q = q * lax.rsqrt(jnp.sum(q * q, axis=-1, keepdims=True) + EPS) * (GDN_DK ** -0.5)
    k = k * lax.rsqrt(jnp.sum(k * k, axis=-1, keepdims=True) + EPS)
    beta = _sigmoid(bl)
    g = -jnp.exp(a_log) * _softplus(al + dtb)
    g_w = jnp.broadcast_to(g, (b, c, LANE))
    gc = _xdot(low_incl, g_w)
    gr = _xdot(g_w[:, :, :c], up_incl, BTN)
    diff = gc[:, :, :c] - gr
    decay = jnp.where(lower, jnp.exp(jnp.where(lower, diff, 0.0)), 0.0)
    kb = k * beta
    lmat = jnp.where(strict, _bdot(kb, k, BNT) * decay, 0.0)
    inv = eye - lmat
    pw = lmat
    for _ in range(int(math.log2(c)) - 1):
        pw = _hdot(pw, pw)
        inv = _hdot(inv, eye + pw)
    eg = jnp.exp(gc)
    u = _hdot(inv, v * beta)
    w = _hdot(inv, kb * eg)
    attn = jnp.where(lower, _bdot(q, k, BNT) * decay, 0.0)
    v_new = u - _bdot(w, s)
    o = _bdot(q * eg, s) + _bdot(attn, v_new)
    g_last = jnp.sum(g_w, axis=1, keepdims=True)
    k_dec = k * jnp.exp(g_last - gc)
    s_new = s * jnp.exp(g_last) + _bdot(k_dec, v_new, BTN)
    on = o * lax.rsqrt(jnp.mean(o * o, axis=-1, keepdims=True) + EPS) * gn
    return on * (z * _sigmoid(z)), s_new


def _head_cols(ba, first, count):
    lane = lax.broadcasted_iota(jnp.int32, ba.shape, 1)
    return jnp.stack([jnp.sum(jnp.where(lane == first + j, ba, 0.0), axis=1, keepdims=True)
                      for j in range(count)])


def _gdn_heads(q, k, v, z, ba, a_log, dtb, gn, s):
    h_n = q.shape[0]
    return _gdn_chunk(q, k, v, z, _head_cols(ba, 0, h_n), _head_cols(ba, h_n, h_n), a_log, dtb, gn, s)


def _gdn_specs(n_chunks, hb, rev):
    c = CHUNK
    nn = (lambda n: n_chunks - 1 - n) if rev else (lambda n: n)
    plane = lambda pidx: pl.BlockSpec((None, c, hb * GDN_DK), lambda hg, n: (pidx, nn(n), hg))
    assert hb == GDN_HEADS
    logits = pl.BlockSpec((c, LANE), lambda hg, n: (nn(n), 0))
    scal = pl.BlockSpec((hb, 1, 1), lambda hg, n: (hg, 0, 0))
    zspec = pl.BlockSpec((c, hb * GDN_DV), lambda hg, n: (nn(n), hg))
    gnspec = pl.BlockSpec((1, GDN_DV), lambda hg, n: (0, 0))
    sspec = pl.BlockSpec((hb, None, GDN_DK, GDN_DV), lambda hg, n: (hg, nn(n), 0, 0))
    return plane, logits, scal, zspec, gnspec, sspec


def _heads_per_step(want):
    return math.gcd(want, GDN_HEADS)


def _heads(ref, hb):
    return jnp.stack([ref[:, j * GDN_DK:(j + 1) * GDN_DK] for j in range(hb)])


def _gdn_fwd_call(qkv, z, ba, a_log, dtb, gn, name):
    t = z.shape[0]
    h_n = GDN_HEADS
    hb = _heads_per_step(GDN_HEADS_PER_STEP)
    n_chunks = t // CHUNK
    plane, logits, scal, zspec, gnspec, sspec = _gdn_specs(n_chunks, hb, False)

    def body(q_ref, k_ref, v_ref, z_ref, ba_ref, a_ref, dtb_ref, gn_ref, o_ref, sall_ref, s_scr):
        n = pl.program_id(1)

        @pl.when(n == 0)
        def _():
            s_scr[...] = jnp.zeros_like(s_scr)

        s = s_scr[...]
        sall_ref[...] = s
        o, s_new = _gdn_heads(_heads(q_ref, hb), _heads(k_ref, hb), _heads(v_ref, hb),
                              _heads(z_ref, hb).astype(F32),
                              ba_ref[...], a_ref[...], dtb_ref[...], gn_ref[...], s)
        for j in range(hb):
            o_ref[:, j * GDN_DV:(j + 1) * GDN_DV] = o[j].astype(o_ref.dtype)
        s_scr[...] = s_new

    return pl.pallas_call(
        body, name=name, grid=(h_n // hb, n_chunks),
        in_specs=[plane(0), plane(1), plane(2), zspec, logits, scal, scal, gnspec],
        out_specs=[zspec, sspec],
        out_shape=[jax.ShapeDtypeStruct((t, h_n * GDN_DV), BF16),
                   jax.ShapeDtypeStruct((h_n, n_chunks, GDN_DK, GDN_DV), F32)],
        scratch_shapes=[pltpu.VMEM((hb, GDN_DK, GDN_DV), F32)],
        compiler_params=_cp(("parallel", "arbitrary")),
    )(qkv, qkv, qkv, z, ba, a_log, dtb, gn)


def _gdn_bwd_call(qkv, z, ba, a_log, dtb, gn, sall, do, name):
    t = z.shape[0]
    h_n = GDN_HEADS
    hb = _heads_per_step(GDN_HEADS_PER_STEP_BWD)
    n_chunks = t // CHUNK
    c = CHUNK
    plane, logits, scal, zspec, gnspec, sspec = _gdn_specs(n_chunks, hb, True)
    dplanes = pl.BlockSpec((3, c, hb * GDN_DK), lambda hg, n: (0, n_chunks - 1 - n, hg))
    gnh = pl.BlockSpec((None, 1, GDN_DV), lambda hg, n: (hg, 0, 0))

    def body(q_ref, k_ref, v_ref, z_ref, ba_ref, a_ref, dtb_ref, gn_ref, s_ref, do_ref,
             dqkv_ref, dz_ref, dba_ref, da_ref, ddtb_ref, dgn_ref, ds_scr):
        n = pl.program_id(1)

        @pl.when(n == 0)
        def _():
            ds_scr[...] = jnp.zeros_like(ds_scr)
            da_ref[...] = jnp.zeros_like(da_ref)
            ddtb_ref[...] = jnp.zeros_like(ddtb_ref)
            dgn_ref[...] = jnp.zeros_like(dgn_ref)

        _, vjp = jax.vjp(_gdn_heads, _heads(q_ref, hb), _heads(k_ref, hb), _heads(v_ref, hb),
                         _heads(z_ref, hb).astype(F32),
                         ba_ref[...], a_ref[...], dtb_ref[...], gn_ref[...], s_ref[...])
        dq, dk, dv, dz, dba, da, ddtb, dgn, ds = vjp((_heads(do_ref, hb).astype(F32), ds_scr[...]))
        for j in range(hb):
            hs = slice(j * GDN_DK, (j + 1) * GDN_DK)
            dqkv_ref[0, :, hs] = dq[j]
            dqkv_ref[1, :, hs] = dk[j]
            dqkv_ref[2, :, hs] = dv[j]
            dz_ref[:, hs] = dz[j].astype(dz_ref.dtype)
        dba_ref[...] = dba
        da_ref[...] += da
        ddtb_ref[...] += ddtb
        dgn_ref[...] += dgn
        ds_scr[...] = ds

    return pl.pallas_call(
        body, name=name, grid=(h_n // hb, n_chunks),
        in_specs=[plane(0), plane(1), plane(2), zspec, logits, scal, scal, gnspec, sspec, zspec],
        out_specs=[dplanes, zspec, logits, scal, scal, gnh],
        out_shape=[jax.ShapeDtypeStruct((3, t, h_n * GDN_DK), F32),
                   jax.ShapeDtypeStruct((t, h_n * GDN_DV), z.dtype),
                   jax.ShapeDtypeStruct((t, LANE), F32),
                   jax.ShapeDtypeStruct((h_n, 1, 1), F32), jax.ShapeDtypeStruct((h_n, 1, 1), F32),
                   jax.ShapeDtypeStruct((h_n // hb, 1, GDN_DV), F32)],
        scratch_shapes=[pltpu.VMEM((hb, GDN_DK, GDN_DV), F32)],
        compiler_params=_cp(("parallel", "arbitrary")),
    )(qkv, qkv, qkv, z, ba, a_log, dtb, gn, sall, do)


@functools.partial(jax.custom_vjp, nondiff_argnums=(6,))
def gdn(qkv, z, ba, a_log, dtb, gn, tag):
    return _gdn_fwd_call(qkv, z, ba, a_log, dtb, gn, "gdn_" + tag)[0]


def _gdn_f(qkv, z, ba, a_log, dtb, gn, tag):
    o, sall = _gdn_fwd_call(qkv, z, ba, a_log, dtb, gn, "gdn_" + tag)
    return o, (qkv, z, ba, a_log, dtb, gn, sall)


def _gdn_b(tag, res, do):
    dqkv, dz, dba, da, ddtb, dgn_h = _gdn_bwd_call(*res, do, "gdn_" + tag + "_bwd")
    return dqkv, dz, dba, da, ddtb, jnp.sum(dgn_h, axis=0)


gdn.defvjp(_gdn_f, _gdn_b)


ADAMW_BLOCK = 256 * 1024


def adamw(w, m, v, *, parts, name):
    n_layers = len(parts)
    n_parts, r, c = parts[0].shape
    assert w.shape == (n_layers * r, c), (w.shape, parts[0].shape)
    tr = r
    for cand in (512, 256, 128, 64, 32, 16, 8):
        if r % cand == 0 and cand * c <= ADAMW_BLOCK:
            tr = cand
            break
    nb = r // tr
    blk = pl.BlockSpec((tr, c), lambda l, i: (l * nb + i, 0))
    bc1 = 1.0 - ADAM_B1 ** ADAM_STEP
    bc2 = 1.0 - ADAM_B2 ** ADAM_STEP

    def part_spec(li):
        return pl.BlockSpec((n_parts, tr, c),
                            lambda l, i: (0, jnp.where(l == li, i, jnp.where(l < li, 0, nb - 1)), 0))

    def body(*refs):
        w_ref, p_refs = refs[0], refs[1:1 + n_layers]
        m_ref, v_ref, g_ref, d_ref, mo_ref, vo_ref = refs[1 + n_layers:]
        for li in range(n_layers):
            @pl.when(pl.program_id(0) == li)
            def _(p_ref=p_refs[li]):
                g = p_ref[0].astype(F32)
                for i in range(1, n_parts):
                    g = g + p_ref[i].astype(F32)
                m2 = ADAM_B1 * m_ref[...] + (1.0 - ADAM_B1) * g
                v2 = ADAM_B2 * v_ref[...] + (1.0 - ADAM_B2) * (g * g)
                g_ref[...] = g
                mo_ref[...] = m2
                vo_ref[...] = v2
                d_ref[...] = -ADAM_LR * ((m2 / bc1) / (jnp.sqrt(v2 / bc2) + ADAM_EPS)
                                         + ADAM_WD * w_ref[...])

    return pl.pallas_call(
        body, name=name, grid=(n_layers, nb),
        in_specs=[blk] + [part_spec(li) for li in range(n_layers)] + [blk, blk],
        out_specs=[blk] * 4, out_shape=[jax.ShapeDtypeStruct(w.shape, F32)] * 4,
        compiler_params=_cp(("arbitrary", "arbitrary")),
    )(w, *parts, m, v)


_HBM = pl.BlockSpec(memory_space=pltpu.HBM)
_SEM = pl.BlockSpec(memory_space=pltpu.SEMAPHORE)
_EFFECT = pltpu.SideEffectType.DATAFLOW_SIDE_EFFECTING


def _peer(x, y, c, d):
    px = 1 - x if d & 4 else x
    py = 1 - y if d & 2 else y
    pc = 1 - c if d & 1 else c
    return (px, py, pc), 4 * px + 2 * py + pc


ALL_PEERS = (1, 2, 3, 4, 5, 6, 7)
SIBLING = 1
SAME_CORE_REMOTE = (2, 4, 6)


def copy_start(arrays, mode, carry, name):
    n = len(arrays)
    if mode == "forward":
        lands = []
    else:
        lands = [lax.empty(a.shape if mode == "scatter" else (N_DEV,) + a.shape, a.dtype) for a in arrays]
    n_in = n + len(lands) + 1

    def body(*refs):
        srcs = refs[:n]
        dsts = refs[n:2 * n] if lands else srcs
        sems = refs[n_in:n_in + 2 * n]
        x, y, c = (lax.axis_index(a) for a in AXES)
        me = 4 * x + 2 * y + c
        for k in range(n):
            if mode == "forward":
                sibling, _ = _peer(x, y, c, SIBLING)
                copies = [(srcs[k].at[_peer(x, y, c, d)[1]], dsts[k].at[_peer(x, y, c, d)[1]], sibling)
                          for d in SAME_CORE_REMOTE]
            elif mode == "gather":
                copies = [(srcs[k], dsts[k].at[me], _peer(x, y, c, d)[0]) for d in (SIBLING,) + SAME_CORE_REMOTE]
            else:
                copies = [(srcs[k].at[_peer(x, y, c, d)[1]], dsts[k].at[me], _peer(x, y, c, d)[0])
                          for d in ALL_PEERS]
            for src, dst, peer in copies:
                pltpu.make_async_remote_copy(src_ref=src, dst_ref=dst, send_sem=sems[2 * k],
                                             recv_sem=sems[2 * k + 1], device_id=peer,
                                             device_id_type=pl.DeviceIdType.MESH).start()

    operands = list(arrays) + lands + [carry]
    outs = pl.pallas_call(
        body, name=name,
        out_shape=tuple([pltpu.SemaphoreType.DMA(())] * (2 * n)
                        + [pltpu.HBM(a.shape, a.dtype) for a in operands]),
        in_specs=[_HBM] * n_in,
        out_specs=tuple([_SEM] * (2 * n) + [_HBM] * n_in),
        input_output_aliases={i: 2 * n + i for i in range(n_in)},
        compiler_params=pltpu.CompilerParams(has_side_effects=_EFFECT),
    )(*[pltpu.with_memory_space_constraint(a, pltpu.HBM) for a in operands])
    sems, thru = outs[:2 * n], outs[2 * n:-1]
    handles = [(sems[2 * k], sems[2 * k + 1], thru[k] if lands else None, thru[n + k] if lands else thru[k])
               for k in range(n)]
    return outs[-1], handles


def copy_wait(handles, n_blocks, after, name):
    n = len(handles)
    sems = [s for h in handles for s in h[:2]]
    srcs = [h[2] for h in handles if h[2] is not None]
    lands = [h[3] for h in handles]
    ns = len(srcs)

    def body(*refs):
        dsts = refs[ns:ns + n]
        sem_refs = refs[ns + n:ns + 3 * n]
        x, y, c = (lax.axis_index(a) for a in AXES)
        for k in range(n):
            blocks = dsts[k].at[pl.ds(0, n_blocks)]
            pltpu.make_async_remote_copy(
                src_ref=blocks, dst_ref=blocks, send_sem=sem_refs[2 * k], recv_sem=sem_refs[2 * k + 1],
                device_id=(x, y, c), device_id_type=pl.DeviceIdType.MESH).wait()

    outs = pl.pallas_call(
        body, name=name,
        out_shape=tuple([pltpu.HBM(a.shape, a.dtype) for a in srcs + lands]),
        in_specs=[_HBM] * (ns + n) + [_SEM] * (2 * n) + [pl.BlockSpec(memory_space=pl.ANY)],
        out_specs=tuple([_HBM] * (ns + n)),
        input_output_aliases={i: i for i in range(ns + n)},
        compiler_params=pltpu.CompilerParams(has_side_effects=_EFFECT),
    )(*srcs, *lands, *sems, after)
    return (list(outs[:ns]) if ns else [None] * n), list(outs[ns:])


def exchange(arrays, modes, name):
    n = len(arrays)
    hbm = pl.BlockSpec(memory_space=pltpu.HBM)
    out_shape = [jax.ShapeDtypeStruct(a.shape if md == "scatter" else (N_DEV,) + a.shape, a.dtype)
                 for a, md in zip(arrays, modes)]

    def body(*refs):
        ins, outs = refs[:n], refs[n:2 * n]
        send_sems, recv_sems, local_sems = refs[2 * n:]
        x, y, c = (lax.axis_index(a) for a in AXES)
        me = 4 * x + 2 * y + c

        def src(k, p):
            return ins[k].at[p] if modes[k] == "scatter" else ins[k]

        local = [pltpu.make_async_copy(src(k, me), outs[k].at[me], local_sems.at[k]) for k in range(n)]
        for cp in local:
            cp.start()
        started = []
        for d in range(1, N_DEV):
            px = 1 - x if d & 4 else x
            py = 1 - y if d & 2 else y
            pc = 1 - c if d & 1 else c
            pid = 4 * px + 2 * py + pc
            for k in range(n):
                pltpu.make_async_remote_copy(
                    src_ref=src(k, pid), dst_ref=outs[k].at[me],
                    send_sem=send_sems.at[k, d - 1], recv_sem=recv_sems.at[k, d - 1],
                    device_id=(px, py, pc), device_id_type=pl.DeviceIdType.MESH).start()
                started.append((k, d, pid, (px, py, pc)))
        for k, d, pid, peer in started:
            pltpu.make_async_remote_copy(
                src_ref=src(k, pid), dst_ref=outs[k].at[pid],
                send_sem=send_sems.at[k, d - 1], recv_sem=recv_sems.at[k, d - 1],
                device_id=peer, device_id_type=pl.DeviceIdType.MESH).wait()
        for cp in local:
            cp.wait()

    outs = pl.pallas_call(
        body, name=name, in_specs=[hbm] * n, out_specs=[hbm] * n, out_shape=out_shape,
        scratch_shapes=[pltpu.SemaphoreType.DMA((n, N_DEV - 1)), pltpu.SemaphoreType.DMA((n, N_DEV - 1)),
                        pltpu.SemaphoreType.DMA((n,))],
        compiler_params=pltpu.CompilerParams(has_side_effects=True),
    )(*arrays)
    return list(outs)


BIG = ("w_in", "w_uq", "w_ukv", "w_o_mla", "w_o_gdn", "w_o", "w_gate_up", "w_down")
ROW_SHARDED = ("w_o", "w_down")
SMALL = ("b_ada", "norm_mix", "norm_ffn", "q_a_norm", "kv_a_norm", "A_log", "dt_bias", "gdn_norm",
         "final_norm")
WEIGHTS = ("w_ada", "b_ada", "norm_mix", "norm_ffn", "w_in", "q_a_norm", "kv_a_norm", "w_uq", "w_ukv",
           "w_o_mla", "conv_w", "A_log", "dt_bias", "gdn_norm", "w_o_gdn", "w_o", "w_gate_up", "w_down",
           "final_norm")


def _unslot(g):
    return g.transpose(1, 0, 2).reshape(g.shape[1], -1)


def _cols(g):
    return g if g.shape[-1] % LANE == 0 else _unslot(g)


def _stack_rows(g):
    return g.reshape(-1, g.shape[-1])


def _rope(xv, cos, sin):
    x1, x2 = jnp.split(xv, 2, axis=-1)
    return jnp.concatenate([x1 * cos - x2 * sin, x2 * cos + x1 * sin], axis=-1)


MIX_WEIGHTS = ("w_uq", "w_ukv", "w_o_mla", "w_o_gdn", "w_o")
FFN_WEIGHTS = ("w_gate_up", "w_down")


def _pad_cols(a):
    return jnp.pad(a, ((0, 0), (0, _pad_lanes(a.shape[1]) - a.shape[1])))


def _stage_in(x, mod, nm, w_in_s, tg):
    d = x.shape[1]
    hg = GDN_HEADS
    w_in = _unslot(w_in_s)
    o1 = Q_LORA + KV_LORA + QK_ROPE
    o2 = o1 + 2 * hg * GDN_DK + hg * GDN_DV
    o3 = o2 + hg * GDN_DV
    o4 = o3 + 2 * hg
    h = ada_norm(x, nm, mod[:, d:2 * d], mod[:, :d], "mix" + tg)
    return (mm(h, _pad_cols(w_in[:, :o1]), "in_a" + tg, BF16), mm(h, w_in[:, o1:o2], "in_qkv" + tg, BF16),
            mm(h, w_in[:, o2:o3], "in_z" + tg, BF16), mm(h, _pad_cols(w_in[:, o3:o4]), "in_ba" + tg, F32),
            mm(h, w_in[:, o4:o4 + 2 * d], "in_g" + tg, BF16))


def _stage_mix(x, mod, seg_a, qkv, z, ba, gl, w_uq_s, w_ukv_s, w_o_mla_s, w_o_gdn_s, w_o_s, conv_s,
               qan, kvan, a_log, dtb, gn, cos, sin, tg):
    t, d = x.shape
    hq, hg = MLA_HEADS, GDN_HEADS
    w_uq = _unslot(w_uq_s).reshape(Q_LORA, hq, QK_NOPE + QK_ROPE)
    w_uq = jnp.concatenate([w_uq[:, :, :QK_NOPE].reshape(Q_LORA, hq * QK_NOPE),
                            w_uq[:, :, QK_NOPE:].reshape(Q_LORA, hq * QK_ROPE)], axis=1)
    c_q = seg_a[:, :Q_LORA]
    c_kv = seg_a[:, Q_LORA:Q_LORA + KV_LORA]
    k_pe = seg_a[:, Q_LORA + KV_LORA:Q_LORA + KV_LORA + QK_ROPE]
    qf = mm(rms_norm(c_q, qan, "qa" + tg), w_uq, "uq" + tg, BF16)
    kvf = mm(rms_norm(c_kv, kvan, "kva" + tg), _cols(w_ukv_s), "ukv" + tg, BF16)
    qn = qf[:, :hq * QK_NOPE]
    q_pe = qf[:, hq * QK_NOPE:].astype(F32).reshape(t, hq, QK_ROPE)
    qr = _rope(q_pe, cos[:, None, :], sin[:, None, :]).transpose(1, 0, 2).astype(BF16)
    kr = _rope(k_pe.astype(F32), cos, sin).astype(BF16)
    y_a = mm(attention(qn, qr, kvf, kr, tg), _cols(w_o_mla_s), "o_mla" + tg, BF16)
    conv_w = conv_s.transpose(1, 0, 2).reshape(CONV_WIDTH, -1)
    qkv_c = conv_silu(qkv, conv_w, tg)
    o_gdn = gdn(qkv_c, z, ba, a_log.reshape(hg, 1, 1), dtb.reshape(hg, 1, 1), gn, tg)
    y_b = mm(o_gdn, _cols(w_o_gdn_s), "o_gdn" + tg, BF16)
    return mm_resid(x, mod[:, 2 * d:3 * d], gate_mix(gl, y_a, y_b, tg), _stack_rows(w_o_s), "w_o" + tg)


def _stage_ffn(x, mod, nf, w_gu_s, w_down_s, tg):
    d = x.shape[1]
    h = ada_norm(x, nf, mod[:, 4 * d:5 * d], mod[:, 3 * d:4 * d], "ffn" + tg)
    gu = mm(h, _cols(w_gu_s), "gu" + tg, BF16)
    return mm_resid(x, mod[:, 5 * d:6 * d], swiglu(gu, tg), _stack_rows(w_down_s), "down" + tg)


def _flat_row(arrs):
    v = jnp.concatenate([a.reshape(-1) for a in arrs])
    return jnp.pad(v, (0, _pad_lanes(v.shape[0]) - v.shape[0]))[None, :]


def kernel(x, c, positions, w_ada, b_ada, norm_mix, norm_ffn, w_in, q_a_norm, kv_a_norm, w_uq, w_ukv, w_o_mla, conv_w, A_log, dt_bias, gdn_norm, w_o_gdn, w_o, w_gate_up, w_down, final_norm, loss_target, m_w_ada, m_b_ada, m_norm_mix, m_norm_ffn, m_w_in, m_q_a_norm, m_kv_a_norm, m_w_uq, m_w_ukv, m_w_o_mla, m_conv_w, m_A_log, m_dt_bias, m_gdn_norm, m_w_o_gdn, m_w_o, m_w_gate_up, m_w_down, m_final_norm, v_w_ada, v_b_ada, v_norm_mix, v_norm_ffn, v_w_in, v_q_a_norm, v_kv_a_norm, v_w_uq, v_w_ukv, v_w_o_mla, v_conv_w, v_A_log, v_dt_bias, v_gdn_norm, v_w_o_gdn, v_w_o, v_w_gate_up, v_w_down, v_final_norm):
    given = dict(locals())
    t, d = x.shape[1], x.shape[2]
    n_ada = w_ada.shape[2]
    me = 4 * lax.axis_index("x") + 2 * lax.axis_index("y") + lax.axis_index("c")

    def with_own(land, own):
        return lax.dynamic_update_slice(land, own[None], (me,) + (0,) * own.ndim)

    got = exchange([c, conv_w], ["gather", "gather"], "gather_small")
    c_all, conv_g = got[0].reshape(N_DEV, d), got[1]
    c_rows = jnp.pad(c_all, ((0, 16 - N_DEV), (0, 0)))
    mod_cols = jnp.stack([_mm(c_rows, w_ada[l], "nn", F32, "ada_mod%d" % l, a_act="silu")[:N_DEV]
                          for l in range(DEPTH)], axis=1)
    mod_mine = exchange([mod_cols], ["scatter"], "scatter_mod")[0]
    mods = mod_mine.transpose(1, 0, 2).reshape(DEPTH, N_DEV * n_ada) + b_ada

    groups = [[(n, l) for n in names] for l in range(DEPTH) for names in (("w_in",), MIX_WEIGHTS, FFN_WEIGHTS)]
    gtags = [s + str(l) for l in range(DEPTH) for s in ("in", "mix", "ffn")]
    keys = [k for ks in groups for k in ks]
    mods, handles = copy_start([given[n][l].astype(BF16) for n, l in keys], "gather", mods, "gather_start")
    handles = dict(zip(keys, handles))
    own, relayed = {}, {}

    def relay(gi, carry):
        ks = groups[gi]
        srcs, lands = copy_wait([handles[k] for k in ks], 1 + len(SAME_CORE_REMOTE), carry,
                                "wait_ici_" + gtags[gi])
        own.update(zip(ks, srcs))
        carry, hs = copy_start(lands, "forward", carry, "relay_" + gtags[gi])
        relayed.update(zip(ks, hs))
        return carry

    def landed(gi, after):
        ks = groups[gi]
        _, lands = copy_wait([relayed[k] for k in ks], len(SAME_CORE_REMOTE), after, "wait_" + gtags[gi])
        return [with_own(land, own[k]) for k, land in zip(ks, lands)]

    inv_freq = 1.0 / (ROPE_THETA ** (jnp.arange(0, QK_ROPE, 2, dtype=F32) / QK_ROPE))
    ang = positions[0].astype(F32)[:, None] * inv_freq
    cos, sin = jnp.cos(ang), jnp.sin(ang)
    relay_before = {0: [0], 1: [1], 2: [2, 3], 3: [4], 4: [5], 5: []}

    def weights_for(stage, carry):
        for gi in relay_before[stage]:
            carry = relay(gi, carry)
        return carry, landed(stage, carry)

    xl = x[0]
    vjps = []
    for l in range(DEPTH):
        tg = str(l)
        mod = mods[l:l + 1]
        xl, (w_in_s,) = weights_for(3 * l, xl)
        seg, vjp_in = jax.vjp(lambda *a, tg=tg: _stage_in(*a, tg), xl, mod, norm_mix[l:l + 1], w_in_s)
        seg0, w_mix = weights_for(3 * l + 1, seg[0])
        seg = (seg0,) + tuple(seg[1:])
        xm, vjp_mix = jax.vjp(lambda *a, tg=tg: _stage_mix(*a, cos, sin, tg), xl, mod, *seg, *w_mix,
                              conv_g[:, l], q_a_norm[l:l + 1], kv_a_norm[l:l + 1], A_log[l], dt_bias[l],
                              gdn_norm[l:l + 1])
        xm, w_ffn = weights_for(3 * l + 2, xm)
        xl, vjp_ffn = jax.vjp(lambda *a, tg=tg: _stage_ffn(*a, tg), xm, mod, norm_ffn[l:l + 1], *w_ffn)
        vjps.append((vjp_in, vjp_mix, vjp_ffn))

    loss_t, g, dfn = loss_head(xl, final_norm[None, :], loss_target[0])
    loss = lax.psum(loss_t[0, 0], AXES)
    dsmall = {n: [None] * DEPTH for n in SMALL + ("conv_w",)}
    dmods = [None] * DEPTH
    sent = {}

    def send(ks, grads, carry, name):
        carry, hs = copy_start(list(grads), "scatter", carry, name)
        sent.update(zip(ks, hs))
        return carry

    for l in reversed(range(DEPTH)):
        tg = str(l)
        vjp_in, vjp_mix, vjp_ffn = vjps[l]
        dxm, dmod_f, dsmall["norm_ffn"][l], *dw = vjp_ffn(g)
        dxm = send([(n, l) for n in FFN_WEIGHTS], dw, dxm, "scatter_ffn" + tg)
        dx_m, dmod_m, *rest = vjp_mix(dxm)
        dseg, dw, rest = rest[:5], rest[5:5 + len(MIX_WEIGHTS)], rest[5 + len(MIX_WEIGHTS):]
        dseg[0] = send([(n, l) for n in MIX_WEIGHTS], dw, dseg[0], "scatter_mix" + tg)
        for n, gr in zip(("conv_w", "q_a_norm", "kv_a_norm", "A_log", "dt_bias", "gdn_norm"), rest):
            dsmall[n][l] = gr
        dx_i, dmod_i, dsmall["norm_mix"][l], dw_in = vjp_in(tuple(dseg))
        g = dx_i + dx_m
        if l > 0:
            g = send([("w_in", l)], [dw_in], g, "scatter_in" + tg)
        dmods[l] = dmod_f + dmod_m + dmod_i
    dx = g
    dmods = jnp.concatenate(dmods, axis=0)
    dconv = jnp.stack(dsmall.pop("conv_w"), axis=1)
    dsmall = {n: jnp.concatenate(v, axis=0) if v[0].ndim == 2 else jnp.stack(v)
              for n, v in dsmall.items() if v[0] is not None}
    dsmall["b_ada"] = dmods
    dsmall["final_norm"] = dfn[0]

    dmod_cols = dmods.reshape(DEPTH, N_DEV, n_ada).transpose(1, 0, 2)
    conv_parts, dmod_all, small_parts = exchange(
        [dconv, dmod_cols, _flat_row([dsmall[n] for n in SMALL])], ["scatter", "scatter", "gather"],
        "exchange_small")
    dmod_all = send([("w_in", 0)], [dw_in], dmod_all, "scatter_in0")

    res = {}
    dm_rows = jnp.pad(dmod_all, ((0, 16 - N_DEV), (0, 0), (0, 0)))
    g_ada = [_mm(c_rows, dm_rows[:, l], "tn", F32, "ada_dw%d" % l, a_act="silu")[None] for l in range(DEPTH)]
    r2 = (DEPTH * d, n_ada)
    outs = adamw(w_ada.reshape(r2), m_w_ada.reshape(r2), v_w_ada.reshape(r2), parts=g_ada, name="adamw_w_ada")
    res["w_ada"] = [o.reshape(w_ada.shape) for o in outs]
    packed = SMALL + ("conv_w",)
    p_all = jnp.concatenate([small_parts, conv_parts.reshape(N_DEV, 1, -1)], axis=2)
    pack = lambda pre: jnp.concatenate([_flat_row([given[pre + n] for n in SMALL]),
                                        given[pre + "conv_w"].reshape(1, -1)], axis=1)
    outs = adamw(pack(""), pack("m_"), pack("v_"), parts=[p_all], name="adamw_small")
    done = [res["w_ada"][1], outs[1]]
    for group, gname in ((FFN_WEIGHTS, "ffn"), (MIX_WEIGHTS, "mix"), (("w_in",), "in")):
        ks = [(n, l) for l in reversed(range(DEPTH)) for n in group]
        after = sum(lax.slice(a, (0,) * a.ndim, (1,) * a.ndim).reshape(1, 1) for a in done)
        srcs, lands = copy_wait([sent[k] for k in ks], len(ALL_PEERS), after, "scatter_wait_" + gname)
        parts = {k: with_own(land, lax.dynamic_index_in_dim(src, me, 0, keepdims=False))
                 for k, src, land in zip(ks, srcs, lands)}
        for n in group:
            w = given[n]
            r2 = (w.shape[0] * w.shape[1], w.shape[2])
            res[n] = [o.reshape(w.shape) for o in
                      adamw(w.reshape(r2), given["m_" + n].reshape(r2), given["v_" + n].reshape(r2),
                            parts=[parts[(n, l)] for l in range(DEPTH)], name="adamw_" + n)]
            done.append(res[n][1])
    off = 0
    for n in packed:
        if n == "conv_w":
            off = small_parts.shape[2]
        size = math.prod(given[n].shape)
        res[n] = [o[0, off:off + size].reshape(given[n].shape) for o in outs]
        off += size

    return (loss, dx[None]) + tuple(res[n][i] for i in range(4) for n in WEIGHTS)
```

```python
import functools
import math

import jax
import jax.numpy as jnp
from jax import lax
from jax.experimental import pallas as pl
from jax.experimental.pallas import tpu as pltpu

F32 = jnp.float32
BF16 = jnp.bfloat16

MLA_HEADS = 8
QK_NOPE = 128
QK_ROPE = 64
V_HEAD = 128
Q_LORA = 512
KV_LORA = 512
ROPE_THETA = 10000.0
GDN_HEADS = 8
GDN_DK = 128
GDN_DV = 128
CONV_WIDTH = 4
CHUNK = 64
DEPTH = 2
EPS = 1e-6
ADAM_LR = 0.001
ADAM_B1 = 0.9
ADAM_B2 = 0.999
ADAM_EPS = 1e-08
ADAM_WD = 0.01
ADAM_STEP = 10

N_DEV = 8
AXES = ("x", "y", "c")
LANE = 128
VMEM_LIMIT = 48 * 1024 * 1024
MM_VMEM_BUDGET = 36 * 1024 * 1024

NN = (((1,), (0,)), ((), ()))
NT = (((1,), (1,)), ((), ()))
TN = (((0,), (0,)), ((), ()))


def _cp(sem=None):
    return pltpu.CompilerParams(dimension_semantics=sem, vmem_limit_bytes=VMEM_LIMIT)


def _tile(n, cap):
    if n <= cap:
        return n
    for t in range(cap - cap % LANE, 0, -LANE):
        if n % t == 0:
            return t
    return n


def _rows(t, cap=256):
    return cap if t % cap == 0 else t


def _pad_lanes(n):
    return -(-n // LANE) * LANE


def _sigmoid(x):
    return 1.0 / (1.0 + jnp.exp(-x))


def _softplus(x):
    return jnp.maximum(x, 0.0) + jnp.log(1.0 + jnp.exp(-jnp.abs(x)))


def _tile_slot(n, cap):
    t = _tile(n, cap)
    return n if t < 256 < n <= 1536 else t


NARROW_SLOT = 512


def _mm_narrow_slots(a, b, dims, out_dtype, name):
    bf = lambda v: v.astype(BF16)
    dot = lambda p, q, dn: lax.dot_general(p, q, dn, preferred_element_type=F32)
    if dims == "nn":
        (m, k), per = a.shape, b.shape[-1]
        spb = min(N_DEV, max(1, 1024 // per))
        tm = _tile(m, 1024)

        def body(a_ref, b_ref, o_ref):
            av = bf(a_ref[...])
            for s in range(spb):
                o_ref[:, s * per:(s + 1) * per] = dot(av, bf(b_ref[s]), NN).astype(o_ref.dtype)

        grid = (m // tm, N_DEV // spb)
        in_specs = [pl.BlockSpec((tm, k), lambda i, j: (i, 0)), pl.BlockSpec((spb, k, per), lambda i, j: (j, 0, 0))]
        out_specs = pl.BlockSpec((tm, spb * per), lambda i, j: (i, j))
        out_shape = jax.ShapeDtypeStruct((m, N_DEV * per), out_dtype)
    elif dims == "nt":
        m, (n, per) = a.shape[0], b.shape[1:]
        tm, tn = _tile(m, 1024), _tile(n, 1024)

        def body(a_ref, b_ref, o_ref):
            av = bf(a_ref[...])
            acc = dot(av[:, :per], bf(b_ref[0]), NT)
            for s in range(1, N_DEV):
                acc += dot(av[:, s * per:(s + 1) * per], bf(b_ref[s]), NT)
            o_ref[...] = acc.astype(o_ref.dtype)

        grid = (m // tm, n // tn)
        in_specs = [pl.BlockSpec((tm, N_DEV * per), lambda i, j: (i, 0)),
                    pl.BlockSpec((N_DEV, tn, per), lambda i, j: (0, j, 0))]
        out_specs = pl.BlockSpec((tm, tn), lambda i, j: (i, j))
        out_shape = jax.ShapeDtypeStruct((m, n), out_dtype)
    else:
        (k, m), per = a.shape, b.shape[1] // N_DEV
        spb = min(N_DEV, max(1, 1024 // per))
        tm = _tile(m, 1024)

        def body(a_ref, b_ref, o_ref):
            av = bf(a_ref[...])
            for s in range(spb):
                o_ref[s] = dot(av, bf(b_ref[:, s * per:(s + 1) * per]), TN).astype(o_ref.dtype)

        grid = (m // tm, N_DEV // spb)
        in_specs = [pl.BlockSpec((k, tm), lambda i, j: (0, i)), pl.BlockSpec((k, spb * per), lambda i, j: (0, j))]
        out_specs = pl.BlockSpec((spb, tm, per), lambda i, j: (j, i, 0))
        out_shape = jax.ShapeDtypeStruct((N_DEV, m, per), out_dtype)
    return pl.pallas_call(body, name=name, grid=grid, in_specs=in_specs, out_specs=out_specs,
                          out_shape=out_shape, compiler_params=_cp(("parallel", "parallel")))(a, b)


def _mm(a, b, dims, out_dtype, name, a_act=None, slots=False, resid=None):
    if dims == "nn":
        m, k = a.shape
        n = b.shape[-1] * (N_DEV if slots else 1)
    elif dims == "nt":
        m, k = a.shape
        n = b.shape[-2]
    else:
        k, m = a.shape
        n = b.shape[-1]
    per = (k if dims == "nt" else n) // N_DEV
    if slots and per % LANE == 0 and per <= NARROW_SLOT and k <= 2048 and a_act is None and resid is None:
        return _mm_narrow_slots(a, b, dims, out_dtype, name)
    tm = _tile(m, 1536)
    tn = _tile_slot(n // N_DEV, 512) if slots and dims != "nt" else _tile(n, 512 if resid else 1024)
    k_slot = k // N_DEV if slots and dims == "nt" else k

    def vmem_bytes(tk_):
        a_b, b_b = tm * tk_ * a.dtype.itemsize, tk_ * tn * b.dtype.itemsize
        casts = (tm * tk_ * 2 if a.dtype != BF16 else 0) + (tk_ * tn * 2 if b.dtype != BF16 else 0)
        return 2 * (a_b + b_b + tm * tn * jnp.dtype(out_dtype).itemsize) + 2 * tm * tn * 4 + casts

    tk = _tile_slot(k_slot, 1536) if slots and dims == "nt" else _tile(k, 2048)
    while vmem_bytes(tk) > MM_VMEM_BUDGET and tk % (2 * LANE) == 0:
        tk //= 2
    nk = k // tk
    per_n = (n // N_DEV) // tn if slots else 1
    per_k = (k // N_DEV) // tk if slots else 1
    if dims == "tn":
        a_spec = pl.BlockSpec((tk, tm), lambda i, j, kk: (kk, i))
    else:
        a_spec = pl.BlockSpec((tm, tk), lambda i, j, kk: (i, kk))
    if dims == "nt":
        if slots:
            b_spec = pl.BlockSpec((None, tn, tk), lambda i, j, kk: (kk // per_k, j, kk % per_k))
        else:
            b_spec = pl.BlockSpec((tn, tk), lambda i, j, kk: (j, kk))
    elif dims == "nn" and slots:
        b_spec = pl.BlockSpec((None, tk, tn), lambda i, j, kk: (j // per_n, kk, j % per_n))
    else:
        b_spec = pl.BlockSpec((tk, tn), lambda i, j, kk: (kk, j))
    if dims == "tn" and slots:
        out_spec = pl.BlockSpec((None, tm, tn), lambda i, j, kk: (j // per_n, i, j % per_n))
        out_shape = jax.ShapeDtypeStruct((N_DEV, m, n // N_DEV), out_dtype)
    else:
        out_spec = pl.BlockSpec((tm, tn), lambda i, j, kk: (i, j))
        out_shape = jax.ShapeDtypeStruct((m, n), out_dtype)
    dn = {"nn": NN, "nt": NT, "tn": TN}[dims]

    def product(a_ref, b_ref):
        av = a_ref[...]
        if a_act == "silu":
            av = av * _sigmoid(av)
        return lax.dot_general(av.astype(BF16), b_ref[...].astype(BF16), dn, preferred_element_type=F32)

    def finish(acc, rest):
        if resid is None:
            (o_ref,) = rest
            o_ref[...] = acc.astype(o_ref.dtype)
        else:
            x_ref, gt_ref, o_ref, p_ref = rest
            o_ref[...] = x_ref[...] + gt_ref[...] * acc
            p_ref[...] = acc.astype(p_ref.dtype)

    def body_one(a_ref, b_ref, *rest):
        finish(product(a_ref, b_ref), rest)

    def body_acc(a_ref, b_ref, *rest):
        acc_ref = rest[-1]
        kk = pl.program_id(2)

        @pl.when(kk == 0)
        def _():
            acc_ref[...] = jnp.zeros_like(acc_ref)

        acc_ref[...] += product(a_ref, b_ref)

        @pl.when(kk == nk - 1)
        def _():
            finish(acc_ref[...], rest[:-1])

    in_specs, operands = [a_spec, b_spec], [a, b]
    if resid is not None:
        in_specs += [out_spec, pl.BlockSpec((1, tn), lambda i, j, kk: (0, j))]
        operands += list(resid)
        out_spec, out_shape = [out_spec, out_spec], [out_shape, jax.ShapeDtypeStruct((m, n), BF16)]
    return pl.pallas_call(
        body_one if nk == 1 else body_acc, name=name, grid=(m // tm, n // tn, nk),
        in_specs=in_specs, out_specs=out_spec, out_shape=out_shape,
        scratch_shapes=[] if nk == 1 else [pltpu.VMEM((tm, tn), F32)],
        compiler_params=_cp(("parallel", "parallel", "arbitrary")),
    )(*operands)


@functools.partial(jax.custom_vjp, nondiff_argnums=(2, 3))
def mm(a, b, tag, out_dtype):
    return _mm(a, b, "nn", out_dtype, "mm_" + tag, slots=b.ndim == 3)


def _mm_f(a, b, tag, out_dtype):
    return mm(a, b, tag, out_dtype), (a, b)


def _mm_b(tag, out_dtype, res, g):
    a, b = res
    slots = b.ndim == 3
    da = _mm(g, b, "nt", a.dtype, "mm_" + tag + "_da", slots=slots)
    db = _mm(a, g, "tn", b.dtype, "mm_" + tag + "_db", slots=slots)
    return da, db


mm.defvjp(_mm_f, _mm_b)


def _norm_fwd_call(x, nw, sc, sh, name):
    t, d = x.shape
    tr = _rows(t)
    mod = sc is not None
    row = pl.BlockSpec((tr, d), lambda i: (i, 0))
    vec = pl.BlockSpec((1, d), lambda i: (0, 0))

    def body(*refs):
        if mod:
            x_ref, nw_ref, sc_ref, sh_ref, o_ref = refs
        else:
            x_ref, nw_ref, o_ref = refs
        xv = x_ref[...].astype(F32)
        r = lax.rsqrt(jnp.mean(xv * xv, axis=-1, keepdims=True) + EPS)
        y = (xv * r) * nw_ref[...]
        if mod:
            y = y * (1.0 + sc_ref[...]) + sh_ref[...]
        o_ref[...] = y.astype(o_ref.dtype)

    args = (x, nw, sc, sh) if mod else (x, nw)
    return pl.pallas_call(
        body, name=name, grid=(t // tr,),
        in_specs=[row] + [vec] * (len(args) - 1), out_specs=row,
        out_shape=jax.ShapeDtypeStruct((t, d), BF16),
        compiler_params=_cp(("parallel",)),
    )(*args)


def _norm_bwd_call(x, nw, sc, dh, name):
    t, d = x.shape
    tr = _rows(t)
    mod = sc is not None
    row = pl.BlockSpec((tr, d), lambda i: (i, 0))
    vec = pl.BlockSpec((1, d), lambda i: (0, 0))

    def body(*refs):
        if mod:
            x_ref, nw_ref, sc_ref, dh_ref, dx_ref, dnw_ref, dsc_ref, dsh_ref = refs
        else:
            x_ref, nw_ref, dh_ref, dx_ref, dnw_ref = refs
        i = pl.program_id(0)
        xv = x_ref[...].astype(F32)
        dh = dh_ref[...].astype(F32)
        r = lax.rsqrt(jnp.mean(xv * xv, axis=-1, keepdims=True) + EPS)
        y = xv * r
        a = nw_ref[...] * (1.0 + sc_ref[...]) if mod else nw_ref[...]
        dy = dh * a
        dx_ref[...] = (r * (dy - y * jnp.mean(dy * y, axis=-1, keepdims=True))).astype(dx_ref.dtype)
        da = jnp.sum(dh * y, axis=0, keepdims=True)

        @pl.when(i == 0)
        def _():
            dnw_ref[...] = jnp.zeros_like(dnw_ref)
            if mod:
                dsc_ref[...] = jnp.zeros_like(dsc_ref)
                dsh_ref[...] = jnp.zeros_like(dsh_ref)

        if mod:
            dnw_ref[...] += da * (1.0 + sc_ref[...])
            dsc_ref[...] += da * nw_ref[...]
            dsh_ref[...] += jnp.sum(dh, axis=0, keepdims=True)
        else:
            dnw_ref[...] += da

    args = (x, nw, sc, dh) if mod else (x, nw, dh)
    n_vec = 3 if mod else 1
    return pl.pallas_call(
        body, name=name, grid=(t // tr,),
        in_specs=[row] + [vec] * (len(args) - 2) + [row],
        out_specs=[row] + [vec] * n_vec,
        out_shape=[jax.ShapeDtypeStruct((t, d), x.dtype)] + [jax.ShapeDtypeStruct((1, d), F32)] * n_vec,
        compiler_params=_cp(("arbitrary",)),
    )(*args)


@functools.partial(jax.custom_vjp, nondiff_argnums=(4,))
def ada_norm(x, nw, sc, sh, tag):
    return _norm_fwd_call(x, nw, sc, sh, "adanorm_" + tag)


def _ada_norm_f(x, nw, sc, sh, tag):
    return _norm_fwd_call(x, nw, sc, sh, "adanorm_" + tag), (x, nw, sc)


def _ada_norm_b(tag, res, dh):
    x, nw, sc = res
    dx, dnw, dsc, dsh = _norm_bwd_call(x, nw, sc, dh, "adanorm_" + tag + "_bwd")
    return dx, dnw, dsc, dsh


ada_norm.defvjp(_ada_norm_f, _ada_norm_b)


@functools.partial(jax.custom_vjp, nondiff_argnums=(2,))
def rms_norm(x, nw, tag):
    return _norm_fwd_call(x, nw, None, None, "rms_" + tag)


def _rms_norm_f(x, nw, tag):
    return _norm_fwd_call(x, nw, None, None, "rms_" + tag), (x, nw)


def _rms_norm_b(tag, res, dh):
    x, nw = res
    dx, dnw = _norm_bwd_call(x, nw, None, dh, "rms_" + tag + "_bwd")
    return dx, dnw


rms_norm.defvjp(_rms_norm_f, _rms_norm_b)


def _gate_mix_fwd_call(gl, ya, yb, name):
    t, d = ya.shape
    tr = _rows(t)
    row = pl.BlockSpec((tr, d), lambda i: (i, 0))

    def body(ga_ref, gb_ref, ya_ref, yb_ref, o_ref):
        o_ref[...] = (_sigmoid(ga_ref[...].astype(F32)) * ya_ref[...].astype(F32)
                      + _sigmoid(gb_ref[...].astype(F32)) * yb_ref[...].astype(F32)).astype(o_ref.dtype)

    return pl.pallas_call(
        body, name=name, grid=(t // tr,),
        in_specs=[row, pl.BlockSpec((tr, d), lambda i: (i, 1)), row, row], out_specs=row,
        out_shape=jax.ShapeDtypeStruct((t, d), BF16),
        compiler_params=_cp(("parallel",)),
    )(gl, gl, ya, yb)


def _gate_mix_bwd_call(gl, ya, yb, dm, name):
    t, d = ya.shape
    tr = _rows(t)
    row = pl.BlockSpec((tr, d), lambda i: (i, 0))
    wide = pl.BlockSpec((tr, 2 * d), lambda i: (i, 0))

    def body(gl_ref, ya_ref, yb_ref, dm_ref, dgl_ref, dya_ref, dyb_ref):
        dm = dm_ref[...].astype(F32)
        ga = _sigmoid(gl_ref[:, :d].astype(F32))
        gb = _sigmoid(gl_ref[:, d:].astype(F32))
        dya_ref[...] = (dm * ga).astype(dya_ref.dtype)
        dyb_ref[...] = (dm * gb).astype(dyb_ref.dtype)
        dgl_ref[:, :d] = (dm * ya_ref[...].astype(F32) * ga * (1.0 - ga)).astype(dgl_ref.dtype)
        dgl_ref[:, d:] = (dm * yb_ref[...].astype(F32) * gb * (1.0 - gb)).astype(dgl_ref.dtype)

    return pl.pallas_call(
        body, name=name, grid=(t // tr,),
        in_specs=[wide, row, row, row], out_specs=[wide, row, row],
        out_shape=[jax.ShapeDtypeStruct((t, 2 * d), gl.dtype), jax.ShapeDtypeStruct((t, d), ya.dtype),
                   jax.ShapeDtypeStruct((t, d), yb.dtype)],
        compiler_params=_cp(("parallel",)),
    )(gl, ya, yb, dm)


@functools.partial(jax.custom_vjp, nondiff_argnums=(3,))
def gate_mix(gl, ya, yb, tag):
    return _gate_mix_fwd_call(gl, ya, yb, "gatemix_" + tag)


def _gate_mix_f(gl, ya, yb, tag):
    return _gate_mix_fwd_call(gl, ya, yb, "gatemix_" + tag), (gl, ya, yb)


def _gate_mix_b(tag, res, dm):
    return tuple(_gate_mix_bwd_call(*res, dm, "gatemix_" + tag + "_bwd"))


gate_mix.defvjp(_gate_mix_f, _gate_mix_b)


def _resid_bwd_call(gt, p, g, name):
    t, d = p.shape
    tr = _rows(t)
    row = pl.BlockSpec((tr, d), lambda i: (i, 0))
    vec = pl.BlockSpec((1, d), lambda i: (0, 0))

    def body(gt_ref, p_ref, g_ref, dp_ref, dgt_ref):
        i = pl.program_id(0)
        g = g_ref[...]
        dp_ref[...] = (g * gt_ref[...]).astype(dp_ref.dtype)

        @pl.when(i == 0)
        def _():
            dgt_ref[...] = jnp.zeros_like(dgt_ref)

        dgt_ref[...] += jnp.sum(g * p_ref[...].astype(F32), axis=0, keepdims=True)

    return pl.pallas_call(
        body, name=name, grid=(t // tr,), in_specs=[vec, row, row], out_specs=[row, vec],
        out_shape=[jax.ShapeDtypeStruct((t, d), BF16), jax.ShapeDtypeStruct((1, d), F32)],
        compiler_params=_cp(("arbitrary",)),
    )(gt, p, g)


@functools.partial(jax.custom_vjp, nondiff_argnums=(4,))
def mm_resid(x, gt, a, b, tag):
    return _mm(a, b, "nn", F32, "mmres_" + tag, resid=(x, gt))[0]


def _mm_resid_f(x, gt, a, b, tag):
    o, p = _mm(a, b, "nn", F32, "mmres_" + tag, resid=(x, gt))
    return o, (gt, a, b, p)


def _mm_resid_b(tag, res, g):
    gt, a, b, p = res
    dp, dgt = _resid_bwd_call(gt, p, g, "mmres_" + tag + "_gate")
    da = _mm(dp, b, "nt", a.dtype, "mmres_" + tag + "_da")
    db = _mm(a, dp, "tn", b.dtype, "mmres_" + tag + "_db")
    return g, dgt, da, db


mm_resid.defvjp(_mm_resid_f, _mm_resid_b)


def _swiglu_fwd_call(gu, name):
    t, f2 = gu.shape
    f = f2 // 2
    tr = _rows(t, 128)
    half = pl.BlockSpec((tr, f), lambda i: (i, 0))

    def body(g_ref, u_ref, o_ref):
        g = g_ref[...].astype(F32)
        o_ref[...] = (g * _sigmoid(g) * u_ref[...].astype(F32)).astype(o_ref.dtype)

    return pl.pallas_call(
        body, name=name, grid=(t // tr,),
        in_specs=[half, pl.BlockSpec((tr, f), lambda i: (i, 1))], out_specs=half,
        out_shape=jax.ShapeDtypeStruct((t, f), BF16), compiler_params=_cp(("parallel",)),
    )(gu, gu)


def _swiglu_bwd_call(gu, da, name):
    t, f2 = gu.shape
    f = f2 // 2
    tr = _rows(t, 128)
    wide = pl.BlockSpec((tr, f2), lambda i: (i, 0))

    def body(gu_ref, da_ref, dgu_ref):
        g = gu_ref[:, :f].astype(F32)
        u = gu_ref[:, f:].astype(F32)
        da = da_ref[...].astype(F32)
        s = _sigmoid(g)
        dgu_ref[:, :f] = (da * u * s * (1.0 + g * (1.0 - s))).astype(dgu_ref.dtype)
        dgu_ref[:, f:] = (da * g * s).astype(dgu_ref.dtype)

    return pl.pallas_call(
        body, name=name, grid=(t // tr,),
        in_specs=[wide, pl.BlockSpec((tr, f), lambda i: (i, 0))], out_specs=wide,
        out_shape=jax.ShapeDtypeStruct((t, f2), gu.dtype), compiler_params=_cp(("parallel",)),
    )(gu, da)


@functools.partial(jax.custom_vjp, nondiff_argnums=(1,))
def swiglu(gu, tag):
    return _swiglu_fwd_call(gu, "swiglu_" + tag)


def _swiglu_f(gu, tag):
    return _swiglu_fwd_call(gu, "swiglu_" + tag), (gu,)


def _swiglu_b(tag, res, da):
    return (_swiglu_bwd_call(res[0], da, "swiglu_" + tag + "_bwd"),)


swiglu.defvjp(_swiglu_f, _swiglu_b)


def loss_head(x, fw, tgt):
    t, d = x.shape
    tr = _rows(t)
    row = pl.BlockSpec((tr, d), lambda i: (i, 0))
    vec = pl.BlockSpec((1, d), lambda i: (0, 0))
    tile = pl.BlockSpec((8, LANE), lambda i: (0, 0))

    def body(x_ref, fw_ref, tgt_ref, loss_ref, dx_ref, dfw_ref):
        i = pl.program_id(0)
        xv = x_ref[...]
        fw = fw_ref[...]
        r = lax.rsqrt(jnp.mean(xv * xv, axis=-1, keepdims=True) + EPS)
        yh = xv * r
        e = yh * fw - tgt_ref[...]
        dy = e * (1.0 / d)
        dyw = dy * fw
        dx_ref[...] = r * (dyw - yh * jnp.mean(dyw * yh, axis=-1, keepdims=True))

        @pl.when(i == 0)
        def _():
            loss_ref[...] = jnp.zeros_like(loss_ref)
            dfw_ref[...] = jnp.zeros_like(dfw_ref)

        loss_ref[...] += 0.5 * jnp.sum(jnp.mean(e * e, axis=-1, keepdims=True))
        dfw_ref[...] += jnp.sum(dy * yh, axis=0, keepdims=True)

    return pl.pallas_call(
        body, name="loss_head", grid=(t // tr,), in_specs=[row, vec, row],
        out_specs=[tile, row, vec],
        out_shape=[jax.ShapeDtypeStruct((8, LANE), F32), jax.ShapeDtypeStruct((t, d), F32),
                   jax.ShapeDtypeStruct((1, d), F32)],
        compiler_params=_cp(("arbitrary",)),
    )(x, fw, tgt)


def _attn_scores(qn_ref, qr_ref, kn_ref, kr_ref, diag):
    tq = qn_ref.shape[0]
    s = lax.dot_general(qn_ref[...].astype(BF16), kn_ref[...].astype(BF16), NT, preferred_element_type=F32)
    s += lax.dot_general(qr_ref[...].astype(BF16), kr_ref[...].astype(BF16), NT, preferred_element_type=F32)
    s = s * (QK_NOPE + QK_ROPE) ** -0.5
    if diag:
        rows = lax.broadcasted_iota(jnp.int32, (tq, tq), 0)
        cols = lax.broadcasted_iota(jnp.int32, (tq, tq), 1)
        s = jnp.where(cols <= rows, s, -1e30)
    return s


def _attn_fwd_call(qn, qr, kv, kr, name):
    t = qn.shape[0]
    h_n = MLA_HEADS
    tq = _rows(t, 512)
    nq = t // tq
    assert V_HEAD == LANE and tq % LANE == 0

    def body(qn_ref, qr_ref, kn_ref, v_ref, kr_ref, o_ref, lse_ref, m_scr, l_scr, acc_scr):
        i, j = pl.program_id(1), pl.program_id(2)

        @pl.when(j == 0)
        def _():
            m_scr[...] = jnp.full_like(m_scr, -1e30)
            l_scr[...] = jnp.zeros_like(l_scr)
            acc_scr[...] = jnp.zeros_like(acc_scr)

        def step(diag):
            s = _attn_scores(qn_ref, qr_ref, kn_ref, kr_ref, diag)
            m_old = m_scr[...]
            m_new = jnp.maximum(m_old, jnp.max(s, axis=-1, keepdims=True))
            p = jnp.exp(s - jnp.tile(m_new, (1, tq // LANE)))
            alpha = jnp.exp(m_old - m_new)
            l_scr[...] = alpha * l_scr[...] + jnp.sum(p, axis=-1, keepdims=True)
            acc_scr[...] = alpha * acc_scr[...] + jnp.dot(p.astype(BF16), v_ref[...].astype(BF16),
                                                           preferred_element_type=F32)
            m_scr[...] = m_new

        @pl.when(j < i)
        def _():
            step(False)

        @pl.when(j == i)
        def _():
            step(True)
            o_ref[...] = (acc_scr[...] / l_scr[...]).astype(o_ref.dtype)
            lse_ref[...] = (m_scr[...] + jnp.log(l_scr[...]))[:, :1]

    return pl.pallas_call(
        body, name=name, grid=(h_n, nq, nq),
        in_specs=[
            pl.BlockSpec((tq, QK_NOPE), lambda h, i, j: (i, h)),
            pl.BlockSpec((None, tq, QK_ROPE), lambda h, i, j: (h, i, 0)),
            pl.BlockSpec((tq, QK_NOPE), lambda h, i, j: (jnp.minimum(j, i), 2 * h)),
            pl.BlockSpec((tq, V_HEAD), lambda h, i, j: (jnp.minimum(j, i), 2 * h + 1)),
            pl.BlockSpec((tq, QK_ROPE), lambda h, i, j: (jnp.minimum(j, i), 0)),
        ],
        out_specs=[
            pl.BlockSpec((tq, V_HEAD), lambda h, i, j: (i, h)),
            pl.BlockSpec((None, tq, 1), lambda h, i, j: (h, i, 0)),
        ],
        out_shape=[jax.ShapeDtypeStruct((t, h_n * V_HEAD), BF16),
                   jax.ShapeDtypeStruct((h_n, t, 1), F32)],
        scratch_shapes=[pltpu.VMEM((tq, LANE), F32), pltpu.VMEM((tq, LANE), F32),
                        pltpu.VMEM((tq, V_HEAD), F32)],
        compiler_params=_cp(("parallel", "parallel", "arbitrary")),
    )(qn, qr, kv, kv, kr)


def _attn_bwd_call(qn, qr, kv, kr, o, lse, do, name):
    t = qn.shape[0]
    h_n = MLA_HEADS
    tq = _rows(t, 512)
    nq = t // tq
    scale = (QK_NOPE + QK_ROPE) ** -0.5

    def body(qn_ref, qr_ref, kn_ref, v_ref, kr_ref, o_ref, lse_ref, do_ref,
             dqn_ref, dqr_ref, dkv_ref, dkr_ref, dqn_scr, dqr_scr, dkn_scr, dv_scr, dkr_scr):
        j, i = pl.program_id(1), pl.program_id(2)

        @pl.when(jnp.logical_and(j == 0, i == 0))
        def _():
            dqn_scr[...] = jnp.zeros_like(dqn_scr)
            dqr_scr[...] = jnp.zeros_like(dqr_scr)

        @pl.when(i == 0)
        def _():
            dkn_scr[...] = jnp.zeros_like(dkn_scr)
            dv_scr[...] = jnp.zeros_like(dv_scr)
            dkr_scr[...] = jnp.zeros_like(dkr_scr)

        def step(diag):
            qn_b = qn_ref[...].astype(BF16)
            qr_b = qr_ref[...].astype(BF16)
            kn_b = kn_ref[...].astype(BF16)
            kr_b = kr_ref[...].astype(BF16)
            do_b = do_ref[...]
            p = jnp.exp(_attn_scores(qn_ref, qr_ref, kn_ref, kr_ref, diag) - lse_ref[...])
            delta = jnp.sum(do_b.astype(F32) * o_ref[...].astype(F32), axis=-1, keepdims=True)
            dp = lax.dot_general(do_b, v_ref[...].astype(BF16), NT, preferred_element_type=F32)
            ds = (p * (dp - delta) * scale).astype(BF16)
            p_b = p.astype(BF16)
            dv_scr[...] += lax.dot_general(p_b, do_b, TN, preferred_element_type=F32)
            dkn_scr[...] += lax.dot_general(ds, qn_b, TN, preferred_element_type=F32)
            dkr_scr[...] += lax.dot_general(ds, qr_b, TN, preferred_element_type=F32)
            sl = pl.ds(pl.multiple_of(i * tq, tq), tq)
            dqn_scr[sl, :] += jnp.dot(ds, kn_b, preferred_element_type=F32)
            dqr_scr[sl, :] += jnp.dot(ds, kr_b, preferred_element_type=F32)

        @pl.when(i > j)
        def _():
            step(False)

        @pl.when(i == j)
        def _():
            step(True)

        @pl.when(i == nq - 1)
        def _():
            dkv_ref[:, :QK_NOPE] = dkn_scr[...].astype(dkv_ref.dtype)
            dkv_ref[:, QK_NOPE:] = dv_scr[...].astype(dkv_ref.dtype)
            dkr_ref[...] = dkr_scr[...]

        @pl.when(jnp.logical_and(j == nq - 1, i == nq - 1))
        def _():
            dqn_ref[...] = dqn_scr[...].astype(dqn_ref.dtype)
            dqr_ref[...] = dqr_scr[...].astype(dqr_ref.dtype)

    qi = lambda j, i: jnp.maximum(i, j)
    return pl.pallas_call(
        body, name=name, grid=(h_n, nq, nq),
        in_specs=[
            pl.BlockSpec((tq, QK_NOPE), lambda h, j, i: (qi(j, i), h)),
            pl.BlockSpec((None, tq, QK_ROPE), lambda h, j, i: (h, qi(j, i), 0)),
            pl.BlockSpec((tq, QK_NOPE), lambda h, j, i: (j, 2 * h)),
            pl.BlockSpec((tq, V_HEAD), lambda h, j, i: (j, 2 * h + 1)),
            pl.BlockSpec((tq, QK_ROPE), lambda h, j, i: (j, 0)),
            pl.BlockSpec((tq, V_HEAD), lambda h, j, i: (qi(j, i), h)),
            pl.BlockSpec((None, tq, 1), lambda h, j, i: (h, qi(j, i), 0)),
            pl.BlockSpec((tq, V_HEAD), lambda h, j, i: (qi(j, i), h)),
        ],
        out_specs=[
            pl.BlockSpec((t, QK_NOPE), lambda h, j, i: (0, h)),
            pl.BlockSpec((None, t, QK_ROPE), lambda h, j, i: (h, 0, 0)),
            pl.BlockSpec((tq, QK_NOPE + V_HEAD), lambda h, j, i: (j, h)),
            pl.BlockSpec((None, tq, QK_ROPE), lambda h, j, i: (h, j, 0)),
        ],
        out_shape=[jax.ShapeDtypeStruct((t, h_n * QK_NOPE), qn.dtype),
                   jax.ShapeDtypeStruct((h_n, t, QK_ROPE), qr.dtype),
                   jax.ShapeDtypeStruct((t, h_n * (QK_NOPE + V_HEAD)), kv.dtype),
                   jax.ShapeDtypeStruct((h_n, t, QK_ROPE), F32)],
        scratch_shapes=[pltpu.VMEM((t, QK_NOPE), F32), pltpu.VMEM((t, QK_ROPE), F32),
                        pltpu.VMEM((tq, QK_NOPE), F32), pltpu.VMEM((tq, V_HEAD), F32),
                        pltpu.VMEM((tq, QK_ROPE), F32)],
        compiler_params=_cp(("parallel", "arbitrary", "arbitrary")),
    )(qn, qr, kv, kv, kr, o, lse, do)


@functools.partial(jax.custom_vjp, nondiff_argnums=(4,))
def attention(qn, qr, kv, kr, tag):
    return _attn_fwd_call(qn, qr, kv, kr, "attn_" + tag)[0]


def _attention_f(qn, qr, kv, kr, tag):
    o, lse = _attn_fwd_call(qn, qr, kv, kr, "attn_" + tag)
    return o, (qn, qr, kv, kr, o, lse)


def _attention_b(tag, res, do):
    dqn, dqr, dkv, dkr_h = _attn_bwd_call(*res, do, "attn_" + tag + "_bwd")
    return dqn, dqr, dkv, jnp.sum(dkr_h, axis=0).astype(res[3].dtype)


attention.defvjp(_attention_f, _attention_b)


def _shift_down(u, s):
    if s == 0:
        return u
    t = u.shape[0]
    rolled = pltpu.roll(u, s, 0)
    return jnp.where(lax.broadcasted_iota(jnp.int32, u.shape, 0) >= s, rolled, 0.0)


def _shift_up(u, s):
    if s == 0:
        return u
    t = u.shape[0]
    rolled = pltpu.roll(u, t - s, 0)
    return jnp.where(lax.broadcasted_iota(jnp.int32, u.shape, 0) < t - s, rolled, 0.0)


def _conv_blocks(t, c3):
    p = c3 // 3
    tc = _tile(p, 512)
    per = p // tc
    return p, tc, per


def _conv_fwd_call(u, w, name):
    t, c3 = u.shape
    p, tc, per = _conv_blocks(t, c3)

    def body(u_ref, w_ref, o_ref):
        u = u_ref[...].astype(F32)
        y = jnp.zeros_like(u)
        for j in range(CONV_WIDTH):
            y = y + w_ref[j:j + 1, :] * _shift_down(u, CONV_WIDTH - 1 - j)
        o_ref[...] = y * _sigmoid(y)

    return pl.pallas_call(
        body, name=name, grid=(c3 // tc,),
        in_specs=[pl.BlockSpec((t, tc), lambda cb: (0, cb)),
                  pl.BlockSpec((CONV_WIDTH, tc), lambda cb: (0, cb))],
        out_specs=pl.BlockSpec((None, t, tc), lambda cb: (cb // per, 0, cb % per)),
        out_shape=jax.ShapeDtypeStruct((3, t, p), F32),
        compiler_params=_cp(("parallel",)),
    )(u, w)


def _conv_bwd_call(u, w, do, name):
    t, c3 = u.shape
    p, tc, per = _conv_blocks(t, c3)

    def body(u_ref, w_ref, do_ref, du_ref, dw_ref):
        u = u_ref[...].astype(F32)
        shifted = [_shift_down(u, CONV_WIDTH - 1 - j) for j in range(CONV_WIDTH)]
        y = jnp.zeros_like(u)
        for j in range(CONV_WIDTH):
            y = y + w_ref[j:j + 1, :] * shifted[j]
        s = _sigmoid(y)
        dy = do_ref[...] * s * (1.0 + y * (1.0 - s))
        du = jnp.zeros_like(u)
        for j in range(CONV_WIDTH):
            du = du + w_ref[j:j + 1, :] * _shift_up(dy, CONV_WIDTH - 1 - j)
            dw_ref[j:j + 1, :] = jnp.sum(dy * shifted[j], axis=0, keepdims=True)
        du_ref[...] = du.astype(du_ref.dtype)

    return pl.pallas_call(
        body, name=name, grid=(c3 // tc,),
        in_specs=[pl.BlockSpec((t, tc), lambda cb: (0, cb)),
                  pl.BlockSpec((CONV_WIDTH, tc), lambda cb: (0, cb)),
                  pl.BlockSpec((None, t, tc), lambda cb: (cb // per, 0, cb % per))],
        out_specs=[pl.BlockSpec((t, tc), lambda cb: (0, cb)),
                   pl.BlockSpec((CONV_WIDTH, tc), lambda cb: (0, cb))],
        out_shape=[jax.ShapeDtypeStruct((t, c3), u.dtype), jax.ShapeDtypeStruct((CONV_WIDTH, c3), F32)],
        compiler_params=_cp(("parallel",)),
    )(u, w, do)


@functools.partial(jax.custom_vjp, nondiff_argnums=(2,))
def conv_silu(u, w, tag):
    return _conv_fwd_call(u, w, "conv_" + tag)


def _conv_silu_f(u, w, tag):
    return _conv_fwd_call(u, w, "conv_" + tag), (u, w)


def _conv_silu_b(tag, res, do):
    return tuple(_conv_bwd_call(*res, do, "conv_" + tag + "_bwd"))


conv_silu.defvjp(_conv_silu_f, _conv_silu_b)


BNN = (((2,), (1,)), ((0,), (0,)))
BNT = (((2,), (2,)), ((0,), (0,)))
BTN = (((1,), (1,)), ((0,), (0,)))


def _bf16_dot(a, b, dn):
    return lax.dot_general(a.astype(BF16), b.astype(BF16), dn, preferred_element_type=F32)


def _mask_dot(mask, v, dn):
    p1 = v.astype(BF16)
    r1 = v - p1.astype(F32)
    p2 = r1.astype(BF16)
    p3 = r1 - p2.astype(F32)
    return _bf16_dot(mask, p1, dn) + (_bf16_dot(mask, p2, dn) + _bf16_dot(mask, p3, dn))


@jax.custom_vjp
def _xdot(mask, v):
    return _mask_dot(mask, v, BNN)


def _xdot_f(mask, v):
    return _mask_dot(mask, v, BNN), (mask,)


def _xdot_b(res, g):
    (mask,) = res
    return jnp.zeros_like(mask), _mask_dot(mask, g, BTN)


_xdot.defvjp(_xdot_f, _xdot_b)


def _dot3(a, b, dn):
    ah, bh = a.astype(BF16), b.astype(BF16)
    al, bl = a - ah.astype(F32), b - bh.astype(F32)
    return _bf16_dot(ah, bh, dn) + (_bf16_dot(ah, bl, dn) + _bf16_dot(al, bh, dn))


def _transposed(dn, a, b, g):
    if dn == BNN:
        return (g, b, BNT), (a, g, BTN)
    if dn == BNT:
        return (g, b, BNN), (g, a, BTN)
    return (b, g, BNT), (a, g, BNN)


@functools.partial(jax.custom_vjp, nondiff_argnums=(2,))
def _hdot(a, b, dn=BNN):
    return _dot3(a, b, dn)


def _hdot_f(a, b, dn):
    return _dot3(a, b, dn), (a, b)


def _hdot_b(dn, res, g):
    da, db = _transposed(dn, *res, g)
    return _dot3(*da), _dot3(*db)


_hdot.defvjp(_hdot_f, _hdot_b)


@functools.partial(jax.custom_vjp, nondiff_argnums=(2,))
def _bdot(a, b, dn=BNN):
    return _bf16_dot(a, b, dn)


def _bdot_f(a, b, dn):
    return _bf16_dot(a, b, dn), (a, b)


def _bdot_b(dn, res, g):
    da, db = _transposed(dn, *res, g)
    return _bf16_dot(*da), _bf16_dot(*db)


_bdot.defvjp(_bdot_f, _bdot_b)


GDN_HEADS_PER_STEP = 8
GDN_HEADS_PER_STEP_BWD = 8


def _gdn_chunk(q, k, v, z, bl, al, a_log, dtb, gn, s):
    b, c = q.shape[0], q.shape[1]
    ri = lax.broadcasted_iota(jnp.int32, (c, c), 0)
    ci = lax.broadcasted_iota(jnp.int32, (c, c), 1)
    lower = (ri >= ci)[None]
    strict = (ri > ci)[None]
    low_incl = jnp.broadcast_to((ri >= ci).astype(F32), (b, c, c))
    up_incl = jnp.broadcast_to((ri <= ci).astype(F32), (b, c, c))
    eye = (ri == ci).astype(F32)[None]

    q = q * lax.rsqrt(jnp.sum(q * q, axis=-1, keepdims=True) + EPS) * (GDN_DK ** -0.5)
    k = k * lax.rsqrt(jnp.sum(k * k, axis=-1, keepdims=True) + EPS)
    beta = _sigmoid(bl)
    g = -jnp.exp(a_log) * _softplus(al + dtb)
    g_w = jnp.broadcast_to(g, (b, c, LANE))
    gc = _xdot(low_incl, g_w)
    gr = _xdot(jnp.ones((b, c, c), F32), g_w[:, :, :c] * up_incl)
    diff = gc[:, :, :c] - gr
    decay = jnp.where(lower, jnp.exp(jnp.where(lower, diff, 0.0)), 0.0)
    kb = k * beta
    lmat = jnp.where(strict, _bdot(kb, k, BNT) * decay, 0.0)
    inv = eye - lmat
    pw = lmat
    for _ in range(int(math.log2(c)) - 1):
        pw = _hdot(pw, pw)
        inv = _hdot(inv, eye + pw)
    eg = jnp.exp(gc)
    u = _hdot(inv, v * beta)
    w = _hdot(inv, kb * eg)
    attn = jnp.where(lower, _bdot(q, k, BNT) * decay, 0.0)
    v_new = u - _bdot(w, s)
    o = _bdot(q * eg, s) + _bdot(attn, v_new)
    g_last = jnp.sum(g_w, axis=1, keepdims=True)
    k_dec = k * jnp.exp(g_last - gc)
    s_new = s * jnp.exp(g_last) + _bdot(k_dec, v_new, BTN)
    on = o * lax.rsqrt(jnp.mean(o * o, axis=-1, keepdims=True) + EPS) * gn
    return on * (z * _sigmoid(z)), s_new


def _head_cols(ba, first, count):
    lane = lax.broadcasted_iota(jnp.int32, ba.shape, 1)
    return jnp.stack([jnp.sum(jnp.where(lane == first + j, ba, 0.0), axis=1, keepdims=True)
                      for j in range(count)])


def _gdn_heads(q, k, v, z, ba, a_log, dtb, gn, s):
    h_n = q.shape[0]
    return _gdn_chunk(q, k, v, z, _head_cols(ba, 0, h_n), _head_cols(ba, h_n, h_n), a_log, dtb, gn, s)


def _gdn_specs(n_chunks, hb, rev):
    c = CHUNK
    nn = (lambda n: n_chunks - 1 - n) if rev else (lambda n: n)
    plane = lambda pidx: pl.BlockSpec((None, c, hb * GDN_DK), lambda hg, n: (pidx, nn(n), hg))
    assert hb == GDN_HEADS
    logits = pl.BlockSpec((c, LANE), lambda hg, n: (nn(n), 0))
    scal = pl.BlockSpec((hb, 1, 1), lambda hg, n: (hg, 0, 0))
    zspec = pl.BlockSpec((c, hb * GDN_DV), lambda hg, n: (nn(n), hg))
    gnspec = pl.BlockSpec((1, GDN_DV), lambda hg, n: (0, 0))
    sspec = pl.BlockSpec((hb, None, GDN_DK, GDN_DV), lambda hg, n: (hg, nn(n), 0, 0))
    return plane, logits, scal, zspec, gnspec, sspec


def _heads_per_step(want):
    return math.gcd(want, GDN_HEADS)


def _heads(ref, hb):
    return jnp.stack([ref[:, j * GDN_DK:(j + 1) * GDN_DK] for j in range(hb)])


def _gdn_fwd_call(qkv, z, ba, a_log, dtb, gn, name):
    t = z.shape[0]
    h_n = GDN_HEADS
    hb = _heads_per_step(GDN_HEADS_PER_STEP)
    n_chunks = t // CHUNK
    plane, logits, scal, zspec, gnspec, sspec = _gdn_specs(n_chunks, hb, False)

    def body(q_ref, k_ref, v_ref, z_ref, ba_ref, a_ref, dtb_ref, gn_ref, o_ref, sall_ref, s_scr):
        n = pl.program_id(1)

        @pl.when(n == 0)
        def _():
            s_scr[...] = jnp.zeros_like(s_scr)

        s = s_scr[...]
        sall_ref[...] = s
        o, s_new = _gdn_heads(_heads(q_ref, hb), _heads(k_ref, hb), _heads(v_ref, hb),
                              _heads(z_ref, hb).astype(F32),
                              ba_ref[...], a_ref[...], dtb_ref[...], gn_ref[...], s)
        for j in range(hb):
            o_ref[:, j * GDN_DV:(j + 1) * GDN_DV] = o[j].astype(o_ref.dtype)
        s_scr[...] = s_new

    return pl.pallas_call(
        body, name=name, grid=(h_n // hb, n_chunks),
        in_specs=[plane(0), plane(1), plane(2), zspec, logits, scal, scal, gnspec],
        out_specs=[zspec, sspec],
        out_shape=[jax.ShapeDtypeStruct((t, h_n * GDN_DV), BF16),
                   jax.ShapeDtypeStruct((h_n, n_chunks, GDN_DK, GDN_DV), F32)],
        scratch_shapes=[pltpu.VMEM((hb, GDN_DK, GDN_DV), F32)],
        compiler_params=_cp(("parallel", "arbitrary")),
    )(qkv, qkv, qkv, z, ba, a_log, dtb, gn)


def _gdn_bwd_call(qkv, z, ba, a_log, dtb, gn, sall, do, name):
    t = z.shape[0]
    h_n = GDN_HEADS
    hb = _heads_per_step(GDN_HEADS_PER_STEP_BWD)
    n_chunks = t // CHUNK
    c = CHUNK
    plane, logits, scal, zspec, gnspec, sspec = _gdn_specs(n_chunks, hb, True)
    dplanes = pl.BlockSpec((3, c, hb * GDN_DK), lambda hg, n: (0, n_chunks - 1 - n, hg))
    gnh = pl.BlockSpec((None, 1, GDN_DV), lambda hg, n: (hg, 0, 0))

    def body(q_ref, k_ref, v_ref, z_ref, ba_ref, a_ref, dtb_ref, gn_ref, s_ref, do_ref,
             dqkv_ref, dz_ref, dba_ref, da_ref, ddtb_ref, dgn_ref, ds_scr):
        n = pl.program_id(1)

        @pl.when(n == 0)
        def _():
            ds_scr[...] = jnp.zeros_like(ds_scr)
            da_ref[...] = jnp.zeros_like(da_ref)
            ddtb_ref[...] = jnp.zeros_like(ddtb_ref)
            dgn_ref[...] = jnp.zeros_like(dgn_ref)

        _, vjp = jax.vjp(_gdn_heads, _heads(q_ref, hb), _heads(k_ref, hb), _heads(v_ref, hb),
                         _heads(z_ref, hb).astype(F32),
                         ba_ref[...], a_ref[...], dtb_ref[...], gn_ref[...], s_ref[...])
        dq, dk, dv, dz, dba, da, ddtb, dgn, ds = vjp((_heads(do_ref, hb).astype(F32), ds_scr[...]))
        for j in range(hb):
            hs = slice(j * GDN_DK, (j + 1) * GDN_DK)
            dqkv_ref[0, :, hs] = dq[j]
            dqkv_ref[1, :, hs] = dk[j]
            dqkv_ref[2, :, hs] = dv[j]
            dz_ref[:, hs] = dz[j].astype(dz_ref.dtype)
        dba_ref[...] = dba
        da_ref[...] += da
        ddtb_ref[...] += ddtb
        dgn_ref[...] += dgn
        ds_scr[...] = ds

    return pl.pallas_call(
        body, name=name, grid=(h_n // hb, n_chunks),
        in_specs=[plane(0), plane(1), plane(2), zspec, logits, scal, scal, gnspec, sspec, zspec],
        out_specs=[dplanes, zspec, logits, scal, scal, gnh],
        out_shape=[jax.ShapeDtypeStruct((3, t, h_n * GDN_DK), F32),
                   jax.ShapeDtypeStruct((t, h_n * GDN_DV), z.dtype),
                   jax.ShapeDtypeStruct((t, LANE), F32),
                   jax.ShapeDtypeStruct((h_n, 1, 1), F32), jax.ShapeDtypeStruct((h_n, 1, 1), F32),
                   jax.ShapeDtypeStruct((h_n // hb, 1, GDN_DV), F32)],
        scratch_shapes=[pltpu.VMEM((hb, GDN_DK, GDN_DV), F32)],
        compiler_params=_cp(("parallel", "arbitrary")),
    )(qkv, qkv, qkv, z, ba, a_log, dtb, gn, sall, do)


@functools.partial(jax.custom_vjp, nondiff_argnums=(6,))
def gdn(qkv, z, ba, a_log, dtb, gn, tag):
    return _gdn_fwd_call(qkv, z, ba, a_log, dtb, gn, "gdn_" + tag)[0]


def _gdn_f(qkv, z, ba, a_log, dtb, gn, tag):
    o, sall = _gdn_fwd_call(qkv, z, ba, a_log, dtb, gn, "gdn_" + tag)
    return o, (qkv, z, ba, a_log, dtb, gn, sall)


def _gdn_b(tag, res, do):
    dqkv, dz, dba, da, ddtb, dgn_h = _gdn_bwd_call(*res, do, "gdn_" + tag + "_bwd")
    return dqkv, dz, dba, da, ddtb, jnp.sum(dgn_h, axis=0)


gdn.defvjp(_gdn_f, _gdn_b)


ADAMW_BLOCK = 256 * 1024


def adamw(w, m, v, *, parts, name):
    n_layers = len(parts)
    n_parts, r, c = parts[0].shape
    assert w.shape == (n_layers * r, c), (w.shape, parts[0].shape)
    tr = r
    for cand in (512, 256, 128, 64, 32, 16, 8):
        if r % cand == 0 and cand * c <= ADAMW_BLOCK:
            tr = cand
            break
    nb = r // tr
    blk = pl.BlockSpec((tr, c), lambda l, i: (l * nb + i, 0))
    bc1 = 1.0 - ADAM_B1 ** ADAM_STEP
    bc2 = 1.0 - ADAM_B2 ** ADAM_STEP

    def part_spec(li):
        return pl.BlockSpec((n_parts, tr, c),
                            lambda l, i: (0, jnp.where(l == li, i, jnp.where(l < li, 0, nb - 1)), 0))

    def body(*refs):
        w_ref, p_refs = refs[0], refs[1:1 + n_layers]
        m_ref, v_ref, g_ref, d_ref, mo_ref, vo_ref = refs[1 + n_layers:]
        for li in range(n_layers):
            @pl.when(pl.program_id(0) == li)
            def _(p_ref=p_refs[li]):
                g = p_ref[0].astype(F32)
                for i in range(1, n_parts):
                    g = g + p_ref[i].astype(F32)
                m2 = ADAM_B1 * m_ref[...] + (1.0 - ADAM_B1) * g
                v2 = ADAM_B2 * v_ref[...] + (1.0 - ADAM_B2) * (g * g)
                g_ref[...] = g
                mo_ref[...] = m2
                vo_ref[...] = v2
                d_ref[...] = -ADAM_LR * ((m2 / bc1) / (jnp.sqrt(v2 / bc2) + ADAM_EPS)
                                         + ADAM_WD * w_ref[...])

    return pl.pallas_call(
        body, name=name, grid=(n_layers, nb),
        in_specs=[blk] + [part_spec(li) for li in range(n_layers)] + [blk, blk],
        out_specs=[blk] * 4, out_shape=[jax.ShapeDtypeStruct(w.shape, F32)] * 4,
        compiler_params=_cp(("arbitrary", "arbitrary")),
    )(w, *parts, m, v)


_HBM = pl.BlockSpec(memory_space=pltpu.HBM)
_SEM = pl.BlockSpec(memory_space=pltpu.SEMAPHORE)
_EFFECT = pltpu.SideEffectType.DATAFLOW_SIDE_EFFECTING


def _peer(x, y, c, d):
    px = 1 - x if d & 4 else x
    py = 1 - y if d & 2 else y
    pc = 1 - c if d & 1 else c
    return (px, py, pc), 4 * px + 2 * py + pc


ALL_PEERS = (1, 2, 3, 4, 5, 6, 7)
SIBLING = 1
SAME_CORE_REMOTE = (2, 4, 6)


def copy_start(arrays, mode, carry, name):
    n = len(arrays)
    if mode == "forward":
        lands = []
    else:
        lands = [lax.empty(a.shape if mode == "scatter" else (N_DEV,) + a.shape, a.dtype) for a in arrays]
    n_in = n + len(lands) + 1

    def body(*refs):
        srcs = refs[:n]
        dsts = refs[n:2 * n] if lands else srcs
        sems = refs[n_in:n_in + 2 * n]
        x, y, c = (lax.axis_index(a) for a in AXES)
        me = 4 * x + 2 * y + c
        for k in range(n):
            if mode == "forward":
                sibling, _ = _peer(x, y, c, SIBLING)
                copies = [(srcs[k].at[_peer(x, y, c, d)[1]], dsts[k].at[_peer(x, y, c, d)[1]], sibling)
                          for d in SAME_CORE_REMOTE]
            elif mode == "gather":
                copies = [(srcs[k], dsts[k].at[me], _peer(x, y, c, d)[0]) for d in (SIBLING,) + SAME_CORE_REMOTE]
            else:
                copies = [(srcs[k].at[_peer(x, y, c, d)[1]], dsts[k].at[me], _peer(x, y, c, d)[0])
                          for d in ALL_PEERS]
            for src, dst, peer in copies:
                pltpu.make_async_remote_copy(src_ref=src, dst_ref=dst, send_sem=sems[2 * k],
                                             recv_sem=sems[2 * k + 1], device_id=peer,
                                             device_id_type=pl.DeviceIdType.MESH).start()

    operands = list(arrays) + lands + [carry]
    outs = pl.pallas_call(
        body, name=name,
        out_shape=tuple([pltpu.SemaphoreType.DMA(())] * (2 * n)
                        + [pltpu.HBM(a.shape, a.dtype) for a in operands]),
        in_specs=[_HBM] * n_in,
        out_specs=tuple([_SEM] * (2 * n) + [_HBM] * n_in),
        input_output_aliases={i: 2 * n + i for i in range(n_in)},
        compiler_params=pltpu.CompilerParams(has_side_effects=_EFFECT),
    )(*[pltpu.with_memory_space_constraint(a, pltpu.HBM) for a in operands])
    sems, thru = outs[:2 * n], outs[2 * n:-1]
    handles = [(sems[2 * k], sems[2 * k + 1], thru[k] if lands else None, thru[n + k] if lands else thru[k])
               for k in range(n)]
    return outs[-1], handles


def copy_wait(handles, n_blocks, after, name):
    n = len(handles)
    sems = [s for h in handles for s in h[:2]]
    srcs = [h[2] for h in handles if h[2] is not None]
    lands = [h[3] for h in handles]
    ns = len(srcs)

    def body(*refs):
        dsts = refs[ns:ns + n]
        sem_refs = refs[ns + n:ns + 3 * n]
        x, y, c = (lax.axis_index(a) for a in AXES)
        for k in range(n):
            blocks = dsts[k].at[pl.ds(0, n_blocks)]
            pltpu.make_async_remote_copy(
                src_ref=blocks, dst_ref=blocks, send_sem=sem_refs[2 * k], recv_sem=sem_refs[2 * k + 1],
                device_id=(x, y, c), device_id_type=pl.DeviceIdType.MESH).wait()

    outs = pl.pallas_call(
        body, name=name,
        out_shape=tuple([pltpu.HBM(a.shape, a.dtype) for a in srcs + lands]),
        in_specs=[_HBM] * (ns + n) + [_SEM] * (2 * n) + [pl.BlockSpec(memory_space=pl.ANY)],
        out_specs=tuple([_HBM] * (ns + n)),
        input_output_aliases={i: i for i in range(ns + n)},
        compiler_params=pltpu.CompilerParams(has_side_effects=_EFFECT),
    )(*srcs, *lands, *sems, after)
    return (list(outs[:ns]) if ns else [None] * n), list(outs[ns:])


def exchange(arrays, modes, name):
    n = len(arrays)
    hbm = pl.BlockSpec(memory_space=pltpu.HBM)
    out_shape = [jax.ShapeDtypeStruct(a.shape if md == "scatter" else (N_DEV,) + a.shape, a.dtype)
                 for a, md in zip(arrays, modes)]

    def body(*refs):
        ins, outs = refs[:n], refs[n:2 * n]
        send_sems, recv_sems, local_sems = refs[2 * n:]
        x, y, c = (lax.axis_index(a) for a in AXES)
        me = 4 * x + 2 * y + c

        def src(k, p):
            return ins[k].at[p] if modes[k] == "scatter" else ins[k]

        local = [pltpu.make_async_copy(src(k, me), outs[k].at[me], local_sems.at[k]) for k in range(n)]
        for cp in local:
            cp.start()
        started = []
        for d in range(1, N_DEV):
            px = 1 - x if d & 4 else x
            py = 1 - y if d & 2 else y
            pc = 1 - c if d & 1 else c
            pid = 4 * px + 2 * py + pc
            for k in range(n):
                pltpu.make_async_remote_copy(
                    src_ref=src(k, pid), dst_ref=outs[k].at[me],
                    send_sem=send_sems.at[k, d - 1], recv_sem=recv_sems.at[k, d - 1],
                    device_id=(px, py, pc), device_id_type=pl.DeviceIdType.MESH).start()
                started.append((k, d, pid, (px, py, pc)))
        for k, d, pid, peer in started:
            pltpu.make_async_remote_copy(
                src_ref=src(k, pid), dst_ref=outs[k].at[pid],
                send_sem=send_sems.at[k, d - 1], recv_sem=recv_sems.at[k, d - 1],
                device_id=peer, device_id_type=pl.DeviceIdType.MESH).wait()
        for cp in local:
            cp.wait()

    outs = pl.pallas_call(
        body, name=name, in_specs=[hbm] * n, out_specs=[hbm] * n, out_shape=out_shape,
        scratch_shapes=[pltpu.SemaphoreType.DMA((n, N_DEV - 1)), pltpu.SemaphoreType.DMA((n, N_DEV - 1)),
                        pltpu.SemaphoreType.DMA((n,))],
        compiler_params=pltpu.CompilerParams(has_side_effects=True),
    )(*arrays)
    return list(outs)


BIG = ("w_in", "w_uq", "w_ukv", "w_o_mla", "w_o_gdn", "w_o", "w_gate_up", "w_down")
ROW_SHARDED = ("w_o", "w_down")
SMALL = ("b_ada", "norm_mix", "norm_ffn", "q_a_norm", "kv_a_norm", "A_log", "dt_bias", "gdn_norm",
         "final_norm")
WEIGHTS = ("w_ada", "b_ada", "norm_mix", "norm_ffn", "w_in", "q_a_norm", "kv_a_norm", "w_uq", "w_ukv",
           "w_o_mla", "conv_w", "A_log", "dt_bias", "gdn_norm", "w_o_gdn", "w_o", "w_gate_up", "w_down",
           "final_norm")


def _unslot(g):
    return g.transpose(1, 0, 2).reshape(g.shape[1], -1)


def _cols(g):
    return g if g.shape[-1] % LANE == 0 else _unslot(g)


def _stack_rows(g):
    return g.reshape(-1, g.shape[-1])


def _rope(xv, cos, sin):
    x1, x2 = jnp.split(xv, 2, axis=-1)
    return jnp.concatenate([x1 * cos - x2 * sin, x2 * cos + x1 * sin], axis=-1)


MIX_WEIGHTS = ("w_uq", "w_ukv", "w_o_mla", "w_o_gdn", "w_o")
FFN_WEIGHTS = ("w_gate_up", "w_down")


def _pad_cols(a):
    return jnp.pad(a, ((0, 0), (0, _pad_lanes(a.shape[1]) - a.shape[1])))


def _stage_in(x, mod, nm, w_in_s, tg):
    d = x.shape[1]
    hg = GDN_HEADS
    w_in = _unslot(w_in_s)
    o1 = Q_LORA + KV_LORA + QK_ROPE
    o2 = o1 + 2 * hg * GDN_DK + hg * GDN_DV
    o3 = o2 + hg * GDN_DV
    o4 = o3 + 2 * hg
    h = ada_norm(x, nm, mod[:, d:2 * d], mod[:, :d], "mix" + tg)
    return (mm(h, _pad_cols(w_in[:, :o1]), "in_a" + tg, BF16), mm(h, w_in[:, o1:o2], "in_qkv" + tg, BF16),
            mm(h, w_in[:, o2:o3], "in_z" + tg, BF16), mm(h, _pad_cols(w_in[:, o3:o4]), "in_ba" + tg, F32),
            mm(h, w_in[:, o4:o4 + 2 * d], "in_g" + tg, BF16))


def _stage_mix(x, mod, seg_a, qkv, z, ba, gl, w_uq_s, w_ukv_s, w_o_mla_s, w_o_gdn_s, w_o_s, conv_s,
               qan, kvan, a_log, dtb, gn, cos, sin, tg):
    t, d = x.shape
    hq, hg = MLA_HEADS, GDN_HEADS
    w_uq = _unslot(w_uq_s).reshape(Q_LORA, hq, QK_NOPE + QK_ROPE)
    w_uq = jnp.concatenate([w_uq[:, :, :QK_NOPE].reshape(Q_LORA, hq * QK_NOPE),
                            w_uq[:, :, QK_NOPE:].reshape(Q_LORA, hq * QK_ROPE)], axis=1)
    c_q = seg_a[:, :Q_LORA]
    c_kv = seg_a[:, Q_LORA:Q_LORA + KV_LORA]
    k_pe = seg_a[:, Q_LORA + KV_LORA:Q_LORA + KV_LORA + QK_ROPE]
    qf = mm(rms_norm(c_q, qan, "qa" + tg), w_uq, "uq" + tg, BF16)
    kvf = mm(rms_norm(c_kv, kvan, "kva" + tg), _cols(w_ukv_s), "ukv" + tg, BF16)
    qn = qf[:, :hq * QK_NOPE]
    q_pe = qf[:, hq * QK_NOPE:].astype(F32).reshape(t, hq, QK_ROPE)
    qr = _rope(q_pe, cos[:, None, :], sin[:, None, :]).transpose(1, 0, 2).astype(BF16)
    kr = _rope(k_pe.astype(F32), cos, sin).astype(BF16)
    y_a = mm(attention(qn, qr, kvf, kr, tg), _cols(w_o_mla_s), "o_mla" + tg, BF16)
    conv_w = conv_s.transpose(1, 0, 2).reshape(CONV_WIDTH, -1)
    qkv_c = conv_silu(qkv, conv_w, tg)
    o_gdn = gdn(qkv_c, z, ba, a_log.reshape(hg, 1, 1), dtb.reshape(hg, 1, 1), gn, tg)
    y_b = mm(o_gdn, _cols(w_o_gdn_s), "o_gdn" + tg, BF16)
    return mm_resid(x, mod[:, 2 * d:3 * d], gate_mix(gl, y_a, y_b, tg), _stack_rows(w_o_s), "w_o" + tg)


def _stage_ffn(x, mod, nf, w_gu_s, w_down_s, tg):
    d = x.shape[1]
    h = ada_norm(x, nf, mod[:, 4 * d:5 * d], mod[:, 3 * d:4 * d], "ffn" + tg)
    gu = mm(h, _cols(w_gu_s), "gu" + tg, BF16)
    return mm_resid(x, mod[:, 5 * d:6 * d], swiglu(gu, tg), _stack_rows(w_down_s), "down" + tg)


def _flat_row(arrs):
    v = jnp.concatenate([a.reshape(-1) for a in arrs])
    return jnp.pad(v, (0, _pad_lanes(v.shape[0]) - v.shape[0]))[None, :]


def kernel(x, c, positions, w_ada, b_ada, norm_mix, norm_ffn, w_in, q_a_norm, kv_a_norm, w_uq, w_ukv, w_o_mla, conv_w, A_log, dt_bias, gdn_norm, w_o_gdn, w_o, w_gate_up, w_down, final_norm, loss_target, m_w_ada, m_b_ada, m_norm_mix, m_norm_ffn, m_w_in, m_q_a_norm, m_kv_a_norm, m_w_uq, m_w_ukv, m_w_o_mla, m_conv_w, m_A_log, m_dt_bias, m_gdn_norm, m_w_o_gdn, m_w_o, m_w_gate_up, m_w_down, m_final_norm, v_w_ada, v_b_ada, v_norm_mix, v_norm_ffn, v_w_in, v_q_a_norm, v_kv_a_norm, v_w_uq, v_w_ukv, v_w_o_mla, v_conv_w, v_A_log, v_dt_bias, v_gdn_norm, v_w_o_gdn, v_w_o, v_w_gate_up, v_w_down, v_final_norm):
    given = dict(locals())
    t, d = x.shape[1], x.shape[2]
    n_ada = w_ada.shape[2]
    me = 4 * lax.axis_index("x") + 2 * lax.axis_index("y") + lax.axis_index("c")

    def with_own(land, own):
        return lax.dynamic_update_slice(land, own[None], (me,) + (0,) * own.ndim)

    got = exchange([c, conv_w], ["gather", "gather"], "gather_small")
    c_all, conv_g = got[0].reshape(N_DEV, d), got[1]
    c_rows = jnp.pad(c_all, ((0, 16 - N_DEV), (0, 0)))
    mod_cols = jnp.stack([_mm(c_rows, w_ada[l], "nn", F32, "ada_mod%d" % l, a_act="silu")[:N_DEV]
                          for l in range(DEPTH)], axis=1)
    mod_mine = exchange([mod_cols], ["scatter"], "scatter_mod")[0]
    mods = mod_mine.transpose(1, 0, 2).reshape(DEPTH, N_DEV * n_ada) + b_ada

    groups = [[(n, l) for n in names] for l in range(DEPTH) for names in (("w_in",), MIX_WEIGHTS, FFN_WEIGHTS)]
    gtags = [s + str(l) for l in range(DEPTH) for s in ("in", "mix", "ffn")]
    keys = [k for ks in groups for k in ks]
    mods, handles = copy_start([given[n][l].astype(BF16) for n, l in keys], "gather", mods, "gather_start")
    handles = dict(zip(keys, handles))
    own, relayed = {}, {}

    def relay(gi, carry):
        ks = groups[gi]
        srcs, lands = copy_wait([handles[k] for k in ks], 1 + len(SAME_CORE_REMOTE), carry,
                                "wait_ici_" + gtags[gi])
        own.update(zip(ks, srcs))
        carry, hs = copy_start(lands, "forward", carry, "relay_" + gtags[gi])
        relayed.update(zip(ks, hs))
        return carry

    def landed(gi, after):
        ks = groups[gi]
        _, lands = copy_wait([relayed[k] for k in ks], len(SAME_CORE_REMOTE), after, "wait_" + gtags[gi])
        return [with_own(land, own[k]) for k, land in zip(ks, lands)]

    inv_freq = 1.0 / (ROPE_THETA ** (jnp.arange(0, QK_ROPE, 2, dtype=F32) / QK_ROPE))
    ang = positions[0].astype(F32)[:, None] * inv_freq
    cos, sin = jnp.cos(ang), jnp.sin(ang)
    relay_before = {0: [0], 1: [1], 2: [2, 3], 3: [4], 4: [5], 5: []}

    def weights_for(stage, carry):
        for gi in relay_before[stage]:
            carry = relay(gi, carry)
        return carry, landed(stage, carry)

    xl = x[0]
    vjps = []
    for l in range(DEPTH):
        tg = str(l)
        mod = mods[l:l + 1]
        xl, (w_in_s,) = weights_for(3 * l, xl)
        seg, vjp_in = jax.vjp(lambda *a, tg=tg: _stage_in(*a, tg), xl, mod, norm_mix[l:l + 1], w_in_s)
        seg0, w_mix = weights_for(3 * l + 1, seg[0])
        seg = (seg0,) + tuple(seg[1:])
        xm, vjp_mix = jax.vjp(lambda *a, tg=tg: _stage_mix(*a, cos, sin, tg), xl, mod, *seg, *w_mix,
                              conv_g[:, l], q_a_norm[l:l + 1], kv_a_norm[l:l + 1], A_log[l], dt_bias[l],
                              gdn_norm[l:l + 1])
        xm, w_ffn = weights_for(3 * l + 2, xm)
        xl, vjp_ffn = jax.vjp(lambda *a, tg=tg: _stage_ffn(*a, tg), xm, mod, norm_ffn[l:l + 1], *w_ffn)
        vjps.append((vjp_in, vjp_mix, vjp_ffn))

    loss_t, g, dfn = loss_head(xl, final_norm[None, :], loss_target[0])
    loss = lax.psum(loss_t[0, 0], AXES)
    dsmall = {n: [None] * DEPTH for n in SMALL + ("conv_w",)}
    dmods = [None] * DEPTH
    sent = {}

    def send(ks, grads, carry, name):
        carry, hs = copy_start(list(grads), "scatter", carry, name)
        sent.update(zip(ks, hs))
        return carry

    for l in reversed(range(DEPTH)):
        tg = str(l)
        vjp_in, vjp_mix, vjp_ffn = vjps[l]
        dxm, dmod_f, dsmall["norm_ffn"][l], *dw = vjp_ffn(g)
        dxm = send([(n, l) for n in FFN_WEIGHTS], dw, dxm, "scatter_ffn" + tg)
        dx_m, dmod_m, *rest = vjp_mix(dxm)
        dseg, dw, rest = rest[:5], rest[5:5 + len(MIX_WEIGHTS)], rest[5 + len(MIX_WEIGHTS):]
        dseg[0] = send([(n, l) for n in MIX_WEIGHTS], dw, dseg[0], "scatter_mix" + tg)
        for n, gr in zip(("conv_w", "q_a_norm", "kv_a_norm", "A_log", "dt_bias", "gdn_norm"), rest):
            dsmall[n][l] = gr
        dx_i, dmod_i, dsmall["norm_mix"][l], dw_in = vjp_in(tuple(dseg))
        g = dx_i + dx_m
        if l > 0:
            g = send([("w_in", l)], [dw_in], g, "scatter_in" + tg)
        dmods[l] = dmod_f + dmod_m + dmod_i
    dx = g
    dmods = jnp.concatenate(dmods, axis=0)
    dconv = jnp.stack(dsmall.pop("conv_w"), axis=1)
    dsmall = {n: jnp.concatenate(v, axis=0) if v[0].ndim == 2 else jnp.stack(v)
              for n, v in dsmall.items() if v[0] is not None}
    dsmall["b_ada"] = dmods
    dsmall["final_norm"] = dfn[0]

    dmod_cols = dmods.reshape(DEPTH, N_DEV, n_ada).transpose(1, 0, 2)
    conv_parts, dmod_all, small_parts = exchange(
        [dconv, dmod_cols, _flat_row([dsmall[n] for n in SMALL])], ["scatter", "scatter", "gather"],
        "exchange_small")
    dmod_all = send([("w_in", 0)], [dw_in], dmod_all, "scatter_in0")

    res = {}
    dm_rows = jnp.pad(dmod_all, ((0, 16 - N_DEV), (0, 0), (0, 0)))
    g_ada = [_mm(c_rows, dm_rows[:, l], "tn", F32, "ada_dw%d" % l, a_act="silu")[None] for l in range(DEPTH)]
    r2 = (DEPTH * d, n_ada)
    outs = adamw(w_ada.reshape(r2), m_w_ada.reshape(r2), v_w_ada.reshape(r2), parts=g_ada, name="adamw_w_ada")
    res["w_ada"] = [o.reshape(w_ada.shape) for o in outs]
    packed = SMALL + ("conv_w",)
    p_all = jnp.concatenate([small_parts, conv_parts.reshape(N_DEV, 1, -1)], axis=2)
    pack = lambda pre: jnp.concatenate([_flat_row([given[pre + n] for n in SMALL]),
                                        given[pre + "conv_w"].reshape(1, -1)], axis=1)
    outs = adamw(pack(""), pack("m_"), pack("v_"), parts=[p_all], name="adamw_small")
    done = [res["w_ada"][1], outs[1]]
    for group, gname in ((FFN_WEIGHTS, "ffn"), (MIX_WEIGHTS, "mix"), (("w_in",), "in")):
        ks = [(n, l) for l in reversed(range(DEPTH)) for n in group]
        after = sum(lax.slice(a, (0,) * a.ndim, (1,) * a.ndim).reshape(1, 1) for a in done)
        srcs, lands = copy_wait([sent[k] for k in ks], len(ALL_PEERS), after, "scatter_wait_" + gname)
        parts = {k: with_own(land, lax.dynamic_index_in_dim(src, me, 0, keepdims=False))
                 for k, src, land in zip(ks, srcs, lands)}
        for n in group:
            w = given[n]
            r2 = (w.shape[0] * w.shape[1], w.shape[2])
            res[n] = [o.reshape(w.shape) for o in
                      adamw(w.reshape(r2), given["m_" + n].reshape(r2), given["v_" + n].reshape(r2),
                            parts=[parts[(n, l)] for l in range(DEPTH)], name="adamw_" + n)]
            done.append(res[n][1])
    off = 0
    for n in packed:
        if n == "conv_w":
            off = small_parts.shape[2]
        size = math.prod(given[n].shape)
        res[n] = [o[0, off:off + size].reshape(given[n].shape) for o in outs]
        off += size

    return (loss, dx[None]) + tuple(res[n][i] for i in range(4) for n in WEIGHTS)
```

```python
import functools
import math

import jax
import jax.numpy as jnp
from jax import lax
from jax.experimental import pallas as pl
from jax.experimental.pallas import tpu as pltpu

F32 = jnp.float32
BF16 = jnp.bfloat16

MLA_HEADS = 8
QK_NOPE = 128
QK_ROPE = 64
V_HEAD = 128
Q_LORA = 512
KV_LORA = 512
ROPE_THETA = 10000.0
GDN_HEADS = 8
GDN_DK = 128
GDN_DV = 128
CONV_WIDTH = 4
CHUNK = 64
DEPTH = 2
EPS = 1e-6
ADAM_LR = 0.001
ADAM_B1 = 0.9
ADAM_B2 = 0.999
ADAM_EPS = 1e-08
ADAM_WD = 0.01
ADAM_STEP = 10

N_DEV = 8
AXES = ("x", "y", "c")
LANE = 128
VMEM_LIMIT = 48 * 1024 * 1024
MM_VMEM_BUDGET = 36 * 1024 * 1024

NN = (((1,), (0,)), ((), ()))
NT = (((1,), (1,)), ((), ()))
TN = (((0,), (0,)), ((), ()))


def _cp(sem=None):
    return pltpu.CompilerParams(dimension_semantics=sem, vmem_limit_bytes=VMEM_LIMIT)


def _tile(n, cap):
    if n <= cap:
        return n
    for t in range(cap - cap % LANE, 0, -LANE):
        if n % t == 0:
            return t
    return n


def _rows(t, cap=256):
    return cap if t % cap == 0 else t


def _pad_lanes(n):
    return -(-n // LANE) * LANE


def _sigmoid(x):
    return 1.0 / (1.0 + jnp.exp(-x))


def _softplus(x):
    return jnp.maximum(x, 0.0) + jnp.log(1.0 + jnp.exp(-jnp.abs(x)))


def _tile_slot(n, cap):
    t = _tile(n, cap)
    return n if t < 256 < n <= 1536 else t


NARROW_SLOT = 512


def _mm_narrow_slots(a, b, dims, out_dtype, name):
    bf = lambda v: v.astype(BF16)
    dot = lambda p, q, dn: lax.dot_general(p, q, dn, preferred_element_type=F32)
    if dims == "nn":
        (m, k), per = a.shape, b.shape[-1]
        spb = min(N_DEV, max(1, 1024 // per))
        tm = _tile(m, 1024)

        def body(a_ref, b_ref, o_ref):
            av = bf(a_ref[...])
            for s in range(spb):
                o_ref[:, s * per:(s + 1) * per] = dot(av, bf(b_ref[s]), NN).astype(o_ref.dtype)

        grid = (m // tm, N_DEV // spb)
        in_specs = [pl.BlockSpec((tm, k), lambda i, j: (i, 0)), pl.BlockSpec((spb, k, per), lambda i, j: (j, 0, 0))]
        out_specs = pl.BlockSpec((tm, spb * per), lambda i, j: (i, j))
        out_shape = jax.ShapeDtypeStruct((m, N_DEV * per), out_dtype)
    elif dims == "nt":
        m, (n, per) = a.shape[0], b.shape[1:]
        tm, tn = _tile(m, 1024), _tile(n, 1024)

        def body(a_ref, b_ref, o_ref):
            av = bf(a_ref[...])
            acc = dot(av[:, :per], bf(b_ref[0]), NT)
            for s in range(1, N_DEV):
                acc += dot(av[:, s * per:(s + 1) * per], bf(b_ref[s]), NT)
            o_ref[...] = acc.astype(o_ref.dtype)

        grid = (m // tm, n // tn)
        in_specs = [pl.BlockSpec((tm, N_DEV * per), lambda i, j: (i, 0)),
                    pl.BlockSpec((N_DEV, tn, per), lambda i, j: (0, j, 0))]
        out_specs = pl.BlockSpec((tm, tn), lambda i, j: (i, j))
        out_shape = jax.ShapeDtypeStruct((m, n), out_dtype)
    else:
        (k, m), per = a.shape, b.shape[1] // N_DEV
        spb = min(N_DEV, max(1, 1024 // per))
        tm = _tile(m, 1024)

        def body(a_ref, b_ref, o_ref):
            av = bf(a_ref[...])
            for s in range(spb):
                o_ref[s] = dot(av, bf(b_ref[:, s * per:(s + 1) * per]), TN).astype(o_ref.dtype)

        grid = (m // tm, N_DEV // spb)
        in_specs = [pl.BlockSpec((k, tm), lambda i, j: (0, i)), pl.BlockSpec((k, spb * per), lambda i, j: (0, j))]
        out_specs = pl.BlockSpec((spb, tm, per), lambda i, j: (j, i, 0))
        out_shape = jax.ShapeDtypeStruct((N_DEV, m, per), out_dtype)
    return pl.pallas_call(body, name=name, grid=grid, in_specs=in_specs, out_specs=out_specs,
                          out_shape=out_shape, compiler_params=_cp(("parallel", "parallel")))(a, b)


def _mm(a, b, dims, out_dtype, name, a_act=None, slots=False, resid=None):
    if dims == "nn":
        m, k = a.shape
        n = b.shape[-1] * (N_DEV if slots else 1)
    elif dims == "nt":
        m, k = a.shape
        n = b.shape[-2]
    else:
        k, m = a.shape
        n = b.shape[-1]
    per = (k if dims == "nt" else n) // N_DEV
    if slots and per % LANE == 0 and per <= NARROW_SLOT and k <= 2048 and a_act is None and resid is None:
        return _mm_narrow_slots(a, b, dims, out_dtype, name)
    tm = _tile(m, 1536)
    tn = _tile_slot(n // N_DEV, 512) if slots and dims != "nt" else _tile(n, 512 if resid else 1024)
    k_slot = k // N_DEV if slots and dims == "nt" else k

    def vmem_bytes(tk_):
        a_b, b_b = tm * tk_ * a.dtype.itemsize, tk_ * tn * b.dtype.itemsize
        casts = (tm * tk_ * 2 if a.dtype != BF16 else 0) + (tk_ * tn * 2 if b.dtype != BF16 else 0)
        return 2 * (a_b + b_b + tm * tn * jnp.dtype(out_dtype).itemsize) + 2 * tm * tn * 4 + casts

    tk = _tile_slot(k_slot, 1536) if slots and dims == "nt" else _tile(k, 2048)
    while vmem_bytes(tk) > MM_VMEM_BUDGET and tk % (2 * LANE) == 0:
        tk //= 2
    nk = k // tk
    per_n = (n // N_DEV) // tn if slots else 1
    per_k = (k // N_DEV) // tk if slots else 1
    if dims == "tn":
        a_spec = pl.BlockSpec((tk, tm), lambda i, j, kk: (kk, i))
    else:
        a_spec = pl.BlockSpec((tm, tk), lambda i, j, kk: (i, kk))
    if dims == "nt":
        if slots:
            b_spec = pl.BlockSpec((None, tn, tk), lambda i, j, kk: (kk // per_k, j, kk % per_k))
        else:
            b_spec = pl.BlockSpec((tn, tk), lambda i, j, kk: (j, kk))
    elif dims == "nn" and slots:
        b_spec = pl.BlockSpec((None, tk, tn), lambda i, j, kk: (j // per_n, kk, j % per_n))
    else:
        b_spec = pl.BlockSpec((tk, tn), lambda i, j, kk: (kk, j))
    if dims == "tn" and slots:
        out_spec = pl.BlockSpec((None, tm, tn), lambda i, j, kk: (j // per_n, i, j % per_n))
        out_shape = jax.ShapeDtypeStruct((N_DEV, m, n // N_DEV), out_dtype)
    else:
        out_spec = pl.BlockSpec((tm, tn), lambda i, j, kk: (i, j))
        out_shape = jax.ShapeDtypeStruct((m, n), out_dtype)
    dn = {"nn": NN, "nt": NT, "tn": TN}[dims]

    def product(a_ref, b_ref):
        av = a_ref[...]
        if a_act == "silu":
            av = av * _sigmoid(av)
        return lax.dot_general(av.astype(BF16), b_ref[...].astype(BF16), dn, preferred_element_type=F32)

    def finish(acc, rest):
        if resid is None:
            (o_ref,) = rest
            o_ref[...] = acc.astype(o_ref.dtype)
        else:
            x_ref, gt_ref, o_ref, p_ref = rest
            o_ref[...] = x_ref[...] + gt_ref[...] * acc
            p_ref[...] = acc.astype(p_ref.dtype)

    def body_one(a_ref, b_ref, *rest):
        finish(product(a_ref, b_ref), rest)

    def body_acc(a_ref, b_ref, *rest):
        acc_ref = rest[-1]
        kk = pl.program_id(2)

        @pl.when(kk == 0)
        def _():
            acc_ref[...] = jnp.zeros_like(acc_ref)

        acc_ref[...] += product(a_ref, b_ref)

        @pl.when(kk == nk - 1)
        def _():
            finish(acc_ref[...], rest[:-1])

    in_specs, operands = [a_spec, b_spec], [a, b]
    if resid is not None:
        in_specs += [out_spec, pl.BlockSpec((1, tn), lambda i, j, kk: (0, j))]
        operands += list(resid)
        out_spec, out_shape = [out_spec, out_spec], [out_shape, jax.ShapeDtypeStruct((m, n), BF16)]
    return pl.pallas_call(
        body_one if nk == 1 else body_acc, name=name, grid=(m // tm, n // tn, nk),
        in_specs=in_specs, out_specs=out_spec, out_shape=out_shape,
        scratch_shapes=[] if nk == 1 else [pltpu.VMEM((tm, tn), F32)],
        compiler_params=_cp(("parallel", "parallel", "arbitrary")),
    )(*operands)


@functools.partial(jax.custom_vjp, nondiff_argnums=(2, 3))
def mm(a, b, tag, out_dtype):
    return _mm(a, b, "nn", out_dtype, "mm_" + tag, slots=b.ndim == 3)


def _mm_f(a, b, tag, out_dtype):
    return mm(a, b, tag, out_dtype), (a, b)


def _mm_b(tag, out_dtype, res, g):
    a, b = res
    slots = b.ndim == 3
    da = _mm(g, b, "nt", a.dtype, "mm_" + tag + "_da", slots=slots)
    db = _mm(a, g, "tn", b.dtype, "mm_" + tag + "_db", slots=slots)
    return da, db


mm.defvjp(_mm_f, _mm_b)


def _norm_fwd_call(x, nw, sc, sh, name):
    t, d = x.shape
    tr = _rows(t)
    mod = sc is not None
    row = pl.BlockSpec((tr, d), lambda i: (i, 0))
    vec = pl.BlockSpec((1, d), lambda i: (0, 0))

    def body(*refs):
        if mod:
            x_ref, nw_ref, sc_ref, sh_ref, o_ref = refs
        else:
            x_ref, nw_ref, o_ref = refs
        xv = x_ref[...].astype(F32)
        r = lax.rsqrt(jnp.mean(xv * xv, axis=-1, keepdims=True) + EPS)
        y = (xv * r) * nw_ref[...]
        if mod:
            y = y * (1.0 + sc_ref[...]) + sh_ref[...]
        o_ref[...] = y.astype(o_ref.dtype)

    args = (x, nw, sc, sh) if mod else (x, nw)
    return pl.pallas_call(
        body, name=name, grid=(t // tr,),
        in_specs=[row] + [vec] * (len(args) - 1), out_specs=row,
        out_shape=jax.ShapeDtypeStruct((t, d), BF16),
        compiler_params=_cp(("parallel",)),
    )(*args)


def _norm_bwd_call(x, nw, sc, dh, name):
    t, d = x.shape
    tr = _rows(t)
    mod = sc is not None
    row = pl.BlockSpec((tr, d), lambda i: (i, 0))
    vec = pl.BlockSpec((1, d), lambda i: (0, 0))

    def body(*refs):
        if mod:
            x_ref, nw_ref, sc_ref, dh_ref, dx_ref, dnw_ref, dsc_ref, dsh_ref = refs
        else:
            x_ref, nw_ref, dh_ref, dx_ref, dnw_ref = refs
        i = pl.program_id(0)
        xv = x_ref[...].astype(F32)
        dh = dh_ref[...].astype(F32)
        r = lax.rsqrt(jnp.mean(xv * xv, axis=-1, keepdims=True) + EPS)
        y = xv * r
        a = nw_ref[...] * (1.0 + sc_ref[...]) if mod else nw_ref[...]
        dy = dh * a
        dx_ref[...] = (r * (dy - y * jnp.mean(dy * y, axis=-1, keepdims=True))).astype(dx_ref.dtype)
        da = jnp.sum(dh * y, axis=0, keepdims=True)

        @pl.when(i == 0)
        def _():
            dnw_ref[...] = jnp.zeros_like(dnw_ref)
            if mod:
                dsc_ref[...] = jnp.zeros_like(dsc_ref)
                dsh_ref[...] = jnp.zeros_like(dsh_ref)

        if mod:
            dnw_ref[...] += da * (1.0 + sc_ref[...])
            dsc_ref[...] += da * nw_ref[...]
            dsh_ref[...] += jnp.sum(dh, axis=0, keepdims=True)
        else:
            dnw_ref[...] += da

    args = (x, nw, sc, dh) if mod else (x, nw, dh)
    n_vec = 3 if mod else 1
    return pl.pallas_call(
        body, name=name, grid=(t // tr,),
        in_specs=[row] + [vec] * (len(args) - 2) + [row],
        out_specs=[row] + [vec] * n_vec,
        out_shape=[jax.ShapeDtypeStruct((t, d), x.dtype)] + [jax.ShapeDtypeStruct((1, d), F32)] * n_vec,
        compiler_params=_cp(("arbitrary",)),
    )(*args)


@functools.partial(jax.custom_vjp, nondiff_argnums=(4,))
def ada_norm(x, nw, sc, sh, tag):
    return _norm_fwd_call(x, nw, sc, sh, "adanorm_" + tag)


def _ada_norm_f(x, nw, sc, sh, tag):
    return _norm_fwd_call(x, nw, sc, sh, "adanorm_" + tag), (x, nw, sc)


def _ada_norm_b(tag, res, dh):
    x, nw, sc = res
    dx, dnw, dsc, dsh = _norm_bwd_call(x, nw, sc, dh, "adanorm_" + tag + "_bwd")
    return dx, dnw, dsc, dsh


ada_norm.defvjp(_ada_norm_f, _ada_norm_b)


@functools.partial(jax.custom_vjp, nondiff_argnums=(2,))
def rms_norm(x, nw, tag):
    return _norm_fwd_call(x, nw, None, None, "rms_" + tag)


def _rms_norm_f(x, nw, tag):
    return _norm_fwd_call(x, nw, None, None, "rms_" + tag), (x, nw)


def _rms_norm_b(tag, res, dh):
    x, nw = res
    dx, dnw = _norm_bwd_call(x, nw, None, dh, "rms_" + tag + "_bwd")
    return dx, dnw


rms_norm.defvjp(_rms_norm_f, _rms_norm_b)


def _gate_mix_fwd_call(gl, ya, yb, name):
    t, d = ya.shape
    tr = _rows(t)
    row = pl.BlockSpec((tr, d), lambda i: (i, 0))

    def body(ga_ref, gb_ref, ya_ref, yb_ref, o_ref):
        o_ref[...] = (_sigmoid(ga_ref[...].astype(F32)) * ya_ref[...].astype(F32)
                      + _sigmoid(gb_ref[...].astype(F32)) * yb_ref[...].astype(F32)).astype(o_ref.dtype)

    return pl.pallas_call(
        body, name=name, grid=(t // tr,),
        in_specs=[row, pl.BlockSpec((tr, d), lambda i: (i, 1)), row, row], out_specs=row,
        out_shape=jax.ShapeDtypeStruct((t, d), BF16),
        compiler_params=_cp(("parallel",)),
    )(gl, gl, ya, yb)


def _gate_mix_bwd_call(gl, ya, yb, dm, name):
    t, d = ya.shape
    tr = _rows(t)
    row = pl.BlockSpec((tr, d), lambda i: (i, 0))
    wide = pl.BlockSpec((tr, 2 * d), lambda i: (i, 0))

    def body(gl_ref, ya_ref, yb_ref, dm_ref, dgl_ref, dya_ref, dyb_ref):
        dm = dm_ref[...].astype(F32)
        ga = _sigmoid(gl_ref[:, :d].astype(F32))
        gb = _sigmoid(gl_ref[:, d:].astype(F32))
        dya_ref[...] = (dm * ga).astype(dya_ref.dtype)
        dyb_ref[...] = (dm * gb).astype(dyb_ref.dtype)
        dgl_ref[:, :d] = (dm * ya_ref[...].astype(F32) * ga * (1.0 - ga)).astype(dgl_ref.dtype)
        dgl_ref[:, d:] = (dm * yb_ref[...].astype(F32) * gb * (1.0 - gb)).astype(dgl_ref.dtype)

    return pl.pallas_call(
        body, name=name, grid=(t // tr,),
        in_specs=[wide, row, row, row], out_specs=[wide, row, row],
        out_shape=[jax.ShapeDtypeStruct((t, 2 * d), gl.dtype), jax.ShapeDtypeStruct((t, d), ya.dtype),
                   jax.ShapeDtypeStruct((t, d), yb.dtype)],
        compiler_params=_cp(("parallel",)),
    )(gl, ya, yb, dm)


@functools.partial(jax.custom_vjp, nondiff_argnums=(3,))
def gate_mix(gl, ya, yb, tag):
    return _gate_mix_fwd_call(gl, ya, yb, "gatemix_" + tag)


def _gate_mix_f(gl, ya, yb, tag):
    return _gate_mix_fwd_call(gl, ya, yb, "gatemix_" + tag), (gl, ya, yb)


def _gate_mix_b(tag, res, dm):
    return tuple(_gate_mix_bwd_call(*res, dm, "gatemix_" + tag + "_bwd"))


gate_mix.defvjp(_gate_mix_f, _gate_mix_b)


def _resid_bwd_call(gt, p, g, name):
    t, d = p.shape
    tr = _rows(t)
    row = pl.BlockSpec((tr, d), lambda i: (i, 0))
    vec = pl.BlockSpec((1, d), lambda i: (0, 0))

    def body(gt_ref, p_ref, g_ref, dp_ref, dgt_ref):
        i = pl.program_id(0)
        g = g_ref[...]
        dp_ref[...] = (g * gt_ref[...]).astype(dp_ref.dtype)

        @pl.when(i == 0)
        def _():
            dgt_ref[...] = jnp.zeros_like(dgt_ref)

        dgt_ref[...] += jnp.sum(g * p_ref[...].astype(F32), axis=0, keepdims=True)

    return pl.pallas_call(
        body, name=name, grid=(t // tr,), in_specs=[vec, row, row], out_specs=[row, vec],
        out_shape=[jax.ShapeDtypeStruct((t, d), BF16), jax.ShapeDtypeStruct((1, d), F32)],
        compiler_params=_cp(("arbitrary",)),
    )(gt, p, g)


@functools.partial(jax.custom_vjp, nondiff_argnums=(4,))
def mm_resid(x, gt, a, b, tag):
    return _mm(a, b, "nn", F32, "mmres_" + tag, resid=(x, gt))[0]


def _mm_resid_f(x, gt, a, b, tag):
    o, p = _mm(a, b, "nn", F32, "mmres_" + tag, resid=(x, gt))
    return o, (gt, a, b, p)


def _mm_resid_b(tag, res, g):
    gt, a, b, p = res
    dp, dgt = _resid_bwd_call(gt, p, g, "mmres_" + tag + "_gate")
    da = _mm(dp, b, "nt", a.dtype, "mmres_" + tag + "_da")
    db = _mm(a, dp, "tn", b.dtype, "mmres_" + tag + "_db")
    return g, dgt, da, db


mm_resid.defvjp(_mm_resid_f, _mm_resid_b)


def _swiglu_fwd_call(gu, name):
    t, f2 = gu.shape
    f = f2 // 2
    tr = _rows(t, 128)
    half = pl.BlockSpec((tr, f), lambda i: (i, 0))

    def body(g_ref, u_ref, o_ref):
        g = g_ref[...].astype(F32)
        o_ref[...] = (g * _sigmoid(g) * u_ref[...].astype(F32)).astype(o_ref.dtype)

    return pl.pallas_call(
        body, name=name, grid=(t // tr,),
        in_specs=[half, pl.BlockSpec((tr, f), lambda i: (i, 1))], out_specs=half,
        out_shape=jax.ShapeDtypeStruct((t, f), BF16), compiler_params=_cp(("parallel",)),
    )(gu, gu)


def _swiglu_bwd_call(gu, da, name):
    t, f2 = gu.shape
    f = f2 // 2
    tr = _rows(t, 128)
    wide = pl.BlockSpec((tr, f2), lambda i: (i, 0))

    def body(gu_ref, da_ref, dgu_ref):
        g = gu_ref[:, :f].astype(F32)
        u = gu_ref[:, f:].astype(F32)
        da = da_ref[...].astype(F32)
        s = _sigmoid(g)
        ds = da * s
        dgu_ref[:, :f] = (ds * u * (1.0 + g * (1.0 - s))).astype(dgu_ref.dtype)
        dgu_ref[:, f:] = (ds * g).astype(dgu_ref.dtype)

    return pl.pallas_call(
        body, name=name, grid=(t // tr,),
        in_specs=[wide, pl.BlockSpec((tr, f), lambda i: (i, 0))], out_specs=wide,
        out_shape=jax.ShapeDtypeStruct((t, f2), gu.dtype), compiler_params=_cp(("parallel",)),
    )(gu, da)


@functools.partial(jax.custom_vjp, nondiff_argnums=(1,))
def swiglu(gu, tag):
    return _swiglu_fwd_call(gu, "swiglu_" + tag)


def _swiglu_f(gu, tag):
    return _swiglu_fwd_call(gu, "swiglu_" + tag), (gu,)


def _swiglu_b(tag, res, da):
    return (_swiglu_bwd_call(res[0], da, "swiglu_" + tag + "_bwd"),)


swiglu.defvjp(_swiglu_f, _swiglu_b)


def loss_head(x, fw, tgt):
    t, d = x.shape
    tr = _rows(t)
    row = pl.BlockSpec((tr, d), lambda i: (i, 0))
    vec = pl.BlockSpec((1, d), lambda i: (0, 0))
    tile = pl.BlockSpec((8, LANE), lambda i: (0, 0))

    def body(x_ref, fw_ref, tgt_ref, loss_ref, dx_ref, dfw_ref):
        i = pl.program_id(0)
        xv = x_ref[...]
        fw = fw_ref[...]
        r = lax.rsqrt(jnp.mean(xv * xv, axis=-1, keepdims=True) + EPS)
        yh = xv * r
        e = yh * fw - tgt_ref[...]
        dy = e * (1.0 / d)
        dyw = dy * fw
        dx_ref[...] = r * (dyw - yh * jnp.mean(dyw * yh, axis=-1, keepdims=True))

        @pl.when(i == 0)
        def _():
            loss_ref[...] = jnp.zeros_like(loss_ref)
            dfw_ref[...] = jnp.zeros_like(dfw_ref)

        loss_ref[...] += 0.5 * jnp.sum(jnp.mean(e * e, axis=-1, keepdims=True))
        dfw_ref[...] += jnp.sum(dy * yh, axis=0, keepdims=True)

    return pl.pallas_call(
        body, name="loss_head", grid=(t // tr,), in_specs=[row, vec, row],
        out_specs=[tile, row, vec],
        out_shape=[jax.ShapeDtypeStruct((8, LANE), F32), jax.ShapeDtypeStruct((t, d), F32),
                   jax.ShapeDtypeStruct((1, d), F32)],
        compiler_params=_cp(("arbitrary",)),
    )(x, fw, tgt)


def _attn_scores(qn_ref, qr_ref, kn_ref, kr_ref, diag):
    tq = qn_ref.shape[0]
    s = lax.dot_general(qn_ref[...].astype(BF16), kn_ref[...].astype(BF16), NT, preferred_element_type=F32)
    s += lax.dot_general(qr_ref[...].astype(BF16), kr_ref[...].astype(BF16), NT, preferred_element_type=F32)
    s = s * (QK_NOPE + QK_ROPE) ** -0.5
    if diag:
        rows = lax.broadcasted_iota(jnp.int32, (tq, tq), 0)
        cols = lax.broadcasted_iota(jnp.int32, (tq, tq), 1)
        s = jnp.where(cols <= rows, s, -1e30)
    return s


def _attn_fwd_call(qn, qr, kv, kr, name):
    t = qn.shape[0]
    h_n = MLA_HEADS
    tq = _rows(t, 512)
    nq = t // tq
    assert V_HEAD == LANE and tq % LANE == 0

    def body(qn_ref, qr_ref, kn_ref, v_ref, kr_ref, o_ref, lse_ref, m_scr, l_scr, acc_scr):
        i, j = pl.program_id(1), pl.program_id(2)

        @pl.when(j == 0)
        def _():
            m_scr[...] = jnp.full_like(m_scr, -1e30)
            l_scr[...] = jnp.zeros_like(l_scr)
            acc_scr[...] = jnp.zeros_like(acc_scr)

        def step(diag):
            s = _attn_scores(qn_ref, qr_ref, kn_ref, kr_ref, diag)
            m_old = m_scr[...]
            m_new = jnp.maximum(m_old, jnp.max(s, axis=-1, keepdims=True))
            p = jnp.exp(s - jnp.tile(m_new, (1, tq // LANE)))
            alpha = jnp.exp(m_old - m_new)
            l_scr[...] = alpha * l_scr[...] + jnp.sum(p, axis=-1, keepdims=True)
            acc_scr[...] = alpha * acc_scr[...] + jnp.dot(p.astype(BF16), v_ref[...].astype(BF16),
                                                           preferred_element_type=F32)
            m_scr[...] = m_new

        @pl.when(j < i)
        def _():
            step(False)

        @pl.when(j == i)
        def _():
            step(True)
            o_ref[...] = (acc_scr[...] / l_scr[...]).astype(o_ref.dtype)
            lse_ref[...] = (m_scr[...] + jnp.log(l_scr[...]))[:, :1]

    return pl.pallas_call(
        body, name=name, grid=(h_n, nq, nq),
        in_specs=[
            pl.BlockSpec((tq, QK_NOPE), lambda h, i, j: (i, h)),
            pl.BlockSpec((None, tq, QK_ROPE), lambda h, i, j: (h, i, 0)),
            pl.BlockSpec((tq, QK_NOPE), lambda h, i, j: (jnp.minimum(j, i), 2 * h)),
            pl.BlockSpec((tq, V_HEAD), lambda h, i, j: (jnp.minimum(j, i), 2 * h + 1)),
            pl.BlockSpec((tq, QK_ROPE), lambda h, i, j: (jnp.minimum(j, i), 0)),
        ],
        out_specs=[
            pl.BlockSpec((tq, V_HEAD), lambda h, i, j: (i, h)),
            pl.BlockSpec((None, tq, 1), lambda h, i, j: (h, i, 0)),
        ],
        out_shape=[jax.ShapeDtypeStruct((t, h_n * V_HEAD), BF16),
                   jax.ShapeDtypeStruct((h_n, t, 1), F32)],
        scratch_shapes=[pltpu.VMEM((tq, LANE), F32), pltpu.VMEM((tq, LANE), F32),
                        pltpu.VMEM((tq, V_HEAD), F32)],
        compiler_params=_cp(("parallel", "parallel", "arbitrary")),
    )(qn, qr, kv, kv, kr)


def _attn_bwd_call(qn, qr, kv, kr, o, lse, do, name):
    t = qn.shape[0]
    h_n = MLA_HEADS
    tq = _rows(t, 512)
    nq = t // tq
    scale = (QK_NOPE + QK_ROPE) ** -0.5

    def body(qn_ref, qr_ref, kn_ref, v_ref, kr_ref, o_ref, lse_ref, do_ref,
             dqn_ref, dqr_ref, dkv_ref, dkr_ref, dqn_scr, dqr_scr, dkn_scr, dv_scr, dkr_scr):
        j, i = pl.program_id(1), pl.program_id(2)

        @pl.when(jnp.logical_and(j == 0, i == 0))
        def _():
            dqn_scr[...] = jnp.zeros_like(dqn_scr)
            dqr_scr[...] = jnp.zeros_like(dqr_scr)

        @pl.when(i == 0)
        def _():
            dkn_scr[...] = jnp.zeros_like(dkn_scr)
            dv_scr[...] = jnp.zeros_like(dv_scr)
            dkr_scr[...] = jnp.zeros_like(dkr_scr)

        def step(diag):
            qn_b = qn_ref[...].astype(BF16)
            qr_b = qr_ref[...].astype(BF16)
            kn_b = kn_ref[...].astype(BF16)
            kr_b = kr_ref[...].astype(BF16)
            do_b = do_ref[...]
            p = jnp.exp(_attn_scores(qn_ref, qr_ref, kn_ref, kr_ref, diag) - lse_ref[...])
            delta = jnp.sum(do_b.astype(F32) * o_ref[...].astype(F32), axis=-1, keepdims=True)
            dp = lax.dot_general(do_b, v_ref[...].astype(BF16), NT, preferred_element_type=F32)
            ds = (p * (dp - delta) * scale).astype(BF16)
            p_b = p.astype(BF16)
            dv_scr[...] += lax.dot_general(p_b, do_b, TN, preferred_element_type=F32)
            dkn_scr[...] += lax.dot_general(ds, qn_b, TN, preferred_element_type=F32)
            dkr_scr[...] += lax.dot_general(ds, qr_b, TN, preferred_element_type=F32)
            sl = pl.ds(pl.multiple_of(i * tq, tq), tq)
            dqn_scr[sl, :] += jnp.dot(ds, kn_b, preferred_element_type=F32)
            dqr_scr[sl, :] += jnp.dot(ds, kr_b, preferred_element_type=F32)

        @pl.when(i > j)
        def _():
            step(False)

        @pl.when(i == j)
        def _():
            step(True)

        @pl.when(i == nq - 1)
        def _():
            dkv_ref[:, :QK_NOPE] = dkn_scr[...].astype(dkv_ref.dtype)
            dkv_ref[:, QK_NOPE:] = dv_scr[...].astype(dkv_ref.dtype)
            dkr_ref[...] = dkr_scr[...]

        @pl.when(jnp.logical_and(j == nq - 1, i == nq - 1))
        def _():
            dqn_ref[...] = dqn_scr[...].astype(dqn_ref.dtype)
            dqr_ref[...] = dqr_scr[...].astype(dqr_ref.dtype)

    qi = lambda j, i: jnp.maximum(i, j)
    return pl.pallas_call(
        body, name=name, grid=(h_n, nq, nq),
        in_specs=[
            pl.BlockSpec((tq, QK_NOPE), lambda h, j, i: (qi(j, i), h)),
            pl.BlockSpec((None, tq, QK_ROPE), lambda h, j, i: (h, qi(j, i), 0)),
            pl.BlockSpec((tq, QK_NOPE), lambda h, j, i: (j, 2 * h)),
            pl.BlockSpec((tq, V_HEAD), lambda h, j, i: (j, 2 * h + 1)),
            pl.BlockSpec((tq, QK_ROPE), lambda h, j, i: (j, 0)),
            pl.BlockSpec((tq, V_HEAD), lambda h, j, i: (qi(j, i), h)),
            pl.BlockSpec((None, tq, 1), lambda h, j, i: (h, qi(j, i), 0)),
            pl.BlockSpec((tq, V_HEAD), lambda h, j, i: (qi(j, i), h)),
        ],
        out_specs=[
            pl.BlockSpec((t, QK_NOPE), lambda h, j, i: (0, h)),
            pl.BlockSpec((None, t, QK_ROPE), lambda h, j, i: (h, 0, 0)),
            pl.BlockSpec((tq, QK_NOPE + V_HEAD), lambda h, j, i: (j, h)),
            pl.BlockSpec((None, tq, QK_ROPE), lambda h, j, i: (h, j, 0)),
        ],
        out_shape=[jax.ShapeDtypeStruct((t, h_n * QK_NOPE), qn.dtype),
                   jax.ShapeDtypeStruct((h_n, t, QK_ROPE), qr.dtype),
                   jax.ShapeDtypeStruct((t, h_n * (QK_NOPE + V_HEAD)), kv.dtype),
                   jax.ShapeDtypeStruct((h_n, t, QK_ROPE), F32)],
        scratch_shapes=[pltpu.VMEM((t, QK_NOPE), F32), pltpu.VMEM((t, QK_ROPE), F32),
                        pltpu.VMEM((tq, QK_NOPE), F32), pltpu.VMEM((tq, V_HEAD), F32),
                        pltpu.VMEM((tq, QK_ROPE), F32)],
        compiler_params=_cp(("parallel", "arbitrary", "arbitrary")),
    )(qn, qr, kv, kv, kr, o, lse, do)


@functools.partial(jax.custom_vjp, nondiff_argnums=(4,))
def attention(qn, qr, kv, kr, tag):
    return _attn_fwd_call(qn, qr, kv, kr, "attn_" + tag)[0]


def _attention_f(qn, qr, kv, kr, tag):
    o, lse = _attn_fwd_call(qn, qr, kv, kr, "attn_" + tag)
    return o, (qn, qr, kv, kr, o, lse)


def _attention_b(tag, res, do):
    dqn, dqr, dkv, dkr_h = _attn_bwd_call(*res, do, "attn_" + tag + "_bwd")
    return dqn, dqr, dkv, jnp.sum(dkr_h, axis=0).astype(res[3].dtype)


attention.defvjp(_attention_f, _attention_b)


def _shift_down(u, s):
    if s == 0:
        return u
    t = u.shape[0]
    rolled = pltpu.roll(u, s, 0)
    return jnp.where(lax.broadcasted_iota(jnp.int32, u.shape, 0) >= s, rolled, 0.0)


def _shift_up(u, s):
    if s == 0:
        return u
    t = u.shape[0]
    rolled = pltpu.roll(u, t - s, 0)
    return jnp.where(lax.broadcasted_iota(jnp.int32, u.shape, 0) < t - s, rolled, 0.0)


def _conv_blocks(t, c3):
    p = c3 // 3
    tc = _tile(p, 512)
    per = p // tc
    return p, tc, per


def _conv_fwd_call(u, w, name):
    t, c3 = u.shape
    p, tc, per = _conv_blocks(t, c3)

    def body(u_ref, w_ref, o_ref):
        u = u_ref[...].astype(F32)
        y = jnp.zeros_like(u)
        for j in range(CONV_WIDTH):
            y = y + w_ref[j:j + 1, :] * _shift_down(u, CONV_WIDTH - 1 - j)
        o_ref[...] = y * _sigmoid(y)

    return pl.pallas_call(
        body, name=name, grid=(c3 // tc,),
        in_specs=[pl.BlockSpec((t, tc), lambda cb: (0, cb)),
                  pl.BlockSpec((CONV_WIDTH, tc), lambda cb: (0, cb))],
        out_specs=pl.BlockSpec((None, t, tc), lambda cb: (cb // per, 0, cb % per)),
        out_shape=jax.ShapeDtypeStruct((3, t, p), F32),
        compiler_params=_cp(("parallel",)),
    )(u, w)


def _conv_bwd_call(u, w, do, name):
    t, c3 = u.shape
    p, tc, per = _conv_blocks(t, c3)

    def body(u_ref, w_ref, do_ref, du_ref, dw_ref):
        u = u_ref[...].astype(F32)
        shifted = [_shift_down(u, CONV_WIDTH - 1 - j) for j in range(CONV_WIDTH)]
        y = jnp.zeros_like(u)
        for j in range(CONV_WIDTH):
            y = y + w_ref[j:j + 1, :] * shifted[j]
        s = _sigmoid(y)
        dy = do_ref[...] * s * (1.0 + y * (1.0 - s))
        du = jnp.zeros_like(u)
        for j in range(CONV_WIDTH):
            du = du + w_ref[j:j + 1, :] * _shift_up(dy, CONV_WIDTH - 1 - j)
            dw_ref[j:j + 1, :] = jnp.sum(dy * shifted[j], axis=0, keepdims=True)
        du_ref[...] = du.astype(du_ref.dtype)

    return pl.pallas_call(
        body, name=name, grid=(c3 // tc,),
        in_specs=[pl.BlockSpec((t, tc), lambda cb: (0, cb)),
                  pl.BlockSpec((CONV_WIDTH, tc), lambda cb: (0, cb)),
                  pl.BlockSpec((None, t, tc), lambda cb: (cb // per, 0, cb % per))],
        out_specs=[pl.BlockSpec((t, tc), lambda cb: (0, cb)),
                   pl.BlockSpec((CONV_WIDTH, tc), lambda cb: (0, cb))],
        out_shape=[jax.ShapeDtypeStruct((t, c3), u.dtype), jax.ShapeDtypeStruct((CONV_WIDTH, c3), F32)],
        compiler_params=_cp(("parallel",)),
    )(u, w, do)


@functools.partial(jax.custom_vjp, nondiff_argnums=(2,))
def conv_silu(u, w, tag):
    return _conv_fwd_call(u, w, "conv_" + tag)


def _conv_silu_f(u, w, tag):
    return _conv_fwd_call(u, w, "conv_" + tag), (u, w)


def _conv_silu_b(tag, res, do):
    return tuple(_conv_bwd_call(*res, do, "conv_" + tag + "_bwd"))


conv_silu.defvjp(_conv_silu_f, _conv_silu_b)


BNN = (((2,), (1,)), ((0,), (0,)))
BNT = (((2,), (2,)), ((0,), (0,)))
BTN = (((1,), (1,)), ((0,), (0,)))


def _bf16_dot(a, b, dn):
    return lax.dot_general(a.astype(BF16), b.astype(BF16), dn, preferred_element_type=F32)


def _mask_dot(mask, v, dn):
    p1 = v.astype(BF16)
    r1 = v - p1.astype(F32)
    p2 = r1.astype(BF16)
    p3 = r1 - p2.astype(F32)
    return _bf16_dot(mask, p1, dn) + (_bf16_dot(mask, p2, dn) + _bf16_dot(mask, p3, dn))


@jax.custom_vjp
def _xdot(mask, v):
    return _mask_dot(mask, v, BNN)


def _xdot_f(mask, v):
    return _mask_dot(mask, v, BNN), (mask,)


def _xdot_b(res, g):
    (mask,) = res
    return jnp.zeros_like(mask), _mask_dot(mask, g, BTN)


_xdot.defvjp(_xdot_f, _xdot_b)


def _dot3(a, b, dn):
    ah, bh = a.astype(BF16), b.astype(BF16)
    al, bl = a - ah.astype(F32), b - bh.astype(F32)
    return _bf16_dot(ah, bh, dn) + (_bf16_dot(ah, bl, dn) + _bf16_dot(al, bh, dn))


def _transposed(dn, a, b, g):
    if dn == BNN:
        return (g, b, BNT), (a, g, BTN)
    if dn == BNT:
        return (g, b, BNN), (g, a, BTN)
    return (b, g, BNT), (a, g, BNN)


@functools.partial(jax.custom_vjp, nondiff_argnums=(2,))
def _hdot(a, b, dn=BNN):
    return _dot3(a, b, dn)


def _hdot_f(a, b, dn):
    return _dot3(a, b, dn), (a, b)


def _hdot_b(dn, res, g):
    da, db = _transposed(dn, *res, g)
    return _dot3(*da), _dot3(*db)


_hdot.defvjp(_hdot_f, _hdot_b)


@functools.partial(jax.custom_vjp, nondiff_argnums=(2,))
def _bdot(a, b, dn=BNN):
    return _bf16_dot(a, b, dn)


def _bdot_f(a, b, dn):
    return _bf16_dot(a, b, dn), (a, b)


def _bdot_b(dn, res, g):
    da, db = _transposed(dn, *res, g)
    return _bf16_dot(*da), _bf16_dot(*db)


_bdot.defvjp(_bdot_f, _bdot_b)


def _gdn_chunk(q, k, v, z, bl, al, a_log, dtb, gn, s):
    b, c = q.shape[0], q.shape[1]
    ri = lax.broadcasted_iota(jnp.int32, (c, c), 0)
    ci = lax.broadcasted_iota(jnp.int32, (c, c), 1)
    lower = (ri >= ci)[None]
    strict = (ri > ci)[None]
    low_incl = jnp.broadcast_to((ri >= ci).astype(F32), (b, c, c))
    up_incl = jnp.broadcast_to((ri <= ci).astype(F32), (b, c, c))
    eye = (ri == ci).astype(F32)[None]

    q = q * lax.rsqrt(jnp.sum(q * q, axis=-1, keepdims=True) + EPS) * (GDN_DK ** -0.5)
    k = k * lax.rsqrt(jnp.sum(k * k, axis=-1, keepdims=True) + EPS)
    beta = _sigmoid(bl)
    g = -jnp.exp(a_log) * _softplus(al + dtb)
    g_w = jnp.broadcast_to(g, (b, c, LANE))
    gc = _xdot(low_incl, g_w)
    gr = _xdot(jnp.ones((b, c, c), F32), g_w[:, :, :c] * up_incl)
    diff = gc[:, :, :c] - gr
    decay = jnp.where(lower, jnp.exp(jnp.where(lower, diff, 0.0)), 0.0)
    kb = k * beta
    lmat = jnp.where(strict, _bdot(kb, k, BNT) * decay, 0.0)
    inv = eye - lmat
    pw = lmat
    for _ in range(int(math.log2(c)) - 1):
        pw = _hdot(pw, pw)
        inv = _hdot(inv, eye + pw)
    eg = jnp.exp(gc)
    u = _hdot(inv, v * beta)
    w = _hdot(inv, kb * eg)
    attn = jnp.where(lower, _bdot(q, k, BNT) * decay, 0.0)
    v_new = u - _bdot(w, s)
    o = _bdot(q * eg, s) + _bdot(attn, v_new)
    g_last = jnp.sum(g_w, axis=1, keepdims=True)
    k_dec = k * jnp.exp(g_last - gc)
    s_new = s * jnp.exp(g_last) + _bdot(k_dec, v_new, BTN)
    on = o * lax.rsqrt(jnp.mean(o * o, axis=-1, keepdims=True) + EPS) * gn
    return on * (z * _sigmoid(z)), s_new


def _head_cols(ba, first, count):
    lane = lax.broadcasted_iota(jnp.int32, ba.shape, 1)
    return jnp.stack([jnp.sum(jnp.where(lane == first + j, ba, 0.0), axis=1, keepdims=True)
                      for j in range(count)])


def _gdn_heads(q, k, v, z, ba, a_log, dtb, gn, s):
    h_n = q.shape[0]
    return _gdn_chunk(q, k, v, z, _head_cols(ba, 0, h_n), _head_cols(ba, h_n, h_n), a_log, dtb, gn, s)


def _gdn_specs(n_chunks, hb, rev):
    c = CHUNK
    nn = (lambda n: n_chunks - 1 - n) if rev else (lambda n: n)
    plane = lambda pidx: pl.BlockSpec((None, c, hb * GDN_DK), lambda hg, n: (pidx, nn(n), hg))
    assert hb == GDN_HEADS
    logits = pl.BlockSpec((c, LANE), lambda hg, n: (nn(n), 0))
    scal = pl.BlockSpec((hb, 1, 1), lambda hg, n: (hg, 0, 0))
    zspec = pl.BlockSpec((c, hb * GDN_DV), lambda hg, n: (nn(n), hg))
    gnspec = pl.BlockSpec((1, GDN_DV), lambda hg, n: (0, 0))
    sspec = pl.BlockSpec((hb, None, GDN_DK, GDN_DV), lambda hg, n: (hg, nn(n), 0, 0))
    return plane, logits, scal, zspec, gnspec, sspec


def _heads(ref, hb):
    return jnp.stack([ref[:, j * GDN_DK:(j + 1) * GDN_DK] for j in range(hb)])


def _gdn_fwd_call(qkv, z, ba, a_log, dtb, gn, name):
    t = z.shape[0]
    h_n = GDN_HEADS
    hb = h_n
    n_chunks = t // CHUNK
    plane, logits, scal, zspec, gnspec, sspec = _gdn_specs(n_chunks, hb, False)

    def body(q_ref, k_ref, v_ref, z_ref, ba_ref, a_ref, dtb_ref, gn_ref, o_ref, sall_ref, s_scr):
        n = pl.program_id(1)

        @pl.when(n == 0)
        def _():
            s_scr[...] = jnp.zeros_like(s_scr)

        s = s_scr[...]
        sall_ref[...] = s
        o, s_new = _gdn_heads(_heads(q_ref, hb), _heads(k_ref, hb), _heads(v_ref, hb),
                              _heads(z_ref, hb).astype(F32),
                              ba_ref[...], a_ref[...], dtb_ref[...], gn_ref[...], s)
        for j in range(hb):
            o_ref[:, j * GDN_DV:(j + 1) * GDN_DV] = o[j].astype(o_ref.dtype)
        s_scr[...] = s_new

    return pl.pallas_call(
        body, name=name, grid=(h_n // hb, n_chunks),
        in_specs=[plane(0), plane(1), plane(2), zspec, logits, scal, scal, gnspec],
        out_specs=[zspec, sspec],
        out_shape=[jax.ShapeDtypeStruct((t, h_n * GDN_DV), BF16),
                   jax.ShapeDtypeStruct((h_n, n_chunks, GDN_DK, GDN_DV), F32)],
        scratch_shapes=[pltpu.VMEM((hb, GDN_DK, GDN_DV), F32)],
        compiler_params=_cp(("parallel", "arbitrary")),
    )(qkv, qkv, qkv, z, ba, a_log, dtb, gn)


def _gdn_bwd_call(qkv, z, ba, a_log, dtb, gn, sall, do, name):
    t = z.shape[0]
    h_n = GDN_HEADS
    hb = h_n
    n_chunks = t // CHUNK
    c = CHUNK
    plane, logits, scal, zspec, gnspec, sspec = _gdn_specs(n_chunks, hb, True)
    dplanes = pl.BlockSpec((3, c, hb * GDN_DK), lambda hg, n: (0, n_chunks - 1 - n, hg))
    gnh = pl.BlockSpec((None, 1, GDN_DV), lambda hg, n: (hg, 0, 0))

    def body(q_ref, k_ref, v_ref, z_ref, ba_ref, a_ref, dtb_ref, gn_ref, s_ref, do_ref,
             dqkv_ref, dz_ref, dba_ref, da_ref, ddtb_ref, dgn_ref, ds_scr):
        n = pl.program_id(1)

        @pl.when(n == 0)
        def _():
            ds_scr[...] = jnp.zeros_like(ds_scr)
            da_ref[...] = jnp.zeros_like(da_ref)
            ddtb_ref[...] = jnp.zeros_like(ddtb_ref)
            dgn_ref[...] = jnp.zeros_like(dgn_ref)

        _, vjp = jax.vjp(_gdn_heads, _heads(q_ref, hb), _heads(k_ref, hb), _heads(v_ref, hb),
                         _heads(z_ref, hb).astype(F32),
                         ba_ref[...], a_ref[...], dtb_ref[...], gn_ref[...], s_ref[...])
        dq, dk, dv, dz, dba, da, ddtb, dgn, ds = vjp((_heads(do_ref, hb).astype(F32), ds_scr[...]))
        for j in range(hb):
            hs = slice(j * GDN_DK, (j + 1) * GDN_DK)
            dqkv_ref[0, :, hs] = dq[j]
            dqkv_ref[1, :, hs] = dk[j]
            dqkv_ref[2, :, hs] = dv[j]
            dz_ref[:, hs] = dz[j].astype(dz_ref.dtype)
        dba_ref[...] = dba
        da_ref[...] += da
        ddtb_ref[...] += ddtb
        dgn_ref[...] += dgn
        ds_scr[...] = ds

    return pl.pallas_call(
        body, name=name, grid=(h_n // hb, n_chunks),
        in_specs=[plane(0), plane(1), plane(2), zspec, logits, scal, scal, gnspec, sspec, zspec],
        out_specs=[dplanes, zspec, logits, scal, scal, gnh],
        out_shape=[jax.ShapeDtypeStruct((3, t, h_n * GDN_DK), F32),
                   jax.ShapeDtypeStruct((t, h_n * GDN_DV), z.dtype),
                   jax.ShapeDtypeStruct((t, LANE), F32),
                   jax.ShapeDtypeStruct((h_n, 1, 1), F32), jax.ShapeDtypeStruct((h_n, 1, 1), F32),
                   jax.ShapeDtypeStruct((h_n // hb, 1, GDN_DV), F32)],
        scratch_shapes=[pltpu.VMEM((hb, GDN_DK, GDN_DV), F32)],
        compiler_params=_cp(("parallel", "arbitrary")),
    )(qkv, qkv, qkv, z, ba, a_log, dtb, gn, sall, do)


@functools.partial(jax.custom_vjp, nondiff_argnums=(6,))
def gdn(qkv, z, ba, a_log, dtb, gn, tag):
    return _gdn_fwd_call(qkv, z, ba, a_log, dtb, gn, "gdn_" + tag)[0]


def _gdn_f(qkv, z, ba, a_log, dtb, gn, tag):
    o, sall = _gdn_fwd_call(qkv, z, ba, a_log, dtb, gn, "gdn_" + tag)
    return o, (qkv, z, ba, a_log, dtb, gn, sall)


def _gdn_b(tag, res, do):
    dqkv, dz, dba, da, ddtb, dgn_h = _gdn_bwd_call(*res, do, "gdn_" + tag + "_bwd")
    return dqkv, dz, dba, da, ddtb, jnp.sum(dgn_h, axis=0)


gdn.defvjp(_gdn_f, _gdn_b)


ADAMW_BLOCK = 384 * 1024


def adamw(w, m, v, *, parts, name):
    n_layers = len(parts)
    n_parts, r, c = parts[0].shape
    assert w.shape == (n_layers * r, c), (w.shape, parts[0].shape)
    tr = r
    for cand in (512, 256, 128, 64, 32, 16, 8):
        if r % cand == 0 and cand * c <= ADAMW_BLOCK:
            tr = cand
            break
    nb = r // tr
    blk = pl.BlockSpec((tr, c), lambda l, i: (l * nb + i, 0))
    bc1 = 1.0 - ADAM_B1 ** ADAM_STEP
    bc2 = 1.0 - ADAM_B2 ** ADAM_STEP

    def part_spec(li):
        return pl.BlockSpec((n_parts, tr, c),
                            lambda l, i: (0, jnp.where(l == li, i, jnp.where(l < li, 0, nb - 1)), 0))

    def body(*refs):
        w_ref, p_refs = refs[0], refs[1:1 + n_layers]
        m_ref, v_ref, g_ref, d_ref, mo_ref, vo_ref = refs[1 + n_layers:]
        for li in range(n_layers):
            @pl.when(pl.program_id(0) == li)
            def _(p_ref=p_refs[li]):
                g = p_ref[0].astype(F32)
                for i in range(1, n_parts):
                    g = g + p_ref[i].astype(F32)
                m2 = ADAM_B1 * m_ref[...] + (1.0 - ADAM_B1) * g
                v2 = ADAM_B2 * v_ref[...] + (1.0 - ADAM_B2) * (g * g)
                g_ref[...] = g
                mo_ref[...] = m2
                vo_ref[...] = v2
                d_ref[...] = -ADAM_LR * ((m2 / bc1) / (jnp.sqrt(v2 / bc2) + ADAM_EPS)
                                         + ADAM_WD * w_ref[...])

    return pl.pallas_call(
        body, name=name, grid=(n_layers, nb),
        in_specs=[blk] + [part_spec(li) for li in range(n_layers)] + [blk, blk],
        out_specs=[blk] * 4, out_shape=[jax.ShapeDtypeStruct(w.shape, F32)] * 4,
        compiler_params=_cp(("arbitrary", "arbitrary")),
    )(w, *parts, m, v)


_HBM = pl.BlockSpec(memory_space=pltpu.HBM)
_SEM = pl.BlockSpec(memory_space=pltpu.SEMAPHORE)
_EFFECT = pltpu.SideEffectType.DATAFLOW_SIDE_EFFECTING


def _peer(x, y, c, d):
    px = 1 - x if d & 4 else x
    py = 1 - y if d & 2 else y
    pc = 1 - c if d & 1 else c
    return (px, py, pc), 4 * px + 2 * py + pc


ALL_PEERS = (1, 2, 3, 4, 5, 6, 7)
SIBLING = 1
SAME_CORE_REMOTE = (2, 4, 6)


def copy_start(arrays, mode, carry, name):
    n = len(arrays)
    if mode == "forward":
        lands = []
    else:
        lands = [lax.empty(a.shape if mode == "scatter" else (N_DEV,) + a.shape, a.dtype) for a in arrays]
    n_in = n + len(lands) + 1

    def body(*refs):
        srcs = refs[:n]
        dsts = refs[n:2 * n] if lands else srcs
        sems = refs[n_in:n_in + 2 * n]
        x, y, c = (lax.axis_index(a) for a in AXES)
        me = 4 * x + 2 * y + c
        for k in range(n):
            if mode == "forward":
                sibling, _ = _peer(x, y, c, SIBLING)
                copies = [(srcs[k].at[_peer(x, y, c, d)[1]], dsts[k].at[_peer(x, y, c, d)[1]], sibling)
                          for d in SAME_CORE_REMOTE]
            elif mode == "gather":
                copies = [(srcs[k], dsts[k].at[me], _peer(x, y, c, d)[0]) for d in (SIBLING,) + SAME_CORE_REMOTE]
            else:
                copies = [(srcs[k].at[_peer(x, y, c, d)[1]], dsts[k].at[me], _peer(x, y, c, d)[0])
                          for d in ALL_PEERS]
            for src, dst, peer in copies:
                pltpu.make_async_remote_copy(src_ref=src, dst_ref=dst, send_sem=sems[2 * k],
                                             recv_sem=sems[2 * k + 1], device_id=peer,
                                             device_id_type=pl.DeviceIdType.MESH).start()

    operands = list(arrays) + lands + [carry]
    outs = pl.pallas_call(
        body, name=name,
        out_shape=tuple([pltpu.SemaphoreType.DMA(())] * (2 * n)
                        + [pltpu.HBM(a.shape, a.dtype) for a in operands]),
        in_specs=[_HBM] * n_in,
        out_specs=tuple([_SEM] * (2 * n) + [_HBM] * n_in),
        input_output_aliases={i: 2 * n + i for i in range(n_in)},
        compiler_params=pltpu.CompilerParams(has_side_effects=_EFFECT),
    )(*[pltpu.with_memory_space_constraint(a, pltpu.HBM) for a in operands])
    sems, thru = outs[:2 * n], outs[2 * n:-1]
    handles = [(sems[2 * k], sems[2 * k + 1], thru[k] if lands else None, thru[n + k] if lands else thru[k])
               for k in range(n)]
    return outs[-1], handles


def copy_wait(handles, n_blocks, after, name):
    n = len(handles)
    sems = [s for h in handles for s in h[:2]]
    srcs = [h[2] for h in handles if h[2] is not None]
    lands = [h[3] for h in handles]
    ns = len(srcs)

    def body(*refs):
        dsts = refs[ns:ns + n]
        sem_refs = refs[ns + n:ns + 3 * n]
        x, y, c = (lax.axis_index(a) for a in AXES)
        for k in range(n):
            blocks = dsts[k].at[pl.ds(0, n_blocks)]
            pltpu.make_async_remote_copy(
                src_ref=blocks, dst_ref=blocks, send_sem=sem_refs[2 * k], recv_sem=sem_refs[2 * k + 1],
                device_id=(x, y, c), device_id_type=pl.DeviceIdType.MESH).wait()

    outs = pl.pallas_call(
        body, name=name,
        out_shape=tuple([pltpu.HBM(a.shape, a.dtype) for a in srcs + lands]),
        in_specs=[_HBM] * (ns + n) + [_SEM] * (2 * n) + [pl.BlockSpec(memory_space=pl.ANY)],
        out_specs=tuple([_HBM] * (ns + n)),
        input_output_aliases={i: i for i in range(ns + n)},
        compiler_params=pltpu.CompilerParams(has_side_effects=_EFFECT),
    )(*srcs, *lands, *sems, after)
    return (list(outs[:ns]) if ns else [None] * n), list(outs[ns:])


def exchange(arrays, modes, name):
    n = len(arrays)
    hbm = pl.BlockSpec(memory_space=pltpu.HBM)
    out_shape = [jax.ShapeDtypeStruct(a.shape if md == "scatter" else (N_DEV,) + a.shape, a.dtype)
                 for a, md in zip(arrays, modes)]

    def body(*refs):
        ins, outs = refs[:n], refs[n:2 * n]
        send_sems, recv_sems, local_sems = refs[2 * n:]
        x, y, c = (lax.axis_index(a) for a in AXES)
        me = 4 * x + 2 * y + c

        def src(k, p):
            return ins[k].at[p] if modes[k] == "scatter" else ins[k]

        local = [pltpu.make_async_copy(src(k, me), outs[k].at[me], local_sems.at[k]) for k in range(n)]
        for cp in local:
            cp.start()
        started = []
        for d in range(1, N_DEV):
            px = 1 - x if d & 4 else x
            py = 1 - y if d & 2 else y
            pc = 1 - c if d & 1 else c
            pid = 4 * px + 2 * py + pc
            for k in range(n):
                pltpu.make_async_remote_copy(
                    src_ref=src(k, pid), dst_ref=outs[k].at[me],
                    send_sem=send_sems.at[k, d - 1], recv_sem=recv_sems.at[k, d - 1],
                    device_id=(px, py, pc), device_id_type=pl.DeviceIdType.MESH).start()
                started.append((k, d, pid, (px, py, pc)))
        for k, d, pid, peer in started:
            pltpu.make_async_remote_copy(
                src_ref=src(k, pid), dst_ref=outs[k].at[pid],
                send_sem=send_sems.at[k, d - 1], recv_sem=recv_sems.at[k, d - 1],
                device_id=peer, device_id_type=pl.DeviceIdType.MESH).wait()
        for cp in local:
            cp.wait()

    outs = pl.pallas_call(
        body, name=name, in_specs=[hbm] * n, out_specs=[hbm] * n, out_shape=out_shape,
        scratch_shapes=[pltpu.SemaphoreType.DMA((n, N_DEV - 1)), pltpu.SemaphoreType.DMA((n, N_DEV - 1)),
                        pltpu.SemaphoreType.DMA((n,))],
        compiler_params=pltpu.CompilerParams(has_side_effects=True),
    )(*arrays)
    return list(outs)


SMALL = ("b_ada", "norm_mix", "norm_ffn", "q_a_norm", "kv_a_norm", "A_log", "dt_bias", "gdn_norm",
         "final_norm")
WEIGHTS = ("w_ada", "b_ada", "norm_mix", "norm_ffn", "w_in", "q_a_norm", "kv_a_norm", "w_uq", "w_ukv",
           "w_o_mla", "conv_w", "A_log", "dt_bias", "gdn_norm", "w_o_gdn", "w_o", "w_gate_up", "w_down",
           "final_norm")


def _unslot(g):
    return g.transpose(1, 0, 2).reshape(g.shape[1], -1)


def _cols(g):
    return g if g.shape[-1] % LANE == 0 else _unslot(g)


def _stack_rows(g):
    return g.reshape(-1, g.shape[-1])


def _rope(xv, cos, sin):
    x1, x2 = jnp.split(xv, 2, axis=-1)
    return jnp.concatenate([x1 * cos - x2 * sin, x2 * cos + x1 * sin], axis=-1)


MIX_WEIGHTS = ("w_uq", "w_ukv", "w_o_mla", "w_o_gdn", "w_o")
FFN_WEIGHTS = ("w_gate_up", "w_down")


def _pad_cols(a):
    return jnp.pad(a, ((0, 0), (0, _pad_lanes(a.shape[1]) - a.shape[1])))


def _stage_in(x, mod, nm, w_in_s, tg):
    d = x.shape[1]
    hg = GDN_HEADS
    w_in = _unslot(w_in_s)
    o1 = Q_LORA + KV_LORA + QK_ROPE
    o2 = o1 + 2 * hg * GDN_DK + hg * GDN_DV
    o3 = o2 + hg * GDN_DV
    o4 = o3 + 2 * hg
    h = ada_norm(x, nm, mod[:, d:2 * d], mod[:, :d], "mix" + tg)
    return (mm(h, _pad_cols(w_in[:, :o1]), "in_a" + tg, BF16), mm(h, w_in[:, o1:o2], "in_qkv" + tg, BF16),
            mm(h, w_in[:, o2:o3], "in_z" + tg, BF16), mm(h, _pad_cols(w_in[:, o3:o4]), "in_ba" + tg, F32),
            mm(h, w_in[:, o4:o4 + 2 * d], "in_g" + tg, BF16))


def _stage_mix(x, mod, seg_a, qkv, z, ba, gl, w_uq_s, w_ukv_s, w_o_mla_s, w_o_gdn_s, w_o_s, conv_s,
               qan, kvan, a_log, dtb, gn, cos, sin, tg):
    t, d = x.shape
    hq, hg = MLA_HEADS, GDN_HEADS
    w_uq = _unslot(w_uq_s).reshape(Q_LORA, hq, QK_NOPE + QK_ROPE)
    w_uq = jnp.concatenate([w_uq[:, :, :QK_NOPE].reshape(Q_LORA, hq * QK_NOPE),
                            w_uq[:, :, QK_NOPE:].reshape(Q_LORA, hq * QK_ROPE)], axis=1)
    c_q = seg_a[:, :Q_LORA]
    c_kv = seg_a[:, Q_LORA:Q_LORA + KV_LORA]
    k_pe = seg_a[:, Q_LORA + KV_LORA:Q_LORA + KV_LORA + QK_ROPE]
    qf = mm(rms_norm(c_q, qan, "qa" + tg), w_uq, "uq" + tg, BF16)
    kvf = mm(rms_norm(c_kv, kvan, "kva" + tg), _cols(w_ukv_s), "ukv" + tg, BF16)
    qn = qf[:, :hq * QK_NOPE]
    q_pe = qf[:, hq * QK_NOPE:].astype(F32).reshape(t, hq, QK_ROPE)
    qr = _rope(q_pe, cos[:, None, :], sin[:, None, :]).transpose(1, 0, 2).astype(BF16)
    kr = _rope(k_pe.astype(F32), cos, sin).astype(BF16)
    y_a = mm(attention(qn, qr, kvf, kr, tg), _cols(w_o_mla_s), "o_mla" + tg, BF16)
    conv_w = conv_s.transpose(1, 0, 2).reshape(CONV_WIDTH, -1)
    qkv_c = conv_silu(qkv, conv_w, tg)
    o_gdn = gdn(qkv_c, z, ba, a_log.reshape(hg, 1, 1), dtb.reshape(hg, 1, 1), gn, tg)
    y_b = mm(o_gdn, _cols(w_o_gdn_s), "o_gdn" + tg, BF16)
    return mm_resid(x, mod[:, 2 * d:3 * d], gate_mix(gl, y_a, y_b, tg), _stack_rows(w_o_s), "w_o" + tg)


def _stage_ffn(x, mod, nf, w_gu_s, w_down_s, tg):
    d = x.shape[1]
    h = ada_norm(x, nf, mod[:, 4 * d:5 * d], mod[:, 3 * d:4 * d], "ffn" + tg)
    gu = mm(h, _cols(w_gu_s), "gu" + tg, BF16)
    return mm_resid(x, mod[:, 5 * d:6 * d], swiglu(gu, tg), _stack_rows(w_down_s), "down" + tg)


def _flat_row(arrs):
    v = jnp.concatenate([a.reshape(-1) for a in arrs])
    return jnp.pad(v, (0, _pad_lanes(v.shape[0]) - v.shape[0]))[None, :]


def kernel(x, c, positions, w_ada, b_ada, norm_mix, norm_ffn, w_in, q_a_norm, kv_a_norm, w_uq, w_ukv, w_o_mla, conv_w, A_log, dt_bias, gdn_norm, w_o_gdn, w_o, w_gate_up, w_down, final_norm, loss_target, m_w_ada, m_b_ada, m_norm_mix, m_norm_ffn, m_w_in, m_q_a_norm, m_kv_a_norm, m_w_uq, m_w_ukv, m_w_o_mla, m_conv_w, m_A_log, m_dt_bias, m_gdn_norm, m_w_o_gdn, m_w_o, m_w_gate_up, m_w_down, m_final_norm, v_w_ada, v_b_ada, v_norm_mix, v_norm_ffn, v_w_in, v_q_a_norm, v_kv_a_norm, v_w_uq, v_w_ukv, v_w_o_mla, v_conv_w, v_A_log, v_dt_bias, v_gdn_norm, v_w_o_gdn, v_w_o, v_w_gate_up, v_w_down, v_final_norm):
    given = dict(locals())
    t, d = x.shape[1], x.shape[2]
    n_ada = w_ada.shape[2]
    me = 4 * lax.axis_index("x") + 2 * lax.axis_index("y") + lax.axis_index("c")

    def with_own(land, own):
        return lax.dynamic_update_slice(land, own[None], (me,) + (0,) * own.ndim)

    got = exchange([c, conv_w], ["gather", "gather"], "gather_small")
    c_all, conv_g = got[0].reshape(N_DEV, d), got[1]
    c_rows = jnp.pad(c_all, ((0, 16 - N_DEV), (0, 0)))
    mod_cols = jnp.stack([_mm(c_rows, w_ada[l], "nn", F32, "ada_mod%d" % l, a_act="silu")[:N_DEV]
                          for l in range(DEPTH)], axis=1)
    mod_mine = exchange([mod_cols], ["scatter"], "scatter_mod")[0]
    mods = mod_mine.transpose(1, 0, 2).reshape(DEPTH, N_DEV * n_ada) + b_ada

    groups = [[(n, l) for n in names] for l in range(DEPTH) for names in (("w_in",), MIX_WEIGHTS, FFN_WEIGHTS)]
    gtags = [s + str(l) for l in range(DEPTH) for s in ("in", "mix", "ffn")]
    keys = [k for ks in groups for k in ks]
    mods, handles = copy_start([given[n][l].astype(BF16) for n, l in keys], "gather", mods, "gather_start")
    handles = dict(zip(keys, handles))
    own, relayed = {}, {}

    def relay(gi, carry):
        ks = groups[gi]
        srcs, lands = copy_wait([handles[k] for k in ks], 1 + len(SAME_CORE_REMOTE), carry,
                                "wait_ici_" + gtags[gi])
        own.update(zip(ks, srcs))
        carry, hs = copy_start(lands, "forward", carry, "relay_" + gtags[gi])
        relayed.update(zip(ks, hs))
        return carry

    def landed(gi, after):
        ks = groups[gi]
        _, lands = copy_wait([relayed[k] for k in ks], len(SAME_CORE_REMOTE), after, "wait_" + gtags[gi])
        return [with_own(land, own[k]) for k, land in zip(ks, lands)]

    inv_freq = 1.0 / (ROPE_THETA ** (jnp.arange(0, QK_ROPE, 2, dtype=F32) / QK_ROPE))
    ang = positions[0].astype(F32)[:, None] * inv_freq
    cos, sin = jnp.cos(ang), jnp.sin(ang)
    relay_before = {0: [0], 1: [1], 2: [2, 3], 3: [4], 4: [5], 5: []}

    def weights_for(stage, carry):
        for gi in relay_before[stage]:
            carry = relay(gi, carry)
        return carry, landed(stage, carry)

    xl = x[0]
    vjps = []
    for l in range(DEPTH):
        tg = str(l)
        mod = mods[l:l + 1]
        if l == 0:
            mods, (w_in_s,) = weights_for(0, mods)
            mod = mods[:1]
        else:
            xl, (w_in_s,) = weights_for(3 * l, xl)
        seg, vjp_in = jax.vjp(lambda *a, tg=tg: _stage_in(*a, tg), xl, mod, norm_mix[l:l + 1], w_in_s)
        seg0, w_mix = weights_for(3 * l + 1, seg[0])
        seg = (seg0,) + tuple(seg[1:])
        xm, vjp_mix = jax.vjp(lambda *a, tg=tg: _stage_mix(*a, cos, sin, tg), xl, mod, *seg, *w_mix,
                              conv_g[:, l], q_a_norm[l:l + 1], kv_a_norm[l:l + 1], A_log[l], dt_bias[l],
                              gdn_norm[l:l + 1])
        xm, w_ffn = weights_for(3 * l + 2, xm)
        xl, vjp_ffn = jax.vjp(lambda *a, tg=tg: _stage_ffn(*a, tg), xm, mod, norm_ffn[l:l + 1], *w_ffn)
        vjps.append((vjp_in, vjp_mix, vjp_ffn))

    loss_t, g, dfn = loss_head(xl, final_norm[None, :], loss_target[0])
    loss = lax.psum(loss_t[0, 0], AXES)
    dsmall = {n: [None] * DEPTH for n in SMALL + ("conv_w",)}
    dmods = [None] * DEPTH
    sent = {}

    def send(ks, grads, carry, name):
        carry, hs = copy_start(list(grads), "scatter", carry, name)
        sent.update(zip(ks, hs))
        return carry

    for l in reversed(range(DEPTH)):
        tg = str(l)
        vjp_in, vjp_mix, vjp_ffn = vjps[l]
        dxm, dmod_f, dsmall["norm_ffn"][l], *dw = vjp_ffn(g)
        dxm = send([(n, l) for n in FFN_WEIGHTS], dw, dxm, "scatter_ffn" + tg)
        dx_m, dmod_m, *rest = vjp_mix(dxm)
        dseg, dw, rest = rest[:5], rest[5:5 + len(MIX_WEIGHTS)], rest[5 + len(MIX_WEIGHTS):]
        dseg[0] = send([(n, l) for n in MIX_WEIGHTS], dw, dseg[0], "scatter_mix" + tg)
        for n, gr in zip(("conv_w", "q_a_norm", "kv_a_norm", "A_log", "dt_bias", "gdn_norm"), rest):
            dsmall[n][l] = gr
        dx_i, dmod_i, dsmall["norm_mix"][l], dw_in = vjp_in(tuple(dseg))
        g = dx_i + dx_m
        if l > 0:
            g = send([("w_in", l)], [dw_in], g, "scatter_in" + tg)
        dmods[l] = dmod_f + dmod_m + dmod_i
    dx = g
    dmods = jnp.concatenate(dmods, axis=0)
    dconv = jnp.stack(dsmall.pop("conv_w"), axis=1)
    dsmall = {n: jnp.concatenate(v, axis=0) if v[0].ndim == 2 else jnp.stack(v)
              for n, v in dsmall.items() if v[0] is not None}
    dsmall["b_ada"] = dmods
    dsmall["final_norm"] = dfn[0]

    dmod_cols = dmods.reshape(DEPTH, N_DEV, n_ada).transpose(1, 0, 2)
    conv_parts, dmod_all, small_parts = exchange(
        [dconv, dmod_cols, _flat_row([dsmall[n] for n in SMALL])], ["scatter", "scatter", "gather"],
        "exchange_small")
    dmod_all = send([("w_in", 0)], [dw_in], dmod_all, "scatter_in0")

    res = {}
    dm_rows = jnp.pad(dmod_all, ((0, 16 - N_DEV), (0, 0), (0, 0)))
    g_ada = [_mm(c_rows, dm_rows[:, l], "tn", F32, "ada_dw%d" % l, a_act="silu")[None] for l in range(DEPTH)]
    r2 = (DEPTH * d, n_ada)
    outs = adamw(w_ada.reshape(r2), m_w_ada.reshape(r2), v_w_ada.reshape(r2), parts=g_ada, name="adamw_w_ada")
    res["w_ada"] = [o.reshape(w_ada.shape) for o in outs]
    packed = SMALL + ("conv_w",)
    p_all = jnp.concatenate([small_parts, conv_parts.reshape(N_DEV, 1, -1)], axis=2)
    pack = lambda pre: jnp.concatenate([_flat_row([given[pre + n] for n in SMALL]),
                                        given[pre + "conv_w"].reshape(1, -1)], axis=1)
    outs = adamw(pack(""), pack("m_"), pack("v_"), parts=[p_all], name="adamw_small")
    done = [res["w_ada"][1], outs[1]]
    for group, gname in ((FFN_WEIGHTS, "ffn"), (MIX_WEIGHTS, "mix"), (("w_in",), "in")):
        ks = [(n, l) for l in reversed(range(DEPTH)) for n in group]
        after = sum(lax.slice(a, (0,) * a.ndim, (1,) * a.ndim).reshape(1, 1) for a in done)
        srcs, lands = copy_wait([sent[k] for k in ks], len(ALL_PEERS), after, "scatter_wait_" + gname)
        parts = {k: with_own(land, lax.dynamic_index_in_dim(src, me, 0, keepdims=False))
                 for k, src, land in zip(ks, srcs, lands)}
        for n in group:
            w = given[n]
            r2 = (w.shape[0] * w.shape[1], w.shape[2])
            res[n] = [o.reshape(w.shape) for o in
                      adamw(w.reshape(r2), given["m_" + n].reshape(r2), given["v_" + n].reshape(r2),
                            parts=[parts[(n, l)] for l in range(DEPTH)], name="adamw_" + n)]
            done.append(res[n][1])
    off = 0
    for n in packed:
        if n == "conv_w":
            off = small_parts.shape[2]
        size = math.prod(given[n].shape)
        res[n] = [o[0, off:off + size].reshape(given[n].shape) for o in outs]
        off += size

    return (loss, dx[None]) + tuple(res[n][i] for i in range(4) for n in WEIGHTS)
```

```python
import functools
import math

import jax
import jax.numpy as jnp
from jax import lax
from jax.experimental import pallas as pl
from jax.experimental.pallas import tpu as pltpu

F32 = jnp.float32
BF16 = jnp.bfloat16

MLA_HEADS = 8
QK_NOPE = 128
QK_ROPE = 64
V_HEAD = 128
Q_LORA = 512
KV_LORA = 512
ROPE_THETA = 10000.0
GDN_HEADS = 8
GDN_DK = 128
GDN_DV = 128
CONV_WIDTH = 4
CHUNK = 64
DEPTH = 2
EPS = 1e-6
ADAM_LR = 0.001
ADAM_B1 = 0.9
ADAM_B2 = 0.999
ADAM_EPS = 1e-08
ADAM_WD = 0.01
ADAM_STEP = 10

N_DEV = 8
AXES = ("x", "y", "c")
LANE = 128
VMEM_LIMIT = 48 * 1024 * 1024
MM_VMEM_BUDGET = 36 * 1024 * 1024

NN = (((1,), (0,)), ((), ()))
NT = (((1,), (1,)), ((), ()))
TN = (((0,), (0,)), ((), ()))


def _cp(sem=None):
    return pltpu.CompilerParams(dimension_semantics=sem, vmem_limit_bytes=VMEM_LIMIT)


def _tile(n, cap):
    if n <= cap:
        return n
    for t in range(cap - cap % LANE, 0, -LANE):
        if n % t == 0:
            return t
    return n


def _rows(t, cap=256):
    return cap if t % cap == 0 else t


def _pad_lanes(n):
    return -(-n // LANE) * LANE


def _sigmoid(x):
    return 1.0 / (1.0 + jnp.exp(-x))


def _softplus(x):
    return jnp.maximum(x, 0.0) + jnp.log(1.0 + jnp.exp(-jnp.abs(x)))


def _tile_slot(n, cap):
    t = _tile(n, cap)
    return n if t < 256 < n <= 1536 else t


NARROW_SLOT = 512


def _mm_narrow_slots(a, b, dims, out_dtype, name):
    bf = lambda v: v.astype(BF16)
    dot = lambda p, q, dn: lax.dot_general(p, q, dn, preferred_element_type=F32)
    if dims == "nn":
        (m, k), per = a.shape, b.shape[-1]
        spb = min(N_DEV, max(1, 1024 // per))
        tm = _tile(m, 1024)

        def body(a_ref, b_ref, o_ref):
            av = bf(a_ref[...])
            for s in range(spb):
                o_ref[:, s * per:(s + 1) * per] = dot(av, bf(b_ref[s]), NN).astype(o_ref.dtype)

        grid = (m // tm, N_DEV // spb)
        in_specs = [pl.BlockSpec((tm, k), lambda i, j: (i, 0)), pl.BlockSpec((spb, k, per), lambda i, j: (j, 0, 0))]
        out_specs = pl.BlockSpec((tm, spb * per), lambda i, j: (i, j))
        out_shape = jax.ShapeDtypeStruct((m, N_DEV * per), out_dtype)
    elif dims == "nt":
        m, (n, per) = a.shape[0], b.shape[1:]
        tm, tn = _tile(m, 1024), _tile(n, 1024)

        def body(a_ref, b_ref, o_ref):
            av = bf(a_ref[...])
            acc = dot(av[:, :per], bf(b_ref[0]), NT)
            for s in range(1, N_DEV):
                acc += dot(av[:, s * per:(s + 1) * per], bf(b_ref[s]), NT)
            o_ref[...] = acc.astype(o_ref.dtype)

        grid = (m // tm, n // tn)
        in_specs = [pl.BlockSpec((tm, N_DEV * per), lambda i, j: (i, 0)),
                    pl.BlockSpec((N_DEV, tn, per), lambda i, j: (0, j, 0))]
        out_specs = pl.BlockSpec((tm, tn), lambda i, j: (i, j))
        out_shape = jax.ShapeDtypeStruct((m, n), out_dtype)
    else:
        (k, m), per = a.shape, b.shape[1] // N_DEV
        spb = min(N_DEV, max(1, 1024 // per))
        tm = _tile(m, 1024)

        def body(a_ref, b_ref, o_ref):
            av = bf(a_ref[...])
            for s in range(spb):
                o_ref[s] = dot(av, bf(b_ref[:, s * per:(s + 1) * per]), TN).astype(o_ref.dtype)

        grid = (m // tm, N_DEV // spb)
        in_specs = [pl.BlockSpec((k, tm), lambda i, j: (0, i)), pl.BlockSpec((k, spb * per), lambda i, j: (0, j))]
        out_specs = pl.BlockSpec((spb, tm, per), lambda i, j: (j, i, 0))
        out_shape = jax.ShapeDtypeStruct((N_DEV, m, per), out_dtype)
    return pl.pallas_call(body, name=name, grid=grid, in_specs=in_specs, out_specs=out_specs,
                          out_shape=out_shape, compiler_params=_cp(("parallel", "parallel")))(a, b)


def _mm(a, b, dims, out_dtype, name, a_act=None, slots=False, resid=None):
    if dims == "nn":
        m, k = a.shape
        n = b.shape[-1] * (N_DEV if slots else 1)
    elif dims == "nt":
        m, k = a.shape
        n = b.shape[-2]
    else:
        k, m = a.shape
        n = b.shape[-1]
    per = (k if dims == "nt" else n) // N_DEV
    if slots and per % LANE == 0 and per <= NARROW_SLOT and k <= 2048 and a_act is None and resid is None:
        return _mm_narrow_slots(a, b, dims, out_dtype, name)
    tm = _tile(m, 1536)
    tn = _tile_slot(n // N_DEV, 512) if slots and dims != "nt" else _tile(n, 512 if resid else 1024)
    k_slot = k // N_DEV if slots and dims == "nt" else k

    def vmem_bytes(tk_):
        a_b, b_b = tm * tk_ * a.dtype.itemsize, tk_ * tn * b.dtype.itemsize
        casts = (tm * tk_ * 2 if a.dtype != BF16 else 0) + (tk_ * tn * 2 if b.dtype != BF16 else 0)
        return 2 * (a_b + b_b + tm * tn * jnp.dtype(out_dtype).itemsize) + 2 * tm * tn * 4 + casts

    tk = _tile_slot(k_slot, 1536) if slots and dims == "nt" else _tile(k, 2048)
    while vmem_bytes(tk) > MM_VMEM_BUDGET and tk % (2 * LANE) == 0:
        tk //= 2
    spk = 2 if slots and dims == "nt" and tk == k_slot and vmem_bytes(2 * tk) <= MM_VMEM_BUDGET else 1
    tk *= spk
    nk = k // tk
    per_n = (n // N_DEV) // tn if slots else 1
    per_k = (k // N_DEV) // tk if slots and spk == 1 else 1
    if dims == "tn":
        a_spec = pl.BlockSpec((tk, tm), lambda i, j, kk: (kk, i))
    else:
        a_spec = pl.BlockSpec((tm, tk), lambda i, j, kk: (i, kk))
    if dims == "nt":
        if spk > 1:
            b_spec = pl.BlockSpec((spk, tn, k_slot), lambda i, j, kk: (kk, j, 0))
        elif slots:
            b_spec = pl.BlockSpec((None, tn, tk), lambda i, j, kk: (kk // per_k, j, kk % per_k))
        else:
            b_spec = pl.BlockSpec((tn, tk), lambda i, j, kk: (j, kk))
    elif dims == "nn" and slots:
        b_spec = pl.BlockSpec((None, tk, tn), lambda i, j, kk: (j // per_n, kk, j % per_n))
    else:
        b_spec = pl.BlockSpec((tk, tn), lambda i, j, kk: (kk, j))
    if dims == "tn" and slots:
        out_spec = pl.BlockSpec((None, tm, tn), lambda i, j, kk: (j // per_n, i, j % per_n))
        out_shape = jax.ShapeDtypeStruct((N_DEV, m, n // N_DEV), out_dtype)
    else:
        out_spec = pl.BlockSpec((tm, tn), lambda i, j, kk: (i, j))
        out_shape = jax.ShapeDtypeStruct((m, n), out_dtype)
    dn = {"nn": NN, "nt": NT, "tn": TN}[dims]

    def product(a_ref, b_ref):
        av = a_ref[...]
        if a_act == "silu":
            av = av * _sigmoid(av)
        av = av.astype(BF16)
        if spk > 1:
            return sum(lax.dot_general(av[:, s * k_slot:(s + 1) * k_slot], b_ref[s].astype(BF16), dn,
                                       preferred_element_type=F32) for s in range(spk))
        return lax.dot_general(av, b_ref[...].astype(BF16), dn, preferred_element_type=F32)

    def finish(acc, rest):
        if resid is None:
            (o_ref,) = rest
            o_ref[...] = acc.astype(o_ref.dtype)
        else:
            x_ref, gt_ref, o_ref, p_ref = rest
            o_ref[...] = x_ref[...] + gt_ref[...] * acc
            p_ref[...] = acc.astype(p_ref.dtype)

    def body_one(a_ref, b_ref, *rest):
        finish(product(a_ref, b_ref), rest)

    def body_acc(a_ref, b_ref, *rest):
        acc_ref = rest[-1]
        kk = pl.program_id(2)

        @pl.when(kk == 0)
        def _():
            acc_ref[...] = jnp.zeros_like(acc_ref)

        acc_ref[...] += product(a_ref, b_ref)

        @pl.when(kk == nk - 1)
        def _():
            finish(acc_ref[...], rest[:-1])

    in_specs, operands = [a_spec, b_spec], [a, b]
    if resid is not None:
        in_specs += [out_spec, pl.BlockSpec((1, tn), lambda i, j, kk: (0, j))]
        operands += list(resid)
        out_spec, out_shape = [out_spec, out_spec], [out_shape, jax.ShapeDtypeStruct((m, n), BF16)]
    return pl.pallas_call(
        body_one if nk == 1 else body_acc, name=name, grid=(m // tm, n // tn, nk),
        in_specs=in_specs, out_specs=out_spec, out_shape=out_shape,
        scratch_shapes=[] if nk == 1 else [pltpu.VMEM((tm, tn), F32)],
        compiler_params=_cp(("parallel", "parallel", "arbitrary")),
    )(*operands)


@functools.partial(jax.custom_vjp, nondiff_argnums=(2, 3))
def mm(a, b, tag, out_dtype):
    return _mm(a, b, "nn", out_dtype, "mm_" + tag, slots=b.ndim == 3)


def _mm_f(a, b, tag, out_dtype):
    return mm(a, b, tag, out_dtype), (a, b)


def _mm_b(tag, out_dtype, res, g):
    a, b = res
    slots = b.ndim == 3
    da = _mm(g, b, "nt", a.dtype, "mm_" + tag + "_da", slots=slots)
    db = _mm(a, g, "tn", b.dtype, "mm_" + tag + "_db", slots=slots)
    return da, db


mm.defvjp(_mm_f, _mm_b)


def _norm_fwd_call(x, nw, sc, sh, name):
    t, d = x.shape
    tr = _rows(t)
    mod = sc is not None
    row = pl.BlockSpec((tr, d), lambda i: (i, 0))
    vec = pl.BlockSpec((1, d), lambda i: (0, 0))

    def body(*refs):
        if mod:
            x_ref, nw_ref, sc_ref, sh_ref, o_ref = refs
        else:
            x_ref, nw_ref, o_ref = refs
        xv = x_ref[...].astype(F32)
        r = lax.rsqrt(jnp.mean(xv * xv, axis=-1, keepdims=True) + EPS)
        y = (xv * r) * nw_ref[...]
        if mod:
            y = y * (1.0 + sc_ref[...]) + sh_ref[...]
        o_ref[...] = y.astype(o_ref.dtype)

    args = (x, nw, sc, sh) if mod else (x, nw)
    return pl.pallas_call(
        body, name=name, grid=(t // tr,),
        in_specs=[row] + [vec] * (len(args) - 1), out_specs=row,
        out_shape=jax.ShapeDtypeStruct((t, d), BF16),
        compiler_params=_cp(("parallel",)),
    )(*args)


def _norm_bwd_call(x, nw, sc, dh, name):
    t, d = x.shape
    tr = _rows(t)
    mod = sc is not None
    row = pl.BlockSpec((tr, d), lambda i: (i, 0))
    vec = pl.BlockSpec((1, d), lambda i: (0, 0))

    def body(*refs):
        if mod:
            x_ref, nw_ref, sc_ref, dh_ref, dx_ref, dnw_ref, dsc_ref, dsh_ref = refs
        else:
            x_ref, nw_ref, dh_ref, dx_ref, dnw_ref = refs
        i = pl.program_id(0)
        xv = x_ref[...].astype(F32)
        dh = dh_ref[...].astype(F32)
        r = lax.rsqrt(jnp.mean(xv * xv, axis=-1, keepdims=True) + EPS)
        y = xv * r
        a = nw_ref[...] * (1.0 + sc_ref[...]) if mod else nw_ref[...]
        dy = dh * a
        dx_ref[...] = (r * (dy - y * jnp.mean(dy * y, axis=-1, keepdims=True))).astype(dx_ref.dtype)
        da = jnp.sum(dh * y, axis=0, keepdims=True)

        @pl.when(i == 0)
        def _():
            dnw_ref[...] = jnp.zeros_like(dnw_ref)
            if mod:
                dsc_ref[...] = jnp.zeros_like(dsc_ref)
                dsh_ref[...] = jnp.zeros_like(dsh_ref)

        if mod:
            dnw_ref[...] += da * (1.0 + sc_ref[...])
            dsc_ref[...] += da * nw_ref[...]
            dsh_ref[...] += jnp.sum(dh, axis=0, keepdims=True)
        else:
            dnw_ref[...] += da

    args = (x, nw, sc, dh) if mod else (x, nw, dh)
    n_vec = 3 if mod else 1
    return pl.pallas_call(
        body, name=name, grid=(t // tr,),
        in_specs=[row] + [vec] * (len(args) - 2) + [row],
        out_specs=[row] + [vec] * n_vec,
        out_shape=[jax.ShapeDtypeStruct((t, d), x.dtype)] + [jax.ShapeDtypeStruct((1, d), F32)] * n_vec,
        compiler_params=_cp(("arbitrary",)),
    )(*args)


@functools.partial(jax.custom_vjp, nondiff_argnums=(4,))
def ada_norm(x, nw, sc, sh, tag):
    return _norm_fwd_call(x, nw, sc, sh, "adanorm_" + tag)


def _ada_norm_f(x, nw, sc, sh, tag):
    return _norm_fwd_call(x, nw, sc, sh, "adanorm_" + tag), (x, nw, sc)


def _ada_norm_b(tag, res, dh):
    x, nw, sc = res
    dx, dnw, dsc, dsh = _norm_bwd_call(x, nw, sc, dh, "adanorm_" + tag + "_bwd")
    return dx, dnw, dsc, dsh


ada_norm.defvjp(_ada_norm_f, _ada_norm_b)


@functools.partial(jax.custom_vjp, nondiff_argnums=(2,))
def rms_norm(x, nw, tag):
    return _norm_fwd_call(x, nw, None, None, "rms_" + tag)


def _rms_norm_f(x, nw, tag):
    return _norm_fwd_call(x, nw, None, None, "rms_" + tag), (x, nw)


def _rms_norm_b(tag, res, dh):
    x, nw = res
    dx, dnw = _norm_bwd_call(x, nw, None, dh, "rms_" + tag + "_bwd")
    return dx, dnw


rms_norm.defvjp(_rms_norm_f, _rms_norm_b)


def _gate_mix_fwd_call(gl, ya, yb, name):
    t, d = ya.shape
    tr = _rows(t)
    row = pl.BlockSpec((tr, d), lambda i: (i, 0))

    def body(ga_ref, gb_ref, ya_ref, yb_ref, o_ref):
        o_ref[...] = (_sigmoid(ga_ref[...].astype(F32)) * ya_ref[...].astype(F32)
                      + _sigmoid(gb_ref[...].astype(F32)) * yb_ref[...].astype(F32)).astype(o_ref.dtype)

    return pl.pallas_call(
        body, name=name, grid=(t // tr,),
        in_specs=[row, pl.BlockSpec((tr, d), lambda i: (i, 1)), row, row], out_specs=row,
        out_shape=jax.ShapeDtypeStruct((t, d), BF16),
        compiler_params=_cp(("parallel",)),
    )(gl, gl, ya, yb)


def _gate_mix_bwd_call(gl, ya, yb, dm, name):
    t, d = ya.shape
    tr = _rows(t)
    row = pl.BlockSpec((tr, d), lambda i: (i, 0))
    wide = pl.BlockSpec((tr, 2 * d), lambda i: (i, 0))

    def body(gl_ref, ya_ref, yb_ref, dm_ref, dgl_ref, dya_ref, dyb_ref):
        dm = dm_ref[...].astype(F32)
        ga = _sigmoid(gl_ref[:, :d].astype(F32))
        gb = _sigmoid(gl_ref[:, d:].astype(F32))
        dya_ref[...] = (dm * ga).astype(dya_ref.dtype)
        dyb_ref[...] = (dm * gb).astype(dyb_ref.dtype)
        dgl_ref[:, :d] = (dm * ya_ref[...].astype(F32) * ga * (1.0 - ga)).astype(dgl_ref.dtype)
        dgl_ref[:, d:] = (dm * yb_ref[...].astype(F32) * gb * (1.0 - gb)).astype(dgl_ref.dtype)

    return pl.pallas_call(
        body, name=name, grid=(t // tr,),
        in_specs=[wide, row, row, row], out_specs=[wide, row, row],
        out_shape=[jax.ShapeDtypeStruct((t, 2 * d), gl.dtype), jax.ShapeDtypeStruct((t, d), ya.dtype),
                   jax.ShapeDtypeStruct((t, d), yb.dtype)],
        compiler_params=_cp(("parallel",)),
    )(gl, ya, yb, dm)


@functools.partial(jax.custom_vjp, nondiff_argnums=(3,))
def gate_mix(gl, ya, yb, tag):
    return _gate_mix_fwd_call(gl, ya, yb, "gatemix_" + tag)


def _gate_mix_f(gl, ya, yb, tag):
    return _gate_mix_fwd_call(gl, ya, yb, "gatemix_" + tag), (gl, ya, yb)


def _gate_mix_b(tag, res, dm):
    return tuple(_gate_mix_bwd_call(*res, dm, "gatemix_" + tag + "_bwd"))


gate_mix.defvjp(_gate_mix_f, _gate_mix_b)


def _resid_bwd_call(gt, p, g, name):
    t, d = p.shape
    tr = _rows(t)
    row = pl.BlockSpec((tr, d), lambda i: (i, 0))
    vec = pl.BlockSpec((1, d), lambda i: (0, 0))

    def body(gt_ref, p_ref, g_ref, dp_ref, dgt_ref):
        i = pl.program_id(0)
        g = g_ref[...]
        dp_ref[...] = (g * gt_ref[...]).astype(dp_ref.dtype)

        @pl.when(i == 0)
        def _():
            dgt_ref[...] = jnp.zeros_like(dgt_ref)

        dgt_ref[...] += jnp.sum(g * p_ref[...].astype(F32), axis=0, keepdims=True)

    return pl.pallas_call(
        body, name=name, grid=(t // tr,), in_specs=[vec, row, row], out_specs=[row, vec],
        out_shape=[jax.ShapeDtypeStruct((t, d), BF16), jax.ShapeDtypeStruct((1, d), F32)],
        compiler_params=_cp(("arbitrary",)),
    )(gt, p, g)


@functools.partial(jax.custom_vjp, nondiff_argnums=(4,))
def mm_resid(x, gt, a, b, tag):
    return _mm(a, b, "nn", F32, "mmres_" + tag, resid=(x, gt))[0]


def _mm_resid_f(x, gt, a, b, tag):
    o, p = _mm(a, b, "nn", F32, "mmres_" + tag, resid=(x, gt))
    return o, (gt, a, b, p)


def _mm_resid_b(tag, res, g):
    gt, a, b, p = res
    dp, dgt = _resid_bwd_call(gt, p, g, "mmres_" + tag + "_gate")
    da = _mm(dp, b, "nt", a.dtype, "mmres_" + tag + "_da")
    db = _mm(a, dp, "tn", b.dtype, "mmres_" + tag + "_db")
    return g, dgt, da, db


mm_resid.defvjp(_mm_resid_f, _mm_resid_b)


def _swiglu_fwd_call(gu, name):
    t, f2 = gu.shape
    f = f2 // 2
    tr = _rows(t, 128)
    half = pl.BlockSpec((tr, f), lambda i: (i, 0))

    def body(g_ref, u_ref, o_ref):
        g = g_ref[...].astype(F32)
        o_ref[...] = (g * _sigmoid(g) * u_ref[...].astype(F32)).astype(o_ref.dtype)

    return pl.pallas_call(
        body, name=name, grid=(t // tr,),
        in_specs=[half, pl.BlockSpec((tr, f), lambda i: (i, 1))], out_specs=half,
        out_shape=jax.ShapeDtypeStruct((t, f), BF16), compiler_params=_cp(("parallel",)),
    )(gu, gu)


def _swiglu_bwd_call(gu, da, name):
    t, f2 = gu.shape
    f = f2 // 2
    tr = _rows(t, 128)
    wide = pl.BlockSpec((tr, f2), lambda i: (i, 0))

    def body(gu_ref, da_ref, dgu_ref):
        g = gu_ref[:, :f].astype(F32)
        u = gu_ref[:, f:].astype(F32)
        da = da_ref[...].astype(F32)
        s = _sigmoid(g)
        ds = da * s
        dgu_ref[:, :f] = (ds * u * (1.0 + g * (1.0 - s))).astype(dgu_ref.dtype)
        dgu_ref[:, f:] = (ds * g).astype(dgu_ref.dtype)

    return pl.pallas_call(
        body, name=name, grid=(t // tr,),
        in_specs=[wide, pl.BlockSpec((tr, f), lambda i: (i, 0))], out_specs=wide,
        out_shape=jax.ShapeDtypeStruct((t, f2), gu.dtype), compiler_params=_cp(("parallel",)),
    )(gu, da)


@functools.partial(jax.custom_vjp, nondiff_argnums=(1,))
def swiglu(gu, tag):
    return _swiglu_fwd_call(gu, "swiglu_" + tag)


def _swiglu_f(gu, tag):
    return _swiglu_fwd_call(gu, "swiglu_" + tag), (gu,)


def _swiglu_b(tag, res, da):
    return (_swiglu_bwd_call(res[0], da, "swiglu_" + tag + "_bwd"),)


swiglu.defvjp(_swiglu_f, _swiglu_b)


def loss_head(x, fw, tgt):
    t, d = x.shape
    tr = _rows(t)
    row = pl.BlockSpec((tr, d), lambda i: (i, 0))
    vec = pl.BlockSpec((1, d), lambda i: (0, 0))
    tile = pl.BlockSpec((8, LANE), lambda i: (0, 0))

    def body(x_ref, fw_ref, tgt_ref, loss_ref, dx_ref, dfw_ref):
        i = pl.program_id(0)
        xv = x_ref[...]
        fw = fw_ref[...]
        r = lax.rsqrt(jnp.mean(xv * xv, axis=-1, keepdims=True) + EPS)
        yh = xv * r
        e = yh * fw - tgt_ref[...]
        dy = e * (1.0 / d)
        dyw = dy * fw
        dx_ref[...] = r * (dyw - yh * jnp.mean(dyw * yh, axis=-1, keepdims=True))

        @pl.when(i == 0)
        def _():
            loss_ref[...] = jnp.zeros_like(loss_ref)
            dfw_ref[...] = jnp.zeros_like(dfw_ref)

        loss_ref[...] += 0.5 * jnp.sum(jnp.mean(e * e, axis=-1, keepdims=True))
        dfw_ref[...] += jnp.sum(dy * yh, axis=0, keepdims=True)

    return pl.pallas_call(
        body, name="loss_head", grid=(t // tr,), in_specs=[row, vec, row],
        out_specs=[tile, row, vec],
        out_shape=[jax.ShapeDtypeStruct((8, LANE), F32), jax.ShapeDtypeStruct((t, d), F32),
                   jax.ShapeDtypeStruct((1, d), F32)],
        compiler_params=_cp(("arbitrary",)),
    )(x, fw, tgt)


def _attn_scores(qn_ref, qr_ref, kn_ref, kr_ref, diag):
    tq = qn_ref.shape[0]
    s = lax.dot_general(qn_ref[...].astype(BF16), kn_ref[...].astype(BF16), NT, preferred_element_type=F32)
    s += lax.dot_general(qr_ref[...].astype(BF16), kr_ref[...].astype(BF16), NT, preferred_element_type=F32)
    s = s * (QK_NOPE + QK_ROPE) ** -0.5
    if diag:
        rows = lax.broadcasted_iota(jnp.int32, (tq, tq), 0)
        cols = lax.broadcasted_iota(jnp.int32, (tq, tq), 1)
        s = jnp.where(cols <= rows, s, -1e30)
    return s


def _attn_fwd_call(qn, qr, kv, kr, name):
    t = qn.shape[0]
    h_n = MLA_HEADS
    tq = _rows(t, 512)
    nq = t // tq
    assert V_HEAD == LANE and tq % LANE == 0

    def body(qn_ref, qr_ref, kn_ref, v_ref, kr_ref, o_ref, lse_ref, m_scr, l_scr, acc_scr):
        i, j = pl.program_id(1), pl.program_id(2)

        @pl.when(j == 0)
        def _():
            m_scr[...] = jnp.full_like(m_scr, -1e30)
            l_scr[...] = jnp.zeros_like(l_scr)
            acc_scr[...] = jnp.zeros_like(acc_scr)

        def step(diag):
            s = _attn_scores(qn_ref, qr_ref, kn_ref, kr_ref, diag)
            m_old = m_scr[...]
            m_new = jnp.maximum(m_old, jnp.max(s, axis=-1, keepdims=True))
            p = jnp.exp(s - jnp.tile(m_new, (1, tq // LANE)))
            alpha = jnp.exp(m_old - m_new)
            l_scr[...] = alpha * l_scr[...] + jnp.sum(p, axis=-1, keepdims=True)
            acc_scr[...] = alpha * acc_scr[...] + jnp.dot(p.astype(BF16), v_ref[...].astype(BF16),
                                                           preferred_element_type=F32)
            m_scr[...] = m_new

        @pl.when(j < i)
        def _():
            step(False)

        @pl.when(j == i)
        def _():
            step(True)
            o_ref[...] = (acc_scr[...] / l_scr[...]).astype(o_ref.dtype)
            lse_ref[...] = (m_scr[...] + jnp.log(l_scr[...]))[:, :1]

    return pl.pallas_call(
        body, name=name, grid=(h_n, nq, nq),
        in_specs=[
            pl.BlockSpec((tq, QK_NOPE), lambda h, i, j: (i, h)),
            pl.BlockSpec((None, tq, QK_ROPE), lambda h, i, j: (h, i, 0)),
            pl.BlockSpec((tq, QK_NOPE), lambda h, i, j: (jnp.minimum(j, i), 2 * h)),
            pl.BlockSpec((tq, V_HEAD), lambda h, i, j: (jnp.minimum(j, i), 2 * h + 1)),
            pl.BlockSpec((tq, QK_ROPE), lambda h, i, j: (jnp.minimum(j, i), 0)),
        ],
        out_specs=[
            pl.BlockSpec((tq, V_HEAD), lambda h, i, j: (i, h)),
            pl.BlockSpec((None, tq, 1), lambda h, i, j: (h, i, 0)),
        ],
        out_shape=[jax.ShapeDtypeStruct((t, h_n * V_HEAD), BF16),
                   jax.ShapeDtypeStruct((h_n, t, 1), F32)],
        scratch_shapes=[pltpu.VMEM((tq, LANE), F32), pltpu.VMEM((tq, LANE), F32),
                        pltpu.VMEM((tq, V_HEAD), F32)],
        compiler_params=_cp(("parallel", "parallel", "arbitrary")),
    )(qn, qr, kv, kv, kr)


def _attn_bwd_call(qn, qr, kv, kr, o, lse, do, name):
    t = qn.shape[0]
    h_n = MLA_HEADS
    tq = _rows(t, 512)
    nq = t // tq
    scale = (QK_NOPE + QK_ROPE) ** -0.5

    def body(qn_ref, qr_ref, kn_ref, v_ref, kr_ref, o_ref, lse_ref, do_ref,
             dqn_ref, dqr_ref, dkv_ref, dkr_ref, dqn_scr, dqr_scr, dkn_scr, dv_scr, dkr_scr):
        j, i = pl.program_id(1), pl.program_id(2)

        @pl.when(jnp.logical_and(j == 0, i == 0))
        def _():
            dqn_scr[...] = jnp.zeros_like(dqn_scr)
            dqr_scr[...] = jnp.zeros_like(dqr_scr)

        @pl.when(i == 0)
        def _():
            dkn_scr[...] = jnp.zeros_like(dkn_scr)
            dv_scr[...] = jnp.zeros_like(dv_scr)
            dkr_scr[...] = jnp.zeros_like(dkr_scr)

        def step(diag):
            qn_b = qn_ref[...].astype(BF16)
            qr_b = qr_ref[...].astype(BF16)
            kn_b = kn_ref[...].astype(BF16)
            kr_b = kr_ref[...].astype(BF16)
            do_b = do_ref[...]
            p = jnp.exp(_attn_scores(qn_ref, qr_ref, kn_ref, kr_ref, diag) - lse_ref[...])
            delta = jnp.sum(do_b.astype(F32) * o_ref[...].astype(F32), axis=-1, keepdims=True)
            dp = lax.dot_general(do_b, v_ref[...].astype(BF16), NT, preferred_element_type=F32)
            ds = (p * (dp - delta) * scale).astype(BF16)
            p_b = p.astype(BF16)
            dv_scr[...] += lax.dot_general(p_b, do_b, TN, preferred_element_type=F32)
            dkn_scr[...] += lax.dot_general(ds, qn_b, TN, preferred_element_type=F32)
            dkr_scr[...] += lax.dot_general(ds, qr_b, TN, preferred_element_type=F32)
            sl = pl.ds(pl.multiple_of(i * tq, tq), tq)
            dqn_scr[sl, :] += jnp.dot(ds, kn_b, preferred_element_type=F32)
            dqr_scr[sl, :] += jnp.dot(ds, kr_b, preferred_element_type=F32)

        @pl.when(i > j)
        def _():
            step(False)

        @pl.when(i == j)
        def _():
            step(True)

        @pl.when(i == nq - 1)
        def _():
            dkv_ref[:, :QK_NOPE] = dkn_scr[...].astype(dkv_ref.dtype)
            dkv_ref[:, QK_NOPE:] = dv_scr[...].astype(dkv_ref.dtype)
            dkr_ref[...] = dkr_scr[...]

        @pl.when(jnp.logical_and(j == nq - 1, i == nq - 1))
        def _():
            dqn_ref[...] = dqn_scr[...].astype(dqn_ref.dtype)
            dqr_ref[...] = dqr_scr[...].astype(dqr_ref.dtype)

    qi = lambda j, i: jnp.maximum(i, j)
    return pl.pallas_call(
        body, name=name, grid=(h_n, nq, nq),
        in_specs=[
            pl.BlockSpec((tq, QK_NOPE), lambda h, j, i: (qi(j, i), h)),
            pl.BlockSpec((None, tq, QK_ROPE), lambda h, j, i: (h, qi(j, i), 0)),
            pl.BlockSpec((tq, QK_NOPE), lambda h, j, i: (j, 2 * h)),
            pl.BlockSpec((tq, V_HEAD), lambda h, j, i: (j, 2 * h + 1)),
            pl.BlockSpec((tq, QK_ROPE), lambda h, j, i: (j, 0)),
            pl.BlockSpec((tq, V_HEAD), lambda h, j, i: (qi(j, i), h)),
            pl.BlockSpec((None, tq, 1), lambda h, j, i: (h, qi(j, i), 0)),
            pl.BlockSpec((tq, V_HEAD), lambda h, j, i: (qi(j, i), h)),
        ],
        out_specs=[
            pl.BlockSpec((t, QK_NOPE), lambda h, j, i: (0, h)),
            pl.BlockSpec((None, t, QK_ROPE), lambda h, j, i: (h, 0, 0)),
            pl.BlockSpec((tq, QK_NOPE + V_HEAD), lambda h, j, i: (j, h)),
            pl.BlockSpec((None, tq, QK_ROPE), lambda h, j, i: (h, j, 0)),
        ],
        out_shape=[jax.ShapeDtypeStruct((t, h_n * QK_NOPE), qn.dtype),
                   jax.ShapeDtypeStruct((h_n, t, QK_ROPE), qr.dtype),
                   jax.ShapeDtypeStruct((t, h_n * (QK_NOPE + V_HEAD)), kv.dtype),
                   jax.ShapeDtypeStruct((h_n, t, QK_ROPE), F32)],
        scratch_shapes=[pltpu.VMEM((t, QK_NOPE), F32), pltpu.VMEM((t, QK_ROPE), F32),
                        pltpu.VMEM((tq, QK_NOPE), F32), pltpu.VMEM((tq, V_HEAD), F32),
                        pltpu.VMEM((tq, QK_ROPE), F32)],
        compiler_params=_cp(("parallel", "arbitrary", "arbitrary")),
    )(qn, qr, kv, kv, kr, o, lse, do)


@functools.partial(jax.custom_vjp, nondiff_argnums=(4,))
def attention(qn, qr, kv, kr, tag):
    return _attn_fwd_call(qn, qr, kv, kr, "attn_" + tag)[0]


def _attention_f(qn, qr, kv, kr, tag):
    o, lse = _attn_fwd_call(qn, qr, kv, kr, "attn_" + tag)
    return o, (qn, qr, kv, kr, o, lse)


def _attention_b(tag, res, do):
    dqn, dqr, dkv, dkr_h = _attn_bwd_call(*res, do, "attn_" + tag + "_bwd")
    return dqn, dqr, dkv, jnp.sum(dkr_h, axis=0).astype(res[3].dtype)


attention.defvjp(_attention_f, _attention_b)


def _shift_down(u, s):
    if s == 0:
        return u
    t = u.shape[0]
    rolled = pltpu.roll(u, s, 0)
    return jnp.where(lax.broadcasted_iota(jnp.int32, u.shape, 0) >= s, rolled, 0.0)


def _shift_up(u, s):
    if s == 0:
        return u
    t = u.shape[0]
    rolled = pltpu.roll(u, t - s, 0)
    return jnp.where(lax.broadcasted_iota(jnp.int32, u.shape, 0) < t - s, rolled, 0.0)


def _conv_blocks(t, c3):
    p = c3 // 3
    tc = _tile(p, 512)
    per = p // tc
    return p, tc, per


def _conv_fwd_call(u, w, name):
    t, c3 = u.shape
    p, tc, per = _conv_blocks(t, c3)

    def body(u_ref, w_ref, o_ref):
        u = u_ref[...].astype(F32)
        y = jnp.zeros_like(u)
        for j in range(CONV_WIDTH):
            y = y + w_ref[j:j + 1, :] * _shift_down(u, CONV_WIDTH - 1 - j)
        o_ref[...] = y * _sigmoid(y)

    return pl.pallas_call(
        body, name=name, grid=(c3 // tc,),
        in_specs=[pl.BlockSpec((t, tc), lambda cb: (0, cb)),
                  pl.BlockSpec((CONV_WIDTH, tc), lambda cb: (0, cb))],
        out_specs=pl.BlockSpec((None, t, tc), lambda cb: (cb // per, 0, cb % per)),
        out_shape=jax.ShapeDtypeStruct((3, t, p), F32),
        compiler_params=_cp(("parallel",)),
    )(u, w)


def _conv_bwd_call(u, w, do, name):
    t, c3 = u.shape
    p, tc, per = _conv_blocks(t, c3)

    def body(u_ref, w_ref, do_ref, du_ref, dw_ref):
        u = u_ref[...].astype(F32)
        shifted = [_shift_down(u, CONV_WIDTH - 1 - j) for j in range(CONV_WIDTH)]
        y = jnp.zeros_like(u)
        for j in range(CONV_WIDTH):
            y = y + w_ref[j:j + 1, :] * shifted[j]
        s = _sigmoid(y)
        dy = do_ref[...] * s * (1.0 + y * (1.0 - s))
        du = jnp.zeros_like(u)
        for j in range(CONV_WIDTH):
            du = du + w_ref[j:j + 1, :] * _shift_up(dy, CONV_WIDTH - 1 - j)
            dw_ref[j:j + 1, :] = jnp.sum(dy * shifted[j], axis=0, keepdims=True)
        du_ref[...] = du.astype(du_ref.dtype)

    return pl.pallas_call(
        body, name=name, grid=(c3 // tc,),
        in_specs=[pl.BlockSpec((t, tc), lambda cb: (0, cb)),
                  pl.BlockSpec((CONV_WIDTH, tc), lambda cb: (0, cb)),
                  pl.BlockSpec((None, t, tc), lambda cb: (cb // per, 0, cb % per))],
        out_specs=[pl.BlockSpec((t, tc), lambda cb: (0, cb)),
                   pl.BlockSpec((CONV_WIDTH, tc), lambda cb: (0, cb))],
        out_shape=[jax.ShapeDtypeStruct((t, c3), u.dtype), jax.ShapeDtypeStruct((CONV_WIDTH, c3), F32)],
        compiler_params=_cp(("parallel",)),
    )(u, w, do)


@functools.partial(jax.custom_vjp, nondiff_argnums=(2,))
def conv_silu(u, w, tag):
    return _conv_fwd_call(u, w, "conv_" + tag)


def _conv_silu_f(u, w, tag):
    return _conv_fwd_call(u, w, "conv_" + tag), (u, w)


def _conv_silu_b(tag, res, do):
    return tuple(_conv_bwd_call(*res, do, "conv_" + tag + "_bwd"))


conv_silu.defvjp(_conv_silu_f, _conv_silu_b)


BNN = (((2,), (1,)), ((0,), (0,)))
BNT = (((2,), (2,)), ((0,), (0,)))
BTN = (((1,), (1,)), ((0,), (0,)))


def _bf16_dot(a, b, dn):
    return lax.dot_general(a.astype(BF16), b.astype(BF16), dn, preferred_element_type=F32)


def _mask_dot(mask, v, dn):
    p1 = v.astype(BF16)
    r1 = v - p1.astype(F32)
    p2 = r1.astype(BF16)
    p3 = r1 - p2.astype(F32)
    return _bf16_dot(mask, p1, dn) + (_bf16_dot(mask, p2, dn) + _bf16_dot(mask, p3, dn))


@jax.custom_vjp
def _xdot(mask, v):
    return _mask_dot(mask, v, BNN)


def _xdot_f(mask, v):
    return _mask_dot(mask, v, BNN), (mask,)


def _xdot_b(res, g):
    (mask,) = res
    return jnp.zeros_like(mask), _mask_dot(mask, g, BTN)


_xdot.defvjp(_xdot_f, _xdot_b)


def _dot3(a, b, dn):
    ah, bh = a.astype(BF16), b.astype(BF16)
    al, bl = a - ah.astype(F32), b - bh.astype(F32)
    return _bf16_dot(ah, bh, dn) + (_bf16_dot(ah, bl, dn) + _bf16_dot(al, bh, dn))


def _transposed(dn, a, b, g):
    if dn == BNN:
        return (g, b, BNT), (a, g, BTN)
    if dn == BNT:
        return (g, b, BNN), (g, a, BTN)
    return (b, g, BNT), (a, g, BNN)


@functools.partial(jax.custom_vjp, nondiff_argnums=(2,))
def _hdot(a, b, dn=BNN):
    return _dot3(a, b, dn)


def _hdot_f(a, b, dn):
    return _dot3(a, b, dn), (a, b)


def _hdot_b(dn, res, g):
    da, db = _transposed(dn, *res, g)
    return _dot3(*da), _dot3(*db)


_hdot.defvjp(_hdot_f, _hdot_b)


@functools.partial(jax.custom_vjp, nondiff_argnums=(2,))
def _bdot(a, b, dn=BNN):
    return _bf16_dot(a, b, dn)


def _bdot_f(a, b, dn):
    return _bf16_dot(a, b, dn), (a, b)


def _bdot_b(dn, res, g):
    da, db = _transposed(dn, *res, g)
    return _bf16_dot(*da), _bf16_dot(*db)


_bdot.defvjp(_bdot_f, _bdot_b)


def _gdn_chunk(q, k, v, z, bl, al, a_log, dtb, gn, s):
    b, c = q.shape[0], q.shape[1]
    ri = lax.broadcasted_iota(jnp.int32, (c, c), 0)
    ci = lax.broadcasted_iota(jnp.int32, (c, c), 1)
    lower = (ri >= ci)[None]
    strict = (ri > ci)[None]
    low_incl = jnp.broadcast_to((ri >= ci).astype(F32), (b, c, c))
    up_incl = jnp.broadcast_to((ri <= ci).astype(F32), (b, c, c))
    eye = (ri == ci).astype(F32)[None]

    q = q * lax.rsqrt(jnp.sum(q * q, axis=-1, keepdims=True) + EPS) * (GDN_DK ** -0.5)
    k = k * lax.rsqrt(jnp.sum(k * k, axis=-1, keepdims=True) + EPS)
    beta = _sigmoid(bl)
    g = -jnp.exp(a_log) * _softplus(al + dtb)
    g_w = jnp.broadcast_to(g, (b, c, LANE))
    gc = _xdot(low_incl, g_w)
    gr = _xdot(jnp.ones((b, c, c), F32), g_w[:, :, :c] * up_incl)
    diff = gc[:, :, :c] - gr
    decay = jnp.where(lower, jnp.exp(jnp.where(lower, diff, 0.0)), 0.0)
    kb = k * beta
    lmat = jnp.where(strict, _bdot(kb, k, BNT) * decay, 0.0)
    inv = eye - lmat
    pw = lmat
    for _ in range(int(math.log2(c)) - 1):
        pw = _hdot(pw, pw)
        inv = _hdot(inv, eye + pw)
    eg = jnp.exp(gc)
    u = _hdot(inv, v * beta)
    w = _hdot(inv, kb * eg)
    attn = jnp.where(lower, _bdot(q, k, BNT) * decay, 0.0)
    v_new = u - _bdot(w, s)
    o = _bdot(q * eg, s) + _bdot(attn, v_new)
    g_last = jnp.sum(g_w, axis=1, keepdims=True)
    k_dec = k * jnp.exp(g_last - gc)
    s_new = s * jnp.exp(g_last) + _bdot(k_dec, v_new, BTN)
    on = o * lax.rsqrt(jnp.mean(o * o, axis=-1, keepdims=True) + EPS) * gn
    return on * (z * _sigmoid(z)), s_new


def _head_cols(ba, first, count):
    lane = lax.broadcasted_iota(jnp.int32, ba.shape, 1)
    return jnp.stack([jnp.sum(jnp.where(lane == first + j, ba, 0.0), axis=1, keepdims=True)
                      for j in range(count)])


def _gdn_heads(q, k, v, z, ba, a_log, dtb, gn, s):
    h_n = q.shape[0]
    return _gdn_chunk(q, k, v, z, _head_cols(ba, 0, h_n), _head_cols(ba, h_n, h_n), a_log, dtb, gn, s)


def _gdn_specs(n_chunks, hb, rev):
    c = CHUNK
    nn = (lambda n: n_chunks - 1 - n) if rev else (lambda n: n)
    plane = lambda pidx: pl.BlockSpec((None, c, hb * GDN_DK), lambda hg, n: (pidx, nn(n), hg))
    assert hb == GDN_HEADS
    logits = pl.BlockSpec((c, LANE), lambda hg, n: (nn(n), 0))
    scal = pl.BlockSpec((hb, 1, 1), lambda hg, n: (hg, 0, 0))
    zspec = pl.BlockSpec((c, hb * GDN_DV), lambda hg, n: (nn(n), hg))
    gnspec = pl.BlockSpec((1, GDN_DV), lambda hg, n: (0, 0))
    sspec = pl.BlockSpec((hb, None, GDN_DK, GDN_DV), lambda hg, n: (hg, nn(n), 0, 0))
    return plane, logits, scal, zspec, gnspec, sspec


def _heads(ref, hb):
    return jnp.stack([ref[:, j * GDN_DK:(j + 1) * GDN_DK] for j in range(hb)])


def _gdn_fwd_call(qkv, z, ba, a_log, dtb, gn, name):
    t = z.shape[0]
    h_n = GDN_HEADS
    hb = h_n
    n_chunks = t // CHUNK
    plane, logits, scal, zspec, gnspec, sspec = _gdn_specs(n_chunks, hb, False)

    def body(q_ref, k_ref, v_ref, z_ref, ba_ref, a_ref, dtb_ref, gn_ref, o_ref, sall_ref, s_scr):
        n = pl.program_id(1)

        @pl.when(n == 0)
        def _():
            s_scr[...] = jnp.zeros_like(s_scr)

        s = s_scr[...]
        sall_ref[...] = s
        o, s_new = _gdn_heads(_heads(q_ref, hb), _heads(k_ref, hb), _heads(v_ref, hb),
                              _heads(z_ref, hb).astype(F32),
                              ba_ref[...], a_ref[...], dtb_ref[...], gn_ref[...], s)
        for j in range(hb):
            o_ref[:, j * GDN_DV:(j + 1) * GDN_DV] = o[j].astype(o_ref.dtype)
        s_scr[...] = s_new

    return pl.pallas_call(
        body, name=name, grid=(h_n // hb, n_chunks),
        in_specs=[plane(0), plane(1), plane(2), zspec, logits, scal, scal, gnspec],
        out_specs=[zspec, sspec],
        out_shape=[jax.ShapeDtypeStruct((t, h_n * GDN_DV), BF16),
                   jax.ShapeDtypeStruct((h_n, n_chunks, GDN_DK, GDN_DV), F32)],
        scratch_shapes=[pltpu.VMEM((hb, GDN_DK, GDN_DV), F32)],
        compiler_params=_cp(("parallel", "arbitrary")),
    )(qkv, qkv, qkv, z, ba, a_log, dtb, gn)


def _gdn_bwd_call(qkv, z, ba, a_log, dtb, gn, sall, do, name):
    t = z.shape[0]
    h_n = GDN_HEADS
    hb = h_n
    n_chunks = t // CHUNK
    c = CHUNK
    plane, logits, scal, zspec, gnspec, sspec = _gdn_specs(n_chunks, hb, True)
    dplanes = pl.BlockSpec((3, c, hb * GDN_DK), lambda hg, n: (0, n_chunks - 1 - n, hg))
    gnh = pl.BlockSpec((None, 1, GDN_DV), lambda hg, n: (hg, 0, 0))

    def body(q_ref, k_ref, v_ref, z_ref, ba_ref, a_ref, dtb_ref, gn_ref, s_ref, do_ref,
             dqkv_ref, dz_ref, dba_ref, da_ref, ddtb_ref, dgn_ref, ds_scr):
        n = pl.program_id(1)

        @pl.when(n == 0)
        def _():
            ds_scr[...] = jnp.zeros_like(ds_scr)
            da_ref[...] = jnp.zeros_like(da_ref)
            ddtb_ref[...] = jnp.zeros_like(ddtb_ref)
            dgn_ref[...] = jnp.zeros_like(dgn_ref)

        _, vjp = jax.vjp(_gdn_heads, _heads(q_ref, hb), _heads(k_ref, hb), _heads(v_ref, hb),
                         _heads(z_ref, hb).astype(F32),
                         ba_ref[...], a_ref[...], dtb_ref[...], gn_ref[...], s_ref[...])
        dq, dk, dv, dz, dba, da, ddtb, dgn, ds = vjp((_heads(do_ref, hb).astype(F32), ds_scr[...]))
        for j in range(hb):
            hs = slice(j * GDN_DK, (j + 1) * GDN_DK)
            dqkv_ref[0, :, hs] = dq[j]
            dqkv_ref[1, :, hs] = dk[j]
            dqkv_ref[2, :, hs] = dv[j]
            dz_ref[:, hs] = dz[j].astype(dz_ref.dtype)
        dba_ref[...] = dba
        da_ref[...] += da
        ddtb_ref[...] += ddtb
        dgn_ref[...] += dgn
        ds_scr[...] = ds

    return pl.pallas_call(
        body, name=name, grid=(h_n // hb, n_chunks),
        in_specs=[plane(0), plane(1), plane(2), zspec, logits, scal, scal, gnspec, sspec, zspec],
        out_specs=[dplanes, zspec, logits, scal, scal, gnh],
        out_shape=[jax.ShapeDtypeStruct((3, t, h_n * GDN_DK), F32),
                   jax.ShapeDtypeStruct((t, h_n * GDN_DV), z.dtype),
                   jax.ShapeDtypeStruct((t, LANE), F32),
                   jax.ShapeDtypeStruct((h_n, 1, 1), F32), jax.ShapeDtypeStruct((h_n, 1, 1), F32),
                   jax.ShapeDtypeStruct((h_n // hb, 1, GDN_DV), F32)],
        scratch_shapes=[pltpu.VMEM((hb, GDN_DK, GDN_DV), F32)],
        compiler_params=_cp(("parallel", "arbitrary")),
    )(qkv, qkv, qkv, z, ba, a_log, dtb, gn, sall, do)


@functools.partial(jax.custom_vjp, nondiff_argnums=(6,))
def gdn(qkv, z, ba, a_log, dtb, gn, tag):
    return _gdn_fwd_call(qkv, z, ba, a_log, dtb, gn, "gdn_" + tag)[0]


def _gdn_f(qkv, z, ba, a_log, dtb, gn, tag):
    o, sall = _gdn_fwd_call(qkv, z, ba, a_log, dtb, gn, "gdn_" + tag)
    return o, (qkv, z, ba, a_log, dtb, gn, sall)


def _gdn_b(tag, res, do):
    dqkv, dz, dba, da, ddtb, dgn_h = _gdn_bwd_call(*res, do, "gdn_" + tag + "_bwd")
    return dqkv, dz, dba, da, ddtb, jnp.sum(dgn_h, axis=0)


gdn.defvjp(_gdn_f, _gdn_b)


ADAMW_BLOCK = 384 * 1024


def adamw(w, m, v, *, parts, name):
    n_layers = len(parts)
    n_parts, r, c = parts[0].shape
    assert w.shape == (n_layers * r, c), (w.shape, parts[0].shape)
    tr = r
    for cand in (512, 256, 128, 64, 32, 16, 8):
        if r % cand == 0 and cand * c <= ADAMW_BLOCK:
            tr = cand
            break
    nb = r // tr
    blk = pl.BlockSpec((tr, c), lambda l, i: (l * nb + i, 0))
    bc1 = 1.0 - ADAM_B1 ** ADAM_STEP
    bc2 = 1.0 - ADAM_B2 ** ADAM_STEP

    def part_spec(li):
        return pl.BlockSpec((n_parts, tr, c),
                            lambda l, i: (0, jnp.where(l == li, i, jnp.where(l < li, 0, nb - 1)), 0))

    def body(*refs):
        w_ref, p_refs = refs[0], refs[1:1 + n_layers]
        m_ref, v_ref, g_ref, d_ref, mo_ref, vo_ref = refs[1 + n_layers:]
        for li in range(n_layers):
            @pl.when(pl.program_id(0) == li)
            def _(p_ref=p_refs[li]):
                g = p_ref[0].astype(F32)
                for i in range(1, n_parts):
                    g = g + p_ref[i].astype(F32)
                m2 = ADAM_B1 * m_ref[...] + (1.0 - ADAM_B1) * g
                v2 = ADAM_B2 * v_ref[...] + (1.0 - ADAM_B2) * (g * g)
                g_ref[...] = g
                mo_ref[...] = m2
                vo_ref[...] = v2
                d_ref[...] = -ADAM_LR * ((m2 / bc1) / (jnp.sqrt(v2 / bc2) + ADAM_EPS)
                                         + ADAM_WD * w_ref[...])

    return pl.pallas_call(
        body, name=name, grid=(n_layers, nb),
        in_specs=[blk] + [part_spec(li) for li in range(n_layers)] + [blk, blk],
        out_specs=[blk] * 4, out_shape=[jax.ShapeDtypeStruct(w.shape, F32)] * 4,
        compiler_params=_cp(("arbitrary", "arbitrary")),
    )(w, *parts, m, v)


_HBM = pl.BlockSpec(memory_space=pltpu.HBM)
_SEM = pl.BlockSpec(memory_space=pltpu.SEMAPHORE)
_EFFECT = pltpu.SideEffectType.DATAFLOW_SIDE_EFFECTING


def _peer(x, y, c, d):
    px = 1 - x if d & 4 else x
    py = 1 - y if d & 2 else y
    pc = 1 - c if d & 1 else c
    return (px, py, pc), 4 * px + 2 * py + pc


ALL_PEERS = (1, 2, 3, 4, 5, 6, 7)
SIBLING = 1
SAME_CORE_REMOTE = (2, 4, 6)


def copy_start(arrays, mode, carry, name):
    n = len(arrays)
    if mode == "forward":
        lands = []
    else:
        lands = [lax.empty(a.shape if mode == "scatter" else (N_DEV,) + a.shape, a.dtype) for a in arrays]
    n_in = n + len(lands) + 1

    def body(*refs):
        srcs = refs[:n]
        dsts = refs[n:2 * n] if lands else srcs
        sems = refs[n_in:n_in + 2 * n]
        x, y, c = (lax.axis_index(a) for a in AXES)
        me = 4 * x + 2 * y + c
        for k in range(n):
            if mode == "forward":
                sibling, _ = _peer(x, y, c, SIBLING)
                copies = [(srcs[k].at[_peer(x, y, c, d)[1]], dsts[k].at[_peer(x, y, c, d)[1]], sibling)
                          for d in SAME_CORE_REMOTE]
            elif mode == "gather":
                copies = [(srcs[k], dsts[k].at[me], _peer(x, y, c, d)[0]) for d in (SIBLING,) + SAME_CORE_REMOTE]
            else:
                copies = [(srcs[k].at[_peer(x, y, c, d)[1]], dsts[k].at[me], _peer(x, y, c, d)[0])
                          for d in ALL_PEERS]
            for src, dst, peer in copies:
                pltpu.make_async_remote_copy(src_ref=src, dst_ref=dst, send_sem=sems[2 * k],
                                             recv_sem=sems[2 * k + 1], device_id=peer,
                                             device_id_type=pl.DeviceIdType.MESH).start()

    operands = list(arrays) + lands + [carry]
    outs = pl.pallas_call(
        body, name=name,
        out_shape=tuple([pltpu.SemaphoreType.DMA(())] * (2 * n)
                        + [pltpu.HBM(a.shape, a.dtype) for a in operands]),
        in_specs=[_HBM] * n_in,
        out_specs=tuple([_SEM] * (2 * n) + [_HBM] * n_in),
        input_output_aliases={i: 2 * n + i for i in range(n_in)},
        compiler_params=pltpu.CompilerParams(has_side_effects=_EFFECT),
    )(*[pltpu.with_memory_space_constraint(a, pltpu.HBM) for a in operands])
    sems, thru = outs[:2 * n], outs[2 * n:-1]
    handles = [(sems[2 * k], sems[2 * k + 1], thru[k] if lands else None, thru[n + k] if lands else thru[k])
               for k in range(n)]
    return outs[-1], handles


def copy_wait(handles, n_blocks, after, name):
    n = len(handles)
    sems = [s for h in handles for s in h[:2]]
    srcs = [h[2] for h in handles if h[2] is not None]
    lands = [h[3] for h in handles]
    ns = len(srcs)

    def body(*refs):
        dsts = refs[ns:ns + n]
        sem_refs = refs[ns + n:ns + 3 * n]
        x, y, c = (lax.axis_index(a) for a in AXES)
        for k in range(n):
            blocks = dsts[k].at[pl.ds(0, n_blocks)]
            pltpu.make_async_remote_copy(
                src_ref=blocks, dst_ref=blocks, send_sem=sem_refs[2 * k], recv_sem=sem_refs[2 * k + 1],
                device_id=(x, y, c), device_id_type=pl.DeviceIdType.MESH).wait()

    outs = pl.pallas_call(
        body, name=name,
        out_shape=tuple([pltpu.HBM(a.shape, a.dtype) for a in srcs + lands]),
        in_specs=[_HBM] * (ns + n) + [_SEM] * (2 * n) + [pl.BlockSpec(memory_space=pl.ANY)],
        out_specs=tuple([_HBM] * (ns + n)),
        input_output_aliases={i: i for i in range(ns + n)},
        compiler_params=pltpu.CompilerParams(has_side_effects=_EFFECT),
    )(*srcs, *lands, *sems, after)
    return (list(outs[:ns]) if ns else [None] * n), list(outs[ns:])


def exchange(arrays, modes, name):
    n = len(arrays)
    hbm = pl.BlockSpec(memory_space=pltpu.HBM)
    out_shape = [jax.ShapeDtypeStruct(a.shape if md == "scatter" else (N_DEV,) + a.shape, a.dtype)
                 for a, md in zip(arrays, modes)]

    def body(*refs):
        ins, outs = refs[:n], refs[n:2 * n]
        send_sems, recv_sems, local_sems = refs[2 * n:]
        x, y, c = (lax.axis_index(a) for a in AXES)
        me = 4 * x + 2 * y + c

        def src(k, p):
            return ins[k].at[p] if modes[k] == "scatter" else ins[k]

        local = [pltpu.make_async_copy(src(k, me), outs[k].at[me], local_sems.at[k]) for k in range(n)]
        for cp in local:
            cp.start()
        started = []
        for d in range(1, N_DEV):
            px = 1 - x if d & 4 else x
            py = 1 - y if d & 2 else y
            pc = 1 - c if d & 1 else c
            pid = 4 * px + 2 * py + pc
            for k in range(n):
                pltpu.make_async_remote_copy(
                    src_ref=src(k, pid), dst_ref=outs[k].at[me],
                    send_sem=send_sems.at[k, d - 1], recv_sem=recv_sems.at[k, d - 1],
                    device_id=(px, py, pc), device_id_type=pl.DeviceIdType.MESH).start()
                started.append((k, d, pid, (px, py, pc)))
        for k, d, pid, peer in started:
            pltpu.make_async_remote_copy(
                src_ref=src(k, pid), dst_ref=outs[k].at[pid],
                send_sem=send_sems.at[k, d - 1], recv_sem=recv_sems.at[k, d - 1],
                device_id=peer, device_id_type=pl.DeviceIdType.MESH).wait()
        for cp in local:
            cp.wait()

    outs = pl.pallas_call(
        body, name=name, in_specs=[hbm] * n, out_specs=[hbm] * n, out_shape=out_shape,
        scratch_shapes=[pltpu.SemaphoreType.DMA((n, N_DEV - 1)), pltpu.SemaphoreType.DMA((n, N_DEV - 1)),
                        pltpu.SemaphoreType.DMA((n,))],
        compiler_params=pltpu.CompilerParams(has_side_effects=True),
    )(*arrays)
    return list(outs)


SMALL = ("b_ada", "norm_mix", "norm_ffn", "q_a_norm", "kv_a_norm", "A_log", "dt_bias", "gdn_norm",
         "final_norm")
WEIGHTS = ("w_ada", "b_ada", "norm_mix", "norm_ffn", "w_in", "q_a_norm", "kv_a_norm", "w_uq", "w_ukv",
           "w_o_mla", "conv_w", "A_log", "dt_bias", "gdn_norm", "w_o_gdn", "w_o", "w_gate_up", "w_down",
           "final_norm")


def _unslot(g):
    return g.transpose(1, 0, 2).reshape(g.shape[1], -1)


def _cols(g):
    return g if g.shape[-1] % LANE == 0 else _unslot(g)


def _stack_rows(g):
    return g.reshape(-1, g.shape[-1])


def _rope(xv, cos, sin):
    x1, x2 = jnp.split(xv, 2, axis=-1)
    return jnp.concatenate([x1 * cos - x2 * sin, x2 * cos + x1 * sin], axis=-1)


MIX_WEIGHTS = ("w_uq", "w_ukv", "w_o_mla", "w_o_gdn", "w_o")
FFN_WEIGHTS = ("w_gate_up", "w_down")


def _pad_cols(a):
    return jnp.pad(a, ((0, 0), (0, _pad_lanes(a.shape[1]) - a.shape[1])))


def _stage_in(x, mod, nm, w_in_s, tg):
    d = x.shape[1]
    hg = GDN_HEADS
    w_in = _unslot(w_in_s)
    o1 = Q_LORA + KV_LORA + QK_ROPE
    o2 = o1 + 2 * hg * GDN_DK + hg * GDN_DV
    o3 = o2 + hg * GDN_DV
    o4 = o3 + 2 * hg
    h = ada_norm(x, nm, mod[:, d:2 * d], mod[:, :d], "mix" + tg)
    return (mm(h, _pad_cols(w_in[:, :o1]), "in_a" + tg, BF16), mm(h, w_in[:, o1:o2], "in_qkv" + tg, BF16),
            mm(h, w_in[:, o2:o3], "in_z" + tg, BF16), mm(h, _pad_cols(w_in[:, o3:o4]), "in_ba" + tg, F32),
            mm(h, w_in[:, o4:o4 + 2 * d], "in_g" + tg, BF16))


def _stage_mix(x, mod, seg_a, qkv, z, ba, gl, w_uq_s, w_ukv_s, w_o_mla_s, w_o_gdn_s, w_o_s, conv_s,
               qan, kvan, a_log, dtb, gn, cos, sin, tg):
    t, d = x.shape
    hq, hg = MLA_HEADS, GDN_HEADS
    w_uq = _unslot(w_uq_s).reshape(Q_LORA, hq, QK_NOPE + QK_ROPE)
    w_uq = jnp.concatenate([w_uq[:, :, :QK_NOPE].reshape(Q_LORA, hq * QK_NOPE),
                            w_uq[:, :, QK_NOPE:].reshape(Q_LORA, hq * QK_ROPE)], axis=1)
    c_q = seg_a[:, :Q_LORA]
    c_kv = seg_a[:, Q_LORA:Q_LORA + KV_LORA]
    k_pe = seg_a[:, Q_LORA + KV_LORA:Q_LORA + KV_LORA + QK_ROPE]
    qf = mm(rms_norm(c_q, qan, "qa" + tg), w_uq, "uq" + tg, BF16)
    kvf = mm(rms_norm(c_kv, kvan, "kva" + tg), _cols(w_ukv_s), "ukv" + tg, BF16)
    qn = qf[:, :hq * QK_NOPE]
    q_pe = qf[:, hq * QK_NOPE:].astype(F32).reshape(t, hq, QK_ROPE)
    qr = _rope(q_pe, cos[:, None, :], sin[:, None, :]).transpose(1, 0, 2).astype(BF16)
    kr = _rope(k_pe.astype(F32), cos, sin).astype(BF16)
    y_a = mm(attention(qn, qr, kvf, kr, tg), _cols(w_o_mla_s), "o_mla" + tg, BF16)
    conv_w = conv_s.transpose(1, 0, 2).reshape(CONV_WIDTH, -1)
    qkv_c = conv_silu(qkv, conv_w, tg)
    o_gdn = gdn(qkv_c, z, ba, a_log.reshape(hg, 1, 1), dtb.reshape(hg, 1, 1), gn, tg)
    y_b = mm(o_gdn, _cols(w_o_gdn_s), "o_gdn" + tg, BF16)
    return mm_resid(x, mod[:, 2 * d:3 * d], gate_mix(gl, y_a, y_b, tg), _stack_rows(w_o_s), "w_o" + tg)


def _stage_ffn(x, mod, nf, w_gu_s, w_down_s, tg):
    d = x.shape[1]
    h = ada_norm(x, nf, mod[:, 4 * d:5 * d], mod[:, 3 * d:4 * d], "ffn" + tg)
    gu = mm(h, _cols(w_gu_s), "gu" + tg, BF16)
    return mm_resid(x, mod[:, 5 * d:6 * d], swiglu(gu, tg), _stack_rows(w_down_s), "down" + tg)


def _flat_row(arrs):
    v = jnp.concatenate([a.reshape(-1) for a in arrs])
    return jnp.pad(v, (0, _pad_lanes(v.shape[0]) - v.shape[0]))[None, :]


def kernel(x, c, positions, w_ada, b_ada, norm_mix, norm_ffn, w_in, q_a_norm, kv_a_norm, w_uq, w_ukv, w_o_mla, conv_w, A_log, dt_bias, gdn_norm, w_o_gdn, w_o, w_gate_up, w_down, final_norm, loss_target, m_w_ada, m_b_ada, m_norm_mix, m_norm_ffn, m_w_in, m_q_a_norm, m_kv_a_norm, m_w_uq, m_w_ukv, m_w_o_mla, m_conv_w, m_A_log, m_dt_bias, m_gdn_norm, m_w_o_gdn, m_w_o, m_w_gate_up, m_w_down, m_final_norm, v_w_ada, v_b_ada, v_norm_mix, v_norm_ffn, v_w_in, v_q_a_norm, v_kv_a_norm, v_w_uq, v_w_ukv, v_w_o_mla, v_conv_w, v_A_log, v_dt_bias, v_gdn_norm, v_w_o_gdn, v_w_o, v_w_gate_up, v_w_down, v_final_norm):
    given = dict(locals())
    t, d = x.shape[1], x.shape[2]
    n_ada = w_ada.shape[2]
    me = 4 * lax.axis_index("x") + 2 * lax.axis_index("y") + lax.axis_index("c")

    def with_own(land, own):
        return lax.dynamic_update_slice(land, own[None], (me,) + (0,) * own.ndim)

    got = exchange([c, conv_w], ["gather", "gather"], "gather_small")
    c_all, conv_g = got[0].reshape(N_DEV, d), got[1]
    c_rows = jnp.pad(c_all, ((0, 16 - N_DEV), (0, 0)))
    mod_cols = jnp.stack([_mm(c_rows, w_ada[l], "nn", F32, "ada_mod%d" % l, a_act="silu")[:N_DEV]
                          for l in range(DEPTH)], axis=1)
    mod_mine = exchange([mod_cols], ["scatter"], "scatter_mod")[0]
    mods = mod_mine.transpose(1, 0, 2).reshape(DEPTH, N_DEV * n_ada) + b_ada

    groups = [[(n, l) for n in names] for l in range(DEPTH) for names in (("w_in",), MIX_WEIGHTS, FFN_WEIGHTS)]
    gtags = [s + str(l) for l in range(DEPTH) for s in ("in", "mix", "ffn")]
    keys = [k for ks in groups for k in ks]
    mods, handles = copy_start([given[n][l].astype(BF16) for n, l in keys], "gather", mods, "gather_start")
    handles = dict(zip(keys, handles))
    own, relayed = {}, {}

    def relay(gi, carry):
        ks = groups[gi]
        srcs, lands = copy_wait([handles[k] for k in ks], 1 + len(SAME_CORE_REMOTE), carry,
                                "wait_ici_" + gtags[gi])
        own.update(zip(ks, srcs))
        carry, hs = copy_start(lands, "forward", carry, "relay_" + gtags[gi])
        relayed.update(zip(ks, hs))
        return carry

    def landed(gi, after):
        ks = groups[gi]
        _, lands = copy_wait([relayed[k] for k in ks], len(SAME_CORE_REMOTE), after, "wait_" + gtags[gi])
        return [with_own(land, own[k]) for k, land in zip(ks, lands)]

    inv_freq = 1.0 / (ROPE_THETA ** (jnp.arange(0, QK_ROPE, 2, dtype=F32) / QK_ROPE))
    ang = positions[0].astype(F32)[:, None] * inv_freq
    cos, sin = jnp.cos(ang), jnp.sin(ang)
    relay_before = {0: [0], 1: [1], 2: [2, 3], 3: [4], 4: [5], 5: []}

    def weights_for(stage, carry):
        for gi in relay_before[stage]:
            carry = relay(gi, carry)
        return carry, landed(stage, carry)

    xl = x[0]
    vjps = []
    for l in range(DEPTH):
        tg = str(l)
        mod = mods[l:l + 1]
        if l == 0:
            mods, (w_in_s,) = weights_for(0, mods)
            mod = mods[:1]
        else:
            xl, (w_in_s,) = weights_for(3 * l, xl)
        seg, vjp_in = jax.vjp(lambda *a, tg=tg: _stage_in(*a, tg), xl, mod, norm_mix[l:l + 1], w_in_s)
        seg0, w_mix = weights_for(3 * l + 1, seg[0])
        seg = (seg0,) + tuple(seg[1:])
        xm, vjp_mix = jax.vjp(lambda *a, tg=tg: _stage_mix(*a, cos, sin, tg), xl, mod, *seg, *w_mix,
                              conv_g[:, l], q_a_norm[l:l + 1], kv_a_norm[l:l + 1], A_log[l], dt_bias[l],
                              gdn_norm[l:l + 1])
        xm, w_ffn = weights_for(3 * l + 2, xm)
        xl, vjp_ffn = jax.vjp(lambda *a, tg=tg: _stage_ffn(*a, tg), xm, mod, norm_ffn[l:l + 1], *w_ffn)
        vjps.append((vjp_in, vjp_mix, vjp_ffn))

    loss_t, g, dfn = loss_head(xl, final_norm[None, :], loss_target[0])
    loss = lax.psum(loss_t[0, 0], AXES)
    dsmall = {n: [None] * DEPTH for n in SMALL + ("conv_w",)}
    dmods = [None] * DEPTH
    sent = {}

    def send(ks, grads, carry, name):
        carry, hs = copy_start(list(grads), "scatter", carry, name)
        sent.update(zip(ks, hs))
        return carry

    for l in reversed(range(DEPTH)):
        tg = str(l)
        vjp_in, vjp_mix, vjp_ffn = vjps[l]
        dxm, dmod_f, dsmall["norm_ffn"][l], *dw = vjp_ffn(g)
        dxm = send([(n, l) for n in FFN_WEIGHTS], dw, dxm, "scatter_ffn" + tg)
        dx_m, dmod_m, *rest = vjp_mix(dxm)
        dseg, dw, rest = rest[:5], rest[5:5 + len(MIX_WEIGHTS)], rest[5 + len(MIX_WEIGHTS):]
        dseg[0] = send([(n, l) for n in MIX_WEIGHTS], dw, dseg[0], "scatter_mix" + tg)
        for n, gr in zip(("conv_w", "q_a_norm", "kv_a_norm", "A_log", "dt_bias", "gdn_norm"), rest):
            dsmall[n][l] = gr
        dx_i, dmod_i, dsmall["norm_mix"][l], dw_in = vjp_in(tuple(dseg))
        g = dx_i + dx_m
        if l > 0:
            g = send([("w_in", l)], [dw_in], g, "scatter_in" + tg)
        dmods[l] = dmod_f + dmod_m + dmod_i
    dx = g
    dmods = jnp.concatenate(dmods, axis=0)
    dconv = jnp.stack(dsmall.pop("conv_w"), axis=1)
    dsmall = {n: jnp.concatenate(v, axis=0) if v[0].ndim == 2 else jnp.stack(v)
              for n, v in dsmall.items() if v[0] is not None}
    dsmall["b_ada"] = dmods
    dsmall["final_norm"] = dfn[0]

    dmod_cols = dmods.reshape(DEPTH, N_DEV, n_ada).transpose(1, 0, 2)
    conv_parts, dmod_all, small_parts = exchange(
        [dconv, dmod_cols, _flat_row([dsmall[n] for n in SMALL])], ["scatter", "scatter", "gather"],
        "exchange_small")
    dmod_all = send([("w_in", 0)], [dw_in], dmod_all, "scatter_in0")

    res = {}
    dm_rows = jnp.pad(dmod_all, ((0, 16 - N_DEV), (0, 0), (0, 0)))
    g_ada = [_mm(c_rows, dm_rows[:, l], "tn", F32, "ada_dw%d" % l, a_act="silu")[None] for l in range(DEPTH)]
    r2 = (DEPTH * d, n_ada)
    outs = adamw(w_ada.reshape(r2), m_w_ada.reshape(r2), v_w_ada.reshape(r2), parts=g_ada, name="adamw_w_ada")
    res["w_ada"] = [o.reshape(w_ada.shape) for o in outs]
    packed = SMALL + ("conv_w",)
    p_all = jnp.concatenate([small_parts, conv_parts.reshape(N_DEV, 1, -1)], axis=2)
    pack = lambda pre: jnp.concatenate([_flat_row([given[pre + n] for n in SMALL]),
                                        given[pre + "conv_w"].reshape(1, -1)], axis=1)
    outs = adamw(pack(""), pack("m_"), pack("v_"), parts=[p_all], name="adamw_small")
    done = [res["w_ada"][1], outs[1]]
    for group, gname in ((FFN_WEIGHTS, "ffn"), (MIX_WEIGHTS, "mix"), (("w_in",), "in")):
        ks = [(n, l) for l in reversed(range(DEPTH)) for n in group]
        after = sum(lax.slice(a, (0,) * a.ndim, (1,) * a.ndim).reshape(1, 1) for a in done)
        srcs, lands = copy_wait([sent[k] for k in ks], len(ALL_PEERS), after, "scatter_wait_" + gname)
        parts = {k: with_own(land, lax.dynamic_index_in_dim(src, me, 0, keepdims=False))
                 for k, src, land in zip(ks, srcs, lands)}
        for n in group:
            w = given[n]
            r2 = (w.shape[0] * w.shape[1], w.shape[2])
            res[n] = [o.reshape(w.shape) for o in
                      adamw(w.reshape(r2), given["m_" + n].reshape(r2), given["v_" + n].reshape(r2),
                            parts=[parts[(n, l)] for l in range(DEPTH)], name="adamw_" + n)]
            done.append(res[n][1])
    off = 0
    for n in packed:
        if n == "conv_w":
            off = small_parts.shape[2]
        size = math.prod(given[n].shape)
        res[n] = [o[0, off:off + size].reshape(given[n].shape) for o in outs]
        off += size

    return (loss, dx[None]) + tuple(res[n][i] for i in range(4) for n in WEIGHTS)
```

```python
import functools
import math

import jax
import jax.numpy as jnp
from jax import lax
from jax.experimental import pallas as pl
from jax.experimental.pallas import tpu as pltpu

F32 = jnp.float32
BF16 = jnp.bfloat16

MLA_HEADS = 8
QK_NOPE = 128
QK_ROPE = 64
V_HEAD = 128
Q_LORA = 512
KV_LORA = 512
ROPE_THETA = 10000.0
GDN_HEADS = 8
GDN_DK = 128
GDN_DV = 128
CONV_WIDTH = 4
CHUNK = 64
DEPTH = 2
EPS = 1e-6
ADAM_LR = 0.001
ADAM_B1 = 0.9
ADAM_B2 = 0.999
ADAM_EPS = 1e-08
ADAM_WD = 0.01
ADAM_STEP = 10

N_DEV = 8
AXES = ("x", "y", "c")
LANE = 128
VMEM_LIMIT = 48 * 1024 * 1024
MM_VMEM_BUDGET = 36 * 1024 * 1024

NN = (((1,), (0,)), ((), ()))
NT = (((1,), (1,)), ((), ()))
TN = (((0,), (0,)), ((), ()))


def _cp(sem=None):
    return pltpu.CompilerParams(dimension_semantics=sem, vmem_limit_bytes=VMEM_LIMIT)


def _tile(n, cap):
    if n <= cap:
        return n
    for t in range(cap - cap % LANE, 0, -LANE):
        if n % t == 0:
            return t
    return n


def _rows(t, cap=256):
    return cap if t % cap == 0 else t


def _pad_lanes(n):
    return -(-n // LANE) * LANE


def _sigmoid(x):
    return 1.0 / (1.0 + jnp.exp(-x))


def _softplus(x):
    return jnp.maximum(x, 0.0) + jnp.log(1.0 + jnp.exp(-jnp.abs(x)))


def _tile_slot(n, cap):
    t = _tile(n, cap)
    return n if t < 256 < n <= 1536 else t


NARROW_SLOT = 512


def _mm_narrow_slots(a, b, dims, out_dtype, name):
    bf = lambda v: v.astype(BF16)
    dot = lambda p, q, dn: lax.dot_general(p, q, dn, preferred_element_type=F32)
    if dims == "nn":
        (m, k), per = a.shape, b.shape[-1]
        spb = min(N_DEV, max(1, 1024 // per))
        tm = _tile(m, 1024)

        def body(a_ref, b_ref, o_ref):
            av = bf(a_ref[...])
            for s in range(spb):
                o_ref[:, s * per:(s + 1) * per] = dot(av, bf(b_ref[s]), NN).astype(o_ref.dtype)

        grid = (m // tm, N_DEV // spb)
        in_specs = [pl.BlockSpec((tm, k), lambda i, j: (i, 0)), pl.BlockSpec((spb, k, per), lambda i, j: (j, 0, 0))]
        out_specs = pl.BlockSpec((tm, spb * per), lambda i, j: (i, j))
        out_shape = jax.ShapeDtypeStruct((m, N_DEV * per), out_dtype)
    elif dims == "nt":
        m, (n, per) = a.shape[0], b.shape[1:]
        tm, tn = _tile(m, 1024), _tile(n, 1024)

        def body(a_ref, b_ref, o_ref):
            av = bf(a_ref[...])
            acc = dot(av[:, :per], bf(b_ref[0]), NT)
            for s in range(1, N_DEV):
                acc += dot(av[:, s * per:(s + 1) * per], bf(b_ref[s]), NT)
            o_ref[...] = acc.astype(o_ref.dtype)

        grid = (m // tm, n // tn)
        in_specs = [pl.BlockSpec((tm, N_DEV * per), lambda i, j: (i, 0)),
                    pl.BlockSpec((N_DEV, tn, per), lambda i, j: (0, j, 0))]
        out_specs = pl.BlockSpec((tm, tn), lambda i, j: (i, j))
        out_shape = jax.ShapeDtypeStruct((m, n), out_dtype)
    else:
        (k, m), per = a.shape, b.shape[1] // N_DEV
        spb = min(N_DEV, max(1, 1024 // per))
        tm = _tile(m, 1024)

        def body(a_ref, b_ref, o_ref):
            av = bf(a_ref[...])
            for s in range(spb):
                o_ref[s] = dot(av, bf(b_ref[:, s * per:(s + 1) * per]), TN).astype(o_ref.dtype)

        grid = (m // tm, N_DEV // spb)
        in_specs = [pl.BlockSpec((k, tm), lambda i, j: (0, i)), pl.BlockSpec((k, spb * per), lambda i, j: (0, j))]
        out_specs = pl.BlockSpec((spb, tm, per), lambda i, j: (j, i, 0))
        out_shape = jax.ShapeDtypeStruct((N_DEV, m, per), out_dtype)
    return pl.pallas_call(body, name=name, grid=grid, in_specs=in_specs, out_specs=out_specs,
                          out_shape=out_shape, compiler_params=_cp(("parallel", "parallel")))(a, b)


def _mm(a, b, dims, out_dtype, name, a_act=None, slots=False, resid=None):
    if dims == "nn":
        m, k = a.shape
        n = b.shape[-1] * (N_DEV if slots else 1)
    elif dims == "nt":
        m, k = a.shape
        n = b.shape[-2]
    else:
        k, m = a.shape
        n = b.shape[-1]
    per = (k if dims == "nt" else n) // N_DEV
    if slots and per % LANE == 0 and per <= NARROW_SLOT and k <= 2048 and a_act is None and resid is None:
        return _mm_narrow_slots(a, b, dims, out_dtype, name)
    tm = _tile(m, 1536)
    tn = _tile_slot(n // N_DEV, 512) if slots and dims != "nt" else _tile(n, 512 if resid else 1024)
    k_slot = k // N_DEV if slots and dims == "nt" else k

    def vmem_bytes(tk_):
        a_b, b_b = tm * tk_ * a.dtype.itemsize, tk_ * tn * b.dtype.itemsize
        casts = (tm * tk_ * 2 if a.dtype != BF16 else 0) + (tk_ * tn * 2 if b.dtype != BF16 else 0)
        return 2 * (a_b + b_b + tm * tn * jnp.dtype(out_dtype).itemsize) + 2 * tm * tn * 4 + casts

    tk = _tile_slot(k_slot, 1536) if slots and dims == "nt" else _tile(k, 2048)
    while vmem_bytes(tk) > MM_VMEM_BUDGET and tk % (2 * LANE) == 0:
        tk //= 2
    nk = k // tk
    per_n = (n // N_DEV) // tn if slots else 1
    per_k = (k // N_DEV) // tk if slots else 1
    if dims == "tn":
        a_spec = pl.BlockSpec((tk, tm), lambda i, j, kk: (kk, i))
    else:
        a_spec = pl.BlockSpec((tm, tk), lambda i, j, kk: (i, kk))
    if dims == "nt":
        if slots:
            b_spec = pl.BlockSpec((None, tn, tk), lambda i, j, kk: (kk // per_k, j, kk % per_k))
        else:
            b_spec = pl.BlockSpec((tn, tk), lambda i, j, kk: (j, kk))
    elif dims == "nn" and slots:
        b_spec = pl.BlockSpec((None, tk, tn), lambda i, j, kk: (j // per_n, kk, j % per_n))
    else:
        b_spec = pl.BlockSpec((tk, tn), lambda i, j, kk: (kk, j))
    if dims == "tn" and slots:
        out_spec = pl.BlockSpec((None, tm, tn), lambda i, j, kk: (j // per_n, i, j % per_n))
        out_shape = jax.ShapeDtypeStruct((N_DEV, m, n // N_DEV), out_dtype)
    else:
        out_spec = pl.BlockSpec((tm, tn), lambda i, j, kk: (i, j))
        out_shape = jax.ShapeDtypeStruct((m, n), out_dtype)
    dn = {"nn": NN, "nt": NT, "tn": TN}[dims]

    def product(a_ref, b_ref):
        av = a_ref[...]
        if a_act == "silu":
            av = av * _sigmoid(av)
        return lax.dot_general(av.astype(BF16), b_ref[...].astype(BF16), dn, preferred_element_type=F32)

    def finish(acc, rest):
        if resid is None:
            (o_ref,) = rest
            o_ref[...] = acc.astype(o_ref.dtype)
        else:
            x_ref, gt_ref, o_ref, p_ref = rest
            o_ref[...] = x_ref[...] + gt_ref[...] * acc
            p_ref[...] = acc.astype(p_ref.dtype)

    def body_one(a_ref, b_ref, *rest):
        finish(product(a_ref, b_ref), rest)

    def body_acc(a_ref, b_ref, *rest):
        acc_ref = rest[-1]
        kk = pl.program_id(2)

        @pl.when(kk == 0)
        def _():
            acc_ref[...] = jnp.zeros_like(acc_ref)

        acc_ref[...] += product(a_ref, b_ref)

        @pl.when(kk == nk - 1)
        def _():
            finish(acc_ref[...], rest[:-1])

    in_specs, operands = [a_spec, b_spec], [a, b]
    if resid is not None:
        in_specs += [out_spec, pl.BlockSpec((1, tn), lambda i, j, kk: (0, j))]
        operands += list(resid)
        out_spec, out_shape = [out_spec, out_spec], [out_shape, jax.ShapeDtypeStruct((m, n), BF16)]
    return pl.pallas_call(
        body_one if nk == 1 else body_acc, name=name, grid=(m // tm, n // tn, nk),
        in_specs=in_specs, out_specs=out_spec, out_shape=out_shape,
        scratch_shapes=[] if nk == 1 else [pltpu.VMEM((tm, tn), F32)],
        compiler_params=_cp(("parallel", "parallel", "arbitrary")),
    )(*operands)


@functools.partial(jax.custom_vjp, nondiff_argnums=(2, 3))
def mm(a, b, tag, out_dtype):
    return _mm(a, b, "nn", out_dtype, "mm_" + tag, slots=b.ndim == 3)


def _mm_f(a, b, tag, out_dtype):
    return mm(a, b, tag, out_dtype), (a, b)


def _mm_b(tag, out_dtype, res, g):
    a, b = res
    slots = b.ndim == 3
    da = _mm(g, b, "nt", a.dtype, "mm_" + tag + "_da", slots=slots)
    db = _mm(a, g, "tn", b.dtype, "mm_" + tag + "_db", slots=slots)
    return da, db


mm.defvjp(_mm_f, _mm_b)


def _norm_fwd_call(x, nw, sc, sh, name):
    t, d = x.shape
    tr = _rows(t)
    mod = sc is not None
    row = pl.BlockSpec((tr, d), lambda i: (i, 0))
    vec = pl.BlockSpec((1, d), lambda i: (0, 0))

    def body(*refs):
        if mod:
            x_ref, nw_ref, sc_ref, sh_ref, o_ref = refs
        else:
            x_ref, nw_ref, o_ref = refs
        xv = x_ref[...].astype(F32)
        r = lax.rsqrt(jnp.mean(xv * xv, axis=-1, keepdims=True) + EPS)
        y = (xv * r) * nw_ref[...]
        if mod:
            y = y * (1.0 + sc_ref[...]) + sh_ref[...]
        o_ref[...] = y.astype(o_ref.dtype)

    args = (x, nw, sc, sh) if mod else (x, nw)
    return pl.pallas_call(
        body, name=name, grid=(t // tr,),
        in_specs=[row] + [vec] * (len(args) - 1), out_specs=row,
        out_shape=jax.ShapeDtypeStruct((t, d), BF16),
        compiler_params=_cp(("parallel",)),
    )(*args)


def _norm_bwd_call(x, nw, sc, dh, name):
    t, d = x.shape
    tr = _rows(t)
    mod = sc is not None
    row = pl.BlockSpec((tr, d), lambda i: (i, 0))
    vec = pl.BlockSpec((1, d), lambda i: (0, 0))

    def body(*refs):
        if mod:
            x_ref, nw_ref, sc_ref, dh_ref, dx_ref, dnw_ref, dsc_ref, dsh_ref = refs
        else:
            x_ref, nw_ref, dh_ref, dx_ref, dnw_ref = refs
        i = pl.program_id(0)
        xv = x_ref[...].astype(F32)
        dh = dh_ref[...].astype(F32)
        r = lax.rsqrt(jnp.mean(xv * xv, axis=-1, keepdims=True) + EPS)
        y = xv * r
        a = nw_ref[...] * (1.0 + sc_ref[...]) if mod else nw_ref[...]
        dy = dh * a
        dx_ref[...] = (r * (dy - y * jnp.mean(dy * y, axis=-1, keepdims=True))).astype(dx_ref.dtype)
        da = jnp.sum(dh * y, axis=0, keepdims=True)

        @pl.when(i == 0)
        def _():
            dnw_ref[...] = jnp.zeros_like(dnw_ref)
            if mod:
                dsc_ref[...] = jnp.zeros_like(dsc_ref)
                dsh_ref[...] = jnp.zeros_like(dsh_ref)

        if mod:
            dnw_ref[...] += da * (1.0 + sc_ref[...])
            dsc_ref[...] += da * nw_ref[...]
            dsh_ref[...] += jnp.sum(dh, axis=0, keepdims=True)
        else:
            dnw_ref[...] += da

    args = (x, nw, sc, dh) if mod else (x, nw, dh)
    n_vec = 3 if mod else 1
    return pl.pallas_call(
        body, name=name, grid=(t // tr,),
        in_specs=[row] + [vec] * (len(args) - 2) + [row],
        out_specs=[row] + [vec] * n_vec,
        out_shape=[jax.ShapeDtypeStruct((t, d), x.dtype)] + [jax.ShapeDtypeStruct((1, d), F32)] * n_vec,
        compiler_params=_cp(("arbitrary",)),
    )(*args)


@functools.partial(jax.custom_vjp, nondiff_argnums=(4,))
def ada_norm(x, nw, sc, sh, tag):
    return _norm_fwd_call(x, nw, sc, sh, "adanorm_" + tag)


def _ada_norm_f(x, nw, sc, sh, tag):
    return _norm_fwd_call(x, nw, sc, sh, "adanorm_" + tag), (x, nw, sc)


def _ada_norm_b(tag, res, dh):
    x, nw, sc = res
    dx, dnw, dsc, dsh = _norm_bwd_call(x, nw, sc, dh, "adanorm_" + tag + "_bwd")
    return dx, dnw, dsc, dsh


ada_norm.defvjp(_ada_norm_f, _ada_norm_b)


@functools.partial(jax.custom_vjp, nondiff_argnums=(2,))
def rms_norm(x, nw, tag):
    return _norm_fwd_call(x, nw, None, None, "rms_" + tag)


def _rms_norm_f(x, nw, tag):
    return _norm_fwd_call(x, nw, None, None, "rms_" + tag), (x, nw)


def _rms_norm_b(tag, res, dh):
    x, nw = res
    dx, dnw = _norm_bwd_call(x, nw, None, dh, "rms_" + tag + "_bwd")
    return dx, dnw


rms_norm.defvjp(_rms_norm_f, _rms_norm_b)


def _gate_mix_fwd_call(gl, ya, yb, name):
    t, d = ya.shape
    tr = _rows(t)
    row = pl.BlockSpec((tr, d), lambda i: (i, 0))

    def body(ga_ref, gb_ref, ya_ref, yb_ref, o_ref):
        o_ref[...] = (_sigmoid(ga_ref[...].astype(F32)) * ya_ref[...].astype(F32)
                      + _sigmoid(gb_ref[...].astype(F32)) * yb_ref[...].astype(F32)).astype(o_ref.dtype)

    return pl.pallas_call(
        body, name=name, grid=(t // tr,),
        in_specs=[row, pl.BlockSpec((tr, d), lambda i: (i, 1)), row, row], out_specs=row,
        out_shape=jax.ShapeDtypeStruct((t, d), BF16),
        compiler_params=_cp(("parallel",)),
    )(gl, gl, ya, yb)


def _gate_mix_bwd_call(gl, ya, yb, dm, name):
    t, d = ya.shape
    tr = _rows(t)
    row = pl.BlockSpec((tr, d), lambda i: (i, 0))
    wide = pl.BlockSpec((tr, 2 * d), lambda i: (i, 0))

    def body(gl_ref, ya_ref, yb_ref, dm_ref, dgl_ref, dya_ref, dyb_ref):
        dm = dm_ref[...].astype(F32)
        ga = _sigmoid(gl_ref[:, :d].astype(F32))
        gb = _sigmoid(gl_ref[:, d:].astype(F32))
        dya_ref[...] = (dm * ga).astype(dya_ref.dtype)
        dyb_ref[...] = (dm * gb).astype(dyb_ref.dtype)
        dgl_ref[:, :d] = (dm * ya_ref[...].astype(F32) * ga * (1.0 - ga)).astype(dgl_ref.dtype)
        dgl_ref[:, d:] = (dm * yb_ref[...].astype(F32) * gb * (1.0 - gb)).astype(dgl_ref.dtype)

    return pl.pallas_call(
        body, name=name, grid=(t // tr,),
        in_specs=[wide, row, row, row], out_specs=[wide, row, row],
        out_shape=[jax.ShapeDtypeStruct((t, 2 * d), gl.dtype), jax.ShapeDtypeStruct((t, d), ya.dtype),
                   jax.ShapeDtypeStruct((t, d), yb.dtype)],
        compiler_params=_cp(("parallel",)),
    )(gl, ya, yb, dm)


@functools.partial(jax.custom_vjp, nondiff_argnums=(3,))
def gate_mix(gl, ya, yb, tag):
    return _gate_mix_fwd_call(gl, ya, yb, "gatemix_" + tag)


def _gate_mix_f(gl, ya, yb, tag):
    return _gate_mix_fwd_call(gl, ya, yb, "gatemix_" + tag), (gl, ya, yb)


def _gate_mix_b(tag, res, dm):
    return tuple(_gate_mix_bwd_call(*res, dm, "gatemix_" + tag + "_bwd"))


gate_mix.defvjp(_gate_mix_f, _gate_mix_b)


def _resid_bwd_call(gt, p, g, name):
    t, d = p.shape
    tr = _rows(t)
    row = pl.BlockSpec((tr, d), lambda i: (i, 0))
    vec = pl.BlockSpec((1, d), lambda i: (0, 0))

    def body(gt_ref, p_ref, g_ref, dp_ref, dgt_ref):
        i = pl.program_id(0)
        g = g_ref[...]
        dp_ref[...] = (g * gt_ref[...]).astype(dp_ref.dtype)

        @pl.when(i == 0)
        def _():
            dgt_ref[...] = jnp.zeros_like(dgt_ref)

        dgt_ref[...] += jnp.sum(g * p_ref[...].astype(F32), axis=0, keepdims=True)

    return pl.pallas_call(
        body, name=name, grid=(t // tr,), in_specs=[vec, row, row], out_specs=[row, vec],
        out_shape=[jax.ShapeDtypeStruct((t, d), BF16), jax.ShapeDtypeStruct((1, d), F32)],
        compiler_params=_cp(("arbitrary",)),
    )(gt, p, g)


@functools.partial(jax.custom_vjp, nondiff_argnums=(4,))
def mm_resid(x, gt, a, b, tag):
    return _mm(a, b, "nn", F32, "mmres_" + tag, resid=(x, gt))[0]


def _mm_resid_f(x, gt, a, b, tag):
    o, p = _mm(a, b, "nn", F32, "mmres_" + tag, resid=(x, gt))
    return o, (gt, a, b, p)


def _mm_resid_b(tag, res, g):
    gt, a, b, p = res
    dp, dgt = _resid_bwd_call(gt, p, g, "mmres_" + tag + "_gate")
    da = _mm(dp, b, "nt", a.dtype, "mmres_" + tag + "_da")
    db = _mm(a, dp, "tn", b.dtype, "mmres_" + tag + "_db")
    return g, dgt, da, db


mm_resid.defvjp(_mm_resid_f, _mm_resid_b)


def _swiglu_fwd_call(gu, name):
    t, f2 = gu.shape
    f = f2 // 2
    tr = _rows(t, 128)
    half = pl.BlockSpec((tr, f), lambda i: (i, 0))

    def body(g_ref, u_ref, o_ref):
        g = g_ref[...].astype(F32)
        o_ref[...] = (g * _sigmoid(g) * u_ref[...].astype(F32)).astype(o_ref.dtype)

    return pl.pallas_call(
        body, name=name, grid=(t // tr,),
        in_specs=[half, pl.BlockSpec((tr, f), lambda i: (i, 1))], out_specs=half,
        out_shape=jax.ShapeDtypeStruct((t, f), BF16), compiler_params=_cp(("parallel",)),
    )(gu, gu)


def _swiglu_bwd_call(gu, da, name):
    t, f2 = gu.shape
    f = f2 // 2
    tr = _rows(t, 128)
    wide = pl.BlockSpec((tr, f2), lambda i: (i, 0))

    def body(gu_ref, da_ref, dgu_ref):
        g = gu_ref[:, :f].astype(F32)
        u = gu_ref[:, f:].astype(F32)
        da = da_ref[...].astype(F32)
        s = _sigmoid(g)
        ds = da * s
        dgu_ref[:, :f] = (ds * u * (1.0 + g * (1.0 - s))).astype(dgu_ref.dtype)
        dgu_ref[:, f:] = (ds * g).astype(dgu_ref.dtype)

    return pl.pallas_call(
        body, name=name, grid=(t // tr,),
        in_specs=[wide, pl.BlockSpec((tr, f), lambda i: (i, 0))], out_specs=wide,
        out_shape=jax.ShapeDtypeStruct((t, f2), gu.dtype), compiler_params=_cp(("parallel",)),
    )(gu, da)


@functools.partial(jax.custom_vjp, nondiff_argnums=(1,))
def swiglu(gu, tag):
    return _swiglu_fwd_call(gu, "swiglu_" + tag)


def _swiglu_f(gu, tag):
    return _swiglu_fwd_call(gu, "swiglu_" + tag), (gu,)


def _swiglu_b(tag, res, da):
    return (_swiglu_bwd_call(res[0], da, "swiglu_" + tag + "_bwd"),)


swiglu.defvjp(_swiglu_f, _swiglu_b)


def loss_head(x, fw, tgt):
    t, d = x.shape
    tr = _rows(t)
    row = pl.BlockSpec((tr, d), lambda i: (i, 0))
    vec = pl.BlockSpec((1, d), lambda i: (0, 0))
    tile = pl.BlockSpec((8, LANE), lambda i: (0, 0))

    def body(x_ref, fw_ref, tgt_ref, loss_ref, dx_ref, dfw_ref):
        i = pl.program_id(0)
        xv = x_ref[...]
        fw = fw_ref[...]
        r = lax.rsqrt(jnp.mean(xv * xv, axis=-1, keepdims=True) + EPS)
        yh = xv * r
        e = yh * fw - tgt_ref[...]
        dy = e * (1.0 / d)
        dyw = dy * fw
        dx_ref[...] = r * (dyw - yh * jnp.mean(dyw * yh, axis=-1, keepdims=True))

        @pl.when(i == 0)
        def _():
            loss_ref[...] = jnp.zeros_like(loss_ref)
            dfw_ref[...] = jnp.zeros_like(dfw_ref)

        loss_ref[...] += 0.5 * jnp.sum(jnp.mean(e * e, axis=-1, keepdims=True))
        dfw_ref[...] += jnp.sum(dy * yh, axis=0, keepdims=True)

    return pl.pallas_call(
        body, name="loss_head", grid=(t // tr,), in_specs=[row, vec, row],
        out_specs=[tile, row, vec],
        out_shape=[jax.ShapeDtypeStruct((8, LANE), F32), jax.ShapeDtypeStruct((t, d), F32),
                   jax.ShapeDtypeStruct((1, d), F32)],
        compiler_params=_cp(("arbitrary",)),
    )(x, fw, tgt)


def _attn_scores(qn_ref, qr_ref, kn_ref, kr_ref, diag):
    tq = qn_ref.shape[0]
    s = lax.dot_general(qn_ref[...].astype(BF16), kn_ref[...].astype(BF16), NT, preferred_element_type=F32)
    s += lax.dot_general(qr_ref[...].astype(BF16), kr_ref[...].astype(BF16), NT, preferred_element_type=F32)
    s = s * (QK_NOPE + QK_ROPE) ** -0.5
    if diag:
        rows = lax.broadcasted_iota(jnp.int32, (tq, tq), 0)
        cols = lax.broadcasted_iota(jnp.int32, (tq, tq), 1)
        s = jnp.where(cols <= rows, s, -1e30)
    return s


def _attn_fwd_call(qn, qr, kv, kr, name):
    t = qn.shape[0]
    h_n = MLA_HEADS
    tq = _rows(t, 512)
    nq = t // tq
    assert V_HEAD == LANE and tq % LANE == 0

    def body(qn_ref, qr_ref, kn_ref, v_ref, kr_ref, o_ref, lse_ref, m_scr, l_scr, acc_scr):
        i, j = pl.program_id(1), pl.program_id(2)

        @pl.when(j == 0)
        def _():
            m_scr[...] = jnp.full_like(m_scr, -1e30)
            l_scr[...] = jnp.zeros_like(l_scr)
            acc_scr[...] = jnp.zeros_like(acc_scr)

        def step(diag):
            s = _attn_scores(qn_ref, qr_ref, kn_ref, kr_ref, diag)
            m_old = m_scr[...]
            m_new = jnp.maximum(m_old, jnp.max(s, axis=-1, keepdims=True))
            p = jnp.exp(s - jnp.tile(m_new, (1, tq // LANE)))
            alpha = jnp.exp(m_old - m_new)
            l_scr[...] = alpha * l_scr[...] + jnp.sum(p, axis=-1, keepdims=True)
            acc_scr[...] = alpha * acc_scr[...] + jnp.dot(p.astype(BF16), v_ref[...].astype(BF16),
                                                           preferred_element_type=F32)
            m_scr[...] = m_new

        @pl.when(j < i)
        def _():
            step(False)

        @pl.when(j == i)
        def _():
            step(True)
            o_ref[...] = (acc_scr[...] / l_scr[...]).astype(o_ref.dtype)
            lse_ref[...] = (m_scr[...] + jnp.log(l_scr[...]))[:, :1]

    return pl.pallas_call(
        body, name=name, grid=(h_n, nq, nq),
        in_specs=[
            pl.BlockSpec((tq, QK_NOPE), lambda h, i, j: (i, h)),
            pl.BlockSpec((None, tq, QK_ROPE), lambda h, i, j: (h, i, 0)),
            pl.BlockSpec((tq, QK_NOPE), lambda h, i, j: (jnp.minimum(j, i), 2 * h)),
            pl.BlockSpec((tq, V_HEAD), lambda h, i, j: (jnp.minimum(j, i), 2 * h + 1)),
            pl.BlockSpec((tq, QK_ROPE), lambda h, i, j: (jnp.minimum(j, i), 0)),
        ],
        out_specs=[
            pl.BlockSpec((tq, V_HEAD), lambda h, i, j: (i, h)),
            pl.BlockSpec((None, tq, 1), lambda h, i, j: (h, i, 0)),
        ],
        out_shape=[jax.ShapeDtypeStruct((t, h_n * V_HEAD), BF16),
                   jax.ShapeDtypeStruct((h_n, t, 1), F32)],
        scratch_shapes=[pltpu.VMEM((tq, LANE), F32), pltpu.VMEM((tq, LANE), F32),
                        pltpu.VMEM((tq, V_HEAD), F32)],
        compiler_params=_cp(("parallel", "parallel", "arbitrary")),
    )(qn, qr, kv, kv, kr)


def _attn_bwd_call(qn, qr, kv, kr, o, lse, do, name):
    t = qn.shape[0]
    h_n = MLA_HEADS
    tq = _rows(t, 512)
    nq = t // tq
    scale = (QK_NOPE + QK_ROPE) ** -0.5

    def body(qn_ref, qr_ref, kn_ref, v_ref, kr_ref, o_ref, lse_ref, do_ref,
             dqn_ref, dqr_ref, dkv_ref, dkr_ref, dqn_scr, dqr_scr, dkn_scr, dv_scr, dkr_scr):
        j, i = pl.program_id(1), pl.program_id(2)

        @pl.when(jnp.logical_and(j == 0, i == 0))
        def _():
            dqn_scr[...] = jnp.zeros_like(dqn_scr)
            dqr_scr[...] = jnp.zeros_like(dqr_scr)

        @pl.when(i == 0)
        def _():
            dkn_scr[...] = jnp.zeros_like(dkn_scr)
            dv_scr[...] = jnp.zeros_like(dv_scr)
            dkr_scr[...] = jnp.zeros_like(dkr_scr)

        def step(diag):
            qn_b = qn_ref[...].astype(BF16)
            qr_b = qr_ref[...].astype(BF16)
            kn_b = kn_ref[...].astype(BF16)
            kr_b = kr_ref[...].astype(BF16)
            do_b = do_ref[...]
            p = jnp.exp(_attn_scores(qn_ref, qr_ref, kn_ref, kr_ref, diag) - lse_ref[...])
            delta = jnp.sum(do_b.astype(F32) * o_ref[...].astype(F32), axis=-1, keepdims=True)
            dp = lax.dot_general(do_b, v_ref[...].astype(BF16), NT, preferred_element_type=F32)
            ds = (p * (dp - delta) * scale).astype(BF16)
            p_b = p.astype(BF16)
            dv_scr[...] += lax.dot_general(p_b, do_b, TN, preferred_element_type=F32)
            dkn_scr[...] += lax.dot_general(ds, qn_b, TN, preferred_element_type=F32)
            dkr_scr[...] += lax.dot_general(ds, qr_b, TN, preferred_element_type=F32)
            sl = pl.ds(pl.multiple_of(i * tq, tq), tq)
            dqn_scr[sl, :] += jnp.dot(ds, kn_b, preferred_element_type=F32)
            dqr_scr[sl, :] += jnp.dot(ds, kr_b, preferred_element_type=F32)

        @pl.when(i > j)
        def _():
            step(False)

        @pl.when(i == j)
        def _():
            step(True)

        @pl.when(i == nq - 1)
        def _():
            dkv_ref[:, :QK_NOPE] = dkn_scr[...].astype(dkv_ref.dtype)
            dkv_ref[:, QK_NOPE:] = dv_scr[...].astype(dkv_ref.dtype)
            dkr_ref[...] = dkr_scr[...]

        @pl.when(jnp.logical_and(j == nq - 1, i == nq - 1))
        def _():
            dqn_ref[...] = dqn_scr[...].astype(dqn_ref.dtype)
            dqr_ref[...] = dqr_scr[...].astype(dqr_ref.dtype)

    qi = lambda j, i: jnp.maximum(i, j)
    return pl.pallas_call(
        body, name=name, grid=(h_n, nq, nq),
        in_specs=[
            pl.BlockSpec((tq, QK_NOPE), lambda h, j, i: (qi(j, i), h)),
            pl.BlockSpec((None, tq, QK_ROPE), lambda h, j, i: (h, qi(j, i), 0)),
            pl.BlockSpec((tq, QK_NOPE), lambda h, j, i: (j, 2 * h)),
            pl.BlockSpec((tq, V_HEAD), lambda h, j, i: (j, 2 * h + 1)),
            pl.BlockSpec((tq, QK_ROPE), lambda h, j, i: (j, 0)),
            pl.BlockSpec((tq, V_HEAD), lambda h, j, i: (qi(j, i), h)),
            pl.BlockSpec((None, tq, 1), lambda h, j, i: (h, qi(j, i), 0)),
            pl.BlockSpec((tq, V_HEAD), lambda h, j, i: (qi(j, i), h)),
        ],
        out_specs=[
            pl.BlockSpec((t, QK_NOPE), lambda h, j, i: (0, h)),
            pl.BlockSpec((None, t, QK_ROPE), lambda h, j, i: (h, 0, 0)),
            pl.BlockSpec((tq, QK_NOPE + V_HEAD), lambda h, j, i: (j, h)),
            pl.BlockSpec((None, tq, QK_ROPE), lambda h, j, i: (h, j, 0)),
        ],
        out_shape=[jax.ShapeDtypeStruct((t, h_n * QK_NOPE), qn.dtype),
                   jax.ShapeDtypeStruct((h_n, t, QK_ROPE), qr.dtype),
                   jax.ShapeDtypeStruct((t, h_n * (QK_NOPE + V_HEAD)), kv.dtype),
                   jax.ShapeDtypeStruct((h_n, t, QK_ROPE), F32)],
        scratch_shapes=[pltpu.VMEM((t, QK_NOPE), F32), pltpu.VMEM((t, QK_ROPE), F32),
                        pltpu.VMEM((tq, QK_NOPE), F32), pltpu.VMEM((tq, V_HEAD), F32),
                        pltpu.VMEM((tq, QK_ROPE), F32)],
        compiler_params=_cp(("parallel", "arbitrary", "arbitrary")),
    )(qn, qr, kv, kv, kr, o, lse, do)


@functools.partial(jax.custom_vjp, nondiff_argnums=(4,))
def attention(qn, qr, kv, kr, tag):
    return _attn_fwd_call(qn, qr, kv, kr, "attn_" + tag)[0]


def _attention_f(qn, qr, kv, kr, tag):
    o, lse = _attn_fwd_call(qn, qr, kv, kr, "attn_" + tag)
    return o, (qn, qr, kv, kr, o, lse)


def _attention_b(tag, res, do):
    dqn, dqr, dkv, dkr_h = _attn_bwd_call(*res, do, "attn_" + tag + "_bwd")
    return dqn, dqr, dkv, jnp.sum(dkr_h, axis=0).astype(res[3].dtype)


attention.defvjp(_attention_f, _attention_b)


def _shift_down(u, s):
    if s == 0:
        return u
    t = u.shape[0]
    rolled = pltpu.roll(u, s, 0)
    return jnp.where(lax.broadcasted_iota(jnp.int32, u.shape, 0) >= s, rolled, 0.0)


def _shift_up(u, s):
    if s == 0:
        return u
    t = u.shape[0]
    rolled = pltpu.roll(u, t - s, 0)
    return jnp.where(lax.broadcasted_iota(jnp.int32, u.shape, 0) < t - s, rolled, 0.0)


def _conv_blocks(t, c3):
    p = c3 // 3
    tc = _tile(p, 512)
    per = p // tc
    return p, tc, per


def _conv_fwd_call(u, w, name):
    t, c3 = u.shape
    p, tc, per = _conv_blocks(t, c3)

    def body(u_ref, w_ref, o_ref):
        u = u_ref[...].astype(F32)
        y = jnp.zeros_like(u)
        for j in range(CONV_WIDTH):
            y = y + w_ref[j:j + 1, :] * _shift_down(u, CONV_WIDTH - 1 - j)
        o_ref[...] = y * _sigmoid(y)

    return pl.pallas_call(
        body, name=name, grid=(c3 // tc,),
        in_specs=[pl.BlockSpec((t, tc), lambda cb: (0, cb)),
                  pl.BlockSpec((CONV_WIDTH, tc), lambda cb: (0, cb))],
        out_specs=pl.BlockSpec((None, t, tc), lambda cb: (cb // per, 0, cb % per)),
        out_shape=jax.ShapeDtypeStruct((3, t, p), F32),
        compiler_params=_cp(("parallel",)),
    )(u, w)


def _conv_bwd_call(u, w, do, name):
    t, c3 = u.shape
    p, tc, per = _conv_blocks(t, c3)

    def body(u_ref, w_ref, do_ref, du_ref, dw_ref):
        u = u_ref[...].astype(F32)
        shifted = [_shift_down(u, CONV_WIDTH - 1 - j) for j in range(CONV_WIDTH)]
        y = jnp.zeros_like(u)
        for j in range(CONV_WIDTH):
            y = y + w_ref[j:j + 1, :] * shifted[j]
        s = _sigmoid(y)
        dy = do_ref[...] * s * (1.0 + y * (1.0 - s))
        du = jnp.zeros_like(u)
        for j in range(CONV_WIDTH):
            du = du + w_ref[j:j + 1, :] * _shift_up(dy, CONV_WIDTH - 1 - j)
            dw_ref[j:j + 1, :] = jnp.sum(dy * shifted[j], axis=0, keepdims=True)
        du_ref[...] = du.astype(du_ref.dtype)

    return pl.pallas_call(
        body, name=name, grid=(c3 // tc,),
        in_specs=[pl.BlockSpec((t, tc), lambda cb: (0, cb)),
                  pl.BlockSpec((CONV_WIDTH, tc), lambda cb: (0, cb)),
                  pl.BlockSpec((None, t, tc), lambda cb: (cb // per, 0, cb % per))],
        out_specs=[pl.BlockSpec((t, tc), lambda cb: (0, cb)),
                   pl.BlockSpec((CONV_WIDTH, tc), lambda cb: (0, cb))],
        out_shape=[jax.ShapeDtypeStruct((t, c3), u.dtype), jax.ShapeDtypeStruct((CONV_WIDTH, c3), F32)],
        compiler_params=_cp(("parallel",)),
    )(u, w, do)


@functools.partial(jax.custom_vjp, nondiff_argnums=(2,))
def conv_silu(u, w, tag):
    return _conv_fwd_call(u, w, "conv_" + tag)


def _conv_silu_f(u, w, tag):
    return _conv_fwd_call(u, w, "conv_" + tag), (u, w)


def _conv_silu_b(tag, res, do):
    return tuple(_conv_bwd_call(*res, do, "conv_" + tag + "_bwd"))


conv_silu.defvjp(_conv_silu_f, _conv_silu_b)


BNN = (((2,), (1,)), ((0,), (0,)))
BNT = (((2,), (2,)), ((0,), (0,)))
BTN = (((1,), (1,)), ((0,), (0,)))


def _bf16_dot(a, b, dn):
    return lax.dot_general(a.astype(BF16), b.astype(BF16), dn, preferred_element_type=F32)


def _mask_dot(mask, v, dn):
    p1 = v.astype(BF16)
    r1 = v - p1.astype(F32)
    p2 = r1.astype(BF16)
    p3 = r1 - p2.astype(F32)
    return _bf16_dot(mask, p1, dn) + (_bf16_dot(mask, p2, dn) + _bf16_dot(mask, p3, dn))


@jax.custom_vjp
def _xdot(mask, v):
    return _mask_dot(mask, v, BNN)


def _xdot_f(mask, v):
    return _mask_dot(mask, v, BNN), (mask,)


def _xdot_b(res, g):
    (mask,) = res
    return jnp.zeros_like(mask), _mask_dot(mask, g, BTN)


_xdot.defvjp(_xdot_f, _xdot_b)


def _dot3(a, b, dn):
    ah, bh = a.astype(BF16), b.astype(BF16)
    al, bl = a - ah.astype(F32), b - bh.astype(F32)
    return _bf16_dot(ah, bh, dn) + (_bf16_dot(ah, bl, dn) + _bf16_dot(al, bh, dn))


def _transposed(dn, a, b, g):
    if dn == BNN:
        return (g, b, BNT), (a, g, BTN)
    if dn == BNT:
        return (g, b, BNN), (g, a, BTN)
    return (b, g, BNT), (a, g, BNN)


@functools.partial(jax.custom_vjp, nondiff_argnums=(2,))
def _hdot(a, b, dn=BNN):
    return _dot3(a, b, dn)


def _hdot_f(a, b, dn):
    return _dot3(a, b, dn), (a, b)


def _hdot_b(dn, res, g):
    da, db = _transposed(dn, *res, g)
    return _dot3(*da), _dot3(*db)


_hdot.defvjp(_hdot_f, _hdot_b)


@functools.partial(jax.custom_vjp, nondiff_argnums=(2,))
def _bdot(a, b, dn=BNN):
    return _bf16_dot(a, b, dn)


def _bdot_f(a, b, dn):
    return _bf16_dot(a, b, dn), (a, b)


def _bdot_b(dn, res, g):
    da, db = _transposed(dn, *res, g)
    return _bf16_dot(*da), _bf16_dot(*db)


_bdot.defvjp(_bdot_f, _bdot_b)


def _gdn_chunk(q, k, v, z, bl, al, a_log, dtb, gn, s):
    b, c = q.shape[0], q.shape[1]
    ri = lax.broadcasted_iota(jnp.int32, (c, c), 0)
    ci = lax.broadcasted_iota(jnp.int32, (c, c), 1)
    lower = (ri >= ci)[None]
    strict = (ri > ci)[None]
    low_incl = jnp.broadcast_to((ri >= ci).astype(F32), (b, c, c))
    up_incl = jnp.broadcast_to((ri <= ci).astype(F32), (b, c, c))
    eye = (ri == ci).astype(F32)[None]

    q = q * lax.rsqrt(jnp.sum(q * q, axis=-1, keepdims=True) + EPS) * (GDN_DK ** -0.5)
    k = k * lax.rsqrt(jnp.sum(k * k, axis=-1, keepdims=True) + EPS)
    beta = _sigmoid(bl)
    g = -jnp.exp(a_log) * _softplus(al + dtb)
    g_w = jnp.broadcast_to(g, (b, c, LANE))
    gc = _xdot(low_incl, g_w)
    gr = _xdot(jnp.ones((b, c, c), F32), g_w[:, :, :c] * up_incl)
    diff = gc[:, :, :c] - gr
    decay = jnp.where(lower, jnp.exp(jnp.where(lower, diff, 0.0)), 0.0)
    kb = k * beta
    lmat = jnp.where(strict, _bdot(kb, k, BNT) * decay, 0.0)
    inv = eye - lmat
    pw = lmat
    for _ in range(int(math.log2(c)) - 1):
        pw = _hdot(pw, pw)
        inv = _hdot(inv, eye + pw)
    eg = jnp.exp(gc)
    u = _hdot(inv, v * beta)
    w = _hdot(inv, kb * eg)
    attn = jnp.where(lower, _bdot(q, k, BNT) * decay, 0.0)
    v_new = u - _bdot(w, s)
    o = _bdot(q * eg, s) + _bdot(attn, v_new)
    g_last = jnp.sum(g_w, axis=1, keepdims=True)
    k_dec = k * jnp.exp(g_last - gc)
    s_new = s * jnp.exp(g_last) + _bdot(k_dec, v_new, BTN)
    on = o * lax.rsqrt(jnp.mean(o * o, axis=-1, keepdims=True) + EPS) * gn
    return on * (z * _sigmoid(z)), s_new


def _head_cols(ba, first, count):
    lane = lax.broadcasted_iota(jnp.int32, ba.shape, 1)
    return jnp.stack([jnp.sum(jnp.where(lane == first + j, ba, 0.0), axis=1, keepdims=True)
                      for j in range(count)])


def _gdn_heads(q, k, v, z, ba, a_log, dtb, gn, s):
    h_n = q.shape[0]
    return _gdn_chunk(q, k, v, z, _head_cols(ba, 0, h_n), _head_cols(ba, h_n, h_n), a_log, dtb, gn, s)


def _gdn_specs(n_chunks, hb, rev):
    c = CHUNK
    nn = (lambda n: n_chunks - 1 - n) if rev else (lambda n: n)
    plane = lambda pidx: pl.BlockSpec((None, c, hb * GDN_DK), lambda hg, n: (pidx, nn(n), hg))
    assert hb == GDN_HEADS
    logits = pl.BlockSpec((c, LANE), lambda hg, n: (nn(n), 0))
    scal = pl.BlockSpec((hb, 1, 1), lambda hg, n: (hg, 0, 0))
    zspec = pl.BlockSpec((c, hb * GDN_DV), lambda hg, n: (nn(n), hg))
    gnspec = pl.BlockSpec((1, GDN_DV), lambda hg, n: (0, 0))
    sspec = pl.BlockSpec((hb, None, GDN_DK, GDN_DV), lambda hg, n: (hg, nn(n), 0, 0))
    return plane, logits, scal, zspec, gnspec, sspec


def _heads(ref, hb):
    return jnp.stack([ref[:, j * GDN_DK:(j + 1) * GDN_DK] for j in range(hb)])


def _gdn_fwd_call(qkv, z, ba, a_log, dtb, gn, name):
    t = z.shape[0]
    h_n = GDN_HEADS
    hb = h_n
    n_chunks = t // CHUNK
    plane, logits, scal, zspec, gnspec, sspec = _gdn_specs(n_chunks, hb, False)

    def body(q_ref, k_ref, v_ref, z_ref, ba_ref, a_ref, dtb_ref, gn_ref, o_ref, sall_ref, s_scr):
        n = pl.program_id(1)

        @pl.when(n == 0)
        def _():
            s_scr[...] = jnp.zeros_like(s_scr)

        s = s_scr[...]
        sall_ref[...] = s
        o, s_new = _gdn_heads(_heads(q_ref, hb), _heads(k_ref, hb), _heads(v_ref, hb),
                              _heads(z_ref, hb).astype(F32),
                              ba_ref[...], a_ref[...], dtb_ref[...], gn_ref[...], s)
        for j in range(hb):
            o_ref[:, j * GDN_DV:(j + 1) * GDN_DV] = o[j].astype(o_ref.dtype)
        s_scr[...] = s_new

    return pl.pallas_call(
        body, name=name, grid=(h_n // hb, n_chunks),
        in_specs=[plane(0), plane(1), plane(2), zspec, logits, scal, scal, gnspec],
        out_specs=[zspec, sspec],
        out_shape=[jax.ShapeDtypeStruct((t, h_n * GDN_DV), BF16),
                   jax.ShapeDtypeStruct((h_n, n_chunks, GDN_DK, GDN_DV), F32)],
        scratch_shapes=[pltpu.VMEM((hb, GDN_DK, GDN_DV), F32)],
        compiler_params=_cp(("parallel", "arbitrary")),
    )(qkv, qkv, qkv, z, ba, a_log, dtb, gn)


def _gdn_bwd_call(qkv, z, ba, a_log, dtb, gn, sall, do, name):
    t = z.shape[0]
    h_n = GDN_HEADS
    hb = h_n
    n_chunks = t // CHUNK
    c = CHUNK
    plane, logits, scal, zspec, gnspec, sspec = _gdn_specs(n_chunks, hb, True)
    dplanes = pl.BlockSpec((3, c, hb * GDN_DK), lambda hg, n: (0, n_chunks - 1 - n, hg))
    gnh = pl.BlockSpec((None, 1, GDN_DV), lambda hg, n: (hg, 0, 0))

    def body(q_ref, k_ref, v_ref, z_ref, ba_ref, a_ref, dtb_ref, gn_ref, s_ref, do_ref,
             dqkv_ref, dz_ref, dba_ref, da_ref, ddtb_ref, dgn_ref, ds_scr):
        n = pl.program_id(1)

        @pl.when(n == 0)
        def _():
            ds_scr[...] = jnp.zeros_like(ds_scr)
            da_ref[...] = jnp.zeros_like(da_ref)
            ddtb_ref[...] = jnp.zeros_like(ddtb_ref)
            dgn_ref[...] = jnp.zeros_like(dgn_ref)

        _, vjp = jax.vjp(_gdn_heads, _heads(q_ref, hb), _heads(k_ref, hb), _heads(v_ref, hb),
                         _heads(z_ref, hb).astype(F32),
                         ba_ref[...], a_ref[...], dtb_ref[...], gn_ref[...], s_ref[...])
        dq, dk, dv, dz, dba, da, ddtb, dgn, ds = vjp((_heads(do_ref, hb).astype(F32), ds_scr[...]))
        for j in range(hb):
            hs = slice(j * GDN_DK, (j + 1) * GDN_DK)
            dqkv_ref[0, :, hs] = dq[j]
            dqkv_ref[1, :, hs] = dk[j]
            dqkv_ref[2, :, hs] = dv[j]
            dz_ref[:, hs] = dz[j].astype(dz_ref.dtype)
        dba_ref[...] = dba
        da_ref[...] += da
        ddtb_ref[...] += ddtb
        dgn_ref[...] += dgn
        ds_scr[...] = ds

    return pl.pallas_call(
        body, name=name, grid=(h_n // hb, n_chunks),
        in_specs=[plane(0), plane(1), plane(2), zspec, logits, scal, scal, gnspec, sspec, zspec],
        out_specs=[dplanes, zspec, logits, scal, scal, gnh],
        out_shape=[jax.ShapeDtypeStruct((3, t, h_n * GDN_DK), F32),
                   jax.ShapeDtypeStruct((t, h_n * GDN_DV), z.dtype),
                   jax.ShapeDtypeStruct((t, LANE), F32),
                   jax.ShapeDtypeStruct((h_n, 1, 1), F32), jax.ShapeDtypeStruct((h_n, 1, 1), F32),
                   jax.ShapeDtypeStruct((h_n // hb, 1, GDN_DV), F32)],
        scratch_shapes=[pltpu.VMEM((hb, GDN_DK, GDN_DV), F32)],
        compiler_params=_cp(("parallel", "arbitrary")),
    )(qkv, qkv, qkv, z, ba, a_log, dtb, gn, sall, do)


@functools.partial(jax.custom_vjp, nondiff_argnums=(6,))
def gdn(qkv, z, ba, a_log, dtb, gn, tag):
    return _gdn_fwd_call(qkv, z, ba, a_log, dtb, gn, "gdn_" + tag)[0]


def _gdn_f(qkv, z, ba, a_log, dtb, gn, tag):
    o, sall = _gdn_fwd_call(qkv, z, ba, a_log, dtb, gn, "gdn_" + tag)
    return o, (qkv, z, ba, a_log, dtb, gn, sall)


def _gdn_b(tag, res, do):
    dqkv, dz, dba, da, ddtb, dgn_h = _gdn_bwd_call(*res, do, "gdn_" + tag + "_bwd")
    return dqkv, dz, dba, da, ddtb, jnp.sum(dgn_h, axis=0)


gdn.defvjp(_gdn_f, _gdn_b)


ADAMW_BLOCK = 384 * 1024


def adamw(w, m, v, *, parts, name):
    n_layers = len(parts)
    n_parts, r, c = parts[0].shape
    assert w.shape == (n_layers * r, c), (w.shape, parts[0].shape)
    tr = r
    for cand in (512, 256, 128, 64, 32, 16, 8):
        if r % cand == 0 and cand * c <= ADAMW_BLOCK:
            tr = cand
            break
    nb = r // tr
    blk = pl.BlockSpec((tr, c), lambda l, i: (l * nb + i, 0))
    bc1 = 1.0 - ADAM_B1 ** ADAM_STEP
    bc2 = 1.0 - ADAM_B2 ** ADAM_STEP

    def part_spec(li):
        return pl.BlockSpec((n_parts, tr, c),
                            lambda l, i: (0, jnp.where(l == li, i, jnp.where(l < li, 0, nb - 1)), 0))

    def body(*refs):
        w_ref, p_refs = refs[0], refs[1:1 + n_layers]
        m_ref, v_ref, g_ref, d_ref, mo_ref, vo_ref = refs[1 + n_layers:]
        for li in range(n_layers):
            @pl.when(pl.program_id(0) == li)
            def _(p_ref=p_refs[li]):
                g = p_ref[0].astype(F32)
                for i in range(1, n_parts):
                    g = g + p_ref[i].astype(F32)
                m2 = ADAM_B1 * m_ref[...] + (1.0 - ADAM_B1) * g
                v2 = ADAM_B2 * v_ref[...] + (1.0 - ADAM_B2) * (g * g)
                g_ref[...] = g
                mo_ref[...] = m2
                vo_ref[...] = v2
                d_ref[...] = -ADAM_LR * ((m2 / bc1) / (jnp.sqrt(v2 / bc2) + ADAM_EPS)
                                         + ADAM_WD * w_ref[...])

    return pl.pallas_call(
        body, name=name, grid=(n_layers, nb),
        in_specs=[blk] + [part_spec(li) for li in range(n_layers)] + [blk, blk],
        out_specs=[blk] * 4, out_shape=[jax.ShapeDtypeStruct(w.shape, F32)] * 4,
        compiler_params=_cp(("arbitrary", "arbitrary")),
    )(w, *parts, m, v)


_HBM = pl.BlockSpec(memory_space=pltpu.HBM)
_SEM = pl.BlockSpec(memory_space=pltpu.SEMAPHORE)
_EFFECT = pltpu.SideEffectType.DATAFLOW_SIDE_EFFECTING


def _peer(x, y, c, d):
    px = 1 - x if d & 4 else x
    py = 1 - y if d & 2 else y
    pc = 1 - c if d & 1 else c
    return (px, py, pc), 4 * px + 2 * py + pc


ALL_PEERS = (1, 2, 3, 4, 5, 6, 7)
SIBLING = 1
SAME_CORE_REMOTE = (2, 4, 6)


def copy_start(arrays, mode, carry, name):
    n = len(arrays)
    if mode == "forward":
        lands = []
    else:
        lands = [lax.empty(a.shape if mode == "scatter" else (N_DEV,) + a.shape, a.dtype) for a in arrays]
    n_in = n + len(lands) + 1

    def body(*refs):
        srcs = refs[:n]
        dsts = refs[n:2 * n] if lands else srcs
        sems = refs[n_in:n_in + 2 * n]
        x, y, c = (lax.axis_index(a) for a in AXES)
        me = 4 * x + 2 * y + c
        for k in range(n):
            if mode == "forward":
                sibling, _ = _peer(x, y, c, SIBLING)
                copies = [(srcs[k].at[_peer(x, y, c, d)[1]], dsts[k].at[_peer(x, y, c, d)[1]], sibling)
                          for d in SAME_CORE_REMOTE]
            elif mode == "gather":
                copies = [(srcs[k], dsts[k].at[me], _peer(x, y, c, d)[0]) for d in (SIBLING,) + SAME_CORE_REMOTE]
            else:
                copies = [(srcs[k].at[_peer(x, y, c, d)[1]], dsts[k].at[me], _peer(x, y, c, d)[0])
                          for d in ALL_PEERS]
            for src, dst, peer in copies:
                pltpu.make_async_remote_copy(src_ref=src, dst_ref=dst, send_sem=sems[2 * k],
                                             recv_sem=sems[2 * k + 1], device_id=peer,
                                             device_id_type=pl.DeviceIdType.MESH).start()

    operands = list(arrays) + lands + [carry]
    outs = pl.pallas_call(
        body, name=name,
        out_shape=tuple([pltpu.SemaphoreType.DMA(())] * (2 * n)
                        + [pltpu.HBM(a.shape, a.dtype) for a in operands]),
        in_specs=[_HBM] * n_in,
        out_specs=tuple([_SEM] * (2 * n) + [_HBM] * n_in),
        input_output_aliases={i: 2 * n + i for i in range(n_in)},
        compiler_params=pltpu.CompilerParams(has_side_effects=_EFFECT),
    )(*[pltpu.with_memory_space_constraint(a, pltpu.HBM) for a in operands])
    sems, thru = outs[:2 * n], outs[2 * n:-1]
    handles = [(sems[2 * k], sems[2 * k + 1], thru[k] if lands else None, thru[n + k] if lands else thru[k])
               for k in range(n)]
    return outs[-1], handles


def copy_wait(handles, n_blocks, after, name):
    n = len(handles)
    sems = [s for h in handles for s in h[:2]]
    srcs = [h[2] for h in handles if h[2] is not None]
    lands = [h[3] for h in handles]
    ns = len(srcs)

    def body(*refs):
        dsts = refs[ns:ns + n]
        sem_refs = refs[ns + n:ns + 3 * n]
        x, y, c = (lax.axis_index(a) for a in AXES)
        for k in range(n):
            blocks = dsts[k].at[pl.ds(0, n_blocks)]
            pltpu.make_async_remote_copy(
                src_ref=blocks, dst_ref=blocks, send_sem=sem_refs[2 * k], recv_sem=sem_refs[2 * k + 1],
                device_id=(x, y, c), device_id_type=pl.DeviceIdType.MESH).wait()

    outs = pl.pallas_call(
        body, name=name,
        out_shape=tuple([pltpu.HBM(a.shape, a.dtype) for a in srcs + lands]),
        in_specs=[_HBM] * (ns + n) + [_SEM] * (2 * n) + [pl.BlockSpec(memory_space=pl.ANY)],
        out_specs=tuple([_HBM] * (ns + n)),
        input_output_aliases={i: i for i in range(ns + n)},
        compiler_params=pltpu.CompilerParams(has_side_effects=_EFFECT),
    )(*srcs, *lands, *sems, after)
    return (list(outs[:ns]) if ns else [None] * n), list(outs[ns:])


def exchange(arrays, modes, name):
    n = len(arrays)
    hbm = pl.BlockSpec(memory_space=pltpu.HBM)
    out_shape = [jax.ShapeDtypeStruct(a.shape if md == "scatter" else (N_DEV,) + a.shape, a.dtype)
                 for a, md in zip(arrays, modes)]

    def body(*refs):
        ins, outs = refs[:n], refs[n:2 * n]
        send_sems, recv_sems, local_sems = refs[2 * n:]
        x, y, c = (lax.axis_index(a) for a in AXES)
        me = 4 * x + 2 * y + c

        def src(k, p):
            return ins[k].at[p] if modes[k] == "scatter" else ins[k]

        local = [pltpu.make_async_copy(src(k, me), outs[k].at[me], local_sems.at[k]) for k in range(n)]
        for cp in local:
            cp.start()
        started = []
        for d in range(1, N_DEV):
            px = 1 - x if d & 4 else x
            py = 1 - y if d & 2 else y
            pc = 1 - c if d & 1 else c
            pid = 4 * px + 2 * py + pc
            for k in range(n):
                pltpu.make_async_remote_copy(
                    src_ref=src(k, pid), dst_ref=outs[k].at[me],
                    send_sem=send_sems.at[k, d - 1], recv_sem=recv_sems.at[k, d - 1],
                    device_id=(px, py, pc), device_id_type=pl.DeviceIdType.MESH).start()
                started.append((k, d, pid, (px, py, pc)))
        for k, d, pid, peer in started:
            pltpu.make_async_remote_copy(
                src_ref=src(k, pid), dst_ref=outs[k].at[pid],
                send_sem=send_sems.at[k, d - 1], recv_sem=recv_sems.at[k, d - 1],
                device_id=peer, device_id_type=pl.DeviceIdType.MESH).wait()
        for cp in local:
            cp.wait()

    outs = pl.pallas_call(
        body, name=name, in_specs=[hbm] * n, out_specs=[hbm] * n, out_shape=out_shape,
        scratch_shapes=[pltpu.SemaphoreType.DMA((n, N_DEV - 1)), pltpu.SemaphoreType.DMA((n, N_DEV - 1)),
                        pltpu.SemaphoreType.DMA((n,))],
        compiler_params=pltpu.CompilerParams(has_side_effects=True),
    )(*arrays)
    return list(outs)


SMALL = ("b_ada", "norm_mix", "norm_ffn", "q_a_norm", "kv_a_norm", "A_log", "dt_bias", "gdn_norm",
         "final_norm")
WEIGHTS = ("w_ada", "b_ada", "norm_mix", "norm_ffn", "w_in", "q_a_norm", "kv_a_norm", "w_uq", "w_ukv",
           "w_o_mla", "conv_w", "A_log", "dt_bias", "gdn_norm", "w_o_gdn", "w_o", "w_gate_up", "w_down",
           "final_norm")


def _unslot(g):
    return g.transpose(1, 0, 2).reshape(g.shape[1], -1)


def _cols(g):
    return g if g.shape[-1] % LANE == 0 else _unslot(g)


def _stack_rows(g):
    return g.reshape(-1, g.shape[-1])


def _rope(xv, cos, sin):
    x1, x2 = jnp.split(xv, 2, axis=-1)
    return jnp.concatenate([x1 * cos - x2 * sin, x2 * cos + x1 * sin], axis=-1)


MIX_WEIGHTS = ("w_uq", "w_ukv", "w_o_mla", "w_o_gdn", "w_o")
FFN_WEIGHTS = ("w_gate_up", "w_down")


def _pad_cols(a):
    return jnp.pad(a, ((0, 0), (0, _pad_lanes(a.shape[1]) - a.shape[1])))


def _stage_in(x, mod, nm, w_in_s, tg):
    d = x.shape[1]
    hg = GDN_HEADS
    w_in = _unslot(w_in_s)
    o1 = Q_LORA + KV_LORA + QK_ROPE
    o2 = o1 + 2 * hg * GDN_DK + hg * GDN_DV
    o3 = o2 + hg * GDN_DV
    o4 = o3 + 2 * hg
    h = ada_norm(x, nm, mod[:, d:2 * d], mod[:, :d], "mix" + tg)
    return (mm(h, _pad_cols(w_in[:, :o1]), "in_a" + tg, BF16), mm(h, w_in[:, o1:o2], "in_qkv" + tg, BF16),
            mm(h, w_in[:, o2:o3], "in_z" + tg, BF16), mm(h, _pad_cols(w_in[:, o3:o4]), "in_ba" + tg, F32),
            mm(h, w_in[:, o4:o4 + 2 * d], "in_g" + tg, BF16))


def _stage_mix(x, mod, seg_a, qkv, z, ba, gl, w_uq_s, w_ukv_s, w_o_mla_s, w_o_gdn_s, w_o_s, conv_s,
               qan, kvan, a_log, dtb, gn, cos, sin, tg):
    t, d = x.shape
    hq, hg = MLA_HEADS, GDN_HEADS
    w_uq = _unslot(w_uq_s).reshape(Q_LORA, hq, QK_NOPE + QK_ROPE)
    w_uq = jnp.concatenate([w_uq[:, :, :QK_NOPE].reshape(Q_LORA, hq * QK_NOPE),
                            w_uq[:, :, QK_NOPE:].reshape(Q_LORA, hq * QK_ROPE)], axis=1)
    c_q = seg_a[:, :Q_LORA]
    c_kv = seg_a[:, Q_LORA:Q_LORA + KV_LORA]
    k_pe = seg_a[:, Q_LORA + KV_LORA:Q_LORA + KV_LORA + QK_ROPE]
    qf = mm(rms_norm(c_q, qan, "qa" + tg), w_uq, "uq" + tg, BF16)
    kvf = mm(rms_norm(c_kv, kvan, "kva" + tg), _cols(w_ukv_s), "ukv" + tg, BF16)
    qn = qf[:, :hq * QK_NOPE]
    q_pe = qf[:, hq * QK_NOPE:].astype(F32).reshape(t, hq, QK_ROPE)
    qr = _rope(q_pe, cos[:, None, :], sin[:, None, :]).transpose(1, 0, 2).astype(BF16)
    kr = _rope(k_pe.astype(F32), cos, sin).astype(BF16)
    y_a = mm(attention(qn, qr, kvf, kr, tg), _cols(w_o_mla_s), "o_mla" + tg, BF16)
    conv_w = conv_s.transpose(1, 0, 2).reshape(CONV_WIDTH, -1)
    qkv_c = conv_silu(qkv, conv_w, tg)
    o_gdn = gdn(qkv_c, z, ba, a_log.reshape(hg, 1, 1), dtb.reshape(hg, 1, 1), gn, tg)
    y_b = mm(o_gdn, _cols(w_o_gdn_s), "o_gdn" + tg, BF16)
    return mm_resid(x, mod[:, 2 * d:3 * d], gate_mix(gl, y_a, y_b, tg), _stack_rows(w_o_s), "w_o" + tg)


def _stage_ffn(x, mod, nf, w_gu_s, w_down_s, tg):
    d = x.shape[1]
    h = ada_norm(x, nf, mod[:, 4 * d:5 * d], mod[:, 3 * d:4 * d], "ffn" + tg)
    gu = mm(h, _cols(w_gu_s), "gu" + tg, BF16)
    return mm_resid(x, mod[:, 5 * d:6 * d], swiglu(gu, tg), _stack_rows(w_down_s), "down" + tg)


def _flat_row(arrs):
    v = jnp.concatenate([a.reshape(-1) for a in arrs])
    return jnp.pad(v, (0, _pad_lanes(v.shape[0]) - v.shape[0]))[None, :]


def kernel(x, c, positions, w_ada, b_ada, norm_mix, norm_ffn, w_in, q_a_norm, kv_a_norm, w_uq, w_ukv, w_o_mla, conv_w, A_log, dt_bias, gdn_norm, w_o_gdn, w_o, w_gate_up, w_down, final_norm, loss_target, m_w_ada, m_b_ada, m_norm_mix, m_norm_ffn, m_w_in, m_q_a_norm, m_kv_a_norm, m_w_uq, m_w_ukv, m_w_o_mla, m_conv_w, m_A_log, m_dt_bias, m_gdn_norm, m_w_o_gdn, m_w_o, m_w_gate_up, m_w_down, m_final_norm, v_w_ada, v_b_ada, v_norm_mix, v_norm_ffn, v_w_in, v_q_a_norm, v_kv_a_norm, v_w_uq, v_w_ukv, v_w_o_mla, v_conv_w, v_A_log, v_dt_bias, v_gdn_norm, v_w_o_gdn, v_w_o, v_w_gate_up, v_w_down, v_final_norm):
    given = dict(locals())
    t, d = x.shape[1], x.shape[2]
    n_ada = w_ada.shape[2]
    me = 4 * lax.axis_index("x") + 2 * lax.axis_index("y") + lax.axis_index("c")

    def with_own(land, own):
        return lax.dynamic_update_slice(land, own[None], (me,) + (0,) * own.ndim)

    groups = [[(n, l) for n in names] for l in range(DEPTH) for names in (("w_in",), MIX_WEIGHTS, FFN_WEIGHTS)]
    gtags = [s + str(l) for l in range(DEPTH) for s in ("in", "mix", "ffn")]
    first, rest = groups[0], [k for ks in groups[1:] for k in ks]
    shard = lambda k: given[k[0]][k[1]].astype(BF16)
    got = exchange([c, conv_w], ["gather", "gather"], "gather_small")
    c_all, conv_g = got[0].reshape(N_DEV, d), got[1]
    c_all, handles_first = copy_start([shard(k) for k in first], "gather", c_all, "gather_start_first")
    c_rows = jnp.pad(c_all, ((0, 16 - N_DEV), (0, 0)))
    mod_cols = jnp.stack([_mm(c_rows, w_ada[l], "nn", F32, "ada_mod%d" % l, a_act="silu")[:N_DEV]
                          for l in range(DEPTH)], axis=1)
    mod_mine = exchange([mod_cols], ["scatter"], "scatter_mod")[0]
    mods = mod_mine.transpose(1, 0, 2).reshape(DEPTH, N_DEV * n_ada) + b_ada

    mods, handles = copy_start([shard(k) for k in rest], "gather", mods, "gather_start")
    handles = dict(zip(first + rest, handles_first + handles))
    own, relayed = {}, {}

    def relay(gi, carry):
        ks = groups[gi]
        srcs, lands = copy_wait([handles[k] for k in ks], 1 + len(SAME_CORE_REMOTE), carry,
                                "wait_ici_" + gtags[gi])
        own.update(zip(ks, srcs))
        carry, hs = copy_start(lands, "forward", carry, "relay_" + gtags[gi])
        relayed.update(zip(ks, hs))
        return carry

    def landed(gi, after):
        ks = groups[gi]
        _, lands = copy_wait([relayed[k] for k in ks], len(SAME_CORE_REMOTE), after, "wait_" + gtags[gi])
        return [with_own(land, own[k]) for k, land in zip(ks, lands)]

    inv_freq = 1.0 / (ROPE_THETA ** (jnp.arange(0, QK_ROPE, 2, dtype=F32) / QK_ROPE))
    ang = positions[0].astype(F32)[:, None] * inv_freq
    cos, sin = jnp.cos(ang), jnp.sin(ang)
    relay_before = {0: [0], 1: [1], 2: [2, 3], 3: [4], 4: [5], 5: []}

    def weights_for(stage, carry):
        for gi in relay_before[stage]:
            carry = relay(gi, carry)
        return carry, landed(stage, carry)

    xl = x[0]
    vjps = []
    for l in range(DEPTH):
        tg = str(l)
        mod = mods[l:l + 1]
        if l == 0:
            mods, (w_in_s,) = weights_for(0, mods)
            mod = mods[:1]
        else:
            xl, (w_in_s,) = weights_for(3 * l, xl)
        seg, vjp_in = jax.vjp(lambda *a, tg=tg: _stage_in(*a, tg), xl, mod, norm_mix[l:l + 1], w_in_s)
        seg0, w_mix = weights_for(3 * l + 1, seg[0])
        seg = (seg0,) + tuple(seg[1:])
        xm, vjp_mix = jax.vjp(lambda *a, tg=tg: _stage_mix(*a, cos, sin, tg), xl, mod, *seg, *w_mix,
                              conv_g[:, l], q_a_norm[l:l + 1], kv_a_norm[l:l + 1], A_log[l], dt_bias[l],
                              gdn_norm[l:l + 1])
        xm, w_ffn = weights_for(3 * l + 2, xm)
        xl, vjp_ffn = jax.vjp(lambda *a, tg=tg: _stage_ffn(*a, tg), xm, mod, norm_ffn[l:l + 1], *w_ffn)
        vjps.append((vjp_in, vjp_mix, vjp_ffn))

    loss_t, g, dfn = loss_head(xl, final_norm[None, :], loss_target[0])
    loss = lax.psum(loss_t[0, 0], AXES)
    dsmall = {n: [None] * DEPTH for n in SMALL + ("conv_w",)}
    dmods = [None] * DEPTH
    sent = {}

    def send(ks, grads, carry, name):
        carry, hs = copy_start(list(grads), "scatter", carry, name)
        sent.update(zip(ks, hs))
        return carry

    for l in reversed(range(DEPTH)):
        tg = str(l)
        vjp_in, vjp_mix, vjp_ffn = vjps[l]
        dxm, dmod_f, dsmall["norm_ffn"][l], *dw = vjp_ffn(g)
        dxm = send([(n, l) for n in FFN_WEIGHTS], dw, dxm, "scatter_ffn" + tg)
        dx_m, dmod_m, *rest = vjp_mix(dxm)
        dseg, dw, rest = rest[:5], rest[5:5 + len(MIX_WEIGHTS)], rest[5 + len(MIX_WEIGHTS):]
        dseg[0] = send([(n, l) for n in MIX_WEIGHTS], dw, dseg[0], "scatter_mix" + tg)
        for n, gr in zip(("conv_w", "q_a_norm", "kv_a_norm", "A_log", "dt_bias", "gdn_norm"), rest):
            dsmall[n][l] = gr
        dx_i, dmod_i, dsmall["norm_mix"][l], dw_in = vjp_in(tuple(dseg))
        g = dx_i + dx_m
        if l > 0:
            g = send([("w_in", l)], [dw_in], g, "scatter_in" + tg)
        dmods[l] = dmod_f + dmod_m + dmod_i
    dx = g
    dmods = jnp.concatenate(dmods, axis=0)
    dconv = jnp.stack(dsmall.pop("conv_w"), axis=1)
    dsmall = {n: jnp.concatenate(v, axis=0) if v[0].ndim == 2 else jnp.stack(v)
              for n, v in dsmall.items() if v[0] is not None}
    dsmall["b_ada"] = dmods
    dsmall["final_norm"] = dfn[0]

    dmod_cols = dmods.reshape(DEPTH, N_DEV, n_ada).transpose(1, 0, 2)
    conv_parts, dmod_all, small_parts = exchange(
        [dconv, dmod_cols, _flat_row([dsmall[n] for n in SMALL])], ["scatter", "scatter", "gather"],
        "exchange_small")
    dmod_all = send([("w_in", 0)], [dw_in], dmod_all, "scatter_in0")

    res = {}
    dm_rows = jnp.pad(dmod_all, ((0, 16 - N_DEV), (0, 0), (0, 0)))
    g_ada = [_mm(c_rows, dm_rows[:, l], "tn", F32, "ada_dw%d" % l, a_act="silu")[None] for l in range(DEPTH)]
    r2 = (DEPTH * d, n_ada)
    outs = adamw(w_ada.reshape(r2), m_w_ada.reshape(r2), v_w_ada.reshape(r2), parts=g_ada, name="adamw_w_ada")
    res["w_ada"] = [o.reshape(w_ada.shape) for o in outs]
    packed = SMALL + ("conv_w",)
    p_all = jnp.concatenate([small_parts, conv_parts.reshape(N_DEV, 1, -1)], axis=2)
    pack = lambda pre: jnp.concatenate([_flat_row([given[pre + n] for n in SMALL]),
                                        given[pre + "conv_w"].reshape(1, -1)], axis=1)
    outs = adamw(pack(""), pack("m_"), pack("v_"), parts=[p_all], name="adamw_small")
    done = [res["w_ada"][1], outs[1]]
    for group, gname in ((FFN_WEIGHTS, "ffn"), (MIX_WEIGHTS, "mix"), (("w_in",), "in")):
        ks = [(n, l) for l in reversed(range(DEPTH)) for n in group]
        after = sum(lax.slice(a, (0,) * a.ndim, (1,) * a.ndim).reshape(1, 1) for a in done)
        srcs, lands = copy_wait([sent[k] for k in ks], len(ALL_PEERS), after, "scatter_wait_" + gname)
        parts = {k: with_own(land, lax.dynamic_index_in_dim(src, me, 0, keepdims=False))
                 for k, src, land in zip(ks, srcs, lands)}
        for n in group:
            w = given[n]
            r2 = (w.shape[0] * w.shape[1], w.shape[2])
            res[n] = [o.reshape(w.shape) for o in
                      adamw(w.reshape(r2), given["m_" + n].reshape(r2), given["v_" + n].reshape(r2),
                            parts=[parts[(n, l)] for l in range(DEPTH)], name="adamw_" + n)]
            done.append(res[n][1])
    off = 0
    for n in packed:
        if n == "conv_w":
            off = small_parts.shape[2]
        size = math.prod(given[n].shape)
        res[n] = [o[0, off:off + size].reshape(given[n].shape) for o in outs]
        off += size

    return (loss, dx[None]) + tuple(res[n][i] for i in range(4) for n in WEIGHTS)
```

```python
import functools
import math

import jax
import jax.numpy as jnp
from jax import lax
from jax.experimental import pallas as pl
from jax.experimental.pallas import tpu as pltpu

F32 = jnp.float32
BF16 = jnp.bfloat16

MLA_HEADS = 8
QK_NOPE = 128
QK_ROPE = 64
V_HEAD = 128
Q_LORA = 512
KV_LORA = 512
ROPE_THETA = 10000.0
GDN_HEADS = 8
GDN_DK = 128
GDN_DV = 128
CONV_WIDTH = 4
CHUNK = 64
DEPTH = 2
EPS = 1e-6
ADAM_LR = 0.001
ADAM_B1 = 0.9
ADAM_B2 = 0.999
ADAM_EPS = 1e-08
ADAM_WD = 0.01
ADAM_STEP = 10

N_DEV = 8
AXES = ("x", "y", "c")
LANE = 128
VMEM_LIMIT = 48 * 1024 * 1024
MM_VMEM_BUDGET = 36 * 1024 * 1024

NN = (((1,), (0,)), ((), ()))
NT = (((1,), (1,)), ((), ()))
TN = (((0,), (0,)), ((), ()))


def _cp(sem=None):
    return pltpu.CompilerParams(dimension_semantics=sem, vmem_limit_bytes=VMEM_LIMIT)


def _tile(n, cap):
    if n <= cap:
        return n
    for t in range(cap - cap % LANE, 0, -LANE):
        if n % t == 0:
            return t
    return n


def _rows(t, cap=256):
    return cap if t % cap == 0 else t


def _pad_lanes(n):
    return -(-n // LANE) * LANE


def _sigmoid(x):
    return 1.0 / (1.0 + jnp.exp(-x))


def _softplus(x):
    return jnp.maximum(x, 0.0) + jnp.log(1.0 + jnp.exp(-jnp.abs(x)))


def _tile_slot(n, cap):
    t = _tile(n, cap)
    return n if t < 256 < n <= 1536 else t


NARROW_SLOT = 512


def _mm_narrow_slots(a, b, dims, out_dtype, name):
    bf = lambda v: v.astype(BF16)
    dot = lambda p, q, dn: lax.dot_general(p, q, dn, preferred_element_type=F32)
    if dims == "nn":
        (m, k), per = a.shape, b.shape[-1]
        spb = min(N_DEV, max(1, 1024 // per))
        tm = _tile(m, 1024)

        def body(a_ref, b_ref, o_ref):
            av = bf(a_ref[...])
            for s in range(spb):
                o_ref[:, s * per:(s + 1) * per] = dot(av, bf(b_ref[s]), NN).astype(o_ref.dtype)

        grid = (m // tm, N_DEV // spb)
        in_specs = [pl.BlockSpec((tm, k), lambda i, j: (i, 0)), pl.BlockSpec((spb, k, per), lambda i, j: (j, 0, 0))]
        out_specs = pl.BlockSpec((tm, spb * per), lambda i, j: (i, j))
        out_shape = jax.ShapeDtypeStruct((m, N_DEV * per), out_dtype)
    elif dims == "nt":
        m, (n, per) = a.shape[0], b.shape[1:]
        tm, tn = _tile(m, 1024), _tile(n, 1024)

        def body(a_ref, b_ref, o_ref):
            av = bf(a_ref[...])
            acc = dot(av[:, :per], bf(b_ref[0]), NT)
            for s in range(1, N_DEV):
                acc += dot(av[:, s * per:(s + 1) * per], bf(b_ref[s]), NT)
            o_ref[...] = acc.astype(o_ref.dtype)

        grid = (m // tm, n // tn)
        in_specs = [pl.BlockSpec((tm, N_DEV * per), lambda i, j: (i, 0)),
                    pl.BlockSpec((N_DEV, tn, per), lambda i, j: (0, j, 0))]
        out_specs = pl.BlockSpec((tm, tn), lambda i, j: (i, j))
        out_shape = jax.ShapeDtypeStruct((m, n), out_dtype)
    else:
        (k, m), per = a.shape, b.shape[1] // N_DEV
        spb = min(N_DEV, max(1, 1024 // per))
        tm = _tile(m, 1024)

        def body(a_ref, b_ref, o_ref):
            av = bf(a_ref[...])
            for s in range(spb):
                o_ref[s] = dot(av, bf(b_ref[:, s * per:(s + 1) * per]), TN).astype(o_ref.dtype)

        grid = (m // tm, N_DEV // spb)
        in_specs = [pl.BlockSpec((k, tm), lambda i, j: (0, i)), pl.BlockSpec((k, spb * per), lambda i, j: (0, j))]
        out_specs = pl.BlockSpec((spb, tm, per), lambda i, j: (j, i, 0))
        out_shape = jax.ShapeDtypeStruct((N_DEV, m, per), out_dtype)
    return pl.pallas_call(body, name=name, grid=grid, in_specs=in_specs, out_specs=out_specs,
                          out_shape=out_shape, compiler_params=_cp(("parallel", "parallel")))(a, b)


def _mm(a, b, dims, out_dtype, name, a_act=None, slots=False, resid=None):
    if dims == "nn":
        m, k = a.shape
        n = b.shape[-1] * (N_DEV if slots else 1)
    elif dims == "nt":
        m, k = a.shape
        n = b.shape[-2]
    else:
        k, m = a.shape
        n = b.shape[-1]
    per = (k if dims == "nt" else n) // N_DEV
    if slots and per % LANE == 0 and per <= NARROW_SLOT and k <= 2048 and a_act is None and resid is None:
        return _mm_narrow_slots(a, b, dims, out_dtype, name)
    tm = _tile(m, 1536)
    tn = _tile_slot(n // N_DEV, 512) if slots and dims != "nt" else _tile(n, 512 if resid else 1024)
    k_slot = k // N_DEV if slots and dims == "nt" else k

    def vmem_bytes(tk_):
        a_b, b_b = tm * tk_ * a.dtype.itemsize, tk_ * tn * b.dtype.itemsize
        casts = (tm * tk_ * 2 if a.dtype != BF16 else 0) + (tk_ * tn * 2 if b.dtype != BF16 else 0)
        return 2 * (a_b + b_b + tm * tn * jnp.dtype(out_dtype).itemsize) + 2 * tm * tn * 4 + casts

    tk = _tile_slot(k_slot, 1536) if slots and dims == "nt" else _tile(k, 2048)
    while vmem_bytes(tk) > MM_VMEM_BUDGET and tk % (2 * LANE) == 0:
        tk //= 2
    nk = k // tk
    per_n = (n // N_DEV) // tn if slots else 1
    per_k = (k // N_DEV) // tk if slots else 1
    if dims == "tn":
        a_spec = pl.BlockSpec((tk, tm), lambda i, j, kk: (kk, i))
    else:
        a_spec = pl.BlockSpec((tm, tk), lambda i, j, kk: (i, kk))
    if dims == "nt":
        if slots:
            b_spec = pl.BlockSpec((None, tn, tk), lambda i, j, kk: (kk // per_k, j, kk % per_k))
        else:
            b_spec = pl.BlockSpec((tn, tk), lambda i, j, kk: (j, kk))
    elif dims == "nn" and slots:
        b_spec = pl.BlockSpec((None, tk, tn), lambda i, j, kk: (j // per_n, kk, j % per_n))
    else:
        b_spec = pl.BlockSpec((tk, tn), lambda i, j, kk: (kk, j))
    if dims == "tn" and slots:
        out_spec = pl.BlockSpec((None, tm, tn), lambda i, j, kk: (j // per_n, i, j % per_n))
        out_shape = jax.ShapeDtypeStruct((N_DEV, m, n // N_DEV), out_dtype)
    else:
        out_spec = pl.BlockSpec((tm, tn), lambda i, j, kk: (i, j))
        out_shape = jax.ShapeDtypeStruct((m, n), out_dtype)
    dn = {"nn": NN, "nt": NT, "tn": TN}[dims]

    def product(a_ref, b_ref):
        av = a_ref[...]
        if a_act == "silu":
            av = av * _sigmoid(av)
        return lax.dot_general(av.astype(BF16), b_ref[...].astype(BF16), dn, preferred_element_type=F32)

    def finish(acc, rest):
        if resid is None:
            (o_ref,) = rest
            o_ref[...] = acc.astype(o_ref.dtype)
        else:
            x_ref, gt_ref, o_ref, p_ref = rest
            o_ref[...] = x_ref[...] + gt_ref[...] * acc
            p_ref[...] = acc.astype(p_ref.dtype)

    def body_one(a_ref, b_ref, *rest):
        finish(product(a_ref, b_ref), rest)

    def body_acc(a_ref, b_ref, *rest):
        acc_ref = rest[-1]
        kk = pl.program_id(2)

        @pl.when(kk == 0)
        def _():
            acc_ref[...] = jnp.zeros_like(acc_ref)

        acc_ref[...] += product(a_ref, b_ref)

        @pl.when(kk == nk - 1)
        def _():
            finish(acc_ref[...], rest[:-1])

    in_specs, operands = [a_spec, b_spec], [a, b]
    if resid is not None:
        in_specs += [out_spec, pl.BlockSpec((1, tn), lambda i, j, kk: (0, j))]
        operands += list(resid)
        out_spec, out_shape = [out_spec, out_spec], [out_shape, jax.ShapeDtypeStruct((m, n), BF16)]
    return pl.pallas_call(
        body_one if nk == 1 else body_acc, name=name, grid=(m // tm, n // tn, nk),
        in_specs=in_specs, out_specs=out_spec, out_shape=out_shape,
        scratch_shapes=[] if nk == 1 else [pltpu.VMEM((tm, tn), F32)],
        compiler_params=_cp(("parallel", "parallel", "arbitrary")),
    )(*operands)


@functools.partial(jax.custom_vjp, nondiff_argnums=(2, 3))
def mm(a, b, tag, out_dtype):
    return _mm(a, b, "nn", out_dtype, "mm_" + tag, slots=b.ndim == 3)


def _mm_f(a, b, tag, out_dtype):
    return mm(a, b, tag, out_dtype), (a, b)


def _mm_b(tag, out_dtype, res, g):
    a, b = res
    slots = b.ndim == 3
    da = _mm(g, b, "nt", a.dtype, "mm_" + tag + "_da", slots=slots)
    db = _mm(a, g, "tn", b.dtype, "mm_" + tag + "_db", slots=slots)
    return da, db


mm.defvjp(_mm_f, _mm_b)


def _norm_fwd_call(x, nw, sc, sh, name):
    t, d = x.shape
    tr = _rows(t)
    mod = sc is not None
    row = pl.BlockSpec((tr, d), lambda i: (i, 0))
    vec = pl.BlockSpec((1, d), lambda i: (0, 0))

    def body(*refs):
        if mod:
            x_ref, nw_ref, sc_ref, sh_ref, o_ref = refs
        else:
            x_ref, nw_ref, o_ref = refs
        xv = x_ref[...].astype(F32)
        r = lax.rsqrt(jnp.mean(xv * xv, axis=-1, keepdims=True) + EPS)
        y = (xv * r) * nw_ref[...]
        if mod:
            y = y * (1.0 + sc_ref[...]) + sh_ref[...]
        o_ref[...] = y.astype(o_ref.dtype)

    args = (x, nw, sc, sh) if mod else (x, nw)
    return pl.pallas_call(
        body, name=name, grid=(t // tr,),
        in_specs=[row] + [vec] * (len(args) - 1), out_specs=row,
        out_shape=jax.ShapeDtypeStruct((t, d), BF16),
        compiler_params=_cp(("parallel",)),
    )(*args)


def _norm_bwd_call(x, nw, sc, dh, name):
    t, d = x.shape
    tr = _rows(t)
    mod = sc is not None
    row = pl.BlockSpec((tr, d), lambda i: (i, 0))
    vec = pl.BlockSpec((1, d), lambda i: (0, 0))

    def body(*refs):
        if mod:
            x_ref, nw_ref, sc_ref, dh_ref, dx_ref, dnw_ref, dsc_ref, dsh_ref = refs
        else:
            x_ref, nw_ref, dh_ref, dx_ref, dnw_ref = refs
        i = pl.program_id(0)
        xv = x_ref[...].astype(F32)
        dh = dh_ref[...].astype(F32)
        r = lax.rsqrt(jnp.mean(xv * xv, axis=-1, keepdims=True) + EPS)
        y = xv * r
        a = nw_ref[...] * (1.0 + sc_ref[...]) if mod else nw_ref[...]
        dy = dh * a
        dx_ref[...] = (r * (dy - y * jnp.mean(dy * y, axis=-1, keepdims=True))).astype(dx_ref.dtype)
        da = jnp.sum(dh * y, axis=0, keepdims=True)

        @pl.when(i == 0)
        def _():
            dnw_ref[...] = jnp.zeros_like(dnw_ref)
            if mod:
                dsc_ref[...] = jnp.zeros_like(dsc_ref)
                dsh_ref[...] = jnp.zeros_like(dsh_ref)

        if mod:
            dnw_ref[...] += da * (1.0 + sc_ref[...])
            dsc_ref[...] += da * nw_ref[...]
            dsh_ref[...] += jnp.sum(dh, axis=0, keepdims=True)
        else:
            dnw_ref[...] += da

    args = (x, nw, sc, dh) if mod else (x, nw, dh)
    n_vec = 3 if mod else 1
    return pl.pallas_call(
        body, name=name, grid=(t // tr,),
        in_specs=[row] + [vec] * (len(args) - 2) + [row],
        out_specs=[row] + [vec] * n_vec,
        out_shape=[jax.ShapeDtypeStruct((t, d), x.dtype)] + [jax.ShapeDtypeStruct((1, d), F32)] * n_vec,
        compiler_params=_cp(("arbitrary",)),
    )(*args)


@functools.partial(jax.custom_vjp, nondiff_argnums=(4,))
def ada_norm(x, nw, sc, sh, tag):
    return _norm_fwd_call(x, nw, sc, sh, "adanorm_" + tag)


def _ada_norm_f(x, nw, sc, sh, tag):
    return _norm_fwd_call(x, nw, sc, sh, "adanorm_" + tag), (x, nw, sc)


def _ada_norm_b(tag, res, dh):
    x, nw, sc = res
    dx, dnw, dsc, dsh = _norm_bwd_call(x, nw, sc, dh, "adanorm_" + tag + "_bwd")
    return dx, dnw, dsc, dsh


ada_norm.defvjp(_ada_norm_f, _ada_norm_b)


@functools.partial(jax.custom_vjp, nondiff_argnums=(2,))
def rms_norm(x, nw, tag):
    return _norm_fwd_call(x, nw, None, None, "rms_" + tag)


def _rms_norm_f(x, nw, tag):
    return _norm_fwd_call(x, nw, None, None, "rms_" + tag), (x, nw)


def _rms_norm_b(tag, res, dh):
    x, nw = res
    dx, dnw = _norm_bwd_call(x, nw, None, dh, "rms_" + tag + "_bwd")
    return dx, dnw


rms_norm.defvjp(_rms_norm_f, _rms_norm_b)


def _gate_mix_fwd_call(gl, ya, yb, name):
    t, d = ya.shape
    tr = _rows(t)
    row = pl.BlockSpec((tr, d), lambda i: (i, 0))

    def body(ga_ref, gb_ref, ya_ref, yb_ref, o_ref):
        o_ref[...] = (_sigmoid(ga_ref[...].astype(F32)) * ya_ref[...].astype(F32)
                      + _sigmoid(gb_ref[...].astype(F32)) * yb_ref[...].astype(F32)).astype(o_ref.dtype)

    return pl.pallas_call(
        body, name=name, grid=(t // tr,),
        in_specs=[row, pl.BlockSpec((tr, d), lambda i: (i, 1)), row, row], out_specs=row,
        out_shape=jax.ShapeDtypeStruct((t, d), BF16),
        compiler_params=_cp(("parallel",)),
    )(gl, gl, ya, yb)


def _gate_mix_bwd_call(gl, ya, yb, dm, name):
    t, d = ya.shape
    tr = _rows(t)
    row = pl.BlockSpec((tr, d), lambda i: (i, 0))
    wide = pl.BlockSpec((tr, 2 * d), lambda i: (i, 0))

    def body(gl_ref, ya_ref, yb_ref, dm_ref, dgl_ref, dya_ref, dyb_ref):
        dm = dm_ref[...].astype(F32)
        ga = _sigmoid(gl_ref[:, :d].astype(F32))
        gb = _sigmoid(gl_ref[:, d:].astype(F32))
        dya_ref[...] = (dm * ga).astype(dya_ref.dtype)
        dyb_ref[...] = (dm * gb).astype(dyb_ref.dtype)
        dgl_ref[:, :d] = (dm * ya_ref[...].astype(F32) * ga * (1.0 - ga)).astype(dgl_ref.dtype)
        dgl_ref[:, d:] = (dm * yb_ref[...].astype(F32) * gb * (1.0 - gb)).astype(dgl_ref.dtype)

    return pl.pallas_call(
        body, name=name, grid=(t // tr,),
        in_specs=[wide, row, row, row], out_specs=[wide, row, row],
        out_shape=[jax.ShapeDtypeStruct((t, 2 * d), gl.dtype), jax.ShapeDtypeStruct((t, d), ya.dtype),
                   jax.ShapeDtypeStruct((t, d), yb.dtype)],
        compiler_params=_cp(("parallel",)),
    )(gl, ya, yb, dm)


@functools.partial(jax.custom_vjp, nondiff_argnums=(3,))
def gate_mix(gl, ya, yb, tag):
    return _gate_mix_fwd_call(gl, ya, yb, "gatemix_" + tag)


def _gate_mix_f(gl, ya, yb, tag):
    return _gate_mix_fwd_call(gl, ya, yb, "gatemix_" + tag), (gl, ya, yb)


def _gate_mix_b(tag, res, dm):
    return tuple(_gate_mix_bwd_call(*res, dm, "gatemix_" + tag + "_bwd"))


gate_mix.defvjp(_gate_mix_f, _gate_mix_b)


def _resid_bwd_call(gt, p, g, name):
    t, d = p.shape
    tr = _rows(t)
    row = pl.BlockSpec((tr, d), lambda i: (i, 0))
    vec = pl.BlockSpec((1, d), lambda i: (0, 0))

    def body(gt_ref, p_ref, g_ref, dp_ref, dgt_ref):
        i = pl.program_id(0)
        g = g_ref[...]
        dp_ref[...] = (g * gt_ref[...]).astype(dp_ref.dtype)

        @pl.when(i == 0)
        def _():
            dgt_ref[...] = jnp.zeros_like(dgt_ref)

        dgt_ref[...] += jnp.sum(g * p_ref[...].astype(F32), axis=0, keepdims=True)

    return pl.pallas_call(
        body, name=name, grid=(t // tr,), in_specs=[vec, row, row], out_specs=[row, vec],
        out_shape=[jax.ShapeDtypeStruct((t, d), BF16), jax.ShapeDtypeStruct((1, d), F32)],
        compiler_params=_cp(("arbitrary",)),
    )(gt, p, g)


@functools.partial(jax.custom_vjp, nondiff_argnums=(4,))
def mm_resid(x, gt, a, b, tag):
    return _mm(a, b, "nn", F32, "mmres_" + tag, resid=(x, gt))[0]


def _mm_resid_f(x, gt, a, b, tag):
    o, p = _mm(a, b, "nn", F32, "mmres_" + tag, resid=(x, gt))
    return o, (gt, a, b, p)


def _mm_resid_b(tag, res, g):
    gt, a, b, p = res
    dp, dgt = _resid_bwd_call(gt, p, g, "mmres_" + tag + "_gate")
    da = _mm(dp, b, "nt", a.dtype, "mmres_" + tag + "_da")
    db = _mm(a, dp, "tn", b.dtype, "mmres_" + tag + "_db")
    return g, dgt, da, db


mm_resid.defvjp(_mm_resid_f, _mm_resid_b)


def _ffn_up_fwd_call(h, w_s, name):
    t, d = h.shape
    per = w_s.shape[-1]
    half = N_DEV // 2
    assert per % LANE == 0
    tm = _tile(t, 256)

    def body(h_ref, wg_ref, wu_ref, a_ref, g_ref, u_ref):
        hv = h_ref[...]
        g = jnp.dot(hv, wg_ref[...], preferred_element_type=F32)
        u = jnp.dot(hv, wu_ref[...], preferred_element_type=F32)
        a_ref[...] = (g * _sigmoid(g) * u).astype(a_ref.dtype)
        g_ref[...] = g.astype(g_ref.dtype)
        u_ref[...] = u.astype(u_ref.dtype)

    out = pl.BlockSpec((tm, per), lambda s, i: (i, s))
    return pl.pallas_call(
        body, name=name, grid=(half, t // tm),
        in_specs=[pl.BlockSpec((tm, d), lambda s, i: (i, 0)),
                  pl.BlockSpec((None, d, per), lambda s, i: (s, 0, 0)),
                  pl.BlockSpec((None, d, per), lambda s, i: (s + half, 0, 0))],
        out_specs=[out, out, out],
        out_shape=[jax.ShapeDtypeStruct((t, half * per), BF16)] * 3,
        compiler_params=_cp(("parallel", "parallel")),
    )(h, w_s, w_s)


def _swiglu_bwd_call(g, u, da, name):
    t, f = g.shape
    tr = _rows(t, 128)
    half = pl.BlockSpec((tr, f), lambda i: (i, 0))

    def body(g_ref, u_ref, da_ref, dgu_ref):
        g = g_ref[...].astype(F32)
        u = u_ref[...].astype(F32)
        da = da_ref[...].astype(F32)
        s = _sigmoid(g)
        ds = da * s
        dgu_ref[:, :f] = (ds * u * (1.0 + g * (1.0 - s))).astype(dgu_ref.dtype)
        dgu_ref[:, f:] = (ds * g).astype(dgu_ref.dtype)

    return pl.pallas_call(
        body, name=name, grid=(t // tr,),
        in_specs=[half, half, half], out_specs=pl.BlockSpec((tr, 2 * f), lambda i: (i, 0)),
        out_shape=jax.ShapeDtypeStruct((t, 2 * f), BF16), compiler_params=_cp(("parallel",)),
    )(g, u, da)


@functools.partial(jax.custom_vjp, nondiff_argnums=(2,))
def ffn_up(h, w_s, tag):
    return _ffn_up_fwd_call(h, w_s, "ffnup_" + tag)[0]


def _ffn_up_f(h, w_s, tag):
    a, g, u = _ffn_up_fwd_call(h, w_s, "ffnup_" + tag)
    return a, (h, w_s, g, u)


def _ffn_up_b(tag, res, da):
    h, w_s, g, u = res
    dgu = _swiglu_bwd_call(g, u, da, "ffnup_" + tag + "_act")
    dh = _mm(dgu, w_s, "nt", h.dtype, "ffnup_" + tag + "_da", slots=True)
    dw = _mm(h, dgu, "tn", w_s.dtype, "ffnup_" + tag + "_db", slots=True)
    return dh, dw


ffn_up.defvjp(_ffn_up_f, _ffn_up_b)


def loss_head(x, fw, tgt):
    t, d = x.shape
    tr = _rows(t)
    row = pl.BlockSpec((tr, d), lambda i: (i, 0))
    vec = pl.BlockSpec((1, d), lambda i: (0, 0))
    tile = pl.BlockSpec((8, LANE), lambda i: (0, 0))

    def body(x_ref, fw_ref, tgt_ref, loss_ref, dx_ref, dfw_ref):
        i = pl.program_id(0)
        xv = x_ref[...]
        fw = fw_ref[...]
        r = lax.rsqrt(jnp.mean(xv * xv, axis=-1, keepdims=True) + EPS)
        yh = xv * r
        e = yh * fw - tgt_ref[...]
        dy = e * (1.0 / d)
        dyw = dy * fw
        dx_ref[...] = r * (dyw - yh * jnp.mean(dyw * yh, axis=-1, keepdims=True))

        @pl.when(i == 0)
        def _():
            loss_ref[...] = jnp.zeros_like(loss_ref)
            dfw_ref[...] = jnp.zeros_like(dfw_ref)

        loss_ref[...] += 0.5 * jnp.sum(jnp.mean(e * e, axis=-1, keepdims=True))
        dfw_ref[...] += jnp.sum(dy * yh, axis=0, keepdims=True)

    return pl.pallas_call(
        body, name="loss_head", grid=(t // tr,), in_specs=[row, vec, row],
        out_specs=[tile, row, vec],
        out_shape=[jax.ShapeDtypeStruct((8, LANE), F32), jax.ShapeDtypeStruct((t, d), F32),
                   jax.ShapeDtypeStruct((1, d), F32)],
        compiler_params=_cp(("arbitrary",)),
    )(x, fw, tgt)


def _attn_scores(qn_ref, qr_ref, kn_ref, kr_ref, diag):
    tq = qn_ref.shape[0]
    s = lax.dot_general(qn_ref[...].astype(BF16), kn_ref[...].astype(BF16), NT, preferred_element_type=F32)
    s += lax.dot_general(qr_ref[...].astype(BF16), kr_ref[...].astype(BF16), NT, preferred_element_type=F32)
    s = s * (QK_NOPE + QK_ROPE) ** -0.5
    if diag:
        rows = lax.broadcasted_iota(jnp.int32, (tq, tq), 0)
        cols = lax.broadcasted_iota(jnp.int32, (tq, tq), 1)
        s = jnp.where(cols <= rows, s, -1e30)
    return s


def _attn_fwd_call(qn, qr, kv, kr, name):
    t = qn.shape[0]
    h_n = MLA_HEADS
    tq = _rows(t, 512)
    nq = t // tq
    assert V_HEAD == LANE and tq % LANE == 0

    def body(qn_ref, qr_ref, kn_ref, v_ref, kr_ref, o_ref, lse_ref, m_scr, l_scr, acc_scr):
        i, j = pl.program_id(1), pl.program_id(2)

        @pl.when(j == 0)
        def _():
            m_scr[...] = jnp.full_like(m_scr, -1e30)
            l_scr[...] = jnp.zeros_like(l_scr)
            acc_scr[...] = jnp.zeros_like(acc_scr)

        def step(diag):
            s = _attn_scores(qn_ref, qr_ref, kn_ref, kr_ref, diag)
            m_old = m_scr[...]
            m_new = jnp.maximum(m_old, jnp.max(s, axis=-1, keepdims=True))
            p = jnp.exp(s - jnp.tile(m_new, (1, tq // LANE)))
            alpha = jnp.exp(m_old - m_new)
            l_scr[...] = alpha * l_scr[...] + jnp.sum(p, axis=-1, keepdims=True)
            acc_scr[...] = alpha * acc_scr[...] + jnp.dot(p.astype(BF16), v_ref[...].astype(BF16),
                                                           preferred_element_type=F32)
            m_scr[...] = m_new

        @pl.when(j < i)
        def _():
            step(False)

        @pl.when(j == i)
        def _():
            step(True)
            o_ref[...] = (acc_scr[...] / l_scr[...]).astype(o_ref.dtype)
            lse_ref[...] = (m_scr[...] + jnp.log(l_scr[...]))[:, :1]

    return pl.pallas_call(
        body, name=name, grid=(h_n, nq, nq),
        in_specs=[
            pl.BlockSpec((tq, QK_NOPE), lambda h, i, j: (i, h)),
            pl.BlockSpec((None, tq, QK_ROPE), lambda h, i, j: (h, i, 0)),
            pl.BlockSpec((tq, QK_NOPE), lambda h, i, j: (jnp.minimum(j, i), 2 * h)),
            pl.BlockSpec((tq, V_HEAD), lambda h, i, j: (jnp.minimum(j, i), 2 * h + 1)),
            pl.BlockSpec((tq, QK_ROPE), lambda h, i, j: (jnp.minimum(j, i), 0)),
        ],
        out_specs=[
            pl.BlockSpec((tq, V_HEAD), lambda h, i, j: (i, h)),
            pl.BlockSpec((None, tq, 1), lambda h, i, j: (h, i, 0)),
        ],
        out_shape=[jax.ShapeDtypeStruct((t, h_n * V_HEAD), BF16),
                   jax.ShapeDtypeStruct((h_n, t, 1), F32)],
        scratch_shapes=[pltpu.VMEM((tq, LANE), F32), pltpu.VMEM((tq, LANE), F32),
                        pltpu.VMEM((tq, V_HEAD), F32)],
        compiler_params=_cp(("parallel", "parallel", "arbitrary")),
    )(qn, qr, kv, kv, kr)


def _attn_bwd_call(qn, qr, kv, kr, o, lse, do, name):
    t = qn.shape[0]
    h_n = MLA_HEADS
    tq = _rows(t, 512)
    nq = t // tq
    scale = (QK_NOPE + QK_ROPE) ** -0.5

    def body(qn_ref, qr_ref, kn_ref, v_ref, kr_ref, o_ref, lse_ref, do_ref,
             dqn_ref, dqr_ref, dkv_ref, dkr_ref, dqn_scr, dqr_scr, dkn_scr, dv_scr, dkr_scr):
        j, i = pl.program_id(1), pl.program_id(2)

        @pl.when(jnp.logical_and(j == 0, i == 0))
        def _():
            dqn_scr[...] = jnp.zeros_like(dqn_scr)
            dqr_scr[...] = jnp.zeros_like(dqr_scr)

        @pl.when(i == 0)
        def _():
            dkn_scr[...] = jnp.zeros_like(dkn_scr)
            dv_scr[...] = jnp.zeros_like(dv_scr)
            dkr_scr[...] = jnp.zeros_like(dkr_scr)

        def step(diag):
            qn_b = qn_ref[...].astype(BF16)
            qr_b = qr_ref[...].astype(BF16)
            kn_b = kn_ref[...].astype(BF16)
            kr_b = kr_ref[...].astype(BF16)
            do_b = do_ref[...]
            p = jnp.exp(_attn_scores(qn_ref, qr_ref, kn_ref, kr_ref, diag) - lse_ref[...])
            delta = jnp.sum(do_b.astype(F32) * o_ref[...].astype(F32), axis=-1, keepdims=True)
            dp = lax.dot_general(do_b, v_ref[...].astype(BF16), NT, preferred_element_type=F32)
            ds = (p * (dp - delta) * scale).astype(BF16)
            p_b = p.astype(BF16)
            dv_scr[...] += lax.dot_general(p_b, do_b, TN, preferred_element_type=F32)
            dkn_scr[...] += lax.dot_general(ds, qn_b, TN, preferred_element_type=F32)
            dkr_scr[...] += lax.dot_general(ds, qr_b, TN, preferred_element_type=F32)
            sl = pl.ds(pl.multiple_of(i * tq, tq), tq)
            dqn_scr[sl, :] += jnp.dot(ds, kn_b, preferred_element_type=F32)
            dqr_scr[sl, :] += jnp.dot(ds, kr_b, preferred_element_type=F32)

        @pl.when(i > j)
        def _():
            step(False)

        @pl.when(i == j)
        def _():
            step(True)

        @pl.when(i == nq - 1)
        def _():
            dkv_ref[:, :QK_NOPE] = dkn_scr[...].astype(dkv_ref.dtype)
            dkv_ref[:, QK_NOPE:] = dv_scr[...].astype(dkv_ref.dtype)
            dkr_ref[...] = dkr_scr[...]

        @pl.when(jnp.logical_and(j == nq - 1, i == nq - 1))
        def _():
            dqn_ref[...] = dqn_scr[...].astype(dqn_ref.dtype)
            dqr_ref[...] = dqr_scr[...].astype(dqr_ref.dtype)

    qi = lambda j, i: jnp.maximum(i, j)
    return pl.pallas_call(
        body, name=name, grid=(h_n, nq, nq),
        in_specs=[
            pl.BlockSpec((tq, QK_NOPE), lambda h, j, i: (qi(j, i), h)),
            pl.BlockSpec((None, tq, QK_ROPE), lambda h, j, i: (h, qi(j, i), 0)),
            pl.BlockSpec((tq, QK_NOPE), lambda h, j, i: (j, 2 * h)),
            pl.BlockSpec((tq, V_HEAD), lambda h, j, i: (j, 2 * h + 1)),
            pl.BlockSpec((tq, QK_ROPE), lambda h, j, i: (j, 0)),
            pl.BlockSpec((tq, V_HEAD), lambda h, j, i: (qi(j, i), h)),
            pl.BlockSpec((None, tq, 1), lambda h, j, i: (h, qi(j, i), 0)),
            pl.BlockSpec((tq, V_HEAD), lambda h, j, i: (qi(j, i), h)),
        ],
        out_specs=[
            pl.BlockSpec((t, QK_NOPE), lambda h, j, i: (0, h)),
            pl.BlockSpec((None, t, QK_ROPE), lambda h, j, i: (h, 0, 0)),
            pl.BlockSpec((tq, QK_NOPE + V_HEAD), lambda h, j, i: (j, h)),
            pl.BlockSpec((None, tq, QK_ROPE), lambda h, j, i: (h, j, 0)),
        ],
        out_shape=[jax.ShapeDtypeStruct((t, h_n * QK_NOPE), qn.dtype),
                   jax.ShapeDtypeStruct((h_n, t, QK_ROPE), qr.dtype),
                   jax.ShapeDtypeStruct((t, h_n * (QK_NOPE + V_HEAD)), kv.dtype),
                   jax.ShapeDtypeStruct((h_n, t, QK_ROPE), F32)],
        scratch_shapes=[pltpu.VMEM((t, QK_NOPE), F32), pltpu.VMEM((t, QK_ROPE), F32),
                        pltpu.VMEM((tq, QK_NOPE), F32), pltpu.VMEM((tq, V_HEAD), F32),
                        pltpu.VMEM((tq, QK_ROPE), F32)],
        compiler_params=_cp(("parallel", "arbitrary", "arbitrary")),
    )(qn, qr, kv, kv, kr, o, lse, do)


@functools.partial(jax.custom_vjp, nondiff_argnums=(4,))
def attention(qn, qr, kv, kr, tag):
    return _attn_fwd_call(qn, qr, kv, kr, "attn_" + tag)[0]


def _attention_f(qn, qr, kv, kr, tag):
    o, lse = _attn_fwd_call(qn, qr, kv, kr, "attn_" + tag)
    return o, (qn, qr, kv, kr, o, lse)


def _attention_b(tag, res, do):
    dqn, dqr, dkv, dkr_h = _attn_bwd_call(*res, do, "attn_" + tag + "_bwd")
    return dqn, dqr, dkv, jnp.sum(dkr_h, axis=0).astype(res[3].dtype)


attention.defvjp(_attention_f, _attention_b)


def _shift_down(u, s):
    if s == 0:
        return u
    t = u.shape[0]
    rolled = pltpu.roll(u, s, 0)
    return jnp.where(lax.broadcasted_iota(jnp.int32, u.shape, 0) >= s, rolled, 0.0)


def _shift_up(u, s):
    if s == 0:
        return u
    t = u.shape[0]
    rolled = pltpu.roll(u, t - s, 0)
    return jnp.where(lax.broadcasted_iota(jnp.int32, u.shape, 0) < t - s, rolled, 0.0)


def _conv_blocks(t, c3):
    p = c3 // 3
    tc = _tile(p, 512)
    per = p // tc
    return p, tc, per


def _conv_fwd_call(u, w, name):
    t, c3 = u.shape
    p, tc, per = _conv_blocks(t, c3)

    def body(u_ref, w_ref, o_ref):
        u = u_ref[...].astype(F32)
        y = jnp.zeros_like(u)
        for j in range(CONV_WIDTH):
            y = y + w_ref[j:j + 1, :] * _shift_down(u, CONV_WIDTH - 1 - j)
        o_ref[...] = y * _sigmoid(y)

    return pl.pallas_call(
        body, name=name, grid=(c3 // tc,),
        in_specs=[pl.BlockSpec((t, tc), lambda cb: (0, cb)),
                  pl.BlockSpec((CONV_WIDTH, tc), lambda cb: (0, cb))],
        out_specs=pl.BlockSpec((None, t, tc), lambda cb: (cb // per, 0, cb % per)),
        out_shape=jax.ShapeDtypeStruct((3, t, p), F32),
        compiler_params=_cp(("parallel",)),
    )(u, w)


def _conv_bwd_call(u, w, do, name):
    t, c3 = u.shape
    p, tc, per = _conv_blocks(t, c3)

    def body(u_ref, w_ref, do_ref, du_ref, dw_ref):
        u = u_ref[...].astype(F32)
        shifted = [_shift_down(u, CONV_WIDTH - 1 - j) for j in range(CONV_WIDTH)]
        y = jnp.zeros_like(u)
        for j in range(CONV_WIDTH):
            y = y + w_ref[j:j + 1, :] * shifted[j]
        s = _sigmoid(y)
        dy = do_ref[...] * s * (1.0 + y * (1.0 - s))
        du = jnp.zeros_like(u)
        for j in range(CONV_WIDTH):
            du = du + w_ref[j:j + 1, :] * _shift_up(dy, CONV_WIDTH - 1 - j)
            dw_ref[j:j + 1, :] = jnp.sum(dy * shifted[j], axis=0, keepdims=True)
        du_ref[...] = du.astype(du_ref.dtype)

    return pl.pallas_call(
        body, name=name, grid=(c3 // tc,),
        in_specs=[pl.BlockSpec((t, tc), lambda cb: (0, cb)),
                  pl.BlockSpec((CONV_WIDTH, tc), lambda cb: (0, cb)),
                  pl.BlockSpec((None, t, tc), lambda cb: (cb // per, 0, cb % per))],
        out_specs=[pl.BlockSpec((t, tc), lambda cb: (0, cb)),
                   pl.BlockSpec((CONV_WIDTH, tc), lambda cb: (0, cb))],
        out_shape=[jax.ShapeDtypeStruct((t, c3), u.dtype), jax.ShapeDtypeStruct((CONV_WIDTH, c3), F32)],
        compiler_params=_cp(("parallel",)),
    )(u, w, do)


@functools.partial(jax.custom_vjp, nondiff_argnums=(2,))
def conv_silu(u, w, tag):
    return _conv_fwd_call(u, w, "conv_" + tag)


def _conv_silu_f(u, w, tag):
    return _conv_fwd_call(u, w, "conv_" + tag), (u, w)


def _conv_silu_b(tag, res, do):
    return tuple(_conv_bwd_call(*res, do, "conv_" + tag + "_bwd"))


conv_silu.defvjp(_conv_silu_f, _conv_silu_b)


BNN = (((2,), (1,)), ((0,), (0,)))
BNT = (((2,), (2,)), ((0,), (0,)))
BTN = (((1,), (1,)), ((0,), (0,)))


def _bf16_dot(a, b, dn):
    return lax.dot_general(a.astype(BF16), b.astype(BF16), dn, preferred_element_type=F32)


def _mask_dot(mask, v, dn):
    p1 = v.astype(BF16)
    r1 = v - p1.astype(F32)
    p2 = r1.astype(BF16)
    p3 = r1 - p2.astype(F32)
    return _bf16_dot(mask, p1, dn) + (_bf16_dot(mask, p2, dn) + _bf16_dot(mask, p3, dn))


@jax.custom_vjp
def _xdot(mask, v):
    return _mask_dot(mask, v, BNN)


def _xdot_f(mask, v):
    return _mask_dot(mask, v, BNN), (mask,)


def _xdot_b(res, g):
    (mask,) = res
    return jnp.zeros_like(mask), _mask_dot(mask, g, BTN)


_xdot.defvjp(_xdot_f, _xdot_b)


def _dot3(a, b, dn):
    ah, bh = a.astype(BF16), b.astype(BF16)
    al, bl = a - ah.astype(F32), b - bh.astype(F32)
    return _bf16_dot(ah, bh, dn) + (_bf16_dot(ah, bl, dn) + _bf16_dot(al, bh, dn))


def _transposed(dn, a, b, g):
    if dn == BNN:
        return (g, b, BNT), (a, g, BTN)
    if dn == BNT:
        return (g, b, BNN), (g, a, BTN)
    return (b, g, BNT), (a, g, BNN)


@functools.partial(jax.custom_vjp, nondiff_argnums=(2,))
def _hdot(a, b, dn=BNN):
    return _dot3(a, b, dn)


def _hdot_f(a, b, dn):
    return _dot3(a, b, dn), (a, b)


def _hdot_b(dn, res, g):
    da, db = _transposed(dn, *res, g)
    return _dot3(*da), _dot3(*db)


_hdot.defvjp(_hdot_f, _hdot_b)


@functools.partial(jax.custom_vjp, nondiff_argnums=(2,))
def _bdot(a, b, dn=BNN):
    return _bf16_dot(a, b, dn)


def _bdot_f(a, b, dn):
    return _bf16_dot(a, b, dn), (a, b)


def _bdot_b(dn, res, g):
    da, db = _transposed(dn, *res, g)
    return _bf16_dot(*da), _bf16_dot(*db)


_bdot.defvjp(_bdot_f, _bdot_b)


def _gdn_chunk(q, k, v, z, bl, al, a_log, dtb, gn, s):
    b, c = q.shape[0], q.shape[1]
    ri = lax.broadcasted_iota(jnp.int32, (c, c), 0)
    ci = lax.broadcasted_iota(jnp.int32, (c, c), 1)
    lower = (ri >= ci)[None]
    strict = (ri > ci)[None]
    low_incl = jnp.broadcast_to((ri >= ci).astype(F32), (b, c, c))
    up_incl = jnp.broadcast_to((ri <= ci).astype(F32), (b, c, c))
    eye = (ri == ci).astype(F32)[None]

    q = q * lax.rsqrt(jnp.sum(q * q, axis=-1, keepdims=True) + EPS) * (GDN_DK ** -0.5)
    k = k * lax.rsqrt(jnp.sum(k * k, axis=-1, keepdims=True) + EPS)
    beta = _sigmoid(bl)
    g = -jnp.exp(a_log) * _softplus(al + dtb)
    g_w = jnp.broadcast_to(g, (b, c, LANE))
    gc = _xdot(low_incl, g_w)
    gr = _xdot(jnp.ones((b, c, c), F32), g_w[:, :, :c] * up_incl)
    diff = gc[:, :, :c] - gr
    decay = jnp.where(lower, jnp.exp(jnp.where(lower, diff, 0.0)), 0.0)
    kb = k * beta
    lmat = jnp.where(strict, _bdot(kb, k, BNT) * decay, 0.0)
    inv = eye - lmat
    pw = lmat
    for _ in range(int(math.log2(c)) - 1):
        pw = _hdot(pw, pw)
        inv = _hdot(inv, eye + pw)
    eg = jnp.exp(gc)
    u = _hdot(inv, v * beta)
    w = _hdot(inv, kb * eg)
    attn = jnp.where(lower, _bdot(q, k, BNT) * decay, 0.0)
    v_new = u - _bdot(w, s)
    o = _bdot(q * eg, s) + _bdot(attn, v_new)
    g_last = jnp.sum(g_w, axis=1, keepdims=True)
    k_dec = k * jnp.exp(g_last - gc)
    s_new = s * jnp.exp(g_last) + _bdot(k_dec, v_new, BTN)
    on = o * lax.rsqrt(jnp.mean(o * o, axis=-1, keepdims=True) + EPS) * gn
    return on * (z * _sigmoid(z)), s_new


def _head_cols(ba, first, count):
    lane = lax.broadcasted_iota(jnp.int32, ba.shape, 1)
    return jnp.stack([jnp.sum(jnp.where(lane == first + j, ba, 0.0), axis=1, keepdims=True)
                      for j in range(count)])


def _gdn_heads(q, k, v, z, ba, a_log, dtb, gn, s):
    h_n = q.shape[0]
    return _gdn_chunk(q, k, v, z, _head_cols(ba, 0, h_n), _head_cols(ba, h_n, h_n), a_log, dtb, gn, s)


def _gdn_specs(n_chunks, hb, rev):
    c = CHUNK
    nn = (lambda n: n_chunks - 1 - n) if rev else (lambda n: n)
    plane = lambda pidx: pl.BlockSpec((None, c, hb * GDN_DK), lambda hg, n: (pidx, nn(n), hg))
    assert hb == GDN_HEADS
    logits = pl.BlockSpec((c, LANE), lambda hg, n: (nn(n), 0))
    scal = pl.BlockSpec((hb, 1, 1), lambda hg, n: (hg, 0, 0))
    zspec = pl.BlockSpec((c, hb * GDN_DV), lambda hg, n: (nn(n), hg))
    gnspec = pl.BlockSpec((1, GDN_DV), lambda hg, n: (0, 0))
    sspec = pl.BlockSpec((hb, None, GDN_DK, GDN_DV), lambda hg, n: (hg, nn(n), 0, 0))
    return plane, logits, scal, zspec, gnspec, sspec


def _heads(ref, hb):
    return jnp.stack([ref[:, j * GDN_DK:(j + 1) * GDN_DK] for j in range(hb)])


def _gdn_fwd_call(qkv, z, ba, a_log, dtb, gn, name):
    t = z.shape[0]
    h_n = GDN_HEADS
    hb = h_n
    n_chunks = t // CHUNK
    plane, logits, scal, zspec, gnspec, sspec = _gdn_specs(n_chunks, hb, False)

    def body(q_ref, k_ref, v_ref, z_ref, ba_ref, a_ref, dtb_ref, gn_ref, o_ref, sall_ref, s_scr):
        n = pl.program_id(1)

        @pl.when(n == 0)
        def _():
            s_scr[...] = jnp.zeros_like(s_scr)

        s = s_scr[...]
        sall_ref[...] = s
        o, s_new = _gdn_heads(_heads(q_ref, hb), _heads(k_ref, hb), _heads(v_ref, hb),
                              _heads(z_ref, hb).astype(F32),
                              ba_ref[...], a_ref[...], dtb_ref[...], gn_ref[...], s)
        for j in range(hb):
            o_ref[:, j * GDN_DV:(j + 1) * GDN_DV] = o[j].astype(o_ref.dtype)
        s_scr[...] = s_new

    return pl.pallas_call(
        body, name=name, grid=(h_n // hb, n_chunks),
        in_specs=[plane(0), plane(1), plane(2), zspec, logits, scal, scal, gnspec],
        out_specs=[zspec, sspec],
        out_shape=[jax.ShapeDtypeStruct((t, h_n * GDN_DV), BF16),
                   jax.ShapeDtypeStruct((h_n, n_chunks, GDN_DK, GDN_DV), F32)],
        scratch_shapes=[pltpu.VMEM((hb, GDN_DK, GDN_DV), F32)],
        compiler_params=_cp(("parallel", "arbitrary")),
    )(qkv, qkv, qkv, z, ba, a_log, dtb, gn)


def _gdn_bwd_call(qkv, z, ba, a_log, dtb, gn, sall, do, name):
    t = z.shape[0]
    h_n = GDN_HEADS
    hb = h_n
    n_chunks = t // CHUNK
    c = CHUNK
    plane, logits, scal, zspec, gnspec, sspec = _gdn_specs(n_chunks, hb, True)
    dplanes = pl.BlockSpec((3, c, hb * GDN_DK), lambda hg, n: (0, n_chunks - 1 - n, hg))
    gnh = pl.BlockSpec((None, 1, GDN_DV), lambda hg, n: (hg, 0, 0))

    def body(q_ref, k_ref, v_ref, z_ref, ba_ref, a_ref, dtb_ref, gn_ref, s_ref, do_ref,
             dqkv_ref, dz_ref, dba_ref, da_ref, ddtb_ref, dgn_ref, ds_scr):
        n = pl.program_id(1)

        @pl.when(n == 0)
        def _():
            ds_scr[...] = jnp.zeros_like(ds_scr)
            da_ref[...] = jnp.zeros_like(da_ref)
            ddtb_ref[...] = jnp.zeros_like(ddtb_ref)
            dgn_ref[...] = jnp.zeros_like(dgn_ref)

        _, vjp = jax.vjp(_gdn_heads, _heads(q_ref, hb), _heads(k_ref, hb), _heads(v_ref, hb),
                         _heads(z_ref, hb).astype(F32),
                         ba_ref[...], a_ref[...], dtb_ref[...], gn_ref[...], s_ref[...])
        dq, dk, dv, dz, dba, da, ddtb, dgn, ds = vjp((_heads(do_ref, hb).astype(F32), ds_scr[...]))
        for j in range(hb):
            hs = slice(j * GDN_DK, (j + 1) * GDN_DK)
            dqkv_ref[0, :, hs] = dq[j]
            dqkv_ref[1, :, hs] = dk[j]
            dqkv_ref[2, :, hs] = dv[j]
            dz_ref[:, hs] = dz[j].astype(dz_ref.dtype)
        dba_ref[...] = dba
        da_ref[...] += da
        ddtb_ref[...] += ddtb
        dgn_ref[...] += dgn
        ds_scr[...] = ds

    return pl.pallas_call(
        body, name=name, grid=(h_n // hb, n_chunks),
        in_specs=[plane(0), plane(1), plane(2), zspec, logits, scal, scal, gnspec, sspec, zspec],
        out_specs=[dplanes, zspec, logits, scal, scal, gnh],
        out_shape=[jax.ShapeDtypeStruct((3, t, h_n * GDN_DK), F32),
                   jax.ShapeDtypeStruct((t, h_n * GDN_DV), z.dtype),
                   jax.ShapeDtypeStruct((t, LANE), F32),
                   jax.ShapeDtypeStruct((h_n, 1, 1), F32), jax.ShapeDtypeStruct((h_n, 1, 1), F32),
                   jax.ShapeDtypeStruct((h_n // hb, 1, GDN_DV), F32)],
        scratch_shapes=[pltpu.VMEM((hb, GDN_DK, GDN_DV), F32)],
        compiler_params=_cp(("parallel", "arbitrary")),
    )(qkv, qkv, qkv, z, ba, a_log, dtb, gn, sall, do)


@functools.partial(jax.custom_vjp, nondiff_argnums=(6,))
def gdn(qkv, z, ba, a_log, dtb, gn, tag):
    return _gdn_fwd_call(qkv, z, ba, a_log, dtb, gn, "gdn_" + tag)[0]


def _gdn_f(qkv, z, ba, a_log, dtb, gn, tag):
    o, sall = _gdn_fwd_call(qkv, z, ba, a_log, dtb, gn, "gdn_" + tag)
    return o, (qkv, z, ba, a_log, dtb, gn, sall)


def _gdn_b(tag, res, do):
    dqkv, dz, dba, da, ddtb, dgn_h = _gdn_bwd_call(*res, do, "gdn_" + tag + "_bwd")
    return dqkv, dz, dba, da, ddtb, jnp.sum(dgn_h, axis=0)


gdn.defvjp(_gdn_f, _gdn_b)


ADAMW_BLOCK = 384 * 1024


def adamw(w, m, v, *, parts, name):
    n_layers = len(parts)
    n_parts, r, c = parts[0].shape
    assert w.shape == (n_layers * r, c), (w.shape, parts[0].shape)
    tr = r
    for cand in (512, 256, 128, 64, 32, 16, 8):
        if r % cand == 0 and cand * c <= ADAMW_BLOCK:
            tr = cand
            break
    nb = r // tr
    blk = pl.BlockSpec((tr, c), lambda l, i: (l * nb + i, 0))
    bc1 = 1.0 - ADAM_B1 ** ADAM_STEP
    bc2 = 1.0 - ADAM_B2 ** ADAM_STEP

    def part_spec(li):
        return pl.BlockSpec((n_parts, tr, c),
                            lambda l, i: (0, jnp.where(l == li, i, jnp.where(l < li, 0, nb - 1)), 0))

    def body(*refs):
        w_ref, p_refs = refs[0], refs[1:1 + n_layers]
        m_ref, v_ref, g_ref, d_ref, mo_ref, vo_ref = refs[1 + n_layers:]
        for li in range(n_layers):
            @pl.when(pl.program_id(0) == li)
            def _(p_ref=p_refs[li]):
                g = p_ref[0].astype(F32)
                for i in range(1, n_parts):
                    g = g + p_ref[i].astype(F32)
                m2 = ADAM_B1 * m_ref[...] + (1.0 - ADAM_B1) * g
                v2 = ADAM_B2 * v_ref[...] + (1.0 - ADAM_B2) * (g * g)
                g_ref[...] = g
                mo_ref[...] = m2
                vo_ref[...] = v2
                d_ref[...] = -ADAM_LR * ((m2 / bc1) / (jnp.sqrt(v2 / bc2) + ADAM_EPS)
                                         + ADAM_WD * w_ref[...])

    return pl.pallas_call(
        body, name=name, grid=(n_layers, nb),
        in_specs=[blk] + [part_spec(li) for li in range(n_layers)] + [blk, blk],
        out_specs=[blk] * 4, out_shape=[jax.ShapeDtypeStruct(w.shape, F32)] * 4,
        compiler_params=_cp(("arbitrary", "arbitrary")),
    )(w, *parts, m, v)


_HBM = pl.BlockSpec(memory_space=pltpu.HBM)
_SEM = pl.BlockSpec(memory_space=pltpu.SEMAPHORE)
_EFFECT = pltpu.SideEffectType.DATAFLOW_SIDE_EFFECTING


def _peer(x, y, c, d):
    px = 1 - x if d & 4 else x
    py = 1 - y if d & 2 else y
    pc = 1 - c if d & 1 else c
    return (px, py, pc), 4 * px + 2 * py + pc


ALL_PEERS = (1, 2, 3, 4, 5, 6, 7)
SIBLING = 1
SAME_CORE_REMOTE = (2, 4, 6)


def copy_start(arrays, mode, carry, name):
    n = len(arrays)
    if mode == "forward":
        lands = []
    else:
        lands = [lax.empty(a.shape if mode == "scatter" else (N_DEV,) + a.shape, a.dtype) for a in arrays]
    n_in = n + len(lands) + 1

    def body(*refs):
        srcs = refs[:n]
        dsts = refs[n:2 * n] if lands else srcs
        sems = refs[n_in:n_in + 2 * n]
        x, y, c = (lax.axis_index(a) for a in AXES)
        me = 4 * x + 2 * y + c
        for k in range(n):
            if mode == "forward":
                sibling, _ = _peer(x, y, c, SIBLING)
                copies = [(srcs[k].at[_peer(x, y, c, d)[1]], dsts[k].at[_peer(x, y, c, d)[1]], sibling)
                          for d in SAME_CORE_REMOTE]
            elif mode == "gather":
                copies = [(srcs[k], dsts[k].at[me], _peer(x, y, c, d)[0]) for d in (SIBLING,) + SAME_CORE_REMOTE]
            else:
                copies = [(srcs[k].at[_peer(x, y, c, d)[1]], dsts[k].at[me], _peer(x, y, c, d)[0])
                          for d in ALL_PEERS]
            for src, dst, peer in copies:
                pltpu.make_async_remote_copy(src_ref=src, dst_ref=dst, send_sem=sems[2 * k],
                                             recv_sem=sems[2 * k + 1], device_id=peer,
                                             device_id_type=pl.DeviceIdType.MESH).start()

    operands = list(arrays) + lands + [carry]
    outs = pl.pallas_call(
        body, name=name,
        out_shape=tuple([pltpu.SemaphoreType.DMA(())] * (2 * n)
                        + [pltpu.HBM(a.shape, a.dtype) for a in operands]),
        in_specs=[_HBM] * n_in,
        out_specs=tuple([_SEM] * (2 * n) + [_HBM] * n_in),
        input_output_aliases={i: 2 * n + i for i in range(n_in)},
        compiler_params=pltpu.CompilerParams(has_side_effects=_EFFECT),
    )(*[pltpu.with_memory_space_constraint(a, pltpu.HBM) for a in operands])
    sems, thru = outs[:2 * n], outs[2 * n:-1]
    handles = [(sems[2 * k], sems[2 * k + 1], thru[k] if lands else None, thru[n + k] if lands else thru[k])
               for k in range(n)]
    return outs[-1], handles


def copy_wait(handles, n_blocks, after, name):
    n = len(handles)
    sems = [s for h in handles for s in h[:2]]
    srcs = [h[2] for h in handles if h[2] is not None]
    lands = [h[3] for h in handles]
    ns = len(srcs)

    def body(*refs):
        dsts = refs[ns:ns + n]
        sem_refs = refs[ns + n:ns + 3 * n]
        x, y, c = (lax.axis_index(a) for a in AXES)
        for k in range(n):
            blocks = dsts[k].at[pl.ds(0, n_blocks)]
            pltpu.make_async_remote_copy(
                src_ref=blocks, dst_ref=blocks, send_sem=sem_refs[2 * k], recv_sem=sem_refs[2 * k + 1],
                device_id=(x, y, c), device_id_type=pl.DeviceIdType.MESH).wait()

    outs = pl.pallas_call(
        body, name=name,
        out_shape=tuple([pltpu.HBM(a.shape, a.dtype) for a in srcs + lands]),
        in_specs=[_HBM] * (ns + n) + [_SEM] * (2 * n) + [pl.BlockSpec(memory_space=pl.ANY)],
        out_specs=tuple([_HBM] * (ns + n)),
        input_output_aliases={i: i for i in range(ns + n)},
        compiler_params=pltpu.CompilerParams(has_side_effects=_EFFECT),
    )(*srcs, *lands, *sems, after)
    return (list(outs[:ns]) if ns else [None] * n), list(outs[ns:])


def exchange(arrays, modes, name):
    n = len(arrays)
    hbm = pl.BlockSpec(memory_space=pltpu.HBM)
    out_shape = [jax.ShapeDtypeStruct(a.shape if md == "scatter" else (N_DEV,) + a.shape, a.dtype)
                 for a, md in zip(arrays, modes)]

    def body(*refs):
        ins, outs = refs[:n], refs[n:2 * n]
        send_sems, recv_sems, local_sems = refs[2 * n:]
        x, y, c = (lax.axis_index(a) for a in AXES)
        me = 4 * x + 2 * y + c

        def src(k, p):
            return ins[k].at[p] if modes[k] == "scatter" else ins[k]

        local = [pltpu.make_async_copy(src(k, me), outs[k].at[me], local_sems.at[k]) for k in range(n)]
        for cp in local:
            cp.start()
        started = []
        for d in range(1, N_DEV):
            px = 1 - x if d & 4 else x
            py = 1 - y if d & 2 else y
            pc = 1 - c if d & 1 else c
            pid = 4 * px + 2 * py + pc
            for k in range(n):
                pltpu.make_async_remote_copy(
                    src_ref=src(k, pid), dst_ref=outs[k].at[me],
                    send_sem=send_sems.at[k, d - 1], recv_sem=recv_sems.at[k, d - 1],
                    device_id=(px, py, pc), device_id_type=pl.DeviceIdType.MESH).start()
                started.append((k, d, pid, (px, py, pc)))
        for k, d, pid, peer in started:
            pltpu.make_async_remote_copy(
                src_ref=src(k, pid), dst_ref=outs[k].at[pid],
                send_sem=send_sems.at[k, d - 1], recv_sem=recv_sems.at[k, d - 1],
                device_id=peer, device_id_type=pl.DeviceIdType.MESH).wait()
        for cp in local:
            cp.wait()

    outs = pl.pallas_call(
        body, name=name, in_specs=[hbm] * n, out_specs=[hbm] * n, out_shape=out_shape,
        scratch_shapes=[pltpu.SemaphoreType.DMA((n, N_DEV - 1)), pltpu.SemaphoreType.DMA((n, N_DEV - 1)),
                        pltpu.SemaphoreType.DMA((n,))],
        compiler_params=pltpu.CompilerParams(has_side_effects=True),
    )(*arrays)
    return list(outs)


SMALL = ("b_ada", "norm_mix", "norm_ffn", "q_a_norm", "kv_a_norm", "A_log", "dt_bias", "gdn_norm",
         "final_norm")
WEIGHTS = ("w_ada", "b_ada", "norm_mix", "norm_ffn", "w_in", "q_a_norm", "kv_a_norm", "w_uq", "w_ukv",
           "w_o_mla", "conv_w", "A_log", "dt_bias", "gdn_norm", "w_o_gdn", "w_o", "w_gate_up", "w_down",
           "final_norm")


def _unslot(g):
    return g.transpose(1, 0, 2).reshape(g.shape[1], -1)


def _cols(g):
    return g if g.shape[-1] % LANE == 0 else _unslot(g)


def _stack_rows(g):
    return g.reshape(-1, g.shape[-1])


def _rope(xv, cos, sin):
    x1, x2 = jnp.split(xv, 2, axis=-1)
    return jnp.concatenate([x1 * cos - x2 * sin, x2 * cos + x1 * sin], axis=-1)


MIX_WEIGHTS = ("w_uq", "w_ukv", "w_o_mla", "w_o_gdn", "w_o")
FFN_WEIGHTS = ("w_gate_up", "w_down")


def _pad_cols(a):
    return jnp.pad(a, ((0, 0), (0, _pad_lanes(a.shape[1]) - a.shape[1])))


def _stage_in(x, mod, nm, w_in_s, tg):
    d = x.shape[1]
    hg = GDN_HEADS
    w_in = _unslot(w_in_s)
    o1 = Q_LORA + KV_LORA + QK_ROPE
    o2 = o1 + 2 * hg * GDN_DK + hg * GDN_DV
    o3 = o2 + hg * GDN_DV
    o4 = o3 + 2 * hg
    h = ada_norm(x, nm, mod[:, d:2 * d], mod[:, :d], "mix" + tg)
    return (mm(h, _pad_cols(w_in[:, :o1]), "in_a" + tg, BF16), mm(h, w_in[:, o1:o2], "in_qkv" + tg, BF16),
            mm(h, w_in[:, o2:o3], "in_z" + tg, BF16), mm(h, _pad_cols(w_in[:, o3:o4]), "in_ba" + tg, F32),
            mm(h, w_in[:, o4:o4 + 2 * d], "in_g" + tg, BF16))


def _stage_mix(x, mod, seg_a, qkv, z, ba, gl, w_uq_s, w_ukv_s, w_o_mla_s, w_o_gdn_s, w_o_s, conv_s,
               qan, kvan, a_log, dtb, gn, cos, sin, tg):
    t, d = x.shape
    hq, hg = MLA_HEADS, GDN_HEADS
    w_uq = _unslot(w_uq_s).reshape(Q_LORA, hq, QK_NOPE + QK_ROPE)
    w_uq = jnp.concatenate([w_uq[:, :, :QK_NOPE].reshape(Q_LORA, hq * QK_NOPE),
                            w_uq[:, :, QK_NOPE:].reshape(Q_LORA, hq * QK_ROPE)], axis=1)
    c_q = seg_a[:, :Q_LORA]
    c_kv = seg_a[:, Q_LORA:Q_LORA + KV_LORA]
    k_pe = seg_a[:, Q_LORA + KV_LORA:Q_LORA + KV_LORA + QK_ROPE]
    qf = mm(rms_norm(c_q, qan, "qa" + tg), w_uq, "uq" + tg, BF16)
    kvf = mm(rms_norm(c_kv, kvan, "kva" + tg), _cols(w_ukv_s), "ukv" + tg, BF16)
    qn = qf[:, :hq * QK_NOPE]
    q_pe = qf[:, hq * QK_NOPE:].astype(F32).reshape(t, hq, QK_ROPE)
    qr = _rope(q_pe, cos[:, None, :], sin[:, None, :]).transpose(1, 0, 2).astype(BF16)
    kr = _rope(k_pe.astype(F32), cos, sin).astype(BF16)
    y_a = mm(attention(qn, qr, kvf, kr, tg), _cols(w_o_mla_s), "o_mla" + tg, BF16)
    conv_w = conv_s.transpose(1, 0, 2).reshape(CONV_WIDTH, -1)
    qkv_c = conv_silu(qkv, conv_w, tg)
    o_gdn = gdn(qkv_c, z, ba, a_log.reshape(hg, 1, 1), dtb.reshape(hg, 1, 1), gn, tg)
    y_b = mm(o_gdn, _cols(w_o_gdn_s), "o_gdn" + tg, BF16)
    return mm_resid(x, mod[:, 2 * d:3 * d], gate_mix(gl, y_a, y_b, tg), _stack_rows(w_o_s), "w_o" + tg)


def _stage_ffn(x, mod, nf, w_gu_s, w_down_s, tg):
    d = x.shape[1]
    h = ada_norm(x, nf, mod[:, 4 * d:5 * d], mod[:, 3 * d:4 * d], "ffn" + tg)
    return mm_resid(x, mod[:, 5 * d:6 * d], ffn_up(h, w_gu_s, tg), _stack_rows(w_down_s), "down" + tg)


def _flat_row(arrs):
    v = jnp.concatenate([a.reshape(-1) for a in arrs])
    return jnp.pad(v, (0, _pad_lanes(v.shape[0]) - v.shape[0]))[None, :]


def kernel(x, c, positions, w_ada, b_ada, norm_mix, norm_ffn, w_in, q_a_norm, kv_a_norm, w_uq, w_ukv, w_o_mla, conv_w, A_log, dt_bias, gdn_norm, w_o_gdn, w_o, w_gate_up, w_down, final_norm, loss_target, m_w_ada, m_b_ada, m_norm_mix, m_norm_ffn, m_w_in, m_q_a_norm, m_kv_a_norm, m_w_uq, m_w_ukv, m_w_o_mla, m_conv_w, m_A_log, m_dt_bias, m_gdn_norm, m_w_o_gdn, m_w_o, m_w_gate_up, m_w_down, m_final_norm, v_w_ada, v_b_ada, v_norm_mix, v_norm_ffn, v_w_in, v_q_a_norm, v_kv_a_norm, v_w_uq, v_w_ukv, v_w_o_mla, v_conv_w, v_A_log, v_dt_bias, v_gdn_norm, v_w_o_gdn, v_w_o, v_w_gate_up, v_w_down, v_final_norm):
    given = dict(locals())
    t, d = x.shape[1], x.shape[2]
    n_ada = w_ada.shape[2]
    me = 4 * lax.axis_index("x") + 2 * lax.axis_index("y") + lax.axis_index("c")

    def with_own(land, own):
        return lax.dynamic_update_slice(land, own[None], (me,) + (0,) * own.ndim)

    got = exchange([c, conv_w], ["gather", "gather"], "gather_small")
    c_all, conv_g = got[0].reshape(N_DEV, d), got[1]
    c_rows = jnp.pad(c_all, ((0, 16 - N_DEV), (0, 0)))
    mod_cols = jnp.stack([_mm(c_rows, w_ada[l], "nn", F32, "ada_mod%d" % l, a_act="silu")[:N_DEV]
                          for l in range(DEPTH)], axis=1)
    mod_mine = exchange([mod_cols], ["scatter"], "scatter_mod")[0]
    mods = mod_mine.transpose(1, 0, 2).reshape(DEPTH, N_DEV * n_ada) + b_ada

    groups = [[(n, l) for n in names] for l in range(DEPTH) for names in (("w_in",), MIX_WEIGHTS, FFN_WEIGHTS)]
    gtags = [s + str(l) for l in range(DEPTH) for s in ("in", "mix", "ffn")]
    keys = [k for ks in groups for k in ks]
    mods, handles = copy_start([given[n][l].astype(BF16) for n, l in keys], "gather", mods, "gather_start")
    handles = dict(zip(keys, handles))
    own, relayed = {}, {}

    def relay(gi, carry):
        ks = groups[gi]
        srcs, lands = copy_wait([handles[k] for k in ks], 1 + len(SAME_CORE_REMOTE), carry,
                                "wait_ici_" + gtags[gi])
        own.update(zip(ks, srcs))
        carry, hs = copy_start(lands, "forward", carry, "relay_" + gtags[gi])
        relayed.update(zip(ks, hs))
        return carry

    def landed(gi, after):
        ks = groups[gi]
        _, lands = copy_wait([relayed[k] for k in ks], len(SAME_CORE_REMOTE), after, "wait_" + gtags[gi])
        return [with_own(land, own[k]) for k, land in zip(ks, lands)]

    inv_freq = 1.0 / (ROPE_THETA ** (jnp.arange(0, QK_ROPE, 2, dtype=F32) / QK_ROPE))
    ang = positions[0].astype(F32)[:, None] * inv_freq
    cos, sin = jnp.cos(ang), jnp.sin(ang)
    relay_before = {0: [0], 1: [1], 2: [2, 3], 3: [4], 4: [5], 5: []}

    def weights_for(stage, carry):
        for gi in relay_before[stage]:
            carry = relay(gi, carry)
        return carry, landed(stage, carry)

    xl = x[0]
    vjps = []
    for l in range(DEPTH):
        tg = str(l)
        mod = mods[l:l + 1]
        if l == 0:
            mods, (w_in_s,) = weights_for(0, mods)
            mod = mods[:1]
        else:
            xl, (w_in_s,) = weights_for(3 * l, xl)
        seg, vjp_in = jax.vjp(lambda *a, tg=tg: _stage_in(*a, tg), xl, mod, norm_mix[l:l + 1], w_in_s)
        seg0, w_mix = weights_for(3 * l + 1, seg[0])
        seg = (seg0,) + tuple(seg[1:])
        xm, vjp_mix = jax.vjp(lambda *a, tg=tg: _stage_mix(*a, cos, sin, tg), xl, mod, *seg, *w_mix,
                              conv_g[:, l], q_a_norm[l:l + 1], kv_a_norm[l:l + 1], A_log[l], dt_bias[l],
                              gdn_norm[l:l + 1])
        xm, w_ffn = weights_for(3 * l + 2, xm)
        xl, vjp_ffn = jax.vjp(lambda *a, tg=tg: _stage_ffn(*a, tg), xm, mod, norm_ffn[l:l + 1], *w_ffn)
        vjps.append((vjp_in, vjp_mix, vjp_ffn))

    loss_t, g, dfn = loss_head(xl, final_norm[None, :], loss_target[0])
    loss = lax.psum(loss_t[0, 0], AXES)
    dsmall = {n: [None] * DEPTH for n in SMALL + ("conv_w",)}
    dmods = [None] * DEPTH
    sent = {}

    def send(ks, grads, carry, name):
        carry, hs = copy_start(list(grads), "scatter", carry, name)
        sent.update(zip(ks, hs))
        return carry

    for l in reversed(range(DEPTH)):
        tg = str(l)
        vjp_in, vjp_mix, vjp_ffn = vjps[l]
        dxm, dmod_f, dsmall["norm_ffn"][l], *dw = vjp_ffn(g)
        dxm = send([(n, l) for n in FFN_WEIGHTS], dw, dxm, "scatter_ffn" + tg)
        dx_m, dmod_m, *rest = vjp_mix(dxm)
        dseg, dw, rest = rest[:5], rest[5:5 + len(MIX_WEIGHTS)], rest[5 + len(MIX_WEIGHTS):]
        dseg[0] = send([(n, l) for n in MIX_WEIGHTS], dw, dseg[0], "scatter_mix" + tg)
        for n, gr in zip(("conv_w", "q_a_norm", "kv_a_norm", "A_log", "dt_bias", "gdn_norm"), rest):
            dsmall[n][l] = gr
        dx_i, dmod_i, dsmall["norm_mix"][l], dw_in = vjp_in(tuple(dseg))
        g = dx_i + dx_m
        if l > 0:
            g = send([("w_in", l)], [dw_in], g, "scatter_in" + tg)
        dmods[l] = dmod_f + dmod_m + dmod_i
    dx = g
    dmods = jnp.concatenate(dmods, axis=0)
    dconv = jnp.stack(dsmall.pop("conv_w"), axis=1)
    dsmall = {n: jnp.concatenate(v, axis=0) if v[0].ndim == 2 else jnp.stack(v)
              for n, v in dsmall.items() if v[0] is not None}
    dsmall["b_ada"] = dmods
    dsmall["final_norm"] = dfn[0]

    dmod_cols = dmods.reshape(DEPTH, N_DEV, n_ada).transpose(1, 0, 2)
    conv_parts, dmod_all, small_parts = exchange(
        [dconv, dmod_cols, _flat_row([dsmall[n] for n in SMALL])], ["scatter", "scatter", "gather"],
        "exchange_small")
    dmod_all = send([("w_in", 0)], [dw_in], dmod_all, "scatter_in0")

    res = {}
    dm_rows = jnp.pad(dmod_all, ((0, 16 - N_DEV), (0, 0), (0, 0)))
    g_ada = [_mm(c_rows, dm_rows[:, l], "tn", F32, "ada_dw%d" % l, a_act="silu")[None] for l in range(DEPTH)]
    r2 = (DEPTH * d, n_ada)
    outs = adamw(w_ada.reshape(r2), m_w_ada.reshape(r2), v_w_ada.reshape(r2), parts=g_ada, name="adamw_w_ada")
    res["w_ada"] = [o.reshape(w_ada.shape) for o in outs]
    packed = SMALL + ("conv_w",)
    p_all = jnp.concatenate([small_parts, conv_parts.reshape(N_DEV, 1, -1)], axis=2)
    pack = lambda pre: jnp.concatenate([_flat_row([given[pre + n] for n in SMALL]),
                                        given[pre + "conv_w"].reshape(1, -1)], axis=1)
    outs = adamw(pack(""), pack("m_"), pack("v_"), parts=[p_all], name="adamw_small")
    done = [res["w_ada"][1], outs[1]]
    for group, gname in ((FFN_WEIGHTS, "ffn"), (MIX_WEIGHTS, "mix"), (("w_in",), "in")):
        ks = [(n, l) for l in reversed(range(DEPTH)) for n in group]
        after = sum(lax.slice(a, (0,) * a.ndim, (1,) * a.ndim).reshape(1, 1) for a in done)
        srcs, lands = copy_wait([sent[k] for k in ks], len(ALL_PEERS), after, "scatter_wait_" + gname)
        parts = {k: with_own(land, lax.dynamic_index_in_dim(src, me, 0, keepdims=False))
                 for k, src, land in zip(ks, srcs, lands)}
        for n in group:
            w = given[n]
            r2 = (w.shape[0] * w.shape[1], w.shape[2])
            res[n] = [o.reshape(w.shape) for o in
                      adamw(w.reshape(r2), given["m_" + n].reshape(r2), given["v_" + n].reshape(r2),
                            parts=[parts[(n, l)] for l in range(DEPTH)], name="adamw_" + n)]
            done.append(res[n][1])
    off = 0
    for n in packed:
        if n == "conv_w":
            off = small_parts.shape[2]
        size = math.prod(given[n].shape)
        res[n] = [o[0, off:off + size].reshape(given[n].shape) for o in outs]
        off += size

    return (loss, dx[None]) + tuple(res[n][i] for i in range(4) for n in WEIGHTS)
```
